```python
import math
import jax, jax.numpy as jnp
from jax import lax
import numpy as np

D_MODEL = 1024
BATCH = 16
SEQ = 2048
DEPTH = 2

N_META = 16
BLOCK = 128
WINDOW = 128
A_HEADS = 8
A_KV_HEADS = 2
A_HEAD_DIM = 64
A_WIDTH = A_HEADS * A_HEAD_DIM
B_HEADS = 8
B_NOPE_DIM = 64
B_ROPE_DIM = 32
B_V_DIM = 64
B_WIDTH = B_HEADS * B_V_DIM
Q_LORA_RANK = 256
KV_LORA_RANK = 128
MIX_WIDTH = A_WIDTH + B_WIDTH
IN_SIZES = (A_WIDTH, A_KV_HEADS * A_HEAD_DIM, A_KV_HEADS * A_HEAD_DIM, A_WIDTH,
            Q_LORA_RANK, KV_LORA_RANK, B_ROPE_DIM, B_WIDTH)
IN_WIDTH = 2 * A_WIDTH + 2 * A_KV_HEADS * A_HEAD_DIM + Q_LORA_RANK + KV_LORA_RANK + B_ROPE_DIM + B_WIDTH
N_BUCKETS = 32
MAX_DISTANCE = 128
ROPE_THETA = 10000.0
EPS = 1e-6

kernel_name = "hymba_swa_mla_hybrid_encoder"


def rms_norm(x, g):
    xf = x.astype(jnp.float32)
    y = xf * lax.rsqrt(jnp.mean(xf * xf, axis=-1, keepdims=True) + EPS)
    return (y * g.astype(jnp.float32)).astype(x.dtype)


def t5_bucket(rel):
    nb = N_BUCKETS // 2
    max_exact = nb // 2
    ret = jnp.where(rel > 0, nb, 0)
    n = jnp.abs(rel)
    nf = jnp.maximum(n, 1).astype(jnp.float32)
    large = max_exact + (jnp.log(nf / max_exact) / math.log(MAX_DISTANCE / max_exact)
                         * (nb - max_exact)).astype(jnp.int32)
    large = jnp.minimum(large, nb - 1)
    return ret + jnp.where(n < max_exact, n, large)


def rel_bias(table, q_pos, k_pos):
    b = t5_bucket(k_pos[..., None, :] - q_pos[..., :, None])
    return jnp.moveaxis(table.astype(jnp.float32)[b], -1, -3)


def softmax_with_sink(logits, sink, mask):
    logits = jnp.where(mask, logits, -jnp.inf)
    s = sink.astype(jnp.float32)[:, None, None]
    m = jnp.maximum(jnp.max(logits, axis=-1, keepdims=True), s)
    e = jnp.exp(logits - m)
    return e / (jnp.sum(e, axis=-1, keepdims=True) + jnp.exp(s - m))


def apply_rope(x, cos, sin):
    x1, x2 = jnp.split(x.astype(jnp.float32), 2, axis=-1)
    return jnp.concatenate([x1 * cos - x2 * sin, x2 * cos + x1 * sin], axis=-1).astype(x.dtype)


def window_attention(q, k, v, sink, table):
    B, L = q.shape[0], q.shape[1]
    S = L - N_META
    nb = S // BLOCK
    g = A_HEADS // A_KV_HEADS
    dh = A_HEAD_DIM
    scale = dh ** -0.5
    q = q.reshape(B, L, A_KV_HEADS, g, dh)
    qm, qr = q[:, :N_META], q[:, N_META:]
    km, kr = k[:, :N_META], k[:, N_META:]
    vm, vr = v[:, :N_META], v[:, N_META:]

    qb = qr.reshape(B, nb, BLOCK, A_KV_HEADS, g, dh)
    pad = ((0, 0), (BLOCK, BLOCK), (0, 0), (0, 0))

    def band(t, tm):
        tp = jnp.pad(t, pad).reshape(B, nb + 2, BLOCK, A_KV_HEADS, dh)
        tb = jnp.concatenate([tp[:, :-2], tp[:, 1:-1], tp[:, 2:]], axis=2)
        tmb = jnp.broadcast_to(tm[:, None], (B, nb, N_META, A_KV_HEADS, dh))
        return jnp.concatenate([tmb, tb], axis=2)

    kb, vb = band(kr, km), band(vr, vm)
    blk = jnp.arange(nb)[:, None]
    q_idx = blk * BLOCK + jnp.arange(BLOCK)[None, :]
    k_idx = (blk - 1) * BLOCK + jnp.arange(3 * BLOCK)[None, :]
    valid = ((k_idx[:, None, :] >= 0) & (k_idx[:, None, :] < S)
             & (jnp.abs(q_idx[:, :, None] - k_idx[:, None, :]) <= WINDOW))
    mask = jnp.concatenate([jnp.ones((nb, BLOCK, N_META), bool), valid], axis=-1)
    q_pos = N_META + q_idx
    k_pos = jnp.concatenate([jnp.broadcast_to(jnp.arange(N_META)[None], (nb, N_META)), N_META + k_idx], axis=-1)
    bias = rel_bias(table, q_pos, k_pos)
    logits = jnp.einsum('bnqhgd,bnkhd->bnhgqk', qb, kb).astype(jnp.float32) * scale
    logits = logits.reshape(B, nb, A_HEADS, BLOCK, N_META + 3 * BLOCK) + bias[None]
    p = softmax_with_sink(logits, sink, mask[None, :, None]).astype(v.dtype)
    p = p.reshape(B, nb, A_KV_HEADS, g, BLOCK, N_META + 3 * BLOCK)
    out_r = jnp.einsum('bnhgqk,bnkhd->bnqhgd', p, vb).reshape(B, S, A_WIDTH)

    km2 = jnp.concatenate([km, kr[:, :BLOCK]], axis=1)
    vm2 = jnp.concatenate([vm, vr[:, :BLOCK]], axis=1)
    qp = jnp.arange(N_META)
    kp = jnp.arange(N_META + BLOCK)
    mmask = jnp.abs(kp[None, :] - qp[:, None]) <= WINDOW
    mbias = rel_bias(table, qp, kp)
    ml = jnp.einsum('bqhgd,bkhd->bhgqk', qm, km2).astype(jnp.float32) * scale
    ml = ml.reshape(B, A_HEADS, N_META, N_META + BLOCK) + mbias[None]
    mp = softmax_with_sink(ml, sink, mmask).astype(v.dtype).reshape(B, A_KV_HEADS, g, N_META, N_META + BLOCK)
    out_m = jnp.einsum('bhgqk,bkhd->bqhgd', mp, vm2).reshape(B, N_META, A_WIDTH)
    return jnp.concatenate([out_m, out_r], axis=1)


def mla_attention(q_nope, q_rope, k_nope, k_rope, v):
    B, L = q_nope.shape[0], q_nope.shape[1]
    S = L - N_META
    nb = S // BLOCK
    scale = (B_NOPE_DIM + B_ROPE_DIM) ** -0.5

    def attend(qn, qr):
        logits = (jnp.einsum('bqhd,bkhd->bhqk', qn, k_nope)
                  + jnp.einsum('bqhd,bkd->bhqk', qr, k_rope)).astype(jnp.float32) * scale
        p = jax.nn.softmax(logits, axis=-1).astype(v.dtype)
        return jnp.einsum('bhqk,bkhd->bqhd', p, v)

    out_m = attend(q_nope[:, :N_META], q_rope[:, :N_META]).reshape(B, N_META, B_WIDTH)
    qn_b = jnp.moveaxis(q_nope[:, N_META:].reshape(B, nb, BLOCK, B_HEADS, B_NOPE_DIM), 1, 0)
    qr_b = jnp.moveaxis(q_rope[:, N_META:].reshape(B, nb, BLOCK, B_HEADS, B_ROPE_DIM), 1, 0)
    out_r = lax.map(lambda a: attend(a[0], a[1]), (qn_b, qr_b))
    out_r = jnp.moveaxis(out_r, 0, 1).reshape(B, S, B_WIDTH)
    return jnp.concatenate([out_m, out_r], axis=1)


def hybrid_layer(h, norm_in, w_in, sink_a, norm_q_lat, w_uq, norm_kv_lat, w_ukv,
                 norm_out_a, norm_out_b, w_out, rel_table, cos, sin):
    B, L = h.shape[0], h.shape[1]
    u = rms_norm(h, norm_in)
    proj = jnp.einsum('bld,de->ble', u, w_in)
    offsets = np.cumsum(IN_SIZES)[:-1].tolist()
    qa, ka, va, ga, cq, ckv, kr, gb = jnp.split(proj, offsets, axis=-1)

    ya = window_attention(qa.reshape(B, L, A_HEADS, A_HEAD_DIM),
                          ka.reshape(B, L, A_KV_HEADS, A_HEAD_DIM),
                          va.reshape(B, L, A_KV_HEADS, A_HEAD_DIM), sink_a, rel_table)

    q = jnp.einsum('blr,re->ble', rms_norm(cq, norm_q_lat), w_uq).reshape(B, L, B_HEADS, B_NOPE_DIM + B_ROPE_DIM)
    q_nope, q_rope = q[..., :B_NOPE_DIM], q[..., B_NOPE_DIM:]
    q_rope = apply_rope(q_rope, cos[:, None], sin[:, None])
    kv = jnp.einsum('blr,re->ble', rms_norm(ckv, norm_kv_lat), w_ukv).reshape(B, L, B_HEADS, B_NOPE_DIM + B_V_DIM)
    k_nope, vb = kv[..., :B_NOPE_DIM], kv[..., B_NOPE_DIM:]
    k_rope = apply_rope(kr, cos, sin)
    yb = mla_attention(q_nope, q_rope, k_nope, k_rope, vb)

    y = jnp.concatenate([rms_norm(ya, norm_out_a) * jax.nn.silu(ga),
                         rms_norm(yb, norm_out_b) * jax.nn.silu(gb)], axis=-1)
    return h + jnp.einsum('ble,ed->bld', y, w_out)


def _fwd_setup_inputs(seed: int = 0) -> dict:
    key = jax.random.key(seed)
    ks = jax.random.split(key, 16)
    f32 = jnp.float32
    nrm = lambda k, s, sc: jax.random.normal(k, s, f32) * sc
    gain = lambda k, s: 1.0 + 0.1 * jax.random.normal(k, s, f32)
    return {
        "x": nrm(ks[0], (BATCH, SEQ, D_MODEL), 1.0),
        "meta_tokens": nrm(ks[1], (N_META, D_MODEL), 1.0),
        "rel_bias_table": nrm(ks[2], (N_BUCKETS, A_HEADS), 0.5),
        "norm_in": gain(ks[3], (DEPTH, D_MODEL)),
        "w_in": nrm(ks[4], (DEPTH, D_MODEL, IN_WIDTH), D_MODEL ** -0.5),
        "sink_a": nrm(ks[5], (DEPTH, A_HEADS), 0.5),
        "norm_q_lat": gain(ks[6], (DEPTH, Q_LORA_RANK)),
        "w_uq": nrm(ks[7], (DEPTH, Q_LORA_RANK, B_HEADS * (B_NOPE_DIM + B_ROPE_DIM)), Q_LORA_RANK ** -0.5),
        "norm_kv_lat": gain(ks[8], (DEPTH, KV_LORA_RANK)),
        "w_ukv": nrm(ks[9], (DEPTH, KV_LORA_RANK, B_HEADS * (B_NOPE_DIM + B_V_DIM)), KV_LORA_RANK ** -0.5),
        "norm_out_a": gain(ks[10], (DEPTH, A_WIDTH)),
        "norm_out_b": gain(ks[11], (DEPTH, B_WIDTH)),
        "w_out": nrm(ks[12], (DEPTH, MIX_WIDTH, D_MODEL), MIX_WIDTH ** -0.5),
        "norm_final": gain(ks[13], (D_MODEL,)),
    }


def _fwd_reference(x, meta_tokens, rel_bias_table, norm_in, w_in, sink_a, norm_q_lat, w_uq,
              norm_kv_lat, w_ukv, norm_out_a, norm_out_b, w_out, norm_final):
    B = x.shape[0]
    meta = jnp.broadcast_to(meta_tokens.astype(x.dtype)[None], (B, N_META, D_MODEL))
    h = jnp.concatenate([meta, x], axis=1)
    L = h.shape[1]
    half = B_ROPE_DIM // 2
    freqs = ROPE_THETA ** (-jnp.arange(half, dtype=jnp.float32) / half)
    ang = jnp.arange(L, dtype=jnp.float32)[:, None] * freqs[None, :]
    cos, sin = jnp.cos(ang), jnp.sin(ang)
    for i in range(DEPTH):
        h = hybrid_layer(h, norm_in[i], w_in[i], sink_a[i], norm_q_lat[i], w_uq[i], norm_kv_lat[i],
                         w_ukv[i], norm_out_a[i], norm_out_b[i], w_out[i], rel_bias_table, cos, sin)
    return rms_norm(h[:, N_META:], norm_final)


import jax as _jax
import jax.numpy as _jnp

TWIN_FORMAT = 'train_step'
FWD_PARAMS = ['x', 'meta_tokens', 'rel_bias_table', 'norm_in', 'w_in', 'sink_a', 'norm_q_lat', 'w_uq', 'norm_kv_lat', 'w_ukv', 'norm_out_a', 'norm_out_b', 'w_out', 'norm_final']
TWIN_WEIGHTS = ['meta_tokens', 'rel_bias_table', 'norm_in', 'w_in', 'sink_a', 'norm_q_lat', 'w_uq', 'norm_kv_lat', 'w_ukv', 'norm_out_a', 'norm_out_b', 'w_out', 'norm_final']
TWIN_DIFF_INPUT = 'x'
TWIN_INPUTS = ['x', 'meta_tokens', 'rel_bias_table', 'norm_in', 'w_in', 'sink_a', 'norm_q_lat', 'w_uq', 'norm_kv_lat', 'w_ukv', 'norm_out_a', 'norm_out_b', 'w_out', 'norm_final', 'loss_target', 'm_meta_tokens', 'm_rel_bias_table', 'm_norm_in', 'm_w_in', 'm_sink_a', 'm_norm_q_lat', 'm_w_uq', 'm_norm_kv_lat', 'm_w_ukv', 'm_norm_out_a', 'm_norm_out_b', 'm_w_out', 'm_norm_final', 'v_meta_tokens', 'v_rel_bias_table', 'v_norm_in', 'v_w_in', 'v_sink_a', 'v_norm_q_lat', 'v_w_uq', 'v_norm_kv_lat', 'v_w_ukv', 'v_norm_out_a', 'v_norm_out_b', 'v_w_out', 'v_norm_final']
TWIN_OUTPUTS = ['loss', 'grad_x', 'grad_meta_tokens', 'grad_rel_bias_table', 'grad_norm_in', 'grad_w_in', 'grad_sink_a', 'grad_norm_q_lat', 'grad_w_uq', 'grad_norm_kv_lat', 'grad_w_ukv', 'grad_norm_out_a', 'grad_norm_out_b', 'grad_w_out', 'grad_norm_final', 'delta_meta_tokens', 'delta_rel_bias_table', 'delta_norm_in', 'delta_w_in', 'delta_sink_a', 'delta_norm_q_lat', 'delta_w_uq', 'delta_norm_kv_lat', 'delta_w_ukv', 'delta_norm_out_a', 'delta_norm_out_b', 'delta_w_out', 'delta_norm_final', 'new_m_meta_tokens', 'new_m_rel_bias_table', 'new_m_norm_in', 'new_m_w_in', 'new_m_sink_a', 'new_m_norm_q_lat', 'new_m_w_uq', 'new_m_norm_kv_lat', 'new_m_w_ukv', 'new_m_norm_out_a', 'new_m_norm_out_b', 'new_m_w_out', 'new_m_norm_final', 'new_v_meta_tokens', 'new_v_rel_bias_table', 'new_v_norm_in', 'new_v_w_in', 'new_v_sink_a', 'new_v_norm_q_lat', 'new_v_w_uq', 'new_v_norm_kv_lat', 'new_v_w_ukv', 'new_v_norm_out_a', 'new_v_norm_out_b', 'new_v_w_out', 'new_v_norm_final']
TWIN_LEAF_KINDS = {'loss': 'loss', 'grad_x': 'grad_x', 'grad_meta_tokens': 'grad_w', 'grad_rel_bias_table': 'grad_w', 'grad_norm_in': 'grad_w', 'grad_w_in': 'grad_w', 'grad_sink_a': 'grad_w', 'grad_norm_q_lat': 'grad_w', 'grad_w_uq': 'grad_w', 'grad_norm_kv_lat': 'grad_w', 'grad_w_ukv': 'grad_w', 'grad_norm_out_a': 'grad_w', 'grad_norm_out_b': 'grad_w', 'grad_w_out': 'grad_w', 'grad_norm_final': 'grad_w', 'delta_meta_tokens': 'delta_w', 'delta_rel_bias_table': 'delta_w', 'delta_norm_in': 'delta_w', 'delta_w_in': 'delta_w', 'delta_sink_a': 'delta_w', 'delta_norm_q_lat': 'delta_w', 'delta_w_uq': 'delta_w', 'delta_norm_kv_lat': 'delta_w', 'delta_w_ukv': 'delta_w', 'delta_norm_out_a': 'delta_w', 'delta_norm_out_b': 'delta_w', 'delta_w_out': 'delta_w', 'delta_norm_final': 'delta_w', 'new_m_meta_tokens': 'new_m', 'new_m_rel_bias_table': 'new_m', 'new_m_norm_in': 'new_m', 'new_m_w_in': 'new_m', 'new_m_sink_a': 'new_m', 'new_m_norm_q_lat': 'new_m', 'new_m_w_uq': 'new_m', 'new_m_norm_kv_lat': 'new_m', 'new_m_w_ukv': 'new_m', 'new_m_norm_out_a': 'new_m', 'new_m_norm_out_b': 'new_m', 'new_m_w_out': 'new_m', 'new_m_norm_final': 'new_m', 'new_v_meta_tokens': 'new_v', 'new_v_rel_bias_table': 'new_v', 'new_v_norm_in': 'new_v', 'new_v_w_in': 'new_v', 'new_v_sink_a': 'new_v', 'new_v_norm_q_lat': 'new_v', 'new_v_w_uq': 'new_v', 'new_v_norm_kv_lat': 'new_v', 'new_v_w_ukv': 'new_v', 'new_v_norm_out_a': 'new_v', 'new_v_norm_out_b': 'new_v', 'new_v_w_out': 'new_v', 'new_v_norm_final': 'new_v'}


def _forward(args):
    return _fwd_reference(*[args[k] for k in FWD_PARAMS])


def _output_shape():
    out = _jax.eval_shape(lambda: _forward(_fwd_setup_inputs(0)))
    return out.shape, out.dtype

N_MICROBATCH = 1
ADAM_LR = 0.001
ADAM_B1 = 0.9
ADAM_B2 = 0.999
ADAM_EPS = 1e-08
ADAM_WD = 0.01
ADAM_STEP = 10
PER_EXAMPLE_BATCH_AXIS = {'x': 0, 'loss_target': 0}
SHARED_INPUTS = []
_WEIGHT_DTYPES = {'meta_tokens': _jnp.float32, 'rel_bias_table': _jnp.float32, 'norm_in': _jnp.float32, 'w_in': _jnp.float32, 'sink_a': _jnp.float32, 'norm_q_lat': _jnp.float32, 'w_uq': _jnp.float32, 'norm_kv_lat': _jnp.float32, 'w_ukv': _jnp.float32, 'norm_out_a': _jnp.float32, 'norm_out_b': _jnp.float32, 'w_out': _jnp.float32, 'norm_final': _jnp.float32}
MOMENT_SCALE = {'meta_tokens': 1.242712e-02, 'rel_bias_table': 1.691199e-01, 'norm_in': 1.854253e-01, 'w_in': 1.249310e-01, 'sink_a': 4.049659e-03, 'norm_q_lat': 1.440510e-01, 'w_uq': 7.780456e-02, 'norm_kv_lat': 3.408130e-01, 'w_ukv': 1.004968e-01, 'norm_out_a': 1.107650e-01, 'norm_out_b': 1.075859e-01, 'w_out': 1.064238e-01, 'norm_final': 3.215852e+01}


def _to_microbatches(a, axis):
    t = _jnp.moveaxis(a, axis, 0)
    t = t.reshape((N_MICROBATCH, t.shape[0] // N_MICROBATCH) + t.shape[1:])
    return _jnp.moveaxis(t, 1, axis + 1)


def setup_inputs(seed: int = 0) -> dict:
    inp = _fwd_setup_inputs(seed)
    key = _jax.random.fold_in(_jax.random.key(seed), 7919)
    shape, _ = _output_shape()
    out = dict(inp)
    out["loss_target"] = _jax.random.normal(_jax.random.fold_in(key, 0), shape, _jnp.float32)
    for i, name in enumerate(TWIN_WEIGHTS):
        w = inp[name].astype(_jnp.float32)
        if MOMENT_SCALE is None:
            s = _jnp.sqrt(_jnp.mean(_jnp.square(w)) + 1e-30)
        else:
            s = MOMENT_SCALE[name]
        km, kv = _jax.random.split(_jax.random.fold_in(key, i + 1))
        out[name] = w
        out["m_" + name] = s * _jax.random.normal(km, w.shape, _jnp.float32)
        out["v_" + name] = (s * s) * _jax.random.uniform(kv, w.shape, _jnp.float32, 0.5, 1.5)
    if N_MICROBATCH > 1:
        for name, axis in PER_EXAMPLE_BATCH_AXIS.items():
            out[name] = _to_microbatches(out[name], axis)
    return {'x': out['x'], 'meta_tokens': out['meta_tokens'], 'rel_bias_table': out['rel_bias_table'], 'norm_in': out['norm_in'], 'w_in': out['w_in'], 'sink_a': out['sink_a'], 'norm_q_lat': out['norm_q_lat'], 'w_uq': out['w_uq'], 'norm_kv_lat': out['norm_kv_lat'], 'w_ukv': out['w_ukv'], 'norm_out_a': out['norm_out_a'], 'norm_out_b': out['norm_out_b'], 'w_out': out['w_out'], 'norm_final': out['norm_final'], 'loss_target': out['loss_target'], 'm_meta_tokens': out['m_meta_tokens'], 'm_rel_bias_table': out['m_rel_bias_table'], 'm_norm_in': out['m_norm_in'], 'm_w_in': out['m_w_in'], 'm_sink_a': out['m_sink_a'], 'm_norm_q_lat': out['m_norm_q_lat'], 'm_w_uq': out['m_w_uq'], 'm_norm_kv_lat': out['m_norm_kv_lat'], 'm_w_ukv': out['m_w_ukv'], 'm_norm_out_a': out['m_norm_out_a'], 'm_norm_out_b': out['m_norm_out_b'], 'm_w_out': out['m_w_out'], 'm_norm_final': out['m_norm_final'], 'v_meta_tokens': out['v_meta_tokens'], 'v_rel_bias_table': out['v_rel_bias_table'], 'v_norm_in': out['v_norm_in'], 'v_w_in': out['v_w_in'], 'v_sink_a': out['v_sink_a'], 'v_norm_q_lat': out['v_norm_q_lat'], 'v_w_uq': out['v_w_uq'], 'v_norm_kv_lat': out['v_norm_kv_lat'], 'v_w_ukv': out['v_w_ukv'], 'v_norm_out_a': out['v_norm_out_a'], 'v_norm_out_b': out['v_norm_out_b'], 'v_w_out': out['v_w_out'], 'v_norm_final': out['v_norm_final']}


def _loss(weights, diff, rest, loss_target):
    with _jax.named_scope("forward"):
        args = {**rest, TWIN_DIFF_INPUT: diff, **{k: w.astype(_WEIGHT_DTYPES[k]) for k, w in weights.items()}}
        y = _forward(args)
    with _jax.named_scope("loss_head"):
        err = _jnp.square(y.astype(_jnp.float32) - loss_target)
        return 0.5 * _jnp.sum(_jnp.mean(err, axis=-1)) if err.ndim else 0.5 * err


def _adamw(w, g, m, v):
    m = ADAM_B1 * m + (1.0 - ADAM_B1) * g
    v = ADAM_B2 * v + (1.0 - ADAM_B2) * _jnp.square(g)
    m_hat = m / (1.0 - ADAM_B1 ** ADAM_STEP)
    v_hat = v / (1.0 - ADAM_B2 ** ADAM_STEP)
    delta = -ADAM_LR * (m_hat / (_jnp.sqrt(v_hat) + ADAM_EPS) + ADAM_WD * w)
    return delta, m, v


def reference(x, meta_tokens, rel_bias_table, norm_in, w_in, sink_a, norm_q_lat, w_uq, norm_kv_lat, w_ukv, norm_out_a, norm_out_b, w_out, norm_final, loss_target, m_meta_tokens, m_rel_bias_table, m_norm_in, m_w_in, m_sink_a, m_norm_q_lat, m_w_uq, m_norm_kv_lat, m_w_ukv, m_norm_out_a, m_norm_out_b, m_w_out, m_norm_final, v_meta_tokens, v_rel_bias_table, v_norm_in, v_w_in, v_sink_a, v_norm_q_lat, v_w_uq, v_norm_kv_lat, v_w_ukv, v_norm_out_a, v_norm_out_b, v_w_out, v_norm_final):
    given = dict(x=x, meta_tokens=meta_tokens, rel_bias_table=rel_bias_table, norm_in=norm_in, w_in=w_in, sink_a=sink_a, norm_q_lat=norm_q_lat, w_uq=w_uq, norm_kv_lat=norm_kv_lat, w_ukv=w_ukv, norm_out_a=norm_out_a, norm_out_b=norm_out_b, w_out=w_out, norm_final=norm_final, loss_target=loss_target, m_meta_tokens=m_meta_tokens, m_rel_bias_table=m_rel_bias_table, m_norm_in=m_norm_in, m_w_in=m_w_in, m_sink_a=m_sink_a, m_norm_q_lat=m_norm_q_lat, m_w_uq=m_w_uq, m_norm_kv_lat=m_norm_kv_lat, m_w_ukv=m_w_ukv, m_norm_out_a=m_norm_out_a, m_norm_out_b=m_norm_out_b, m_w_out=m_w_out, m_norm_final=m_norm_final, v_meta_tokens=v_meta_tokens, v_rel_bias_table=v_rel_bias_table, v_norm_in=v_norm_in, v_w_in=v_w_in, v_sink_a=v_sink_a, v_norm_q_lat=v_norm_q_lat, v_w_uq=v_w_uq, v_norm_kv_lat=v_norm_kv_lat, v_w_ukv=v_w_ukv, v_norm_out_a=v_norm_out_a, v_norm_out_b=v_norm_out_b, v_w_out=v_w_out, v_norm_final=v_norm_final)
    weights = {n: given[n] for n in TWIN_WEIGHTS}
    shared = {n: given[n] for n in SHARED_INPUTS}
    per_example = {n: given[n] for n in ['x']}
    grad_fn = _jax.value_and_grad(_loss, argnums=(0, 1))

    def one_microbatch(ex, loss_target):
        ex = dict(ex)
        diff = ex.pop(TWIN_DIFF_INPUT)
        return grad_fn(weights, diff, {**shared, **ex}, loss_target)

    if N_MICROBATCH == 1:
        loss, (grad_w, grad_x) = one_microbatch(per_example, given["loss_target"])
    else:
        def body(carry, xs):
            loss_sum, grad_sum = carry
            l_k, (gw_k, gx_k) = one_microbatch(xs[0], xs[1])
            with _jax.named_scope("update"):
                return (loss_sum + l_k, _jax.tree.map(_jnp.add, grad_sum, gw_k)), gx_k

        init = (_jnp.zeros((), _jnp.float32), _jax.tree.map(_jnp.zeros_like, weights))
        (loss, grad_w), grad_x = _jax.lax.scan(body, init, (per_example, given["loss_target"]))
    with _jax.named_scope("update"):
        delta_w, new_m, new_v = {}, {}, {}
        for n in TWIN_WEIGHTS:
            delta_w[n], new_m[n], new_v[n] = _adamw(weights[n], grad_w[n], given["m_" + n], given["v_" + n])
    return (loss, grad_x, *[grad_w[n] for n in TWIN_WEIGHTS], *[delta_w[n] for n in TWIN_WEIGHTS],
            *[new_m[n] for n in TWIN_WEIGHTS], *[new_v[n] for n in TWIN_WEIGHTS])
```

```python
import functools
import math

import numpy as np
import jax
import jax.numpy as jnp
from jax import lax
from jax.experimental import pallas as pl
from jax.experimental.pallas import tpu as pltpu

F32, BF16 = jnp.float32, jnp.bfloat16
D = 1024
NMETA = 16
BLK = 128
A_HEADS, A_KV, A_DH, A_W = 8, 2, 64, 512
B_HEADS, NOPE, ROPE, B_V, B_W = 8, 64, 32, 64, 512
Q_RANK, KV_RANK = 256, 128
IN_W = 2208
N_BUCKETS, MAX_DIST = 32, 128
THETA = 10000.0
EPS = 1e-6
NEG = -1e30
A_SCALE = A_DH ** -0.5
B_SCALE = (NOPE + ROPE) ** -0.5
LANES = 128
NDEV = 8
MESH = pl.DeviceIdType.MESH

NX = 2560
QA0, KA0, VA0, GA0, CQ0, CKV0, KR0, GB0 = 0, 512, 768, 1024, 1536, 1792, 1920, 2048

ADAM_LR, ADAM_B1, ADAM_B2, ADAM_EPS, ADAM_WD, ADAM_STEP = 0.001, 0.9, 0.999, 1e-08, 0.01, 10

ROWS_W_IN = 2 * D * (IN_W // NDEV) // LANES
ROWS_W_UQ = 2 * Q_RANK * (B_HEADS * (NOPE + ROPE) // NDEV) // LANES
ROWS_W_UKV = 2 * KV_RANK * (B_HEADS * (NOPE + B_V) // NDEV) // LANES
ROWS_W_OUT = 2 * (D // NDEV) * D // LANES
ROWS_W = ROWS_W_IN + ROWS_W_UQ + ROWS_W_UKV + ROWS_W_OUT
ROWS_META = NMETA * (D // NDEV) // LANES
ROWS_G = ROWS_W + ROWS_META
SMALL_SIZES = (N_BUCKETS * A_HEADS, 2 * D, 2 * A_HEADS, 2 * Q_RANK, 2 * KV_RANK, 2 * A_W, 2 * B_W, D)
ROWS_SMALL = 56


def _dot(a, b):
    return jnp.dot(a, b, preferred_element_type=F32)


def _dot_nt(a, b):
    return lax.dot_general(a, b, (((1,), (1,)), ((), ())), preferred_element_type=F32)


def _dot_tn(a, b):
    return lax.dot_general(a, b, (((0,), (0,)), ((), ())), preferred_element_type=F32)


def _lane(shape):
    return lax.broadcasted_iota(jnp.int32, shape, len(shape) - 1)


def _rstd(x):
    return lax.rsqrt(jnp.mean(x * x, axis=-1, keepdims=True) + EPS)


def _rms_bwd(dn, x, r, g):
    z = dn * g
    dx = r * z - x * (r * r * r) * jnp.mean(z * x, axis=-1, keepdims=True)
    return dx, jnp.sum(dn * (x * r), axis=0, keepdims=True)


def _row(i):
    return lambda *ids: (ids[0], i)


def _const2(*ids):
    return (0, 0)


def _params(*sem):
    return pltpu.CompilerParams(dimension_semantics=sem)


def _pack_w_in(w):
    qa, ka, va, ga = w[:, 0:512], w[:, 512:640], w[:, 640:768], w[:, 768:1280]
    cq, ckv, kr, gb = w[:, 1280:1536], w[:, 1536:1664], w[:, 1664:1696], w[:, 1696:2208]
    dup = lambda t: jnp.concatenate([t[:, 0:64], t[:, 0:64], t[:, 64:128], t[:, 64:128]], axis=1)
    krp = jnp.concatenate([kr, jnp.zeros((w.shape[0], LANES - ROPE), w.dtype)], axis=1)
    return jnp.concatenate([qa, dup(ka), dup(va), ga, cq, ckv, krp, gb], axis=1)


def _unpack_w_in_grad(g):
    fold = lambda t: jnp.concatenate([t[:, 0:64] + t[:, 64:128], t[:, 128:192] + t[:, 192:256]], axis=1)
    return jnp.concatenate([g[:, QA0:QA0 + 512], fold(g[:, KA0:KA0 + 256]), fold(g[:, VA0:VA0 + 256]),
                            g[:, GA0:GA0 + 512], g[:, CQ0:CQ0 + 256], g[:, CKV0:CKV0 + 128],
                            g[:, KR0:KR0 + ROPE], g[:, GB0:GB0 + 512]], axis=1)


def _pack_w_uq(w):
    t = w.reshape(Q_RANK, B_HEADS, NOPE + ROPE)
    t = jnp.concatenate([t, jnp.zeros((Q_RANK, B_HEADS, LANES - NOPE - ROPE), w.dtype)], axis=-1)
    return t.reshape(Q_RANK, B_HEADS * LANES)


def _unpack_w_uq_grad(g):
    return g.reshape(Q_RANK, B_HEADS, LANES)[:, :, :NOPE + ROPE].reshape(Q_RANK, B_HEADS * (NOPE + ROPE))


def _pack_w_ukv(w):
    t = w.reshape(KV_RANK, B_HEADS, NOPE + B_V)
    kk = jnp.concatenate([t[:, :, :NOPE], jnp.zeros((KV_RANK, B_HEADS, LANES - NOPE), w.dtype)], axis=-1)
    return kk.reshape(KV_RANK, B_HEADS * LANES), t[:, :, NOPE:].reshape(KV_RANK, B_W)


def _unpack_w_ukv_grad(gk, gv):
    t = jnp.concatenate([gk.reshape(KV_RANK, B_HEADS, LANES)[:, :, :NOPE], gv.reshape(KV_RANK, B_HEADS, B_V)], axis=-1)
    return t.reshape(KV_RANK, B_HEADS * (NOPE + B_V))


def _t5_bucket(rel):
    nb = N_BUCKETS // 2
    max_exact = nb // 2
    ret = jnp.where(rel > 0, nb, 0)
    n = jnp.abs(rel)
    nf = jnp.maximum(n, 1).astype(F32)
    large = max_exact + (jnp.log(nf / max_exact) / math.log(MAX_DIST / max_exact) * (nb - max_exact)).astype(jnp.int32)
    large = jnp.minimum(large, nb - 1)
    return ret + jnp.where(n < max_exact, n, large)


def _bias_buckets(S):
    q = np.arange(BLK)[:, None]
    k = np.arange(4 * BLK)[None, :]
    is_meta_key = (k >= 3 * BLK) & (k < 3 * BLK + NMETA)
    mi = k - 3 * BLK
    rels, valids = [], []
    for shift in (0, BLK, 2 * BLK):
        rel_real = k - shift - q
        valid_real = (k < 3 * BLK) & (np.abs(rel_real) <= BLK)
        far = -(NMETA + MAX_DIST + BLK)
        rel_meta = (mi - (NMETA + q)) if shift == 0 else np.full_like(k + q, far)
        rels.append(np.where(is_meta_key, rel_meta, rel_real))
        valids.append(valid_real | is_meta_key)
    qm = q < NMETA
    rel_real = NMETA + k - q
    valid_real = (k < BLK) & (np.abs(rel_real) <= BLK) & qm
    rels.append(np.where(is_meta_key, mi - q, rel_real))
    valids.append(valid_real | is_meta_key)
    rel = jnp.asarray(np.stack(rels).astype(np.int32))
    valid = jnp.asarray(np.stack(valids))
    return jnp.where(valid, _t5_bucket(rel), -1).astype(jnp.int32)


def _rope_tables(S, Lp, off, passthrough):
    half = ROPE // 2
    r = np.arange(Lp)
    pos = np.where(r < S, NMETA + r, np.where(r < S + NMETA, r - S, 0)).astype(np.float32)
    freqs = THETA ** (-jnp.arange(half, dtype=F32) / half)
    ang = jnp.asarray(pos)[:, None] * freqs[None, :]
    cos, sin = jnp.cos(ang), jnp.sin(ang)
    z = lambda n: jnp.zeros((Lp, n), F32)
    head = jnp.ones((Lp, off), F32) if passthrough else z(off)
    c = jnp.concatenate([head, cos, cos, z(LANES - off - ROPE)], axis=1)
    s1 = jnp.concatenate([z(off), -sin, z(LANES - off - half)], axis=1)
    s2 = jnp.concatenate([z(off + half), sin, z(LANES - off - ROPE)], axis=1)
    return jnp.stack([c, s1, s2])


def _rope(x, t):
    return x * t[0] + pltpu.roll(x, LANES - 16, 1) * t[1] + pltpu.roll(x, 16, 1) * t[2]


def _rope_t(dy, t):
    return dy * t[0] + pltpu.roll(dy * t[1], 16, 1) + pltpu.roll(dy * t[2], LANES - 16, 1)


def _me_and_peers():
    x, y, c = lax.axis_index("x"), lax.axis_index("y"), lax.axis_index("c")
    flip = lambda v, b: 1 - v if b else v
    peers = [(flip(x, k & 4), flip(y, k & 2), flip(c, k & 1)) for k in range(1, NDEV)]
    return 4 * x + 2 * y + c, peers


def _all_gather(pack, meta):
    R = pack.shape[0]

    def body(p_ref, m_ref, out_ref, mout_ref, sbuf, ssem, rsem, msem, mrsem, lsem):
        me, peers = _me_and_peers()
        sbuf[...] = p_ref[...].astype(BF16)
        loc = [pltpu.make_async_copy(sbuf, out_ref.at[me], lsem.at[0]),
               pltpu.make_async_copy(m_ref, mout_ref.at[me], lsem.at[1])]
        for cp in loc:
            cp.start()
        cps = []
        for k, peer in enumerate(peers):
            cps.append(pltpu.make_async_remote_copy(src_ref=sbuf, dst_ref=out_ref.at[me], send_sem=ssem.at[k],
                                                    recv_sem=rsem.at[k], device_id=peer, device_id_type=MESH))
            cps.append(pltpu.make_async_remote_copy(src_ref=m_ref, dst_ref=mout_ref.at[me], send_sem=msem.at[k],
                                                    recv_sem=mrsem.at[k], device_id=peer, device_id_type=MESH))
        for cp in cps:
            cp.start()
        for cp in cps:
            cp.wait()
        for cp in loc:
            cp.wait()

    vm = pl.BlockSpec(memory_space=pltpu.VMEM)
    hbm = pl.BlockSpec(memory_space=pl.ANY)
    return pl.pallas_call(
        body, name="all_gather_weights",
        out_shape=(jax.ShapeDtypeStruct((NDEV, R, LANES), BF16), jax.ShapeDtypeStruct((NDEV,) + meta.shape, F32)),
        in_specs=[vm, vm], out_specs=(hbm, hbm),
        scratch_shapes=[pltpu.VMEM((R, LANES), BF16)] + [pltpu.SemaphoreType.DMA((NDEV - 1,))] * 4
        + [pltpu.SemaphoreType.DMA((2,))],
    )(pack, meta)


def _reduce_scatter(gpack, spack):
    def body(g_ref, s_ref, out_ref, sout_ref, ssem, rsem, msem, mrsem, lsem):
        me, peers = _me_and_peers()
        loc = [pltpu.make_async_copy(g_ref.at[me], out_ref.at[me], lsem.at[0]),
               pltpu.make_async_copy(s_ref, sout_ref.at[me], lsem.at[1])]
        for cp in loc:
            cp.start()
        cps = []
        for k, (px, py, pc) in enumerate(peers):
            pid = 4 * px + 2 * py + pc
            cps.append(pltpu.make_async_remote_copy(src_ref=g_ref.at[pid], dst_ref=out_ref.at[me], send_sem=ssem.at[k],
                                                    recv_sem=rsem.at[k], device_id=(px, py, pc), device_id_type=MESH))
            cps.append(pltpu.make_async_remote_copy(src_ref=s_ref, dst_ref=sout_ref.at[me], send_sem=msem.at[k],
                                                    recv_sem=mrsem.at[k], device_id=(px, py, pc), device_id_type=MESH))
        for cp in cps:
            cp.start()
        for cp in cps:
            cp.wait()
        for cp in loc:
            cp.wait()

    hbm = pl.BlockSpec(memory_space=pl.ANY)
    return pl.pallas_call(
        body, name="reduce_scatter_grads",
        out_shape=(jax.ShapeDtypeStruct(gpack.shape, F32), jax.ShapeDtypeStruct((NDEV,) + spack.shape, F32)),
        in_specs=[hbm, hbm], out_specs=(hbm, hbm),
        scratch_shapes=[pltpu.SemaphoreType.DMA((NDEV - 1,))] * 4 + [pltpu.SemaphoreType.DMA((2,))],
    )(gpack, spack)


def _sum_adamw(recv, w, m, v, name):
    R = w.shape[0]
    tr = R // 10 if R % 80 == 0 else R
    c1 = 1.0 - ADAM_B1 ** ADAM_STEP
    c2 = 1.0 - ADAM_B2 ** ADAM_STEP

    def body(r_ref, w_ref, m_ref, v_ref, g_out, d_out, m_out, v_out):
        g = r_ref[0]
        for p in range(1, NDEV):
            g = g + r_ref[p]
        mn = ADAM_B1 * m_ref[...] + (1.0 - ADAM_B1) * g
        vn = ADAM_B2 * v_ref[...] + (1.0 - ADAM_B2) * (g * g)
        m_hat = mn / c1
        v_hat = vn / c2
        g_out[...] = g
        d_out[...] = -ADAM_LR * (m_hat / (jnp.sqrt(v_hat) + ADAM_EPS) + ADAM_WD * w_ref[...])
        m_out[...] = mn
        v_out[...] = vn

    blk = pl.BlockSpec((tr, LANES), lambda i: (i, 0))
    return pl.pallas_call(
        body, name=name, grid=(R // tr,),
        in_specs=[pl.BlockSpec((NDEV, tr, LANES), lambda i: (0, i, 0)), blk, blk, blk],
        out_specs=(blk, blk, blk, blk), out_shape=(jax.ShapeDtypeStruct((R, LANES), F32),) * 4,
        compiler_params=_params("parallel"),
    )(recv, w, m, v)


def _inproj_fwd(h, g, wx, tm, name):
    T = h.shape[0]

    def body(h_ref, g_ref, w_ref, o_ref):
        x = h_ref[...]
        u = (x * _rstd(x) * g_ref[...]).astype(BF16)
        o_ref[...] = _dot(u, w_ref[...])

    return pl.pallas_call(
        body, name=name, grid=(T // tm,),
        in_specs=[pl.BlockSpec((tm, D), _row(0)), pl.BlockSpec((1, D), _const2), pl.BlockSpec((D, NX), _const2)],
        out_specs=pl.BlockSpec((tm, NX), _row(0)), out_shape=jax.ShapeDtypeStruct((T, NX), F32),
        compiler_params=_params("parallel"),
    )(h, g, wx)


def _mla_prep_fwd(proj, gq, gkv, wq, wkk, wkv, tq, tk, tm, Lp, name):
    T = proj.shape[0]
    nt = Lp // tm

    def body(cq_ref, ckv_ref, kr_ref, gq_ref, gkv_ref, wq_ref, wkk_ref, wkv_ref, tq_ref, tk_ref, q_ref, k_ref, v_ref):
        cq = cq_ref[...]
        q = _dot((cq * _rstd(cq) * gq_ref[...]).astype(BF16), wq_ref[...])
        tqv = tq_ref[...]
        for h in range(B_HEADS):
            cs = slice(LANES * h, LANES * (h + 1))
            q_ref[:, cs] = _rope(q[:, cs], tqv).astype(BF16)
        ksh = pltpu.roll(_rope(kr_ref[...], tk_ref[...]), NOPE, 1)
        ckv = ckv_ref[...]
        cb = (ckv * _rstd(ckv) * gkv_ref[...]).astype(BF16)
        kn = _dot(cb, wkk_ref[...])
        for h in range(B_HEADS):
            cs = slice(LANES * h, LANES * (h + 1))
            k_ref[:, cs] = (kn[:, cs] + ksh).astype(BF16)
        v_ref[...] = _dot(cb, wkv_ref[...]).astype(BF16)

    tab = pl.BlockSpec((3, tm, LANES), lambda i: (0, i % nt, 0))
    return pl.pallas_call(
        body, name=name, grid=(T // tm,),
        in_specs=[pl.BlockSpec((tm, 256), _row(CQ0 // 256)), pl.BlockSpec((tm, 128), _row(CKV0 // 128)),
                  pl.BlockSpec((tm, 128), _row(KR0 // 128)), pl.BlockSpec((1, Q_RANK), _const2),
                  pl.BlockSpec((1, KV_RANK), _const2), pl.BlockSpec((Q_RANK, 1024), _const2),
                  pl.BlockSpec((KV_RANK, 1024), _const2), pl.BlockSpec((KV_RANK, B_W), _const2), tab, tab],
        out_specs=(pl.BlockSpec((tm, 1024), _row(0)), pl.BlockSpec((tm, 1024), _row(0)), pl.BlockSpec((tm, B_W), _row(0))),
        out_shape=(jax.ShapeDtypeStruct((T, 1024), BF16), jax.ShapeDtypeStruct((T, 1024), BF16),
                   jax.ShapeDtypeStruct((T, B_W), BF16)),
        compiler_params=_params("parallel"),
    )(proj, proj, proj, gq, gkv, wq, wkk, wkv, tq, tk)


def _build_bias(buckets, table, name):
    def body(tab_ref, b_ref, o_ref):
        bidx = b_ref[0]
        for h in range(A_HEADS):
            def step(j, acc):
                return jnp.where(bidx == j, tab_ref[j, h], acc)
            o_ref[0, h] = lax.fori_loop(0, N_BUCKETS, step, jnp.full(bidx.shape, NEG, F32))

    return pl.pallas_call(
        body, name=name, grid=(4,),
        in_specs=[pl.BlockSpec(memory_space=pltpu.SMEM), pl.BlockSpec((1, BLK, 4 * BLK), lambda i: (i, 0, 0))],
        out_specs=pl.BlockSpec((1, A_HEADS, BLK, 4 * BLK), lambda i: (i, 0, 0, 0)),
        out_shape=jax.ShapeDtypeStruct((4, A_HEADS, BLK, 4 * BLK), F32),
        compiler_params=_params("parallel"),
    )(table, buckets)


def _win_base(n, NB):
    return jnp.where(n == NB, 0, jnp.clip(n - 1, 0, NB - 3))


def _win_variant(n, NB):
    return jnp.where(n == 0, 0, jnp.where(n < NB - 1, 1, jnp.where(n == NB - 1, 2, 3)))


def _win_fwd(proj, bias, sink, Bl, S, name):
    T = proj.shape[0]
    Lp = S + BLK
    NB = S // BLK
    nblk = Lp // BLK

    def body(sink_ref, q_ref, k_ref, v_ref, b_ref, o_ref, lse_ref):
        n = pl.program_id(1)
        base = pl.multiple_of(_win_base(n, NB) * BLK, BLK)
        lane = _lane((BLK, LANES))
        lo = lane < 64
        outs, lses = [None] * A_HEADS, [None] * A_HEADS
        for kvh in range(A_KV):
            cs = slice(LANES * kvh, LANES * (kvh + 1))
            k2 = jnp.concatenate([k_ref[pl.ds(base, 3 * BLK), cs], k_ref[S:S + BLK, cs]], axis=0).astype(BF16)
            v2 = jnp.concatenate([v_ref[pl.ds(base, 3 * BLK), cs], v_ref[S:S + BLK, cs]], axis=0).astype(BF16)
            for gi in range(A_HEADS // A_KV):
                h = kvh * (A_HEADS // A_KV) + gi
                qp = q_ref[:, LANES * (h // 2):LANES * (h // 2 + 1)]
                qm = jnp.where(lo if h % 2 == 0 else ~lo, qp, 0.0).astype(BF16)
                s = _dot_nt(qm, k2) * A_SCALE + b_ref[0, h]
                sk = sink_ref[0, h]
                m = jnp.maximum(jnp.max(s, axis=-1, keepdims=True), sk)
                e = jnp.exp(s - m)
                den = jnp.sum(e, axis=-1, keepdims=True) + jnp.exp(sk - m)
                p = e * (1.0 / den)
                outs[h] = _dot(p.astype(BF16), v2)
                lses[h] = m + jnp.log(den)
        for j in range(A_HEADS // 2):
            o_ref[:, LANES * j:LANES * (j + 1)] = jnp.where(lo, outs[2 * j], outs[2 * j + 1])
        t = jnp.zeros((BLK, LANES), F32)
        for h in range(A_HEADS):
            t = jnp.where(lane == h, lses[h], t)
        lse_ref[...] = t

    qrow = lambda b, n: (b * nblk + n, 0)
    return pl.pallas_call(
        body, name=name, grid=(Bl, nblk),
        in_specs=[pl.BlockSpec(memory_space=pltpu.SMEM), pl.BlockSpec((BLK, A_W), qrow),
                  pl.BlockSpec((Lp, 256), lambda b, n: (b, KA0 // 256)), pl.BlockSpec((Lp, 256), lambda b, n: (b, VA0 // 256)),
                  pl.BlockSpec((1, A_HEADS, BLK, 4 * BLK), lambda b, n: (_win_variant(n, NB), 0, 0, 0))],
        out_specs=(pl.BlockSpec((BLK, A_W), qrow), pl.BlockSpec((BLK, LANES), qrow)),
        out_shape=(jax.ShapeDtypeStruct((T, A_W), F32), jax.ShapeDtypeStruct((T, LANES), F32)),
        compiler_params=_params("parallel", "arbitrary"),
    )(sink, proj, proj, proj, bias)


def _mla_fwd(q, k, v, kmask, Bl, Lp, tq, name):
    T = q.shape[0]
    nq = Lp // tq

    def body(q_ref, k_ref, v_ref, km_ref, o_ref, lse_ref):
        lane = _lane((tq, LANES))
        vb = v_ref[...]
        km = km_ref[...]
        outs = []
        t = jnp.zeros((tq, LANES), F32)
        for hh in range(2):
            cs = slice(LANES * hh, LANES * (hh + 1))
            s = _dot_nt(q_ref[:, cs], k_ref[:, cs]) * B_SCALE + km
            m = jnp.max(s, axis=-1, keepdims=True)
            e = jnp.exp(s - m)
            den = jnp.sum(e, axis=-1, keepdims=True)
            outs.append(_dot(e.astype(BF16), vb) * (1.0 / den))
            t = jnp.where(lane == hh, m + jnp.log(den), t)
        o_ref[...] = jnp.where(lane < 64, outs[0], outs[1])
        lse_ref[...] = t

    qrow = lambda b, j, i: (b * nq + i, j)
    return pl.pallas_call(
        body, name=name, grid=(Bl, B_HEADS // 2, nq),
        in_specs=[pl.BlockSpec((tq, 256), qrow), pl.BlockSpec((Lp, 256), lambda b, j, i: (b, j)),
                  pl.BlockSpec((Lp, LANES), lambda b, j, i: (b, j)), pl.BlockSpec((1, Lp), lambda b, j, i: (0, 0))],
        out_specs=(pl.BlockSpec((tq, LANES), qrow), pl.BlockSpec((tq, LANES), qrow)),
        out_shape=(jax.ShapeDtypeStruct((T, B_W), F32), jax.ShapeDtypeStruct((T, B_W), F32)),
        compiler_params=_params("parallel", "parallel", "arbitrary"),
    )(q, k, v, kmask)


def _gated(y, gate, gain):
    r = _rstd(y)
    nrm = y * r
    sg = jax.nn.sigmoid(gate)
    return r, nrm, sg, nrm * gain, gate * sg


def _out_fwd(ya, yb, proj, na, nb, wout, h, tm, name):
    T = h.shape[0]

    def body(ya_ref, yb_ref, ga_ref, gb_ref, na_ref, nb_ref, w_ref, h_ref, o_ref):
        _, _, _, n_a, sl_a = _gated(ya_ref[...], ga_ref[...], na_ref[...])
        _, _, _, n_b, sl_b = _gated(yb_ref[...], gb_ref[...], nb_ref[...])
        acc = _dot((n_a * sl_a).astype(BF16), w_ref[0:A_W, :]) + _dot((n_b * sl_b).astype(BF16), w_ref[A_W:A_W + B_W, :])
        o_ref[...] = h_ref[...] + acc

    half = pl.BlockSpec((tm, 512), _row(0))
    return pl.pallas_call(
        body, name=name, grid=(T // tm,),
        in_specs=[half, half, pl.BlockSpec((tm, 512), _row(GA0 // 512)), pl.BlockSpec((tm, 512), _row(GB0 // 512)),
                  pl.BlockSpec((1, 512), _const2), pl.BlockSpec((1, 512), _const2), pl.BlockSpec((D, D), _const2),
                  pl.BlockSpec((tm, D), _row(0))],
        out_specs=pl.BlockSpec((tm, D), _row(0)), out_shape=jax.ShapeDtypeStruct((T, D), F32),
        compiler_params=_params("parallel"),
    )(ya, yb, proj, proj, na, nb, wout, h)


def _loss_head(h, gf, target, Bl, S, name):
    T = h.shape[0]
    nblk = (S + BLK) // BLK
    NB = S // BLK

    def body(h_ref, g_ref, t_ref, dh_ref, loss_ref, dg_ref):
        b, n = pl.program_id(0), pl.program_id(1)

        @pl.when((b == 0) & (n == 0))
        def _():
            loss_ref[...] = jnp.zeros_like(loss_ref)
            dg_ref[...] = jnp.zeros_like(dg_ref)

        @pl.when(n < NB)
        def _():
            x = h_ref[...]
            g = g_ref[...]
            r = _rstd(x)
            err = x * r * g - t_ref[0]
            loss_ref[...] += 0.5 * jnp.sum(jnp.mean(err * err, axis=-1, keepdims=True))
            dx, dg = _rms_bwd(err * (1.0 / D), x, r, g)
            dh_ref[...] = dx
            dg_ref[...] += dg

        @pl.when(n >= NB)
        def _():
            dh_ref[...] = jnp.zeros_like(dh_ref)

    return pl.pallas_call(
        body, name=name, grid=(Bl, nblk),
        in_specs=[pl.BlockSpec((BLK, D), lambda b, n: (b * nblk + n, 0)), pl.BlockSpec((1, D), lambda b, n: (0, 0)),
                  pl.BlockSpec((1, BLK, D), lambda b, n: (b, jnp.minimum(n, NB - 1), 0))],
        out_specs=(pl.BlockSpec((BLK, D), lambda b, n: (b * nblk + n, 0)), pl.BlockSpec((8, LANES), lambda b, n: (0, 0)),
                   pl.BlockSpec((1, D), lambda b, n: (0, 0))),
        out_shape=(jax.ShapeDtypeStruct((T, D), F32), jax.ShapeDtypeStruct((8, LANES), F32), jax.ShapeDtypeStruct((1, D), F32)),
        compiler_params=_params("arbitrary", "arbitrary"),
    )(h, gf, target)


def _out_bwd(dh, ya, yb, proj, na, nb, wout, tm, name):
    T = dh.shape[0]

    def body(dh_ref, ya_ref, yb_ref, ga_ref, gb_ref, na_ref, nb_ref, w_ref,
             dya_ref, dyb_ref, dga_ref, dgb_ref, dw_ref, dna_ref, dnb_ref):
        @pl.when(pl.program_id(0) == 0)
        def _():
            dw_ref[...] = jnp.zeros_like(dw_ref)
            dna_ref[...] = jnp.zeros_like(dna_ref)
            dnb_ref[...] = jnp.zeros_like(dnb_ref)

        dhb = dh_ref[...].astype(BF16)
        dy = _dot_nt(dhb, w_ref[...])
        parts = ((ya_ref, ga_ref, na_ref, dya_ref, dga_ref, dna_ref, 0), (yb_ref, gb_ref, nb_ref, dyb_ref, dgb_ref, dnb_ref, A_W))
        for y_ref, gate_ref, gain_ref, dy_out, dgate_out, dgain_out, c0 in parts:
            y, gate, gain = y_ref[...], gate_ref[...], gain_ref[...]
            r, nrm, sg, n_g, sl = _gated(y, gate, gain)
            dyp = dy[:, c0:c0 + 512]
            dgate_out[...] = dyp * n_g * (sg * (1.0 + gate * (1.0 - sg)))
            dx, dgain = _rms_bwd(dyp * sl, y, r, gain)
            dy_out[...] = dx
            dgain_out[...] += dgain
            dw_ref[c0:c0 + 512, :] += _dot_tn((n_g * sl).astype(BF16), dhb)

    half = pl.BlockSpec((tm, 512), _row(0))
    vec = pl.BlockSpec((1, 512), _const2)
    return pl.pallas_call(
        body, name=name, grid=(T // tm,),
        in_specs=[pl.BlockSpec((tm, D), _row(0)), half, half, pl.BlockSpec((tm, 512), _row(GA0 // 512)),
                  pl.BlockSpec((tm, 512), _row(GB0 // 512)), vec, vec, pl.BlockSpec((D, D), _const2)],
        out_specs=(half, half, half, half, pl.BlockSpec((D, D), _const2), vec, vec),
        out_shape=(jax.ShapeDtypeStruct((T, 512), F32),) * 4 + (jax.ShapeDtypeStruct((D, D), F32),)
        + (jax.ShapeDtypeStruct((1, 512), F32),) * 2,
        compiler_params=_params("arbitrary"),
    )(dh, ya, yb, proj, proj, na, nb, wout)


def _mla_bwd(q, k, v, kmask, o, do, lse, Bl, Lp, tq, name):
    T = q.shape[0]
    nq = Lp // tq

    def body(q_ref, k_ref, v_ref, km_ref, o_ref, do_ref, lse_ref, dq_ref, dk_ref, dv_ref):
        @pl.when(pl.program_id(2) == 0)
        def _():
            dk_ref[...] = jnp.zeros_like(dk_ref)
            dv_ref[...] = jnp.zeros_like(dv_ref)

        lane = _lane((tq, LANES))
        lo = lane < 64
        vb = v_ref[...]
        km = km_ref[...]
        lsev = lse_ref[...]
        dop = do_ref[...]
        op = o_ref[...]
        for hh in range(2):
            cs = slice(LANES * hh, LANES * (hh + 1))
            qh, kh = q_ref[:, cs], k_ref[:, cs]
            s = _dot_nt(qh, kh) * B_SCALE + km
            lse_h = jnp.sum(jnp.where(lane == hh, lsev, 0.0), axis=-1, keepdims=True)
            p = jnp.exp(s - lse_h)
            dom = jnp.where(lo if hh == 0 else ~lo, dop, 0.0)
            delta = jnp.sum(dom * op, axis=-1, keepdims=True)
            domb = dom.astype(BF16)
            dp = _dot_nt(domb, vb)
            ds = (p * (dp - delta) * B_SCALE).astype(BF16)
            dq_ref[:, cs] = _dot(ds, kh)
            dk_ref[:, cs] += _dot_tn(ds, qh)
            dv_ref[...] += _dot_tn(p.astype(BF16), domb)

    qrow = lambda b, j, i: (b * nq + i, j)
    kv = lambda b, j, i: (b, j)
    return pl.pallas_call(
        body, name=name, grid=(Bl, B_HEADS // 2, nq),
        in_specs=[pl.BlockSpec((tq, 256), qrow), pl.BlockSpec((Lp, 256), kv), pl.BlockSpec((Lp, LANES), kv),
                  pl.BlockSpec((1, Lp), lambda b, j, i: (0, 0)), pl.BlockSpec((tq, LANES), qrow),
                  pl.BlockSpec((tq, LANES), qrow), pl.BlockSpec((tq, LANES), qrow)],
        out_specs=(pl.BlockSpec((tq, 256), qrow), pl.BlockSpec((Lp, 256), kv), pl.BlockSpec((Lp, LANES), kv)),
        out_shape=(jax.ShapeDtypeStruct((T, 1024), F32), jax.ShapeDtypeStruct((T, 1024), F32),
                   jax.ShapeDtypeStruct((T, B_W), F32)),
        compiler_params=_params("parallel", "parallel", "arbitrary"),
    )(q, k, v, kmask, o, do, lse)


def _win_bwd(proj, bias, sink, ya, dya, lse, Bl, S, name):
    T = proj.shape[0]
    Lp = S + BLK
    NB = S // BLK
    nblk = Lp // BLK
    G = A_HEADS // A_KV

    def body(sink_ref, q_ref, k_ref, v_ref, b_ref, o_ref, do_ref, lse_ref, dq_ref, dk_ref, dv_ref, db_ref, ds_ref):
        n = pl.program_id(1)

        @pl.when(n == 0)
        def _():
            dk_ref[...] = jnp.zeros_like(dk_ref)
            dv_ref[...] = jnp.zeros_like(dv_ref)
            ds_ref[...] = jnp.zeros_like(ds_ref)

        @pl.when((n == 0) | (n == 1) | (n == NB - 1) | (n == NB))
        def _():
            db_ref[...] = jnp.zeros_like(db_ref)

        base = pl.multiple_of(_win_base(n, NB) * BLK, BLK)
        lane = _lane((BLK, LANES))
        lo = lane < 64
        lsev = lse_ref[...]
        dqs = [None] * A_HEADS
        for kvh in range(A_KV):
            cs = slice(LANES * kvh, LANES * (kvh + 1))
            k2 = jnp.concatenate([k_ref[pl.ds(base, 3 * BLK), cs], k_ref[S:S + BLK, cs]], axis=0).astype(BF16)
            v2 = jnp.concatenate([v_ref[pl.ds(base, 3 * BLK), cs], v_ref[S:S + BLK, cs]], axis=0).astype(BF16)
            dk_acc = jnp.zeros((4 * BLK, LANES), F32)
            dv_acc = jnp.zeros((4 * BLK, LANES), F32)
            for gi in range(G):
                h = kvh * G + gi
                ps = slice(LANES * (h // 2), LANES * (h // 2 + 1))
                hm = lo if h % 2 == 0 else ~lo
                qm = jnp.where(hm, q_ref[:, ps], 0.0).astype(BF16)
                s = _dot_nt(qm, k2) * A_SCALE + b_ref[0, h]
                lse_h = jnp.sum(jnp.where(lane == h, lsev, 0.0), axis=-1, keepdims=True)
                p = jnp.exp(s - lse_h)
                dom = jnp.where(hm, do_ref[:, ps], 0.0)
                delta = jnp.sum(dom * o_ref[:, ps], axis=-1, keepdims=True)
                domb = dom.astype(BF16)
                dsc = p * (_dot_nt(domb, v2) - delta)
                db_ref[0, 0, h] += dsc
                dsink = -jnp.sum(jnp.exp(sink_ref[0, h] - lse_h) * delta, axis=0, keepdims=True)
                ds_ref[0, h:h + 1, :] += jnp.broadcast_to(dsink, (1, LANES))
                dsl = (dsc * A_SCALE).astype(BF16)
                dqs[h] = _dot(dsl, k2)
                dk_acc = dk_acc + _dot_tn(dsl, qm)
                dv_acc = dv_acc + _dot_tn(p.astype(BF16), domb)
            dk_ref[pl.ds(base, 3 * BLK), cs] += dk_acc[0:3 * BLK]
            dk_ref[S:S + BLK, cs] += dk_acc[3 * BLK:4 * BLK]
            dv_ref[pl.ds(base, 3 * BLK), cs] += dv_acc[0:3 * BLK]
            dv_ref[S:S + BLK, cs] += dv_acc[3 * BLK:4 * BLK]
        for j in range(A_HEADS // 2):
            dq_ref[:, LANES * j:LANES * (j + 1)] = jnp.where(lo, dqs[2 * j], dqs[2 * j + 1])

    qrow = lambda b, n: (b * nblk + n, 0)
    kvs = pl.BlockSpec((Lp, 256), lambda b, n: (b, 0))
    return pl.pallas_call(
        body, name=name, grid=(Bl, nblk),
        in_specs=[pl.BlockSpec(memory_space=pltpu.SMEM), pl.BlockSpec((BLK, A_W), qrow),
                  pl.BlockSpec((Lp, 256), lambda b, n: (b, KA0 // 256)), pl.BlockSpec((Lp, 256), lambda b, n: (b, VA0 // 256)),
                  pl.BlockSpec((1, A_HEADS, BLK, 4 * BLK), lambda b, n: (_win_variant(n, NB), 0, 0, 0)),
                  pl.BlockSpec((BLK, A_W), qrow), pl.BlockSpec((BLK, A_W), qrow), pl.BlockSpec((BLK, LANES), qrow)],
        out_specs=(pl.BlockSpec((BLK, A_W), qrow), kvs, kvs,
                   pl.BlockSpec((1, 1, A_HEADS, BLK, 4 * BLK), lambda b, n: (b, _win_variant(n, NB), 0, 0, 0)),
                   pl.BlockSpec((1, 8, LANES), lambda b, n: (b, 0, 0))),
        out_shape=(jax.ShapeDtypeStruct((T, A_W), F32), jax.ShapeDtypeStruct((T, 256), F32),
                   jax.ShapeDtypeStruct((T, 256), F32), jax.ShapeDtypeStruct((Bl, 4, A_HEADS, BLK, 4 * BLK), F32),
                   jax.ShapeDtypeStruct((Bl, 8, LANES), F32)),
        compiler_params=_params("parallel", "arbitrary"),
    )(sink, proj, proj, proj, bias, ya, dya, lse)


def _table_grad(dbias, buckets, name):
    N = dbias.shape[0]

    def body(d_ref, b_ref, o_ref):
        @pl.when((pl.program_id(0) == 0) & (pl.program_id(1) == 0))
        def _():
            o_ref[...] = jnp.zeros_like(o_ref)

        bidx = b_ref[0]
        lane = _lane((1, LANES))
        for h in range(A_HEADS):
            d = d_ref[0, 0, h]

            def step(j, acc):
                val = jnp.sum(jnp.sum(jnp.where(bidx == j, d, 0.0), axis=0, keepdims=True), axis=1, keepdims=True)
                return jnp.where(lane == j, val, acc)

            o_ref[h:h + 1, :] += lax.fori_loop(0, N_BUCKETS, step, jnp.zeros((1, LANES), F32))

    return pl.pallas_call(
        body, name=name, grid=(N, 4),
        in_specs=[pl.BlockSpec((1, 1, A_HEADS, BLK, 4 * BLK), lambda b, v: (b, v, 0, 0, 0)),
                  pl.BlockSpec((1, BLK, 4 * BLK), lambda b, v: (v, 0, 0))],
        out_specs=pl.BlockSpec((8, LANES), lambda b, v: (0, 0)), out_shape=jax.ShapeDtypeStruct((8, LANES), F32),
        compiler_params=_params("arbitrary", "arbitrary"),
    )(dbias, buckets)


def _mla_prep_bwd(dq, dk, dv, proj, gq, gkv, wq, wkk, wkv, tq, tk, tm, Lp, name):
    T = proj.shape[0]
    nt = Lp // tm

    def body(dq_ref, dk_ref, dv_ref, cq_ref, ckv_ref, gq_ref, gkv_ref, wq_ref, wkk_ref, wkv_ref, tq_ref, tk_ref,
             dcq_ref, dckv_ref, dkr_ref, dwq_ref, dwkk_ref, dwkv_ref, dgq_ref, dgkv_ref):
        @pl.when(pl.program_id(0) == 0)
        def _():
            for r in (dwq_ref, dwkk_ref, dwkv_ref, dgq_ref, dgkv_ref):
                r[...] = jnp.zeros_like(r)

        tqv = tq_ref[...]
        dqp = jnp.concatenate([_rope_t(dq_ref[:, LANES * h:LANES * (h + 1)], tqv) for h in range(B_HEADS)],
                              axis=1).astype(BF16)
        cq, gq_ = cq_ref[...], gq_ref[...]
        rq = _rstd(cq)
        dwq_ref[...] += _dot_tn((cq * rq * gq_).astype(BF16), dqp)
        dx, dg = _rms_bwd(_dot_nt(dqp, wq_ref[...]), cq, rq, gq_)
        dcq_ref[...] = dx
        dgq_ref[...] += dg

        dkv_ = dk_ref[...]
        dks = dkv_[:, 0:LANES]
        for h in range(1, B_HEADS):
            dks = dks + dkv_[:, LANES * h:LANES * (h + 1)]
        dkr_ref[...] = _rope_t(pltpu.roll(dks, NOPE, 1), tk_ref[...])
        dkb = dkv_.astype(BF16)
        dvb = dv_ref[...].astype(BF16)
        ckv, gkv_ = ckv_ref[...], gkv_ref[...]
        rk = _rstd(ckv)
        cb = (ckv * rk * gkv_).astype(BF16)
        dwkk_ref[...] += _dot_tn(cb, dkb)
        dwkv_ref[...] += _dot_tn(cb, dvb)
        dx, dg = _rms_bwd(_dot_nt(dkb, wkk_ref[...]) + _dot_nt(dvb, wkv_ref[...]), ckv, rk, gkv_)
        dckv_ref[...] = dx
        dgkv_ref[...] += dg

    tab = pl.BlockSpec((3, tm, LANES), lambda i: (0, i % nt, 0))
    wide = pl.BlockSpec((tm, 1024), _row(0))
    return pl.pallas_call(
        body, name=name, grid=(T // tm,),
        in_specs=[wide, wide, pl.BlockSpec((tm, B_W), _row(0)), pl.BlockSpec((tm, 256), _row(CQ0 // 256)),
                  pl.BlockSpec((tm, 128), _row(CKV0 // 128)), pl.BlockSpec((1, Q_RANK), _const2),
                  pl.BlockSpec((1, KV_RANK), _const2), pl.BlockSpec((Q_RANK, 1024), _const2),
                  pl.BlockSpec((KV_RANK, 1024), _const2), pl.BlockSpec((KV_RANK, B_W), _const2), tab, tab],
        out_specs=(pl.BlockSpec((tm, 256), _row(0)), pl.BlockSpec((tm, 128), _row(0)), pl.BlockSpec((tm, 128), _row(0)),
                   pl.BlockSpec((Q_RANK, 1024), _const2), pl.BlockSpec((KV_RANK, 1024), _const2),
                   pl.BlockSpec((KV_RANK, B_W), _const2), pl.BlockSpec((1, Q_RANK), _const2),
                   pl.BlockSpec((1, KV_RANK), _const2)),
        out_shape=(jax.ShapeDtypeStruct((T, 256), F32), jax.ShapeDtypeStruct((T, 128), F32),
                   jax.ShapeDtypeStruct((T, 128), F32), jax.ShapeDtypeStruct((Q_RANK, 1024), F32),
                   jax.ShapeDtypeStruct((KV_RANK, 1024), F32), jax.ShapeDtypeStruct((KV_RANK, B_W), F32),
                   jax.ShapeDtypeStruct((1, Q_RANK), F32), jax.ShapeDtypeStruct((1, KV_RANK), F32)),
        compiler_params=_params("arbitrary"),
    )(dq, dk, dv, proj, proj, gq, gkv, wq, wkk, wkv, tq, tk)


def _inproj_bwd(pieces, h, g, wx, dh_out, tm, name):
    T = h.shape[0]
    nsteps = T // tm
    CH = 512

    def body(*refs):
        p_refs = refs[:len(pieces)]
        h_ref, g_ref, w_ref, dho_ref, dh_ref, dw_hbm, dg_ref, acc, sem = refs[len(pieces):]
        i = pl.program_id(0)

        @pl.when(i == 0)
        def _():
            acc[...] = jnp.zeros_like(acc)
            dg_ref[...] = jnp.zeros_like(dg_ref)

        dp = jnp.concatenate([r[...].astype(BF16) for r in p_refs], axis=1)
        x, gain = h_ref[...], g_ref[...]
        r = _rstd(x)
        u = (x * r * gain).astype(BF16)
        for c in range(0, NX, CH):
            acc[:, c:c + CH] += _dot_tn(u, dp[:, c:c + CH])
        dx, dg = _rms_bwd(_dot_nt(dp, w_ref[...]), x, r, gain)
        dh_ref[...] = dho_ref[...] + dx
        dg_ref[...] += dg

        @pl.when(i == nsteps - 1)
        def _():
            cp = pltpu.make_async_copy(acc, dw_hbm, sem)
            cp.start()
            cp.wait()

    return pl.pallas_call(
        body, name=name, grid=(nsteps,),
        in_specs=[pl.BlockSpec((tm, p.shape[1]), _row(0)) for p in pieces]
        + [pl.BlockSpec((tm, D), _row(0)), pl.BlockSpec((1, D), _const2), pl.BlockSpec((D, NX), _const2),
           pl.BlockSpec((tm, D), _row(0))],
        out_specs=(pl.BlockSpec((tm, D), _row(0)), pl.BlockSpec(memory_space=pl.ANY), pl.BlockSpec((1, D), _const2)),
        out_shape=(jax.ShapeDtypeStruct((T, D), F32), jax.ShapeDtypeStruct((D, NX), F32), jax.ShapeDtypeStruct((1, D), F32)),
        scratch_shapes=[pltpu.VMEM((D, NX), F32), pltpu.SemaphoreType.DMA(())],
        compiler_params=_params("arbitrary"),
    )(*pieces, h, g, wx, dh_out)


def _flat(a):
    return a.reshape(-1, LANES)


def _pack_sharded(w_in, w_uq, w_ukv, w_out, meta):
    return jnp.concatenate([_flat(w_in), _flat(w_uq), _flat(w_ukv), _flat(w_out), _flat(meta)], axis=0)


def _unpack_sharded(p, like):
    out, r = [], 0
    for a in like:
        n = a.size // LANES
        out.append(p[r:r + n].reshape(a.shape))
        r += n
    return out


def _pack_small(parts):
    flat = jnp.concatenate([a.reshape(-1) for a in parts])
    return jnp.concatenate([flat, jnp.zeros((ROWS_SMALL * LANES - flat.shape[0],), F32)]).reshape(ROWS_SMALL, LANES)


def _unpack_small(p, like):
    flat, out, r = p.reshape(-1), [], 0
    for a in like:
        out.append(flat[r:r + a.size].reshape(a.shape))
        r += a.size
    return out


def kernel(x, meta_tokens, rel_bias_table, norm_in, w_in, sink_a, norm_q_lat, w_uq, norm_kv_lat, w_ukv, norm_out_a, norm_out_b, w_out, norm_final, loss_target, m_meta_tokens, m_rel_bias_table, m_norm_in, m_w_in, m_sink_a, m_norm_q_lat, m_w_uq, m_norm_kv_lat, m_w_ukv, m_norm_out_a, m_norm_out_b, m_w_out, m_norm_final, v_meta_tokens, v_rel_bias_table, v_norm_in, v_w_in, v_sink_a, v_norm_q_lat, v_w_uq, v_norm_kv_lat, v_w_ukv, v_norm_out_a, v_norm_out_b, v_w_out, v_norm_final):
    Bl, S, _ = x.shape
    Lp = S + BLK
    T = Bl * Lp
    tm = Lp // 4
    tq = Lp // 8
    depth = w_in.shape[0]

    wpack = jnp.concatenate([_flat(w_in), _flat(w_uq), _flat(w_ukv), _flat(w_out)], axis=0)
    gw, gmeta = _all_gather(wpack, meta_tokens)
    r0, r1, r2 = ROWS_W_IN, ROWS_W_IN + ROWS_W_UQ, ROWS_W_IN + ROWS_W_UQ + ROWS_W_UKV
    cols = lambda t, n: jnp.transpose(t, (1, 2, 0, 3)).reshape(depth, t.shape[2], NDEV * n)
    w_in_f = cols(gw[:, :r0].reshape(NDEV, depth, D, IN_W // NDEV), IN_W // NDEV)
    w_uq_f = cols(gw[:, r0:r1].reshape(NDEV, depth, Q_RANK, 768 // NDEV), 768 // NDEV)
    w_ukv_f = cols(gw[:, r1:r2].reshape(NDEV, depth, KV_RANK, 1024 // NDEV), 1024 // NDEV)
    w_out_f = jnp.transpose(gw[:, r2:].reshape(NDEV, depth, D // NDEV, D), (1, 0, 2, 3)).reshape(depth, D, D)
    meta_f = jnp.transpose(gmeta, (1, 0, 2)).reshape(NMETA, D)

    buckets = _bias_buckets(S)
    bias = _build_bias(buckets, rel_bias_table, "build_bias")
    tq_tab = _rope_tables(S, Lp, NOPE, True)
    tk_tab = _rope_tables(S, Lp, 0, False)
    kmask = jnp.where(jnp.arange(Lp) < S + NMETA, 0.0, NEG).astype(F32).reshape(1, Lp)

    tail = jnp.concatenate([meta_f, jnp.zeros((BLK - NMETA, D), F32)], axis=0)
    h = jnp.concatenate([x, jnp.broadcast_to(tail[None], (Bl, BLK, D))], axis=1).reshape(T, D)

    saved = []
    for i in range(depth):
        wx = _pack_w_in(w_in_f[i])
        wq = _pack_w_uq(w_uq_f[i])
        wkk, wkv = _pack_w_ukv(w_ukv_f[i])
        g_in, g_q, g_kv = norm_in[i][None], norm_q_lat[i][None], norm_kv_lat[i][None]
        g_a, g_b, sink = norm_out_a[i][None], norm_out_b[i][None], sink_a[i][None]
        proj = _inproj_fwd(h, g_in, wx, tm, f"inproj_fwd_{i}")
        q, k, v = _mla_prep_fwd(proj, g_q, g_kv, wq, wkk, wkv, tq_tab, tk_tab, tm, Lp, f"mla_prep_fwd_{i}")
        ya, lse_a = _win_fwd(proj, bias, sink, Bl, S, f"win_fwd_{i}")
        yb, lse_b = _mla_fwd(q, k, v, kmask, Bl, Lp, tq, f"mla_fwd_{i}")
        h_new = _out_fwd(ya, yb, proj, g_a, g_b, w_out_f[i], h, tm, f"out_fwd_{i}")
        saved.append((h, proj, q, k, v, ya, lse_a, yb, lse_b, wx, wq, wkk, wkv))
        h = h_new

    dh, loss_acc, dg_final = _loss_head(h, norm_final[None], loss_target, Bl, S, "loss_head")
    loss = lax.psum(loss_acc[0, 0], ("x", "y", "c"))

    g_w_in, g_w_uq, g_w_ukv, g_w_out = [None] * depth, [None] * depth, [None] * depth, [None] * depth
    g_n_in, g_sink, g_nq, g_nkv, g_na, g_nb = ([None] * depth for _ in range(6))
    dbias_all = []
    for i in reversed(range(depth)):
        h_in, proj, q, k, v, ya, lse_a, yb, lse_b, wx, wq, wkk, wkv = saved[i]
        g_in, g_q, g_kv = norm_in[i][None], norm_q_lat[i][None], norm_kv_lat[i][None]
        g_a, g_b, sink = norm_out_a[i][None], norm_out_b[i][None], sink_a[i][None]
        dya, dyb, dga, dgb, g_w_out[i], g_na[i], g_nb[i] = _out_bwd(dh, ya, yb, proj, g_a, g_b, w_out_f[i], tm, f"out_bwd_{i}")
        dq, dk, dv = _mla_bwd(q, k, v, kmask, yb, dyb, lse_b, Bl, Lp, tq, f"mla_bwd_{i}")
        dqa, dka, dva, dbias, dsink = _win_bwd(proj, bias, sink, ya, dya, lse_a, Bl, S, f"win_bwd_{i}")
        dcq, dckv, dkr, dwq, dwkk, dwkv, g_nq[i], g_nkv[i] = _mla_prep_bwd(
            dq, dk, dv, proj, g_q, g_kv, wq, wkk, wkv, tq_tab, tk_tab, tm, Lp, f"mla_prep_bwd_{i}")
        dh, dwx, g_n_in[i] = _inproj_bwd((dqa, dka, dva, dga, dcq, dckv, dkr, dgb), h_in, g_in, wx, dh, tq,
                                         f"inproj_bwd_{i}")
        g_w_in[i] = _unpack_w_in_grad(dwx)
        g_w_uq[i] = _unpack_w_uq_grad(dwq)
        g_w_ukv[i] = _unpack_w_ukv_grad(dwkk, dwkv)
        g_sink[i] = jnp.sum(dsink[:, :, 0], axis=0)
        dbias_all.append(dbias)

    dtab = _table_grad(jnp.concatenate(dbias_all, axis=0), buckets, "table_grad")
    g_table = jnp.transpose(dtab[:, :N_BUCKETS])
    dh3 = dh.reshape(Bl, Lp, D)
    grad_x = dh3[:, :S]
    g_meta = jnp.sum(dh3[:, S:S + NMETA], axis=0)

    gwi, gwq, gwk, gwo = jnp.stack(g_w_in), jnp.stack(g_w_uq), jnp.stack(g_w_ukv), jnp.stack(g_w_out)
    split = lambda t, n: jnp.transpose(t.reshape(t.shape[0], t.shape[1], NDEV, n), (2, 0, 1, 3)).reshape(NDEV, -1, LANES)
    gpack = jnp.concatenate([
        split(gwi, IN_W // NDEV), split(gwq, 768 // NDEV), split(gwk, 1024 // NDEV),
        jnp.transpose(gwo.reshape(depth, NDEV, D // NDEV, D), (1, 0, 2, 3)).reshape(NDEV, -1, LANES),
        jnp.transpose(g_meta.reshape(NMETA, NDEV, D // NDEV), (1, 0, 2)).reshape(NDEV, -1, LANES)], axis=1)
    small_w = [rel_bias_table, norm_in, sink_a, norm_q_lat, norm_kv_lat, norm_out_a, norm_out_b, norm_final]
    small_m = [m_rel_bias_table, m_norm_in, m_sink_a, m_norm_q_lat, m_norm_kv_lat, m_norm_out_a, m_norm_out_b, m_norm_final]
    small_v = [v_rel_bias_table, v_norm_in, v_sink_a, v_norm_q_lat, v_norm_kv_lat, v_norm_out_a, v_norm_out_b, v_norm_final]
    small_g = [g_table, jnp.concatenate(g_n_in), jnp.stack(g_sink), jnp.concatenate(g_nq), jnp.concatenate(g_nkv),
               jnp.concatenate(g_na), jnp.concatenate(g_nb), dg_final[0]]
    recv, srecv = _reduce_scatter(gpack, _pack_small(small_g))

    big_w = [w_in, w_uq, w_ukv, w_out, meta_tokens]
    outs_big = _sum_adamw(recv, _pack_sharded(*big_w), _pack_sharded(m_w_in, m_w_uq, m_w_ukv, m_w_out, m_meta_tokens),
                          _pack_sharded(v_w_in, v_w_uq, v_w_ukv, v_w_out, v_meta_tokens), "adamw_sharded")
    outs_small = _sum_adamw(srecv, _pack_small(small_w), _pack_small(small_m), _pack_small(small_v), "adamw_replicated")

    res = {}
    for kind, pb, ps in zip(("grad", "delta", "new_m", "new_v"), outs_big, outs_small):
        b = _unpack_sharded(pb, big_w)
        s = _unpack_small(ps, small_w)
        res[kind] = dict(w_in=b[0], w_uq=b[1], w_ukv=b[2], w_out=b[3], meta_tokens=b[4], rel_bias_table=s[0],
                         norm_in=s[1], sink_a=s[2], norm_q_lat=s[3], norm_kv_lat=s[4], norm_out_a=s[5],
                         norm_out_b=s[6], norm_final=s[7])
    order = ["meta_tokens", "rel_bias_table", "norm_in", "w_in", "sink_a", "norm_q_lat", "w_uq", "norm_kv_lat", "w_ukv",
             "norm_out_a", "norm_out_b", "w_out", "norm_final"]
    return (loss, grad_x, *[res[kind][n] for kind in ("grad", "delta", "new_m", "new_v") for n in order])
```

```python
import functools
import math

import numpy as np
import jax
import jax.numpy as jnp
from jax import lax
from jax.experimental import pallas as pl
from jax.experimental.pallas import tpu as pltpu

F32, BF16 = jnp.float32, jnp.bfloat16
D = 1024
NMETA = 16
BLK = 128
A_HEADS, A_KV, A_DH, A_W = 8, 2, 64, 512
B_HEADS, NOPE, ROPE, B_V, B_W = 8, 64, 32, 64, 512
Q_RANK, KV_RANK = 256, 128
IN_W = 2208
N_BUCKETS, MAX_DIST = 32, 128
THETA = 10000.0
EPS = 1e-6
NEG = -1e30
A_SCALE = A_DH ** -0.5
B_SCALE = (NOPE + ROPE) ** -0.5
LANES = 128
NDEV = 8
MESH = pl.DeviceIdType.MESH

NX = 2560
QA0, KA0, VA0, GA0, CQ0, CKV0, KR0, GB0 = 0, 512, 768, 1024, 1536, 1792, 1920, 2048

ADAM_LR, ADAM_B1, ADAM_B2, ADAM_EPS, ADAM_WD, ADAM_STEP = 0.001, 0.9, 0.999, 1e-08, 0.01, 10

ROWS_W_IN = 2 * D * (IN_W // NDEV) // LANES
ROWS_W_UQ = 2 * Q_RANK * (B_HEADS * (NOPE + ROPE) // NDEV) // LANES
ROWS_W_UKV = 2 * KV_RANK * (B_HEADS * (NOPE + B_V) // NDEV) // LANES
ROWS_W_OUT = 2 * (D // NDEV) * D // LANES
ROWS_W = ROWS_W_IN + ROWS_W_UQ + ROWS_W_UKV + ROWS_W_OUT
ROWS_META = NMETA * (D // NDEV) // LANES
ROWS_G = ROWS_W + ROWS_META
SMALL_SIZES = (N_BUCKETS * A_HEADS, 2 * D, 2 * A_HEADS, 2 * Q_RANK, 2 * KV_RANK, 2 * A_W, 2 * B_W, D)
ROWS_SMALL = 56


def _dot(a, b):
    return jnp.dot(a, b, preferred_element_type=F32)


def _dot_nt(a, b):
    return lax.dot_general(a, b, (((1,), (1,)), ((), ())), preferred_element_type=F32)


def _dot_tn(a, b):
    return lax.dot_general(a, b, (((0,), (0,)), ((), ())), preferred_element_type=F32)


def _lane(shape):
    return lax.broadcasted_iota(jnp.int32, shape, len(shape) - 1)


def _rstd(x):
    return lax.rsqrt(jnp.mean(x * x, axis=-1, keepdims=True) + EPS)


def _rms_bwd(dn, x, r, g):
    z = dn * g
    dx = r * z - x * (r * r * r) * jnp.mean(z * x, axis=-1, keepdims=True)
    return dx, jnp.sum(dn * (x * r), axis=0, keepdims=True)


def _row(i):
    return lambda *ids: (ids[0], i)


def _const2(*ids):
    return (0, 0)


def _params(*sem):
    return pltpu.CompilerParams(dimension_semantics=sem)


def _pack_w_in(w):
    qa, ka, va, ga = w[:, 0:512], w[:, 512:640], w[:, 640:768], w[:, 768:1280]
    cq, ckv, kr, gb = w[:, 1280:1536], w[:, 1536:1664], w[:, 1664:1696], w[:, 1696:2208]
    dup = lambda t: jnp.concatenate([t[:, 0:64], t[:, 0:64], t[:, 64:128], t[:, 64:128]], axis=1)
    krp = jnp.concatenate([kr, jnp.zeros((w.shape[0], LANES - ROPE), w.dtype)], axis=1)
    return jnp.concatenate([qa, dup(ka), dup(va), ga, cq, ckv, krp, gb], axis=1)


def _unpack_w_in_grad(g):
    fold = lambda t: jnp.concatenate([t[:, 0:64] + t[:, 64:128], t[:, 128:192] + t[:, 192:256]], axis=1)
    return jnp.concatenate([g[:, QA0:QA0 + 512], fold(g[:, KA0:KA0 + 256]), fold(g[:, VA0:VA0 + 256]),
                            g[:, GA0:GA0 + 512], g[:, CQ0:CQ0 + 256], g[:, CKV0:CKV0 + 128],
                            g[:, KR0:KR0 + ROPE], g[:, GB0:GB0 + 512]], axis=1)


def _pack_w_uq(w):
    t = w.reshape(Q_RANK, B_HEADS, NOPE + ROPE)
    t = jnp.concatenate([t, jnp.zeros((Q_RANK, B_HEADS, LANES - NOPE - ROPE), w.dtype)], axis=-1)
    return t.reshape(Q_RANK, B_HEADS * LANES)


def _unpack_w_uq_grad(g):
    return g.reshape(Q_RANK, B_HEADS, LANES)[:, :, :NOPE + ROPE].reshape(Q_RANK, B_HEADS * (NOPE + ROPE))


def _pack_w_ukv(w):
    t = w.reshape(KV_RANK, B_HEADS, NOPE + B_V)
    kk = jnp.concatenate([t[:, :, :NOPE], jnp.zeros((KV_RANK, B_HEADS, LANES - NOPE), w.dtype)], axis=-1)
    return kk.reshape(KV_RANK, B_HEADS * LANES), t[:, :, NOPE:].reshape(KV_RANK, B_W)


def _unpack_w_ukv_grad(gk, gv):
    t = jnp.concatenate([gk.reshape(KV_RANK, B_HEADS, LANES)[:, :, :NOPE], gv.reshape(KV_RANK, B_HEADS, B_V)], axis=-1)
    return t.reshape(KV_RANK, B_HEADS * (NOPE + B_V))


def _t5_bucket(rel):
    nb = N_BUCKETS // 2
    max_exact = nb // 2
    ret = jnp.where(rel > 0, nb, 0)
    n = jnp.abs(rel)
    nf = jnp.maximum(n, 1).astype(F32)
    large = max_exact + (jnp.log(nf / max_exact) / math.log(MAX_DIST / max_exact) * (nb - max_exact)).astype(jnp.int32)
    large = jnp.minimum(large, nb - 1)
    return ret + jnp.where(n < max_exact, n, large)


def _bias_buckets(S):
    q = np.arange(BLK)[:, None]
    k = np.arange(4 * BLK)[None, :]
    is_meta_key = (k >= 3 * BLK) & (k < 3 * BLK + NMETA)
    mi = k - 3 * BLK
    rels, valids = [], []
    for shift in (0, BLK, 2 * BLK):
        rel_real = k - shift - q
        valid_real = (k < 3 * BLK) & (np.abs(rel_real) <= BLK)
        far = -(NMETA + MAX_DIST + BLK)
        rel_meta = (mi - (NMETA + q)) if shift == 0 else np.full_like(k + q, far)
        rels.append(np.where(is_meta_key, rel_meta, rel_real))
        valids.append(valid_real | is_meta_key)
    qm = q < NMETA
    rel_real = NMETA + k - q
    valid_real = (k < BLK) & (np.abs(rel_real) <= BLK) & qm
    rels.append(np.where(is_meta_key, mi - q, rel_real))
    valids.append(valid_real | is_meta_key)
    rel = jnp.asarray(np.stack(rels).astype(np.int32))
    valid = jnp.asarray(np.stack(valids))
    return jnp.where(valid, _t5_bucket(rel), -1).astype(jnp.int32)


def _rope_tables(S, Lp, off, passthrough):
    half = ROPE // 2
    r = np.arange(Lp)
    pos = np.where(r < S, NMETA + r, np.where(r < S + NMETA, r - S, 0)).astype(np.float32)
    freqs = THETA ** (-jnp.arange(half, dtype=F32) / half)
    ang = jnp.asarray(pos)[:, None] * freqs[None, :]
    cos, sin = jnp.cos(ang), jnp.sin(ang)
    z = lambda n: jnp.zeros((Lp, n), F32)
    head = jnp.ones((Lp, off), F32) if passthrough else z(off)
    c = jnp.concatenate([head, cos, cos, z(LANES - off - ROPE)], axis=1)
    s1 = jnp.concatenate([z(off), -sin, z(LANES - off - half)], axis=1)
    s2 = jnp.concatenate([z(off + half), sin, z(LANES - off - ROPE)], axis=1)
    return jnp.stack([c, s1, s2])


def _rope(x, t):
    return x * t[0] + pltpu.roll(x, LANES - 16, 1) * t[1] + pltpu.roll(x, 16, 1) * t[2]


def _rope_t(dy, t):
    return dy * t[0] + pltpu.roll(dy * t[1], 16, 1) + pltpu.roll(dy * t[2], LANES - 16, 1)


def _me_and_peers():
    x, y, c = lax.axis_index("x"), lax.axis_index("y"), lax.axis_index("c")
    flip = lambda v, b: 1 - v if b else v
    peers = [(flip(x, k & 4), flip(y, k & 2), flip(c, k & 1)) for k in range(1, NDEV)]
    return 4 * x + 2 * y + c, peers


def _xchg_start(srcs, kinds, name):
    n = len(srcs)
    lands = [lax.empty((NDEV,) + s.shape[-2:], s.dtype) for s in srcs]

    def body(*refs):
        src_refs, land_refs = refs[:n], refs[n:2 * n]
        ssems, rsems = refs[2 * n:3 * n], refs[3 * n:4 * n]
        token = refs[6 * n]
        me, peers = _me_and_peers()
        for it in range(n):
            for k, (px, py, pc) in enumerate(peers):
                src = src_refs[it] if kinds[it] == "gather" else src_refs[it].at[4 * px + 2 * py + pc]
                pltpu.make_async_remote_copy(src_ref=src, dst_ref=land_refs[it].at[me], send_sem=ssems[it].at[k],
                                             recv_sem=rsems[it].at[k], device_id=(px, py, pc), device_id_type=MESH).start()
        token[...] = jnp.zeros_like(token)

    hbm = pl.BlockSpec(memory_space=pltpu.HBM)
    sem = pl.BlockSpec(memory_space=pltpu.SEMAPHORE)
    outs = pl.pallas_call(
        body, name=name,
        out_shape=tuple([pltpu.SemaphoreType.DMA((NDEV - 1,))] * (2 * n) + [pltpu.HBM(a.shape, a.dtype) for a in srcs + lands]
                        + [jax.ShapeDtypeStruct((8, LANES), F32)]),
        in_specs=[hbm] * (2 * n), out_specs=tuple([sem] * (2 * n) + [hbm] * (2 * n) + [pl.BlockSpec(memory_space=pltpu.VMEM)]),
        input_output_aliases={i: 2 * n + i for i in range(2 * n)},
        compiler_params=pltpu.CompilerParams(has_side_effects=pltpu.SideEffectType.DATAFLOW_SIDE_EFFECTING),
    )(*[pltpu.with_memory_space_constraint(a, pltpu.HBM) for a in srcs + lands])
    handles = [(outs[it], outs[n + it], outs[2 * n + it], outs[3 * n + it], kinds[it]) for it in range(n)]
    return handles, outs[4 * n]


def _xchg_wait(handles, after, name):
    n = len(handles)

    def body(*refs):
        src_refs, land_refs = refs[:n], refs[n:2 * n]
        ssems, rsems = refs[2 * n:3 * n], refs[3 * n:4 * n]
        me, peers = _me_and_peers()
        for it in range(n):
            for k, (px, py, pc) in enumerate(peers):
                src = src_refs[it] if handles[it][4] == "gather" else src_refs[it].at[4 * px + 2 * py + pc]
                cp = pltpu.make_async_remote_copy(src_ref=src, dst_ref=land_refs[it].at[me], send_sem=ssems[it].at[k],
                                                  recv_sem=rsems[it].at[k], device_id=(px, py, pc), device_id_type=MESH)
                cp.wait_send()
                cp.wait_recv()

    hbm = pl.BlockSpec(memory_space=pltpu.HBM)
    sem = pl.BlockSpec(memory_space=pltpu.SEMAPHORE)
    srcs, lands = [h[2] for h in handles], [h[3] for h in handles]
    outs = pl.pallas_call(
        body, name=name, out_shape=tuple(pltpu.HBM(a.shape, a.dtype) for a in srcs + lands),
        in_specs=[hbm] * (2 * n) + [sem] * (2 * n) + [pl.BlockSpec(memory_space=pl.ANY)], out_specs=tuple([hbm] * (2 * n)),
        input_output_aliases={i: i for i in range(2 * n)},
        compiler_params=pltpu.CompilerParams(has_side_effects=pltpu.SideEffectType.DATAFLOW_SIDE_EFFECTING),
    )(*srcs, *lands, *[h[0] for h in handles], *[h[1] for h in handles], after)
    return list(outs[n:])


def _own_slot(land, own, me):
    return lax.dynamic_update_slice(land, own[None].astype(land.dtype), (me, 0, 0))


def _row_tile(R):
    for k in range(1, R + 1):
        if R % k == 0 and (R // k) % 8 == 0 and R // k <= 2048:
            return R // k
    return R


def _sum_adamw(recv, w, m, v, name):
    R = w.shape[0]
    tr = _row_tile(R)
    c1 = 1.0 - ADAM_B1 ** ADAM_STEP
    c2 = 1.0 - ADAM_B2 ** ADAM_STEP

    def body(r_ref, w_ref, m_ref, v_ref, g_out, d_out, m_out, v_out):
        g = r_ref[0].astype(F32)
        for p in range(1, NDEV):
            g = g + r_ref[p].astype(F32)
        mn = ADAM_B1 * m_ref[...] + (1.0 - ADAM_B1) * g
        vn = ADAM_B2 * v_ref[...] + (1.0 - ADAM_B2) * (g * g)
        m_hat = mn / c1
        v_hat = vn / c2
        g_out[...] = g
        d_out[...] = -ADAM_LR * (m_hat / (jnp.sqrt(v_hat) + ADAM_EPS) + ADAM_WD * w_ref[...])
        m_out[...] = mn
        v_out[...] = vn

    blk = pl.BlockSpec((tr, LANES), lambda i: (i, 0))
    return pl.pallas_call(
        body, name=name, grid=(R // tr,),
        in_specs=[pl.BlockSpec((NDEV, tr, LANES), lambda i: (0, i, 0)), blk, blk, blk],
        out_specs=(blk, blk, blk, blk), out_shape=(jax.ShapeDtypeStruct((R, LANES), F32),) * 4,
        compiler_params=_params("parallel"),
    )(recv, w, m, v)


def _inproj_fwd(h, g, wx, tm, name):
    T = h.shape[0]

    def body(h_ref, g_ref, w_ref, o_ref):
        x = h_ref[...]
        u = (x * _rstd(x) * g_ref[...]).astype(BF16)
        o_ref[...] = _dot(u, w_ref[...])

    return pl.pallas_call(
        body, name=name, grid=(T // tm,),
        in_specs=[pl.BlockSpec((tm, D), _row(0)), pl.BlockSpec((1, D), _const2), pl.BlockSpec((D, NX), _const2)],
        out_specs=pl.BlockSpec((tm, NX), _row(0)), out_shape=jax.ShapeDtypeStruct((T, NX), F32),
        compiler_params=_params("parallel"),
    )(h, g, wx)


def _mla_prep_fwd(proj, gq, gkv, wq, wkk, wkv, tq, tk, tm, Lp, name):
    T = proj.shape[0]
    nt = Lp // tm

    def body(cq_ref, ckv_ref, kr_ref, gq_ref, gkv_ref, wq_ref, wkk_ref, wkv_ref, tq_ref, tk_ref, q_ref, k_ref, v_ref):
        cq = cq_ref[...]
        q = _dot((cq * _rstd(cq) * gq_ref[...]).astype(BF16), wq_ref[...])
        tqv = tq_ref[...]
        for h in range(B_HEADS):
            cs = slice(LANES * h, LANES * (h + 1))
            q_ref[:, cs] = _rope(q[:, cs], tqv).astype(BF16)
        ksh = pltpu.roll(_rope(kr_ref[...], tk_ref[...]), NOPE, 1)
        ckv = ckv_ref[...]
        cb = (ckv * _rstd(ckv) * gkv_ref[...]).astype(BF16)
        kn = _dot(cb, wkk_ref[...])
        for h in range(B_HEADS):
            cs = slice(LANES * h, LANES * (h + 1))
            k_ref[:, cs] = (kn[:, cs] + ksh).astype(BF16)
        v_ref[...] = _dot(cb, wkv_ref[...]).astype(BF16)

    tab = pl.BlockSpec((3, tm, LANES), lambda i: (0, i % nt, 0))
    return pl.pallas_call(
        body, name=name, grid=(T // tm,),
        in_specs=[pl.BlockSpec((tm, 256), _row(CQ0 // 256)), pl.BlockSpec((tm, 128), _row(CKV0 // 128)),
                  pl.BlockSpec((tm, 128), _row(KR0 // 128)), pl.BlockSpec((1, Q_RANK), _const2),
                  pl.BlockSpec((1, KV_RANK), _const2), pl.BlockSpec((Q_RANK, 1024), _const2),
                  pl.BlockSpec((KV_RANK, 1024), _const2), pl.BlockSpec((KV_RANK, B_W), _const2), tab, tab],
        out_specs=(pl.BlockSpec((tm, 1024), _row(0)), pl.BlockSpec((tm, 1024), _row(0)), pl.BlockSpec((tm, B_W), _row(0))),
        out_shape=(jax.ShapeDtypeStruct((T, 1024), BF16), jax.ShapeDtypeStruct((T, 1024), BF16),
                   jax.ShapeDtypeStruct((T, B_W), BF16)),
        compiler_params=_params("parallel"),
    )(proj, proj, proj, gq, gkv, wq, wkk, wkv, tq, tk)


def _build_bias(buckets, table, name):
    def body(tab_ref, b_ref, o_ref):
        bidx = b_ref[0]
        for h in range(A_HEADS):
            def step(j, acc):
                return jnp.where(bidx == j, tab_ref[j, h], acc)
            o_ref[0, h] = lax.fori_loop(0, N_BUCKETS, step, jnp.full(bidx.shape, NEG, F32))

    return pl.pallas_call(
        body, name=name, grid=(4,),
        in_specs=[pl.BlockSpec(memory_space=pltpu.SMEM), pl.BlockSpec((1, BLK, 4 * BLK), lambda i: (i, 0, 0))],
        out_specs=pl.BlockSpec((1, A_HEADS, BLK, 4 * BLK), lambda i: (i, 0, 0, 0)),
        out_shape=jax.ShapeDtypeStruct((4, A_HEADS, BLK, 4 * BLK), F32),
        compiler_params=_params("parallel"),
    )(table, buckets)


def _win_base(n, NB):
    return jnp.where(n == NB, 0, jnp.clip(n - 1, 0, NB - 3))


def _win_variant(n, NB):
    return jnp.where(n == 0, 0, jnp.where(n < NB - 1, 1, jnp.where(n == NB - 1, 2, 3)))


def _win_fwd(proj, bias, sink, Bl, S, name):
    T = proj.shape[0]
    Lp = S + BLK
    NB = S // BLK
    nblk = Lp // BLK

    def body(sink_ref, q_ref, k_ref, v_ref, b_ref, o_ref, lse_ref):
        n = pl.program_id(1)
        base = pl.multiple_of(_win_base(n, NB) * BLK, BLK)
        lane = _lane((BLK, LANES))
        lo = lane < 64
        outs, lses = [None] * A_HEADS, [None] * A_HEADS
        for kvh in range(A_KV):
            cs = slice(LANES * kvh, LANES * (kvh + 1))
            k2 = jnp.concatenate([k_ref[pl.ds(base, 3 * BLK), cs], k_ref[S:S + BLK, cs]], axis=0).astype(BF16)
            v2 = jnp.concatenate([v_ref[pl.ds(base, 3 * BLK), cs], v_ref[S:S + BLK, cs]], axis=0).astype(BF16)
            for gi in range(A_HEADS // A_KV):
                h = kvh * (A_HEADS // A_KV) + gi
                qp = q_ref[:, LANES * (h // 2):LANES * (h // 2 + 1)]
                qm = jnp.where(lo if h % 2 == 0 else ~lo, qp, 0.0).astype(BF16)
                s = _dot_nt(qm, k2) * A_SCALE + b_ref[0, h]
                sk = sink_ref[0, h]
                m = jnp.maximum(jnp.max(s, axis=-1, keepdims=True), sk)
                e = jnp.exp(s - m)
                den = jnp.sum(e, axis=-1, keepdims=True) + jnp.exp(sk - m)
                p = e * (1.0 / den)
                outs[h] = _dot(p.astype(BF16), v2)
                lses[h] = m + jnp.log(den)
        for j in range(A_HEADS // 2):
            o_ref[:, LANES * j:LANES * (j + 1)] = jnp.where(lo, outs[2 * j], outs[2 * j + 1])
        t = jnp.zeros((BLK, LANES), F32)
        for h in range(A_HEADS):
            t = jnp.where(lane == h, lses[h], t)
        lse_ref[...] = t

    qrow = lambda b, n: (b * nblk + n, 0)
    return pl.pallas_call(
        body, name=name, grid=(Bl, nblk),
        in_specs=[pl.BlockSpec(memory_space=pltpu.SMEM), pl.BlockSpec((BLK, A_W), qrow),
                  pl.BlockSpec((Lp, 256), lambda b, n: (b, KA0 // 256)), pl.BlockSpec((Lp, 256), lambda b, n: (b, VA0 // 256)),
                  pl.BlockSpec((1, A_HEADS, BLK, 4 * BLK), lambda b, n: (_win_variant(n, NB), 0, 0, 0))],
        out_specs=(pl.BlockSpec((BLK, A_W), qrow), pl.BlockSpec((BLK, LANES), qrow)),
        out_shape=(jax.ShapeDtypeStruct((T, A_W), F32), jax.ShapeDtypeStruct((T, LANES), F32)),
        compiler_params=_params("parallel", "arbitrary"),
    )(sink, proj, proj, proj, bias)


def _mla_fwd(q, k, v, kmask, Bl, Lp, tq, name):
    T = q.shape[0]
    nq = Lp // tq

    def body(q_ref, k_ref, v_ref, km_ref, o_ref, lse_ref):
        lane = _lane((tq, LANES))
        vb = v_ref[...]
        km = km_ref[...]
        outs = []
        t = jnp.zeros((tq, LANES), F32)
        for hh in range(2):
            cs = slice(LANES * hh, LANES * (hh + 1))
            s = _dot_nt(q_ref[:, cs], k_ref[:, cs]) * B_SCALE + km
            m = jnp.max(s, axis=-1, keepdims=True)
            e = jnp.exp(s - m)
            den = jnp.sum(e, axis=-1, keepdims=True)
            outs.append(_dot(e.astype(BF16), vb) * (1.0 / den))
            t = jnp.where(lane == hh, m + jnp.log(den), t)
        o_ref[...] = jnp.where(lane < 64, outs[0], outs[1])
        lse_ref[...] = t

    qrow = lambda b, j, i: (b * nq + i, j)
    return pl.pallas_call(
        body, name=name, grid=(Bl, B_HEADS // 2, nq),
        in_specs=[pl.BlockSpec((tq, 256), qrow), pl.BlockSpec((Lp, 256), lambda b, j, i: (b, j)),
                  pl.BlockSpec((Lp, LANES), lambda b, j, i: (b, j)), pl.BlockSpec((1, Lp), lambda b, j, i: (0, 0))],
        out_specs=(pl.BlockSpec((tq, LANES), qrow), pl.BlockSpec((tq, LANES), qrow)),
        out_shape=(jax.ShapeDtypeStruct((T, B_W), F32), jax.ShapeDtypeStruct((T, B_W), F32)),
        compiler_params=_params("parallel", "parallel", "arbitrary"),
    )(q, k, v, kmask)


def _gated(y, gate, gain):
    r = _rstd(y)
    nrm = y * r
    sg = jax.nn.sigmoid(gate)
    return r, nrm, sg, nrm * gain, gate * sg


def _out_fwd(ya, yb, proj, na, nb, wout, h, tm, name):
    T = h.shape[0]

    def body(ya_ref, yb_ref, ga_ref, gb_ref, na_ref, nb_ref, w_ref, h_ref, o_ref):
        _, _, _, n_a, sl_a = _gated(ya_ref[...], ga_ref[...], na_ref[...])
        _, _, _, n_b, sl_b = _gated(yb_ref[...], gb_ref[...], nb_ref[...])
        acc = _dot((n_a * sl_a).astype(BF16), w_ref[0:A_W, :]) + _dot((n_b * sl_b).astype(BF16), w_ref[A_W:A_W + B_W, :])
        o_ref[...] = h_ref[...] + acc

    half = pl.BlockSpec((tm, 512), _row(0))
    return pl.pallas_call(
        body, name=name, grid=(T // tm,),
        in_specs=[half, half, pl.BlockSpec((tm, 512), _row(GA0 // 512)), pl.BlockSpec((tm, 512), _row(GB0 // 512)),
                  pl.BlockSpec((1, 512), _const2), pl.BlockSpec((1, 512), _const2), pl.BlockSpec((D, D), _const2),
                  pl.BlockSpec((tm, D), _row(0))],
        out_specs=pl.BlockSpec((tm, D), _row(0)), out_shape=jax.ShapeDtypeStruct((T, D), F32),
        compiler_params=_params("parallel"),
    )(ya, yb, proj, proj, na, nb, wout, h)


def _loss_head(h, gf, target, Bl, S, name):
    T = h.shape[0]
    nblk = (S + BLK) // BLK
    NB = S // BLK

    def body(h_ref, g_ref, t_ref, dh_ref, loss_ref, dg_ref):
        b, n = pl.program_id(0), pl.program_id(1)

        @pl.when((b == 0) & (n == 0))
        def _():
            loss_ref[...] = jnp.zeros_like(loss_ref)
            dg_ref[...] = jnp.zeros_like(dg_ref)

        @pl.when(n < NB)
        def _():
            x = h_ref[...]
            g = g_ref[...]
            r = _rstd(x)
            err = x * r * g - t_ref[0]
            loss_ref[...] += 0.5 * jnp.sum(jnp.mean(err * err, axis=-1, keepdims=True))
            dx, dg = _rms_bwd(err * (1.0 / D), x, r, g)
            dh_ref[...] = dx
            dg_ref[...] += dg

        @pl.when(n >= NB)
        def _():
            dh_ref[...] = jnp.zeros_like(dh_ref)

    return pl.pallas_call(
        body, name=name, grid=(Bl, nblk),
        in_specs=[pl.BlockSpec((BLK, D), lambda b, n: (b * nblk + n, 0)), pl.BlockSpec((1, D), lambda b, n: (0, 0)),
                  pl.BlockSpec((1, BLK, D), lambda b, n: (b, jnp.minimum(n, NB - 1), 0))],
        out_specs=(pl.BlockSpec((BLK, D), lambda b, n: (b * nblk + n, 0)), pl.BlockSpec((8, LANES), lambda b, n: (0, 0)),
                   pl.BlockSpec((1, D), lambda b, n: (0, 0))),
        out_shape=(jax.ShapeDtypeStruct((T, D), F32), jax.ShapeDtypeStruct((8, LANES), F32), jax.ShapeDtypeStruct((1, D), F32)),
        compiler_params=_params("arbitrary", "arbitrary"),
    )(h, gf, target)


def _out_bwd(dh, ya, yb, proj, na, nb, wout, tm, name):
    T = dh.shape[0]

    def body(dh_ref, ya_ref, yb_ref, ga_ref, gb_ref, na_ref, nb_ref, w_ref,
             dya_ref, dyb_ref, dga_ref, dgb_ref, dw_ref, dna_ref, dnb_ref):
        @pl.when(pl.program_id(0) == 0)
        def _():
            dw_ref[...] = jnp.zeros_like(dw_ref)
            dna_ref[...] = jnp.zeros_like(dna_ref)
            dnb_ref[...] = jnp.zeros_like(dnb_ref)

        dhb = dh_ref[...].astype(BF16)
        dy = _dot_nt(dhb, w_ref[...])
        parts = ((ya_ref, ga_ref, na_ref, dya_ref, dga_ref, dna_ref, 0), (yb_ref, gb_ref, nb_ref, dyb_ref, dgb_ref, dnb_ref, A_W))
        for y_ref, gate_ref, gain_ref, dy_out, dgate_out, dgain_out, c0 in parts:
            y, gate, gain = y_ref[...], gate_ref[...], gain_ref[...]
            r, nrm, sg, n_g, sl = _gated(y, gate, gain)
            dyp = dy[:, c0:c0 + 512]
            dgate_out[...] = dyp * n_g * (sg * (1.0 + gate * (1.0 - sg)))
            dx, dgain = _rms_bwd(dyp * sl, y, r, gain)
            dy_out[...] = dx
            dgain_out[...] += dgain
            dw_ref[c0:c0 + 512, :] += _dot_tn((n_g * sl).astype(BF16), dhb)

    half = pl.BlockSpec((tm, 512), _row(0))
    vec = pl.BlockSpec((1, 512), _const2)
    return pl.pallas_call(
        body, name=name, grid=(T // tm,),
        in_specs=[pl.BlockSpec((tm, D), _row(0)), half, half, pl.BlockSpec((tm, 512), _row(GA0 // 512)),
                  pl.BlockSpec((tm, 512), _row(GB0 // 512)), vec, vec, pl.BlockSpec((D, D), _const2)],
        out_specs=(half, half, half, half, pl.BlockSpec((D, D), _const2), vec, vec),
        out_shape=(jax.ShapeDtypeStruct((T, 512), F32),) * 4 + (jax.ShapeDtypeStruct((D, D), F32),)
        + (jax.ShapeDtypeStruct((1, 512), F32),) * 2,
        compiler_params=_params("arbitrary"),
    )(dh, ya, yb, proj, proj, na, nb, wout)


def _mla_bwd(q, k, v, kmask, o, do, lse, Bl, Lp, tq, name):
    T = q.shape[0]
    nq = Lp // tq

    def body(q_ref, k_ref, v_ref, km_ref, o_ref, do_ref, lse_ref, dq_ref, dk_ref, dv_ref):
        @pl.when(pl.program_id(2) == 0)
        def _():
            dk_ref[...] = jnp.zeros_like(dk_ref)
            dv_ref[...] = jnp.zeros_like(dv_ref)

        lane = _lane((tq, LANES))
        lo = lane < 64
        vb = v_ref[...]
        km = km_ref[...]
        lsev = lse_ref[...]
        dop = do_ref[...]
        op = o_ref[...]
        for hh in range(2):
            cs = slice(LANES * hh, LANES * (hh + 1))
            qh, kh = q_ref[:, cs], k_ref[:, cs]
            s = _dot_nt(qh, kh) * B_SCALE + km
            lse_h = jnp.sum(jnp.where(lane == hh, lsev, 0.0), axis=-1, keepdims=True)
            p = jnp.exp(s - lse_h)
            dom = jnp.where(lo if hh == 0 else ~lo, dop, 0.0)
            delta = jnp.sum(dom * op, axis=-1, keepdims=True)
            domb = dom.astype(BF16)
            dp = _dot_nt(domb, vb)
            ds = (p * (dp - delta) * B_SCALE).astype(BF16)
            dq_ref[:, cs] = _dot(ds, kh)
            dk_ref[:, cs] += _dot_tn(ds, qh)
            dv_ref[...] += _dot_tn(p.astype(BF16), domb)

    qrow = lambda b, j, i: (b * nq + i, j)
    kv = lambda b, j, i: (b, j)
    return pl.pallas_call(
        body, name=name, grid=(Bl, B_HEADS // 2, nq),
        in_specs=[pl.BlockSpec((tq, 256), qrow), pl.BlockSpec((Lp, 256), kv), pl.BlockSpec((Lp, LANES), kv),
                  pl.BlockSpec((1, Lp), lambda b, j, i: (0, 0)), pl.BlockSpec((tq, LANES), qrow),
                  pl.BlockSpec((tq, LANES), qrow), pl.BlockSpec((tq, LANES), qrow)],
        out_specs=(pl.BlockSpec((tq, 256), qrow), pl.BlockSpec((Lp, 256), kv), pl.BlockSpec((Lp, LANES), kv)),
        out_shape=(jax.ShapeDtypeStruct((T, 1024), F32), jax.ShapeDtypeStruct((T, 1024), F32),
                   jax.ShapeDtypeStruct((T, B_W), F32)),
        compiler_params=_params("parallel", "parallel", "arbitrary"),
    )(q, k, v, kmask, o, do, lse)


def _win_bwd(proj, bias, sink, ya, dya, lse, Bl, S, name):
    T = proj.shape[0]
    Lp = S + BLK
    NB = S // BLK
    nblk = Lp // BLK
    G = A_HEADS // A_KV

    def body(sink_ref, q_ref, k_ref, v_ref, b_ref, o_ref, do_ref, lse_ref, dq_ref, dk_ref, dv_ref, db_ref, ds_ref):
        n = pl.program_id(1)

        @pl.when(n == 0)
        def _():
            dk_ref[...] = jnp.zeros_like(dk_ref)
            dv_ref[...] = jnp.zeros_like(dv_ref)
            ds_ref[...] = jnp.zeros_like(ds_ref)

        @pl.when((n == 0) | (n == 1) | (n == NB - 1) | (n == NB))
        def _():
            db_ref[...] = jnp.zeros_like(db_ref)

        base = pl.multiple_of(_win_base(n, NB) * BLK, BLK)
        lane = _lane((BLK, LANES))
        lo = lane < 64
        lsev = lse_ref[...]
        dqs = [None] * A_HEADS
        for kvh in range(A_KV):
            cs = slice(LANES * kvh, LANES * (kvh + 1))
            k2 = jnp.concatenate([k_ref[pl.ds(base, 3 * BLK), cs], k_ref[S:S + BLK, cs]], axis=0).astype(BF16)
            v2 = jnp.concatenate([v_ref[pl.ds(base, 3 * BLK), cs], v_ref[S:S + BLK, cs]], axis=0).astype(BF16)
            dk_acc = jnp.zeros((4 * BLK, LANES), F32)
            dv_acc = jnp.zeros((4 * BLK, LANES), F32)
            for gi in range(G):
                h = kvh * G + gi
                ps = slice(LANES * (h // 2), LANES * (h // 2 + 1))
                hm = lo if h % 2 == 0 else ~lo
                qm = jnp.where(hm, q_ref[:, ps], 0.0).astype(BF16)
                s = _dot_nt(qm, k2) * A_SCALE + b_ref[0, h]
                lse_h = jnp.sum(jnp.where(lane == h, lsev, 0.0), axis=-1, keepdims=True)
                p = jnp.exp(s - lse_h)
                dom = jnp.where(hm, do_ref[:, ps], 0.0)
                delta = jnp.sum(dom * o_ref[:, ps], axis=-1, keepdims=True)
                domb = dom.astype(BF16)
                dsc = p * (_dot_nt(domb, v2) - delta)
                db_ref[0, 0, h] += dsc
                dsink = -jnp.sum(jnp.exp(sink_ref[0, h] - lse_h) * delta, axis=0, keepdims=True)
                ds_ref[0, h:h + 1, :] += jnp.broadcast_to(dsink, (1, LANES))
                dsl = (dsc * A_SCALE).astype(BF16)
                dqs[h] = _dot(dsl, k2)
                dk_acc = dk_acc + _dot_tn(dsl, qm)
                dv_acc = dv_acc + _dot_tn(p.astype(BF16), domb)
            dk_ref[pl.ds(base, 3 * BLK), cs] += dk_acc[0:3 * BLK]
            dk_ref[S:S + BLK, cs] += dk_acc[3 * BLK:4 * BLK]
            dv_ref[pl.ds(base, 3 * BLK), cs] += dv_acc[0:3 * BLK]
            dv_ref[S:S + BLK, cs] += dv_acc[3 * BLK:4 * BLK]
        for j in range(A_HEADS // 2):
            dq_ref[:, LANES * j:LANES * (j + 1)] = jnp.where(lo, dqs[2 * j], dqs[2 * j + 1])

    qrow = lambda b, n: (b * nblk + n, 0)
    kvs = pl.BlockSpec((Lp, 256), lambda b, n: (b, 0))
    return pl.pallas_call(
        body, name=name, grid=(Bl, nblk),
        in_specs=[pl.BlockSpec(memory_space=pltpu.SMEM), pl.BlockSpec((BLK, A_W), qrow),
                  pl.BlockSpec((Lp, 256), lambda b, n: (b, KA0 // 256)), pl.BlockSpec((Lp, 256), lambda b, n: (b, VA0 // 256)),
                  pl.BlockSpec((1, A_HEADS, BLK, 4 * BLK), lambda b, n: (_win_variant(n, NB), 0, 0, 0)),
                  pl.BlockSpec((BLK, A_W), qrow), pl.BlockSpec((BLK, A_W), qrow), pl.BlockSpec((BLK, LANES), qrow)],
        out_specs=(pl.BlockSpec((BLK, A_W), qrow), kvs, kvs,
                   pl.BlockSpec((1, 1, A_HEADS, BLK, 4 * BLK), lambda b, n: (b, _win_variant(n, NB), 0, 0, 0)),
                   pl.BlockSpec((1, 8, LANES), lambda b, n: (b, 0, 0))),
        out_shape=(jax.ShapeDtypeStruct((T, A_W), F32), jax.ShapeDtypeStruct((T, 256), F32),
                   jax.ShapeDtypeStruct((T, 256), F32), jax.ShapeDtypeStruct((Bl, 4, A_HEADS, BLK, 4 * BLK), F32),
                   jax.ShapeDtypeStruct((Bl, 8, LANES), F32)),
        compiler_params=_params("parallel", "arbitrary"),
    )(sink, proj, proj, proj, bias, ya, dya, lse)


def _table_grad(dbias_list, buckets, name):
    nl = len(dbias_list)
    Bl = dbias_list[0].shape[0]

    def body(*refs):
        d_refs = refs[:nl]
        b_ref, o_ref, acc, part = refs[nl:]
        v, b = pl.program_id(0), pl.program_id(1)

        @pl.when((v == 0) & (b == 0))
        def _():
            o_ref[...] = jnp.zeros_like(o_ref)

        tot = d_refs[0][0, 0]
        for r in d_refs[1:]:
            tot = tot + r[0, 0]

        @pl.when(b == 0)
        def _():
            acc[...] = tot

        @pl.when(b > 0)
        def _():
            acc[...] += tot

        @pl.when(b == Bl - 1)
        def _():
            bidx = b_ref[0]

            def step(j, c):
                mask = bidx == j
                for h in range(A_HEADS):
                    part[h, j] = jnp.sum(jnp.where(mask, acc[h], 0.0).reshape(BLK // 8, 8, 4 * BLK), axis=0)
                return c

            lax.fori_loop(0, N_BUCKETS, step, 0)
            rows = lax.broadcasted_iota(jnp.int32, (N_BUCKETS, LANES), 0)
            lanes = _lane((N_BUCKETS, LANES))
            for h in range(A_HEADS):
                col = jnp.sum(jnp.sum(part[h], axis=1), axis=-1, keepdims=True)
                o_ref[h:h + 1, :] += jnp.sum(jnp.where(rows == lanes, col, 0.0), axis=0, keepdims=True)

    return pl.pallas_call(
        body, name=name, grid=(4, Bl),
        in_specs=[pl.BlockSpec((1, 1, A_HEADS, BLK, 4 * BLK), lambda v, b: (b, v, 0, 0, 0))] * nl
        + [pl.BlockSpec((1, BLK, 4 * BLK), lambda v, b: (v, 0, 0))],
        out_specs=pl.BlockSpec((8, LANES), lambda v, b: (0, 0)), out_shape=jax.ShapeDtypeStruct((8, LANES), F32),
        scratch_shapes=[pltpu.VMEM((A_HEADS, BLK, 4 * BLK), F32), pltpu.VMEM((A_HEADS, N_BUCKETS, 8, 4 * BLK), F32)],
        compiler_params=_params("arbitrary", "arbitrary"),
    )(*dbias_list, buckets)


def _mla_prep_bwd(dq, dk, dv, proj, gq, gkv, wq, wkk, wkv, tq, tk, tm, Lp, name):
    T = proj.shape[0]
    nt = Lp // tm

    def body(dq_ref, dk_ref, dv_ref, cq_ref, ckv_ref, gq_ref, gkv_ref, wq_ref, wkk_ref, wkv_ref, tq_ref, tk_ref,
             dcq_ref, dckv_ref, dkr_ref, dwq_ref, dwkk_ref, dwkv_ref, dgq_ref, dgkv_ref):
        @pl.when(pl.program_id(0) == 0)
        def _():
            for r in (dwq_ref, dwkk_ref, dwkv_ref, dgq_ref, dgkv_ref):
                r[...] = jnp.zeros_like(r)

        tqv = tq_ref[...]
        dqp = jnp.concatenate([_rope_t(dq_ref[:, LANES * h:LANES * (h + 1)], tqv) for h in range(B_HEADS)],
                              axis=1).astype(BF16)
        cq, gq_ = cq_ref[...], gq_ref[...]
        rq = _rstd(cq)
        dwq_ref[...] += _dot_tn((cq * rq * gq_).astype(BF16), dqp)
        dx, dg = _rms_bwd(_dot_nt(dqp, wq_ref[...]), cq, rq, gq_)
        dcq_ref[...] = dx
        dgq_ref[...] += dg

        dkv_ = dk_ref[...]
        dks = dkv_[:, 0:LANES]
        for h in range(1, B_HEADS):
            dks = dks + dkv_[:, LANES * h:LANES * (h + 1)]
        dkr_ref[...] = _rope_t(pltpu.roll(dks, NOPE, 1), tk_ref[...])
        dkb = dkv_.astype(BF16)
        dvb = dv_ref[...].astype(BF16)
        ckv, gkv_ = ckv_ref[...], gkv_ref[...]
        rk = _rstd(ckv)
        cb = (ckv * rk * gkv_).astype(BF16)
        dwkk_ref[...] += _dot_tn(cb, dkb)
        dwkv_ref[...] += _dot_tn(cb, dvb)
        dx, dg = _rms_bwd(_dot_nt(dkb, wkk_ref[...]) + _dot_nt(dvb, wkv_ref[...]), ckv, rk, gkv_)
        dckv_ref[...] = dx
        dgkv_ref[...] += dg

    tab = pl.BlockSpec((3, tm, LANES), lambda i: (0, i % nt, 0))
    wide = pl.BlockSpec((tm, 1024), _row(0))
    return pl.pallas_call(
        body, name=name, grid=(T // tm,),
        in_specs=[wide, wide, pl.BlockSpec((tm, B_W), _row(0)), pl.BlockSpec((tm, 256), _row(CQ0 // 256)),
                  pl.BlockSpec((tm, 128), _row(CKV0 // 128)), pl.BlockSpec((1, Q_RANK), _const2),
                  pl.BlockSpec((1, KV_RANK), _const2), pl.BlockSpec((Q_RANK, 1024), _const2),
                  pl.BlockSpec((KV_RANK, 1024), _const2), pl.BlockSpec((KV_RANK, B_W), _const2), tab, tab],
        out_specs=(pl.BlockSpec((tm, 256), _row(0)), pl.BlockSpec((tm, 128), _row(0)), pl.BlockSpec((tm, 128), _row(0)),
                   pl.BlockSpec((Q_RANK, 1024), _const2), pl.BlockSpec((KV_RANK, 1024), _const2),
                   pl.BlockSpec((KV_RANK, B_W), _const2), pl.BlockSpec((1, Q_RANK), _const2),
                   pl.BlockSpec((1, KV_RANK), _const2)),
        out_shape=(jax.ShapeDtypeStruct((T, 256), F32), jax.ShapeDtypeStruct((T, 128), F32),
                   jax.ShapeDtypeStruct((T, 128), F32), jax.ShapeDtypeStruct((Q_RANK, 1024), F32),
                   jax.ShapeDtypeStruct((KV_RANK, 1024), F32), jax.ShapeDtypeStruct((KV_RANK, B_W), F32),
                   jax.ShapeDtypeStruct((1, Q_RANK), F32), jax.ShapeDtypeStruct((1, KV_RANK), F32)),
        compiler_params=_params("arbitrary"),
    )(dq, dk, dv, proj, proj, gq, gkv, wq, wkk, wkv, tq, tk)


def _inproj_bwd(pieces, h, g, wx, dh_out, tm, name):
    T = h.shape[0]
    nsteps = T // tm
    CH = 512

    def body(*refs):
        p_refs = refs[:len(pieces)]
        h_ref, g_ref, w_ref, dho_ref, dh_ref, dw_hbm, dg_ref, acc, sem = refs[len(pieces):]
        i = pl.program_id(0)

        @pl.when(i == 0)
        def _():
            acc[...] = jnp.zeros_like(acc)
            dg_ref[...] = jnp.zeros_like(dg_ref)

        dp = jnp.concatenate([r[...].astype(BF16) for r in p_refs], axis=1)
        x, gain = h_ref[...], g_ref[...]
        r = _rstd(x)
        u = (x * r * gain).astype(BF16)
        for c in range(0, NX, CH):
            acc[:, c:c + CH] += _dot_tn(u, dp[:, c:c + CH])
        dx, dg = _rms_bwd(_dot_nt(dp, w_ref[...]), x, r, gain)
        dh_ref[...] = dho_ref[...] + dx
        dg_ref[...] += dg

        @pl.when(i == nsteps - 1)
        def _():
            cp = pltpu.make_async_copy(acc, dw_hbm, sem)
            cp.start()
            cp.wait()

    return pl.pallas_call(
        body, name=name, grid=(nsteps,),
        in_specs=[pl.BlockSpec((tm, p.shape[1]), _row(0)) for p in pieces]
        + [pl.BlockSpec((tm, D), _row(0)), pl.BlockSpec((1, D), _const2), pl.BlockSpec((D, NX), _const2),
           pl.BlockSpec((tm, D), _row(0))],
        out_specs=(pl.BlockSpec((tm, D), _row(0)), pl.BlockSpec(memory_space=pl.ANY), pl.BlockSpec((1, D), _const2)),
        out_shape=(jax.ShapeDtypeStruct((T, D), F32), jax.ShapeDtypeStruct((D, NX), F32), jax.ShapeDtypeStruct((1, D), F32)),
        scratch_shapes=[pltpu.VMEM((D, NX), F32), pltpu.SemaphoreType.DMA(())],
        compiler_params=_params("arbitrary"),
    )(*pieces, h, g, wx, dh_out)


def _flat(a):
    return a.reshape(-1, LANES)


def _pack_sharded(w_in, w_uq, w_ukv, w_out, meta):
    return jnp.concatenate([_flat(w_in), _flat(w_uq), _flat(w_ukv), _flat(w_out), _flat(meta)], axis=0)


def _unpack_sharded(p, like):
    out, r = [], 0
    for a in like:
        n = a.size // LANES
        out.append(p[r:r + n].reshape(a.shape))
        r += n
    return out


def _pack_small(parts):
    flat = jnp.concatenate([a.reshape(-1) for a in parts])
    return jnp.concatenate([flat, jnp.zeros((ROWS_SMALL * LANES - flat.shape[0],), F32)]).reshape(ROWS_SMALL, LANES)


def _unpack_small(p, like):
    flat, out, r = p.reshape(-1), [], 0
    for a in like:
        out.append(flat[r:r + a.size].reshape(a.shape))
        r += a.size
    return out


def kernel(x, meta_tokens, rel_bias_table, norm_in, w_in, sink_a, norm_q_lat, w_uq, norm_kv_lat, w_ukv, norm_out_a, norm_out_b, w_out, norm_final, loss_target, m_meta_tokens, m_rel_bias_table, m_norm_in, m_w_in, m_sink_a, m_norm_q_lat, m_w_uq, m_norm_kv_lat, m_w_ukv, m_norm_out_a, m_norm_out_b, m_w_out, m_norm_final, v_meta_tokens, v_rel_bias_table, v_norm_in, v_w_in, v_sink_a, v_norm_q_lat, v_w_uq, v_norm_kv_lat, v_w_ukv, v_norm_out_a, v_norm_out_b, v_w_out, v_norm_final):
    Bl, S, _ = x.shape
    Lp = S + BLK
    T = Bl * Lp
    tm = Lp // 4
    tq = Lp // 8
    depth = w_in.shape[0]

    me = 4 * lax.axis_index("x") + 2 * lax.axis_index("y") + lax.axis_index("c")
    wpacks = [jnp.concatenate([_flat(w_in[i]), _flat(w_uq[i]), _flat(w_ukv[i]), _flat(w_out[i])], axis=0).astype(BF16)
              for i in range(depth)]
    gathers, tokens = [], []
    for i in range(depth):
        hd, tok = _xchg_start([wpacks[i]] + ([meta_tokens] if i == 0 else []), ["gather"] * (2 if i == 0 else 1),
                              f"gather_start_{i}")
        gathers.append(hd)
        tokens.append(tok)

    def unpack_weights(i, gw):
        r0, r1, r2 = ROWS_W_IN // 2, (ROWS_W_IN + ROWS_W_UQ) // 2, (ROWS_W_IN + ROWS_W_UQ + ROWS_W_UKV) // 2
        cols = lambda t: jnp.transpose(t, (1, 0, 2)).reshape(t.shape[1], NDEV * t.shape[2])
        return (cols(gw[:, :r0].reshape(NDEV, D, IN_W // NDEV)), cols(gw[:, r0:r1].reshape(NDEV, Q_RANK, 768 // NDEV)),
                cols(gw[:, r1:r2].reshape(NDEV, KV_RANK, 1024 // NDEV)), gw[:, r2:].reshape(D, D))

    lands0 = _xchg_wait(gathers[0], tokens[-1], "gather_wait_0")
    full_w = {0: unpack_weights(0, _own_slot(lands0[0], wpacks[0], me))}
    meta_f = jnp.transpose(_own_slot(lands0[1], meta_tokens, me), (1, 0, 2)).reshape(NMETA, D)

    buckets = _bias_buckets(S)
    bias = _build_bias(buckets, rel_bias_table, "build_bias")
    tq_tab = _rope_tables(S, Lp, NOPE, True)
    tk_tab = _rope_tables(S, Lp, 0, False)
    kmask = jnp.where(jnp.arange(Lp) < S + NMETA, 0.0, NEG).astype(F32).reshape(1, Lp)

    tail = jnp.concatenate([meta_f, jnp.zeros((BLK - NMETA, D), F32)], axis=0)
    h = jnp.concatenate([x, jnp.broadcast_to(tail[None], (Bl, BLK, D))], axis=1).reshape(T, D)
    h = h + tokens[-1][0, 0]

    saved = []
    w_out_f = [None] * depth
    for i in range(depth):
        if i not in full_w:
            full_w[i] = unpack_weights(i, _own_slot(_xchg_wait(gathers[i], h, f"gather_wait_{i}")[0], wpacks[i], me))
        w_in_i, w_uq_i, w_ukv_i, w_out_f[i] = full_w[i]
        wx = _pack_w_in(w_in_i)
        wq = _pack_w_uq(w_uq_i)
        wkk, wkv = _pack_w_ukv(w_ukv_i)
        g_in, g_q, g_kv = norm_in[i][None], norm_q_lat[i][None], norm_kv_lat[i][None]
        g_a, g_b, sink = norm_out_a[i][None], norm_out_b[i][None], sink_a[i][None]
        proj = _inproj_fwd(h, g_in, wx, tm, f"inproj_fwd_{i}")
        q, k, v = _mla_prep_fwd(proj, g_q, g_kv, wq, wkk, wkv, tq_tab, tk_tab, tm, Lp, f"mla_prep_fwd_{i}")
        ya, lse_a = _win_fwd(proj, bias, sink, Bl, S, f"win_fwd_{i}")
        yb, lse_b = _mla_fwd(q, k, v, kmask, Bl, Lp, tq, f"mla_fwd_{i}")
        h_new = _out_fwd(ya, yb, proj, g_a, g_b, w_out_f[i], h, tm, f"out_fwd_{i}")
        saved.append((h, proj, q, k, v, ya, lse_a, yb, lse_b, wx, wq, wkk, wkv))
        h = h_new

    dh, loss_acc, dg_final = _loss_head(h, norm_final[None], loss_target, Bl, S, "loss_head")
    loss = lax.psum(loss_acc[0, 0], ("x", "y", "c"))

    small_w = [rel_bias_table, norm_in, sink_a, norm_q_lat, norm_kv_lat, norm_out_a, norm_out_b, norm_final]
    small_m = [m_rel_bias_table, m_norm_in, m_sink_a, m_norm_q_lat, m_norm_kv_lat, m_norm_out_a, m_norm_out_b, m_norm_final]
    small_v = [v_rel_bias_table, v_norm_in, v_sink_a, v_norm_q_lat, v_norm_kv_lat, v_norm_out_a, v_norm_out_b, v_norm_final]
    g_n_in, g_sink, g_nq, g_nkv, g_na, g_nb = ([None] * depth for _ in range(6))
    dbias_all, scatters, gpacks = [], [None] * depth, [None] * depth
    split = lambda t, n: jnp.transpose(t.reshape(t.shape[0], NDEV, n), (1, 0, 2)).reshape(NDEV, -1, LANES)
    tok = None
    for i in reversed(range(depth)):
        h_in, proj, q, k, v, ya, lse_a, yb, lse_b, wx, wq, wkk, wkv = saved[i]
        g_in, g_q, g_kv = norm_in[i][None], norm_q_lat[i][None], norm_kv_lat[i][None]
        g_a, g_b, sink = norm_out_a[i][None], norm_out_b[i][None], sink_a[i][None]
        if tok is not None:
            g_a = g_a + tok[0, 0]
        dya, dyb, dga, dgb, g_w_out, g_na[i], g_nb[i] = _out_bwd(dh, ya, yb, proj, g_a, g_b, w_out_f[i], tm, f"out_bwd_{i}")
        dq, dk, dv = _mla_bwd(q, k, v, kmask, yb, dyb, lse_b, Bl, Lp, tq, f"mla_bwd_{i}")
        dqa, dka, dva, dbias, dsink = _win_bwd(proj, bias, sink, ya, dya, lse_a, Bl, S, f"win_bwd_{i}")
        dcq, dckv, dkr, dwq, dwkk, dwkv, g_nq[i], g_nkv[i] = _mla_prep_bwd(
            dq, dk, dv, proj, g_q, g_kv, wq, wkk, wkv, tq_tab, tk_tab, tm, Lp, f"mla_prep_bwd_{i}")
        dh, dwx, g_n_in[i] = _inproj_bwd((dqa, dka, dva, dga, dcq, dckv, dkr, dgb), h_in, g_in, wx, dh, tq,
                                         f"inproj_bwd_{i}")
        g_sink[i] = jnp.sum(dsink[:, :, 0], axis=0)
        dbias_all.append(dbias)
        parts = [split(_unpack_w_in_grad(dwx), IN_W // NDEV), split(_unpack_w_uq_grad(dwq), 768 // NDEV),
                 split(_unpack_w_ukv_grad(dwkk, dwkv), 1024 // NDEV), g_w_out.reshape(NDEV, -1, LANES)]
        if i == 0:
            dh3 = dh.reshape(Bl, Lp, D)
            grad_x = dh3[:, :S]
            g_meta = jnp.sum(dh3[:, S:S + NMETA], axis=0)
            parts.append(jnp.transpose(g_meta.reshape(NMETA, NDEV, D // NDEV), (1, 0, 2)))
        gpacks[i] = jnp.concatenate(parts, axis=1)
        if i > 0:
            scatters[i], tok = _xchg_start([gpacks[i]], ["scatter"], f"scatter_start_{i}")

    dtab = _table_grad(dbias_all, buckets, "table_grad")
    small_g = [jnp.transpose(dtab[:, :N_BUCKETS]), jnp.concatenate(g_n_in), jnp.stack(g_sink), jnp.concatenate(g_nq),
               jnp.concatenate(g_nkv), jnp.concatenate(g_na), jnp.concatenate(g_nb), dg_final[0]]
    spack = _pack_small(small_g)
    scatters[0], tok = _xchg_start([gpacks[0], spack], ["scatter", "gather"], "scatter_start_0")

    layer_outs = [None] * depth
    for i in reversed(range(depth)):
        lands = _xchg_wait(scatters[i], tok, f"scatter_wait_{i}")
        recv = _own_slot(lands[0], lax.dynamic_index_in_dim(gpacks[i], me, 0, keepdims=False), me)
        ws = [[t[i] for t in grp] + ([mt] if i == 0 else [])
              for grp, mt in (((w_in, w_uq, w_ukv, w_out), meta_tokens), ((m_w_in, m_w_uq, m_w_ukv, m_w_out), m_meta_tokens),
                              ((v_w_in, v_w_uq, v_w_ukv, v_w_out), v_meta_tokens))]
        packs = [jnp.concatenate([_flat(a) for a in grp], axis=0) for grp in ws]
        outs = _sum_adamw(recv, *packs, f"adamw_layer_{i}")
        layer_outs[i] = [_unpack_sharded(o, ws[0]) for o in outs]
        if i == 0:
            srecv = _own_slot(lands[1], spack, me)
    outs_small = _sum_adamw(srecv, _pack_small(small_w), _pack_small(small_m), _pack_small(small_v), "adamw_replicated")

    res = {}
    for ki, kind in enumerate(("grad", "delta", "new_m", "new_v")):
        s = _unpack_small(outs_small[ki], small_w)
        big = [jnp.stack([layer_outs[i][ki][j] for i in range(depth)]) for j in range(4)]
        res[kind] = dict(w_in=big[0], w_uq=big[1], w_ukv=big[2], w_out=big[3], meta_tokens=layer_outs[0][ki][4],
                         rel_bias_table=s[0], norm_in=s[1], sink_a=s[2], norm_q_lat=s[3], norm_kv_lat=s[4],
                         norm_out_a=s[5], norm_out_b=s[6], norm_final=s[7])
    order = ["meta_tokens", "rel_bias_table", "norm_in", "w_in", "sink_a", "norm_q_lat", "w_uq", "norm_kv_lat", "w_ukv",
             "norm_out_a", "norm_out_b", "w_out", "norm_final"]
    return (loss, grad_x, *[res[kind][n] for kind in ("grad", "delta", "new_m", "new_v") for n in order])
```

```python
import functools
import math

import numpy as np
import jax
import jax.numpy as jnp
from jax import lax
from jax.experimental import pallas as pl
from jax.experimental.pallas import tpu as pltpu

F32, BF16 = jnp.float32, jnp.bfloat16
D = 1024
NMETA = 16
BLK = 128
A_HEADS, A_KV, A_DH, A_W = 8, 2, 64, 512
B_HEADS, NOPE, ROPE, B_V, B_W = 8, 64, 32, 64, 512
Q_RANK, KV_RANK = 256, 128
IN_W = 2208
N_BUCKETS, MAX_DIST = 32, 128
THETA = 10000.0
EPS = 1e-6
NEG = -1e30
A_SCALE = A_DH ** -0.5
B_SCALE = (NOPE + ROPE) ** -0.5
LANES = 128
NDEV = 8
MESH = pl.DeviceIdType.MESH

NX = 2560
QA0, KA0, VA0, GA0, CQ0, CKV0, KR0, GB0 = 0, 512, 768, 1024, 1536, 1792, 1920, 2048

ADAM_LR, ADAM_B1, ADAM_B2, ADAM_EPS, ADAM_WD, ADAM_STEP = 0.001, 0.9, 0.999, 1e-08, 0.01, 10


def _dot(a, b):
    return jnp.dot(a, b, preferred_element_type=F32)


def _dot_nt(a, b):
    return lax.dot_general(a, b, (((1,), (1,)), ((), ())), preferred_element_type=F32)


def _dot_tn(a, b):
    return lax.dot_general(a, b, (((0,), (0,)), ((), ())), preferred_element_type=F32)


def _lane(shape):
    return lax.broadcasted_iota(jnp.int32, shape, len(shape) - 1)


def _rstd(x):
    return lax.rsqrt(jnp.mean(x * x, axis=-1, keepdims=True) + EPS)


def _rms_bwd(dn, x, r, g):
    z = dn * g
    dx = r * z - x * (r * r * r) * jnp.mean(z * x, axis=-1, keepdims=True)
    return dx, jnp.sum(dn * (x * r), axis=0, keepdims=True)


def _row(i):
    return lambda *ids: (ids[0], i)


def _const2(*ids):
    return (0, 0)


def _params(*sem):
    return pltpu.CompilerParams(dimension_semantics=sem)


def _pack_w_in(w):
    qa, ka, va, ga = w[:, 0:512], w[:, 512:640], w[:, 640:768], w[:, 768:1280]
    cq, ckv, kr, gb = w[:, 1280:1536], w[:, 1536:1664], w[:, 1664:1696], w[:, 1696:2208]
    dup = lambda t: jnp.concatenate([t[:, 0:64], t[:, 0:64], t[:, 64:128], t[:, 64:128]], axis=1)
    krp = jnp.concatenate([kr, jnp.zeros((w.shape[0], LANES - ROPE), w.dtype)], axis=1)
    return jnp.concatenate([qa, dup(ka), dup(va), ga, cq, ckv, krp, gb], axis=1)


def _unpack_w_in_grad(g):
    fold = lambda t: jnp.concatenate([t[:, 0:64] + t[:, 64:128], t[:, 128:192] + t[:, 192:256]], axis=1)
    return jnp.concatenate([g[:, QA0:QA0 + 512], fold(g[:, KA0:KA0 + 256]), fold(g[:, VA0:VA0 + 256]),
                            g[:, GA0:GA0 + 512], g[:, CQ0:CQ0 + 256], g[:, CKV0:CKV0 + 128],
                            g[:, KR0:KR0 + ROPE], g[:, GB0:GB0 + 512]], axis=1)


def _pack_w_uq(w):
    t = w.reshape(Q_RANK, B_HEADS, NOPE + ROPE)
    t = jnp.concatenate([t, jnp.zeros((Q_RANK, B_HEADS, LANES - NOPE - ROPE), w.dtype)], axis=-1)
    return t.reshape(Q_RANK, B_HEADS * LANES)


def _unpack_w_uq_grad(g):
    return g.reshape(Q_RANK, B_HEADS, LANES)[:, :, :NOPE + ROPE].reshape(Q_RANK, B_HEADS * (NOPE + ROPE))


def _pack_w_ukv(w):
    t = w.reshape(KV_RANK, B_HEADS, NOPE + B_V)
    kk = jnp.concatenate([t[:, :, :NOPE], jnp.zeros((KV_RANK, B_HEADS, LANES - NOPE), w.dtype)], axis=-1)
    return kk.reshape(KV_RANK, B_HEADS * LANES), t[:, :, NOPE:].reshape(KV_RANK, B_W)


def _unpack_w_ukv_grad(gk, gv):
    t = jnp.concatenate([gk.reshape(KV_RANK, B_HEADS, LANES)[:, :, :NOPE], gv.reshape(KV_RANK, B_HEADS, B_V)], axis=-1)
    return t.reshape(KV_RANK, B_HEADS * (NOPE + B_V))


def _t5_bucket(rel):
    nb = N_BUCKETS // 2
    max_exact = nb // 2
    ret = jnp.where(rel > 0, nb, 0)
    n = jnp.abs(rel)
    nf = jnp.maximum(n, 1).astype(F32)
    large = max_exact + (jnp.log(nf / max_exact) / math.log(MAX_DIST / max_exact) * (nb - max_exact)).astype(jnp.int32)
    large = jnp.minimum(large, nb - 1)
    return ret + jnp.where(n < max_exact, n, large)


def _bias_buckets(S):
    q = np.arange(BLK)[:, None]
    k = np.arange(4 * BLK)[None, :]
    is_meta_key = (k >= 3 * BLK) & (k < 3 * BLK + NMETA)
    mi = k - 3 * BLK
    rels, valids = [], []
    for shift in (0, BLK, 2 * BLK):
        rel_real = k - shift - q
        valid_real = (k < 3 * BLK) & (np.abs(rel_real) <= BLK)
        far = -(NMETA + MAX_DIST + BLK)
        rel_meta = (mi - (NMETA + q)) if shift == 0 else np.full_like(k + q, far)
        rels.append(np.where(is_meta_key, rel_meta, rel_real))
        valids.append(valid_real | is_meta_key)
    qm = q < NMETA
    rel_real = NMETA + k - q
    valid_real = (k < BLK) & (np.abs(rel_real) <= BLK) & qm
    rels.append(np.where(is_meta_key, mi - q, rel_real))
    valids.append(valid_real | is_meta_key)
    rel = jnp.asarray(np.stack(rels).astype(np.int32))
    valid = jnp.asarray(np.stack(valids))
    return jnp.where(valid, _t5_bucket(rel), -1).astype(jnp.int32)


def _rope_tables(S, Lp, off, passthrough):
    half = ROPE // 2
    r = np.arange(Lp)
    pos = np.where(r < S, NMETA + r, np.where(r < S + NMETA, r - S, 0)).astype(np.float32)
    freqs = THETA ** (-jnp.arange(half, dtype=F32) / half)
    ang = jnp.asarray(pos)[:, None] * freqs[None, :]
    cos, sin = jnp.cos(ang), jnp.sin(ang)
    z = lambda n: jnp.zeros((Lp, n), F32)
    head = jnp.ones((Lp, off), F32) if passthrough else z(off)
    c = jnp.concatenate([head, cos, cos, z(LANES - off - ROPE)], axis=1)
    s1 = jnp.concatenate([z(off), -sin, z(LANES - off - half)], axis=1)
    s2 = jnp.concatenate([z(off + half), sin, z(LANES - off - ROPE)], axis=1)
    return jnp.stack([c, s1, s2])


def _rope(x, t):
    return x * t[0] + pltpu.roll(x, LANES - 16, 1) * t[1] + pltpu.roll(x, 16, 1) * t[2]


def _rope_t(dy, t):
    return dy * t[0] + pltpu.roll(dy * t[1], 16, 1) + pltpu.roll(dy * t[2], LANES - 16, 1)


def _me_and_peers():
    x, y, c = lax.axis_index("x"), lax.axis_index("y"), lax.axis_index("c")
    flip = lambda v, b: 1 - v if b else v
    peers = [(flip(x, k & 4), flip(y, k & 2), flip(c, k & 1)) for k in range(1, NDEV)]
    return 4 * x + 2 * y + c, peers


def _xchg_start(srcs, kinds, name):
    n = len(srcs)
    lands = [lax.empty((NDEV,) + s.shape[-2:], s.dtype) for s in srcs]

    def body(*refs):
        src_refs, land_refs = refs[:n], refs[n:2 * n]
        ssems, rsems = refs[2 * n:3 * n], refs[3 * n:4 * n]
        token = refs[6 * n]
        me, peers = _me_and_peers()
        for it in range(n):
            for k, (px, py, pc) in enumerate(peers):
                src = src_refs[it] if kinds[it] == "gather" else src_refs[it].at[4 * px + 2 * py + pc]
                pltpu.make_async_remote_copy(src_ref=src, dst_ref=land_refs[it].at[me], send_sem=ssems[it].at[k],
                                             recv_sem=rsems[it].at[k], device_id=(px, py, pc), device_id_type=MESH).start()
        token[...] = jnp.zeros_like(token)

    hbm = pl.BlockSpec(memory_space=pltpu.HBM)
    sem = pl.BlockSpec(memory_space=pltpu.SEMAPHORE)
    outs = pl.pallas_call(
        body, name=name,
        out_shape=tuple([pltpu.SemaphoreType.DMA((NDEV - 1,))] * (2 * n) + [pltpu.HBM(a.shape, a.dtype) for a in srcs + lands]
                        + [jax.ShapeDtypeStruct((8, LANES), F32)]),
        in_specs=[hbm] * (2 * n), out_specs=tuple([sem] * (2 * n) + [hbm] * (2 * n) + [pl.BlockSpec(memory_space=pltpu.VMEM)]),
        input_output_aliases={i: 2 * n + i for i in range(2 * n)},
        compiler_params=pltpu.CompilerParams(has_side_effects=pltpu.SideEffectType.DATAFLOW_SIDE_EFFECTING),
    )(*[pltpu.with_memory_space_constraint(a, pltpu.HBM) for a in srcs + lands])
    handles = [(outs[it], outs[n + it], outs[2 * n + it], outs[3 * n + it], kinds[it]) for it in range(n)]
    return handles, outs[4 * n]


def _xchg_wait(handles, after, name):
    n = len(handles)

    def body(*refs):
        src_refs, land_refs = refs[:n], refs[n:2 * n]
        ssems, rsems = refs[2 * n:3 * n], refs[3 * n:4 * n]
        me, peers = _me_and_peers()
        for it in range(n):
            for k, (px, py, pc) in enumerate(peers):
                src = src_refs[it] if handles[it][4] == "gather" else src_refs[it].at[4 * px + 2 * py + pc]
                cp = pltpu.make_async_remote_copy(src_ref=src, dst_ref=land_refs[it].at[me], send_sem=ssems[it].at[k],
                                                  recv_sem=rsems[it].at[k], device_id=(px, py, pc), device_id_type=MESH)
                cp.wait_send()
                cp.wait_recv()

    hbm = pl.BlockSpec(memory_space=pltpu.HBM)
    sem = pl.BlockSpec(memory_space=pltpu.SEMAPHORE)
    srcs, lands = [h[2] for h in handles], [h[3] for h in handles]
    outs = pl.pallas_call(
        body, name=name, out_shape=tuple(pltpu.HBM(a.shape, a.dtype) for a in srcs + lands),
        in_specs=[hbm] * (2 * n) + [sem] * (2 * n) + [pl.BlockSpec(memory_space=pl.ANY)], out_specs=tuple([hbm] * (2 * n)),
        input_output_aliases={i: i for i in range(2 * n)},
        compiler_params=pltpu.CompilerParams(has_side_effects=pltpu.SideEffectType.DATAFLOW_SIDE_EFFECTING),
    )(*srcs, *lands, *[h[0] for h in handles], *[h[1] for h in handles], after)
    return list(outs[:n]), list(outs[n:])


def _own_slot(land, own, me):
    return lax.dynamic_update_slice(land, own[None].astype(land.dtype), (me, 0, 0))


def _after(x, dep):
    return lax.optimization_barrier((x, dep))[0]


def _adamw_update(g, w, m, v):
    mn = ADAM_B1 * m + (1.0 - ADAM_B1) * g
    vn = ADAM_B2 * v + (1.0 - ADAM_B2) * (g * g)
    m_hat = mn / (1.0 - ADAM_B1 ** ADAM_STEP)
    v_hat = vn / (1.0 - ADAM_B2 ** ADAM_STEP)
    return -ADAM_LR * (m_hat / (jnp.sqrt(v_hat) + ADAM_EPS) + ADAM_WD * w), mn, vn


def _sum_partials(r_ref):
    g = r_ref[0].astype(F32)
    for p in range(1, NDEV):
        g = g + r_ref[p].astype(F32)
    return g


def _adamw_param(recvs, w, m, v, name):
    depth, r, c = w.shape
    tr = min(r, 256)
    nl = len(recvs)

    def body(*refs):
        r_refs = refs[:nl]
        w_ref, m_ref, v_ref, g_out, d_out, m_out, v_out = refs[nl:]
        for li in range(nl):
            @pl.when(pl.program_id(0) == li)
            def _(li=li):
                g = _sum_partials(r_refs[li])
                g_out[0] = g
                d_out[0], m_out[0], v_out[0] = _adamw_update(g, w_ref[0], m_ref[0], v_ref[0])

    blk = pl.BlockSpec((1, tr, c), lambda l, i: (l, i, 0))
    return pl.pallas_call(
        body, name=name, grid=(depth, r // tr),
        in_specs=[pl.BlockSpec((NDEV, tr, c), lambda l, i: (0, i, 0))] * nl + [blk, blk, blk],
        out_specs=(blk, blk, blk, blk), out_shape=(jax.ShapeDtypeStruct(w.shape, F32),) * 4,
        compiler_params=_params("arbitrary", "arbitrary"),
    )(*recvs, w, m, v)


def _adamw_small(recvs, ws, ms, vs, name):
    n = len(ws)

    def body(*refs):
        r_refs, w_refs, m_refs, v_refs = refs[:n], refs[n:2 * n], refs[2 * n:3 * n], refs[3 * n:4 * n]
        outs = refs[4 * n:]
        for j in range(n):
            g = _sum_partials(r_refs[j])
            outs[j][...] = g
            outs[n + j][...], outs[2 * n + j][...], outs[3 * n + j][...] = _adamw_update(
                g, w_refs[j][...], m_refs[j][...], v_refs[j][...])

    vm = pl.BlockSpec(memory_space=pltpu.VMEM)
    return pl.pallas_call(
        body, name=name, in_specs=[vm] * (4 * n), out_specs=tuple([vm] * (4 * n)),
        out_shape=tuple(jax.ShapeDtypeStruct(a.shape, F32) for a in ws) * 4,
    )(*recvs, *ws, *ms, *vs)


def _inproj_fwd(h, g, wx, tm, name):
    T = h.shape[0]

    def body(h_ref, g_ref, w_ref, o_ref):
        x = h_ref[...]
        u = (x * _rstd(x) * g_ref[...]).astype(BF16)
        o_ref[...] = _dot(u, w_ref[...])

    return pl.pallas_call(
        body, name=name, grid=(T // tm,),
        in_specs=[pl.BlockSpec((tm, D), _row(0)), pl.BlockSpec((1, D), _const2), pl.BlockSpec((D, NX), _const2)],
        out_specs=pl.BlockSpec((tm, NX), _row(0)), out_shape=jax.ShapeDtypeStruct((T, NX), F32),
        compiler_params=_params("parallel"),
    )(h, g, wx)


def _mla_prep_fwd(proj, gq, gkv, wq, wkk, wkv, tq, tk, tm, Lp, name):
    T = proj.shape[0]
    nt = Lp // tm

    def body(cq_ref, ckv_ref, kr_ref, gq_ref, gkv_ref, wq_ref, wkk_ref, wkv_ref, tq_ref, tk_ref, q_ref, k_ref, v_ref):
        cq = cq_ref[...]
        q = _dot((cq * _rstd(cq) * gq_ref[...]).astype(BF16), wq_ref[...])
        tqv = tq_ref[...]
        for h in range(B_HEADS):
            cs = slice(LANES * h, LANES * (h + 1))
            q_ref[:, cs] = _rope(q[:, cs], tqv).astype(BF16)
        ksh = pltpu.roll(_rope(kr_ref[...], tk_ref[...]), NOPE, 1)
        ckv = ckv_ref[...]
        cb = (ckv * _rstd(ckv) * gkv_ref[...]).astype(BF16)
        kn = _dot(cb, wkk_ref[...])
        for h in range(B_HEADS):
            cs = slice(LANES * h, LANES * (h + 1))
            k_ref[:, cs] = (kn[:, cs] + ksh).astype(BF16)
        v_ref[...] = _dot(cb, wkv_ref[...]).astype(BF16)

    tab = pl.BlockSpec((3, tm, LANES), lambda i: (0, i % nt, 0))
    return pl.pallas_call(
        body, name=name, grid=(T // tm,),
        in_specs=[pl.BlockSpec((tm, 256), _row(CQ0 // 256)), pl.BlockSpec((tm, 128), _row(CKV0 // 128)),
                  pl.BlockSpec((tm, 128), _row(KR0 // 128)), pl.BlockSpec((1, Q_RANK), _const2),
                  pl.BlockSpec((1, KV_RANK), _const2), pl.BlockSpec((Q_RANK, 1024), _const2),
                  pl.BlockSpec((KV_RANK, 1024), _const2), pl.BlockSpec((KV_RANK, B_W), _const2), tab, tab],
        out_specs=(pl.BlockSpec((tm, 1024), _row(0)), pl.BlockSpec((tm, 1024), _row(0)), pl.BlockSpec((tm, B_W), _row(0))),
        out_shape=(jax.ShapeDtypeStruct((T, 1024), BF16), jax.ShapeDtypeStruct((T, 1024), BF16),
                   jax.ShapeDtypeStruct((T, B_W), BF16)),
        compiler_params=_params("parallel"),
    )(proj, proj, proj, gq, gkv, wq, wkk, wkv, tq, tk)


def _build_bias(buckets, table, name):
    def body(tab_ref, b_ref, o_ref):
        bidx = b_ref[0]
        for h in range(A_HEADS):
            def step(j, acc):
                return jnp.where(bidx == j, tab_ref[j, h], acc)
            o_ref[0, h] = lax.fori_loop(0, N_BUCKETS, step, jnp.full(bidx.shape, NEG, F32))

    return pl.pallas_call(
        body, name=name, grid=(4,),
        in_specs=[pl.BlockSpec(memory_space=pltpu.SMEM), pl.BlockSpec((1, BLK, 4 * BLK), lambda i: (i, 0, 0))],
        out_specs=pl.BlockSpec((1, A_HEADS, BLK, 4 * BLK), lambda i: (i, 0, 0, 0)),
        out_shape=jax.ShapeDtypeStruct((4, A_HEADS, BLK, 4 * BLK), F32),
        compiler_params=_params("parallel"),
    )(table, buckets)


def _win_base(n, NB):
    return jnp.where(n == NB, 0, jnp.clip(n - 1, 0, NB - 3))


def _win_variant(n, NB):
    return jnp.where(n == 0, 0, jnp.where(n < NB - 1, 1, jnp.where(n == NB - 1, 2, 3)))


def _win_fwd(proj, bias, sink, Bl, S, name):
    T = proj.shape[0]
    Lp = S + BLK
    NB = S // BLK
    nblk = Lp // BLK

    def body(sink_ref, q_ref, k_ref, v_ref, b_ref, o_ref, lse_ref):
        n = pl.program_id(1)
        base = pl.multiple_of(_win_base(n, NB) * BLK, BLK)
        lane = _lane((BLK, LANES))
        lo = lane < 64
        outs, lses = [None] * A_HEADS, [None] * A_HEADS
        for kvh in range(A_KV):
            cs = slice(LANES * kvh, LANES * (kvh + 1))
            k2 = jnp.concatenate([k_ref[pl.ds(base, 3 * BLK), cs], k_ref[S:S + BLK, cs]], axis=0).astype(BF16)
            v2 = jnp.concatenate([v_ref[pl.ds(base, 3 * BLK), cs], v_ref[S:S + BLK, cs]], axis=0).astype(BF16)
            for gi in range(A_HEADS // A_KV):
                h = kvh * (A_HEADS // A_KV) + gi
                qp = q_ref[:, LANES * (h // 2):LANES * (h // 2 + 1)]
                qm = jnp.where(lo if h % 2 == 0 else ~lo, qp, 0.0).astype(BF16)
                s = _dot_nt(qm, k2) * A_SCALE + b_ref[0, h]
                sk = sink_ref[0, h]
                m = jnp.maximum(jnp.max(s, axis=-1, keepdims=True), sk)
                e = jnp.exp(s - m)
                den = jnp.sum(e, axis=-1, keepdims=True) + jnp.exp(sk - m)
                p = e * (1.0 / den)
                outs[h] = _dot(p.astype(BF16), v2)
                lses[h] = m + jnp.log(den)
        for j in range(A_HEADS // 2):
            o_ref[:, LANES * j:LANES * (j + 1)] = jnp.where(lo, outs[2 * j], outs[2 * j + 1])
        t = jnp.zeros((BLK, LANES), F32)
        for h in range(A_HEADS):
            t = jnp.where(lane == h, lses[h], t)
        lse_ref[...] = t

    qrow = lambda b, n: (b * nblk + n, 0)
    return pl.pallas_call(
        body, name=name, grid=(Bl, nblk),
        in_specs=[pl.BlockSpec(memory_space=pltpu.SMEM), pl.BlockSpec((BLK, A_W), qrow),
                  pl.BlockSpec((Lp, 256), lambda b, n: (b, KA0 // 256)), pl.BlockSpec((Lp, 256), lambda b, n: (b, VA0 // 256)),
                  pl.BlockSpec((1, A_HEADS, BLK, 4 * BLK), lambda b, n: (_win_variant(n, NB), 0, 0, 0))],
        out_specs=(pl.BlockSpec((BLK, A_W), qrow), pl.BlockSpec((BLK, LANES), qrow)),
        out_shape=(jax.ShapeDtypeStruct((T, A_W), F32), jax.ShapeDtypeStruct((T, LANES), F32)),
        compiler_params=_params("parallel", "arbitrary"),
    )(sink, proj, proj, proj, bias)


def _mla_fwd(q, k, v, kmask, Bl, Lp, tq, name):
    T = q.shape[0]
    nq = Lp // tq

    def body(q_ref, k_ref, v_ref, km_ref, o_ref, lse_ref):
        lane = _lane((tq, LANES))
        vb = v_ref[...]
        km = km_ref[...]
        outs = []
        t = jnp.zeros((tq, LANES), F32)
        for hh in range(2):
            cs = slice(LANES * hh, LANES * (hh + 1))
            s = _dot_nt(q_ref[:, cs], k_ref[:, cs]) * B_SCALE + km
            m = jnp.max(s, axis=-1, keepdims=True)
            e = jnp.exp(s - m)
            den = jnp.sum(e, axis=-1, keepdims=True)
            outs.append(_dot(e.astype(BF16), vb) * (1.0 / den))
            t = jnp.where(lane == hh, m + jnp.log(den), t)
        o_ref[...] = jnp.where(lane < 64, outs[0], outs[1])
        lse_ref[...] = t

    qrow = lambda b, j, i: (b * nq + i, j)
    return pl.pallas_call(
        body, name=name, grid=(Bl, B_HEADS // 2, nq),
        in_specs=[pl.BlockSpec((tq, 256), qrow), pl.BlockSpec((Lp, 256), lambda b, j, i: (b, j)),
                  pl.BlockSpec((Lp, LANES), lambda b, j, i: (b, j)), pl.BlockSpec((1, Lp), lambda b, j, i: (0, 0))],
        out_specs=(pl.BlockSpec((tq, LANES), qrow), pl.BlockSpec((tq, LANES), qrow)),
        out_shape=(jax.ShapeDtypeStruct((T, B_W), F32), jax.ShapeDtypeStruct((T, B_W), F32)),
        compiler_params=_params("parallel", "parallel", "arbitrary"),
    )(q, k, v, kmask)


def _gated(y, gate, gain):
    r = _rstd(y)
    nrm = y * r
    sg = jax.nn.sigmoid(gate)
    return r, nrm, sg, nrm * gain, gate * sg


def _out_fwd(ya, yb, proj, na, nb, wout, h, tm, name):
    T = h.shape[0]

    def body(ya_ref, yb_ref, ga_ref, gb_ref, na_ref, nb_ref, w_ref, h_ref, o_ref):
        _, _, _, n_a, sl_a = _gated(ya_ref[...], ga_ref[...], na_ref[...])
        _, _, _, n_b, sl_b = _gated(yb_ref[...], gb_ref[...], nb_ref[...])
        acc = _dot((n_a * sl_a).astype(BF16), w_ref[0:A_W, :]) + _dot((n_b * sl_b).astype(BF16), w_ref[A_W:A_W + B_W, :])
        o_ref[...] = h_ref[...] + acc

    half = pl.BlockSpec((tm, 512), _row(0))
    return pl.pallas_call(
        body, name=name, grid=(T // tm,),
        in_specs=[half, half, pl.BlockSpec((tm, 512), _row(GA0 // 512)), pl.BlockSpec((tm, 512), _row(GB0 // 512)),
                  pl.BlockSpec((1, 512), _const2), pl.BlockSpec((1, 512), _const2), pl.BlockSpec((D, D), _const2),
                  pl.BlockSpec((tm, D), _row(0))],
        out_specs=pl.BlockSpec((tm, D), _row(0)), out_shape=jax.ShapeDtypeStruct((T, D), F32),
        compiler_params=_params("parallel"),
    )(ya, yb, proj, proj, na, nb, wout, h)


def _loss_head(h, gf, target, Bl, S, name):
    T = h.shape[0]
    nblk = (S + BLK) // BLK
    NB = S // BLK

    def body(h_ref, g_ref, t_ref, dh_ref, loss_ref, dg_ref):
        b, n = pl.program_id(0), pl.program_id(1)

        @pl.when((b == 0) & (n == 0))
        def _():
            loss_ref[...] = jnp.zeros_like(loss_ref)
            dg_ref[...] = jnp.zeros_like(dg_ref)

        @pl.when(n < NB)
        def _():
            x = h_ref[...]
            g = g_ref[...]
            r = _rstd(x)
            err = x * r * g - t_ref[0]
            loss_ref[...] += 0.5 * jnp.sum(jnp.mean(err * err, axis=-1, keepdims=True))
            dx, dg = _rms_bwd(err * (1.0 / D), x, r, g)
            dh_ref[...] = dx
            dg_ref[...] += dg

        @pl.when(n >= NB)
        def _():
            dh_ref[...] = jnp.zeros_like(dh_ref)

    return pl.pallas_call(
        body, name=name, grid=(Bl, nblk),
        in_specs=[pl.BlockSpec((BLK, D), lambda b, n: (b * nblk + n, 0)), pl.BlockSpec((1, D), lambda b, n: (0, 0)),
                  pl.BlockSpec((1, BLK, D), lambda b, n: (b, jnp.minimum(n, NB - 1), 0))],
        out_specs=(pl.BlockSpec((BLK, D), lambda b, n: (b * nblk + n, 0)), pl.BlockSpec((8, LANES), lambda b, n: (0, 0)),
                   pl.BlockSpec((1, D), lambda b, n: (0, 0))),
        out_shape=(jax.ShapeDtypeStruct((T, D), F32), jax.ShapeDtypeStruct((8, LANES), F32), jax.ShapeDtypeStruct((1, D), F32)),
        compiler_params=_params("arbitrary", "arbitrary"),
    )(h, gf, target)


def _out_bwd(dh, ya, yb, proj, na, nb, wout, tm, name):
    T = dh.shape[0]

    def body(dh_ref, ya_ref, yb_ref, ga_ref, gb_ref, na_ref, nb_ref, w_ref,
             dya_ref, dyb_ref, dga_ref, dgb_ref, dw_ref, dna_ref, dnb_ref):
        @pl.when(pl.program_id(0) == 0)
        def _():
            dw_ref[...] = jnp.zeros_like(dw_ref)
            dna_ref[...] = jnp.zeros_like(dna_ref)
            dnb_ref[...] = jnp.zeros_like(dnb_ref)

        dhb = dh_ref[...].astype(BF16)
        dy = _dot_nt(dhb, w_ref[...])
        parts = ((ya_ref, ga_ref, na_ref, dya_ref, dga_ref, dna_ref, 0), (yb_ref, gb_ref, nb_ref, dyb_ref, dgb_ref, dnb_ref, A_W))
        for y_ref, gate_ref, gain_ref, dy_out, dgate_out, dgain_out, c0 in parts:
            y, gate, gain = y_ref[...], gate_ref[...], gain_ref[...]
            r, nrm, sg, n_g, sl = _gated(y, gate, gain)
            dyp = dy[:, c0:c0 + 512]
            dgate_out[...] = dyp * n_g * (sg * (1.0 + gate * (1.0 - sg)))
            dx, dgain = _rms_bwd(dyp * sl, y, r, gain)
            dy_out[...] = dx
            dgain_out[...] += dgain
            dw_ref[c0:c0 + 512, :] += _dot_tn((n_g * sl).astype(BF16), dhb)

    half = pl.BlockSpec((tm, 512), _row(0))
    vec = pl.BlockSpec((1, 512), _const2)
    return pl.pallas_call(
        body, name=name, grid=(T // tm,),
        in_specs=[pl.BlockSpec((tm, D), _row(0)), half, half, pl.BlockSpec((tm, 512), _row(GA0 // 512)),
                  pl.BlockSpec((tm, 512), _row(GB0 // 512)), vec, vec, pl.BlockSpec((D, D), _const2)],
        out_specs=(half, half, half, half, pl.BlockSpec((D, D), _const2), vec, vec),
        out_shape=(jax.ShapeDtypeStruct((T, 512), F32),) * 4 + (jax.ShapeDtypeStruct((D, D), F32),)
        + (jax.ShapeDtypeStruct((1, 512), F32),) * 2,
        compiler_params=_params("arbitrary"),
    )(dh, ya, yb, proj, proj, na, nb, wout)


def _mla_bwd(q, k, v, kmask, o, do, lse, Bl, Lp, tq, name):
    T = q.shape[0]
    nq = Lp // tq

    def body(q_ref, k_ref, v_ref, km_ref, o_ref, do_ref, lse_ref, dq_ref, dk_ref, dv_ref):
        @pl.when(pl.program_id(2) == 0)
        def _():
            dk_ref[...] = jnp.zeros_like(dk_ref)
            dv_ref[...] = jnp.zeros_like(dv_ref)

        lane = _lane((tq, LANES))
        lo = lane < 64
        vb = v_ref[...]
        km = km_ref[...]
        lsev = lse_ref[...]
        dop = do_ref[...]
        op = o_ref[...]
        for hh in range(2):
            cs = slice(LANES * hh, LANES * (hh + 1))
            qh, kh = q_ref[:, cs], k_ref[:, cs]
            s = _dot_nt(qh, kh) * B_SCALE + km
            lse_h = jnp.sum(jnp.where(lane == hh, lsev, 0.0), axis=-1, keepdims=True)
            p = jnp.exp(s - lse_h)
            dom = jnp.where(lo if hh == 0 else ~lo, dop, 0.0)
            delta = jnp.sum(dom * op, axis=-1, keepdims=True)
            domb = dom.astype(BF16)
            dp = _dot_nt(domb, vb)
            ds = (p * (dp - delta) * B_SCALE).astype(BF16)
            dq_ref[:, cs] = _dot(ds, kh)
            dk_ref[:, cs] += _dot_tn(ds, qh)
            dv_ref[...] += _dot_tn(p.astype(BF16), domb)

    qrow = lambda b, j, i: (b * nq + i, j)
    kv = lambda b, j, i: (b, j)
    return pl.pallas_call(
        body, name=name, grid=(Bl, B_HEADS // 2, nq),
        in_specs=[pl.BlockSpec((tq, 256), qrow), pl.BlockSpec((Lp, 256), kv), pl.BlockSpec((Lp, LANES), kv),
                  pl.BlockSpec((1, Lp), lambda b, j, i: (0, 0)), pl.BlockSpec((tq, LANES), qrow),
                  pl.BlockSpec((tq, LANES), qrow), pl.BlockSpec((tq, LANES), qrow)],
        out_specs=(pl.BlockSpec((tq, 256), qrow), pl.BlockSpec((Lp, 256), kv), pl.BlockSpec((Lp, LANES), kv)),
        out_shape=(jax.ShapeDtypeStruct((T, 1024), F32), jax.ShapeDtypeStruct((T, 1024), F32),
                   jax.ShapeDtypeStruct((T, B_W), F32)),
        compiler_params=_params("parallel", "parallel", "arbitrary"),
    )(q, k, v, kmask, o, do, lse)


def _win_bwd(proj, bias, sink, ya, dya, lse, Bl, S, name):
    T = proj.shape[0]
    Lp = S + BLK
    NB = S // BLK
    nblk = Lp // BLK
    G = A_HEADS // A_KV

    def body(sink_ref, q_ref, k_ref, v_ref, b_ref, o_ref, do_ref, lse_ref, dq_ref, dk_ref, dv_ref, db_ref, ds_ref):
        n = pl.program_id(1)

        @pl.when(n == 0)
        def _():
            dk_ref[...] = jnp.zeros_like(dk_ref)
            dv_ref[...] = jnp.zeros_like(dv_ref)
            ds_ref[...] = jnp.zeros_like(ds_ref)

        @pl.when((n == 0) | (n == 1) | (n == NB - 1) | (n == NB))
        def _():
            db_ref[...] = jnp.zeros_like(db_ref)

        base = pl.multiple_of(_win_base(n, NB) * BLK, BLK)
        lane = _lane((BLK, LANES))
        lo = lane < 64
        lsev = lse_ref[...]
        dqs = [None] * A_HEADS
        for kvh in range(A_KV):
            cs = slice(LANES * kvh, LANES * (kvh + 1))
            k2 = jnp.concatenate([k_ref[pl.ds(base, 3 * BLK), cs], k_ref[S:S + BLK, cs]], axis=0).astype(BF16)
            v2 = jnp.concatenate([v_ref[pl.ds(base, 3 * BLK), cs], v_ref[S:S + BLK, cs]], axis=0).astype(BF16)
            dk_acc = jnp.zeros((4 * BLK, LANES), F32)
            dv_acc = jnp.zeros((4 * BLK, LANES), F32)
            for gi in range(G):
                h = kvh * G + gi
                ps = slice(LANES * (h // 2), LANES * (h // 2 + 1))
                hm = lo if h % 2 == 0 else ~lo
                qm = jnp.where(hm, q_ref[:, ps], 0.0).astype(BF16)
                s = _dot_nt(qm, k2) * A_SCALE + b_ref[0, h]
                lse_h = jnp.sum(jnp.where(lane == h, lsev, 0.0), axis=-1, keepdims=True)
                p = jnp.exp(s - lse_h)
                dom = jnp.where(hm, do_ref[:, ps], 0.0)
                delta = jnp.sum(dom * o_ref[:, ps], axis=-1, keepdims=True)
                domb = dom.astype(BF16)
                dsc = p * (_dot_nt(domb, v2) - delta)
                db_ref[0, 0, h] += dsc
                dsink = -jnp.sum(jnp.exp(sink_ref[0, h] - lse_h) * delta, axis=0, keepdims=True)
                ds_ref[0, h:h + 1, :] += jnp.broadcast_to(dsink, (1, LANES))
                dsl = (dsc * A_SCALE).astype(BF16)
                dqs[h] = _dot(dsl, k2)
                dk_acc = dk_acc + _dot_tn(dsl, qm)
                dv_acc = dv_acc + _dot_tn(p.astype(BF16), domb)
            dk_ref[pl.ds(base, 3 * BLK), cs] += dk_acc[0:3 * BLK]
            dk_ref[S:S + BLK, cs] += dk_acc[3 * BLK:4 * BLK]
            dv_ref[pl.ds(base, 3 * BLK), cs] += dv_acc[0:3 * BLK]
            dv_ref[S:S + BLK, cs] += dv_acc[3 * BLK:4 * BLK]
        for j in range(A_HEADS // 2):
            dq_ref[:, LANES * j:LANES * (j + 1)] = jnp.where(lo, dqs[2 * j], dqs[2 * j + 1])

    qrow = lambda b, n: (b * nblk + n, 0)
    kvs = pl.BlockSpec((Lp, 256), lambda b, n: (b, 0))
    return pl.pallas_call(
        body, name=name, grid=(Bl, nblk),
        in_specs=[pl.BlockSpec(memory_space=pltpu.SMEM), pl.BlockSpec((BLK, A_W), qrow),
                  pl.BlockSpec((Lp, 256), lambda b, n: (b, KA0 // 256)), pl.BlockSpec((Lp, 256), lambda b, n: (b, VA0 // 256)),
                  pl.BlockSpec((1, A_HEADS, BLK, 4 * BLK), lambda b, n: (_win_variant(n, NB), 0, 0, 0)),
                  pl.BlockSpec((BLK, A_W), qrow), pl.BlockSpec((BLK, A_W), qrow), pl.BlockSpec((BLK, LANES), qrow)],
        out_specs=(pl.BlockSpec((BLK, A_W), qrow), kvs, kvs,
                   pl.BlockSpec((1, 1, A_HEADS, BLK, 4 * BLK), lambda b, n: (b, _win_variant(n, NB), 0, 0, 0)),
                   pl.BlockSpec((1, 8, LANES), lambda b, n: (b, 0, 0))),
        out_shape=(jax.ShapeDtypeStruct((T, A_W), F32), jax.ShapeDtypeStruct((T, 256), F32),
                   jax.ShapeDtypeStruct((T, 256), F32), jax.ShapeDtypeStruct((Bl, 4, A_HEADS, BLK, 4 * BLK), F32),
                   jax.ShapeDtypeStruct((Bl, 8, LANES), F32)),
        compiler_params=_params("parallel", "arbitrary"),
    )(sink, proj, proj, proj, bias, ya, dya, lse)


def _table_grad(dbias_list, buckets, after, name):
    nl = len(dbias_list)
    Bl = dbias_list[0].shape[0]

    def body(*refs):
        d_refs = refs[:nl]
        b_ref, _, o_ref, acc, part = refs[nl:]
        v, b = pl.program_id(0), pl.program_id(1)

        @pl.when((v == 0) & (b == 0))
        def _():
            o_ref[...] = jnp.zeros_like(o_ref)

        tot = d_refs[0][0, 0]
        for r in d_refs[1:]:
            tot = tot + r[0, 0]

        @pl.when(b == 0)
        def _():
            acc[...] = tot

        @pl.when(b > 0)
        def _():
            acc[...] += tot

        @pl.when(b == Bl - 1)
        def _():
            bidx = b_ref[0]

            def step(j, c):
                mask = bidx == j
                for h in range(A_HEADS):
                    part[h, j] = jnp.sum(jnp.where(mask, acc[h], 0.0).reshape(BLK // 8, 8, 4 * BLK), axis=0)
                return c

            lax.fori_loop(0, N_BUCKETS, step, 0)
            rows = lax.broadcasted_iota(jnp.int32, (N_BUCKETS, LANES), 0)
            lanes = _lane((N_BUCKETS, LANES))
            for h in range(A_HEADS):
                col = jnp.sum(jnp.sum(part[h], axis=1), axis=-1, keepdims=True)
                o_ref[h:h + 1, :] += jnp.sum(jnp.where(rows == lanes, col, 0.0), axis=0, keepdims=True)

    return pl.pallas_call(
        body, name=name, grid=(4, Bl),
        in_specs=[pl.BlockSpec((1, 1, A_HEADS, BLK, 4 * BLK), lambda v, b: (b, v, 0, 0, 0))] * nl
        + [pl.BlockSpec((1, BLK, 4 * BLK), lambda v, b: (v, 0, 0)), pl.BlockSpec(memory_space=pl.ANY)],
        out_specs=pl.BlockSpec((8, LANES), lambda v, b: (0, 0)), out_shape=jax.ShapeDtypeStruct((8, LANES), F32),
        scratch_shapes=[pltpu.VMEM((A_HEADS, BLK, 4 * BLK), F32), pltpu.VMEM((A_HEADS, N_BUCKETS, 8, 4 * BLK), F32)],
        compiler_params=_params("arbitrary", "arbitrary"),
    )(*dbias_list, buckets, after)


def _mla_prep_bwd(dq, dk, dv, proj, gq, gkv, wq, wkk, wkv, tq, tk, tm, Lp, name):
    T = proj.shape[0]
    nt = Lp // tm

    def body(dq_ref, dk_ref, dv_ref, cq_ref, ckv_ref, gq_ref, gkv_ref, wq_ref, wkk_ref, wkv_ref, tq_ref, tk_ref,
             dcq_ref, dckv_ref, dkr_ref, dwq_ref, dwkk_ref, dwkv_ref, dgq_ref, dgkv_ref):
        @pl.when(pl.program_id(0) == 0)
        def _():
            for r in (dwq_ref, dwkk_ref, dwkv_ref, dgq_ref, dgkv_ref):
                r[...] = jnp.zeros_like(r)

        tqv = tq_ref[...]
        dqp = jnp.concatenate([_rope_t(dq_ref[:, LANES * h:LANES * (h + 1)], tqv) for h in range(B_HEADS)],
                              axis=1).astype(BF16)
        cq, gq_ = cq_ref[...], gq_ref[...]
        rq = _rstd(cq)
        dwq_ref[...] += _dot_tn((cq * rq * gq_).astype(BF16), dqp)
        dx, dg = _rms_bwd(_dot_nt(dqp, wq_ref[...]), cq, rq, gq_)
        dcq_ref[...] = dx
        dgq_ref[...] += dg

        dkv_ = dk_ref[...]
        dks = dkv_[:, 0:LANES]
        for h in range(1, B_HEADS):
            dks = dks + dkv_[:, LANES * h:LANES * (h + 1)]
        dkr_ref[...] = _rope_t(pltpu.roll(dks, NOPE, 1), tk_ref[...])
        dkb = dkv_.astype(BF16)
        dvb = dv_ref[...].astype(BF16)
        ckv, gkv_ = ckv_ref[...], gkv_ref[...]
        rk = _rstd(ckv)
        cb = (ckv * rk * gkv_).astype(BF16)
        dwkk_ref[...] += _dot_tn(cb, dkb)
        dwkv_ref[...] += _dot_tn(cb, dvb)
        dx, dg = _rms_bwd(_dot_nt(dkb, wkk_ref[...]) + _dot_nt(dvb, wkv_ref[...]), ckv, rk, gkv_)
        dckv_ref[...] = dx
        dgkv_ref[...] += dg

    tab = pl.BlockSpec((3, tm, LANES), lambda i: (0, i % nt, 0))
    wide = pl.BlockSpec((tm, 1024), _row(0))
    return pl.pallas_call(
        body, name=name, grid=(T // tm,),
        in_specs=[wide, wide, pl.BlockSpec((tm, B_W), _row(0)), pl.BlockSpec((tm, 256), _row(CQ0 // 256)),
                  pl.BlockSpec((tm, 128), _row(CKV0 // 128)), pl.BlockSpec((1, Q_RANK), _const2),
                  pl.BlockSpec((1, KV_RANK), _const2), pl.BlockSpec((Q_RANK, 1024), _const2),
                  pl.BlockSpec((KV_RANK, 1024), _const2), pl.BlockSpec((KV_RANK, B_W), _const2), tab, tab],
        out_specs=(pl.BlockSpec((tm, 256), _row(0)), pl.BlockSpec((tm, 128), _row(0)), pl.BlockSpec((tm, 128), _row(0)),
                   pl.BlockSpec((Q_RANK, 1024), _const2), pl.BlockSpec((KV_RANK, 1024), _const2),
                   pl.BlockSpec((KV_RANK, B_W), _const2), pl.BlockSpec((1, Q_RANK), _const2),
                   pl.BlockSpec((1, KV_RANK), _const2)),
        out_shape=(jax.ShapeDtypeStruct((T, 256), F32), jax.ShapeDtypeStruct((T, 128), F32),
                   jax.ShapeDtypeStruct((T, 128), F32), jax.ShapeDtypeStruct((Q_RANK, 1024), F32),
                   jax.ShapeDtypeStruct((KV_RANK, 1024), F32), jax.ShapeDtypeStruct((KV_RANK, B_W), F32),
                   jax.ShapeDtypeStruct((1, Q_RANK), F32), jax.ShapeDtypeStruct((1, KV_RANK), F32)),
        compiler_params=_params("arbitrary"),
    )(dq, dk, dv, proj, proj, gq, gkv, wq, wkk, wkv, tq, tk)


def _inproj_bwd(pieces, h, g, wx, dh_out, tm, name):
    T = h.shape[0]
    nsteps = T // tm
    CH = 512

    def body(*refs):
        p_refs = refs[:len(pieces)]
        h_ref, g_ref, w_ref, dho_ref, dh_ref, dw_hbm, dg_ref, acc, sem = refs[len(pieces):]
        i = pl.program_id(0)

        @pl.when(i == 0)
        def _():
            acc[...] = jnp.zeros_like(acc)
            dg_ref[...] = jnp.zeros_like(dg_ref)

        dp = jnp.concatenate([r[...].astype(BF16) for r in p_refs], axis=1)
        x, gain = h_ref[...], g_ref[...]
        r = _rstd(x)
        u = (x * r * gain).astype(BF16)
        for c in range(0, NX, CH):
            acc[:, c:c + CH] += _dot_tn(u, dp[:, c:c + CH])
        dx, dg = _rms_bwd(_dot_nt(dp, w_ref[...]), x, r, gain)
        dh_ref[...] = dho_ref[...] + dx
        dg_ref[...] += dg

        @pl.when(i == nsteps - 1)
        def _():
            cp = pltpu.make_async_copy(acc, dw_hbm, sem)
            cp.start()
            cp.wait()

    return pl.pallas_call(
        body, name=name, grid=(nsteps,),
        in_specs=[pl.BlockSpec((tm, p.shape[1]), _row(0)) for p in pieces]
        + [pl.BlockSpec((tm, D), _row(0)), pl.BlockSpec((1, D), _const2), pl.BlockSpec((D, NX), _const2),
           pl.BlockSpec((tm, D), _row(0))],
        out_specs=(pl.BlockSpec((tm, D), _row(0)), pl.BlockSpec(memory_space=pl.ANY), pl.BlockSpec((1, D), _const2)),
        out_shape=(jax.ShapeDtypeStruct((T, D), F32), jax.ShapeDtypeStruct((D, NX), F32), jax.ShapeDtypeStruct((1, D), F32)),
        scratch_shapes=[pltpu.VMEM((D, NX), F32), pltpu.SemaphoreType.DMA(())],
        compiler_params=_params("arbitrary"),
    )(*pieces, h, g, wx, dh_out)


def kernel(x, meta_tokens, rel_bias_table, norm_in, w_in, sink_a, norm_q_lat, w_uq, norm_kv_lat, w_ukv, norm_out_a, norm_out_b, w_out, norm_final, loss_target, m_meta_tokens, m_rel_bias_table, m_norm_in, m_w_in, m_sink_a, m_norm_q_lat, m_w_uq, m_norm_kv_lat, m_w_ukv, m_norm_out_a, m_norm_out_b, m_w_out, m_norm_final, v_meta_tokens, v_rel_bias_table, v_norm_in, v_w_in, v_sink_a, v_norm_q_lat, v_w_uq, v_norm_kv_lat, v_w_ukv, v_norm_out_a, v_norm_out_b, v_w_out, v_norm_final):
    Bl, S, _ = x.shape
    Lp = S + BLK
    T = Bl * Lp
    tm = Lp // 4
    tq = Lp // 8
    depth = w_in.shape[0]
    me = 4 * lax.axis_index("x") + 2 * lax.axis_index("y") + lax.axis_index("c")

    def finish(handles, after, name):
        srcs, lands = _xchg_wait(handles, after, name)
        full = []
        for hd, src, land in zip(handles, srcs, lands):
            own = src if hd[4] == "gather" else lax.dynamic_index_in_dim(src, me, 0, keepdims=False)
            full.append(_own_slot(land, own, me))
        return full

    def weights_of(i):
        return [w_in[i].astype(BF16), w_uq[i].astype(BF16), w_ukv[i].astype(BF16), w_out[i].astype(BF16)]

    def packed(l_in, l_uq, l_ukv, l_out):
        cols = lambda t: jnp.transpose(t, (1, 0, 2)).reshape(t.shape[1], NDEV * t.shape[2])
        wkk, wkv = _pack_w_ukv(cols(l_ukv))
        return _pack_w_in(cols(l_in)), _pack_w_uq(cols(l_uq)), wkk, wkv, l_out.reshape(D, D)

    wb = weights_of(0)
    gat_a, tok = _xchg_start([wb[0], meta_tokens], ["gather", "gather"], "gather_start_0a")
    buckets = _bias_buckets(S)
    bias = _build_bias(buckets, _after(rel_bias_table, tok), "build_bias")
    tq_tab = _rope_tables(S, Lp, NOPE, True)
    tk_tab = _rope_tables(S, Lp, 0, False)
    kmask = jnp.where(jnp.arange(Lp) < S + NMETA, 0.0, NEG).astype(F32).reshape(1, Lp)
    l_in, l_meta = finish(gat_a, bias, "gather_wait_0a")
    gat_b, tok = _xchg_start(_after(wb[1:], l_in), ["gather"] * 3, "gather_start_0b")

    meta_f = jnp.transpose(l_meta, (1, 0, 2)).reshape(NMETA, D)
    tail = jnp.concatenate([meta_f, jnp.zeros((BLK - NMETA, D), F32)], axis=0)
    h = jnp.concatenate([x, jnp.broadcast_to(tail[None], (Bl, BLK, D))], axis=1).reshape(T, D)

    saved = []
    w_out_f = [None] * depth
    gat_next = None
    for i in range(depth):
        g_in, g_q, g_kv = norm_in[i][None], norm_q_lat[i][None], norm_kv_lat[i][None]
        g_a, g_b, sink = norm_out_a[i][None], norm_out_b[i][None], sink_a[i][None]
        if i == 0:
            proj = _inproj_fwd(_after(h, tok), g_in, _pack_w_in(jnp.transpose(l_in, (1, 0, 2)).reshape(D, IN_W)), tm,
                               f"inproj_fwd_{i}")
            l_uq, l_ukv, l_out = finish(gat_b, proj, "gather_wait_0b")
            wx, wq, wkk, wkv, w_out_f[i] = packed(l_in, l_uq, l_ukv, l_out)
        else:
            wx, wq, wkk, wkv, w_out_f[i] = packed(*finish(gat_next, h, f"gather_wait_{i}"))
            proj = _inproj_fwd(h, g_in, wx, tm, f"inproj_fwd_{i}")
        if i + 1 < depth:
            gat_next, tok = _xchg_start(_after(weights_of(i + 1), proj), ["gather"] * 4, f"gather_start_{i + 1}")
            g_q = _after(g_q, tok)
        q, k, v = _mla_prep_fwd(proj, g_q, g_kv, wq, wkk, wkv, tq_tab, tk_tab, tm, Lp, f"mla_prep_fwd_{i}")
        ya, lse_a = _win_fwd(proj, bias, sink, Bl, S, f"win_fwd_{i}")
        yb, lse_b = _mla_fwd(q, k, v, kmask, Bl, Lp, tq, f"mla_fwd_{i}")
        h_new = _out_fwd(ya, yb, proj, g_a, g_b, w_out_f[i], h, tm, f"out_fwd_{i}")
        saved.append((h, proj, q, k, v, ya, lse_a, yb, lse_b, wx, wq, wkk, wkv))
        h = h_new

    dh, loss_acc, dg_final = _loss_head(h, norm_final[None], loss_target, Bl, S, "loss_head")
    loss = lax.psum(loss_acc[0, 0], ("x", "y", "c"))

    g_n_in, g_sink, g_nq, g_nkv, g_na, g_nb = ([None] * depth for _ in range(6))
    dbias_all, sc_out, sc_rest = [], [None] * depth, [None] * depth
    split = lambda t, n: jnp.transpose(t.reshape(t.shape[0], NDEV, n), (1, 0, 2)).astype(BF16)
    tok = None
    for i in reversed(range(depth)):
        h_in, proj, q, k, v, ya, lse_a, yb, lse_b, wx, wq, wkk, wkv = saved[i]
        g_in, g_q, g_kv = norm_in[i][None], norm_q_lat[i][None], norm_kv_lat[i][None]
        g_a, g_b, sink = norm_out_a[i][None], norm_out_b[i][None], sink_a[i][None]
        if tok is not None:
            g_a = _after(g_a, tok)
        dya, dyb, dga, dgb, g_w_out, g_na[i], g_nb[i] = _out_bwd(dh, ya, yb, proj, g_a, g_b, w_out_f[i], tm, f"out_bwd_{i}")
        sc_out[i], tok = _xchg_start([g_w_out.reshape(NDEV, D // NDEV, D).astype(BF16)], ["scatter"], f"scatter_start_{i}a")
        dq, dk, dv = _mla_bwd(q, k, v, _after(kmask, tok), yb, dyb, lse_b, Bl, Lp, tq, f"mla_bwd_{i}")
        dqa, dka, dva, dbias, dsink = _win_bwd(proj, bias, sink, ya, dya, lse_a, Bl, S, f"win_bwd_{i}")
        dcq, dckv, dkr, dwq, dwkk, dwkv, g_nq[i], g_nkv[i] = _mla_prep_bwd(
            dq, dk, dv, proj, g_q, g_kv, wq, wkk, wkv, tq_tab, tk_tab, tm, Lp, f"mla_prep_bwd_{i}")
        dh, dwx, g_n_in[i] = _inproj_bwd((dqa, dka, dva, dga, dcq, dckv, dkr, dgb), h_in, g_in, wx, dh, tq,
                                         f"inproj_bwd_{i}")
        g_sink[i] = jnp.sum(dsink[:, :, 0], axis=0)
        dbias_all.append(dbias)
        rest = [split(_unpack_w_in_grad(dwx), IN_W // NDEV), split(_unpack_w_uq_grad(dwq), 768 // NDEV),
                split(_unpack_w_ukv_grad(dwkk, dwkv), 1024 // NDEV)]
        if i == 0:
            dh3 = dh.reshape(Bl, Lp, D)
            grad_x = dh3[:, :S]
            rest.append(jnp.transpose(jnp.sum(dh3[:, S:S + NMETA], axis=0).reshape(NMETA, NDEV, D // NDEV), (1, 0, 2)))
        sc_rest[i], tok = _xchg_start(rest, ["scatter"] * len(rest), f"scatter_start_{i}b")

    dtab = _table_grad(dbias_all, buckets, tok, "table_grad")
    two_d = lambda a: a.reshape(1, -1) if a.ndim == 1 else a
    small_w = [rel_bias_table, norm_in, sink_a, norm_q_lat, norm_kv_lat, norm_out_a, norm_out_b, norm_final]
    small_m = [m_rel_bias_table, m_norm_in, m_sink_a, m_norm_q_lat, m_norm_kv_lat, m_norm_out_a, m_norm_out_b, m_norm_final]
    small_v = [v_rel_bias_table, v_norm_in, v_sink_a, v_norm_q_lat, v_norm_kv_lat, v_norm_out_a, v_norm_out_b, v_norm_final]
    small_g = [jnp.transpose(dtab[:, :N_BUCKETS]), jnp.concatenate(g_n_in), jnp.stack(g_sink), jnp.concatenate(g_nq),
               jnp.concatenate(g_nkv), jnp.concatenate(g_na), jnp.concatenate(g_nb), dg_final]
    sc_small, tok = _xchg_start(small_g, ["gather"] * len(small_g), "scatter_start_small")

    r_out, r_in, r_uq, r_ukv = [None] * depth, [None] * depth, [None] * depth, [None] * depth
    for i in reversed(range(depth)):
        (r_out[i],) = finish(sc_out[i], tok, f"scatter_wait_{i}a")
        got = finish(sc_rest[i], tok, f"scatter_wait_{i}b")
        r_in[i], r_uq[i], r_ukv[i] = got[:3]
        if i == 0:
            r_meta = got[3]
    r_small = finish(sc_small, tok, "scatter_wait_small")

    res = {}
    sharded = dict(w_in=(r_in, w_in, m_w_in, v_w_in), w_uq=(r_uq, w_uq, m_w_uq, v_w_uq), w_ukv=(r_ukv, w_ukv, m_w_ukv, v_w_ukv),
                   w_out=(r_out, w_out, m_w_out, v_w_out),
                   meta_tokens=([r_meta], meta_tokens[None], m_meta_tokens[None], v_meta_tokens[None]))
    for n, (r, w, m, v) in sharded.items():
        outs = _adamw_param(r, w, m, v, f"adamw_{n}")
        res[n] = [o[0] for o in outs] if n == "meta_tokens" else list(outs)
    names_small = ["rel_bias_table", "norm_in", "sink_a", "norm_q_lat", "norm_kv_lat", "norm_out_a", "norm_out_b", "norm_final"]
    outs = _adamw_small(r_small, [two_d(a) for a in small_w], [two_d(a) for a in small_m], [two_d(a) for a in small_v],
                        "adamw_replicated")
    ns = len(names_small)
    for j, n in enumerate(names_small):
        res[n] = [outs[kk * ns + j].reshape(small_w[j].shape) for kk in range(4)]
    order = ["meta_tokens", "rel_bias_table", "norm_in", "w_in", "sink_a", "norm_q_lat", "w_uq", "norm_kv_lat", "w_ukv",
             "norm_out_a", "norm_out_b", "w_out", "norm_final"]
    return (loss, grad_x, *[res[n][kk] for kk in range(4) for n in order])
```

```python
import functools
import math

import numpy as np
import jax
import jax.numpy as jnp
from jax import lax
from jax.experimental import pallas as pl
from jax.experimental.pallas import tpu as pltpu

F32, BF16 = jnp.float32, jnp.bfloat16
D = 1024
NMETA = 16
BLK = 128
A_HEADS, A_KV, A_DH, A_W = 8, 2, 64, 512
B_HEADS, NOPE, ROPE, B_V, B_W = 8, 64, 32, 64, 512
Q_RANK, KV_RANK = 256, 128
IN_W = 2208
N_BUCKETS, MAX_DIST = 32, 128
THETA = 10000.0
EPS = 1e-6
NEG = -1e30
A_SCALE = A_DH ** -0.5
B_SCALE = (NOPE + ROPE) ** -0.5
LANES = 128
NDEV = 8
MESH = pl.DeviceIdType.MESH

NX = 2560
QA0, KA0, VA0, GA0, CQ0, CKV0, KR0, GB0 = 0, 512, 768, 1024, 1536, 1792, 1920, 2048

ADAM_LR, ADAM_B1, ADAM_B2, ADAM_EPS, ADAM_WD, ADAM_STEP = 0.001, 0.9, 0.999, 1e-08, 0.01, 10


def _dot(a, b):
    return jnp.dot(a, b, preferred_element_type=F32)


def _dot_nt(a, b):
    return lax.dot_general(a, b, (((1,), (1,)), ((), ())), preferred_element_type=F32)


def _dot_tn(a, b):
    return lax.dot_general(a, b, (((0,), (0,)), ((), ())), preferred_element_type=F32)


def _lane(shape):
    return lax.broadcasted_iota(jnp.int32, shape, len(shape) - 1)


def _rstd(x):
    return lax.rsqrt(jnp.mean(x * x, axis=-1, keepdims=True) + EPS)


def _rms_bwd(dn, x, r, g):
    z = dn * g
    dx = r * z - x * (r * r * r) * jnp.mean(z * x, axis=-1, keepdims=True)
    return dx, jnp.sum(dn * (x * r), axis=0, keepdims=True)


def _row(i):
    return lambda *ids: (ids[0], i)


def _const2(*ids):
    return (0, 0)


def _params(*sem):
    return pltpu.CompilerParams(dimension_semantics=sem)


def _pack_w_in(w):
    qa, ka, va, ga = w[:, 0:512], w[:, 512:640], w[:, 640:768], w[:, 768:1280]
    cq, ckv, kr, gb = w[:, 1280:1536], w[:, 1536:1664], w[:, 1664:1696], w[:, 1696:2208]
    dup = lambda t: jnp.concatenate([t[:, 0:64], t[:, 0:64], t[:, 64:128], t[:, 64:128]], axis=1)
    krp = jnp.concatenate([kr, jnp.zeros((w.shape[0], LANES - ROPE), w.dtype)], axis=1)
    return jnp.concatenate([qa, dup(ka), dup(va), ga, cq, ckv, krp, gb], axis=1)


def _unpack_w_in_grad(g):
    fold = lambda t: jnp.concatenate([t[:, 0:64] + t[:, 64:128], t[:, 128:192] + t[:, 192:256]], axis=1)
    return jnp.concatenate([g[:, QA0:QA0 + 512], fold(g[:, KA0:KA0 + 256]), fold(g[:, VA0:VA0 + 256]),
                            g[:, GA0:GA0 + 512], g[:, CQ0:CQ0 + 256], g[:, CKV0:CKV0 + 128],
                            g[:, KR0:KR0 + ROPE], g[:, GB0:GB0 + 512]], axis=1)


def _pack_w_uq(w):
    t = w.reshape(Q_RANK, B_HEADS, NOPE + ROPE)
    t = jnp.concatenate([t, jnp.zeros((Q_RANK, B_HEADS, LANES - NOPE - ROPE), w.dtype)], axis=-1)
    return t.reshape(Q_RANK, B_HEADS * LANES)


def _unpack_w_uq_grad(g):
    return g.reshape(Q_RANK, B_HEADS, LANES)[:, :, :NOPE + ROPE].reshape(Q_RANK, B_HEADS * (NOPE + ROPE))


def _pack_w_ukv(w):
    t = w.reshape(KV_RANK, B_HEADS, NOPE + B_V)
    kk = jnp.concatenate([t[:, :, :NOPE], jnp.zeros((KV_RANK, B_HEADS, LANES - NOPE), w.dtype)], axis=-1)
    return kk.reshape(KV_RANK, B_HEADS * LANES), t[:, :, NOPE:].reshape(KV_RANK, B_W)


def _unpack_w_ukv_grad(gk, gv):
    t = jnp.concatenate([gk.reshape(KV_RANK, B_HEADS, LANES)[:, :, :NOPE], gv.reshape(KV_RANK, B_HEADS, B_V)], axis=-1)
    return t.reshape(KV_RANK, B_HEADS * (NOPE + B_V))


def _t5_bucket(rel):
    nb = N_BUCKETS // 2
    max_exact = nb // 2
    ret = jnp.where(rel > 0, nb, 0)
    n = jnp.abs(rel)
    nf = jnp.maximum(n, 1).astype(F32)
    large = max_exact + (jnp.log(nf / max_exact) / math.log(MAX_DIST / max_exact) * (nb - max_exact)).astype(jnp.int32)
    large = jnp.minimum(large, nb - 1)
    return ret + jnp.where(n < max_exact, n, large)


def _bias_buckets(S):
    q = np.arange(BLK)[:, None]
    k = np.arange(4 * BLK)[None, :]
    is_meta_key = (k >= 3 * BLK) & (k < 3 * BLK + NMETA)
    mi = k - 3 * BLK
    rels, valids = [], []
    for shift in (0, BLK, 2 * BLK):
        rel_real = k - shift - q
        valid_real = (k < 3 * BLK) & (np.abs(rel_real) <= BLK)
        far = -(NMETA + MAX_DIST + BLK)
        rel_meta = (mi - (NMETA + q)) if shift == 0 else np.full_like(k + q, far)
        rels.append(np.where(is_meta_key, rel_meta, rel_real))
        valids.append(valid_real | is_meta_key)
    qm = q < NMETA
    rel_real = NMETA + k - q
    valid_real = (k < BLK) & (np.abs(rel_real) <= BLK) & qm
    rels.append(np.where(is_meta_key, mi - q, rel_real))
    valids.append(valid_real | is_meta_key)
    rel = jnp.asarray(np.stack(rels).astype(np.int32))
    valid = jnp.asarray(np.stack(valids))
    return jnp.where(valid, _t5_bucket(rel), -1).astype(jnp.int32)


def _rope_tables(S, Lp, off, passthrough):
    half = ROPE // 2
    r = np.arange(Lp)
    pos = np.where(r < S, NMETA + r, np.where(r < S + NMETA, r - S, 0)).astype(np.float32)
    freqs = THETA ** (-jnp.arange(half, dtype=F32) / half)
    ang = jnp.asarray(pos)[:, None] * freqs[None, :]
    cos, sin = jnp.cos(ang), jnp.sin(ang)
    z = lambda n: jnp.zeros((Lp, n), F32)
    head = jnp.ones((Lp, off), F32) if passthrough else z(off)
    c = jnp.concatenate([head, cos, cos, z(LANES - off - ROPE)], axis=1)
    s1 = jnp.concatenate([z(off), -sin, z(LANES - off - half)], axis=1)
    s2 = jnp.concatenate([z(off + half), sin, z(LANES - off - ROPE)], axis=1)
    return jnp.stack([c, s1, s2])


def _rope(x, t):
    return x * t[0] + pltpu.roll(x, LANES - 16, 1) * t[1] + pltpu.roll(x, 16, 1) * t[2]


def _rope_t(dy, t):
    return dy * t[0] + pltpu.roll(dy * t[1], 16, 1) + pltpu.roll(dy * t[2], LANES - 16, 1)


def _me_and_peers():
    x, y, c = lax.axis_index("x"), lax.axis_index("y"), lax.axis_index("c")
    flip = lambda v, b: 1 - v if b else v
    peers = [(flip(x, k & 4), flip(y, k & 2), flip(c, k & 1)) for k in range(1, NDEV)]
    return 4 * x + 2 * y + c, peers


def _xchg_start(srcs, kinds, name, after=None):
    n = len(srcs)
    lands = [lax.empty((NDEV,) + s.shape[-2:], s.dtype) for s in srcs]
    extra = [] if after is None else [after]

    def body(*refs):
        src_refs, land_refs = refs[:n], refs[n:2 * n]
        ssems, rsems = refs[2 * n + len(extra):3 * n + len(extra)], refs[3 * n + len(extra):4 * n + len(extra)]
        token = refs[6 * n + len(extra)]
        me, peers = _me_and_peers()
        for it in range(n):
            for k, (px, py, pc) in enumerate(peers):
                src = src_refs[it] if kinds[it] == "gather" else src_refs[it].at[4 * px + 2 * py + pc]
                pltpu.make_async_remote_copy(src_ref=src, dst_ref=land_refs[it].at[me], send_sem=ssems[it].at[k],
                                             recv_sem=rsems[it].at[k], device_id=(px, py, pc), device_id_type=MESH).start()
        token[...] = jnp.zeros_like(token)

    hbm = pl.BlockSpec(memory_space=pltpu.HBM)
    sem = pl.BlockSpec(memory_space=pltpu.SEMAPHORE)
    outs = pl.pallas_call(
        body, name=name,
        out_shape=tuple([pltpu.SemaphoreType.DMA((NDEV - 1,))] * (2 * n) + [pltpu.HBM(a.shape, a.dtype) for a in srcs + lands]
                        + [jax.ShapeDtypeStruct((8, LANES), F32)]),
        in_specs=[hbm] * (2 * n) + [pl.BlockSpec(memory_space=pl.ANY)] * len(extra),
        out_specs=tuple([sem] * (2 * n) + [hbm] * (2 * n) + [pl.BlockSpec(memory_space=pltpu.VMEM)]),
        input_output_aliases={i: 2 * n + i for i in range(2 * n)},
        compiler_params=pltpu.CompilerParams(has_side_effects=pltpu.SideEffectType.DATAFLOW_SIDE_EFFECTING),
    )(*[pltpu.with_memory_space_constraint(a, pltpu.HBM) for a in srcs + lands], *extra)
    handles = [(outs[it], outs[n + it], outs[2 * n + it], outs[3 * n + it], kinds[it]) for it in range(n)]
    return handles, outs[4 * n]


def _xchg_wait(handles, after, name):
    n = len(handles)

    def body(*refs):
        src_refs, land_refs = refs[:n], refs[n:2 * n]
        ssems, rsems = refs[2 * n:3 * n], refs[3 * n:4 * n]
        me, peers = _me_and_peers()
        for it in range(n):
            for k, (px, py, pc) in enumerate(peers):
                src = src_refs[it] if handles[it][4] == "gather" else src_refs[it].at[4 * px + 2 * py + pc]
                cp = pltpu.make_async_remote_copy(src_ref=src, dst_ref=land_refs[it].at[me], send_sem=ssems[it].at[k],
                                                  recv_sem=rsems[it].at[k], device_id=(px, py, pc), device_id_type=MESH)
                cp.wait_send()
                cp.wait_recv()

    hbm = pl.BlockSpec(memory_space=pltpu.HBM)
    sem = pl.BlockSpec(memory_space=pltpu.SEMAPHORE)
    srcs, lands = [h[2] for h in handles], [h[3] for h in handles]
    outs = pl.pallas_call(
        body, name=name, out_shape=tuple(pltpu.HBM(a.shape, a.dtype) for a in srcs + lands),
        in_specs=[hbm] * (2 * n) + [sem] * (2 * n) + [pl.BlockSpec(memory_space=pl.ANY)], out_specs=tuple([hbm] * (2 * n)),
        input_output_aliases={i: i for i in range(2 * n)},
        compiler_params=pltpu.CompilerParams(has_side_effects=pltpu.SideEffectType.DATAFLOW_SIDE_EFFECTING),
    )(*srcs, *lands, *[h[0] for h in handles], *[h[1] for h in handles], after)
    return list(outs[:n]), list(outs[n:])


def _own_slot(land, own, me):
    return lax.dynamic_update_slice(land, own[None].astype(land.dtype), (me, 0, 0))


def _after(x, token):
    return x + token[0, 0]


def _adamw_update(g, w, m, v):
    mn = ADAM_B1 * m + (1.0 - ADAM_B1) * g
    vn = ADAM_B2 * v + (1.0 - ADAM_B2) * (g * g)
    m_hat = mn / (1.0 - ADAM_B1 ** ADAM_STEP)
    v_hat = vn / (1.0 - ADAM_B2 ** ADAM_STEP)
    return -ADAM_LR * (m_hat / (jnp.sqrt(v_hat) + ADAM_EPS) + ADAM_WD * w), mn, vn


def _sum_partials(r_ref):
    g = r_ref[0].astype(F32)
    for p in range(1, NDEV):
        g = g + r_ref[p].astype(F32)
    return g


def _adamw_param(recvs, w, m, v, name):
    depth, r, c = w.shape
    tr = min(r, 256)
    nl = len(recvs)

    def body(*refs):
        r_refs = refs[:nl]
        w_ref, m_ref, v_ref, g_out, d_out, m_out, v_out = refs[nl:]
        for li in range(nl):
            @pl.when(pl.program_id(0) == li)
            def _(li=li):
                g = _sum_partials(r_refs[li])
                g_out[0] = g
                d_out[0], m_out[0], v_out[0] = _adamw_update(g, w_ref[0], m_ref[0], v_ref[0])

    blk = pl.BlockSpec((1, tr, c), lambda l, i: (l, i, 0))
    return pl.pallas_call(
        body, name=name, grid=(depth, r // tr),
        in_specs=[pl.BlockSpec((NDEV, tr, c), lambda l, i: (0, i, 0))] * nl + [blk, blk, blk],
        out_specs=(blk, blk, blk, blk), out_shape=(jax.ShapeDtypeStruct(w.shape, F32),) * 4,
        compiler_params=_params("arbitrary", "arbitrary"),
    )(*recvs, w, m, v)


def _adamw_small(recvs, ws, ms, vs, name):
    n = len(ws)

    def body(*refs):
        r_refs, w_refs, m_refs, v_refs = refs[:n], refs[n:2 * n], refs[2 * n:3 * n], refs[3 * n:4 * n]
        outs = refs[4 * n:]
        for j in range(n):
            g = _sum_partials(r_refs[j])
            outs[j][...] = g
            outs[n + j][...], outs[2 * n + j][...], outs[3 * n + j][...] = _adamw_update(
                g, w_refs[j][...], m_refs[j][...], v_refs[j][...])

    vm = pl.BlockSpec(memory_space=pltpu.VMEM)
    return pl.pallas_call(
        body, name=name, in_specs=[vm] * (4 * n), out_specs=tuple([vm] * (4 * n)),
        out_shape=tuple(jax.ShapeDtypeStruct(a.shape, F32) for a in ws) * 4,
    )(*recvs, *ws, *ms, *vs)


def _inproj_fwd(h, g, wx, tm, name):
    T = h.shape[0]

    def body(h_ref, g_ref, w_ref, o_ref):
        x = h_ref[...]
        u = (x * _rstd(x) * g_ref[...]).astype(BF16)
        o_ref[...] = _dot(u, w_ref[...])

    return pl.pallas_call(
        body, name=name, grid=(T // tm,),
        in_specs=[pl.BlockSpec((tm, D), _row(0)), pl.BlockSpec((1, D), _const2), pl.BlockSpec((D, NX), _const2)],
        out_specs=pl.BlockSpec((tm, NX), _row(0)), out_shape=jax.ShapeDtypeStruct((T, NX), F32),
        compiler_params=_params("parallel"),
    )(h, g, wx)


def _mla_prep_fwd(proj, gq, gkv, wq, wkk, wkv, tq, tk, tm, Lp, name):
    T = proj.shape[0]
    nt = Lp // tm

    def body(cq_ref, ckv_ref, kr_ref, gq_ref, gkv_ref, wq_ref, wkk_ref, wkv_ref, tq_ref, tk_ref, q_ref, k_ref, v_ref):
        cq = cq_ref[...]
        q = _dot((cq * _rstd(cq) * gq_ref[...]).astype(BF16), wq_ref[...])
        tqv = tq_ref[...]
        for h in range(B_HEADS):
            cs = slice(LANES * h, LANES * (h + 1))
            q_ref[:, cs] = _rope(q[:, cs], tqv).astype(BF16)
        ksh = pltpu.roll(_rope(kr_ref[...], tk_ref[...]), NOPE, 1)
        ckv = ckv_ref[...]
        cb = (ckv * _rstd(ckv) * gkv_ref[...]).astype(BF16)
        kn = _dot(cb, wkk_ref[...])
        for h in range(B_HEADS):
            cs = slice(LANES * h, LANES * (h + 1))
            k_ref[:, cs] = (kn[:, cs] + ksh).astype(BF16)
        v_ref[...] = _dot(cb, wkv_ref[...]).astype(BF16)

    tab = pl.BlockSpec((3, tm, LANES), lambda i: (0, i % nt, 0))
    return pl.pallas_call(
        body, name=name, grid=(T // tm,),
        in_specs=[pl.BlockSpec((tm, 256), _row(CQ0 // 256)), pl.BlockSpec((tm, 128), _row(CKV0 // 128)),
                  pl.BlockSpec((tm, 128), _row(KR0 // 128)), pl.BlockSpec((1, Q_RANK), _const2),
                  pl.BlockSpec((1, KV_RANK), _const2), pl.BlockSpec((Q_RANK, 1024), _const2),
                  pl.BlockSpec((KV_RANK, 1024), _const2), pl.BlockSpec((KV_RANK, B_W), _const2), tab, tab],
        out_specs=(pl.BlockSpec((tm, 1024), _row(0)), pl.BlockSpec((tm, 1024), _row(0)), pl.BlockSpec((tm, B_W), _row(0))),
        out_shape=(jax.ShapeDtypeStruct((T, 1024), BF16), jax.ShapeDtypeStruct((T, 1024), BF16),
                   jax.ShapeDtypeStruct((T, B_W), BF16)),
        compiler_params=_params("parallel"),
    )(proj, proj, proj, gq, gkv, wq, wkk, wkv, tq, tk)


def _build_bias(buckets, table, name):
    def body(tab_ref, b_ref, o_ref):
        bidx = b_ref[0]
        for h in range(A_HEADS):
            def step(j, acc):
                return jnp.where(bidx == j, tab_ref[j, h], acc)
            o_ref[0, h] = lax.fori_loop(0, N_BUCKETS, step, jnp.full(bidx.shape, NEG, F32))

    return pl.pallas_call(
        body, name=name, grid=(4,),
        in_specs=[pl.BlockSpec(memory_space=pltpu.SMEM), pl.BlockSpec((1, BLK, 4 * BLK), lambda i: (i, 0, 0))],
        out_specs=pl.BlockSpec((1, A_HEADS, BLK, 4 * BLK), lambda i: (i, 0, 0, 0)),
        out_shape=jax.ShapeDtypeStruct((4, A_HEADS, BLK, 4 * BLK), F32),
        compiler_params=_params("parallel"),
    )(table, buckets)


def _win_base(n, NB):
    return jnp.where(n == NB, 0, jnp.clip(n - 1, 0, NB - 3))


def _win_variant(n, NB):
    return jnp.where(n == 0, 0, jnp.where(n < NB - 1, 1, jnp.where(n == NB - 1, 2, 3)))


def _win_fwd(proj, bias, sink, Bl, S, name):
    T = proj.shape[0]
    Lp = S + BLK
    NB = S // BLK
    nblk = Lp // BLK

    def body(sink_ref, q_ref, k_ref, v_ref, b_ref, o_ref, lse_ref):
        n = pl.program_id(1)
        base = pl.multiple_of(_win_base(n, NB) * BLK, BLK)
        lane = _lane((BLK, LANES))
        lo = lane < 64
        outs, lses = [None] * A_HEADS, [None] * A_HEADS
        for kvh in range(A_KV):
            cs = slice(LANES * kvh, LANES * (kvh + 1))
            k2 = jnp.concatenate([k_ref[pl.ds(base, 3 * BLK), cs], k_ref[S:S + BLK, cs]], axis=0).astype(BF16)
            v2 = jnp.concatenate([v_ref[pl.ds(base, 3 * BLK), cs], v_ref[S:S + BLK, cs]], axis=0).astype(BF16)
            for gi in range(A_HEADS // A_KV):
                h = kvh * (A_HEADS // A_KV) + gi
                qp = q_ref[:, LANES * (h // 2):LANES * (h // 2 + 1)]
                qm = jnp.where(lo if h % 2 == 0 else ~lo, qp, 0.0).astype(BF16)
                s = _dot_nt(qm, k2) * A_SCALE + b_ref[0, h]
                sk = sink_ref[0, h]
                m = jnp.maximum(jnp.max(s, axis=-1, keepdims=True), sk)
                e = jnp.exp(s - m)
                den = jnp.sum(e, axis=-1, keepdims=True) + jnp.exp(sk - m)
                p = e * (1.0 / den)
                outs[h] = _dot(p.astype(BF16), v2)
                lses[h] = m + jnp.log(den)
        for j in range(A_HEADS // 2):
            o_ref[:, LANES * j:LANES * (j + 1)] = jnp.where(lo, outs[2 * j], outs[2 * j + 1])
        t = jnp.zeros((BLK, LANES), F32)
        for h in range(A_HEADS):
            t = jnp.where(lane == h, lses[h], t)
        lse_ref[...] = t

    qrow = lambda b, n: (b * nblk + n, 0)
    return pl.pallas_call(
        body, name=name, grid=(Bl, nblk),
        in_specs=[pl.BlockSpec(memory_space=pltpu.SMEM), pl.BlockSpec((BLK, A_W), qrow),
                  pl.BlockSpec((Lp, 256), lambda b, n: (b, KA0 // 256)), pl.BlockSpec((Lp, 256), lambda b, n: (b, VA0 // 256)),
                  pl.BlockSpec((1, A_HEADS, BLK, 4 * BLK), lambda b, n: (_win_variant(n, NB), 0, 0, 0))],
        out_specs=(pl.BlockSpec((BLK, A_W), qrow), pl.BlockSpec((BLK, LANES), qrow)),
        out_shape=(jax.ShapeDtypeStruct((T, A_W), F32), jax.ShapeDtypeStruct((T, LANES), F32)),
        compiler_params=_params("parallel", "arbitrary"),
    )(sink, proj, proj, proj, bias)


def _mla_fwd(q, k, v, kmask, Bl, Lp, tq, name):
    T = q.shape[0]
    nq = Lp // tq

    def body(q_ref, k_ref, v_ref, km_ref, o_ref, lse_ref):
        lane = _lane((tq, LANES))
        vb = v_ref[...]
        km = km_ref[...]
        outs = []
        t = jnp.zeros((tq, LANES), F32)
        for hh in range(2):
            cs = slice(LANES * hh, LANES * (hh + 1))
            s = _dot_nt(q_ref[:, cs], k_ref[:, cs]) * B_SCALE + km
            m = jnp.max(s, axis=-1, keepdims=True)
            e = jnp.exp(s - m)
            den = jnp.sum(e, axis=-1, keepdims=True)
            outs.append(_dot(e.astype(BF16), vb) * (1.0 / den))
            t = jnp.where(lane == hh, m + jnp.log(den), t)
        o_ref[...] = jnp.where(lane < 64, outs[0], outs[1])
        lse_ref[...] = t

    qrow = lambda b, j, i: (b * nq + i, j)
    return pl.pallas_call(
        body, name=name, grid=(Bl, B_HEADS // 2, nq),
        in_specs=[pl.BlockSpec((tq, 256), qrow), pl.BlockSpec((Lp, 256), lambda b, j, i: (b, j)),
                  pl.BlockSpec((Lp, LANES), lambda b, j, i: (b, j)), pl.BlockSpec((1, Lp), lambda b, j, i: (0, 0))],
        out_specs=(pl.BlockSpec((tq, LANES), qrow), pl.BlockSpec((tq, LANES), qrow)),
        out_shape=(jax.ShapeDtypeStruct((T, B_W), F32), jax.ShapeDtypeStruct((T, B_W), F32)),
        compiler_params=_params("parallel", "parallel", "arbitrary"),
    )(q, k, v, kmask)


def _gated(y, gate, gain):
    r = _rstd(y)
    nrm = y * r
    sg = jax.nn.sigmoid(gate)
    return r, nrm, sg, nrm * gain, gate * sg


def _out_fwd(ya, yb, proj, na, nb, wout, h, tm, name):
    T = h.shape[0]

    def body(ya_ref, yb_ref, ga_ref, gb_ref, na_ref, nb_ref, w_ref, h_ref, o_ref):
        _, _, _, n_a, sl_a = _gated(ya_ref[...], ga_ref[...], na_ref[...])
        _, _, _, n_b, sl_b = _gated(yb_ref[...], gb_ref[...], nb_ref[...])
        acc = _dot((n_a * sl_a).astype(BF16), w_ref[0:A_W, :]) + _dot((n_b * sl_b).astype(BF16), w_ref[A_W:A_W + B_W, :])
        o_ref[...] = h_ref[...] + acc

    half = pl.BlockSpec((tm, 512), _row(0))
    return pl.pallas_call(
        body, name=name, grid=(T // tm,),
        in_specs=[half, half, pl.BlockSpec((tm, 512), _row(GA0 // 512)), pl.BlockSpec((tm, 512), _row(GB0 // 512)),
                  pl.BlockSpec((1, 512), _const2), pl.BlockSpec((1, 512), _const2), pl.BlockSpec((D, D), _const2),
                  pl.BlockSpec((tm, D), _row(0))],
        out_specs=pl.BlockSpec((tm, D), _row(0)), out_shape=jax.ShapeDtypeStruct((T, D), F32),
        compiler_params=_params("parallel"),
    )(ya, yb, proj, proj, na, nb, wout, h)


def _loss_head(h, gf, target, Bl, S, name):
    T = h.shape[0]
    nblk = (S + BLK) // BLK
    NB = S // BLK

    def body(h_ref, g_ref, t_ref, dh_ref, loss_ref, dg_ref):
        b, n = pl.program_id(0), pl.program_id(1)

        @pl.when((b == 0) & (n == 0))
        def _():
            loss_ref[...] = jnp.zeros_like(loss_ref)
            dg_ref[...] = jnp.zeros_like(dg_ref)

        @pl.when(n < NB)
        def _():
            x = h_ref[...]
            g = g_ref[...]
            r = _rstd(x)
            err = x * r * g - t_ref[0]
            loss_ref[...] += 0.5 * jnp.sum(jnp.mean(err * err, axis=-1, keepdims=True))
            dx, dg = _rms_bwd(err * (1.0 / D), x, r, g)
            dh_ref[...] = dx
            dg_ref[...] += dg

        @pl.when(n >= NB)
        def _():
            dh_ref[...] = jnp.zeros_like(dh_ref)

    return pl.pallas_call(
        body, name=name, grid=(Bl, nblk),
        in_specs=[pl.BlockSpec((BLK, D), lambda b, n: (b * nblk + n, 0)), pl.BlockSpec((1, D), lambda b, n: (0, 0)),
                  pl.BlockSpec((1, BLK, D), lambda b, n: (b, jnp.minimum(n, NB - 1), 0))],
        out_specs=(pl.BlockSpec((BLK, D), lambda b, n: (b * nblk + n, 0)), pl.BlockSpec((8, LANES), lambda b, n: (0, 0)),
                   pl.BlockSpec((1, D), lambda b, n: (0, 0))),
        out_shape=(jax.ShapeDtypeStruct((T, D), F32), jax.ShapeDtypeStruct((8, LANES), F32), jax.ShapeDtypeStruct((1, D), F32)),
        compiler_params=_params("arbitrary", "arbitrary"),
    )(h, gf, target)


def _out_bwd(dh, ya, yb, proj, na, nb, wout, tm, name):
    T = dh.shape[0]

    def body(dh_ref, ya_ref, yb_ref, ga_ref, gb_ref, na_ref, nb_ref, w_ref,
             dya_ref, dyb_ref, dga_ref, dgb_ref, dw_ref, dna_ref, dnb_ref):
        @pl.when(pl.program_id(0) == 0)
        def _():
            dw_ref[...] = jnp.zeros_like(dw_ref)
            dna_ref[...] = jnp.zeros_like(dna_ref)
            dnb_ref[...] = jnp.zeros_like(dnb_ref)

        dhb = dh_ref[...].astype(BF16)
        dy = _dot_nt(dhb, w_ref[...])
        parts = ((ya_ref, ga_ref, na_ref, dya_ref, dga_ref, dna_ref, 0), (yb_ref, gb_ref, nb_ref, dyb_ref, dgb_ref, dnb_ref, A_W))
        for y_ref, gate_ref, gain_ref, dy_out, dgate_out, dgain_out, c0 in parts:
            y, gate, gain = y_ref[...], gate_ref[...], gain_ref[...]
            r, nrm, sg, n_g, sl = _gated(y, gate, gain)
            dyp = dy[:, c0:c0 + 512]
            dgate_out[...] = dyp * n_g * (sg * (1.0 + gate * (1.0 - sg)))
            dx, dgain = _rms_bwd(dyp * sl, y, r, gain)
            dy_out[...] = dx
            dgain_out[...] += dgain
            dw_ref[c0:c0 + 512, :] += _dot_tn((n_g * sl).astype(BF16), dhb)

    half = pl.BlockSpec((tm, 512), _row(0))
    vec = pl.BlockSpec((1, 512), _const2)
    return pl.pallas_call(
        body, name=name, grid=(T // tm,),
        in_specs=[pl.BlockSpec((tm, D), _row(0)), half, half, pl.BlockSpec((tm, 512), _row(GA0 // 512)),
                  pl.BlockSpec((tm, 512), _row(GB0 // 512)), vec, vec, pl.BlockSpec((D, D), _const2)],
        out_specs=(half, half, half, half, pl.BlockSpec((D, D), _const2), vec, vec),
        out_shape=(jax.ShapeDtypeStruct((T, 512), F32),) * 4 + (jax.ShapeDtypeStruct((D, D), F32),)
        + (jax.ShapeDtypeStruct((1, 512), F32),) * 2,
        compiler_params=_params("arbitrary"),
    )(dh, ya, yb, proj, proj, na, nb, wout)


def _mla_bwd(q, k, v, kmask, o, do, lse, Bl, Lp, tq, name):
    T = q.shape[0]
    nq = Lp // tq

    def body(q_ref, k_ref, v_ref, km_ref, o_ref, do_ref, lse_ref, dq_ref, dk_ref, dv_ref):
        @pl.when(pl.program_id(2) == 0)
        def _():
            dk_ref[...] = jnp.zeros_like(dk_ref)
            dv_ref[...] = jnp.zeros_like(dv_ref)

        lane = _lane((tq, LANES))
        lo = lane < 64
        vb = v_ref[...]
        km = km_ref[...]
        lsev = lse_ref[...]
        dop = do_ref[...]
        op = o_ref[...]
        for hh in range(2):
            cs = slice(LANES * hh, LANES * (hh + 1))
            qh, kh = q_ref[:, cs], k_ref[:, cs]
            s = _dot_nt(qh, kh) * B_SCALE + km
            lse_h = jnp.sum(jnp.where(lane == hh, lsev, 0.0), axis=-1, keepdims=True)
            p = jnp.exp(s - lse_h)
            dom = jnp.where(lo if hh == 0 else ~lo, dop, 0.0)
            delta = jnp.sum(dom * op, axis=-1, keepdims=True)
            domb = dom.astype(BF16)
            dp = _dot_nt(domb, vb)
            ds = (p * (dp - delta) * B_SCALE).astype(BF16)
            dq_ref[:, cs] = _dot(ds, kh)
            dk_ref[:, cs] += _dot_tn(ds, qh)
            dv_ref[...] += _dot_tn(p.astype(BF16), domb)

    qrow = lambda b, j, i: (b * nq + i, j)
    kv = lambda b, j, i: (b, j)
    return pl.pallas_call(
        body, name=name, grid=(Bl, B_HEADS // 2, nq),
        in_specs=[pl.BlockSpec((tq, 256), qrow), pl.BlockSpec((Lp, 256), kv), pl.BlockSpec((Lp, LANES), kv),
                  pl.BlockSpec((1, Lp), lambda b, j, i: (0, 0)), pl.BlockSpec((tq, LANES), qrow),
                  pl.BlockSpec((tq, LANES), qrow), pl.BlockSpec((tq, LANES), qrow)],
        out_specs=(pl.BlockSpec((tq, 256), qrow), pl.BlockSpec((Lp, 256), kv), pl.BlockSpec((Lp, LANES), kv)),
        out_shape=(jax.ShapeDtypeStruct((T, 1024), F32), jax.ShapeDtypeStruct((T, 1024), F32),
                   jax.ShapeDtypeStruct((T, B_W), F32)),
        compiler_params=_params("parallel", "parallel", "arbitrary"),
    )(q, k, v, kmask, o, do, lse)


def _win_bwd(proj, bias, sink, ya, dya, lse, Bl, S, name):
    T = proj.shape[0]
    Lp = S + BLK
    NB = S // BLK
    nblk = Lp // BLK
    G = A_HEADS // A_KV

    def body(sink_ref, q_ref, k_ref, v_ref, b_ref, o_ref, do_ref, lse_ref, dq_ref, dk_ref, dv_ref, db_ref, ds_ref):
        n = pl.program_id(1)

        @pl.when(n == 0)
        def _():
            dk_ref[...] = jnp.zeros_like(dk_ref)
            dv_ref[...] = jnp.zeros_like(dv_ref)
            ds_ref[...] = jnp.zeros_like(ds_ref)

        @pl.when((n == 0) | (n == 1) | (n == NB - 1) | (n == NB))
        def _():
            db_ref[...] = jnp.zeros_like(db_ref)

        base = pl.multiple_of(_win_base(n, NB) * BLK, BLK)
        lane = _lane((BLK, LANES))
        lo = lane < 64
        lsev = lse_ref[...]
        dqs = [None] * A_HEADS
        for kvh in range(A_KV):
            cs = slice(LANES * kvh, LANES * (kvh + 1))
            k2 = jnp.concatenate([k_ref[pl.ds(base, 3 * BLK), cs], k_ref[S:S + BLK, cs]], axis=0).astype(BF16)
            v2 = jnp.concatenate([v_ref[pl.ds(base, 3 * BLK), cs], v_ref[S:S + BLK, cs]], axis=0).astype(BF16)
            dk_acc = jnp.zeros((4 * BLK, LANES), F32)
            dv_acc = jnp.zeros((4 * BLK, LANES), F32)
            for gi in range(G):
                h = kvh * G + gi
                ps = slice(LANES * (h // 2), LANES * (h // 2 + 1))
                hm = lo if h % 2 == 0 else ~lo
                qm = jnp.where(hm, q_ref[:, ps], 0.0).astype(BF16)
                s = _dot_nt(qm, k2) * A_SCALE + b_ref[0, h]
                lse_h = jnp.sum(jnp.where(lane == h, lsev, 0.0), axis=-1, keepdims=True)
                p = jnp.exp(s - lse_h)
                dom = jnp.where(hm, do_ref[:, ps], 0.0)
                delta = jnp.sum(dom * o_ref[:, ps], axis=-1, keepdims=True)
                domb = dom.astype(BF16)
                dsc = p * (_dot_nt(domb, v2) - delta)
                db_ref[0, 0, h] += dsc
                dsink = -jnp.sum(jnp.exp(sink_ref[0, h] - lse_h) * delta, axis=0, keepdims=True)
                ds_ref[0, h:h + 1, :] += jnp.broadcast_to(dsink, (1, LANES))
                dsl = (dsc * A_SCALE).astype(BF16)
                dqs[h] = _dot(dsl, k2)
                dk_acc = dk_acc + _dot_tn(dsl, qm)
                dv_acc = dv_acc + _dot_tn(p.astype(BF16), domb)
            dk_ref[pl.ds(base, 3 * BLK), cs] += dk_acc[0:3 * BLK]
            dk_ref[S:S + BLK, cs] += dk_acc[3 * BLK:4 * BLK]
            dv_ref[pl.ds(base, 3 * BLK), cs] += dv_acc[0:3 * BLK]
            dv_ref[S:S + BLK, cs] += dv_acc[3 * BLK:4 * BLK]
        for j in range(A_HEADS // 2):
            dq_ref[:, LANES * j:LANES * (j + 1)] = jnp.where(lo, dqs[2 * j], dqs[2 * j + 1])

    qrow = lambda b, n: (b * nblk + n, 0)
    kvs = pl.BlockSpec((Lp, 256), lambda b, n: (b, 0))
    return pl.pallas_call(
        body, name=name, grid=(Bl, nblk),
        in_specs=[pl.BlockSpec(memory_space=pltpu.SMEM), pl.BlockSpec((BLK, A_W), qrow),
                  pl.BlockSpec((Lp, 256), lambda b, n: (b, KA0 // 256)), pl.BlockSpec((Lp, 256), lambda b, n: (b, VA0 // 256)),
                  pl.BlockSpec((1, A_HEADS, BLK, 4 * BLK), lambda b, n: (_win_variant(n, NB), 0, 0, 0)),
                  pl.BlockSpec((BLK, A_W), qrow), pl.BlockSpec((BLK, A_W), qrow), pl.BlockSpec((BLK, LANES), qrow)],
        out_specs=(pl.BlockSpec((BLK, A_W), qrow), kvs, kvs,
                   pl.BlockSpec((1, 1, A_HEADS, BLK, 4 * BLK), lambda b, n: (b, _win_variant(n, NB), 0, 0, 0)),
                   pl.BlockSpec((1, 8, LANES), lambda b, n: (b, 0, 0))),
        out_shape=(jax.ShapeDtypeStruct((T, A_W), F32), jax.ShapeDtypeStruct((T, 256), F32),
                   jax.ShapeDtypeStruct((T, 256), F32), jax.ShapeDtypeStruct((Bl, 4, A_HEADS, BLK, 4 * BLK), F32),
                   jax.ShapeDtypeStruct((Bl, 8, LANES), F32)),
        compiler_params=_params("parallel", "arbitrary"),
    )(sink, proj, proj, proj, bias, ya, dya, lse)


def _table_grad(dbias_list, buckets, after, name):
    nl = len(dbias_list)
    Bl = dbias_list[0].shape[0]

    def body(*refs):
        d_refs = refs[:nl]
        b_ref, _, o_ref, acc, part = refs[nl:]
        v, b = pl.program_id(0), pl.program_id(1)

        @pl.when((v == 0) & (b == 0))
        def _():
            o_ref[...] = jnp.zeros_like(o_ref)

        tot = d_refs[0][0, 0]
        for r in d_refs[1:]:
            tot = tot + r[0, 0]

        @pl.when(b == 0)
        def _():
            acc[...] = tot

        @pl.when(b > 0)
        def _():
            acc[...] += tot

        @pl.when(b == Bl - 1)
        def _():
            bidx = b_ref[0]

            def step(j, c):
                mask = bidx == j
                for h in range(A_HEADS):
                    part[h, j] = jnp.sum(jnp.where(mask, acc[h], 0.0).reshape(BLK // 8, 8, 4 * BLK), axis=0)
                return c

            lax.fori_loop(0, N_BUCKETS, step, 0)
            rows = lax.broadcasted_iota(jnp.int32, (N_BUCKETS, LANES), 0)
            lanes = _lane((N_BUCKETS, LANES))
            for h in range(A_HEADS):
                col = jnp.sum(jnp.sum(part[h], axis=1), axis=-1, keepdims=True)
                o_ref[h:h + 1, :] += jnp.sum(jnp.where(rows == lanes, col, 0.0), axis=0, keepdims=True)

    return pl.pallas_call(
        body, name=name, grid=(4, Bl),
        in_specs=[pl.BlockSpec((1, 1, A_HEADS, BLK, 4 * BLK), lambda v, b: (b, v, 0, 0, 0))] * nl
        + [pl.BlockSpec((1, BLK, 4 * BLK), lambda v, b: (v, 0, 0)), pl.BlockSpec(memory_space=pl.ANY)],
        out_specs=pl.BlockSpec((8, LANES), lambda v, b: (0, 0)), out_shape=jax.ShapeDtypeStruct((8, LANES), F32),
        scratch_shapes=[pltpu.VMEM((A_HEADS, BLK, 4 * BLK), F32), pltpu.VMEM((A_HEADS, N_BUCKETS, 8, 4 * BLK), F32)],
        compiler_params=_params("arbitrary", "arbitrary"),
    )(*dbias_list, buckets, after)


def _mla_prep_bwd(dq, dk, dv, proj, gq, gkv, wq, wkk, wkv, tq, tk, tm, Lp, name):
    T = proj.shape[0]
    nt = Lp // tm

    def body(dq_ref, dk_ref, dv_ref, cq_ref, ckv_ref, gq_ref, gkv_ref, wq_ref, wkk_ref, wkv_ref, tq_ref, tk_ref,
             dcq_ref, dckv_ref, dkr_ref, dwq_ref, dwkk_ref, dwkv_ref, dgq_ref, dgkv_ref):
        @pl.when(pl.program_id(0) == 0)
        def _():
            for r in (dwq_ref, dwkk_ref, dwkv_ref, dgq_ref, dgkv_ref):
                r[...] = jnp.zeros_like(r)

        tqv = tq_ref[...]
        dqp = jnp.concatenate([_rope_t(dq_ref[:, LANES * h:LANES * (h + 1)], tqv) for h in range(B_HEADS)],
                              axis=1).astype(BF16)
        cq, gq_ = cq_ref[...], gq_ref[...]
        rq = _rstd(cq)
        dwq_ref[...] += _dot_tn((cq * rq * gq_).astype(BF16), dqp)
        dx, dg = _rms_bwd(_dot_nt(dqp, wq_ref[...]), cq, rq, gq_)
        dcq_ref[...] = dx
        dgq_ref[...] += dg

        dkv_ = dk_ref[...]
        dks = dkv_[:, 0:LANES]
        for h in range(1, B_HEADS):
            dks = dks + dkv_[:, LANES * h:LANES * (h + 1)]
        dkr_ref[...] = _rope_t(pltpu.roll(dks, NOPE, 1), tk_ref[...])
        dkb = dkv_.astype(BF16)
        dvb = dv_ref[...].astype(BF16)
        ckv, gkv_ = ckv_ref[...], gkv_ref[...]
        rk = _rstd(ckv)
        cb = (ckv * rk * gkv_).astype(BF16)
        dwkk_ref[...] += _dot_tn(cb, dkb)
        dwkv_ref[...] += _dot_tn(cb, dvb)
        dx, dg = _rms_bwd(_dot_nt(dkb, wkk_ref[...]) + _dot_nt(dvb, wkv_ref[...]), ckv, rk, gkv_)
        dckv_ref[...] = dx
        dgkv_ref[...] += dg

    tab = pl.BlockSpec((3, tm, LANES), lambda i: (0, i % nt, 0))
    wide = pl.BlockSpec((tm, 1024), _row(0))
    return pl.pallas_call(
        body, name=name, grid=(T // tm,),
        in_specs=[wide, wide, pl.BlockSpec((tm, B_W), _row(0)), pl.BlockSpec((tm, 256), _row(CQ0 // 256)),
                  pl.BlockSpec((tm, 128), _row(CKV0 // 128)), pl.BlockSpec((1, Q_RANK), _const2),
                  pl.BlockSpec((1, KV_RANK), _const2), pl.BlockSpec((Q_RANK, 1024), _const2),
                  pl.BlockSpec((KV_RANK, 1024), _const2), pl.BlockSpec((KV_RANK, B_W), _const2), tab, tab],
        out_specs=(pl.BlockSpec((tm, 256), _row(0)), pl.BlockSpec((tm, 128), _row(0)), pl.BlockSpec((tm, 128), _row(0)),
                   pl.BlockSpec((Q_RANK, 1024), _const2), pl.BlockSpec((KV_RANK, 1024), _const2),
                   pl.BlockSpec((KV_RANK, B_W), _const2), pl.BlockSpec((1, Q_RANK), _const2),
                   pl.BlockSpec((1, KV_RANK), _const2)),
        out_shape=(jax.ShapeDtypeStruct((T, 256), F32), jax.ShapeDtypeStruct((T, 128), F32),
                   jax.ShapeDtypeStruct((T, 128), F32), jax.ShapeDtypeStruct((Q_RANK, 1024), F32),
                   jax.ShapeDtypeStruct((KV_RANK, 1024), F32), jax.ShapeDtypeStruct((KV_RANK, B_W), F32),
                   jax.ShapeDtypeStruct((1, Q_RANK), F32), jax.ShapeDtypeStruct((1, KV_RANK), F32)),
        compiler_params=_params("arbitrary"),
    )(dq, dk, dv, proj, proj, gq, gkv, wq, wkk, wkv, tq, tk)


def _inproj_bwd(pieces, h, g, wx, dh_out, tm, name):
    T = h.shape[0]
    nsteps = T // tm
    CH = 512

    def body(*refs):
        p_refs = refs[:len(pieces)]
        h_ref, g_ref, w_ref, dho_ref, dh_ref, dw_hbm, dg_ref, acc, sem = refs[len(pieces):]
        i = pl.program_id(0)

        @pl.when(i == 0)
        def _():
            acc[...] = jnp.zeros_like(acc)
            dg_ref[...] = jnp.zeros_like(dg_ref)

        dp = jnp.concatenate([r[...].astype(BF16) for r in p_refs], axis=1)
        x, gain = h_ref[...], g_ref[...]
        r = _rstd(x)
        u = (x * r * gain).astype(BF16)
        for c in range(0, NX, CH):
            acc[:, c:c + CH] += _dot_tn(u, dp[:, c:c + CH])
        dx, dg = _rms_bwd(_dot_nt(dp, w_ref[...]), x, r, gain)
        dh_ref[...] = dho_ref[...] + dx
        dg_ref[...] += dg

        @pl.when(i == nsteps - 1)
        def _():
            cp = pltpu.make_async_copy(acc, dw_hbm, sem)
            cp.start()
            cp.wait()

    return pl.pallas_call(
        body, name=name, grid=(nsteps,),
        in_specs=[pl.BlockSpec((tm, p.shape[1]), _row(0)) for p in pieces]
        + [pl.BlockSpec((tm, D), _row(0)), pl.BlockSpec((1, D), _const2), pl.BlockSpec((D, NX), _const2),
           pl.BlockSpec((tm, D), _row(0))],
        out_specs=(pl.BlockSpec((tm, D), _row(0)), pl.BlockSpec(memory_space=pl.ANY), pl.BlockSpec((1, D), _const2)),
        out_shape=(jax.ShapeDtypeStruct((T, D), F32), jax.ShapeDtypeStruct((D, NX), F32), jax.ShapeDtypeStruct((1, D), F32)),
        scratch_shapes=[pltpu.VMEM((D, NX), F32), pltpu.SemaphoreType.DMA(())],
        compiler_params=_params("arbitrary"),
    )(*pieces, h, g, wx, dh_out)


def kernel(x, meta_tokens, rel_bias_table, norm_in, w_in, sink_a, norm_q_lat, w_uq, norm_kv_lat, w_ukv, norm_out_a, norm_out_b, w_out, norm_final, loss_target, m_meta_tokens, m_rel_bias_table, m_norm_in, m_w_in, m_sink_a, m_norm_q_lat, m_w_uq, m_norm_kv_lat, m_w_ukv, m_norm_out_a, m_norm_out_b, m_w_out, m_norm_final, v_meta_tokens, v_rel_bias_table, v_norm_in, v_w_in, v_sink_a, v_norm_q_lat, v_w_uq, v_norm_kv_lat, v_w_ukv, v_norm_out_a, v_norm_out_b, v_w_out, v_norm_final):
    Bl, S, _ = x.shape
    Lp = S + BLK
    T = Bl * Lp
    tm = Lp // 4
    tq = Lp // 8
    depth = w_in.shape[0]
    me = 4 * lax.axis_index("x") + 2 * lax.axis_index("y") + lax.axis_index("c")

    def finish(handles, after, name):
        srcs, lands = _xchg_wait(handles, after, name)
        full = []
        for hd, src, land in zip(handles, srcs, lands):
            own = src if hd[4] == "gather" else lax.dynamic_index_in_dim(src, me, 0, keepdims=False)
            full.append(_own_slot(land, own, me))
        return full

    def weights_of(i):
        return [w_in[i].astype(BF16), w_uq[i].astype(BF16), w_ukv[i].astype(BF16), w_out[i].astype(BF16)]

    def packed(l_in, l_uq, l_ukv, l_out):
        cols = lambda t: jnp.transpose(t, (1, 0, 2)).reshape(t.shape[1], NDEV * t.shape[2])
        wkk, wkv = _pack_w_ukv(cols(l_ukv))
        return _pack_w_in(cols(l_in)), _pack_w_uq(cols(l_uq)), wkk, wkv, l_out.reshape(D, D)

    wb = weights_of(0)
    gat_a, tok = _xchg_start([wb[0], meta_tokens], ["gather", "gather"], "gather_start_0a")
    buckets = _bias_buckets(S)
    bias = _build_bias(buckets, _after(rel_bias_table, tok), "build_bias")
    tq_tab = _rope_tables(S, Lp, NOPE, True)
    tk_tab = _rope_tables(S, Lp, 0, False)
    kmask = jnp.where(jnp.arange(Lp) < S + NMETA, 0.0, NEG).astype(F32).reshape(1, Lp)
    l_in, l_meta = finish(gat_a, bias, "gather_wait_0a")
    gat_b, tok = _xchg_start(wb[1:], ["gather"] * 3, "gather_start_0b", after=l_in)

    meta_f = jnp.transpose(l_meta, (1, 0, 2)).reshape(NMETA, D)
    tail = jnp.concatenate([meta_f, jnp.zeros((BLK - NMETA, D), F32)], axis=0)
    h = jnp.concatenate([x, jnp.broadcast_to(tail[None], (Bl, BLK, D))], axis=1).reshape(T, D)

    saved = []
    w_out_f = [None] * depth
    gat_next = None
    for i in range(depth):
        g_in, g_q, g_kv = norm_in[i][None], norm_q_lat[i][None], norm_kv_lat[i][None]
        g_a, g_b, sink = norm_out_a[i][None], norm_out_b[i][None], sink_a[i][None]
        if i == 0:
            proj = _inproj_fwd(h, _after(g_in, tok), _pack_w_in(jnp.transpose(l_in, (1, 0, 2)).reshape(D, IN_W)), tm,
                               f"inproj_fwd_{i}")
            l_uq, l_ukv, l_out = finish(gat_b, proj, "gather_wait_0b")
            wx, wq, wkk, wkv, w_out_f[i] = packed(l_in, l_uq, l_ukv, l_out)
        else:
            wx, wq, wkk, wkv, w_out_f[i] = packed(*finish(gat_next, h, f"gather_wait_{i}"))
            proj = _inproj_fwd(h, g_in, wx, tm, f"inproj_fwd_{i}")
        if i + 1 < depth:
            gat_next, tok = _xchg_start(weights_of(i + 1), ["gather"] * 4, f"gather_start_{i + 1}", after=w_out_f[i])
            g_q = _after(g_q, tok)
        q, k, v = _mla_prep_fwd(proj, g_q, g_kv, wq, wkk, wkv, tq_tab, tk_tab, tm, Lp, f"mla_prep_fwd_{i}")
        ya, lse_a = _win_fwd(proj, bias, sink, Bl, S, f"win_fwd_{i}")
        yb, lse_b = _mla_fwd(q, k, v, kmask, Bl, Lp, tq, f"mla_fwd_{i}")
        h_new = _out_fwd(ya, yb, proj, g_a, g_b, w_out_f[i], h, tm, f"out_fwd_{i}")
        saved.append((h, proj, q, k, v, ya, lse_a, yb, lse_b, wx, wq, wkk, wkv))
        h = h_new

    dh, loss_acc, dg_final = _loss_head(h, norm_final[None], loss_target, Bl, S, "loss_head")
    loss = lax.psum(loss_acc[0, 0], ("x", "y", "c"))

    g_n_in, g_sink, g_nq, g_nkv, g_na, g_nb = ([None] * depth for _ in range(6))
    dbias_all, sc_out, sc_rest = [], [None] * depth, [None] * depth
    split = lambda t, n: jnp.transpose(t.reshape(t.shape[0], NDEV, n), (1, 0, 2)).astype(BF16)
    tok = None
    for i in reversed(range(depth)):
        h_in, proj, q, k, v, ya, lse_a, yb, lse_b, wx, wq, wkk, wkv = saved[i]
        g_in, g_q, g_kv = norm_in[i][None], norm_q_lat[i][None], norm_kv_lat[i][None]
        g_a, g_b, sink = norm_out_a[i][None], norm_out_b[i][None], sink_a[i][None]
        if tok is not None:
            g_a = _after(g_a, tok)
        dya, dyb, dga, dgb, g_w_out, g_na[i], g_nb[i] = _out_bwd(dh, ya, yb, proj, g_a, g_b, w_out_f[i], tm, f"out_bwd_{i}")
        sc_out[i], tok = _xchg_start([g_w_out.reshape(NDEV, D // NDEV, D).astype(BF16)], ["scatter"], f"scatter_start_{i}a")
        dq, dk, dv = _mla_bwd(q, k, v, _after(kmask, tok), yb, dyb, lse_b, Bl, Lp, tq, f"mla_bwd_{i}")
        dqa, dka, dva, dbias, dsink = _win_bwd(proj, bias, sink, ya, dya, lse_a, Bl, S, f"win_bwd_{i}")
        dcq, dckv, dkr, dwq, dwkk, dwkv, g_nq[i], g_nkv[i] = _mla_prep_bwd(
            dq, dk, dv, proj, g_q, g_kv, wq, wkk, wkv, tq_tab, tk_tab, tm, Lp, f"mla_prep_bwd_{i}")
        dh, dwx, g_n_in[i] = _inproj_bwd((dqa, dka, dva, dga, dcq, dckv, dkr, dgb), h_in, g_in, wx, dh, tq,
                                         f"inproj_bwd_{i}")
        g_sink[i] = jnp.sum(dsink[:, :, 0], axis=0)
        dbias_all.append(dbias)
        rest = [split(_unpack_w_in_grad(dwx), IN_W // NDEV), split(_unpack_w_uq_grad(dwq), 768 // NDEV),
                split(_unpack_w_ukv_grad(dwkk, dwkv), 1024 // NDEV)]
        if i == 0:
            dh3 = dh.reshape(Bl, Lp, D)
            grad_x = dh3[:, :S]
            rest.append(jnp.transpose(jnp.sum(dh3[:, S:S + NMETA], axis=0).reshape(NMETA, NDEV, D // NDEV), (1, 0, 2)))
        sc_rest[i], tok = _xchg_start(rest, ["scatter"] * len(rest), f"scatter_start_{i}b")

    dtab = _table_grad(dbias_all, buckets, tok, "table_grad")
    two_d = lambda a: a.reshape(1, -1) if a.ndim == 1 else a
    small_w = [rel_bias_table, norm_in, sink_a, norm_q_lat, norm_kv_lat, norm_out_a, norm_out_b, norm_final]
    small_m = [m_rel_bias_table, m_norm_in, m_sink_a, m_norm_q_lat, m_norm_kv_lat, m_norm_out_a, m_norm_out_b, m_norm_final]
    small_v = [v_rel_bias_table, v_norm_in, v_sink_a, v_norm_q_lat, v_norm_kv_lat, v_norm_out_a, v_norm_out_b, v_norm_final]
    small_g = [jnp.transpose(dtab[:, :N_BUCKETS]), jnp.concatenate(g_n_in), jnp.stack(g_sink), jnp.concatenate(g_nq),
               jnp.concatenate(g_nkv), jnp.concatenate(g_na), jnp.concatenate(g_nb), dg_final]
    sc_small, tok = _xchg_start(small_g, ["gather"] * len(small_g), "scatter_start_small")

    r_out, r_in, r_uq, r_ukv = [None] * depth, [None] * depth, [None] * depth, [None] * depth
    for i in reversed(range(depth)):
        (r_out[i],) = finish(sc_out[i], tok, f"scatter_wait_{i}a")
        got = finish(sc_rest[i], tok, f"scatter_wait_{i}b")
        r_in[i], r_uq[i], r_ukv[i] = got[:3]
        if i == 0:
            r_meta = got[3]
    r_small = finish(sc_small, tok, "scatter_wait_small")

    res = {}
    sharded = dict(w_in=(r_in, w_in, m_w_in, v_w_in), w_uq=(r_uq, w_uq, m_w_uq, v_w_uq), w_ukv=(r_ukv, w_ukv, m_w_ukv, v_w_ukv),
                   w_out=(r_out, w_out, m_w_out, v_w_out),
                   meta_tokens=([r_meta], meta_tokens[None], m_meta_tokens[None], v_meta_tokens[None]))
    for n, (r, w, m, v) in sharded.items():
        outs = _adamw_param(r, w, m, v, f"adamw_{n}")
        res[n] = [o[0] for o in outs] if n == "meta_tokens" else list(outs)
    names_small = ["rel_bias_table", "norm_in", "sink_a", "norm_q_lat", "norm_kv_lat", "norm_out_a", "norm_out_b", "norm_final"]
    outs = _adamw_small(r_small, [two_d(a) for a in small_w], [two_d(a) for a in small_m], [two_d(a) for a in small_v],
                        "adamw_replicated")
    ns = len(names_small)
    for j, n in enumerate(names_small):
        res[n] = [outs[kk * ns + j].reshape(small_w[j].shape) for kk in range(4)]
    order = ["meta_tokens", "rel_bias_table", "norm_in", "w_in", "sink_a", "norm_q_lat", "w_uq", "norm_kv_lat", "w_ukv",
             "norm_out_a", "norm_out_b", "w_out", "norm_final"]
    return (loss, grad_x, *[res[n][kk] for kk in range(4) for n in order])
```

```python
import functools
import math

import numpy as np
import jax
import jax.numpy as jnp
from jax import lax
from jax.experimental import pallas as pl
from jax.experimental.pallas import tpu as pltpu

F32, BF16 = jnp.float32, jnp.bfloat16
D = 1024
NMETA = 16
BLK = 128
A_HEADS, A_KV, A_DH, A_W = 8, 2, 64, 512
B_HEADS, NOPE, ROPE, B_V, B_W = 8, 64, 32, 64, 512
Q_RANK, KV_RANK = 256, 128
IN_W = 2208
N_BUCKETS, MAX_DIST = 32, 128
THETA = 10000.0
EPS = 1e-6
NEG = -1e30
A_SCALE = A_DH ** -0.5
B_SCALE = (NOPE + ROPE) ** -0.5
LOG2E = math.log2(math.e)
LANES = 128
NDEV = 8
MESH = pl.DeviceIdType.MESH

NX = 2560
QA0, KA0, VA0, GA0, CQ0, CKV0, KR0, GB0 = 0, 512, 768, 1024, 1536, 1792, 1920, 2048

ADAM_LR, ADAM_B1, ADAM_B2, ADAM_EPS, ADAM_WD, ADAM_STEP = 0.001, 0.9, 0.999, 1e-08, 0.01, 10


def _dot(a, b):
    return jnp.dot(a, b, preferred_element_type=F32)


def _dot_nt(a, b):
    return lax.dot_general(a, b, (((1,), (1,)), ((), ())), preferred_element_type=F32)


def _dot_tn(a, b):
    return lax.dot_general(a, b, (((0,), (0,)), ((), ())), preferred_element_type=F32)


def _lane(shape):
    return lax.broadcasted_iota(jnp.int32, shape, len(shape) - 1)


def _rstd(x):
    return lax.rsqrt(jnp.mean(x * x, axis=-1, keepdims=True) + EPS)


def _rms_bwd(dn, x, r, g):
    z = dn * g
    dx = r * z - x * (r * r * r) * jnp.mean(z * x, axis=-1, keepdims=True)
    return dx, jnp.sum(dn * (x * r), axis=0, keepdims=True)


def _row(i):
    return lambda *ids: (ids[0], i)


def _const2(*ids):
    return (0, 0)


def _params(*sem):
    return pltpu.CompilerParams(dimension_semantics=sem)


def _pack_w_in(w):
    qa, ka, va, ga = w[:, 0:512], w[:, 512:640], w[:, 640:768], w[:, 768:1280]
    cq, ckv, kr, gb = w[:, 1280:1536], w[:, 1536:1664], w[:, 1664:1696], w[:, 1696:2208]
    dup = lambda t: jnp.concatenate([t[:, 0:64], t[:, 0:64], t[:, 64:128], t[:, 64:128]], axis=1)
    krp = jnp.concatenate([kr, jnp.zeros((w.shape[0], LANES - ROPE), w.dtype)], axis=1)
    return jnp.concatenate([qa, dup(ka), dup(va), ga, cq, ckv, krp, gb], axis=1)


def _unpack_w_in_grad(g):
    fold = lambda t: jnp.concatenate([t[:, 0:64] + t[:, 64:128], t[:, 128:192] + t[:, 192:256]], axis=1)
    return jnp.concatenate([g[:, QA0:QA0 + 512], fold(g[:, KA0:KA0 + 256]), fold(g[:, VA0:VA0 + 256]),
                            g[:, GA0:GA0 + 512], g[:, CQ0:CQ0 + 256], g[:, CKV0:CKV0 + 128],
                            g[:, KR0:KR0 + ROPE], g[:, GB0:GB0 + 512]], axis=1)


def _pack_w_uq(w):
    t = w.reshape(Q_RANK, B_HEADS, NOPE + ROPE)
    t = jnp.concatenate([t, jnp.zeros((Q_RANK, B_HEADS, LANES - NOPE - ROPE), w.dtype)], axis=-1)
    return t.reshape(Q_RANK, B_HEADS * LANES)


def _unpack_w_uq_grad(g):
    return g.reshape(Q_RANK, B_HEADS, LANES)[:, :, :NOPE + ROPE].reshape(Q_RANK, B_HEADS * (NOPE + ROPE))


def _pack_w_ukv(w):
    t = w.reshape(KV_RANK, B_HEADS, NOPE + B_V)
    kk = jnp.concatenate([t[:, :, :NOPE], jnp.zeros((KV_RANK, B_HEADS, LANES - NOPE), w.dtype)], axis=-1)
    return kk.reshape(KV_RANK, B_HEADS * LANES), t[:, :, NOPE:].reshape(KV_RANK, B_W)


def _unpack_w_ukv_grad(gk, gv):
    t = jnp.concatenate([gk.reshape(KV_RANK, B_HEADS, LANES)[:, :, :NOPE], gv.reshape(KV_RANK, B_HEADS, B_V)], axis=-1)
    return t.reshape(KV_RANK, B_HEADS * (NOPE + B_V))


def _t5_bucket(rel):
    nb = N_BUCKETS // 2
    max_exact = nb // 2
    ret = jnp.where(rel > 0, nb, 0)
    n = jnp.abs(rel)
    nf = jnp.maximum(n, 1).astype(F32)
    large = max_exact + (jnp.log(nf / max_exact) / math.log(MAX_DIST / max_exact) * (nb - max_exact)).astype(jnp.int32)
    large = jnp.minimum(large, nb - 1)
    return ret + jnp.where(n < max_exact, n, large)


def _bias_buckets(S):
    q = np.arange(BLK)[:, None]
    k = np.arange(4 * BLK)[None, :]
    is_meta_key = (k >= 3 * BLK) & (k < 3 * BLK + NMETA)
    mi = k - 3 * BLK
    rels, valids = [], []
    for shift in (0, BLK, 2 * BLK):
        rel_real = k - shift - q
        valid_real = (k < 3 * BLK) & (np.abs(rel_real) <= BLK)
        far = -(NMETA + MAX_DIST + BLK)
        rel_meta = (mi - (NMETA + q)) if shift == 0 else np.full_like(k + q, far)
        rels.append(np.where(is_meta_key, rel_meta, rel_real))
        valids.append(valid_real | is_meta_key)
    qm = q < NMETA
    rel_real = NMETA + k - q
    valid_real = (k < BLK) & (np.abs(rel_real) <= BLK) & qm
    rels.append(np.where(is_meta_key, mi - q, rel_real))
    valids.append(valid_real | is_meta_key)
    rel = jnp.asarray(np.stack(rels).astype(np.int32))
    valid = jnp.asarray(np.stack(valids))
    return jnp.where(valid, _t5_bucket(rel), -1).astype(jnp.int32)


def _rope_tables(S, Lp, off, passthrough):
    half = ROPE // 2
    r = np.arange(Lp)
    pos = np.where(r < S, NMETA + r, np.where(r < S + NMETA, r - S, 0)).astype(np.float32)
    freqs = THETA ** (-jnp.arange(half, dtype=F32) / half)
    ang = jnp.asarray(pos)[:, None] * freqs[None, :]
    cos, sin = jnp.cos(ang), jnp.sin(ang)
    z = lambda n: jnp.zeros((Lp, n), F32)
    head = jnp.ones((Lp, off), F32) if passthrough else z(off)
    c = jnp.concatenate([head, cos, cos, z(LANES - off - ROPE)], axis=1)
    s1 = jnp.concatenate([z(off), -sin, z(LANES - off - half)], axis=1)
    s2 = jnp.concatenate([z(off + half), sin, z(LANES - off - ROPE)], axis=1)
    return jnp.stack([c, s1, s2])


def _rope(x, t):
    return x * t[0] + pltpu.roll(x, LANES - 16, 1) * t[1] + pltpu.roll(x, 16, 1) * t[2]


def _rope_t(dy, t):
    return dy * t[0] + pltpu.roll(dy * t[1], 16, 1) + pltpu.roll(dy * t[2], LANES - 16, 1)


def _me_and_peers():
    x, y, c = lax.axis_index("x"), lax.axis_index("y"), lax.axis_index("c")
    flip = lambda v, b: 1 - v if b else v
    peers = [(flip(x, k & 4), flip(y, k & 2), flip(c, k & 1)) for k in range(1, NDEV)]
    return 4 * x + 2 * y + c, peers


def _xchg_start(srcs, kinds, name, after=None):
    n = len(srcs)
    lands = [lax.empty((NDEV,) + s.shape[-2:], s.dtype) for s in srcs]
    extra = [] if after is None else [after]

    def body(*refs):
        src_refs, land_refs = refs[:n], refs[n:2 * n]
        ssems, rsems = refs[2 * n + len(extra):3 * n + len(extra)], refs[3 * n + len(extra):4 * n + len(extra)]
        token = refs[6 * n + len(extra)]
        me, peers = _me_and_peers()
        for it in range(n):
            for k, (px, py, pc) in enumerate(peers):
                src = src_refs[it] if kinds[it] == "gather" else src_refs[it].at[4 * px + 2 * py + pc]
                pltpu.make_async_remote_copy(src_ref=src, dst_ref=land_refs[it].at[me], send_sem=ssems[it].at[k],
                                             recv_sem=rsems[it].at[k], device_id=(px, py, pc), device_id_type=MESH).start()
        token[...] = jnp.zeros_like(token)

    hbm = pl.BlockSpec(memory_space=pltpu.HBM)
    sem = pl.BlockSpec(memory_space=pltpu.SEMAPHORE)
    outs = pl.pallas_call(
        body, name=name,
        out_shape=tuple([pltpu.SemaphoreType.DMA((NDEV - 1,))] * (2 * n) + [pltpu.HBM(a.shape, a.dtype) for a in srcs + lands]
                        + [jax.ShapeDtypeStruct((8, LANES), F32)]),
        in_specs=[hbm] * (2 * n) + [pl.BlockSpec(memory_space=pl.ANY)] * len(extra),
        out_specs=tuple([sem] * (2 * n) + [hbm] * (2 * n) + [pl.BlockSpec(memory_space=pltpu.VMEM)]),
        input_output_aliases={i: 2 * n + i for i in range(2 * n)},
        compiler_params=pltpu.CompilerParams(has_side_effects=pltpu.SideEffectType.DATAFLOW_SIDE_EFFECTING),
    )(*[pltpu.with_memory_space_constraint(a, pltpu.HBM) for a in srcs + lands], *extra)
    handles = [(outs[it], outs[n + it], outs[2 * n + it], outs[3 * n + it], kinds[it]) for it in range(n)]
    return handles, outs[4 * n]


def _xchg_wait(handles, after, name):
    n = len(handles)

    def body(*refs):
        src_refs, land_refs = refs[:n], refs[n:2 * n]
        ssems, rsems = refs[2 * n:3 * n], refs[3 * n:4 * n]
        me, peers = _me_and_peers()
        for it in range(n):
            for k, (px, py, pc) in enumerate(peers):
                src = src_refs[it] if handles[it][4] == "gather" else src_refs[it].at[4 * px + 2 * py + pc]
                cp = pltpu.make_async_remote_copy(src_ref=src, dst_ref=land_refs[it].at[me], send_sem=ssems[it].at[k],
                                                  recv_sem=rsems[it].at[k], device_id=(px, py, pc), device_id_type=MESH)
                cp.wait_send()
                cp.wait_recv()

    hbm = pl.BlockSpec(memory_space=pltpu.HBM)
    sem = pl.BlockSpec(memory_space=pltpu.SEMAPHORE)
    srcs, lands = [h[2] for h in handles], [h[3] for h in handles]
    outs = pl.pallas_call(
        body, name=name, out_shape=tuple(pltpu.HBM(a.shape, a.dtype) for a in srcs + lands),
        in_specs=[hbm] * (2 * n) + [sem] * (2 * n) + [pl.BlockSpec(memory_space=pl.ANY)], out_specs=tuple([hbm] * (2 * n)),
        input_output_aliases={i: i for i in range(2 * n)},
        compiler_params=pltpu.CompilerParams(has_side_effects=pltpu.SideEffectType.DATAFLOW_SIDE_EFFECTING),
    )(*srcs, *lands, *[h[0] for h in handles], *[h[1] for h in handles], after)
    return list(outs[:n]), list(outs[n:])


def _own_slot(land, own, me):
    return lax.dynamic_update_slice(land, own[None].astype(land.dtype), (me, 0, 0))


def _after(x, token):
    return x + token[0, 0]


def _adamw_update(g, w, m, v):
    mn = ADAM_B1 * m + (1.0 - ADAM_B1) * g
    vn = ADAM_B2 * v + (1.0 - ADAM_B2) * (g * g)
    m_hat = mn / (1.0 - ADAM_B1 ** ADAM_STEP)
    v_hat = vn / (1.0 - ADAM_B2 ** ADAM_STEP)
    return -ADAM_LR * (m_hat / (jnp.sqrt(v_hat) + ADAM_EPS) + ADAM_WD * w), mn, vn


def _sum_partials(r_ref):
    g = r_ref[0].astype(F32)
    for p in range(1, NDEV):
        g = g + r_ref[p].astype(F32)
    return g


def _adamw_param(recvs, w, m, v, name):
    depth, r, c = w.shape
    tr = min(r, 256)
    nl = len(recvs)

    def body(*refs):
        r_refs = refs[:nl]
        w_ref, m_ref, v_ref, g_out, d_out, m_out, v_out = refs[nl:]
        for li in range(nl):
            @pl.when(pl.program_id(0) == li)
            def _(li=li):
                g = _sum_partials(r_refs[li])
                g_out[0] = g
                d_out[0], m_out[0], v_out[0] = _adamw_update(g, w_ref[0], m_ref[0], v_ref[0])

    blk = pl.BlockSpec((1, tr, c), lambda l, i: (l, i, 0))
    return pl.pallas_call(
        body, name=name, grid=(depth, r // tr),
        in_specs=[pl.BlockSpec((NDEV, tr, c), lambda l, i: (0, i, 0))] * nl + [blk, blk, blk],
        out_specs=(blk, blk, blk, blk), out_shape=(jax.ShapeDtypeStruct(w.shape, F32),) * 4,
        compiler_params=_params("arbitrary", "arbitrary"),
    )(*recvs, w, m, v)


def _adamw_small(recvs, ws, ms, vs, name):
    n = len(ws)

    def body(*refs):
        r_refs, w_refs, m_refs, v_refs = refs[:n], refs[n:2 * n], refs[2 * n:3 * n], refs[3 * n:4 * n]
        outs = refs[4 * n:]
        for j in range(n):
            g = _sum_partials(r_refs[j])
            outs[j][...] = g
            outs[n + j][...], outs[2 * n + j][...], outs[3 * n + j][...] = _adamw_update(
                g, w_refs[j][...], m_refs[j][...], v_refs[j][...])

    vm = pl.BlockSpec(memory_space=pltpu.VMEM)
    return pl.pallas_call(
        body, name=name, in_specs=[vm] * (4 * n), out_specs=tuple([vm] * (4 * n)),
        out_shape=tuple(jax.ShapeDtypeStruct(a.shape, F32) for a in ws) * 4,
    )(*recvs, *ws, *ms, *vs)


def _inproj_fwd(h, g, wx, tm, name):
    T = h.shape[0]

    def body(h_ref, g_ref, w_ref, o_ref):
        x = h_ref[...]
        u = (x * _rstd(x) * g_ref[...]).astype(BF16)
        o_ref[...] = _dot(u, w_ref[...])

    return pl.pallas_call(
        body, name=name, grid=(T // tm,),
        in_specs=[pl.BlockSpec((tm, D), _row(0)), pl.BlockSpec((1, D), _const2), pl.BlockSpec((D, NX), _const2)],
        out_specs=pl.BlockSpec((tm, NX), _row(0)), out_shape=jax.ShapeDtypeStruct((T, NX), F32),
        compiler_params=_params("parallel"),
    )(h, g, wx)


def _mla_prep_fwd(proj, gq, gkv, wq, wkk, wkv, tq, tk, tm, Lp, name):
    T = proj.shape[0]
    nt = Lp // tm

    def body(cq_ref, ckv_ref, kr_ref, gq_ref, gkv_ref, wq_ref, wkk_ref, wkv_ref, tq_ref, tk_ref, q_ref, k_ref, v_ref):
        cq = cq_ref[...]
        q = _dot((cq * _rstd(cq) * gq_ref[...]).astype(BF16), wq_ref[...])
        tqv = tq_ref[...]
        for h in range(B_HEADS):
            cs = slice(LANES * h, LANES * (h + 1))
            q_ref[:, cs] = (_rope(q[:, cs], tqv) * (B_SCALE * LOG2E)).astype(BF16)
        ksh = pltpu.roll(_rope(kr_ref[...], tk_ref[...]), NOPE, 1)
        ckv = ckv_ref[...]
        cb = (ckv * _rstd(ckv) * gkv_ref[...]).astype(BF16)
        kn = _dot(cb, wkk_ref[...])
        for h in range(B_HEADS):
            cs = slice(LANES * h, LANES * (h + 1))
            k_ref[:, cs] = (kn[:, cs] + ksh).astype(BF16)
        v_ref[...] = _dot(cb, wkv_ref[...]).astype(BF16)

    tab = pl.BlockSpec((3, tm, LANES), lambda i: (0, i % nt, 0))
    return pl.pallas_call(
        body, name=name, grid=(T // tm,),
        in_specs=[pl.BlockSpec((tm, 256), _row(CQ0 // 256)), pl.BlockSpec((tm, 128), _row(CKV0 // 128)),
                  pl.BlockSpec((tm, 128), _row(KR0 // 128)), pl.BlockSpec((1, Q_RANK), _const2),
                  pl.BlockSpec((1, KV_RANK), _const2), pl.BlockSpec((Q_RANK, 1024), _const2),
                  pl.BlockSpec((KV_RANK, 1024), _const2), pl.BlockSpec((KV_RANK, B_W), _const2), tab, tab],
        out_specs=(pl.BlockSpec((tm, 1024), _row(0)), pl.BlockSpec((tm, 1024), _row(0)), pl.BlockSpec((tm, B_W), _row(0))),
        out_shape=(jax.ShapeDtypeStruct((T, 1024), BF16), jax.ShapeDtypeStruct((T, 1024), BF16),
                   jax.ShapeDtypeStruct((T, B_W), BF16)),
        compiler_params=_params("parallel"),
    )(proj, proj, proj, gq, gkv, wq, wkk, wkv, tq, tk)


def _build_bias(buckets, table, name):
    def body(tab_ref, b_ref, o_ref):
        bidx = b_ref[0]
        for h in range(A_HEADS):
            def step(j, acc):
                return jnp.where(bidx == j, tab_ref[j, h], acc)
            o_ref[0, h] = lax.fori_loop(0, N_BUCKETS, step, jnp.full(bidx.shape, NEG, F32))

    return pl.pallas_call(
        body, name=name, grid=(4,),
        in_specs=[pl.BlockSpec(memory_space=pltpu.SMEM), pl.BlockSpec((1, BLK, 4 * BLK), lambda i: (i, 0, 0))],
        out_specs=pl.BlockSpec((1, A_HEADS, BLK, 4 * BLK), lambda i: (i, 0, 0, 0)),
        out_shape=jax.ShapeDtypeStruct((4, A_HEADS, BLK, 4 * BLK), F32),
        compiler_params=_params("parallel"),
    )(table, buckets)


def _win_base(n, NB):
    return jnp.where(n == NB, 0, jnp.clip(n - 1, 0, NB - 3))


def _win_variant(n, NB):
    return jnp.where(n == 0, 0, jnp.where(n < NB - 1, 1, jnp.where(n == NB - 1, 2, 3)))


def _win_fwd(proj, bias, sink, Bl, S, name):
    T = proj.shape[0]
    Lp = S + BLK
    NB = S // BLK
    nblk = Lp // BLK

    def body(sink_ref, q_ref, k_ref, v_ref, b_ref, o_ref, lse_ref):
        n = pl.program_id(1)
        base = pl.multiple_of(_win_base(n, NB) * BLK, BLK)
        lane = _lane((BLK, LANES))
        lo = lane < 64
        outs, lses = [None] * A_HEADS, [None] * A_HEADS
        for kvh in range(A_KV):
            cs = slice(LANES * kvh, LANES * (kvh + 1))
            k2 = jnp.concatenate([k_ref[pl.ds(base, 3 * BLK), cs], k_ref[S:S + BLK, cs]], axis=0).astype(BF16)
            v2 = jnp.concatenate([v_ref[pl.ds(base, 3 * BLK), cs], v_ref[S:S + BLK, cs]], axis=0).astype(BF16)
            for gi in range(A_HEADS // A_KV):
                h = kvh * (A_HEADS // A_KV) + gi
                qp = q_ref[:, LANES * (h // 2):LANES * (h // 2 + 1)]
                qm = jnp.where(lo if h % 2 == 0 else ~lo, qp, 0.0).astype(BF16)
                s = _dot_nt(qm, k2) * A_SCALE + b_ref[0, h]
                sk = sink_ref[0, h]
                m = jnp.maximum(jnp.max(s, axis=-1, keepdims=True), sk)
                e = jnp.exp(s - m)
                den = jnp.sum(e, axis=-1, keepdims=True) + jnp.exp(sk - m)
                p = e * (1.0 / den)
                outs[h] = _dot(p.astype(BF16), v2)
                lses[h] = m + jnp.log(den)
        for j in range(A_HEADS // 2):
            o_ref[:, LANES * j:LANES * (j + 1)] = jnp.where(lo, outs[2 * j], outs[2 * j + 1])
        t = jnp.zeros((BLK, LANES), F32)
        for h in range(A_HEADS):
            t = jnp.where(lane == h, lses[h], t)
        lse_ref[...] = t

    qrow = lambda b, n: (b * nblk + n, 0)
    return pl.pallas_call(
        body, name=name, grid=(Bl, nblk),
        in_specs=[pl.BlockSpec(memory_space=pltpu.SMEM), pl.BlockSpec((BLK, A_W), qrow),
                  pl.BlockSpec((Lp, 256), lambda b, n: (b, KA0 // 256)), pl.BlockSpec((Lp, 256), lambda b, n: (b, VA0 // 256)),
                  pl.BlockSpec((1, A_HEADS, BLK, 4 * BLK), lambda b, n: (_win_variant(n, NB), 0, 0, 0))],
        out_specs=(pl.BlockSpec((BLK, A_W), qrow), pl.BlockSpec((BLK, LANES), qrow)),
        out_shape=(jax.ShapeDtypeStruct((T, A_W), F32), jax.ShapeDtypeStruct((T, LANES), F32)),
        compiler_params=_params("parallel", "arbitrary"),
    )(sink, proj, proj, proj, bias)


def _key_chunks(Lp):
    S = Lp - BLK
    return [(c, 256) for c in range(0, S, 256)] + [(S, BLK)]


def _mla_fwd(q, k, v, kmask, Bl, Lp, name):
    T = q.shape[0]
    S = Lp - BLK
    chunks = _key_chunks(Lp)

    def body(q_ref, k_ref, v_ref, km_ref, o_ref, lse_ref, vt_ref):
        vt_ref[...] = v_ref[...].astype(F32).T.astype(BF16)
        lse_ref[...] = jnp.zeros_like(lse_ref)

        def qtile(r0, nq, ti):
            lane = _lane((nq, LANES))
            outs = []
            for hh in range(2):
                cs = slice(LANES * hh, LANES * (hh + 1))
                qh = q_ref[pl.ds(r0, nq), cs]
                m = jnp.full((1, nq), NEG, F32)
                l = jnp.zeros((1, nq), F32)
                acc = jnp.zeros((LANES, nq), F32)
                for k0, kc in chunks:
                    s = _dot_nt(k_ref[k0:k0 + kc, cs], qh)
                    if k0 == S:
                        s = s + km_ref[:, 0:nq]
                    m_new = jnp.maximum(m, jnp.max(s, axis=0, keepdims=True))
                    alpha = jnp.exp2(m - m_new)
                    p = jnp.exp2(s - m_new)
                    l = l * alpha + jnp.sum(p, axis=0, keepdims=True)
                    acc = acc * alpha + _dot(vt_ref[:, k0:k0 + kc], p.astype(BF16))
                    m = m_new
                outs.append((acc * (1.0 / l)).T)
                lse2 = m + jnp.log(l) * LOG2E
                for t in range(nq // LANES):
                    lse_ref[0, ti + t, hh:hh + 1, :] = lse2[:, LANES * t:LANES * (t + 1)]
            o_ref[pl.ds(r0, nq), :] = jnp.where(lane < 64, outs[0], outs[1])

        def step(i, c):
            qtile(pl.multiple_of(i * 256, 256), 256, 2 * i)
            return c

        lax.fori_loop(0, S // 256, step, 0)
        qtile(S, BLK, S // BLK)

    pair = lambda b, j: (b, j)
    return pl.pallas_call(
        body, name=name, grid=(Bl, B_HEADS // 2),
        in_specs=[pl.BlockSpec((Lp, 256), pair), pl.BlockSpec((Lp, 256), pair), pl.BlockSpec((Lp, LANES), pair),
                  pl.BlockSpec((BLK, 256), lambda b, j: (0, 0))],
        out_specs=(pl.BlockSpec((Lp, LANES), pair),
                   pl.BlockSpec((1, Lp // BLK, 8, LANES), lambda b, j: (b * (B_HEADS // 2) + j, 0, 0, 0))),
        out_shape=(jax.ShapeDtypeStruct((T, B_W), F32), jax.ShapeDtypeStruct((Bl * B_HEADS // 2, Lp // BLK, 8, LANES), F32)),
        scratch_shapes=[pltpu.VMEM((LANES, Lp), BF16)],
        compiler_params=_params("parallel", "parallel"),
    )(q, k, v, kmask)


def _gated(y, gate, gain):
    r = _rstd(y)
    nrm = y * r
    sg = jax.nn.sigmoid(gate)
    return r, nrm, sg, nrm * gain, gate * sg


def _out_fwd(ya, yb, proj, na, nb, wout, h, tm, name):
    T = h.shape[0]

    def body(ya_ref, yb_ref, ga_ref, gb_ref, na_ref, nb_ref, w_ref, h_ref, o_ref):
        _, _, _, n_a, sl_a = _gated(ya_ref[...], ga_ref[...], na_ref[...])
        _, _, _, n_b, sl_b = _gated(yb_ref[...], gb_ref[...], nb_ref[...])
        acc = _dot((n_a * sl_a).astype(BF16), w_ref[0:A_W, :]) + _dot((n_b * sl_b).astype(BF16), w_ref[A_W:A_W + B_W, :])
        o_ref[...] = h_ref[...] + acc

    half = pl.BlockSpec((tm, 512), _row(0))
    return pl.pallas_call(
        body, name=name, grid=(T // tm,),
        in_specs=[half, half, pl.BlockSpec((tm, 512), _row(GA0 // 512)), pl.BlockSpec((tm, 512), _row(GB0 // 512)),
                  pl.BlockSpec((1, 512), _const2), pl.BlockSpec((1, 512), _const2), pl.BlockSpec((D, D), _const2),
                  pl.BlockSpec((tm, D), _row(0))],
        out_specs=pl.BlockSpec((tm, D), _row(0)), out_shape=jax.ShapeDtypeStruct((T, D), F32),
        compiler_params=_params("parallel"),
    )(ya, yb, proj, proj, na, nb, wout, h)


def _loss_head(h, gf, target, Bl, S, name):
    T = h.shape[0]
    nblk = (S + BLK) // BLK
    NB = S // BLK

    def body(h_ref, g_ref, t_ref, dh_ref, loss_ref, dg_ref):
        b, n = pl.program_id(0), pl.program_id(1)

        @pl.when((b == 0) & (n == 0))
        def _():
            loss_ref[...] = jnp.zeros_like(loss_ref)
            dg_ref[...] = jnp.zeros_like(dg_ref)

        @pl.when(n < NB)
        def _():
            x = h_ref[...]
            g = g_ref[...]
            r = _rstd(x)
            err = x * r * g - t_ref[0]
            loss_ref[...] += 0.5 * jnp.sum(jnp.mean(err * err, axis=-1, keepdims=True))
            dx, dg = _rms_bwd(err * (1.0 / D), x, r, g)
            dh_ref[...] = dx
            dg_ref[...] += dg

        @pl.when(n >= NB)
        def _():
            dh_ref[...] = jnp.zeros_like(dh_ref)

    return pl.pallas_call(
        body, name=name, grid=(Bl, nblk),
        in_specs=[pl.BlockSpec((BLK, D), lambda b, n: (b * nblk + n, 0)), pl.BlockSpec((1, D), lambda b, n: (0, 0)),
                  pl.BlockSpec((1, BLK, D), lambda b, n: (b, jnp.minimum(n, NB - 1), 0))],
        out_specs=(pl.BlockSpec((BLK, D), lambda b, n: (b * nblk + n, 0)), pl.BlockSpec((8, LANES), lambda b, n: (0, 0)),
                   pl.BlockSpec((1, D), lambda b, n: (0, 0))),
        out_shape=(jax.ShapeDtypeStruct((T, D), F32), jax.ShapeDtypeStruct((8, LANES), F32), jax.ShapeDtypeStruct((1, D), F32)),
        compiler_params=_params("arbitrary", "arbitrary"),
    )(h, gf, target)


def _out_bwd(dh, ya, yb, proj, na, nb, wout, tm, name):
    T = dh.shape[0]

    def body(dh_ref, ya_ref, yb_ref, ga_ref, gb_ref, na_ref, nb_ref, w_ref,
             dya_ref, dyb_ref, dga_ref, dgb_ref, dw_ref, dna_ref, dnb_ref):
        @pl.when(pl.program_id(0) == 0)
        def _():
            dw_ref[...] = jnp.zeros_like(dw_ref)
            dna_ref[...] = jnp.zeros_like(dna_ref)
            dnb_ref[...] = jnp.zeros_like(dnb_ref)

        dhb = dh_ref[...].astype(BF16)
        dy = _dot_nt(dhb, w_ref[...])
        parts = ((ya_ref, ga_ref, na_ref, dya_ref, dga_ref, dna_ref, 0), (yb_ref, gb_ref, nb_ref, dyb_ref, dgb_ref, dnb_ref, A_W))
        for y_ref, gate_ref, gain_ref, dy_out, dgate_out, dgain_out, c0 in parts:
            y, gate, gain = y_ref[...], gate_ref[...], gain_ref[...]
            r, nrm, sg, n_g, sl = _gated(y, gate, gain)
            dyp = dy[:, c0:c0 + 512]
            dgate_out[...] = dyp * n_g * (sg * (1.0 + gate * (1.0 - sg)))
            dx, dgain = _rms_bwd(dyp * sl, y, r, gain)
            dy_out[...] = dx
            dgain_out[...] += dgain
            dw_ref[c0:c0 + 512, :] += _dot_tn((n_g * sl).astype(BF16), dhb)

    half = pl.BlockSpec((tm, 512), _row(0))
    vec = pl.BlockSpec((1, 512), _const2)
    return pl.pallas_call(
        body, name=name, grid=(T // tm,),
        in_specs=[pl.BlockSpec((tm, D), _row(0)), half, half, pl.BlockSpec((tm, 512), _row(GA0 // 512)),
                  pl.BlockSpec((tm, 512), _row(GB0 // 512)), vec, vec, pl.BlockSpec((D, D), _const2)],
        out_specs=(half, half, half, half, pl.BlockSpec((D, D), _const2), vec, vec),
        out_shape=(jax.ShapeDtypeStruct((T, 512), F32),) * 4 + (jax.ShapeDtypeStruct((D, D), F32),)
        + (jax.ShapeDtypeStruct((1, 512), F32),) * 2,
        compiler_params=_params("arbitrary"),
    )(dh, ya, yb, proj, proj, na, nb, wout)


def _mla_bwd(q, k, v, kmask, o, do, lse, Bl, Lp, name):
    T = q.shape[0]
    S = Lp - BLK
    chunks = _key_chunks(Lp)

    def body(q_ref, k_ref, v_ref, km_ref, o_ref, do_ref, lse_ref, dq_ref, dk_ref, dv_ref, kt_ref):
        kt_ref[...] = k_ref[...].astype(F32).T.astype(BF16)
        dk_ref[...] = jnp.zeros_like(dk_ref)
        dv_ref[...] = jnp.zeros_like(dv_ref)

        def qtile(r0, nq, ti):
            lane = _lane((nq, LANES))
            dop = do_ref[pl.ds(r0, nq), :]
            op = o_ref[pl.ds(r0, nq), :]
            for hh in range(2):
                cs = slice(LANES * hh, LANES * (hh + 1))
                dom = jnp.where((lane < 64) if hh == 0 else (lane >= 64), dop, 0.0)
                delta = jnp.sum(dom * op, axis=-1, keepdims=True)
                delta_row = jnp.broadcast_to(delta, (nq, LANES)).T[0:1, :]
                domb = dom.astype(BF16)
                qh = q_ref[pl.ds(r0, nq), cs]
                lse_row = jnp.concatenate([lse_ref[0, ti + t, hh:hh + 1, :] for t in range(nq // LANES)], axis=1)
                dq_t = jnp.zeros((LANES, nq), F32)
                for k0, kc in chunks:
                    s = _dot_nt(k_ref[k0:k0 + kc, cs], qh)
                    if k0 == S:
                        s = s + km_ref[:, 0:nq]
                    p = jnp.exp2(s - lse_row)
                    ds = (p * (_dot_nt(v_ref[k0:k0 + kc, :], domb) - delta_row)).astype(BF16)
                    dv_ref[k0:k0 + kc, :] += _dot(p.astype(BF16), domb)
                    dk_ref[k0:k0 + kc, cs] += _dot(ds, qh)
                    dq_t = dq_t + _dot(kt_ref[cs, k0:k0 + kc], ds)
                dq_ref[pl.ds(r0, nq), cs] = dq_t.T * B_SCALE

        def step(i, c):
            qtile(pl.multiple_of(i * 256, 256), 256, 2 * i)
            return c

        lax.fori_loop(0, S // 256, step, 0)
        qtile(S, BLK, S // BLK)
        dk_ref[...] = dk_ref[...] * (1.0 / LOG2E)

    pair = lambda b, j: (b, j)
    wide, narrow = pl.BlockSpec((Lp, 256), pair), pl.BlockSpec((Lp, LANES), pair)
    return pl.pallas_call(
        body, name=name, grid=(Bl, B_HEADS // 2),
        in_specs=[wide, wide, narrow, pl.BlockSpec((BLK, 256), lambda b, j: (0, 0)), narrow, narrow,
                  pl.BlockSpec((1, Lp // BLK, 8, LANES), lambda b, j: (b * (B_HEADS // 2) + j, 0, 0, 0))],
        out_specs=(wide, wide, narrow),
        out_shape=(jax.ShapeDtypeStruct((T, 1024), F32), jax.ShapeDtypeStruct((T, 1024), F32),
                   jax.ShapeDtypeStruct((T, B_W), F32)),
        scratch_shapes=[pltpu.VMEM((256, Lp), BF16)],
        compiler_params=_params("parallel", "parallel"),
    )(q, k, v, kmask, o, do, lse)


def _win_bwd(proj, bias, sink, ya, dya, lse, Bl, S, name):
    T = proj.shape[0]
    Lp = S + BLK
    NB = S // BLK
    nblk = Lp // BLK
    G = A_HEADS // A_KV

    def body(sink_ref, q_ref, k_ref, v_ref, b_ref, o_ref, do_ref, lse_ref, dq_ref, dk_ref, dv_ref, db_ref, ds_ref):
        n = pl.program_id(1)

        @pl.when(n == 0)
        def _():
            dk_ref[...] = jnp.zeros_like(dk_ref)
            dv_ref[...] = jnp.zeros_like(dv_ref)
            ds_ref[...] = jnp.zeros_like(ds_ref)

        @pl.when((n == 0) | (n == 1) | (n == NB - 1) | (n == NB))
        def _():
            db_ref[...] = jnp.zeros_like(db_ref)

        base = pl.multiple_of(_win_base(n, NB) * BLK, BLK)
        lane = _lane((BLK, LANES))
        lo = lane < 64
        lsev = lse_ref[...]
        dqs = [None] * A_HEADS
        for kvh in range(A_KV):
            cs = slice(LANES * kvh, LANES * (kvh + 1))
            k2 = jnp.concatenate([k_ref[pl.ds(base, 3 * BLK), cs], k_ref[S:S + BLK, cs]], axis=0).astype(BF16)
            v2 = jnp.concatenate([v_ref[pl.ds(base, 3 * BLK), cs], v_ref[S:S + BLK, cs]], axis=0).astype(BF16)
            dk_acc = jnp.zeros((4 * BLK, LANES), F32)
            dv_acc = jnp.zeros((4 * BLK, LANES), F32)
            for gi in range(G):
                h = kvh * G + gi
                ps = slice(LANES * (h // 2), LANES * (h // 2 + 1))
                hm = lo if h % 2 == 0 else ~lo
                qm = jnp.where(hm, q_ref[:, ps], 0.0).astype(BF16)
                s = _dot_nt(qm, k2) * A_SCALE + b_ref[0, h]
                lse_h = jnp.sum(jnp.where(lane == h, lsev, 0.0), axis=-1, keepdims=True)
                p = jnp.exp(s - lse_h)
                dom = jnp.where(hm, do_ref[:, ps], 0.0)
                delta = jnp.sum(dom * o_ref[:, ps], axis=-1, keepdims=True)
                domb = dom.astype(BF16)
                dsc = p * (_dot_nt(domb, v2) - delta)
                db_ref[0, 0, h] += dsc
                dsink = -jnp.sum(jnp.exp(sink_ref[0, h] - lse_h) * delta, axis=0, keepdims=True)
                ds_ref[0, h:h + 1, :] += jnp.broadcast_to(dsink, (1, LANES))
                dsl = (dsc * A_SCALE).astype(BF16)
                dqs[h] = _dot(dsl, k2)
                dk_acc = dk_acc + _dot_tn(dsl, qm)
                dv_acc = dv_acc + _dot_tn(p.astype(BF16), domb)
            dk_ref[pl.ds(base, 3 * BLK), cs] += dk_acc[0:3 * BLK]
            dk_ref[S:S + BLK, cs] += dk_acc[3 * BLK:4 * BLK]
            dv_ref[pl.ds(base, 3 * BLK), cs] += dv_acc[0:3 * BLK]
            dv_ref[S:S + BLK, cs] += dv_acc[3 * BLK:4 * BLK]
        for j in range(A_HEADS // 2):
            dq_ref[:, LANES * j:LANES * (j + 1)] = jnp.where(lo, dqs[2 * j], dqs[2 * j + 1])

    qrow = lambda b, n: (b * nblk + n, 0)
    kvs = pl.BlockSpec((Lp, 256), lambda b, n: (b, 0))
    return pl.pallas_call(
        body, name=name, grid=(Bl, nblk),
        in_specs=[pl.BlockSpec(memory_space=pltpu.SMEM), pl.BlockSpec((BLK, A_W), qrow),
                  pl.BlockSpec((Lp, 256), lambda b, n: (b, KA0 // 256)), pl.BlockSpec((Lp, 256), lambda b, n: (b, VA0 // 256)),
                  pl.BlockSpec((1, A_HEADS, BLK, 4 * BLK), lambda b, n: (_win_variant(n, NB), 0, 0, 0)),
                  pl.BlockSpec((BLK, A_W), qrow), pl.BlockSpec((BLK, A_W), qrow), pl.BlockSpec((BLK, LANES), qrow)],
        out_specs=(pl.BlockSpec((BLK, A_W), qrow), kvs, kvs,
                   pl.BlockSpec((1, 1, A_HEADS, BLK, 4 * BLK), lambda b, n: (b, _win_variant(n, NB), 0, 0, 0)),
                   pl.BlockSpec((1, 8, LANES), lambda b, n: (b, 0, 0))),
        out_shape=(jax.ShapeDtypeStruct((T, A_W), F32), jax.ShapeDtypeStruct((T, 256), F32),
                   jax.ShapeDtypeStruct((T, 256), F32), jax.ShapeDtypeStruct((Bl, 4, A_HEADS, BLK, 4 * BLK), F32),
                   jax.ShapeDtypeStruct((Bl, 8, LANES), F32)),
        compiler_params=_params("parallel", "arbitrary"),
    )(sink, proj, proj, proj, bias, ya, dya, lse)


def _table_grad(dbias_list, buckets, after, name):
    nl = len(dbias_list)
    Bl = dbias_list[0].shape[0]

    def body(*refs):
        d_refs = refs[:nl]
        b_ref, _, o_ref, acc, part = refs[nl:]
        v, b = pl.program_id(0), pl.program_id(1)

        @pl.when((v == 0) & (b == 0))
        def _():
            o_ref[...] = jnp.zeros_like(o_ref)

        tot = d_refs[0][0, 0]
        for r in d_refs[1:]:
            tot = tot + r[0, 0]

        @pl.when(b == 0)
        def _():
            acc[...] = tot

        @pl.when(b > 0)
        def _():
            acc[...] += tot

        @pl.when(b == Bl - 1)
        def _():
            bidx = b_ref[0]

            def step(j, c):
                mask = bidx == j
                for h in range(A_HEADS):
                    part[h, j] = jnp.sum(jnp.where(mask, acc[h], 0.0).reshape(BLK // 8, 8, 4 * BLK), axis=0)
                return c

            lax.fori_loop(0, N_BUCKETS, step, 0)
            rows = lax.broadcasted_iota(jnp.int32, (N_BUCKETS, LANES), 0)
            lanes = _lane((N_BUCKETS, LANES))
            for h in range(A_HEADS):
                col = jnp.sum(jnp.sum(part[h], axis=1), axis=-1, keepdims=True)
                o_ref[h:h + 1, :] += jnp.sum(jnp.where(rows == lanes, col, 0.0), axis=0, keepdims=True)

    return pl.pallas_call(
        body, name=name, grid=(4, Bl),
        in_specs=[pl.BlockSpec((1, 1, A_HEADS, BLK, 4 * BLK), lambda v, b: (b, v, 0, 0, 0))] * nl
        + [pl.BlockSpec((1, BLK, 4 * BLK), lambda v, b: (v, 0, 0)), pl.BlockSpec(memory_space=pl.ANY)],
        out_specs=pl.BlockSpec((8, LANES), lambda v, b: (0, 0)), out_shape=jax.ShapeDtypeStruct((8, LANES), F32),
        scratch_shapes=[pltpu.VMEM((A_HEADS, BLK, 4 * BLK), F32), pltpu.VMEM((A_HEADS, N_BUCKETS, 8, 4 * BLK), F32)],
        compiler_params=_params("arbitrary", "arbitrary"),
    )(*dbias_list, buckets, after)


def _mla_prep_bwd(dq, dk, dv, proj, gq, gkv, wq, wkk, wkv, tq, tk, tm, Lp, name):
    T = proj.shape[0]
    nt = Lp // tm

    def body(dq_ref, dk_ref, dv_ref, cq_ref, ckv_ref, gq_ref, gkv_ref, wq_ref, wkk_ref, wkv_ref, tq_ref, tk_ref,
             dcq_ref, dckv_ref, dkr_ref, dwq_ref, dwkk_ref, dwkv_ref, dgq_ref, dgkv_ref):
        @pl.when(pl.program_id(0) == 0)
        def _():
            for r in (dwq_ref, dwkk_ref, dwkv_ref, dgq_ref, dgkv_ref):
                r[...] = jnp.zeros_like(r)

        tqv = tq_ref[...]
        dqp = jnp.concatenate([_rope_t(dq_ref[:, LANES * h:LANES * (h + 1)], tqv) for h in range(B_HEADS)],
                              axis=1).astype(BF16)
        cq, gq_ = cq_ref[...], gq_ref[...]
        rq = _rstd(cq)
        dwq_ref[...] += _dot_tn((cq * rq * gq_).astype(BF16), dqp)
        dx, dg = _rms_bwd(_dot_nt(dqp, wq_ref[...]), cq, rq, gq_)
        dcq_ref[...] = dx
        dgq_ref[...] += dg

        dkv_ = dk_ref[...]
        dks = dkv_[:, 0:LANES]
        for h in range(1, B_HEADS):
            dks = dks + dkv_[:, LANES * h:LANES * (h + 1)]
        dkr_ref[...] = _rope_t(pltpu.roll(dks, NOPE, 1), tk_ref[...])
        dkb = dkv_.astype(BF16)
        dvb = dv_ref[...].astype(BF16)
        ckv, gkv_ = ckv_ref[...], gkv_ref[...]
        rk = _rstd(ckv)
        cb = (ckv * rk * gkv_).astype(BF16)
        dwkk_ref[...] += _dot_tn(cb, dkb)
        dwkv_ref[...] += _dot_tn(cb, dvb)
        dx, dg = _rms_bwd(_dot_nt(dkb, wkk_ref[...]) + _dot_nt(dvb, wkv_ref[...]), ckv, rk, gkv_)
        dckv_ref[...] = dx
        dgkv_ref[...] += dg

    tab = pl.BlockSpec((3, tm, LANES), lambda i: (0, i % nt, 0))
    wide = pl.BlockSpec((tm, 1024), _row(0))
    return pl.pallas_call(
        body, name=name, grid=(T // tm,),
        in_specs=[wide, wide, pl.BlockSpec((tm, B_W), _row(0)), pl.BlockSpec((tm, 256), _row(CQ0 // 256)),
                  pl.BlockSpec((tm, 128), _row(CKV0 // 128)), pl.BlockSpec((1, Q_RANK), _const2),
                  pl.BlockSpec((1, KV_RANK), _const2), pl.BlockSpec((Q_RANK, 1024), _const2),
                  pl.BlockSpec((KV_RANK, 1024), _const2), pl.BlockSpec((KV_RANK, B_W), _const2), tab, tab],
        out_specs=(pl.BlockSpec((tm, 256), _row(0)), pl.BlockSpec((tm, 128), _row(0)), pl.BlockSpec((tm, 128), _row(0)),
                   pl.BlockSpec((Q_RANK, 1024), _const2), pl.BlockSpec((KV_RANK, 1024), _const2),
                   pl.BlockSpec((KV_RANK, B_W), _const2), pl.BlockSpec((1, Q_RANK), _const2),
                   pl.BlockSpec((1, KV_RANK), _const2)),
        out_shape=(jax.ShapeDtypeStruct((T, 256), F32), jax.ShapeDtypeStruct((T, 128), F32),
                   jax.ShapeDtypeStruct((T, 128), F32), jax.ShapeDtypeStruct((Q_RANK, 1024), F32),
                   jax.ShapeDtypeStruct((KV_RANK, 1024), F32), jax.ShapeDtypeStruct((KV_RANK, B_W), F32),
                   jax.ShapeDtypeStruct((1, Q_RANK), F32), jax.ShapeDtypeStruct((1, KV_RANK), F32)),
        compiler_params=_params("arbitrary"),
    )(dq, dk, dv, proj, proj, gq, gkv, wq, wkk, wkv, tq, tk)


def _inproj_bwd(pieces, h, g, wx, dh_out, tm, name):
    T = h.shape[0]
    nsteps = T // tm
    CH = 512

    def body(*refs):
        p_refs = refs[:len(pieces)]
        h_ref, g_ref, w_ref, dho_ref, dh_ref, dw_hbm, dg_ref, acc, sem = refs[len(pieces):]
        i = pl.program_id(0)

        @pl.when(i == 0)
        def _():
            acc[...] = jnp.zeros_like(acc)
            dg_ref[...] = jnp.zeros_like(dg_ref)

        dp = jnp.concatenate([r[...].astype(BF16) for r in p_refs], axis=1)
        x, gain = h_ref[...], g_ref[...]
        r = _rstd(x)
        u = (x * r * gain).astype(BF16)
        for c in range(0, NX, CH):
            acc[:, c:c + CH] += _dot_tn(u, dp[:, c:c + CH])
        dx, dg = _rms_bwd(_dot_nt(dp, w_ref[...]), x, r, gain)
        dh_ref[...] = dho_ref[...] + dx
        dg_ref[...] += dg

        @pl.when(i == nsteps - 1)
        def _():
            cp = pltpu.make_async_copy(acc, dw_hbm, sem)
            cp.start()
            cp.wait()

    return pl.pallas_call(
        body, name=name, grid=(nsteps,),
        in_specs=[pl.BlockSpec((tm, p.shape[1]), _row(0)) for p in pieces]
        + [pl.BlockSpec((tm, D), _row(0)), pl.BlockSpec((1, D), _const2), pl.BlockSpec((D, NX), _const2),
           pl.BlockSpec((tm, D), _row(0))],
        out_specs=(pl.BlockSpec((tm, D), _row(0)), pl.BlockSpec(memory_space=pl.ANY), pl.BlockSpec((1, D), _const2)),
        out_shape=(jax.ShapeDtypeStruct((T, D), F32), jax.ShapeDtypeStruct((D, NX), F32), jax.ShapeDtypeStruct((1, D), F32)),
        scratch_shapes=[pltpu.VMEM((D, NX), F32), pltpu.SemaphoreType.DMA(())],
        compiler_params=_params("arbitrary"),
    )(*pieces, h, g, wx, dh_out)


def kernel(x, meta_tokens, rel_bias_table, norm_in, w_in, sink_a, norm_q_lat, w_uq, norm_kv_lat, w_ukv, norm_out_a, norm_out_b, w_out, norm_final, loss_target, m_meta_tokens, m_rel_bias_table, m_norm_in, m_w_in, m_sink_a, m_norm_q_lat, m_w_uq, m_norm_kv_lat, m_w_ukv, m_norm_out_a, m_norm_out_b, m_w_out, m_norm_final, v_meta_tokens, v_rel_bias_table, v_norm_in, v_w_in, v_sink_a, v_norm_q_lat, v_w_uq, v_norm_kv_lat, v_w_ukv, v_norm_out_a, v_norm_out_b, v_w_out, v_norm_final):
    Bl, S, _ = x.shape
    Lp = S + BLK
    T = Bl * Lp
    tm = Lp // 4
    tq = Lp // 8
    depth = w_in.shape[0]
    me = 4 * lax.axis_index("x") + 2 * lax.axis_index("y") + lax.axis_index("c")

    def finish(handles, after, name):
        srcs, lands = _xchg_wait(handles, after, name)
        full = []
        for hd, src, land in zip(handles, srcs, lands):
            own = src if hd[4] == "gather" else lax.dynamic_index_in_dim(src, me, 0, keepdims=False)
            full.append(_own_slot(land, own, me))
        return full

    def weights_of(i):
        return [w_in[i].astype(BF16), w_uq[i].astype(BF16), w_ukv[i].astype(BF16), w_out[i].astype(BF16)]

    def packed(l_in, l_uq, l_ukv, l_out):
        cols = lambda t: jnp.transpose(t, (1, 0, 2)).reshape(t.shape[1], NDEV * t.shape[2])
        wkk, wkv = _pack_w_ukv(cols(l_ukv))
        return _pack_w_in(cols(l_in)), _pack_w_uq(cols(l_uq)), wkk, wkv, l_out.reshape(D, D)

    wb = weights_of(0)
    gat_a, tok = _xchg_start([wb[0], meta_tokens], ["gather", "gather"], "gather_start_0a")
    buckets = _bias_buckets(S)
    bias = _build_bias(buckets, _after(rel_bias_table, tok), "build_bias")
    tq_tab = _rope_tables(S, Lp, NOPE, True)
    tk_tab = _rope_tables(S, Lp, 0, False)
    kmask = jnp.broadcast_to(jnp.where(jnp.arange(BLK) < NMETA, 0.0, NEG).astype(F32)[:, None], (BLK, 256))
    l_in, l_meta = finish(gat_a, bias, "gather_wait_0a")
    gat_b, tok = _xchg_start(wb[1:], ["gather"] * 3, "gather_start_0b", after=l_in)

    meta_f = jnp.transpose(l_meta, (1, 0, 2)).reshape(NMETA, D)
    tail = jnp.concatenate([meta_f, jnp.zeros((BLK - NMETA, D), F32)], axis=0)
    h = jnp.concatenate([x, jnp.broadcast_to(tail[None], (Bl, BLK, D))], axis=1).reshape(T, D)

    saved = []
    w_out_f = [None] * depth
    gat_next = None
    for i in range(depth):
        g_in, g_q, g_kv = norm_in[i][None], norm_q_lat[i][None], norm_kv_lat[i][None]
        g_a, g_b, sink = norm_out_a[i][None], norm_out_b[i][None], sink_a[i][None]
        if i == 0:
            proj = _inproj_fwd(h, _after(g_in, tok), _pack_w_in(jnp.transpose(l_in, (1, 0, 2)).reshape(D, IN_W)), tm,
                               f"inproj_fwd_{i}")
            l_uq, l_ukv, l_out = finish(gat_b, proj, "gather_wait_0b")
            wx, wq, wkk, wkv, w_out_f[i] = packed(l_in, l_uq, l_ukv, l_out)
        else:
            wx, wq, wkk, wkv, w_out_f[i] = packed(*finish(gat_next, h, f"gather_wait_{i}"))
            proj = _inproj_fwd(h, g_in, wx, tm, f"inproj_fwd_{i}")
        if i + 1 < depth:
            gat_next, tok = _xchg_start(weights_of(i + 1), ["gather"] * 4, f"gather_start_{i + 1}", after=w_out_f[i])
            g_q = _after(g_q, tok)
        q, k, v = _mla_prep_fwd(proj, g_q, g_kv, wq, wkk, wkv, tq_tab, tk_tab, tm, Lp, f"mla_prep_fwd_{i}")
        ya, lse_a = _win_fwd(proj, bias, sink, Bl, S, f"win_fwd_{i}")
        yb, lse_b = _mla_fwd(q, k, v, kmask, Bl, Lp, f"mla_fwd_{i}")
        h_new = _out_fwd(ya, yb, proj, g_a, g_b, w_out_f[i], h, tm, f"out_fwd_{i}")
        saved.append((h, proj, q, k, v, ya, lse_a, yb, lse_b, wx, wq, wkk, wkv))
        h = h_new

    dh, loss_acc, dg_final = _loss_head(h, norm_final[None], loss_target, Bl, S, "loss_head")
    loss = lax.psum(loss_acc[0, 0], ("x", "y", "c"))

    g_n_in, g_sink, g_nq, g_nkv, g_na, g_nb = ([None] * depth for _ in range(6))
    dbias_all, sc_out, sc_rest = [], [None] * depth, [None] * depth
    split = lambda t, n: jnp.transpose(t.reshape(t.shape[0], NDEV, n), (1, 0, 2)).astype(BF16)
    tok = None
    for i in reversed(range(depth)):
        h_in, proj, q, k, v, ya, lse_a, yb, lse_b, wx, wq, wkk, wkv = saved[i]
        g_in, g_q, g_kv = norm_in[i][None], norm_q_lat[i][None], norm_kv_lat[i][None]
        g_a, g_b, sink = norm_out_a[i][None], norm_out_b[i][None], sink_a[i][None]
        if tok is not None:
            g_a = _after(g_a, tok)
        dya, dyb, dga, dgb, g_w_out, g_na[i], g_nb[i] = _out_bwd(dh, ya, yb, proj, g_a, g_b, w_out_f[i], tm, f"out_bwd_{i}")
        sc_out[i], tok = _xchg_start([g_w_out.reshape(NDEV, D // NDEV, D).astype(BF16)], ["scatter"], f"scatter_start_{i}a")
        dq, dk, dv = _mla_bwd(q, k, v, _after(kmask, tok), yb, dyb, lse_b, Bl, Lp, f"mla_bwd_{i}")
        dqa, dka, dva, dbias, dsink = _win_bwd(proj, bias, sink, ya, dya, lse_a, Bl, S, f"win_bwd_{i}")
        dcq, dckv, dkr, dwq, dwkk, dwkv, g_nq[i], g_nkv[i] = _mla_prep_bwd(
            dq, dk, dv, proj, g_q, g_kv, wq, wkk, wkv, tq_tab, tk_tab, tm, Lp, f"mla_prep_bwd_{i}")
        dh, dwx, g_n_in[i] = _inproj_bwd((dqa, dka, dva, dga, dcq, dckv, dkr, dgb), h_in, g_in, wx, dh, tq,
                                         f"inproj_bwd_{i}")
        g_sink[i] = jnp.sum(dsink[:, :, 0], axis=0)
        dbias_all.append(dbias)
        rest = [split(_unpack_w_in_grad(dwx), IN_W // NDEV), split(_unpack_w_uq_grad(dwq), 768 // NDEV),
                split(_unpack_w_ukv_grad(dwkk, dwkv), 1024 // NDEV)]
        if i == 0:
            dh3 = dh.reshape(Bl, Lp, D)
            grad_x = dh3[:, :S]
            rest.append(jnp.transpose(jnp.sum(dh3[:, S:S + NMETA], axis=0).reshape(NMETA, NDEV, D // NDEV), (1, 0, 2)))
        sc_rest[i], tok = _xchg_start(rest, ["scatter"] * len(rest), f"scatter_start_{i}b")

    dtab = _table_grad(dbias_all, buckets, tok, "table_grad")
    two_d = lambda a: a.reshape(1, -1) if a.ndim == 1 else a
    small_w = [rel_bias_table, norm_in, sink_a, norm_q_lat, norm_kv_lat, norm_out_a, norm_out_b, norm_final]
    small_m = [m_rel_bias_table, m_norm_in, m_sink_a, m_norm_q_lat, m_norm_kv_lat, m_norm_out_a, m_norm_out_b, m_norm_final]
    small_v = [v_rel_bias_table, v_norm_in, v_sink_a, v_norm_q_lat, v_norm_kv_lat, v_norm_out_a, v_norm_out_b, v_norm_final]
    small_g = [jnp.transpose(dtab[:, :N_BUCKETS]), jnp.concatenate(g_n_in), jnp.stack(g_sink), jnp.concatenate(g_nq),
               jnp.concatenate(g_nkv), jnp.concatenate(g_na), jnp.concatenate(g_nb), dg_final]
    sc_small, tok = _xchg_start(small_g, ["gather"] * len(small_g), "scatter_start_small")

    r_out, r_in, r_uq, r_ukv = [None] * depth, [None] * depth, [None] * depth, [None] * depth
    for i in reversed(range(depth)):
        (r_out[i],) = finish(sc_out[i], tok, f"scatter_wait_{i}a")
        got = finish(sc_rest[i], tok, f"scatter_wait_{i}b")
        r_in[i], r_uq[i], r_ukv[i] = got[:3]
        if i == 0:
            r_meta = got[3]
    r_small = finish(sc_small, tok, "scatter_wait_small")

    res = {}
    sharded = dict(w_in=(r_in, w_in, m_w_in, v_w_in), w_uq=(r_uq, w_uq, m_w_uq, v_w_uq), w_ukv=(r_ukv, w_ukv, m_w_ukv, v_w_ukv),
                   w_out=(r_out, w_out, m_w_out, v_w_out),
                   meta_tokens=([r_meta], meta_tokens[None], m_meta_tokens[None], v_meta_tokens[None]))
    for n, (r, w, m, v) in sharded.items():
        outs = _adamw_param(r, w, m, v, f"adamw_{n}")
        res[n] = [o[0] for o in outs] if n == "meta_tokens" else list(outs)
    names_small = ["rel_bias_table", "norm_in", "sink_a", "norm_q_lat", "norm_kv_lat", "norm_out_a", "norm_out_b", "norm_final"]
    outs = _adamw_small(r_small, [two_d(a) for a in small_w], [two_d(a) for a in small_m], [two_d(a) for a in small_v],
                        "adamw_replicated")
    ns = len(names_small)
    for j, n in enumerate(names_small):
        res[n] = [outs[kk * ns + j].reshape(small_w[j].shape) for kk in range(4)]
    order = ["meta_tokens", "rel_bias_table", "norm_in", "w_in", "sink_a", "norm_q_lat", "w_uq", "norm_kv_lat", "w_ukv",
             "norm_out_a", "norm_out_b", "w_out", "norm_final"]
    return (loss, grad_x, *[res[n][kk] for kk in range(4) for n in order])
```

```python
import functools
import math

import numpy as np
import jax
import jax.numpy as jnp
from jax import lax
from jax.experimental import pallas as pl
from jax.experimental.pallas import tpu as pltpu

F32, BF16 = jnp.float32, jnp.bfloat16
D = 1024
NMETA = 16
BLK = 128
A_HEADS, A_KV, A_DH, A_W = 8, 2, 64, 512
B_HEADS, NOPE, ROPE, B_V, B_W = 8, 64, 32, 64, 512
Q_RANK, KV_RANK = 256, 128
IN_W = 2208
N_BUCKETS, MAX_DIST = 32, 128
THETA = 10000.0
EPS = 1e-6
NEG = -1e30
A_SCALE = A_DH ** -0.5
B_SCALE = (NOPE + ROPE) ** -0.5
LOG2E = math.log2(math.e)
MASK_LANE = NOPE + ROPE
SUM_LANE = B_V
LANES = 128
NDEV = 8
MESH = pl.DeviceIdType.MESH

NX = 2560
QA0, KA0, VA0, GA0, CQ0, CKV0, KR0, GB0 = 0, 512, 768, 1024, 1536, 1792, 1920, 2048

ADAM_LR, ADAM_B1, ADAM_B2, ADAM_EPS, ADAM_WD, ADAM_STEP = 0.001, 0.9, 0.999, 1e-08, 0.01, 10


def _dot(a, b):
    return jnp.dot(a, b, preferred_element_type=F32)


def _dot_nt(a, b):
    return lax.dot_general(a, b, (((1,), (1,)), ((), ())), preferred_element_type=F32)


def _dot_tn(a, b):
    return lax.dot_general(a, b, (((0,), (0,)), ((), ())), preferred_element_type=F32)


def _lane(shape):
    return lax.broadcasted_iota(jnp.int32, shape, len(shape) - 1)


def _rstd(x):
    return lax.rsqrt(jnp.mean(x * x, axis=-1, keepdims=True) + EPS)


def _rms_bwd(dn, x, r, g):
    z = dn * g
    dx = r * z - x * (r * r * r) * jnp.mean(z * x, axis=-1, keepdims=True)
    return dx, jnp.sum(dn * (x * r), axis=0, keepdims=True)


def _row(i):
    return lambda *ids: (ids[0], i)


def _const2(*ids):
    return (0, 0)


def _params(*sem):
    return pltpu.CompilerParams(dimension_semantics=sem)


def _pack_w_in(w):
    qa, ka, va, ga = w[:, 0:512], w[:, 512:640], w[:, 640:768], w[:, 768:1280]
    cq, ckv, kr, gb = w[:, 1280:1536], w[:, 1536:1664], w[:, 1664:1696], w[:, 1696:2208]
    dup = lambda t: jnp.concatenate([t[:, 0:64], t[:, 0:64], t[:, 64:128], t[:, 64:128]], axis=1)
    krp = jnp.concatenate([kr, jnp.zeros((w.shape[0], LANES - ROPE), w.dtype)], axis=1)
    return jnp.concatenate([qa, dup(ka), dup(va), ga, cq, ckv, krp, gb], axis=1)


def _unpack_w_in_grad(g):
    fold = lambda t: jnp.concatenate([t[:, 0:64] + t[:, 64:128], t[:, 128:192] + t[:, 192:256]], axis=1)
    return jnp.concatenate([g[:, QA0:QA0 + 512], fold(g[:, KA0:KA0 + 256]), fold(g[:, VA0:VA0 + 256]),
                            g[:, GA0:GA0 + 512], g[:, CQ0:CQ0 + 256], g[:, CKV0:CKV0 + 128],
                            g[:, KR0:KR0 + ROPE], g[:, GB0:GB0 + 512]], axis=1)


def _pack_w_uq(w):
    t = w.reshape(Q_RANK, B_HEADS, NOPE + ROPE)
    t = jnp.concatenate([t, jnp.zeros((Q_RANK, B_HEADS, LANES - NOPE - ROPE), w.dtype)], axis=-1)
    return t.reshape(Q_RANK, B_HEADS * LANES)


def _unpack_w_uq_grad(g):
    return g.reshape(Q_RANK, B_HEADS, LANES)[:, :, :NOPE + ROPE].reshape(Q_RANK, B_HEADS * (NOPE + ROPE))


def _pack_w_ukv(w):
    t = w.reshape(KV_RANK, B_HEADS, NOPE + B_V)
    z = jnp.zeros((KV_RANK, B_HEADS, LANES - NOPE), w.dtype)
    pad = lambda u: jnp.concatenate([u, z], axis=-1).reshape(KV_RANK, B_HEADS * LANES)
    return pad(t[:, :, :NOPE]), pad(t[:, :, NOPE:])


def _unpack_w_ukv_grad(gk, gv):
    head = lambda g: g.reshape(KV_RANK, B_HEADS, LANES)[:, :, :NOPE]
    return jnp.concatenate([head(gk), head(gv)], axis=-1).reshape(KV_RANK, B_HEADS * (NOPE + B_V))


def _t5_bucket(rel):
    nb = N_BUCKETS // 2
    max_exact = nb // 2
    ret = jnp.where(rel > 0, nb, 0)
    n = jnp.abs(rel)
    nf = jnp.maximum(n, 1).astype(F32)
    large = max_exact + (jnp.log(nf / max_exact) / math.log(MAX_DIST / max_exact) * (nb - max_exact)).astype(jnp.int32)
    large = jnp.minimum(large, nb - 1)
    return ret + jnp.where(n < max_exact, n, large)


def _bias_buckets(S):
    q = np.arange(BLK)[:, None]
    k = np.arange(4 * BLK)[None, :]
    is_meta_key = (k >= 3 * BLK) & (k < 3 * BLK + NMETA)
    mi = k - 3 * BLK
    rels, valids = [], []
    for shift in (0, BLK, 2 * BLK):
        rel_real = k - shift - q
        valid_real = (k < 3 * BLK) & (np.abs(rel_real) <= BLK)
        far = -(NMETA + MAX_DIST + BLK)
        rel_meta = (mi - (NMETA + q)) if shift == 0 else np.full_like(k + q, far)
        rels.append(np.where(is_meta_key, rel_meta, rel_real))
        valids.append(valid_real | is_meta_key)
    qm = q < NMETA
    rel_real = NMETA + k - q
    valid_real = (k < BLK) & (np.abs(rel_real) <= BLK) & qm
    rels.append(np.where(is_meta_key, mi - q, rel_real))
    valids.append(valid_real | is_meta_key)
    rel = jnp.asarray(np.stack(rels).astype(np.int32))
    valid = jnp.asarray(np.stack(valids))
    return jnp.where(valid, _t5_bucket(rel), -1).astype(jnp.int32)


def _rope_tables(S, Lp, off, passthrough):
    half = ROPE // 2
    r = np.arange(Lp)
    pos = np.where(r < S, NMETA + r, np.where(r < S + NMETA, r - S, 0)).astype(np.float32)
    freqs = THETA ** (-jnp.arange(half, dtype=F32) / half)
    ang = jnp.asarray(pos)[:, None] * freqs[None, :]
    cos, sin = jnp.cos(ang), jnp.sin(ang)
    z = lambda n: jnp.zeros((Lp, n), F32)
    head = jnp.ones((Lp, off), F32) if passthrough else z(off)
    c = jnp.concatenate([head, cos, cos, z(LANES - off - ROPE)], axis=1)
    s1 = jnp.concatenate([z(off), -sin, z(LANES - off - half)], axis=1)
    s2 = jnp.concatenate([z(off + half), sin, z(LANES - off - ROPE)], axis=1)
    return jnp.stack([c, s1, s2])


def _rope(x, t):
    return x * t[0] + pltpu.roll(x, LANES - 16, 1) * t[1] + pltpu.roll(x, 16, 1) * t[2]


def _rope_t(dy, t):
    return dy * t[0] + pltpu.roll(dy * t[1], 16, 1) + pltpu.roll(dy * t[2], LANES - 16, 1)


def _me_and_peers():
    x, y, c = lax.axis_index("x"), lax.axis_index("y"), lax.axis_index("c")
    flip = lambda v, b: 1 - v if b else v
    peers = [(flip(x, k & 4), flip(y, k & 2), flip(c, k & 1)) for k in range(1, NDEV)]
    return 4 * x + 2 * y + c, peers


def _xchg_start(srcs, kinds, name, after=None):
    n = len(srcs)
    lands = [lax.empty((NDEV,) + s.shape[-2:], s.dtype) for s in srcs]
    extra = [] if after is None else [after]

    def body(*refs):
        src_refs, land_refs = refs[:n], refs[n:2 * n]
        ssems, rsems = refs[2 * n + len(extra):3 * n + len(extra)], refs[3 * n + len(extra):4 * n + len(extra)]
        token = refs[6 * n + len(extra)]
        me, peers = _me_and_peers()
        for it in range(n):
            for k, (px, py, pc) in enumerate(peers):
                src = src_refs[it] if kinds[it] == "gather" else src_refs[it].at[4 * px + 2 * py + pc]
                pltpu.make_async_remote_copy(src_ref=src, dst_ref=land_refs[it].at[me], send_sem=ssems[it].at[k],
                                             recv_sem=rsems[it].at[k], device_id=(px, py, pc), device_id_type=MESH).start()
        token[...] = jnp.zeros_like(token)

    hbm = pl.BlockSpec(memory_space=pltpu.HBM)
    sem = pl.BlockSpec(memory_space=pltpu.SEMAPHORE)
    outs = pl.pallas_call(
        body, name=name,
        out_shape=tuple([pltpu.SemaphoreType.DMA((NDEV - 1,))] * (2 * n) + [pltpu.HBM(a.shape, a.dtype) for a in srcs + lands]
                        + [jax.ShapeDtypeStruct((8, LANES), F32)]),
        in_specs=[hbm] * (2 * n) + [pl.BlockSpec(memory_space=pl.ANY)] * len(extra),
        out_specs=tuple([sem] * (2 * n) + [hbm] * (2 * n) + [pl.BlockSpec(memory_space=pltpu.VMEM)]),
        input_output_aliases={i: 2 * n + i for i in range(2 * n)},
        compiler_params=pltpu.CompilerParams(has_side_effects=pltpu.SideEffectType.DATAFLOW_SIDE_EFFECTING),
    )(*[pltpu.with_memory_space_constraint(a, pltpu.HBM) for a in srcs + lands], *extra)
    handles = [(outs[it], outs[n + it], outs[2 * n + it], outs[3 * n + it], kinds[it]) for it in range(n)]
    return handles, outs[4 * n]


def _xchg_wait(handles, after, name):
    n = len(handles)

    def body(*refs):
        src_refs, land_refs = refs[:n], refs[n:2 * n]
        ssems, rsems = refs[2 * n:3 * n], refs[3 * n:4 * n]
        me, peers = _me_and_peers()
        for it in range(n):
            for k, (px, py, pc) in enumerate(peers):
                src = src_refs[it] if handles[it][4] == "gather" else src_refs[it].at[4 * px + 2 * py + pc]
                cp = pltpu.make_async_remote_copy(src_ref=src, dst_ref=land_refs[it].at[me], send_sem=ssems[it].at[k],
                                                  recv_sem=rsems[it].at[k], device_id=(px, py, pc), device_id_type=MESH)
                cp.wait_send()
                cp.wait_recv()

    hbm = pl.BlockSpec(memory_space=pltpu.HBM)
    sem = pl.BlockSpec(memory_space=pltpu.SEMAPHORE)
    srcs, lands = [h[2] for h in handles], [h[3] for h in handles]
    outs = pl.pallas_call(
        body, name=name, out_shape=tuple(pltpu.HBM(a.shape, a.dtype) for a in srcs + lands),
        in_specs=[hbm] * (2 * n) + [sem] * (2 * n) + [pl.BlockSpec(memory_space=pl.ANY)], out_specs=tuple([hbm] * (2 * n)),
        input_output_aliases={i: i for i in range(2 * n)},
        compiler_params=pltpu.CompilerParams(has_side_effects=pltpu.SideEffectType.DATAFLOW_SIDE_EFFECTING),
    )(*srcs, *lands, *[h[0] for h in handles], *[h[1] for h in handles], after)
    return list(outs[:n]), list(outs[n:])


def _own_slot(land, own, me):
    return lax.dynamic_update_slice(land, own[None].astype(land.dtype), (me, 0, 0))


def _after(x, token):
    return x + token[0, 0]


def _adamw_update(g, w, m, v):
    mn = ADAM_B1 * m + (1.0 - ADAM_B1) * g
    vn = ADAM_B2 * v + (1.0 - ADAM_B2) * (g * g)
    m_hat = mn / (1.0 - ADAM_B1 ** ADAM_STEP)
    v_hat = vn / (1.0 - ADAM_B2 ** ADAM_STEP)
    return -ADAM_LR * (m_hat / (jnp.sqrt(v_hat) + ADAM_EPS) + ADAM_WD * w), mn, vn


def _sum_partials(r_ref):
    g = r_ref[0].astype(F32)
    for p in range(1, NDEV):
        g = g + r_ref[p].astype(F32)
    return g


def _adamw_param(recvs, w, m, v, name):
    depth, r, c = w.shape
    tr = min(r, 256)
    nl = len(recvs)

    def body(*refs):
        r_refs = refs[:nl]
        w_ref, m_ref, v_ref, g_out, d_out, m_out, v_out = refs[nl:]
        for li in range(nl):
            @pl.when(pl.program_id(0) == li)
            def _(li=li):
                g = _sum_partials(r_refs[li])
                g_out[0] = g
                d_out[0], m_out[0], v_out[0] = _adamw_update(g, w_ref[0], m_ref[0], v_ref[0])

    blk = pl.BlockSpec((1, tr, c), lambda l, i: (l, i, 0))
    return pl.pallas_call(
        body, name=name, grid=(depth, r // tr),
        in_specs=[pl.BlockSpec((NDEV, tr, c), lambda l, i: (0, i, 0))] * nl + [blk, blk, blk],
        out_specs=(blk, blk, blk, blk), out_shape=(jax.ShapeDtypeStruct(w.shape, F32),) * 4,
        compiler_params=_params("arbitrary", "arbitrary"),
    )(*recvs, w, m, v)


def _adamw_small(recvs, ws, ms, vs, name):
    n = len(ws)

    def body(*refs):
        r_refs, w_refs, m_refs, v_refs = refs[:n], refs[n:2 * n], refs[2 * n:3 * n], refs[3 * n:4 * n]
        outs = refs[4 * n:]
        for j in range(n):
            g = _sum_partials(r_refs[j])
            outs[j][...] = g
            outs[n + j][...], outs[2 * n + j][...], outs[3 * n + j][...] = _adamw_update(
                g, w_refs[j][...], m_refs[j][...], v_refs[j][...])

    vm = pl.BlockSpec(memory_space=pltpu.VMEM)
    return pl.pallas_call(
        body, name=name, in_specs=[vm] * (4 * n), out_specs=tuple([vm] * (4 * n)),
        out_shape=tuple(jax.ShapeDtypeStruct(a.shape, F32) for a in ws) * 4,
    )(*recvs, *ws, *ms, *vs)


def _inproj_fwd(h, g, wx, tm, name):
    T = h.shape[0]

    def body(h_ref, g_ref, w_ref, o_ref):
        x = h_ref[...]
        u = (x * _rstd(x) * g_ref[...]).astype(BF16)
        o_ref[...] = _dot(u, w_ref[...])

    return pl.pallas_call(
        body, name=name, grid=(T // tm,),
        in_specs=[pl.BlockSpec((tm, D), _row(0)), pl.BlockSpec((1, D), _const2), pl.BlockSpec((D, NX), _const2)],
        out_specs=pl.BlockSpec((tm, NX), _row(0)), out_shape=jax.ShapeDtypeStruct((T, NX), F32),
        compiler_params=_params("parallel"),
    )(h, g, wx)


def _mla_prep_fwd(proj, gq, gkv, wq, wkk, wkv, tq, tk, tm, Lp, name):
    T = proj.shape[0]
    nt = Lp // tm

    def body(cq_ref, ckv_ref, kr_ref, gq_ref, gkv_ref, wq_ref, wkk_ref, wkv_ref, tq_ref, tk_ref, q_ref, k_ref, v_ref):
        cq = cq_ref[...]
        q = _dot((cq * _rstd(cq) * gq_ref[...]).astype(BF16), wq_ref[...])
        tqv = tq_ref[...]
        lane = _lane((1, LANES))
        one = jnp.where(lane == MASK_LANE, 1.0, 0.0)
        for h in range(B_HEADS):
            cs = slice(LANES * h, LANES * (h + 1))
            q_ref[:, cs] = (_rope(q[:, cs], tqv) * (B_SCALE * LOG2E) + one).astype(BF16)
        tkv = tk_ref[...]
        ksh = pltpu.roll(_rope(kr_ref[...], tkv), NOPE, 1) + tkv[3]
        ckv = ckv_ref[...]
        cb = (ckv * _rstd(ckv) * gkv_ref[...]).astype(BF16)
        kn = _dot(cb, wkk_ref[...])
        vn = _dot(cb, wkv_ref[...])
        minus = jnp.where((lane == SUM_LANE) | (lane == SUM_LANE + 1), -1.0, 0.0)
        for h in range(B_HEADS):
            cs = slice(LANES * h, LANES * (h + 1))
            k_ref[:, cs] = (kn[:, cs] + ksh).astype(BF16)
            v_ref[:, cs] = (vn[:, cs] + minus).astype(BF16)

    tab = lambda n: pl.BlockSpec((n, tm, LANES), lambda i: (0, i % nt, 0))
    wide = pl.BlockSpec((tm, 1024), _row(0))
    return pl.pallas_call(
        body, name=name, grid=(T // tm,),
        in_specs=[pl.BlockSpec((tm, 256), _row(CQ0 // 256)), pl.BlockSpec((tm, 128), _row(CKV0 // 128)),
                  pl.BlockSpec((tm, 128), _row(KR0 // 128)), pl.BlockSpec((1, Q_RANK), _const2),
                  pl.BlockSpec((1, KV_RANK), _const2), pl.BlockSpec((Q_RANK, 1024), _const2),
                  pl.BlockSpec((KV_RANK, 1024), _const2), pl.BlockSpec((KV_RANK, 1024), _const2), tab(3), tab(4)],
        out_specs=(wide, wide, wide), out_shape=(jax.ShapeDtypeStruct((T, 1024), BF16),) * 3,
        compiler_params=_params("parallel"),
    )(proj, proj, proj, gq, gkv, wq, wkk, wkv, tq, tk)


def _build_bias(buckets, table, name):
    def body(tab_ref, b_ref, o_ref):
        bidx = b_ref[0]
        for h in range(A_HEADS):
            def step(j, acc):
                return jnp.where(bidx == j, tab_ref[j, h], acc)
            o_ref[0, h] = lax.fori_loop(0, N_BUCKETS, step, jnp.full(bidx.shape, NEG, F32))

    return pl.pallas_call(
        body, name=name, grid=(4,),
        in_specs=[pl.BlockSpec(memory_space=pltpu.SMEM), pl.BlockSpec((1, BLK, 4 * BLK), lambda i: (i, 0, 0))],
        out_specs=pl.BlockSpec((1, A_HEADS, BLK, 4 * BLK), lambda i: (i, 0, 0, 0)),
        out_shape=jax.ShapeDtypeStruct((4, A_HEADS, BLK, 4 * BLK), F32),
        compiler_params=_params("parallel"),
    )(table, buckets)


def _win_base(n, NB):
    return jnp.where(n == NB, 0, jnp.clip(n - 1, 0, NB - 3))


def _win_variant(n, NB):
    return jnp.where(n == 0, 0, jnp.where(n < NB - 1, 1, jnp.where(n == NB - 1, 2, 3)))


def _win_fwd(proj, bias, sink, Bl, S, name):
    T = proj.shape[0]
    Lp = S + BLK
    NB = S // BLK
    nblk = Lp // BLK

    def body(sink_ref, q_ref, k_ref, v_ref, b_ref, o_ref, lse_ref):
        n = pl.program_id(1)
        base = pl.multiple_of(_win_base(n, NB) * BLK, BLK)
        lane = _lane((BLK, LANES))
        lo = lane < 64
        outs, lses = [None] * A_HEADS, [None] * A_HEADS
        for kvh in range(A_KV):
            cs = slice(LANES * kvh, LANES * (kvh + 1))
            k2 = jnp.concatenate([k_ref[pl.ds(base, 3 * BLK), cs], k_ref[S:S + BLK, cs]], axis=0).astype(BF16)
            v2 = jnp.concatenate([v_ref[pl.ds(base, 3 * BLK), cs], v_ref[S:S + BLK, cs]], axis=0).astype(BF16)
            for gi in range(A_HEADS // A_KV):
                h = kvh * (A_HEADS // A_KV) + gi
                qp = q_ref[:, LANES * (h // 2):LANES * (h // 2 + 1)]
                qm = jnp.where(lo if h % 2 == 0 else ~lo, qp, 0.0).astype(BF16)
                s = _dot_nt(qm, k2) * A_SCALE + b_ref[0, h]
                sk = sink_ref[0, h]
                m = jnp.maximum(jnp.max(s, axis=-1, keepdims=True), sk)
                e = jnp.exp(s - m)
                den = jnp.sum(e, axis=-1, keepdims=True) + jnp.exp(sk - m)
                p = e * (1.0 / den)
                outs[h] = _dot(p.astype(BF16), v2)
                lses[h] = m + jnp.log(den)
        for j in range(A_HEADS // 2):
            o_ref[:, LANES * j:LANES * (j + 1)] = jnp.where(lo, outs[2 * j], outs[2 * j + 1])
        t = jnp.zeros((BLK, LANES), F32)
        for h in range(A_HEADS):
            t = jnp.where(lane == h, lses[h], t)
        lse_ref[...] = t

    qrow = lambda b, n: (b * nblk + n, 0)
    return pl.pallas_call(
        body, name=name, grid=(Bl, nblk),
        in_specs=[pl.BlockSpec(memory_space=pltpu.SMEM), pl.BlockSpec((BLK, A_W), qrow),
                  pl.BlockSpec((Lp, 256), lambda b, n: (b, KA0 // 256)), pl.BlockSpec((Lp, 256), lambda b, n: (b, VA0 // 256)),
                  pl.BlockSpec((1, A_HEADS, BLK, 4 * BLK), lambda b, n: (_win_variant(n, NB), 0, 0, 0))],
        out_specs=(pl.BlockSpec((BLK, A_W), qrow), pl.BlockSpec((BLK, LANES), qrow)),
        out_shape=(jax.ShapeDtypeStruct((T, A_W), F32), jax.ShapeDtypeStruct((T, LANES), F32)),
        compiler_params=_params("parallel", "arbitrary"),
    )(sink, proj, proj, proj, bias)


def _q_tiles(Lp):
    S = Lp - BLK
    return S // 256, S


def _mla_fwd(q, k, v, Bl, Lp, name):
    T = q.shape[0]
    n_big, S = _q_tiles(Lp)

    def body(q_ref, k_ref, v_ref, o_ref, lse_ref):
        def qtile(r0, nq):
            lane = _lane((nq, LANES))
            outs = []
            t = jnp.zeros((nq, LANES), F32)
            cols = [slice(LANES * hh, LANES * (hh + 1)) for hh in range(2)]
            scores = [_dot_nt(q_ref[pl.ds(r0, nq), cs], k_ref[:, cs]) for cs in cols]
            for hh in range(2):
                cs, s = cols[hh], scores[hh]
                m = jnp.max(s, axis=-1, keepdims=True)
                o2 = _dot(jnp.exp2(s - m).astype(BF16), v_ref[:, cs])
                l = -jnp.sum(jnp.where(lane == SUM_LANE, o2, 0.0), axis=-1, keepdims=True)
                outs.append(o2 * (1.0 / l))
                t = jnp.where(lane == hh, m + jnp.log(l) * LOG2E, t)
            o_ref[pl.ds(r0, nq), :] = jnp.where(lane < B_V, outs[0], pltpu.roll(outs[1], B_V, 1))
            lse_ref[pl.ds(r0, nq), :] = t

        def step(i, c):
            qtile(pl.multiple_of(i * 256, 256), 256)
            return c

        lax.fori_loop(0, n_big, step, 0)
        qtile(S, BLK)

    pair = lambda b, j: (b, j)
    wide, narrow = pl.BlockSpec((Lp, 256), pair), pl.BlockSpec((Lp, LANES), pair)
    return pl.pallas_call(
        body, name=name, grid=(Bl, B_HEADS // 2), in_specs=[wide, wide, wide], out_specs=(narrow, narrow),
        out_shape=(jax.ShapeDtypeStruct((T, B_W), F32), jax.ShapeDtypeStruct((T, B_W), F32)),
        compiler_params=_params("parallel", "parallel"),
    )(q, k, v)


def _gated(y, gate, gain):
    r = _rstd(y)
    nrm = y * r
    sg = jax.nn.sigmoid(gate)
    return r, nrm, sg, nrm * gain, gate * sg


def _out_fwd(ya, yb, proj, na, nb, wout, h, tm, name):
    T = h.shape[0]

    def body(ya_ref, yb_ref, ga_ref, gb_ref, na_ref, nb_ref, w_ref, h_ref, o_ref):
        _, _, _, n_a, sl_a = _gated(ya_ref[...], ga_ref[...], na_ref[...])
        _, _, _, n_b, sl_b = _gated(yb_ref[...], gb_ref[...], nb_ref[...])
        acc = _dot((n_a * sl_a).astype(BF16), w_ref[0:A_W, :]) + _dot((n_b * sl_b).astype(BF16), w_ref[A_W:A_W + B_W, :])
        o_ref[...] = h_ref[...] + acc

    half = pl.BlockSpec((tm, 512), _row(0))
    return pl.pallas_call(
        body, name=name, grid=(T // tm,),
        in_specs=[half, half, pl.BlockSpec((tm, 512), _row(GA0 // 512)), pl.BlockSpec((tm, 512), _row(GB0 // 512)),
                  pl.BlockSpec((1, 512), _const2), pl.BlockSpec((1, 512), _const2), pl.BlockSpec((D, D), _const2),
                  pl.BlockSpec((tm, D), _row(0))],
        out_specs=pl.BlockSpec((tm, D), _row(0)), out_shape=jax.ShapeDtypeStruct((T, D), F32),
        compiler_params=_params("parallel"),
    )(ya, yb, proj, proj, na, nb, wout, h)


def _loss_head(h, gf, target, Bl, S, name):
    T = h.shape[0]
    nblk = (S + BLK) // BLK
    NB = S // BLK

    def body(h_ref, g_ref, t_ref, dh_ref, loss_ref, dg_ref):
        b, n = pl.program_id(0), pl.program_id(1)

        @pl.when((b == 0) & (n == 0))
        def _():
            loss_ref[...] = jnp.zeros_like(loss_ref)
            dg_ref[...] = jnp.zeros_like(dg_ref)

        @pl.when(n < NB)
        def _():
            x = h_ref[...]
            g = g_ref[...]
            r = _rstd(x)
            err = x * r * g - t_ref[0]
            loss_ref[...] += 0.5 * jnp.sum(jnp.mean(err * err, axis=-1, keepdims=True))
            dx, dg = _rms_bwd(err * (1.0 / D), x, r, g)
            dh_ref[...] = dx
            dg_ref[...] += dg

        @pl.when(n >= NB)
        def _():
            dh_ref[...] = jnp.zeros_like(dh_ref)

    return pl.pallas_call(
        body, name=name, grid=(Bl, nblk),
        in_specs=[pl.BlockSpec((BLK, D), lambda b, n: (b * nblk + n, 0)), pl.BlockSpec((1, D), lambda b, n: (0, 0)),
                  pl.BlockSpec((1, BLK, D), lambda b, n: (b, jnp.minimum(n, NB - 1), 0))],
        out_specs=(pl.BlockSpec((BLK, D), lambda b, n: (b * nblk + n, 0)), pl.BlockSpec((8, LANES), lambda b, n: (0, 0)),
                   pl.BlockSpec((1, D), lambda b, n: (0, 0))),
        out_shape=(jax.ShapeDtypeStruct((T, D), F32), jax.ShapeDtypeStruct((8, LANES), F32), jax.ShapeDtypeStruct((1, D), F32)),
        compiler_params=_params("arbitrary", "arbitrary"),
    )(h, gf, target)


def _out_bwd(dh, ya, yb, proj, na, nb, wout, tm, name):
    T = dh.shape[0]

    def body(dh_ref, ya_ref, yb_ref, ga_ref, gb_ref, na_ref, nb_ref, w_ref,
             dya_ref, dyb_ref, dga_ref, dgb_ref, dw_ref, dna_ref, dnb_ref):
        @pl.when(pl.program_id(0) == 0)
        def _():
            dw_ref[...] = jnp.zeros_like(dw_ref)
            dna_ref[...] = jnp.zeros_like(dna_ref)
            dnb_ref[...] = jnp.zeros_like(dnb_ref)

        dhb = dh_ref[...].astype(BF16)
        dy = _dot_nt(dhb, w_ref[...])
        parts = ((ya_ref, ga_ref, na_ref, dya_ref, dga_ref, dna_ref, 0), (yb_ref, gb_ref, nb_ref, dyb_ref, dgb_ref, dnb_ref, A_W))
        for y_ref, gate_ref, gain_ref, dy_out, dgate_out, dgain_out, c0 in parts:
            y, gate, gain = y_ref[...], gate_ref[...], gain_ref[...]
            r, nrm, sg, n_g, sl = _gated(y, gate, gain)
            dyp = dy[:, c0:c0 + 512]
            dgate_out[...] = dyp * n_g * (sg * (1.0 + gate * (1.0 - sg)))
            dx, dgain = _rms_bwd(dyp * sl, y, r, gain)
            dy_out[...] = dx
            dgain_out[...] += dgain
            dw_ref[c0:c0 + 512, :] += _dot_tn((n_g * sl).astype(BF16), dhb)

    half = pl.BlockSpec((tm, 512), _row(0))
    vec = pl.BlockSpec((1, 512), _const2)
    return pl.pallas_call(
        body, name=name, grid=(T // tm,),
        in_specs=[pl.BlockSpec((tm, D), _row(0)), half, half, pl.BlockSpec((tm, 512), _row(GA0 // 512)),
                  pl.BlockSpec((tm, 512), _row(GB0 // 512)), vec, vec, pl.BlockSpec((D, D), _const2)],
        out_specs=(half, half, half, half, pl.BlockSpec((D, D), _const2), vec, vec),
        out_shape=(jax.ShapeDtypeStruct((T, 512), F32),) * 4 + (jax.ShapeDtypeStruct((D, D), F32),)
        + (jax.ShapeDtypeStruct((1, 512), F32),) * 2,
        compiler_params=_params("arbitrary"),
    )(dh, ya, yb, proj, proj, na, nb, wout)


def _mla_bwd(q, k, v, o, do, lse, after, Bl, Lp, name):
    T = q.shape[0]
    n_big, S = _q_tiles(Lp)

    def body(q_ref, k_ref, v_ref, o_ref, do_ref, lse_ref, _, dq_ref, dk_ref, dv_ref):
        dk_ref[...] = jnp.zeros_like(dk_ref)
        dv_ref[...] = jnp.zeros_like(dv_ref)

        def qtile(r0, nq):
            lane = _lane((nq, LANES))
            dop = do_ref[pl.ds(r0, nq), :]
            op = o_ref[pl.ds(r0, nq), :]
            lsev = lse_ref[pl.ds(r0, nq), :]
            for hh in range(2):
                cs = slice(LANES * hh, LANES * (hh + 1))
                dom = jnp.where((lane < B_V) if hh == 0 else (lane >= B_V), dop, 0.0)
                delta = jnp.sum(dom * op, axis=-1, keepdims=True)
                d_hi = delta.astype(BF16).astype(F32)
                x = dom if hh == 0 else pltpu.roll(dom, B_V, 1)
                domx = jnp.where(lane == SUM_LANE, d_hi, jnp.where(lane == SUM_LANE + 1, delta - d_hi, x)).astype(BF16)
                qh, kh, vh = q_ref[pl.ds(r0, nq), cs], k_ref[:, cs], v_ref[:, cs]
                lse_h = jnp.sum(jnp.where(lane == hh, lsev, 0.0), axis=-1, keepdims=True)
                p = jnp.exp2(_dot_nt(qh, kh) - lse_h)
                ds = (p * _dot_nt(domx, vh)).astype(BF16)
                dq_ref[pl.ds(r0, nq), cs] = _dot(ds, kh) * B_SCALE
                dk_ref[:, cs] += _dot_tn(ds, qh)
                dv_ref[:, cs] += _dot_tn(p.astype(BF16), domx)

        def step(i, c):
            qtile(pl.multiple_of(i * 256, 256), 256)
            return c

        lax.fori_loop(0, n_big, step, 0)
        qtile(S, BLK)
        dk_ref[...] = dk_ref[...] * (1.0 / LOG2E)

    pair = lambda b, j: (b, j)
    wide, narrow = pl.BlockSpec((Lp, 256), pair), pl.BlockSpec((Lp, LANES), pair)
    return pl.pallas_call(
        body, name=name, grid=(Bl, B_HEADS // 2),
        in_specs=[wide, wide, wide, narrow, narrow, narrow, pl.BlockSpec(memory_space=pl.ANY)],
        out_specs=(wide, wide, wide), out_shape=(jax.ShapeDtypeStruct((T, 1024), F32),) * 3,
        compiler_params=_params("parallel", "parallel"),
    )(q, k, v, o, do, lse, after)


def _win_bwd(proj, bias, sink, ya, dya, lse, Bl, S, name):
    T = proj.shape[0]
    Lp = S + BLK
    NB = S // BLK
    nblk = Lp // BLK
    G = A_HEADS // A_KV

    def body(sink_ref, q_ref, k_ref, v_ref, b_ref, o_ref, do_ref, lse_ref, dq_ref, dk_ref, dv_ref, db_ref, ds_ref):
        n = pl.program_id(1)

        @pl.when(n == 0)
        def _():
            dk_ref[...] = jnp.zeros_like(dk_ref)
            dv_ref[...] = jnp.zeros_like(dv_ref)
            ds_ref[...] = jnp.zeros_like(ds_ref)

        @pl.when((n == 0) | (n == 1) | (n == NB - 1) | (n == NB))
        def _():
            db_ref[...] = jnp.zeros_like(db_ref)

        base = pl.multiple_of(_win_base(n, NB) * BLK, BLK)
        lane = _lane((BLK, LANES))
        lo = lane < 64
        lsev = lse_ref[...]
        dqs = [None] * A_HEADS
        for kvh in range(A_KV):
            cs = slice(LANES * kvh, LANES * (kvh + 1))
            k2 = jnp.concatenate([k_ref[pl.ds(base, 3 * BLK), cs], k_ref[S:S + BLK, cs]], axis=0).astype(BF16)
            v2 = jnp.concatenate([v_ref[pl.ds(base, 3 * BLK), cs], v_ref[S:S + BLK, cs]], axis=0).astype(BF16)
            dk_acc = jnp.zeros((4 * BLK, LANES), F32)
            dv_acc = jnp.zeros((4 * BLK, LANES), F32)
            for gi in range(G):
                h = kvh * G + gi
                ps = slice(LANES * (h // 2), LANES * (h // 2 + 1))
                hm = lo if h % 2 == 0 else ~lo
                qm = jnp.where(hm, q_ref[:, ps], 0.0).astype(BF16)
                s = _dot_nt(qm, k2) * A_SCALE + b_ref[0, h]
                lse_h = jnp.sum(jnp.where(lane == h, lsev, 0.0), axis=-1, keepdims=True)
                p = jnp.exp(s - lse_h)
                dom = jnp.where(hm, do_ref[:, ps], 0.0)
                delta = jnp.sum(dom * o_ref[:, ps], axis=-1, keepdims=True)
                domb = dom.astype(BF16)
                dsc = p * (_dot_nt(domb, v2) - delta)
                db_ref[0, 0, h] += dsc
                dsink = -jnp.sum(jnp.exp(sink_ref[0, h] - lse_h) * delta, axis=0, keepdims=True)
                ds_ref[0, h:h + 1, :] += jnp.broadcast_to(dsink, (1, LANES))
                dsl = (dsc * A_SCALE).astype(BF16)
                dqs[h] = _dot(dsl, k2)
                dk_acc = dk_acc + _dot_tn(dsl, qm)
                dv_acc = dv_acc + _dot_tn(p.astype(BF16), domb)
            dk_ref[pl.ds(base, 3 * BLK), cs] += dk_acc[0:3 * BLK]
            dk_ref[S:S + BLK, cs] += dk_acc[3 * BLK:4 * BLK]
            dv_ref[pl.ds(base, 3 * BLK), cs] += dv_acc[0:3 * BLK]
            dv_ref[S:S + BLK, cs] += dv_acc[3 * BLK:4 * BLK]
        for j in range(A_HEADS // 2):
            dq_ref[:, LANES * j:LANES * (j + 1)] = jnp.where(lo, dqs[2 * j], dqs[2 * j + 1])

    qrow = lambda b, n: (b * nblk + n, 0)
    kvs = pl.BlockSpec((Lp, 256), lambda b, n: (b, 0))
    return pl.pallas_call(
        body, name=name, grid=(Bl, nblk),
        in_specs=[pl.BlockSpec(memory_space=pltpu.SMEM), pl.BlockSpec((BLK, A_W), qrow),
                  pl.BlockSpec((Lp, 256), lambda b, n: (b, KA0 // 256)), pl.BlockSpec((Lp, 256), lambda b, n: (b, VA0 // 256)),
                  pl.BlockSpec((1, A_HEADS, BLK, 4 * BLK), lambda b, n: (_win_variant(n, NB), 0, 0, 0)),
                  pl.BlockSpec((BLK, A_W), qrow), pl.BlockSpec((BLK, A_W), qrow), pl.BlockSpec((BLK, LANES), qrow)],
        out_specs=(pl.BlockSpec((BLK, A_W), qrow), kvs, kvs,
                   pl.BlockSpec((1, 1, A_HEADS, BLK, 4 * BLK), lambda b, n: (b, _win_variant(n, NB), 0, 0, 0)),
                   pl.BlockSpec((1, 8, LANES), lambda b, n: (b, 0, 0))),
        out_shape=(jax.ShapeDtypeStruct((T, A_W), F32), jax.ShapeDtypeStruct((T, 256), F32),
                   jax.ShapeDtypeStruct((T, 256), F32), jax.ShapeDtypeStruct((Bl, 4, A_HEADS, BLK, 4 * BLK), F32),
                   jax.ShapeDtypeStruct((Bl, 8, LANES), F32)),
        compiler_params=_params("parallel", "arbitrary"),
    )(sink, proj, proj, proj, bias, ya, dya, lse)


def _table_grad(dbias_list, buckets, after, name):
    nl = len(dbias_list)
    Bl = dbias_list[0].shape[0]

    def body(*refs):
        d_refs = refs[:nl]
        b_ref, _, o_ref, acc, part = refs[nl:]
        v, b = pl.program_id(0), pl.program_id(1)

        @pl.when((v == 0) & (b == 0))
        def _():
            o_ref[...] = jnp.zeros_like(o_ref)

        tot = d_refs[0][0, 0]
        for r in d_refs[1:]:
            tot = tot + r[0, 0]

        @pl.when(b == 0)
        def _():
            acc[...] = tot

        @pl.when(b > 0)
        def _():
            acc[...] += tot

        @pl.when(b == Bl - 1)
        def _():
            bidx = b_ref[0]

            def step(j, c):
                mask = bidx == j
                for h in range(A_HEADS):
                    part[h, j] = jnp.sum(jnp.where(mask, acc[h], 0.0).reshape(BLK // 8, 8, 4 * BLK), axis=0)
                return c

            lax.fori_loop(0, N_BUCKETS, step, 0)
            rows = lax.broadcasted_iota(jnp.int32, (N_BUCKETS, LANES), 0)
            lanes = _lane((N_BUCKETS, LANES))
            for h in range(A_HEADS):
                col = jnp.sum(jnp.sum(part[h], axis=1), axis=-1, keepdims=True)
                o_ref[h:h + 1, :] += jnp.sum(jnp.where(rows == lanes, col, 0.0), axis=0, keepdims=True)

    return pl.pallas_call(
        body, name=name, grid=(4, Bl),
        in_specs=[pl.BlockSpec((1, 1, A_HEADS, BLK, 4 * BLK), lambda v, b: (b, v, 0, 0, 0))] * nl
        + [pl.BlockSpec((1, BLK, 4 * BLK), lambda v, b: (v, 0, 0)), pl.BlockSpec(memory_space=pl.ANY)],
        out_specs=pl.BlockSpec((8, LANES), lambda v, b: (0, 0)), out_shape=jax.ShapeDtypeStruct((8, LANES), F32),
        scratch_shapes=[pltpu.VMEM((A_HEADS, BLK, 4 * BLK), F32), pltpu.VMEM((A_HEADS, N_BUCKETS, 8, 4 * BLK), F32)],
        compiler_params=_params("arbitrary", "arbitrary"),
    )(*dbias_list, buckets, after)


def _mla_prep_bwd(dq, dk, dv, proj, gq, gkv, wq, wkk, wkv, tq, tk, tm, Lp, name):
    T = proj.shape[0]
    nt = Lp // tm

    def body(dq_ref, dk_ref, dv_ref, cq_ref, ckv_ref, gq_ref, gkv_ref, wq_ref, wkk_ref, wkv_ref, tq_ref, tk_ref,
             dcq_ref, dckv_ref, dkr_ref, dwq_ref, dwkk_ref, dwkv_ref, dgq_ref, dgkv_ref):
        @pl.when(pl.program_id(0) == 0)
        def _():
            for r in (dwq_ref, dwkk_ref, dwkv_ref, dgq_ref, dgkv_ref):
                r[...] = jnp.zeros_like(r)

        tqv = tq_ref[...]
        dqp = jnp.concatenate([_rope_t(dq_ref[:, LANES * h:LANES * (h + 1)], tqv) for h in range(B_HEADS)],
                              axis=1).astype(BF16)
        cq, gq_ = cq_ref[...], gq_ref[...]
        rq = _rstd(cq)
        dwq_ref[...] += _dot_tn((cq * rq * gq_).astype(BF16), dqp)
        dx, dg = _rms_bwd(_dot_nt(dqp, wq_ref[...]), cq, rq, gq_)
        dcq_ref[...] = dx
        dgq_ref[...] += dg

        dkv_ = dk_ref[...]
        dks = dkv_[:, 0:LANES]
        for h in range(1, B_HEADS):
            dks = dks + dkv_[:, LANES * h:LANES * (h + 1)]
        dkr_ref[...] = _rope_t(pltpu.roll(dks, NOPE, 1), tk_ref[...])
        dkb = dkv_.astype(BF16)
        dvb = dv_ref[...].astype(BF16)
        ckv, gkv_ = ckv_ref[...], gkv_ref[...]
        rk = _rstd(ckv)
        cb = (ckv * rk * gkv_).astype(BF16)
        dwkk_ref[...] += _dot_tn(cb, dkb)
        dwkv_ref[...] += _dot_tn(cb, dvb)
        dx, dg = _rms_bwd(_dot_nt(dkb, wkk_ref[...]) + _dot_nt(dvb, wkv_ref[...]), ckv, rk, gkv_)
        dckv_ref[...] = dx
        dgkv_ref[...] += dg

    tab = lambda n: pl.BlockSpec((n, tm, LANES), lambda i: (0, i % nt, 0))
    wide = pl.BlockSpec((tm, 1024), _row(0))
    return pl.pallas_call(
        body, name=name, grid=(T // tm,),
        in_specs=[wide, wide, wide, pl.BlockSpec((tm, 256), _row(CQ0 // 256)),
                  pl.BlockSpec((tm, 128), _row(CKV0 // 128)), pl.BlockSpec((1, Q_RANK), _const2),
                  pl.BlockSpec((1, KV_RANK), _const2), pl.BlockSpec((Q_RANK, 1024), _const2),
                  pl.BlockSpec((KV_RANK, 1024), _const2), pl.BlockSpec((KV_RANK, 1024), _const2), tab(3), tab(4)],
        out_specs=(pl.BlockSpec((tm, 256), _row(0)), pl.BlockSpec((tm, 128), _row(0)), pl.BlockSpec((tm, 128), _row(0)),
                   pl.BlockSpec((Q_RANK, 1024), _const2), pl.BlockSpec((KV_RANK, 1024), _const2),
                   pl.BlockSpec((KV_RANK, 1024), _const2), pl.BlockSpec((1, Q_RANK), _const2),
                   pl.BlockSpec((1, KV_RANK), _const2)),
        out_shape=(jax.ShapeDtypeStruct((T, 256), F32), jax.ShapeDtypeStruct((T, 128), F32),
                   jax.ShapeDtypeStruct((T, 128), F32), jax.ShapeDtypeStruct((Q_RANK, 1024), F32),
                   jax.ShapeDtypeStruct((KV_RANK, 1024), F32), jax.ShapeDtypeStruct((KV_RANK, 1024), F32),
                   jax.ShapeDtypeStruct((1, Q_RANK), F32), jax.ShapeDtypeStruct((1, KV_RANK), F32)),
        compiler_params=_params("arbitrary"),
    )(dq, dk, dv, proj, proj, gq, gkv, wq, wkk, wkv, tq, tk)


def _inproj_bwd(pieces, h, g, wx, dh_out, tm, name):
    T = h.shape[0]
    nsteps = T // tm
    CH = 512

    def body(*refs):
        p_refs = refs[:len(pieces)]
        h_ref, g_ref, w_ref, dho_ref, dh_ref, dw_hbm, dg_ref, acc, sem = refs[len(pieces):]
        i = pl.program_id(0)

        @pl.when(i == 0)
        def _():
            acc[...] = jnp.zeros_like(acc)
            dg_ref[...] = jnp.zeros_like(dg_ref)

        dp = jnp.concatenate([r[...].astype(BF16) for r in p_refs], axis=1)
        x, gain = h_ref[...], g_ref[...]
        r = _rstd(x)
        u = (x * r * gain).astype(BF16)
        for c in range(0, NX, CH):
            acc[:, c:c + CH] += _dot_tn(u, dp[:, c:c + CH])
        dx, dg = _rms_bwd(_dot_nt(dp, w_ref[...]), x, r, gain)
        dh_ref[...] = dho_ref[...] + dx
        dg_ref[...] += dg

        @pl.when(i == nsteps - 1)
        def _():
            cp = pltpu.make_async_copy(acc, dw_hbm, sem)
            cp.start()
            cp.wait()

    return pl.pallas_call(
        body, name=name, grid=(nsteps,),
        in_specs=[pl.BlockSpec((tm, p.shape[1]), _row(0)) for p in pieces]
        + [pl.BlockSpec((tm, D), _row(0)), pl.BlockSpec((1, D), _const2), pl.BlockSpec((D, NX), _const2),
           pl.BlockSpec((tm, D), _row(0))],
        out_specs=(pl.BlockSpec((tm, D), _row(0)), pl.BlockSpec(memory_space=pl.ANY), pl.BlockSpec((1, D), _const2)),
        out_shape=(jax.ShapeDtypeStruct((T, D), F32), jax.ShapeDtypeStruct((D, NX), F32), jax.ShapeDtypeStruct((1, D), F32)),
        scratch_shapes=[pltpu.VMEM((D, NX), F32), pltpu.SemaphoreType.DMA(())],
        compiler_params=_params("arbitrary"),
    )(*pieces, h, g, wx, dh_out)


def kernel(x, meta_tokens, rel_bias_table, norm_in, w_in, sink_a, norm_q_lat, w_uq, norm_kv_lat, w_ukv, norm_out_a, norm_out_b, w_out, norm_final, loss_target, m_meta_tokens, m_rel_bias_table, m_norm_in, m_w_in, m_sink_a, m_norm_q_lat, m_w_uq, m_norm_kv_lat, m_w_ukv, m_norm_out_a, m_norm_out_b, m_w_out, m_norm_final, v_meta_tokens, v_rel_bias_table, v_norm_in, v_w_in, v_sink_a, v_norm_q_lat, v_w_uq, v_norm_kv_lat, v_w_ukv, v_norm_out_a, v_norm_out_b, v_w_out, v_norm_final):
    Bl, S, _ = x.shape
    Lp = S + BLK
    T = Bl * Lp
    tm = Lp // 4
    tq = Lp // 8
    depth = w_in.shape[0]
    me = 4 * lax.axis_index("x") + 2 * lax.axis_index("y") + lax.axis_index("c")

    def finish(handles, after, name):
        srcs, lands = _xchg_wait(handles, after, name)
        full = []
        for hd, src, land in zip(handles, srcs, lands):
            own = src if hd[4] == "gather" else lax.dynamic_index_in_dim(src, me, 0, keepdims=False)
            full.append(_own_slot(land, own, me))
        return full

    def weights_of(i):
        return [w_in[i].astype(BF16), w_uq[i].astype(BF16), w_ukv[i].astype(BF16), w_out[i].astype(BF16)]

    def packed(l_in, l_uq, l_ukv, l_out):
        cols = lambda t: jnp.transpose(t, (1, 0, 2)).reshape(t.shape[1], NDEV * t.shape[2])
        wkk, wkv = _pack_w_ukv(cols(l_ukv))
        return _pack_w_in(cols(l_in)), _pack_w_uq(cols(l_uq)), wkk, wkv, l_out.reshape(D, D)

    wb = weights_of(0)
    gat_a, tok = _xchg_start([wb[0], meta_tokens], ["gather", "gather"], "gather_start_0a")
    buckets = _bias_buckets(S)
    bias = _build_bias(buckets, _after(rel_bias_table, tok), "build_bias")
    tq_tab = _rope_tables(S, Lp, NOPE, True)
    key_pad = jnp.where((jnp.arange(Lp) >= S + NMETA)[:, None] & (jnp.arange(LANES) == MASK_LANE)[None, :], NEG, 0.0)
    tk_tab = jnp.concatenate([_rope_tables(S, Lp, 0, False), key_pad.astype(F32)[None]], axis=0)
    l_in, l_meta = finish(gat_a, bias, "gather_wait_0a")
    gat_b, tok = _xchg_start(wb[1:], ["gather"] * 3, "gather_start_0b", after=l_in)

    meta_f = jnp.transpose(l_meta, (1, 0, 2)).reshape(NMETA, D)
    tail = jnp.concatenate([meta_f, jnp.zeros((BLK - NMETA, D), F32)], axis=0)
    h = jnp.concatenate([x, jnp.broadcast_to(tail[None], (Bl, BLK, D))], axis=1).reshape(T, D)

    saved = []
    w_out_f = [None] * depth
    gat_next = None
    for i in range(depth):
        g_in, g_q, g_kv = norm_in[i][None], norm_q_lat[i][None], norm_kv_lat[i][None]
        g_a, g_b, sink = norm_out_a[i][None], norm_out_b[i][None], sink_a[i][None]
        if i == 0:
            proj = _inproj_fwd(h, _after(g_in, tok), _pack_w_in(jnp.transpose(l_in, (1, 0, 2)).reshape(D, IN_W)), tm,
                               f"inproj_fwd_{i}")
            l_uq, l_ukv, l_out = finish(gat_b, proj, "gather_wait_0b")
            wx, wq, wkk, wkv, w_out_f[i] = packed(l_in, l_uq, l_ukv, l_out)
        else:
            wx, wq, wkk, wkv, w_out_f[i] = packed(*finish(gat_next, h, f"gather_wait_{i}"))
            proj = _inproj_fwd(h, g_in, wx, tm, f"inproj_fwd_{i}")
        if i + 1 < depth:
            gat_next, tok = _xchg_start(weights_of(i + 1), ["gather"] * 4, f"gather_start_{i + 1}", after=w_out_f[i])
            g_q = _after(g_q, tok)
        q, k, v = _mla_prep_fwd(proj, g_q, g_kv, wq, wkk, wkv, tq_tab, tk_tab, tm, Lp, f"mla_prep_fwd_{i}")
        ya, lse_a = _win_fwd(proj, bias, sink, Bl, S, f"win_fwd_{i}")
        yb, lse_b = _mla_fwd(q, k, v, Bl, Lp, f"mla_fwd_{i}")
        h_new = _out_fwd(ya, yb, proj, g_a, g_b, w_out_f[i], h, tm, f"out_fwd_{i}")
        saved.append((h, proj, q, k, v, ya, lse_a, yb, lse_b, wx, wq, wkk, wkv))
        h = h_new

    dh, loss_acc, dg_final = _loss_head(h, norm_final[None], loss_target, Bl, S, "loss_head")
    loss = lax.psum(loss_acc[0, 0], ("x", "y", "c"))

    g_n_in, g_sink, g_nq, g_nkv, g_na, g_nb = ([None] * depth for _ in range(6))
    dbias_all, sc_out, sc_rest = [], [None] * depth, [None] * depth
    split = lambda t, n: jnp.transpose(t.reshape(t.shape[0], NDEV, n), (1, 0, 2)).astype(BF16)
    tok = None
    for i in reversed(range(depth)):
        h_in, proj, q, k, v, ya, lse_a, yb, lse_b, wx, wq, wkk, wkv = saved[i]
        g_in, g_q, g_kv = norm_in[i][None], norm_q_lat[i][None], norm_kv_lat[i][None]
        g_a, g_b, sink = norm_out_a[i][None], norm_out_b[i][None], sink_a[i][None]
        if tok is not None:
            g_a = _after(g_a, tok)
        dya, dyb, dga, dgb, g_w_out, g_na[i], g_nb[i] = _out_bwd(dh, ya, yb, proj, g_a, g_b, w_out_f[i], tm, f"out_bwd_{i}")
        sc_out[i], tok = _xchg_start([g_w_out.reshape(NDEV, D // NDEV, D).astype(BF16)], ["scatter"], f"scatter_start_{i}a")
        dq, dk, dv = _mla_bwd(q, k, v, yb, dyb, lse_b, tok, Bl, Lp, f"mla_bwd_{i}")
        dqa, dka, dva, dbias, dsink = _win_bwd(proj, bias, sink, ya, dya, lse_a, Bl, S, f"win_bwd_{i}")
        dcq, dckv, dkr, dwq, dwkk, dwkv, g_nq[i], g_nkv[i] = _mla_prep_bwd(
            dq, dk, dv, proj, g_q, g_kv, wq, wkk, wkv, tq_tab, tk_tab, tm, Lp, f"mla_prep_bwd_{i}")
        dh, dwx, g_n_in[i] = _inproj_bwd((dqa, dka, dva, dga, dcq, dckv, dkr, dgb), h_in, g_in, wx, dh, tq,
                                         f"inproj_bwd_{i}")
        g_sink[i] = jnp.sum(dsink[:, :, 0], axis=0)
        dbias_all.append(dbias)
        rest = [split(_unpack_w_in_grad(dwx), IN_W // NDEV), split(_unpack_w_uq_grad(dwq), 768 // NDEV),
                split(_unpack_w_ukv_grad(dwkk, dwkv), 1024 // NDEV)]
        if i == 0:
            dh3 = dh.reshape(Bl, Lp, D)
            grad_x = dh3[:, :S]
            rest.append(jnp.transpose(jnp.sum(dh3[:, S:S + NMETA], axis=0).reshape(NMETA, NDEV, D // NDEV), (1, 0, 2)))
        sc_rest[i], tok = _xchg_start(rest, ["scatter"] * len(rest), f"scatter_start_{i}b")

    dtab = _table_grad(dbias_all, buckets, tok, "table_grad")
    two_d = lambda a: a.reshape(1, -1) if a.ndim == 1 else a
    small_w = [rel_bias_table, norm_in, sink_a, norm_q_lat, norm_kv_lat, norm_out_a, norm_out_b, norm_final]
    small_m = [m_rel_bias_table, m_norm_in, m_sink_a, m_norm_q_lat, m_norm_kv_lat, m_norm_out_a, m_norm_out_b, m_norm_final]
    small_v = [v_rel_bias_table, v_norm_in, v_sink_a, v_norm_q_lat, v_norm_kv_lat, v_norm_out_a, v_norm_out_b, v_norm_final]
    small_g = [jnp.transpose(dtab[:, :N_BUCKETS]), jnp.concatenate(g_n_in), jnp.stack(g_sink), jnp.concatenate(g_nq),
               jnp.concatenate(g_nkv), jnp.concatenate(g_na), jnp.concatenate(g_nb), dg_final]
    sc_small, tok = _xchg_start(small_g, ["gather"] * len(small_g), "scatter_start_small")

    r_out, r_in, r_uq, r_ukv = [None] * depth, [None] * depth, [None] * depth, [None] * depth
    for i in reversed(range(depth)):
        (r_out[i],) = finish(sc_out[i], tok, f"scatter_wait_{i}a")
        got = finish(sc_rest[i], tok, f"scatter_wait_{i}b")
        r_in[i], r_uq[i], r_ukv[i] = got[:3]
        if i == 0:
            r_meta = got[3]
    r_small = finish(sc_small, tok, "scatter_wait_small")

    res = {}
    sharded = dict(w_in=(r_in, w_in, m_w_in, v_w_in), w_uq=(r_uq, w_uq, m_w_uq, v_w_uq), w_ukv=(r_ukv, w_ukv, m_w_ukv, v_w_ukv),
                   w_out=(r_out, w_out, m_w_out, v_w_out),
                   meta_tokens=([r_meta], meta_tokens[None], m_meta_tokens[None], v_meta_tokens[None]))
    for n, (r, w, m, v) in sharded.items():
        outs = _adamw_param(r, w, m, v, f"adamw_{n}")
        res[n] = [o[0] for o in outs] if n == "meta_tokens" else list(outs)
    names_small = ["rel_bias_table", "norm_in", "sink_a", "norm_q_lat", "norm_kv_lat", "norm_out_a", "norm_out_b", "norm_final"]
    outs = _adamw_small(r_small, [two_d(a) for a in small_w], [two_d(a) for a in small_m], [two_d(a) for a in small_v],
                        "adamw_replicated")
    ns = len(names_small)
    for j, n in enumerate(names_small):
        res[n] = [outs[kk * ns + j].reshape(small_w[j].shape) for kk in range(4)]
    order = ["meta_tokens", "rel_bias_table", "norm_in", "w_in", "sink_a", "norm_q_lat", "w_uq", "norm_kv_lat", "w_ukv",
             "norm_out_a", "norm_out_b", "w_out", "norm_final"]
    return (loss, grad_x, *[res[n][kk] for kk in range(4) for n in order])
```

```python
import functools
import math

import numpy as np
import jax
import jax.numpy as jnp
from jax import lax
from jax.experimental import pallas as pl
from jax.experimental.pallas import tpu as pltpu

F32, BF16 = jnp.float32, jnp.bfloat16
D = 1024
NMETA = 16
BLK = 128
A_HEADS, A_KV, A_DH, A_W = 8, 2, 64, 512
B_HEADS, NOPE, ROPE, B_V, B_W = 8, 64, 32, 64, 512
Q_RANK, KV_RANK = 256, 128
IN_W = 2208
N_BUCKETS, MAX_DIST = 32, 128
THETA = 10000.0
EPS = 1e-6
NEG = -1e30
A_SCALE = A_DH ** -0.5
B_SCALE = (NOPE + ROPE) ** -0.5
LOG2E = math.log2(math.e)
MASK_LANE = NOPE + ROPE
SUM_LANE = B_V
LANES = 128
NDEV = 8
MESH = pl.DeviceIdType.MESH

NX = 2560
QA0, KA0, VA0, GA0, CQ0, CKV0, KR0, GB0 = 0, 512, 768, 1024, 1536, 1792, 1920, 2048

ADAM_LR, ADAM_B1, ADAM_B2, ADAM_EPS, ADAM_WD, ADAM_STEP = 0.001, 0.9, 0.999, 1e-08, 0.01, 10


def _dot(a, b):
    return jnp.dot(a, b, preferred_element_type=F32)


def _dot_nt(a, b):
    return lax.dot_general(a, b, (((1,), (1,)), ((), ())), preferred_element_type=F32)


def _dot_tn(a, b):
    return lax.dot_general(a, b, (((0,), (0,)), ((), ())), preferred_element_type=F32)


def _lane(shape):
    return lax.broadcasted_iota(jnp.int32, shape, len(shape) - 1)


def _rstd(x):
    return lax.rsqrt(jnp.mean(x * x, axis=-1, keepdims=True) + EPS)


def _rms_bwd(dn, x, r, g):
    z = dn * g
    dx = r * z - x * (r * r * r) * jnp.mean(z * x, axis=-1, keepdims=True)
    return dx, jnp.sum(dn * (x * r), axis=0, keepdims=True)


def _row(i):
    return lambda *ids: (ids[0], i)


def _const2(*ids):
    return (0, 0)


def _params(*sem):
    return pltpu.CompilerParams(dimension_semantics=sem)


def _pack_w_in(w):
    qa, ka, va, ga = w[:, 0:512], w[:, 512:640], w[:, 640:768], w[:, 768:1280]
    cq, ckv, kr, gb = w[:, 1280:1536], w[:, 1536:1664], w[:, 1664:1696], w[:, 1696:2208]
    dup = lambda t: jnp.concatenate([t[:, 0:64], t[:, 0:64], t[:, 64:128], t[:, 64:128]], axis=1)
    krp = jnp.concatenate([kr, jnp.zeros((w.shape[0], LANES - ROPE), w.dtype)], axis=1)
    return jnp.concatenate([qa, dup(ka), dup(va), ga, cq, ckv, krp, gb], axis=1)


def _unpack_w_in_grad(g):
    fold = lambda t: jnp.concatenate([t[:, 0:64] + t[:, 64:128], t[:, 128:192] + t[:, 192:256]], axis=1)
    return jnp.concatenate([g[:, QA0:QA0 + 512], fold(g[:, KA0:KA0 + 256]), fold(g[:, VA0:VA0 + 256]),
                            g[:, GA0:GA0 + 512], g[:, CQ0:CQ0 + 256], g[:, CKV0:CKV0 + 128],
                            g[:, KR0:KR0 + ROPE], g[:, GB0:GB0 + 512]], axis=1)


def _pack_w_uq(w):
    t = w.reshape(Q_RANK, B_HEADS, NOPE + ROPE)
    t = jnp.concatenate([t, jnp.zeros((Q_RANK, B_HEADS, LANES - NOPE - ROPE), w.dtype)], axis=-1)
    return t.reshape(Q_RANK, B_HEADS * LANES)


def _unpack_w_uq_grad(g):
    return g.reshape(Q_RANK, B_HEADS, LANES)[:, :, :NOPE + ROPE].reshape(Q_RANK, B_HEADS * (NOPE + ROPE))


def _pack_w_ukv(w):
    t = w.reshape(KV_RANK, B_HEADS, NOPE + B_V)
    z = jnp.zeros((KV_RANK, B_HEADS, LANES - NOPE), w.dtype)
    pad = lambda u: jnp.concatenate([u, z], axis=-1).reshape(KV_RANK, B_HEADS * LANES)
    return pad(t[:, :, :NOPE]), pad(t[:, :, NOPE:])


def _unpack_w_ukv_grad(gk, gv):
    head = lambda g: g.reshape(KV_RANK, B_HEADS, LANES)[:, :, :NOPE]
    return jnp.concatenate([head(gk), head(gv)], axis=-1).reshape(KV_RANK, B_HEADS * (NOPE + B_V))


def _t5_bucket(rel):
    nb = N_BUCKETS // 2
    max_exact = nb // 2
    ret = jnp.where(rel > 0, nb, 0)
    n = jnp.abs(rel)
    nf = jnp.maximum(n, 1).astype(F32)
    large = max_exact + (jnp.log(nf / max_exact) / math.log(MAX_DIST / max_exact) * (nb - max_exact)).astype(jnp.int32)
    large = jnp.minimum(large, nb - 1)
    return ret + jnp.where(n < max_exact, n, large)


def _bias_buckets(S):
    q = np.arange(BLK)[:, None]
    k = np.arange(4 * BLK)[None, :]
    is_meta_key = (k >= 3 * BLK) & (k < 3 * BLK + NMETA)
    mi = k - 3 * BLK
    rels, valids = [], []
    for shift in (0, BLK, 2 * BLK):
        rel_real = k - shift - q
        valid_real = (k < 3 * BLK) & (np.abs(rel_real) <= BLK)
        far = -(NMETA + MAX_DIST + BLK)
        rel_meta = (mi - (NMETA + q)) if shift == 0 else np.full_like(k + q, far)
        rels.append(np.where(is_meta_key, rel_meta, rel_real))
        valids.append(valid_real | is_meta_key)
    qm = q < NMETA
    rel_real = NMETA + k - q
    valid_real = (k < BLK) & (np.abs(rel_real) <= BLK) & qm
    rels.append(np.where(is_meta_key, mi - q, rel_real))
    valids.append(valid_real | is_meta_key)
    rel = jnp.asarray(np.stack(rels).astype(np.int32))
    valid = jnp.asarray(np.stack(valids))
    return jnp.where(valid, _t5_bucket(rel), -1).astype(jnp.int32)


def _rope_tables(S, Lp, off, passthrough):
    half = ROPE // 2
    r = np.arange(Lp)
    pos = np.where(r < S, NMETA + r, np.where(r < S + NMETA, r - S, 0)).astype(np.float32)
    freqs = THETA ** (-jnp.arange(half, dtype=F32) / half)
    ang = jnp.asarray(pos)[:, None] * freqs[None, :]
    cos, sin = jnp.cos(ang), jnp.sin(ang)
    z = lambda n: jnp.zeros((Lp, n), F32)
    head = jnp.ones((Lp, off), F32) if passthrough else z(off)
    c = jnp.concatenate([head, cos, cos, z(LANES - off - ROPE)], axis=1)
    s1 = jnp.concatenate([z(off), -sin, z(LANES - off - half)], axis=1)
    s2 = jnp.concatenate([z(off + half), sin, z(LANES - off - ROPE)], axis=1)
    return jnp.stack([c, s1, s2])


def _rope(x, t):
    return x * t[0] + pltpu.roll(x, LANES - 16, 1) * t[1] + pltpu.roll(x, 16, 1) * t[2]


def _rope_t(dy, t):
    return dy * t[0] + pltpu.roll(dy * t[1], 16, 1) + pltpu.roll(dy * t[2], LANES - 16, 1)


def _me_and_peers():
    x, y, c = lax.axis_index("x"), lax.axis_index("y"), lax.axis_index("c")
    flip = lambda v, b: 1 - v if b else v
    peers = [(flip(x, k & 4), flip(y, k & 2), flip(c, k & 1)) for k in range(1, NDEV)]
    return 4 * x + 2 * y + c, peers


def _xchg_start(srcs, kinds, name, after=None):
    n = len(srcs)
    lands = [lax.empty((NDEV,) + s.shape[-2:], s.dtype) for s in srcs]
    extra = [] if after is None else [after]

    def body(*refs):
        src_refs, land_refs = refs[:n], refs[n:2 * n]
        ssems, rsems = refs[2 * n + len(extra):3 * n + len(extra)], refs[3 * n + len(extra):4 * n + len(extra)]
        token = refs[6 * n + len(extra)]
        me, peers = _me_and_peers()
        for it in range(n):
            for k, (px, py, pc) in enumerate(peers):
                src = src_refs[it] if kinds[it] == "gather" else src_refs[it].at[4 * px + 2 * py + pc]
                pltpu.make_async_remote_copy(src_ref=src, dst_ref=land_refs[it].at[me], send_sem=ssems[it].at[k],
                                             recv_sem=rsems[it].at[k], device_id=(px, py, pc), device_id_type=MESH).start()
        token[...] = jnp.zeros_like(token)

    hbm = pl.BlockSpec(memory_space=pltpu.HBM)
    sem = pl.BlockSpec(memory_space=pltpu.SEMAPHORE)
    outs = pl.pallas_call(
        body, name=name,
        out_shape=tuple([pltpu.SemaphoreType.DMA((NDEV - 1,))] * (2 * n) + [pltpu.HBM(a.shape, a.dtype) for a in srcs + lands]
                        + [jax.ShapeDtypeStruct((8, LANES), F32)]),
        in_specs=[hbm] * (2 * n) + [pl.BlockSpec(memory_space=pl.ANY)] * len(extra),
        out_specs=tuple([sem] * (2 * n) + [hbm] * (2 * n) + [pl.BlockSpec(memory_space=pltpu.VMEM)]),
        input_output_aliases={i: 2 * n + i for i in range(2 * n)},
        compiler_params=pltpu.CompilerParams(has_side_effects=pltpu.SideEffectType.DATAFLOW_SIDE_EFFECTING),
    )(*[pltpu.with_memory_space_constraint(a, pltpu.HBM) for a in srcs + lands], *extra)
    handles = [(outs[it], outs[n + it], outs[2 * n + it], outs[3 * n + it], kinds[it]) for it in range(n)]
    return handles, outs[4 * n]


def _xchg_wait(handles, after, name):
    n = len(handles)

    def body(*refs):
        src_refs, land_refs = refs[:n], refs[n:2 * n]
        ssems, rsems = refs[2 * n:3 * n], refs[3 * n:4 * n]
        me, peers = _me_and_peers()
        for it in range(n):
            for k, (px, py, pc) in enumerate(peers):
                src = src_refs[it] if handles[it][4] == "gather" else src_refs[it].at[4 * px + 2 * py + pc]
                cp = pltpu.make_async_remote_copy(src_ref=src, dst_ref=land_refs[it].at[me], send_sem=ssems[it].at[k],
                                                  recv_sem=rsems[it].at[k], device_id=(px, py, pc), device_id_type=MESH)
                cp.wait_send()
                cp.wait_recv()

    hbm = pl.BlockSpec(memory_space=pltpu.HBM)
    sem = pl.BlockSpec(memory_space=pltpu.SEMAPHORE)
    srcs, lands = [h[2] for h in handles], [h[3] for h in handles]
    outs = pl.pallas_call(
        body, name=name, out_shape=tuple(pltpu.HBM(a.shape, a.dtype) for a in srcs + lands),
        in_specs=[hbm] * (2 * n) + [sem] * (2 * n) + [pl.BlockSpec(memory_space=pl.ANY)], out_specs=tuple([hbm] * (2 * n)),
        input_output_aliases={i: i for i in range(2 * n)},
        compiler_params=pltpu.CompilerParams(has_side_effects=pltpu.SideEffectType.DATAFLOW_SIDE_EFFECTING),
    )(*srcs, *lands, *[h[0] for h in handles], *[h[1] for h in handles], after)
    return list(outs[:n]), list(outs[n:])


def _own_slot(land, own, me):
    return lax.dynamic_update_slice(land, own[None].astype(land.dtype), (me, 0, 0))


def _after(x, token):
    return x + token[0, 0]


def _adamw_update(g, w, m, v):
    mn = ADAM_B1 * m + (1.0 - ADAM_B1) * g
    vn = ADAM_B2 * v + (1.0 - ADAM_B2) * (g * g)
    m_hat = mn / (1.0 - ADAM_B1 ** ADAM_STEP)
    v_hat = vn / (1.0 - ADAM_B2 ** ADAM_STEP)
    return -ADAM_LR * (m_hat / (jnp.sqrt(v_hat) + ADAM_EPS) + ADAM_WD * w), mn, vn


def _sum_partials(r_ref):
    g = r_ref[0].astype(F32)
    for p in range(1, NDEV):
        g = g + r_ref[p].astype(F32)
    return g


def _adamw_param(recvs, w, m, v, name):
    depth, r, c = w.shape
    tr = min(r, 256)
    nl = len(recvs)

    def body(*refs):
        r_refs = refs[:nl]
        w_ref, m_ref, v_ref, g_out, d_out, m_out, v_out = refs[nl:]
        for li in range(nl):
            @pl.when(pl.program_id(0) == li)
            def _(li=li):
                g = _sum_partials(r_refs[li])
                g_out[0] = g
                d_out[0], m_out[0], v_out[0] = _adamw_update(g, w_ref[0], m_ref[0], v_ref[0])

    blk = pl.BlockSpec((1, tr, c), lambda l, i: (l, i, 0))
    return pl.pallas_call(
        body, name=name, grid=(depth, r // tr),
        in_specs=[pl.BlockSpec((NDEV, tr, c), lambda l, i: (0, i, 0))] * nl + [blk, blk, blk],
        out_specs=(blk, blk, blk, blk), out_shape=(jax.ShapeDtypeStruct(w.shape, F32),) * 4,
        compiler_params=_params("arbitrary", "arbitrary"),
    )(*recvs, w, m, v)


def _adamw_small(recvs, ws, ms, vs, name):
    n = len(ws)

    def body(*refs):
        r_refs, w_refs, m_refs, v_refs = refs[:n], refs[n:2 * n], refs[2 * n:3 * n], refs[3 * n:4 * n]
        outs = refs[4 * n:]
        for j in range(n):
            g = _sum_partials(r_refs[j])
            outs[j][...] = g
            outs[n + j][...], outs[2 * n + j][...], outs[3 * n + j][...] = _adamw_update(
                g, w_refs[j][...], m_refs[j][...], v_refs[j][...])

    vm = pl.BlockSpec(memory_space=pltpu.VMEM)
    return pl.pallas_call(
        body, name=name, in_specs=[vm] * (4 * n), out_specs=tuple([vm] * (4 * n)),
        out_shape=tuple(jax.ShapeDtypeStruct(a.shape, F32) for a in ws) * 4,
    )(*recvs, *ws, *ms, *vs)


def _inproj_fwd(h, g, wx, tm, name):
    T = h.shape[0]

    def body(h_ref, g_ref, w_ref, o_ref):
        x = h_ref[...]
        u = (x * _rstd(x) * g_ref[...]).astype(BF16)
        o_ref[...] = _dot(u, w_ref[...])

    return pl.pallas_call(
        body, name=name, grid=(T // tm,),
        in_specs=[pl.BlockSpec((tm, D), _row(0)), pl.BlockSpec((1, D), _const2), pl.BlockSpec((D, NX), _const2)],
        out_specs=pl.BlockSpec((tm, NX), _row(0)), out_shape=jax.ShapeDtypeStruct((T, NX), F32),
        compiler_params=_params("parallel"),
    )(h, g, wx)


def _mla_prep_fwd(proj, gq, gkv, wq, wkk, wkv, tq, tk, tm, Lp, name):
    T = proj.shape[0]
    nt = Lp // tm

    def body(cq_ref, ckv_ref, kr_ref, gq_ref, gkv_ref, wq_ref, wkk_ref, wkv_ref, tq_ref, tk_ref, q_ref, k_ref, v_ref):
        cq = cq_ref[...]
        q = _dot((cq * _rstd(cq) * gq_ref[...]).astype(BF16), wq_ref[...])
        tqv = tq_ref[...]
        lane = _lane((1, LANES))
        one = jnp.where(lane == MASK_LANE, 1.0, 0.0)
        for h in range(B_HEADS):
            cs = slice(LANES * h, LANES * (h + 1))
            q_ref[:, cs] = (_rope(q[:, cs], tqv) * (B_SCALE * LOG2E) + one).astype(BF16)
        tkv = tk_ref[...]
        ksh = pltpu.roll(_rope(kr_ref[...], tkv), NOPE, 1) + tkv[3]
        ckv = ckv_ref[...]
        cb = (ckv * _rstd(ckv) * gkv_ref[...]).astype(BF16)
        kn = _dot(cb, wkk_ref[...])
        vn = _dot(cb, wkv_ref[...])
        minus = jnp.where((lane == SUM_LANE) | (lane == SUM_LANE + 1), -1.0, 0.0)
        for h in range(B_HEADS):
            cs = slice(LANES * h, LANES * (h + 1))
            k_ref[:, cs] = (kn[:, cs] + ksh).astype(BF16)
            v_ref[:, cs] = (vn[:, cs] + minus).astype(BF16)

    tab = lambda n: pl.BlockSpec((n, tm, LANES), lambda i: (0, i % nt, 0))
    wide = pl.BlockSpec((tm, 1024), _row(0))
    return pl.pallas_call(
        body, name=name, grid=(T // tm,),
        in_specs=[pl.BlockSpec((tm, 256), _row(CQ0 // 256)), pl.BlockSpec((tm, 128), _row(CKV0 // 128)),
                  pl.BlockSpec((tm, 128), _row(KR0 // 128)), pl.BlockSpec((1, Q_RANK), _const2),
                  pl.BlockSpec((1, KV_RANK), _const2), pl.BlockSpec((Q_RANK, 1024), _const2),
                  pl.BlockSpec((KV_RANK, 1024), _const2), pl.BlockSpec((KV_RANK, 1024), _const2), tab(3), tab(4)],
        out_specs=(wide, wide, wide), out_shape=(jax.ShapeDtypeStruct((T, 1024), BF16),) * 3,
        compiler_params=_params("parallel"),
    )(proj, proj, proj, gq, gkv, wq, wkk, wkv, tq, tk)


SINK_SLOT = 4 * BLK - 1


def _build_bias(buckets, table, sinks, name):
    depth = sinks.shape[0]

    def body(tab_ref, sink_ref, b_ref, *o_refs):
        bidx = b_ref[0]
        slot = _lane(bidx.shape)
        for h in range(A_HEADS):
            acc = jnp.full(bidx.shape, NEG, F32)
            for j in range(N_BUCKETS):
                acc = jnp.where(bidx == j, tab_ref[j, h] * LOG2E, acc)
            for l in range(depth):
                o_refs[l][0, h] = jnp.where(slot == SINK_SLOT, sink_ref[l, h] * LOG2E, acc)

    smem = pl.BlockSpec(memory_space=pltpu.SMEM)
    blk = pl.BlockSpec((1, A_HEADS, BLK, 4 * BLK), lambda i: (i, 0, 0, 0))
    return pl.pallas_call(
        body, name=name, grid=(4,),
        in_specs=[smem, smem, pl.BlockSpec((1, BLK, 4 * BLK), lambda i: (i, 0, 0))],
        out_specs=tuple([blk] * depth), out_shape=tuple([jax.ShapeDtypeStruct((4, A_HEADS, BLK, 4 * BLK), F32)] * depth),
        compiler_params=_params("parallel"),
    )(table, sinks, buckets)


def _win_base(n, NB):
    return jnp.where(n == NB, 0, jnp.clip(n - 1, 0, NB - 3))


def _win_variant(n, NB):
    return jnp.where(n == 0, 0, jnp.where(n < NB - 1, 1, jnp.where(n == NB - 1, 2, 3)))


def _win_kv(k_ref, v_ref, base, S, kvh):
    cs = slice(LANES * kvh, LANES * (kvh + 1))
    k2 = jnp.concatenate([k_ref[pl.ds(base, 3 * BLK), cs], k_ref[S:S + BLK, cs]], axis=0).astype(BF16)
    v2 = jnp.concatenate([v_ref[pl.ds(base, 3 * BLK), cs], v_ref[S:S + BLK, cs]], axis=0)
    lane = _lane(v2.shape)
    v2 = jnp.where(lane < A_DH, v2, jnp.where(lane < A_DH + 2, -1.0, 0.0)).astype(BF16)
    return k2, v2


def _win_fwd(proj, bias, Bl, S, name):
    T = proj.shape[0]
    Lp = S + BLK
    NB = S // BLK
    nblk = Lp // BLK
    G = A_HEADS // A_KV

    def body(q_ref, k_ref, v_ref, b_ref, o_ref, lse_ref):
        n = pl.program_id(1)
        base = pl.multiple_of(_win_base(n, NB) * BLK, BLK)
        lane = _lane((BLK, LANES))
        lo = lane < A_DH
        kv = [_win_kv(k_ref, v_ref, base, S, kvh) for kvh in range(A_KV)]
        scores = []
        for h in range(A_HEADS):
            qp = q_ref[:, LANES * (h // 2):LANES * (h // 2 + 1)]
            qm = (jnp.where(lo if h % 2 == 0 else ~lo, qp, 0.0) * (A_SCALE * LOG2E)).astype(BF16)
            scores.append(_dot_nt(qm, kv[h // G][0]) + b_ref[0, h])
        ms = [jnp.max(s, axis=-1, keepdims=True) for s in scores]
        es = [jnp.exp2(s - m).astype(BF16) for s, m in zip(scores, ms)]
        outs = []
        t = jnp.zeros((BLK, LANES), F32)
        for h in range(A_HEADS):
            o2 = _dot(es[h], kv[h // G][1])
            l = -jnp.sum(jnp.where(lane == A_DH, o2, 0.0), axis=-1, keepdims=True)
            outs.append(o2 * (1.0 / l))
            t = jnp.where(lane == h, ms[h] + jnp.log(l) * LOG2E, t)
        for j in range(A_HEADS // 2):
            o_ref[:, LANES * j:LANES * (j + 1)] = jnp.where(lo, outs[2 * j], pltpu.roll(outs[2 * j + 1], A_DH, 1))
        lse_ref[...] = t

    qrow = lambda b, n: (b * nblk + n, 0)
    return pl.pallas_call(
        body, name=name, grid=(Bl, nblk),
        in_specs=[pl.BlockSpec((BLK, A_W), qrow),
                  pl.BlockSpec((Lp, 256), lambda b, n: (b, KA0 // 256)), pl.BlockSpec((Lp, 256), lambda b, n: (b, VA0 // 256)),
                  pl.BlockSpec((1, A_HEADS, BLK, 4 * BLK), lambda b, n: (_win_variant(n, NB), 0, 0, 0))],
        out_specs=(pl.BlockSpec((BLK, A_W), qrow), pl.BlockSpec((BLK, LANES), qrow)),
        out_shape=(jax.ShapeDtypeStruct((T, A_W), F32), jax.ShapeDtypeStruct((T, LANES), F32)),
        compiler_params=_params("parallel", "arbitrary"),
    )(proj, proj, proj, bias)


def _q_tiles(Lp):
    S = Lp - BLK
    return S // 256, S


def _mla_fwd(q, k, v, Bl, Lp, name):
    T = q.shape[0]
    n_big, S = _q_tiles(Lp)

    def body(q_ref, k_ref, v_ref, o_ref, lse_ref):
        def qtile(r0, nq):
            lane = _lane((nq, LANES))
            outs = []
            t = jnp.zeros((nq, LANES), F32)
            cols = [slice(LANES * hh, LANES * (hh + 1)) for hh in range(2)]
            scores = [_dot_nt(q_ref[pl.ds(r0, nq), cs], k_ref[:, cs]) for cs in cols]
            for hh in range(2):
                cs, s = cols[hh], scores[hh]
                m = jnp.max(s, axis=-1, keepdims=True)
                o2 = _dot(jnp.exp2(s - m).astype(BF16), v_ref[:, cs])
                l = -jnp.sum(jnp.where(lane == SUM_LANE, o2, 0.0), axis=-1, keepdims=True)
                outs.append(o2 * (1.0 / l))
                t = jnp.where(lane == hh, m + jnp.log(l) * LOG2E, t)
            o_ref[pl.ds(r0, nq), :] = jnp.where(lane < B_V, outs[0], pltpu.roll(outs[1], B_V, 1))
            lse_ref[pl.ds(r0, nq), :] = t

        def step(i, c):
            qtile(pl.multiple_of(i * 256, 256), 256)
            return c

        lax.fori_loop(0, n_big, step, 0)
        qtile(S, BLK)

    pair = lambda b, j: (b, j)
    wide, narrow = pl.BlockSpec((Lp, 256), pair), pl.BlockSpec((Lp, LANES), pair)
    return pl.pallas_call(
        body, name=name, grid=(Bl, B_HEADS // 2), in_specs=[wide, wide, wide], out_specs=(narrow, narrow),
        out_shape=(jax.ShapeDtypeStruct((T, B_W), F32), jax.ShapeDtypeStruct((T, B_W), F32)),
        compiler_params=_params("parallel", "parallel"),
    )(q, k, v)


def _gated(y, gate, gain):
    r = _rstd(y)
    nrm = y * r
    sg = jax.nn.sigmoid(gate)
    return r, nrm, sg, nrm * gain, gate * sg


def _out_fwd(ya, yb, proj, na, nb, wout, h, tm, name):
    T = h.shape[0]

    def body(ya_ref, yb_ref, ga_ref, gb_ref, na_ref, nb_ref, w_ref, h_ref, o_ref):
        _, _, _, n_a, sl_a = _gated(ya_ref[...], ga_ref[...], na_ref[...])
        _, _, _, n_b, sl_b = _gated(yb_ref[...], gb_ref[...], nb_ref[...])
        acc = _dot((n_a * sl_a).astype(BF16), w_ref[0:A_W, :]) + _dot((n_b * sl_b).astype(BF16), w_ref[A_W:A_W + B_W, :])
        o_ref[...] = h_ref[...] + acc

    half = pl.BlockSpec((tm, 512), _row(0))
    return pl.pallas_call(
        body, name=name, grid=(T // tm,),
        in_specs=[half, half, pl.BlockSpec((tm, 512), _row(GA0 // 512)), pl.BlockSpec((tm, 512), _row(GB0 // 512)),
                  pl.BlockSpec((1, 512), _const2), pl.BlockSpec((1, 512), _const2), pl.BlockSpec((D, D), _const2),
                  pl.BlockSpec((tm, D), _row(0))],
        out_specs=pl.BlockSpec((tm, D), _row(0)), out_shape=jax.ShapeDtypeStruct((T, D), F32),
        compiler_params=_params("parallel"),
    )(ya, yb, proj, proj, na, nb, wout, h)


def _loss_head(h, gf, target, Bl, S, name):
    T = h.shape[0]
    nblk = (S + BLK) // BLK
    NB = S // BLK

    def body(h_ref, g_ref, t_ref, dh_ref, loss_ref, dg_ref):
        b, n = pl.program_id(0), pl.program_id(1)

        @pl.when((b == 0) & (n == 0))
        def _():
            loss_ref[...] = jnp.zeros_like(loss_ref)
            dg_ref[...] = jnp.zeros_like(dg_ref)

        @pl.when(n < NB)
        def _():
            x = h_ref[...]
            g = g_ref[...]
            r = _rstd(x)
            err = x * r * g - t_ref[0]
            loss_ref[...] += 0.5 * jnp.sum(jnp.mean(err * err, axis=-1, keepdims=True))
            dx, dg = _rms_bwd(err * (1.0 / D), x, r, g)
            dh_ref[...] = dx
            dg_ref[...] += dg

        @pl.when(n >= NB)
        def _():
            dh_ref[...] = jnp.zeros_like(dh_ref)

    return pl.pallas_call(
        body, name=name, grid=(Bl, nblk),
        in_specs=[pl.BlockSpec((BLK, D), lambda b, n: (b * nblk + n, 0)), pl.BlockSpec((1, D), lambda b, n: (0, 0)),
                  pl.BlockSpec((1, BLK, D), lambda b, n: (b, jnp.minimum(n, NB - 1), 0))],
        out_specs=(pl.BlockSpec((BLK, D), lambda b, n: (b * nblk + n, 0)), pl.BlockSpec((8, LANES), lambda b, n: (0, 0)),
                   pl.BlockSpec((1, D), lambda b, n: (0, 0))),
        out_shape=(jax.ShapeDtypeStruct((T, D), F32), jax.ShapeDtypeStruct((8, LANES), F32), jax.ShapeDtypeStruct((1, D), F32)),
        compiler_params=_params("arbitrary", "arbitrary"),
    )(h, gf, target)


def _out_bwd(dh, ya, yb, proj, na, nb, wout, tm, name):
    T = dh.shape[0]

    def body(dh_ref, ya_ref, yb_ref, ga_ref, gb_ref, na_ref, nb_ref, w_ref,
             dya_ref, dyb_ref, dga_ref, dgb_ref, dw_ref, dna_ref, dnb_ref):
        @pl.when(pl.program_id(0) == 0)
        def _():
            dw_ref[...] = jnp.zeros_like(dw_ref)
            dna_ref[...] = jnp.zeros_like(dna_ref)
            dnb_ref[...] = jnp.zeros_like(dnb_ref)

        dhb = dh_ref[...].astype(BF16)
        dy = _dot_nt(dhb, w_ref[...])
        parts = ((ya_ref, ga_ref, na_ref, dya_ref, dga_ref, dna_ref, 0), (yb_ref, gb_ref, nb_ref, dyb_ref, dgb_ref, dnb_ref, A_W))
        for y_ref, gate_ref, gain_ref, dy_out, dgate_out, dgain_out, c0 in parts:
            y, gate, gain = y_ref[...], gate_ref[...], gain_ref[...]
            r, nrm, sg, n_g, sl = _gated(y, gate, gain)
            dyp = dy[:, c0:c0 + 512]
            dgate_out[...] = dyp * n_g * (sg * (1.0 + gate * (1.0 - sg)))
            dx, dgain = _rms_bwd(dyp * sl, y, r, gain)
            dy_out[...] = dx
            dgain_out[...] += dgain
            dw_ref[c0:c0 + 512, :] += _dot_tn((n_g * sl).astype(BF16), dhb)

    half = pl.BlockSpec((tm, 512), _row(0))
    vec = pl.BlockSpec((1, 512), _const2)
    return pl.pallas_call(
        body, name=name, grid=(T // tm,),
        in_specs=[pl.BlockSpec((tm, D), _row(0)), half, half, pl.BlockSpec((tm, 512), _row(GA0 // 512)),
                  pl.BlockSpec((tm, 512), _row(GB0 // 512)), vec, vec, pl.BlockSpec((D, D), _const2)],
        out_specs=(half, half, half, half, pl.BlockSpec((D, D), _const2), vec, vec),
        out_shape=(jax.ShapeDtypeStruct((T, 512), F32),) * 4 + (jax.ShapeDtypeStruct((D, D), F32),)
        + (jax.ShapeDtypeStruct((1, 512), F32),) * 2,
        compiler_params=_params("arbitrary"),
    )(dh, ya, yb, proj, proj, na, nb, wout)


def _mla_bwd(q, k, v, o, do, lse, after, Bl, Lp, name):
    T = q.shape[0]
    n_big, S = _q_tiles(Lp)

    def body(q_ref, k_ref, v_ref, o_ref, do_ref, lse_ref, _, dq_ref, dk_ref, dv_ref):
        dk_ref[...] = jnp.zeros_like(dk_ref)
        dv_ref[...] = jnp.zeros_like(dv_ref)

        def qtile(r0, nq):
            lane = _lane((nq, LANES))
            dop = do_ref[pl.ds(r0, nq), :]
            op = o_ref[pl.ds(r0, nq), :]
            lsev = lse_ref[pl.ds(r0, nq), :]
            for hh in range(2):
                cs = slice(LANES * hh, LANES * (hh + 1))
                dom = jnp.where((lane < B_V) if hh == 0 else (lane >= B_V), dop, 0.0)
                delta = jnp.sum(dom * op, axis=-1, keepdims=True)
                d_hi = delta.astype(BF16).astype(F32)
                x = dom if hh == 0 else pltpu.roll(dom, B_V, 1)
                domx = jnp.where(lane == SUM_LANE, d_hi, jnp.where(lane == SUM_LANE + 1, delta - d_hi, x)).astype(BF16)
                qh, kh, vh = q_ref[pl.ds(r0, nq), cs], k_ref[:, cs], v_ref[:, cs]
                lse_h = jnp.sum(jnp.where(lane == hh, lsev, 0.0), axis=-1, keepdims=True)
                p = jnp.exp2(_dot_nt(qh, kh) - lse_h)
                ds = (p * _dot_nt(domx, vh)).astype(BF16)
                dq_ref[pl.ds(r0, nq), cs] = _dot(ds, kh) * B_SCALE
                dk_ref[:, cs] += _dot_tn(ds, qh)
                dv_ref[:, cs] += _dot_tn(p.astype(BF16), domx)

        def step(i, c):
            qtile(pl.multiple_of(i * 256, 256), 256)
            return c

        lax.fori_loop(0, n_big, step, 0)
        qtile(S, BLK)
        dk_ref[...] = dk_ref[...] * (1.0 / LOG2E)

    pair = lambda b, j: (b, j)
    wide, narrow = pl.BlockSpec((Lp, 256), pair), pl.BlockSpec((Lp, LANES), pair)
    return pl.pallas_call(
        body, name=name, grid=(Bl, B_HEADS // 2),
        in_specs=[wide, wide, wide, narrow, narrow, narrow, pl.BlockSpec(memory_space=pl.ANY)],
        out_specs=(wide, wide, wide), out_shape=(jax.ShapeDtypeStruct((T, 1024), F32),) * 3,
        compiler_params=_params("parallel", "parallel"),
    )(q, k, v, o, do, lse, after)


def _win_bwd(proj, bias, ya, dya, lse, Bl, S, name):
    T = proj.shape[0]
    Lp = S + BLK
    NB = S // BLK
    nblk = Lp // BLK
    G = A_HEADS // A_KV

    def body(q_ref, k_ref, v_ref, b_ref, o_ref, do_ref, lse_ref, dq_ref, dk_ref, dv_ref, db_ref):
        n = pl.program_id(1)

        @pl.when(n == 0)
        def _():
            dk_ref[...] = jnp.zeros_like(dk_ref)
            dv_ref[...] = jnp.zeros_like(dv_ref)

        @pl.when((n == 0) | (n == 1) | (n == NB - 1) | (n == NB))
        def _():
            db_ref[...] = jnp.zeros_like(db_ref)

        base = pl.multiple_of(_win_base(n, NB) * BLK, BLK)
        lane = _lane((BLK, LANES))
        lo = lane < A_DH
        lsev = lse_ref[...]
        kv = [_win_kv(k_ref, v_ref, base, S, kvh) for kvh in range(A_KV)]
        qms, pbs, domxs, dsls = [], [], [], []
        for h in range(A_HEADS):
            k2, v2 = kv[h // G]
            ps = slice(LANES * (h // 2), LANES * (h // 2 + 1))
            qm = (jnp.where(lo if h % 2 == 0 else ~lo, q_ref[:, ps], 0.0) * (A_SCALE * LOG2E)).astype(BF16)
            lse_h = jnp.sum(jnp.where(lane == h, lsev, 0.0), axis=-1, keepdims=True)
            p = jnp.exp2(_dot_nt(qm, k2) + b_ref[0, h] - lse_h)
            dop, op = do_ref[:, ps], o_ref[:, ps]
            if h % 2 == 1:
                dop, op = pltpu.roll(dop, A_DH, 1), pltpu.roll(op, A_DH, 1)
            dom = jnp.where(lo, dop, 0.0)
            delta = jnp.sum(dom * op, axis=-1, keepdims=True)
            d_hi = delta.astype(BF16).astype(F32)
            domx = jnp.where(lane == A_DH, d_hi, jnp.where(lane == A_DH + 1, delta - d_hi, dom)).astype(BF16)
            dsc = p * _dot_nt(domx, v2)
            db_ref[0, 0, h] += dsc
            qms.append(qm)
            pbs.append(p.astype(BF16))
            domxs.append(domx)
            dsls.append(dsc.astype(BF16))
        dqs = [_dot(dsls[h], kv[h // G][0]) * A_SCALE for h in range(A_HEADS)]
        for kvh in range(A_KV):
            cs = slice(LANES * kvh, LANES * (kvh + 1))
            dk_acc = jnp.zeros((4 * BLK, LANES), F32)
            dv_acc = jnp.zeros((4 * BLK, LANES), F32)
            for h in range(kvh * G, (kvh + 1) * G):
                dk_acc = dk_acc + _dot_tn(dsls[h], qms[h])
                dv_acc = dv_acc + _dot_tn(pbs[h], domxs[h])
            dk_acc = dk_acc * (1.0 / LOG2E)
            dv_acc = jnp.where(_lane(dv_acc.shape) < A_DH, dv_acc, 0.0)
            dk_ref[pl.ds(base, 3 * BLK), cs] += dk_acc[0:3 * BLK]
            dk_ref[S:S + BLK, cs] += dk_acc[3 * BLK:4 * BLK]
            dv_ref[pl.ds(base, 3 * BLK), cs] += dv_acc[0:3 * BLK]
            dv_ref[S:S + BLK, cs] += dv_acc[3 * BLK:4 * BLK]
        for j in range(A_HEADS // 2):
            dq_ref[:, LANES * j:LANES * (j + 1)] = jnp.where(lo, dqs[2 * j], dqs[2 * j + 1])

    qrow = lambda b, n: (b * nblk + n, 0)
    kvs = pl.BlockSpec((Lp, 256), lambda b, n: (b, 0))
    return pl.pallas_call(
        body, name=name, grid=(Bl, nblk),
        in_specs=[pl.BlockSpec((BLK, A_W), qrow),
                  pl.BlockSpec((Lp, 256), lambda b, n: (b, KA0 // 256)), pl.BlockSpec((Lp, 256), lambda b, n: (b, VA0 // 256)),
                  pl.BlockSpec((1, A_HEADS, BLK, 4 * BLK), lambda b, n: (_win_variant(n, NB), 0, 0, 0)),
                  pl.BlockSpec((BLK, A_W), qrow), pl.BlockSpec((BLK, A_W), qrow), pl.BlockSpec((BLK, LANES), qrow)],
        out_specs=(pl.BlockSpec((BLK, A_W), qrow), kvs, kvs,
                   pl.BlockSpec((1, 1, A_HEADS, BLK, 4 * BLK), lambda b, n: (b, _win_variant(n, NB), 0, 0, 0))),
        out_shape=(jax.ShapeDtypeStruct((T, A_W), F32), jax.ShapeDtypeStruct((T, 256), F32),
                   jax.ShapeDtypeStruct((T, 256), F32), jax.ShapeDtypeStruct((Bl, 4, A_HEADS, BLK, 4 * BLK), F32)),
        compiler_params=_params("parallel", "arbitrary"),
    )(proj, proj, proj, bias, ya, dya, lse)


def _table_grad(dbias_list, buckets, after, name):
    nl = len(dbias_list)
    Bl = dbias_list[0].shape[0]

    def body(*refs):
        d_refs = refs[:nl]
        b_ref, _, o_ref, s_ref, acc, part = refs[nl:]
        v, b = pl.program_id(0), pl.program_id(1)

        @pl.when((v == 0) & (b == 0))
        def _():
            o_ref[...] = jnp.zeros_like(o_ref)
            s_ref[...] = jnp.zeros_like(s_ref)

        slot = _lane((BLK, 4 * BLK))
        for l in range(nl):
            for h in range(A_HEADS):
                col = jnp.sum(jnp.where(slot == SINK_SLOT, d_refs[l][0, 0, h], 0.0), axis=0, keepdims=True)
                s_ref[l, h:h + 1, :] += jnp.sum(col, axis=1, keepdims=True)

        tot = d_refs[0][0, 0]
        for r in d_refs[1:]:
            tot = tot + r[0, 0]

        @pl.when(b == 0)
        def _():
            acc[...] = tot

        @pl.when(b > 0)
        def _():
            acc[...] += tot

        @pl.when(b == Bl - 1)
        def _():
            bidx = b_ref[0]

            def step(j, c):
                mask = bidx == j
                for h in range(A_HEADS):
                    part[h, j] = jnp.sum(jnp.where(mask, acc[h], 0.0).reshape(BLK // 8, 8, 4 * BLK), axis=0)
                return c

            lax.fori_loop(0, N_BUCKETS, step, 0)
            rows = lax.broadcasted_iota(jnp.int32, (N_BUCKETS, LANES), 0)
            lanes = _lane((N_BUCKETS, LANES))
            for h in range(A_HEADS):
                col = jnp.sum(jnp.sum(part[h], axis=1), axis=-1, keepdims=True)
                o_ref[h:h + 1, :] += jnp.sum(jnp.where(rows == lanes, col, 0.0), axis=0, keepdims=True)

    return pl.pallas_call(
        body, name=name, grid=(4, Bl),
        in_specs=[pl.BlockSpec((1, 1, A_HEADS, BLK, 4 * BLK), lambda v, b: (b, v, 0, 0, 0))] * nl
        + [pl.BlockSpec((1, BLK, 4 * BLK), lambda v, b: (v, 0, 0)), pl.BlockSpec(memory_space=pl.ANY)],
        out_specs=(pl.BlockSpec((8, LANES), lambda v, b: (0, 0)), pl.BlockSpec((nl, 8, LANES), lambda v, b: (0, 0, 0))),
        out_shape=(jax.ShapeDtypeStruct((8, LANES), F32), jax.ShapeDtypeStruct((nl, 8, LANES), F32)),
        scratch_shapes=[pltpu.VMEM((A_HEADS, BLK, 4 * BLK), F32), pltpu.VMEM((A_HEADS, N_BUCKETS, 8, 4 * BLK), F32)],
        compiler_params=_params("arbitrary", "arbitrary"),
    )(*dbias_list, buckets, after)


def _mla_prep_bwd(dq, dk, dv, proj, gq, gkv, wq, wkk, wkv, tq, tk, tm, Lp, name):
    T = proj.shape[0]
    nt = Lp // tm

    def body(dq_ref, dk_ref, dv_ref, cq_ref, ckv_ref, gq_ref, gkv_ref, wq_ref, wkk_ref, wkv_ref, tq_ref, tk_ref,
             dcq_ref, dckv_ref, dkr_ref, dwq_ref, dwkk_ref, dwkv_ref, dgq_ref, dgkv_ref):
        @pl.when(pl.program_id(0) == 0)
        def _():
            for r in (dwq_ref, dwkk_ref, dwkv_ref, dgq_ref, dgkv_ref):
                r[...] = jnp.zeros_like(r)

        tqv = tq_ref[...]
        dqp = jnp.concatenate([_rope_t(dq_ref[:, LANES * h:LANES * (h + 1)], tqv) for h in range(B_HEADS)],
                              axis=1).astype(BF16)
        cq, gq_ = cq_ref[...], gq_ref[...]
        rq = _rstd(cq)
        dwq_ref[...] += _dot_tn((cq * rq * gq_).astype(BF16), dqp)
        dx, dg = _rms_bwd(_dot_nt(dqp, wq_ref[...]), cq, rq, gq_)
        dcq_ref[...] = dx
        dgq_ref[...] += dg

        dkv_ = dk_ref[...]
        dks = dkv_[:, 0:LANES]
        for h in range(1, B_HEADS):
            dks = dks + dkv_[:, LANES * h:LANES * (h + 1)]
        dkr_ref[...] = _rope_t(pltpu.roll(dks, NOPE, 1), tk_ref[...])
        dkb = dkv_.astype(BF16)
        dvb = dv_ref[...].astype(BF16)
        ckv, gkv_ = ckv_ref[...], gkv_ref[...]
        rk = _rstd(ckv)
        cb = (ckv * rk * gkv_).astype(BF16)
        dwkk_ref[...] += _dot_tn(cb, dkb)
        dwkv_ref[...] += _dot_tn(cb, dvb)
        dx, dg = _rms_bwd(_dot_nt(dkb, wkk_ref[...]) + _dot_nt(dvb, wkv_ref[...]), ckv, rk, gkv_)
        dckv_ref[...] = dx
        dgkv_ref[...] += dg

    tab = lambda n: pl.BlockSpec((n, tm, LANES), lambda i: (0, i % nt, 0))
    wide = pl.BlockSpec((tm, 1024), _row(0))
    return pl.pallas_call(
        body, name=name, grid=(T // tm,),
        in_specs=[wide, wide, wide, pl.BlockSpec((tm, 256), _row(CQ0 // 256)),
                  pl.BlockSpec((tm, 128), _row(CKV0 // 128)), pl.BlockSpec((1, Q_RANK), _const2),
                  pl.BlockSpec((1, KV_RANK), _const2), pl.BlockSpec((Q_RANK, 1024), _const2),
                  pl.BlockSpec((KV_RANK, 1024), _const2), pl.BlockSpec((KV_RANK, 1024), _const2), tab(3), tab(4)],
        out_specs=(pl.BlockSpec((tm, 256), _row(0)), pl.BlockSpec((tm, 128), _row(0)), pl.BlockSpec((tm, 128), _row(0)),
                   pl.BlockSpec((Q_RANK, 1024), _const2), pl.BlockSpec((KV_RANK, 1024), _const2),
                   pl.BlockSpec((KV_RANK, 1024), _const2), pl.BlockSpec((1, Q_RANK), _const2),
                   pl.BlockSpec((1, KV_RANK), _const2)),
        out_shape=(jax.ShapeDtypeStruct((T, 256), F32), jax.ShapeDtypeStruct((T, 128), F32),
                   jax.ShapeDtypeStruct((T, 128), F32), jax.ShapeDtypeStruct((Q_RANK, 1024), F32),
                   jax.ShapeDtypeStruct((KV_RANK, 1024), F32), jax.ShapeDtypeStruct((KV_RANK, 1024), F32),
                   jax.ShapeDtypeStruct((1, Q_RANK), F32), jax.ShapeDtypeStruct((1, KV_RANK), F32)),
        compiler_params=_params("arbitrary"),
    )(dq, dk, dv, proj, proj, gq, gkv, wq, wkk, wkv, tq, tk)


def _inproj_bwd(pieces, h, g, wx, dh_out, tm, name):
    T = h.shape[0]
    nsteps = T // tm
    CH = 512

    def body(*refs):
        p_refs = refs[:len(pieces)]
        h_ref, g_ref, w_ref, dho_ref, dh_ref, dw_hbm, dg_ref, acc, sem = refs[len(pieces):]
        i = pl.program_id(0)

        @pl.when(i == 0)
        def _():
            acc[...] = jnp.zeros_like(acc)
            dg_ref[...] = jnp.zeros_like(dg_ref)

        dp = jnp.concatenate([r[...].astype(BF16) for r in p_refs], axis=1)
        x, gain = h_ref[...], g_ref[...]
        r = _rstd(x)
        u = (x * r * gain).astype(BF16)
        for c in range(0, NX, CH):
            acc[:, c:c + CH] += _dot_tn(u, dp[:, c:c + CH])
        dx, dg = _rms_bwd(_dot_nt(dp, w_ref[...]), x, r, gain)
        dh_ref[...] = dho_ref[...] + dx
        dg_ref[...] += dg

        @pl.when(i == nsteps - 1)
        def _():
            cp = pltpu.make_async_copy(acc, dw_hbm, sem)
            cp.start()
            cp.wait()

    return pl.pallas_call(
        body, name=name, grid=(nsteps,),
        in_specs=[pl.BlockSpec((tm, p.shape[1]), _row(0)) for p in pieces]
        + [pl.BlockSpec((tm, D), _row(0)), pl.BlockSpec((1, D), _const2), pl.BlockSpec((D, NX), _const2),
           pl.BlockSpec((tm, D), _row(0))],
        out_specs=(pl.BlockSpec((tm, D), _row(0)), pl.BlockSpec(memory_space=pl.ANY), pl.BlockSpec((1, D), _const2)),
        out_shape=(jax.ShapeDtypeStruct((T, D), F32), jax.ShapeDtypeStruct((D, NX), F32), jax.ShapeDtypeStruct((1, D), F32)),
        scratch_shapes=[pltpu.VMEM((D, NX), F32), pltpu.SemaphoreType.DMA(())],
        compiler_params=_params("arbitrary"),
    )(*pieces, h, g, wx, dh_out)


def kernel(x, meta_tokens, rel_bias_table, norm_in, w_in, sink_a, norm_q_lat, w_uq, norm_kv_lat, w_ukv, norm_out_a, norm_out_b, w_out, norm_final, loss_target, m_meta_tokens, m_rel_bias_table, m_norm_in, m_w_in, m_sink_a, m_norm_q_lat, m_w_uq, m_norm_kv_lat, m_w_ukv, m_norm_out_a, m_norm_out_b, m_w_out, m_norm_final, v_meta_tokens, v_rel_bias_table, v_norm_in, v_w_in, v_sink_a, v_norm_q_lat, v_w_uq, v_norm_kv_lat, v_w_ukv, v_norm_out_a, v_norm_out_b, v_w_out, v_norm_final):
    Bl, S, _ = x.shape
    Lp = S + BLK
    T = Bl * Lp
    tm = Lp // 4
    tq = Lp // 8
    depth = w_in.shape[0]
    me = 4 * lax.axis_index("x") + 2 * lax.axis_index("y") + lax.axis_index("c")

    def finish(handles, after, name):
        srcs, lands = _xchg_wait(handles, after, name)
        full = []
        for hd, src, land in zip(handles, srcs, lands):
            own = src if hd[4] == "gather" else lax.dynamic_index_in_dim(src, me, 0, keepdims=False)
            full.append(_own_slot(land, own, me))
        return full

    def weights_of(i):
        return [w_in[i].astype(BF16), w_uq[i].astype(BF16), w_ukv[i].astype(BF16), w_out[i].astype(BF16)]

    def packed(l_in, l_uq, l_ukv, l_out):
        cols = lambda t: jnp.transpose(t, (1, 0, 2)).reshape(t.shape[1], NDEV * t.shape[2])
        wkk, wkv = _pack_w_ukv(cols(l_ukv))
        return _pack_w_in(cols(l_in)), _pack_w_uq(cols(l_uq)), wkk, wkv, l_out.reshape(D, D)

    wb = weights_of(0)
    gat_a, tok = _xchg_start([wb[0], meta_tokens], ["gather", "gather"], "gather_start_0a")
    buckets = _bias_buckets(S)
    biases = _build_bias(buckets, _after(rel_bias_table, tok), sink_a, "build_bias")
    tq_tab = _rope_tables(S, Lp, NOPE, True)
    key_pad = jnp.where((jnp.arange(Lp) >= S + NMETA)[:, None] & (jnp.arange(LANES) == MASK_LANE)[None, :], NEG, 0.0)
    tk_tab = jnp.concatenate([_rope_tables(S, Lp, 0, False), key_pad.astype(F32)[None]], axis=0)
    l_in, l_meta = finish(gat_a, biases[0], "gather_wait_0a")
    gat_b, tok = _xchg_start(wb[1:], ["gather"] * 3, "gather_start_0b", after=l_in)

    meta_f = jnp.transpose(l_meta, (1, 0, 2)).reshape(NMETA, D)
    tail = jnp.concatenate([meta_f, jnp.zeros((BLK - NMETA, D), F32)], axis=0)
    h = jnp.concatenate([x, jnp.broadcast_to(tail[None], (Bl, BLK, D))], axis=1).reshape(T, D)

    saved = []
    w_out_f = [None] * depth
    gat_next = None
    for i in range(depth):
        g_in, g_q, g_kv = norm_in[i][None], norm_q_lat[i][None], norm_kv_lat[i][None]
        g_a, g_b = norm_out_a[i][None], norm_out_b[i][None]
        if i == 0:
            proj = _inproj_fwd(h, _after(g_in, tok), _pack_w_in(jnp.transpose(l_in, (1, 0, 2)).reshape(D, IN_W)), tm,
                               f"inproj_fwd_{i}")
            l_uq, l_ukv, l_out = finish(gat_b, proj, "gather_wait_0b")
            wx, wq, wkk, wkv, w_out_f[i] = packed(l_in, l_uq, l_ukv, l_out)
        else:
            wx, wq, wkk, wkv, w_out_f[i] = packed(*finish(gat_next, h, f"gather_wait_{i}"))
            proj = _inproj_fwd(h, g_in, wx, tm, f"inproj_fwd_{i}")
        if i + 1 < depth:
            gat_next, tok = _xchg_start(weights_of(i + 1), ["gather"] * 4, f"gather_start_{i + 1}", after=w_out_f[i])
            g_q = _after(g_q, tok)
        q, k, v = _mla_prep_fwd(proj, g_q, g_kv, wq, wkk, wkv, tq_tab, tk_tab, tm, Lp, f"mla_prep_fwd_{i}")
        ya, lse_a = _win_fwd(proj, biases[i], Bl, S, f"win_fwd_{i}")
        yb, lse_b = _mla_fwd(q, k, v, Bl, Lp, f"mla_fwd_{i}")
        h_new = _out_fwd(ya, yb, proj, g_a, g_b, w_out_f[i], h, tm, f"out_fwd_{i}")
        saved.append((h, proj, q, k, v, ya, lse_a, yb, lse_b, wx, wq, wkk, wkv))
        h = h_new

    dh, loss_acc, dg_final = _loss_head(h, norm_final[None], loss_target, Bl, S, "loss_head")
    loss = lax.psum(loss_acc[0, 0], ("x", "y", "c"))

    g_n_in, g_nq, g_nkv, g_na, g_nb = ([None] * depth for _ in range(5))
    dbias_all, sc_out, sc_rest = [None] * depth, [None] * depth, [None] * depth
    split = lambda t, n: jnp.transpose(t.reshape(t.shape[0], NDEV, n), (1, 0, 2)).astype(BF16)
    tok = None
    for i in reversed(range(depth)):
        h_in, proj, q, k, v, ya, lse_a, yb, lse_b, wx, wq, wkk, wkv = saved[i]
        g_in, g_q, g_kv = norm_in[i][None], norm_q_lat[i][None], norm_kv_lat[i][None]
        g_a, g_b = norm_out_a[i][None], norm_out_b[i][None]
        if tok is not None:
            g_a = _after(g_a, tok)
        dya, dyb, dga, dgb, g_w_out, g_na[i], g_nb[i] = _out_bwd(dh, ya, yb, proj, g_a, g_b, w_out_f[i], tm, f"out_bwd_{i}")
        sc_out[i], tok = _xchg_start([g_w_out.reshape(NDEV, D // NDEV, D).astype(BF16)], ["scatter"], f"scatter_start_{i}a")
        dq, dk, dv = _mla_bwd(q, k, v, yb, dyb, lse_b, tok, Bl, Lp, f"mla_bwd_{i}")
        dqa, dka, dva, dbias = _win_bwd(proj, biases[i], ya, dya, lse_a, Bl, S, f"win_bwd_{i}")
        dcq, dckv, dkr, dwq, dwkk, dwkv, g_nq[i], g_nkv[i] = _mla_prep_bwd(
            dq, dk, dv, proj, g_q, g_kv, wq, wkk, wkv, tq_tab, tk_tab, tm, Lp, f"mla_prep_bwd_{i}")
        dh, dwx, g_n_in[i] = _inproj_bwd((dqa, dka, dva, dga, dcq, dckv, dkr, dgb), h_in, g_in, wx, dh, tq,
                                         f"inproj_bwd_{i}")
        dbias_all[i] = dbias
        rest = [split(_unpack_w_in_grad(dwx), IN_W // NDEV), split(_unpack_w_uq_grad(dwq), 768 // NDEV),
                split(_unpack_w_ukv_grad(dwkk, dwkv), 1024 // NDEV)]
        if i == 0:
            dh3 = dh.reshape(Bl, Lp, D)
            grad_x = dh3[:, :S]
            rest.append(jnp.transpose(jnp.sum(dh3[:, S:S + NMETA], axis=0).reshape(NMETA, NDEV, D // NDEV), (1, 0, 2)))
        sc_rest[i], tok = _xchg_start(rest, ["scatter"] * len(rest), f"scatter_start_{i}b")

    dtab, dsink = _table_grad(dbias_all, buckets, tok, "table_grad")
    two_d = lambda a: a.reshape(1, -1) if a.ndim == 1 else a
    small_w = [rel_bias_table, norm_in, sink_a, norm_q_lat, norm_kv_lat, norm_out_a, norm_out_b, norm_final]
    small_m = [m_rel_bias_table, m_norm_in, m_sink_a, m_norm_q_lat, m_norm_kv_lat, m_norm_out_a, m_norm_out_b, m_norm_final]
    small_v = [v_rel_bias_table, v_norm_in, v_sink_a, v_norm_q_lat, v_norm_kv_lat, v_norm_out_a, v_norm_out_b, v_norm_final]
    small_g = [jnp.transpose(dtab[:, :N_BUCKETS]), jnp.concatenate(g_n_in), dsink[:, :, 0], jnp.concatenate(g_nq),
               jnp.concatenate(g_nkv), jnp.concatenate(g_na), jnp.concatenate(g_nb), dg_final]
    sc_small, tok = _xchg_start(small_g, ["gather"] * len(small_g), "scatter_start_small")

    r_out, r_in, r_uq, r_ukv = [None] * depth, [None] * depth, [None] * depth, [None] * depth
    for i in reversed(range(depth)):
        (r_out[i],) = finish(sc_out[i], tok, f"scatter_wait_{i}a")
        got = finish(sc_rest[i], tok, f"scatter_wait_{i}b")
        r_in[i], r_uq[i], r_ukv[i] = got[:3]
        if i == 0:
            r_meta = got[3]
    r_small = finish(sc_small, tok, "scatter_wait_small")

    res = {}
    sharded = dict(w_in=(r_in, w_in, m_w_in, v_w_in), w_uq=(r_uq, w_uq, m_w_uq, v_w_uq), w_ukv=(r_ukv, w_ukv, m_w_ukv, v_w_ukv),
                   w_out=(r_out, w_out, m_w_out, v_w_out),
                   meta_tokens=([r_meta], meta_tokens[None], m_meta_tokens[None], v_meta_tokens[None]))
    for n, (r, w, m, v) in sharded.items():
        outs = _adamw_param(r, w, m, v, f"adamw_{n}")
        res[n] = [o[0] for o in outs] if n == "meta_tokens" else list(outs)
    names_small = ["rel_bias_table", "norm_in", "sink_a", "norm_q_lat", "norm_kv_lat", "norm_out_a", "norm_out_b", "norm_final"]
    outs = _adamw_small(r_small, [two_d(a) for a in small_w], [two_d(a) for a in small_m], [two_d(a) for a in small_v],
                        "adamw_replicated")
    ns = len(names_small)
    for j, n in enumerate(names_small):
        res[n] = [outs[kk * ns + j].reshape(small_w[j].shape) for kk in range(4)]
    order = ["meta_tokens", "rel_bias_table", "norm_in", "w_in", "sink_a", "norm_q_lat", "w_uq", "norm_kv_lat", "w_ukv",
             "norm_out_a", "norm_out_b", "w_out", "norm_final"]
    return (loss, grad_x, *[res[n][kk] for kk in range(4) for n in order])
```

```python
import functools
import math

import numpy as np
import jax
import jax.numpy as jnp
from jax import lax
from jax.experimental import pallas as pl
from jax.experimental.pallas import tpu as pltpu

F32, BF16 = jnp.float32, jnp.bfloat16
D = 1024
NMETA = 16
BLK = 128
A_HEADS, A_KV, A_DH, A_W = 8, 2, 64, 512
B_HEADS, NOPE, ROPE, B_V, B_W = 8, 64, 32, 64, 512
Q_RANK, KV_RANK = 256, 128
IN_W = 2208
N_BUCKETS, MAX_DIST = 32, 128
THETA = 10000.0
EPS = 1e-6
NEG = -1e30
A_SCALE = A_DH ** -0.5
B_SCALE = (NOPE + ROPE) ** -0.5
LOG2E = math.log2(math.e)
MASK_LANE = NOPE + ROPE
SUM_LANE = B_V
LANES = 128
NDEV = 8
MESH = pl.DeviceIdType.MESH

NX = 2560
QA0, KA0, VA0, GA0, CQ0, CKV0, KR0, GB0 = 0, 512, 768, 1024, 1536, 1792, 1920, 2048

ADAM_LR, ADAM_B1, ADAM_B2, ADAM_EPS, ADAM_WD, ADAM_STEP = 0.001, 0.9, 0.999, 1e-08, 0.01, 10


def _dot(a, b):
    return jnp.dot(a, b, preferred_element_type=F32)


def _dot_nt(a, b):
    return lax.dot_general(a, b, (((1,), (1,)), ((), ())), preferred_element_type=F32)


def _dot_tn(a, b):
    return lax.dot_general(a, b, (((0,), (0,)), ((), ())), preferred_element_type=F32)


def _lane(shape):
    return lax.broadcasted_iota(jnp.int32, shape, len(shape) - 1)


def _rstd(x):
    return lax.rsqrt(jnp.mean(x * x, axis=-1, keepdims=True) + EPS)


def _rms_bwd(dn, x, r, g):
    z = dn * g
    dx = r * z - x * (r * r * r) * jnp.mean(z * x, axis=-1, keepdims=True)
    return dx, jnp.sum(dn * (x * r), axis=0, keepdims=True)


def _row(i):
    return lambda *ids: (ids[0], i)


def _const2(*ids):
    return (0, 0)


def _params(*sem):
    return pltpu.CompilerParams(dimension_semantics=sem)


def _pack_w_in(w):
    qa, ka, va, ga = w[:, 0:512], w[:, 512:640], w[:, 640:768], w[:, 768:1280]
    cq, ckv, kr, gb = w[:, 1280:1536], w[:, 1536:1664], w[:, 1664:1696], w[:, 1696:2208]
    dup = lambda t: jnp.concatenate([t[:, 0:64], t[:, 0:64], t[:, 64:128], t[:, 64:128]], axis=1)
    krp = jnp.concatenate([kr, jnp.zeros((w.shape[0], LANES - ROPE), w.dtype)], axis=1)
    return jnp.concatenate([qa, dup(ka), dup(va), ga, cq, ckv, krp, gb], axis=1)


def _unpack_w_in_grad(g):
    fold = lambda t: jnp.concatenate([t[:, 0:64] + t[:, 64:128], t[:, 128:192] + t[:, 192:256]], axis=1)
    return jnp.concatenate([g[:, QA0:QA0 + 512], fold(g[:, KA0:KA0 + 256]), fold(g[:, VA0:VA0 + 256]),
                            g[:, GA0:GA0 + 512], g[:, CQ0:CQ0 + 256], g[:, CKV0:CKV0 + 128],
                            g[:, KR0:KR0 + ROPE], g[:, GB0:GB0 + 512]], axis=1)


def _pack_w_uq(w):
    t = w.reshape(Q_RANK, B_HEADS, NOPE + ROPE)
    t = jnp.concatenate([t, jnp.zeros((Q_RANK, B_HEADS, LANES - NOPE - ROPE), w.dtype)], axis=-1)
    return t.reshape(Q_RANK, B_HEADS * LANES)


def _unpack_w_uq_grad(g):
    return g.reshape(Q_RANK, B_HEADS, LANES)[:, :, :NOPE + ROPE].reshape(Q_RANK, B_HEADS * (NOPE + ROPE))


def _pack_w_ukv(w):
    t = w.reshape(KV_RANK, B_HEADS, NOPE + B_V)
    z = jnp.zeros((KV_RANK, B_HEADS, LANES - NOPE), w.dtype)
    pad = lambda u: jnp.concatenate([u, z], axis=-1).reshape(KV_RANK, B_HEADS * LANES)
    return pad(t[:, :, :NOPE]), pad(t[:, :, NOPE:])


def _unpack_w_ukv_grad(gk, gv):
    head = lambda g: g.reshape(KV_RANK, B_HEADS, LANES)[:, :, :NOPE]
    return jnp.concatenate([head(gk), head(gv)], axis=-1).reshape(KV_RANK, B_HEADS * (NOPE + B_V))


def _t5_bucket(rel):
    nb = N_BUCKETS // 2
    max_exact = nb // 2
    ret = jnp.where(rel > 0, nb, 0)
    n = jnp.abs(rel)
    nf = jnp.maximum(n, 1).astype(F32)
    large = max_exact + (jnp.log(nf / max_exact) / math.log(MAX_DIST / max_exact) * (nb - max_exact)).astype(jnp.int32)
    large = jnp.minimum(large, nb - 1)
    return ret + jnp.where(n < max_exact, n, large)


def _bias_buckets(S):
    q = np.arange(BLK)[:, None]
    k = np.arange(4 * BLK)[None, :]
    is_meta_key = (k >= 3 * BLK) & (k < 3 * BLK + NMETA)
    mi = k - 3 * BLK
    rels, valids = [], []
    for shift in (0, BLK, 2 * BLK):
        rel_real = k - shift - q
        valid_real = (k < 3 * BLK) & (np.abs(rel_real) <= BLK)
        far = -(NMETA + MAX_DIST + BLK)
        rel_meta = (mi - (NMETA + q)) if shift == 0 else np.full_like(k + q, far)
        rels.append(np.where(is_meta_key, rel_meta, rel_real))
        valids.append(valid_real | is_meta_key)
    qm = q < NMETA
    rel_real = NMETA + k - q
    valid_real = (k < BLK) & (np.abs(rel_real) <= BLK) & qm
    rels.append(np.where(is_meta_key, mi - q, rel_real))
    valids.append(valid_real | is_meta_key)
    rel = jnp.asarray(np.stack(rels).astype(np.int32))
    valid = jnp.asarray(np.stack(valids))
    return jnp.where(valid, _t5_bucket(rel), -1).astype(jnp.int32)


def _rope_tables(S, Lp, off, passthrough):
    half = ROPE // 2
    r = np.arange(Lp)
    pos = np.where(r < S, NMETA + r, np.where(r < S + NMETA, r - S, 0)).astype(np.float32)
    freqs = THETA ** (-jnp.arange(half, dtype=F32) / half)
    ang = jnp.asarray(pos)[:, None] * freqs[None, :]
    cos, sin = jnp.cos(ang), jnp.sin(ang)
    z = lambda n: jnp.zeros((Lp, n), F32)
    head = jnp.ones((Lp, off), F32) if passthrough else z(off)
    c = jnp.concatenate([head, cos, cos, z(LANES - off - ROPE)], axis=1)
    s1 = jnp.concatenate([z(off), -sin, z(LANES - off - half)], axis=1)
    s2 = jnp.concatenate([z(off + half), sin, z(LANES - off - ROPE)], axis=1)
    return jnp.stack([c, s1, s2])


def _rope(x, t):
    return x * t[0] + pltpu.roll(x, LANES - 16, 1) * t[1] + pltpu.roll(x, 16, 1) * t[2]


def _rope_t(dy, t):
    return dy * t[0] + pltpu.roll(dy * t[1], 16, 1) + pltpu.roll(dy * t[2], LANES - 16, 1)


def _me_and_peers():
    x, y, c = lax.axis_index("x"), lax.axis_index("y"), lax.axis_index("c")
    flip = lambda v, b: 1 - v if b else v
    peers = [(flip(x, k & 4), flip(y, k & 2), flip(c, k & 1)) for k in range(1, NDEV)]
    return 4 * x + 2 * y + c, peers


def _xchg_start(srcs, kinds, name, after=None):
    n = len(srcs)
    lands = [lax.empty((NDEV,) + s.shape[-2:], s.dtype) for s in srcs]
    extra = [] if after is None else [after]

    def body(*refs):
        src_refs, land_refs = refs[:n], refs[n:2 * n]
        ssems, rsems = refs[2 * n + len(extra):3 * n + len(extra)], refs[3 * n + len(extra):4 * n + len(extra)]
        token = refs[6 * n + len(extra)]
        me, peers = _me_and_peers()
        for it in range(n):
            for k, (px, py, pc) in enumerate(peers):
                src = src_refs[it] if kinds[it] == "gather" else src_refs[it].at[4 * px + 2 * py + pc]
                pltpu.make_async_remote_copy(src_ref=src, dst_ref=land_refs[it].at[me], send_sem=ssems[it].at[k],
                                             recv_sem=rsems[it].at[k], device_id=(px, py, pc), device_id_type=MESH).start()
        token[...] = jnp.zeros_like(token)

    hbm = pl.BlockSpec(memory_space=pltpu.HBM)
    sem = pl.BlockSpec(memory_space=pltpu.SEMAPHORE)
    outs = pl.pallas_call(
        body, name=name,
        out_shape=tuple([pltpu.SemaphoreType.DMA((NDEV - 1,))] * (2 * n) + [pltpu.HBM(a.shape, a.dtype) for a in srcs + lands]
                        + [jax.ShapeDtypeStruct((8, LANES), F32)]),
        in_specs=[hbm] * (2 * n) + [pl.BlockSpec(memory_space=pl.ANY)] * len(extra),
        out_specs=tuple([sem] * (2 * n) + [hbm] * (2 * n) + [pl.BlockSpec(memory_space=pltpu.VMEM)]),
        input_output_aliases={i: 2 * n + i for i in range(2 * n)},
        compiler_params=pltpu.CompilerParams(has_side_effects=pltpu.SideEffectType.DATAFLOW_SIDE_EFFECTING),
    )(*[pltpu.with_memory_space_constraint(a, pltpu.HBM) for a in srcs + lands], *extra)
    handles = [(outs[it], outs[n + it], outs[2 * n + it], outs[3 * n + it], kinds[it]) for it in range(n)]
    return handles, outs[4 * n]


def _xchg_wait(handles, after, name):
    n = len(handles)

    def body(*refs):
        src_refs, land_refs = refs[:n], refs[n:2 * n]
        ssems, rsems = refs[2 * n:3 * n], refs[3 * n:4 * n]
        me, peers = _me_and_peers()
        for it in range(n):
            for k, (px, py, pc) in enumerate(peers):
                src = src_refs[it] if handles[it][4] == "gather" else src_refs[it].at[4 * px + 2 * py + pc]
                cp = pltpu.make_async_remote_copy(src_ref=src, dst_ref=land_refs[it].at[me], send_sem=ssems[it].at[k],
                                                  recv_sem=rsems[it].at[k], device_id=(px, py, pc), device_id_type=MESH)
                cp.wait_send()
                cp.wait_recv()

    hbm = pl.BlockSpec(memory_space=pltpu.HBM)
    sem = pl.BlockSpec(memory_space=pltpu.SEMAPHORE)
    srcs, lands = [h[2] for h in handles], [h[3] for h in handles]
    outs = pl.pallas_call(
        body, name=name, out_shape=tuple(pltpu.HBM(a.shape, a.dtype) for a in srcs + lands),
        in_specs=[hbm] * (2 * n) + [sem] * (2 * n) + [pl.BlockSpec(memory_space=pl.ANY)], out_specs=tuple([hbm] * (2 * n)),
        input_output_aliases={i: i for i in range(2 * n)},
        compiler_params=pltpu.CompilerParams(has_side_effects=pltpu.SideEffectType.DATAFLOW_SIDE_EFFECTING),
    )(*srcs, *lands, *[h[0] for h in handles], *[h[1] for h in handles], after)
    return list(outs[:n]), list(outs[n:])


def _own_slot(land, own, me):
    return lax.dynamic_update_slice(land, own[None].astype(land.dtype), (me, 0, 0))


def _after(x, token):
    return x + token[0, 0]


def _adamw_update(g, w, m, v):
    mn = ADAM_B1 * m + (1.0 - ADAM_B1) * g
    vn = ADAM_B2 * v + (1.0 - ADAM_B2) * (g * g)
    m_hat = mn / (1.0 - ADAM_B1 ** ADAM_STEP)
    v_hat = vn / (1.0 - ADAM_B2 ** ADAM_STEP)
    return -ADAM_LR * (m_hat / (jnp.sqrt(v_hat) + ADAM_EPS) + ADAM_WD * w), mn, vn


def _sum_partials(r_ref):
    g = r_ref[0].astype(F32)
    for p in range(1, NDEV):
        g = g + r_ref[p].astype(F32)
    return g


def _adamw_param(recvs, w, m, v, name):
    depth, r, c = w.shape
    tr = min(r, 256)
    nl = len(recvs)

    def body(*refs):
        r_refs = refs[:nl]
        w_ref, m_ref, v_ref, g_out, d_out, m_out, v_out = refs[nl:]
        for li in range(nl):
            @pl.when(pl.program_id(0) == li)
            def _(li=li):
                g = _sum_partials(r_refs[li])
                g_out[0] = g
                d_out[0], m_out[0], v_out[0] = _adamw_update(g, w_ref[0], m_ref[0], v_ref[0])

    blk = pl.BlockSpec((1, tr, c), lambda l, i: (l, i, 0))
    return pl.pallas_call(
        body, name=name, grid=(depth, r // tr),
        in_specs=[pl.BlockSpec((NDEV, tr, c), lambda l, i: (0, i, 0))] * nl + [blk, blk, blk],
        out_specs=(blk, blk, blk, blk), out_shape=(jax.ShapeDtypeStruct(w.shape, F32),) * 4,
        compiler_params=_params("arbitrary", "arbitrary"),
    )(*recvs, w, m, v)


def _adamw_small(recvs, ws, ms, vs, loss_parts, name):
    n = len(ws)

    def body(*refs):
        r_refs, w_refs, m_refs, v_refs = refs[:n], refs[n:2 * n], refs[2 * n:3 * n], refs[3 * n:4 * n]
        outs = refs[4 * n + 1:]
        for j in range(n):
            g = _sum_partials(r_refs[j])
            outs[j][...] = g
            outs[n + j][...], outs[2 * n + j][...], outs[3 * n + j][...] = _adamw_update(
                g, w_refs[j][...], m_refs[j][...], v_refs[j][...])
        outs[4 * n][...] = _sum_partials(refs[4 * n])

    vm = pl.BlockSpec(memory_space=pltpu.VMEM)
    return pl.pallas_call(
        body, name=name, in_specs=[vm] * (4 * n + 1), out_specs=tuple([vm] * (4 * n + 1)),
        out_shape=tuple(jax.ShapeDtypeStruct(a.shape, F32) for a in ws) * 4 + (jax.ShapeDtypeStruct(loss_parts.shape[1:], F32),),
    )(*recvs, *ws, *ms, *vs, loss_parts)


def _inproj_fwd(h, g, wx, tm, name):
    T = h.shape[0]

    def body(h_ref, g_ref, w_ref, o_ref):
        x = h_ref[...]
        u = (x * _rstd(x) * g_ref[...]).astype(BF16)
        o_ref[...] = _dot(u, w_ref[...])

    return pl.pallas_call(
        body, name=name, grid=(T // tm,),
        in_specs=[pl.BlockSpec((tm, D), _row(0)), pl.BlockSpec((1, D), _const2), pl.BlockSpec((D, NX), _const2)],
        out_specs=pl.BlockSpec((tm, NX), _row(0)), out_shape=jax.ShapeDtypeStruct((T, NX), F32),
        compiler_params=_params("parallel"),
    )(h, g, wx)


def _mla_prep_fwd(proj, gq, gkv, wq, wkk, wkv, tq, tk, tm, name):
    T = proj.shape[0]

    def body(cq_ref, ckv_ref, kr_ref, gq_ref, gkv_ref, wq_ref, wkk_ref, wkv_ref, tq_ref, tk_ref, q_ref, k_ref, v_ref):
        cq = cq_ref[...]
        q = _dot((cq * _rstd(cq) * gq_ref[...]).astype(BF16), wq_ref[...])
        tqv = tq_ref[...]
        lane = _lane((1, LANES))
        one = jnp.where(lane == MASK_LANE, 1.0, 0.0)
        for h in range(B_HEADS):
            cs = slice(LANES * h, LANES * (h + 1))
            q_ref[:, cs] = (_rope(q[:, cs], tqv) * (B_SCALE * LOG2E) + one).astype(BF16)
        tkv = tk_ref[...]
        ksh = pltpu.roll(_rope(kr_ref[...], tkv), NOPE, 1) + tkv[3]
        ckv = ckv_ref[...]
        cb = (ckv * _rstd(ckv) * gkv_ref[...]).astype(BF16)
        kn = _dot(cb, wkk_ref[...])
        vn = _dot(cb, wkv_ref[...])
        minus = jnp.where((lane == SUM_LANE) | (lane == SUM_LANE + 1), -1.0, 0.0)
        for h in range(B_HEADS):
            cs = slice(LANES * h, LANES * (h + 1))
            k_ref[:, cs] = (kn[:, cs] + ksh).astype(BF16)
            v_ref[:, cs] = (vn[:, cs] + minus).astype(BF16)

    tab = lambda n: pl.BlockSpec((n, tm, LANES), lambda i: (0, i, 0))
    wide = pl.BlockSpec((tm, 1024), _row(0))
    return pl.pallas_call(
        body, name=name, grid=(T // tm,),
        in_specs=[pl.BlockSpec((tm, 256), _row(CQ0 // 256)), pl.BlockSpec((tm, 128), _row(CKV0 // 128)),
                  pl.BlockSpec((tm, 128), _row(KR0 // 128)), pl.BlockSpec((1, Q_RANK), _const2),
                  pl.BlockSpec((1, KV_RANK), _const2), pl.BlockSpec((Q_RANK, 1024), _const2),
                  pl.BlockSpec((KV_RANK, 1024), _const2), pl.BlockSpec((KV_RANK, 1024), _const2), tab(3), tab(4)],
        out_specs=(wide, wide, wide), out_shape=(jax.ShapeDtypeStruct((T, 1024), BF16),) * 3,
        compiler_params=_params("parallel"),
    )(proj, proj, proj, gq, gkv, wq, wkk, wkv, tq, tk)


SINK_SLOT = 4 * BLK - 1


def _build_bias(buckets, table, sinks, name):
    depth = sinks.shape[0]

    def body(tab_ref, sink_ref, b_ref, *o_refs):
        bidx = b_ref[0]
        slot = _lane(bidx.shape)
        for h in range(A_HEADS):
            acc = jnp.full(bidx.shape, NEG, F32)
            for j in range(N_BUCKETS):
                acc = jnp.where(bidx == j, tab_ref[j, h] * LOG2E, acc)
            for l in range(depth):
                o_refs[l][0, h] = jnp.where(slot == SINK_SLOT, sink_ref[l, h] * LOG2E, acc)

    smem = pl.BlockSpec(memory_space=pltpu.SMEM)
    blk = pl.BlockSpec((1, A_HEADS, BLK, 4 * BLK), lambda i: (i, 0, 0, 0))
    return pl.pallas_call(
        body, name=name, grid=(4,),
        in_specs=[smem, smem, pl.BlockSpec((1, BLK, 4 * BLK), lambda i: (i, 0, 0))],
        out_specs=tuple([blk] * depth), out_shape=tuple([jax.ShapeDtypeStruct((4, A_HEADS, BLK, 4 * BLK), F32)] * depth),
        compiler_params=_params("parallel"),
    )(table, sinks, buckets)


def _win_base(n, NB):
    return jnp.where(n == NB, 0, jnp.clip(n - 1, 0, NB - 3))


def _win_variant(n, NB):
    return jnp.where(n == 0, 0, jnp.where(n < NB - 1, 1, jnp.where(n == NB - 1, 2, 3)))


def _win_kv(k_ref, v_ref, base, S, kvh):
    cs = slice(LANES * kvh, LANES * (kvh + 1))
    k2 = jnp.concatenate([k_ref[pl.ds(base, 3 * BLK), cs], k_ref[S:S + BLK, cs]], axis=0).astype(BF16)
    v2 = jnp.concatenate([v_ref[pl.ds(base, 3 * BLK), cs], v_ref[S:S + BLK, cs]], axis=0)
    lane = _lane(v2.shape)
    v2 = jnp.where(lane < A_DH, v2, jnp.where(lane < A_DH + 2, -1.0, 0.0)).astype(BF16)
    return k2, v2


def _win_fwd(proj, bias, Bl, S, name):
    T = proj.shape[0]
    Lp = S + BLK
    NB = S // BLK
    nblk = Lp // BLK
    G = A_HEADS // A_KV

    def body(q_ref, k_ref, v_ref, b_ref, o_ref, lse_ref):
        n = pl.program_id(1)
        base = pl.multiple_of(_win_base(n, NB) * BLK, BLK)
        lane = _lane((BLK, LANES))
        lo = lane < A_DH
        kv = [_win_kv(k_ref, v_ref, base, S, kvh) for kvh in range(A_KV)]
        scores = []
        for h in range(A_HEADS):
            qp = q_ref[:, LANES * (h // 2):LANES * (h // 2 + 1)]
            qm = (jnp.where(lo if h % 2 == 0 else ~lo, qp, 0.0) * (A_SCALE * LOG2E)).astype(BF16)
            scores.append(_dot_nt(qm, kv[h // G][0]) + b_ref[0, h])
        ms = [jnp.max(s, axis=-1, keepdims=True) for s in scores]
        es = [jnp.exp2(s - m).astype(BF16) for s, m in zip(scores, ms)]
        outs = []
        t = jnp.zeros((BLK, LANES), F32)
        for h in range(A_HEADS):
            o2 = _dot(es[h], kv[h // G][1])
            l = -jnp.sum(jnp.where(lane == A_DH, o2, 0.0), axis=-1, keepdims=True)
            outs.append(o2 * (1.0 / l))
            t = jnp.where(lane == h, ms[h] + jnp.log(l) * LOG2E, t)
        for j in range(A_HEADS // 2):
            o_ref[:, LANES * j:LANES * (j + 1)] = jnp.where(lo, outs[2 * j], pltpu.roll(outs[2 * j + 1], A_DH, 1))
        lse_ref[...] = t

    qrow = lambda b, n: (b * nblk + n, 0)
    return pl.pallas_call(
        body, name=name, grid=(Bl, nblk),
        in_specs=[pl.BlockSpec((BLK, A_W), qrow),
                  pl.BlockSpec((Lp, 256), lambda b, n: (b, KA0 // 256)), pl.BlockSpec((Lp, 256), lambda b, n: (b, VA0 // 256)),
                  pl.BlockSpec((1, A_HEADS, BLK, 4 * BLK), lambda b, n: (_win_variant(n, NB), 0, 0, 0))],
        out_specs=(pl.BlockSpec((BLK, A_W), qrow), pl.BlockSpec((BLK, LANES), qrow)),
        out_shape=(jax.ShapeDtypeStruct((T, A_W), F32), jax.ShapeDtypeStruct((T, LANES), F32)),
        compiler_params=_params("parallel", "arbitrary"),
    )(proj, proj, proj, bias)


def _q_tiles(Lp):
    S = Lp - BLK
    return S // 256, S


def _mla_fwd(q, k, v, Bl, Lp, name):
    T = q.shape[0]
    n_big, S = _q_tiles(Lp)

    def body(q_ref, k_ref, v_ref, o_ref, lse_ref):
        def qtile(r0, nq):
            lane = _lane((nq, LANES))
            outs = []
            t = jnp.zeros((nq, LANES), F32)
            cols = [slice(LANES * hh, LANES * (hh + 1)) for hh in range(2)]
            scores = [_dot_nt(q_ref[pl.ds(r0, nq), cs], k_ref[:, cs]) for cs in cols]
            for hh in range(2):
                cs, s = cols[hh], scores[hh]
                m = jnp.max(s, axis=-1, keepdims=True)
                o2 = _dot(jnp.exp2(s - m).astype(BF16), v_ref[:, cs])
                l = -jnp.sum(jnp.where(lane == SUM_LANE, o2, 0.0), axis=-1, keepdims=True)
                outs.append(o2 * (1.0 / l))
                t = jnp.where(lane == hh, m + jnp.log(l) * LOG2E, t)
            o_ref[pl.ds(r0, nq), :] = jnp.where(lane < B_V, outs[0], pltpu.roll(outs[1], B_V, 1))
            lse_ref[pl.ds(r0, nq), :] = t

        def step(i, c):
            qtile(pl.multiple_of(i * 256, 256), 256)
            return c

        lax.fori_loop(0, n_big, step, 0)
        qtile(S, BLK)

    pair = lambda b, j: (b, j)
    wide, narrow = pl.BlockSpec((Lp, 256), pair), pl.BlockSpec((Lp, LANES), pair)
    return pl.pallas_call(
        body, name=name, grid=(Bl, B_HEADS // 2), in_specs=[wide, wide, wide], out_specs=(narrow, narrow),
        out_shape=(jax.ShapeDtypeStruct((T, B_W), F32), jax.ShapeDtypeStruct((T, B_W), F32)),
        compiler_params=_params("parallel", "parallel"),
    )(q, k, v)


def _gated(y, gate, gain):
    r = _rstd(y)
    nrm = y * r
    sg = jax.nn.sigmoid(gate)
    return r, nrm, sg, nrm * gain, gate * sg


def _out_fwd(ya, yb, proj, na, nb, wout, h, tm, name):
    T = h.shape[0]

    def body(ya_ref, yb_ref, ga_ref, gb_ref, na_ref, nb_ref, w_ref, h_ref, o_ref):
        _, _, _, n_a, sl_a = _gated(ya_ref[...], ga_ref[...], na_ref[...])
        _, _, _, n_b, sl_b = _gated(yb_ref[...], gb_ref[...], nb_ref[...])
        acc = _dot((n_a * sl_a).astype(BF16), w_ref[0:A_W, :]) + _dot((n_b * sl_b).astype(BF16), w_ref[A_W:A_W + B_W, :])
        o_ref[...] = h_ref[...] + acc

    half = pl.BlockSpec((tm, 512), _row(0))
    return pl.pallas_call(
        body, name=name, grid=(T // tm,),
        in_specs=[half, half, pl.BlockSpec((tm, 512), _row(GA0 // 512)), pl.BlockSpec((tm, 512), _row(GB0 // 512)),
                  pl.BlockSpec((1, 512), _const2), pl.BlockSpec((1, 512), _const2), pl.BlockSpec((D, D), _const2),
                  pl.BlockSpec((tm, D), _row(0))],
        out_specs=pl.BlockSpec((tm, D), _row(0)), out_shape=jax.ShapeDtypeStruct((T, D), F32),
        compiler_params=_params("parallel"),
    )(ya, yb, proj, proj, na, nb, wout, h)


def _loss_head(h, gf, target, Bl, S, name):
    T = h.shape[0]
    nblk = (S + BLK) // BLK
    NB = S // BLK

    def body(h_ref, g_ref, t_ref, dh_ref, loss_ref, dg_ref):
        b, n = pl.program_id(0), pl.program_id(1)

        @pl.when((b == 0) & (n == 0))
        def _():
            loss_ref[...] = jnp.zeros_like(loss_ref)
            dg_ref[...] = jnp.zeros_like(dg_ref)

        @pl.when(n < NB)
        def _():
            x = h_ref[...]
            g = g_ref[...]
            r = _rstd(x)
            err = x * r * g - t_ref[0]
            loss_ref[...] += 0.5 * jnp.sum(jnp.mean(err * err, axis=-1, keepdims=True))
            dx, dg = _rms_bwd(err * (1.0 / D), x, r, g)
            dh_ref[...] = dx
            dg_ref[...] += dg

        @pl.when(n >= NB)
        def _():
            dh_ref[...] = jnp.zeros_like(dh_ref)

    return pl.pallas_call(
        body, name=name, grid=(Bl, nblk),
        in_specs=[pl.BlockSpec((BLK, D), lambda b, n: (b * nblk + n, 0)), pl.BlockSpec((1, D), lambda b, n: (0, 0)),
                  pl.BlockSpec((1, BLK, D), lambda b, n: (b, jnp.minimum(n, NB - 1), 0))],
        out_specs=(pl.BlockSpec((BLK, D), lambda b, n: (b * nblk + n, 0)), pl.BlockSpec((8, LANES), lambda b, n: (0, 0)),
                   pl.BlockSpec((1, D), lambda b, n: (0, 0))),
        out_shape=(jax.ShapeDtypeStruct((T, D), F32), jax.ShapeDtypeStruct((8, LANES), F32), jax.ShapeDtypeStruct((1, D), F32)),
        compiler_params=_params("arbitrary", "arbitrary"),
    )(h, gf, target)


def _out_bwd(dh, ya, yb, proj, na, nb, wout, tm, name):
    T = dh.shape[0]

    def body(dh_ref, ya_ref, yb_ref, ga_ref, gb_ref, na_ref, nb_ref, w_ref,
             dya_ref, dyb_ref, dga_ref, dgb_ref, dw_ref, dna_ref, dnb_ref):
        @pl.when(pl.program_id(0) == 0)
        def _():
            dw_ref[...] = jnp.zeros_like(dw_ref)
            dna_ref[...] = jnp.zeros_like(dna_ref)
            dnb_ref[...] = jnp.zeros_like(dnb_ref)

        dhb = dh_ref[...].astype(BF16)
        dy = _dot_nt(dhb, w_ref[...])
        parts = ((ya_ref, ga_ref, na_ref, dya_ref, dga_ref, dna_ref, 0), (yb_ref, gb_ref, nb_ref, dyb_ref, dgb_ref, dnb_ref, A_W))
        for y_ref, gate_ref, gain_ref, dy_out, dgate_out, dgain_out, c0 in parts:
            y, gate, gain = y_ref[...], gate_ref[...], gain_ref[...]
            r, nrm, sg, n_g, sl = _gated(y, gate, gain)
            dyp = dy[:, c0:c0 + 512]
            dgate_out[...] = dyp * n_g * (sg * (1.0 + gate * (1.0 - sg)))
            dx, dgain = _rms_bwd(dyp * sl, y, r, gain)
            dy_out[...] = dx
            dgain_out[...] += dgain
            dw_ref[c0:c0 + 512, :] += _dot_tn((n_g * sl).astype(BF16), dhb)

    half = pl.BlockSpec((tm, 512), _row(0))
    vec = pl.BlockSpec((1, 512), _const2)
    return pl.pallas_call(
        body, name=name, grid=(T // tm,),
        in_specs=[pl.BlockSpec((tm, D), _row(0)), half, half, pl.BlockSpec((tm, 512), _row(GA0 // 512)),
                  pl.BlockSpec((tm, 512), _row(GB0 // 512)), vec, vec, pl.BlockSpec((D, D), _const2)],
        out_specs=(half, half, half, half, pl.BlockSpec((D, D), _const2), vec, vec),
        out_shape=(jax.ShapeDtypeStruct((T, 512), F32),) * 4 + (jax.ShapeDtypeStruct((D, D), F32),)
        + (jax.ShapeDtypeStruct((1, 512), F32),) * 2,
        compiler_params=_params("arbitrary"),
    )(dh, ya, yb, proj, proj, na, nb, wout)


def _mla_bwd(q, k, v, o, do, lse, after, Bl, Lp, name):
    T = q.shape[0]
    n_big, S = _q_tiles(Lp)

    def body(q_ref, k_ref, v_ref, o_ref, do_ref, lse_ref, _, dq_ref, dk_ref, dv_ref):
        dk_ref[...] = jnp.zeros_like(dk_ref)
        dv_ref[...] = jnp.zeros_like(dv_ref)

        def qtile(r0, nq):
            lane = _lane((nq, LANES))
            dop = do_ref[pl.ds(r0, nq), :]
            op = o_ref[pl.ds(r0, nq), :]
            lsev = lse_ref[pl.ds(r0, nq), :]
            for hh in range(2):
                cs = slice(LANES * hh, LANES * (hh + 1))
                dom = jnp.where((lane < B_V) if hh == 0 else (lane >= B_V), dop, 0.0)
                delta = jnp.sum(dom * op, axis=-1, keepdims=True)
                d_hi = delta.astype(BF16).astype(F32)
                x = dom if hh == 0 else pltpu.roll(dom, B_V, 1)
                domx = jnp.where(lane == SUM_LANE, d_hi, jnp.where(lane == SUM_LANE + 1, delta - d_hi, x)).astype(BF16)
                qh, kh, vh = q_ref[pl.ds(r0, nq), cs], k_ref[:, cs], v_ref[:, cs]
                lse_h = jnp.sum(jnp.where(lane == hh, lsev, 0.0), axis=-1, keepdims=True)
                p = jnp.exp2(_dot_nt(qh, kh) - lse_h)
                ds = (p * _dot_nt(domx, vh)).astype(BF16)
                dq_ref[pl.ds(r0, nq), cs] = _dot(ds, kh) * B_SCALE
                dk_ref[:, cs] += _dot_tn(ds, qh)
                dv_ref[:, cs] += _dot_tn(p.astype(BF16), domx)

        def step(i, c):
            qtile(pl.multiple_of(i * 256, 256), 256)
            return c

        lax.fori_loop(0, n_big, step, 0)
        qtile(S, BLK)
        dk_ref[...] = dk_ref[...] * (1.0 / LOG2E)

    pair = lambda b, j: (b, j)
    wide, narrow = pl.BlockSpec((Lp, 256), pair), pl.BlockSpec((Lp, LANES), pair)
    return pl.pallas_call(
        body, name=name, grid=(Bl, B_HEADS // 2),
        in_specs=[wide, wide, wide, narrow, narrow, narrow, pl.BlockSpec(memory_space=pl.ANY)],
        out_specs=(wide, wide, wide), out_shape=(jax.ShapeDtypeStruct((T, 1024), F32),) * 3,
        compiler_params=_params("parallel", "parallel"),
    )(q, k, v, o, do, lse, after)


def _win_bwd(proj, bias, ya, dya, lse, Bl, S, name):
    T = proj.shape[0]
    Lp = S + BLK
    NB = S // BLK
    nblk = Lp // BLK
    G = A_HEADS // A_KV

    def body(q_ref, k_ref, v_ref, b_ref, o_ref, do_ref, lse_ref, dq_ref, dk_ref, dv_ref, db_ref):
        n = pl.program_id(1)

        @pl.when(n == 0)
        def _():
            dk_ref[...] = jnp.zeros_like(dk_ref)
            dv_ref[...] = jnp.zeros_like(dv_ref)

        @pl.when((n == 0) | (n == 1) | (n == NB - 1) | (n == NB))
        def _():
            db_ref[...] = jnp.zeros_like(db_ref)

        base = pl.multiple_of(_win_base(n, NB) * BLK, BLK)
        lane = _lane((BLK, LANES))
        lo = lane < A_DH
        lsev = lse_ref[...]
        kv = [_win_kv(k_ref, v_ref, base, S, kvh) for kvh in range(A_KV)]
        qms, pbs, domxs, dsls = [], [], [], []
        for h in range(A_HEADS):
            k2, v2 = kv[h // G]
            ps = slice(LANES * (h // 2), LANES * (h // 2 + 1))
            qm = (jnp.where(lo if h % 2 == 0 else ~lo, q_ref[:, ps], 0.0) * (A_SCALE * LOG2E)).astype(BF16)
            lse_h = jnp.sum(jnp.where(lane == h, lsev, 0.0), axis=-1, keepdims=True)
            p = jnp.exp2(_dot_nt(qm, k2) + b_ref[0, h] - lse_h)
            dop, op = do_ref[:, ps], o_ref[:, ps]
            if h % 2 == 1:
                dop, op = pltpu.roll(dop, A_DH, 1), pltpu.roll(op, A_DH, 1)
            dom = jnp.where(lo, dop, 0.0)
            delta = jnp.sum(dom * op, axis=-1, keepdims=True)
            d_hi = delta.astype(BF16).astype(F32)
            domx = jnp.where(lane == A_DH, d_hi, jnp.where(lane == A_DH + 1, delta - d_hi, dom)).astype(BF16)
            dsc = p * _dot_nt(domx, v2)
            db_ref[0, 0, h] += dsc
            qms.append(qm)
            pbs.append(p.astype(BF16))
            domxs.append(domx)
            dsls.append(dsc.astype(BF16))
        dqs = [_dot(dsls[h], kv[h // G][0]) * A_SCALE for h in range(A_HEADS)]
        for kvh in range(A_KV):
            cs = slice(LANES * kvh, LANES * (kvh + 1))
            dk_acc = jnp.zeros((4 * BLK, LANES), F32)
            dv_acc = jnp.zeros((4 * BLK, LANES), F32)
            for h in range(kvh * G, (kvh + 1) * G):
                dk_acc = dk_acc + _dot_tn(dsls[h], qms[h])
                dv_acc = dv_acc + _dot_tn(pbs[h], domxs[h])
            dk_acc = dk_acc * (1.0 / LOG2E)
            dv_acc = jnp.where(_lane(dv_acc.shape) < A_DH, dv_acc, 0.0)
            dk_ref[pl.ds(base, 3 * BLK), cs] += dk_acc[0:3 * BLK]
            dk_ref[S:S + BLK, cs] += dk_acc[3 * BLK:4 * BLK]
            dv_ref[pl.ds(base, 3 * BLK), cs] += dv_acc[0:3 * BLK]
            dv_ref[S:S + BLK, cs] += dv_acc[3 * BLK:4 * BLK]
        for j in range(A_HEADS // 2):
            dq_ref[:, LANES * j:LANES * (j + 1)] = jnp.where(lo, dqs[2 * j], dqs[2 * j + 1])

    qrow = lambda b, n: (b * nblk + n, 0)
    kvs = pl.BlockSpec((Lp, 256), lambda b, n: (b, 0))
    return pl.pallas_call(
        body, name=name, grid=(Bl, nblk),
        in_specs=[pl.BlockSpec((BLK, A_W), qrow),
                  pl.BlockSpec((Lp, 256), lambda b, n: (b, KA0 // 256)), pl.BlockSpec((Lp, 256), lambda b, n: (b, VA0 // 256)),
                  pl.BlockSpec((1, A_HEADS, BLK, 4 * BLK), lambda b, n: (_win_variant(n, NB), 0, 0, 0)),
                  pl.BlockSpec((BLK, A_W), qrow), pl.BlockSpec((BLK, A_W), qrow), pl.BlockSpec((BLK, LANES), qrow)],
        out_specs=(pl.BlockSpec((BLK, A_W), qrow), kvs, kvs,
                   pl.BlockSpec((1, 1, A_HEADS, BLK, 4 * BLK), lambda b, n: (b, _win_variant(n, NB), 0, 0, 0))),
        out_shape=(jax.ShapeDtypeStruct((T, A_W), F32), jax.ShapeDtypeStruct((T, 256), F32),
                   jax.ShapeDtypeStruct((T, 256), F32), jax.ShapeDtypeStruct((Bl, 4, A_HEADS, BLK, 4 * BLK), F32)),
        compiler_params=_params("parallel", "arbitrary"),
    )(proj, proj, proj, bias, ya, dya, lse)


def _table_grad(dbias_list, buckets, after, name):
    nl = len(dbias_list)
    Bl = dbias_list[0].shape[0]

    def body(*refs):
        d_refs = refs[:nl]
        b_ref, _, o_ref, s_ref, acc, part = refs[nl:]
        v, b = pl.program_id(0), pl.program_id(1)

        @pl.when((v == 0) & (b == 0))
        def _():
            o_ref[...] = jnp.zeros_like(o_ref)
            s_ref[...] = jnp.zeros_like(s_ref)

        slot = _lane((BLK, 4 * BLK))
        for l in range(nl):
            for h in range(A_HEADS):
                col = jnp.sum(jnp.where(slot == SINK_SLOT, d_refs[l][0, 0, h], 0.0), axis=0, keepdims=True)
                s_ref[l, h:h + 1, :] += jnp.sum(col, axis=1, keepdims=True)

        tot = d_refs[0][0, 0]
        for r in d_refs[1:]:
            tot = tot + r[0, 0]

        @pl.when(b == 0)
        def _():
            acc[...] = tot

        @pl.when(b > 0)
        def _():
            acc[...] += tot

        @pl.when(b == Bl - 1)
        def _():
            bidx = b_ref[0]

            def step(j, c):
                mask = bidx == j
                for h in range(A_HEADS):
                    part[h, j] = jnp.sum(jnp.where(mask, acc[h], 0.0).reshape(BLK // 8, 8, 4 * BLK), axis=0)
                return c

            lax.fori_loop(0, N_BUCKETS, step, 0)
            rows = lax.broadcasted_iota(jnp.int32, (N_BUCKETS, LANES), 0)
            lanes = _lane((N_BUCKETS, LANES))
            for h in range(A_HEADS):
                col = jnp.sum(jnp.sum(part[h], axis=1), axis=-1, keepdims=True)
                o_ref[h:h + 1, :] += jnp.sum(jnp.where(rows == lanes, col, 0.0), axis=0, keepdims=True)

    return pl.pallas_call(
        body, name=name, grid=(4, Bl),
        in_specs=[pl.BlockSpec((1, 1, A_HEADS, BLK, 4 * BLK), lambda v, b: (b, v, 0, 0, 0))] * nl
        + [pl.BlockSpec((1, BLK, 4 * BLK), lambda v, b: (v, 0, 0)), pl.BlockSpec(memory_space=pl.ANY)],
        out_specs=(pl.BlockSpec((8, LANES), lambda v, b: (0, 0)), pl.BlockSpec((nl, 8, LANES), lambda v, b: (0, 0, 0))),
        out_shape=(jax.ShapeDtypeStruct((8, LANES), F32), jax.ShapeDtypeStruct((nl, 8, LANES), F32)),
        scratch_shapes=[pltpu.VMEM((A_HEADS, BLK, 4 * BLK), F32), pltpu.VMEM((A_HEADS, N_BUCKETS, 8, 4 * BLK), F32)],
        compiler_params=_params("arbitrary", "arbitrary"),
    )(*dbias_list, buckets, after)


def _mla_prep_bwd(dq, dk, dv, proj, gq, gkv, wq, wkk, wkv, tq, tk, tm, name):
    T = proj.shape[0]

    def body(dq_ref, dk_ref, dv_ref, cq_ref, ckv_ref, gq_ref, gkv_ref, wq_ref, wkk_ref, wkv_ref, tq_ref, tk_ref,
             dcq_ref, dckv_ref, dkr_ref, dwq_ref, dwkk_ref, dwkv_ref, dgq_ref, dgkv_ref):
        @pl.when(pl.program_id(0) == 0)
        def _():
            for r in (dwq_ref, dwkk_ref, dwkv_ref, dgq_ref, dgkv_ref):
                r[...] = jnp.zeros_like(r)

        tqv = tq_ref[...]
        dqp = jnp.concatenate([_rope_t(dq_ref[:, LANES * h:LANES * (h + 1)], tqv) for h in range(B_HEADS)],
                              axis=1).astype(BF16)
        cq, gq_ = cq_ref[...], gq_ref[...]
        rq = _rstd(cq)
        dwq_ref[...] += _dot_tn((cq * rq * gq_).astype(BF16), dqp)
        dx, dg = _rms_bwd(_dot_nt(dqp, wq_ref[...]), cq, rq, gq_)
        dcq_ref[...] = dx
        dgq_ref[...] += dg

        dkv_ = dk_ref[...]
        dks = dkv_[:, 0:LANES]
        for h in range(1, B_HEADS):
            dks = dks + dkv_[:, LANES * h:LANES * (h + 1)]
        dkr_ref[...] = _rope_t(pltpu.roll(dks, NOPE, 1), tk_ref[...])
        dkb = dkv_.astype(BF16)
        dvb = dv_ref[...].astype(BF16)
        ckv, gkv_ = ckv_ref[...], gkv_ref[...]
        rk = _rstd(ckv)
        cb = (ckv * rk * gkv_).astype(BF16)
        dwkk_ref[...] += _dot_tn(cb, dkb)
        dwkv_ref[...] += _dot_tn(cb, dvb)
        dx, dg = _rms_bwd(_dot_nt(dkb, wkk_ref[...]) + _dot_nt(dvb, wkv_ref[...]), ckv, rk, gkv_)
        dckv_ref[...] = dx
        dgkv_ref[...] += dg

    tab = lambda n: pl.BlockSpec((n, tm, LANES), lambda i: (0, i, 0))
    wide = pl.BlockSpec((tm, 1024), _row(0))
    return pl.pallas_call(
        body, name=name, grid=(T // tm,),
        in_specs=[wide, wide, wide, pl.BlockSpec((tm, 256), _row(CQ0 // 256)),
                  pl.BlockSpec((tm, 128), _row(CKV0 // 128)), pl.BlockSpec((1, Q_RANK), _const2),
                  pl.BlockSpec((1, KV_RANK), _const2), pl.BlockSpec((Q_RANK, 1024), _const2),
                  pl.BlockSpec((KV_RANK, 1024), _const2), pl.BlockSpec((KV_RANK, 1024), _const2), tab(3), tab(4)],
        out_specs=(pl.BlockSpec((tm, 256), _row(0)), pl.BlockSpec((tm, 128), _row(0)), pl.BlockSpec((tm, 128), _row(0)),
                   pl.BlockSpec((Q_RANK, 1024), _const2), pl.BlockSpec((KV_RANK, 1024), _const2),
                   pl.BlockSpec((KV_RANK, 1024), _const2), pl.BlockSpec((1, Q_RANK), _const2),
                   pl.BlockSpec((1, KV_RANK), _const2)),
        out_shape=(jax.ShapeDtypeStruct((T, 256), F32), jax.ShapeDtypeStruct((T, 128), F32),
                   jax.ShapeDtypeStruct((T, 128), F32), jax.ShapeDtypeStruct((Q_RANK, 1024), F32),
                   jax.ShapeDtypeStruct((KV_RANK, 1024), F32), jax.ShapeDtypeStruct((KV_RANK, 1024), F32),
                   jax.ShapeDtypeStruct((1, Q_RANK), F32), jax.ShapeDtypeStruct((1, KV_RANK), F32)),
        compiler_params=_params("arbitrary"),
    )(dq, dk, dv, proj, proj, gq, gkv, wq, wkk, wkv, tq, tk)


def _inproj_bwd(pieces, h, g, wx, dh_out, tm, name):
    T = h.shape[0]
    nsteps = T // tm
    CH = 512

    def body(*refs):
        p_refs = refs[:len(pieces)]
        h_ref, g_ref, w_ref, dho_ref, dh_ref, dw_hbm, dg_ref, acc, sem = refs[len(pieces):]
        i = pl.program_id(0)

        @pl.when(i == 0)
        def _():
            acc[...] = jnp.zeros_like(acc)
            dg_ref[...] = jnp.zeros_like(dg_ref)

        dp = jnp.concatenate([r[...].astype(BF16) for r in p_refs], axis=1)
        x, gain = h_ref[...], g_ref[...]
        r = _rstd(x)
        u = (x * r * gain).astype(BF16)
        for c in range(0, NX, CH):
            acc[:, c:c + CH] += _dot_tn(u, dp[:, c:c + CH])
        dx, dg = _rms_bwd(_dot_nt(dp, w_ref[...]), x, r, gain)
        dh_ref[...] = dho_ref[...] + dx
        dg_ref[...] += dg

        @pl.when(i == nsteps - 1)
        def _():
            cp = pltpu.make_async_copy(acc, dw_hbm, sem)
            cp.start()
            cp.wait()

    return pl.pallas_call(
        body, name=name, grid=(nsteps,),
        in_specs=[pl.BlockSpec((tm, p.shape[1]), _row(0)) for p in pieces]
        + [pl.BlockSpec((tm, D), _row(0)), pl.BlockSpec((1, D), _const2), pl.BlockSpec((D, NX), _const2),
           pl.BlockSpec((tm, D), _row(0))],
        out_specs=(pl.BlockSpec((tm, D), _row(0)), pl.BlockSpec(memory_space=pl.ANY), pl.BlockSpec((1, D), _const2)),
        out_shape=(jax.ShapeDtypeStruct((T, D), F32), jax.ShapeDtypeStruct((D, NX), F32), jax.ShapeDtypeStruct((1, D), F32)),
        scratch_shapes=[pltpu.VMEM((D, NX), F32), pltpu.SemaphoreType.DMA(())],
        compiler_params=_params("arbitrary"),
    )(*pieces, h, g, wx, dh_out)


def kernel(x, meta_tokens, rel_bias_table, norm_in, w_in, sink_a, norm_q_lat, w_uq, norm_kv_lat, w_ukv, norm_out_a, norm_out_b, w_out, norm_final, loss_target, m_meta_tokens, m_rel_bias_table, m_norm_in, m_w_in, m_sink_a, m_norm_q_lat, m_w_uq, m_norm_kv_lat, m_w_ukv, m_norm_out_a, m_norm_out_b, m_w_out, m_norm_final, v_meta_tokens, v_rel_bias_table, v_norm_in, v_w_in, v_sink_a, v_norm_q_lat, v_w_uq, v_norm_kv_lat, v_w_ukv, v_norm_out_a, v_norm_out_b, v_w_out, v_norm_final):
    Bl, S, _ = x.shape
    assert S % 256 == 0 and S >= 3 * BLK, "the attention kernels tile the real tokens in 256-row blocks"
    Lp = S + BLK
    T = Bl * Lp
    tm = Lp // 4
    tq = Lp // 8
    depth = w_in.shape[0]
    me = 4 * lax.axis_index("x") + 2 * lax.axis_index("y") + lax.axis_index("c")

    def finish(handles, after, name):
        srcs, lands = _xchg_wait(handles, after, name)
        full = []
        for hd, src, land in zip(handles, srcs, lands):
            own = src if hd[4] == "gather" else lax.dynamic_index_in_dim(src, me, 0, keepdims=False)
            full.append(_own_slot(land, own, me))
        return full

    def weights_of(i):
        return [w_in[i].astype(BF16), w_uq[i].astype(BF16), w_ukv[i].astype(BF16), w_out[i].astype(BF16)]

    def packed(l_in, l_uq, l_ukv, l_out):
        cols = lambda t: jnp.transpose(t, (1, 0, 2)).reshape(t.shape[1], NDEV * t.shape[2])
        wkk, wkv = _pack_w_ukv(cols(l_ukv))
        return _pack_w_in(cols(l_in)), _pack_w_uq(cols(l_uq)), wkk, wkv, l_out.reshape(D, D)

    wb = weights_of(0)
    gat_a, tok = _xchg_start([wb[0], meta_tokens], ["gather", "gather"], "gather_start_0a")
    buckets = _bias_buckets(S)
    biases = _build_bias(buckets, _after(rel_bias_table, tok), sink_a, "build_bias")
    tb = 256 if T % 256 == 0 else tq
    tq_tab = jnp.tile(_rope_tables(S, Lp, NOPE, True), (1, Bl, 1))
    key_pad = jnp.where((jnp.arange(Lp) >= S + NMETA)[:, None] & (jnp.arange(LANES) == MASK_LANE)[None, :], NEG, 0.0)
    tk_tab = jnp.tile(jnp.concatenate([_rope_tables(S, Lp, 0, False), key_pad.astype(F32)[None]], axis=0), (1, Bl, 1))
    l_in, l_meta = finish(gat_a, biases[0], "gather_wait_0a")
    gat_b, tok = _xchg_start(wb[1:], ["gather"] * 3, "gather_start_0b", after=l_in)

    meta_f = jnp.transpose(l_meta, (1, 0, 2)).reshape(NMETA, D)
    tail = jnp.concatenate([meta_f, jnp.zeros((BLK - NMETA, D), F32)], axis=0)
    h = jnp.concatenate([x, jnp.broadcast_to(tail[None], (Bl, BLK, D))], axis=1).reshape(T, D)

    saved = []
    w_out_f = [None] * depth
    gat_next = None
    for i in range(depth):
        g_in, g_q, g_kv = norm_in[i][None], norm_q_lat[i][None], norm_kv_lat[i][None]
        g_a, g_b = norm_out_a[i][None], norm_out_b[i][None]
        if i == 0:
            proj = _inproj_fwd(h, _after(g_in, tok), _pack_w_in(jnp.transpose(l_in, (1, 0, 2)).reshape(D, IN_W)), tm,
                               f"inproj_fwd_{i}")
            l_uq, l_ukv, l_out = finish(gat_b, proj, "gather_wait_0b")
            wx, wq, wkk, wkv, w_out_f[i] = packed(l_in, l_uq, l_ukv, l_out)
        else:
            wx, wq, wkk, wkv, w_out_f[i] = packed(*finish(gat_next, h, f"gather_wait_{i}"))
            proj = _inproj_fwd(h, g_in, wx, tm, f"inproj_fwd_{i}")
        if i + 1 < depth:
            gat_next, tok = _xchg_start(weights_of(i + 1), ["gather"] * 4, f"gather_start_{i + 1}", after=w_out_f[i])
            g_q = _after(g_q, tok)
        q, k, v = _mla_prep_fwd(proj, g_q, g_kv, wq, wkk, wkv, tq_tab, tk_tab, tm, f"mla_prep_fwd_{i}")
        ya, lse_a = _win_fwd(proj, biases[i], Bl, S, f"win_fwd_{i}")
        yb, lse_b = _mla_fwd(q, k, v, Bl, Lp, f"mla_fwd_{i}")
        h_new = _out_fwd(ya, yb, proj, g_a, g_b, w_out_f[i], h, tm, f"out_fwd_{i}")
        saved.append((h, proj, q, k, v, ya, lse_a, yb, lse_b, wx, wq, wkk, wkv))
        h = h_new

    dh, loss_acc, dg_final = _loss_head(h, norm_final[None], loss_target, Bl, S, "loss_head")

    g_n_in, g_nq, g_nkv, g_na, g_nb = ([None] * depth for _ in range(5))
    dbias_all, sc_out, sc_rest = [None] * depth, [None] * depth, [None] * depth
    split = lambda t, n: jnp.transpose(t.reshape(t.shape[0], NDEV, n), (1, 0, 2)).astype(BF16)
    tok = None
    for i in reversed(range(depth)):
        h_in, proj, q, k, v, ya, lse_a, yb, lse_b, wx, wq, wkk, wkv = saved[i]
        g_in, g_q, g_kv = norm_in[i][None], norm_q_lat[i][None], norm_kv_lat[i][None]
        g_a, g_b = norm_out_a[i][None], norm_out_b[i][None]
        if tok is not None:
            g_a = _after(g_a, tok)
        dya, dyb, dga, dgb, g_w_out, g_na[i], g_nb[i] = _out_bwd(dh, ya, yb, proj, g_a, g_b, w_out_f[i], tb, f"out_bwd_{i}")
        sc_out[i], tok = _xchg_start([g_w_out.reshape(NDEV, D // NDEV, D).astype(BF16)], ["scatter"], f"scatter_start_{i}a")
        dq, dk, dv = _mla_bwd(q, k, v, yb, dyb, lse_b, tok, Bl, Lp, f"mla_bwd_{i}")
        dqa, dka, dva, dbias = _win_bwd(proj, biases[i], ya, dya, lse_a, Bl, S, f"win_bwd_{i}")
        dcq, dckv, dkr, dwq, dwkk, dwkv, g_nq[i], g_nkv[i] = _mla_prep_bwd(
            dq, dk, dv, proj, g_q, g_kv, wq, wkk, wkv, tq_tab, tk_tab, tb, f"mla_prep_bwd_{i}")
        dh, dwx, g_n_in[i] = _inproj_bwd((dqa, dka, dva, dga, dcq, dckv, dkr, dgb), h_in, g_in, wx, dh, tb,
                                         f"inproj_bwd_{i}")
        dbias_all[i] = dbias
        rest = [split(_unpack_w_in_grad(dwx), IN_W // NDEV), split(_unpack_w_uq_grad(dwq), 768 // NDEV),
                split(_unpack_w_ukv_grad(dwkk, dwkv), 1024 // NDEV)]
        if i == 0:
            dh3 = dh.reshape(Bl, Lp, D)
            grad_x = dh3[:, :S]
            rest.append(jnp.transpose(jnp.sum(dh3[:, S:S + NMETA], axis=0).reshape(NMETA, NDEV, D // NDEV), (1, 0, 2)))
        sc_rest[i], tok = _xchg_start(rest, ["scatter"] * len(rest), f"scatter_start_{i}b")

    dtab, dsink = _table_grad(dbias_all, buckets, tok, "table_grad")
    two_d = lambda a: a.reshape(1, -1) if a.ndim == 1 else a
    small_w = [rel_bias_table, norm_in, sink_a, norm_q_lat, norm_kv_lat, norm_out_a, norm_out_b, norm_final]
    small_m = [m_rel_bias_table, m_norm_in, m_sink_a, m_norm_q_lat, m_norm_kv_lat, m_norm_out_a, m_norm_out_b, m_norm_final]
    small_v = [v_rel_bias_table, v_norm_in, v_sink_a, v_norm_q_lat, v_norm_kv_lat, v_norm_out_a, v_norm_out_b, v_norm_final]
    small_g = [jnp.transpose(dtab[:, :N_BUCKETS]), jnp.concatenate(g_n_in), dsink[:, :, 0], jnp.concatenate(g_nq),
               jnp.concatenate(g_nkv), jnp.concatenate(g_na), jnp.concatenate(g_nb), dg_final]
    sc_small, tok = _xchg_start(small_g + [loss_acc], ["gather"] * (len(small_g) + 1), "scatter_start_small")

    r_out, r_in, r_uq, r_ukv = [None] * depth, [None] * depth, [None] * depth, [None] * depth
    for i in reversed(range(depth)):
        (r_out[i],) = finish(sc_out[i], tok, f"scatter_wait_{i}a")
        got = finish(sc_rest[i], tok, f"scatter_wait_{i}b")
        r_in[i], r_uq[i], r_ukv[i] = got[:3]
        if i == 0:
            r_meta = got[3]
    r_small = finish(sc_small, tok, "scatter_wait_small")

    res = {}
    sharded = dict(w_in=(r_in, w_in, m_w_in, v_w_in), w_uq=(r_uq, w_uq, m_w_uq, v_w_uq), w_ukv=(r_ukv, w_ukv, m_w_ukv, v_w_ukv),
                   w_out=(r_out, w_out, m_w_out, v_w_out),
                   meta_tokens=([r_meta], meta_tokens[None], m_meta_tokens[None], v_meta_tokens[None]))
    for n, (r, w, m, v) in sharded.items():
        outs = _adamw_param(r, w, m, v, f"adamw_{n}")
        res[n] = [o[0] for o in outs] if n == "meta_tokens" else list(outs)
    names_small = ["rel_bias_table", "norm_in", "sink_a", "norm_q_lat", "norm_kv_lat", "norm_out_a", "norm_out_b", "norm_final"]
    outs = _adamw_small(r_small[:-1], [two_d(a) for a in small_w], [two_d(a) for a in small_m], [two_d(a) for a in small_v],
                        r_small[-1], "adamw_replicated")
    loss = outs[-1][0, 0]
    ns = len(names_small)
    for j, n in enumerate(names_small):
        res[n] = [outs[kk * ns + j].reshape(small_w[j].shape) for kk in range(4)]
    order = ["meta_tokens", "rel_bias_table", "norm_in", "w_in", "sink_a", "norm_q_lat", "w_uq", "norm_kv_lat", "w_ukv",
             "norm_out_a", "norm_out_b", "w_out", "norm_final"]
    return (loss, grad_x, *[res[n][kk] for kk in range(4) for n in order])
```

```python
import functools
import math

import numpy as np
import jax
import jax.numpy as jnp
from jax import lax
from jax.experimental import pallas as pl
from jax.experimental.pallas import tpu as pltpu

F32, BF16 = jnp.float32, jnp.bfloat16
D = 1024
NMETA = 16
BLK = 128
A_HEADS, A_KV, A_DH, A_W = 8, 2, 64, 512
B_HEADS, NOPE, ROPE, B_V, B_W = 8, 64, 32, 64, 512
Q_RANK, KV_RANK = 256, 128
IN_W = 2208
N_BUCKETS, MAX_DIST = 32, 128
THETA = 10000.0
EPS = 1e-6
NEG = -1e30
A_SCALE = A_DH ** -0.5
B_SCALE = (NOPE + ROPE) ** -0.5
LOG2E = math.log2(math.e)
MASK_LANE = NOPE + ROPE
SUM_LANE = B_V
LANES = 128
NDEV = 8
MESH = pl.DeviceIdType.MESH

NX = 2560
QA0, KA0, VA0, GA0, CQ0, CKV0, KR0, GB0 = 0, 512, 768, 1024, 1536, 1792, 1920, 2048

ADAM_LR, ADAM_B1, ADAM_B2, ADAM_EPS, ADAM_WD, ADAM_STEP = 0.001, 0.9, 0.999, 1e-08, 0.01, 10


def _dot(a, b):
    return jnp.dot(a, b, preferred_element_type=F32)


def _dot_nt(a, b):
    return lax.dot_general(a, b, (((1,), (1,)), ((), ())), preferred_element_type=F32)


def _dot_tn(a, b):
    return lax.dot_general(a, b, (((0,), (0,)), ((), ())), preferred_element_type=F32)


def _lane(shape):
    return lax.broadcasted_iota(jnp.int32, shape, len(shape) - 1)


def _rstd(x):
    return lax.rsqrt(jnp.mean(x * x, axis=-1, keepdims=True) + EPS)


def _rms_bwd(dn, x, r, g):
    z = dn * g
    dx = r * z - x * (r * r * r) * jnp.mean(z * x, axis=-1, keepdims=True)
    return dx, jnp.sum(dn * (x * r), axis=0, keepdims=True)


def _row(i):
    return lambda *ids: (ids[0], i)


def _const2(*ids):
    return (0, 0)


def _params(*sem):
    return pltpu.CompilerParams(dimension_semantics=sem)


def _pack_w_in(w):
    qa, ka, va, ga = w[:, 0:512], w[:, 512:640], w[:, 640:768], w[:, 768:1280]
    cq, ckv, kr, gb = w[:, 1280:1536], w[:, 1536:1664], w[:, 1664:1696], w[:, 1696:2208]
    dup = lambda t: jnp.concatenate([t[:, 0:64], t[:, 0:64], t[:, 64:128], t[:, 64:128]], axis=1)
    krp = jnp.concatenate([kr, jnp.zeros((w.shape[0], LANES - ROPE), w.dtype)], axis=1)
    return jnp.concatenate([qa, dup(ka), dup(va), ga, cq, ckv, krp, gb], axis=1)


def _unpack_w_in_grad(g):
    fold = lambda t: jnp.concatenate([t[:, 0:64] + t[:, 64:128], t[:, 128:192] + t[:, 192:256]], axis=1)
    return jnp.concatenate([g[:, QA0:QA0 + 512], fold(g[:, KA0:KA0 + 256]), fold(g[:, VA0:VA0 + 256]),
                            g[:, GA0:GA0 + 512], g[:, CQ0:CQ0 + 256], g[:, CKV0:CKV0 + 128],
                            g[:, KR0:KR0 + ROPE], g[:, GB0:GB0 + 512]], axis=1)


def _pack_w_uq(w):
    t = w.reshape(Q_RANK, B_HEADS, NOPE + ROPE)
    t = jnp.concatenate([t, jnp.zeros((Q_RANK, B_HEADS, LANES - NOPE - ROPE), w.dtype)], axis=-1)
    return t.reshape(Q_RANK, B_HEADS * LANES)


def _unpack_w_uq_grad(g):
    return g.reshape(Q_RANK, B_HEADS, LANES)[:, :, :NOPE + ROPE].reshape(Q_RANK, B_HEADS * (NOPE + ROPE))


def _pack_w_ukv(w):
    t = w.reshape(KV_RANK, B_HEADS, NOPE + B_V)
    z = jnp.zeros((KV_RANK, B_HEADS, LANES - NOPE), w.dtype)
    pad = lambda u: jnp.concatenate([u, z], axis=-1).reshape(KV_RANK, B_HEADS * LANES)
    return pad(t[:, :, :NOPE]), pad(t[:, :, NOPE:])


def _unpack_w_ukv_grad(gk, gv):
    head = lambda g: g.reshape(KV_RANK, B_HEADS, LANES)[:, :, :NOPE]
    return jnp.concatenate([head(gk), head(gv)], axis=-1).reshape(KV_RANK, B_HEADS * (NOPE + B_V))


def _t5_bucket(rel):
    nb = N_BUCKETS // 2
    max_exact = nb // 2
    ret = jnp.where(rel > 0, nb, 0)
    n = jnp.abs(rel)
    nf = jnp.maximum(n, 1).astype(F32)
    large = max_exact + (jnp.log(nf / max_exact) / math.log(MAX_DIST / max_exact) * (nb - max_exact)).astype(jnp.int32)
    large = jnp.minimum(large, nb - 1)
    return ret + jnp.where(n < max_exact, n, large)


def _bias_buckets(S):
    q = np.arange(BLK)[:, None]
    k = np.arange(4 * BLK)[None, :]
    is_meta_key = (k >= 3 * BLK) & (k < 3 * BLK + NMETA)
    mi = k - 3 * BLK
    rels, valids = [], []
    for shift in (0, BLK, 2 * BLK):
        rel_real = k - shift - q
        valid_real = (k < 3 * BLK) & (np.abs(rel_real) <= BLK)
        far = -(NMETA + MAX_DIST + BLK)
        rel_meta = (mi - (NMETA + q)) if shift == 0 else np.full_like(k + q, far)
        rels.append(np.where(is_meta_key, rel_meta, rel_real))
        valids.append(valid_real | is_meta_key)
    qm = q < NMETA
    rel_real = NMETA + k - q
    valid_real = (k < BLK) & (np.abs(rel_real) <= BLK) & qm
    rels.append(np.where(is_meta_key, mi - q, rel_real))
    valids.append(valid_real | is_meta_key)
    rel = jnp.asarray(np.stack(rels).astype(np.int32))
    valid = jnp.asarray(np.stack(valids))
    return jnp.where(valid, _t5_bucket(rel), -1).astype(jnp.int32)


def _rope_tables(S, Lp, off, passthrough):
    half = ROPE // 2
    r = np.arange(Lp)
    pos = np.where(r < S, NMETA + r, np.where(r < S + NMETA, r - S, 0)).astype(np.float32)
    freqs = THETA ** (-jnp.arange(half, dtype=F32) / half)
    ang = jnp.asarray(pos)[:, None] * freqs[None, :]
    cos, sin = jnp.cos(ang), jnp.sin(ang)
    z = lambda n: jnp.zeros((Lp, n), F32)
    head = jnp.ones((Lp, off), F32) if passthrough else z(off)
    c = jnp.concatenate([head, cos, cos, z(LANES - off - ROPE)], axis=1)
    s1 = jnp.concatenate([z(off), -sin, z(LANES - off - half)], axis=1)
    s2 = jnp.concatenate([z(off + half), sin, z(LANES - off - ROPE)], axis=1)
    return jnp.stack([c, s1, s2])


def _rope(x, t):
    return x * t[0] + pltpu.roll(x, LANES - 16, 1) * t[1] + pltpu.roll(x, 16, 1) * t[2]


def _rope_t(dy, t):
    return dy * t[0] + pltpu.roll(dy * t[1], 16, 1) + pltpu.roll(dy * t[2], LANES - 16, 1)


def _me_and_peers():
    x, y, c = lax.axis_index("x"), lax.axis_index("y"), lax.axis_index("c")
    flip = lambda v, b: 1 - v if b else v
    peers = [(flip(x, k & 4), flip(y, k & 2), flip(c, k & 1)) for k in range(1, NDEV)]
    return 4 * x + 2 * y + c, peers


def _xchg_start(srcs, kinds, name, after=None):
    n = len(srcs)
    lands = [lax.empty((NDEV,) + s.shape[-2:], s.dtype) for s in srcs]
    extra = [] if after is None else [after]

    def body(*refs):
        src_refs, land_refs = refs[:n], refs[n:2 * n]
        ssems, rsems = refs[2 * n + len(extra):3 * n + len(extra)], refs[3 * n + len(extra):4 * n + len(extra)]
        token = refs[6 * n + len(extra)]
        me, peers = _me_and_peers()
        for it in range(n):
            for k, (px, py, pc) in enumerate(peers):
                src = src_refs[it] if kinds[it] == "gather" else src_refs[it].at[4 * px + 2 * py + pc]
                pltpu.make_async_remote_copy(src_ref=src, dst_ref=land_refs[it].at[me], send_sem=ssems[it].at[k],
                                             recv_sem=rsems[it].at[k], device_id=(px, py, pc), device_id_type=MESH).start()
        token[...] = jnp.zeros_like(token)

    hbm = pl.BlockSpec(memory_space=pltpu.HBM)
    sem = pl.BlockSpec(memory_space=pltpu.SEMAPHORE)
    outs = pl.pallas_call(
        body, name=name,
        out_shape=tuple([pltpu.SemaphoreType.DMA((NDEV - 1,))] * (2 * n) + [pltpu.HBM(a.shape, a.dtype) for a in srcs + lands]
                        + [jax.ShapeDtypeStruct((8, LANES), F32)]),
        in_specs=[hbm] * (2 * n) + [pl.BlockSpec(memory_space=pl.ANY)] * len(extra),
        out_specs=tuple([sem] * (2 * n) + [hbm] * (2 * n) + [pl.BlockSpec(memory_space=pltpu.VMEM)]),
        input_output_aliases={i: 2 * n + i for i in range(2 * n)},
        compiler_params=pltpu.CompilerParams(has_side_effects=pltpu.SideEffectType.DATAFLOW_SIDE_EFFECTING),
    )(*[pltpu.with_memory_space_constraint(a, pltpu.HBM) for a in srcs + lands], *extra)
    handles = [(outs[it], outs[n + it], outs[2 * n + it], outs[3 * n + it], kinds[it]) for it in range(n)]
    return handles, outs[4 * n]


def _xchg_wait(handles, after, name):
    n = len(handles)

    def body(*refs):
        src_refs, land_refs = refs[:n], refs[n:2 * n]
        ssems, rsems = refs[2 * n:3 * n], refs[3 * n:4 * n]
        me, peers = _me_and_peers()
        for it in range(n):
            for k, (px, py, pc) in enumerate(peers):
                src = src_refs[it] if handles[it][4] == "gather" else src_refs[it].at[4 * px + 2 * py + pc]
                cp = pltpu.make_async_remote_copy(src_ref=src, dst_ref=land_refs[it].at[me], send_sem=ssems[it].at[k],
                                                  recv_sem=rsems[it].at[k], device_id=(px, py, pc), device_id_type=MESH)
                cp.wait_send()
                cp.wait_recv()

    hbm = pl.BlockSpec(memory_space=pltpu.HBM)
    sem = pl.BlockSpec(memory_space=pltpu.SEMAPHORE)
    srcs, lands = [h[2] for h in handles], [h[3] for h in handles]
    outs = pl.pallas_call(
        body, name=name, out_shape=tuple(pltpu.HBM(a.shape, a.dtype) for a in srcs + lands),
        in_specs=[hbm] * (2 * n) + [sem] * (2 * n) + [pl.BlockSpec(memory_space=pl.ANY)], out_specs=tuple([hbm] * (2 * n)),
        input_output_aliases={i: i for i in range(2 * n)},
        compiler_params=pltpu.CompilerParams(has_side_effects=pltpu.SideEffectType.DATAFLOW_SIDE_EFFECTING),
    )(*srcs, *lands, *[h[0] for h in handles], *[h[1] for h in handles], after)
    return list(outs[:n]), list(outs[n:])


def _own_slot(land, own, me):
    return lax.dynamic_update_slice(land, own[None].astype(land.dtype), (me, 0, 0))


def _after(x, token):
    return x + token[0, 0]


def _adamw_update(g, w, m, v):
    mn = ADAM_B1 * m + (1.0 - ADAM_B1) * g
    vn = ADAM_B2 * v + (1.0 - ADAM_B2) * (g * g)
    m_hat = mn / (1.0 - ADAM_B1 ** ADAM_STEP)
    v_hat = vn / (1.0 - ADAM_B2 ** ADAM_STEP)
    return -ADAM_LR * (m_hat / (jnp.sqrt(v_hat) + ADAM_EPS) + ADAM_WD * w), mn, vn


def _sum_partials(r_ref):
    g = r_ref[0].astype(F32)
    for p in range(1, NDEV):
        g = g + r_ref[p].astype(F32)
    return g


def _adamw_param(recvs, w, m, v, name):
    depth, r, c = w.shape
    tr = min(r, 256)
    nl = len(recvs)

    def body(*refs):
        r_refs = refs[:nl]
        w_ref, m_ref, v_ref, g_out, d_out, m_out, v_out = refs[nl:]
        for li in range(nl):
            @pl.when(pl.program_id(0) == li)
            def _(li=li):
                g = _sum_partials(r_refs[li])
                g_out[0] = g
                d_out[0], m_out[0], v_out[0] = _adamw_update(g, w_ref[0], m_ref[0], v_ref[0])

    blk = pl.BlockSpec((1, tr, c), lambda l, i: (l, i, 0))
    return pl.pallas_call(
        body, name=name, grid=(depth, r // tr),
        in_specs=[pl.BlockSpec((NDEV, tr, c), lambda l, i: (0, i, 0))] * nl + [blk, blk, blk],
        out_specs=(blk, blk, blk, blk), out_shape=(jax.ShapeDtypeStruct(w.shape, F32),) * 4,
        compiler_params=_params("arbitrary", "arbitrary"),
    )(*recvs, w, m, v)


def _adamw_small(recvs, ws, ms, vs, loss_parts, name):
    n = len(ws)

    def body(*refs):
        r_refs, w_refs, m_refs, v_refs = refs[:n], refs[n:2 * n], refs[2 * n:3 * n], refs[3 * n:4 * n]
        outs = refs[4 * n + 1:]
        for j in range(n):
            g = _sum_partials(r_refs[j])
            outs[j][...] = g
            outs[n + j][...], outs[2 * n + j][...], outs[3 * n + j][...] = _adamw_update(
                g, w_refs[j][...], m_refs[j][...], v_refs[j][...])
        outs[4 * n][...] = _sum_partials(refs[4 * n])

    vm = pl.BlockSpec(memory_space=pltpu.VMEM)
    return pl.pallas_call(
        body, name=name, in_specs=[vm] * (4 * n + 1), out_specs=tuple([vm] * (4 * n + 1)),
        out_shape=tuple(jax.ShapeDtypeStruct(a.shape, F32) for a in ws) * 4 + (jax.ShapeDtypeStruct(loss_parts.shape[1:], F32),),
    )(*recvs, *ws, *ms, *vs, loss_parts)


def _inproj_fwd(h, g, wx, tm, name):
    T = h.shape[0]

    def body(h_ref, g_ref, w_ref, o_ref):
        x = h_ref[...]
        u = (x * _rstd(x) * g_ref[...]).astype(BF16)
        o_ref[...] = _dot(u, w_ref[...])

    return pl.pallas_call(
        body, name=name, grid=(T // tm,),
        in_specs=[pl.BlockSpec((tm, D), _row(0)), pl.BlockSpec((1, D), _const2), pl.BlockSpec((D, NX), _const2)],
        out_specs=pl.BlockSpec((tm, NX), _row(0)), out_shape=jax.ShapeDtypeStruct((T, NX), F32),
        compiler_params=_params("parallel"),
    )(h, g, wx)


def _mla_prep_fwd(proj, gq, gkv, wq, wkk, wkv, tq, tk, tm, name):
    T = proj.shape[0]

    def body(cq_ref, ckv_ref, kr_ref, gq_ref, gkv_ref, wq_ref, wkk_ref, wkv_ref, tq_ref, tk_ref, q_ref, k_ref, v_ref):
        cq = cq_ref[...]
        q = _dot((cq * _rstd(cq) * gq_ref[...]).astype(BF16), wq_ref[...])
        tqv = tq_ref[...]
        lane = _lane((1, LANES))
        one = jnp.where(lane == MASK_LANE, 1.0, 0.0)
        for h in range(B_HEADS):
            cs = slice(LANES * h, LANES * (h + 1))
            q_ref[:, cs] = (_rope(q[:, cs], tqv) * (B_SCALE * LOG2E) + one).astype(BF16)
        tkv = tk_ref[...]
        ksh = pltpu.roll(_rope(kr_ref[...], tkv), NOPE, 1) + tkv[3]
        ckv = ckv_ref[...]
        cb = (ckv * _rstd(ckv) * gkv_ref[...]).astype(BF16)
        kn = _dot(cb, wkk_ref[...])
        vn = _dot(cb, wkv_ref[...])
        minus = jnp.where((lane == SUM_LANE) | (lane == SUM_LANE + 1), -1.0, 0.0)
        for h in range(B_HEADS):
            cs = slice(LANES * h, LANES * (h + 1))
            k_ref[:, cs] = (kn[:, cs] + ksh).astype(BF16)
            v_ref[:, cs] = (vn[:, cs] + minus).astype(BF16)

    tab = lambda n: pl.BlockSpec((n, tm, LANES), lambda i: (0, i, 0))
    wide = pl.BlockSpec((tm, 1024), _row(0))
    return pl.pallas_call(
        body, name=name, grid=(T // tm,),
        in_specs=[pl.BlockSpec((tm, 256), _row(CQ0 // 256)), pl.BlockSpec((tm, 128), _row(CKV0 // 128)),
                  pl.BlockSpec((tm, 128), _row(KR0 // 128)), pl.BlockSpec((1, Q_RANK), _const2),
                  pl.BlockSpec((1, KV_RANK), _const2), pl.BlockSpec((Q_RANK, 1024), _const2),
                  pl.BlockSpec((KV_RANK, 1024), _const2), pl.BlockSpec((KV_RANK, 1024), _const2), tab(3), tab(4)],
        out_specs=(wide, wide, wide), out_shape=(jax.ShapeDtypeStruct((T, 1024), BF16),) * 3,
        compiler_params=_params("parallel"),
    )(proj, proj, proj, gq, gkv, wq, wkk, wkv, tq, tk)


SINK_SLOT = 4 * BLK - 1


def _build_bias(buckets, table, sinks, name):
    depth = sinks.shape[0]

    def body(tab_ref, sink_ref, b_ref, *o_refs):
        bidx = b_ref[0]
        slot = _lane(bidx.shape)
        for h in range(A_HEADS):
            acc = jnp.full(bidx.shape, NEG, F32)
            for j in range(N_BUCKETS):
                acc = jnp.where(bidx == j, tab_ref[j, h] * LOG2E, acc)
            for l in range(depth):
                o_refs[l][0, h] = jnp.where(slot == SINK_SLOT, sink_ref[l, h] * LOG2E, acc)

    smem = pl.BlockSpec(memory_space=pltpu.SMEM)
    blk = pl.BlockSpec((1, A_HEADS, BLK, 4 * BLK), lambda i: (i, 0, 0, 0))
    return pl.pallas_call(
        body, name=name, grid=(4,),
        in_specs=[smem, smem, pl.BlockSpec((1, BLK, 4 * BLK), lambda i: (i, 0, 0))],
        out_specs=tuple([blk] * depth), out_shape=tuple([jax.ShapeDtypeStruct((4, A_HEADS, BLK, 4 * BLK), F32)] * depth),
        compiler_params=_params("parallel"),
    )(table, sinks, buckets)


def _win_base(n, NB):
    return jnp.where(n == NB, 0, jnp.clip(n - 1, 0, NB - 3))


def _win_variant(n, NB):
    return jnp.where(n == 0, 0, jnp.where(n < NB - 1, 1, jnp.where(n == NB - 1, 2, 3)))


def _win_kv(k_ref, v_ref, base, S, kvh):
    cs = slice(LANES * kvh, LANES * (kvh + 1))
    k2 = jnp.concatenate([k_ref[pl.ds(base, 3 * BLK), cs], k_ref[S:S + BLK, cs]], axis=0).astype(BF16)
    v2 = jnp.concatenate([v_ref[pl.ds(base, 3 * BLK), cs], v_ref[S:S + BLK, cs]], axis=0)
    lane = _lane(v2.shape)
    v2 = jnp.where(lane < A_DH, v2, jnp.where(lane < A_DH + 2, -1.0, 0.0)).astype(BF16)
    return k2, v2


def _win_fwd(proj, bias, Bl, S, name):
    T = proj.shape[0]
    Lp = S + BLK
    NB = S // BLK
    nblk = Lp // BLK
    G = A_HEADS // A_KV

    def body(q_ref, k_ref, v_ref, b_ref, o_ref, lse_ref):
        n = pl.program_id(1)
        base = pl.multiple_of(_win_base(n, NB) * BLK, BLK)
        lane = _lane((BLK, LANES))
        lo = lane < A_DH
        kv = [_win_kv(k_ref, v_ref, base, S, kvh) for kvh in range(A_KV)]
        scores = []
        for h in range(A_HEADS):
            qp = q_ref[:, LANES * (h // 2):LANES * (h // 2 + 1)]
            qm = (jnp.where(lo if h % 2 == 0 else ~lo, qp, 0.0) * (A_SCALE * LOG2E)).astype(BF16)
            scores.append(_dot_nt(qm, kv[h // G][0]) + b_ref[0, h])
        ms = [jnp.max(s, axis=-1, keepdims=True) for s in scores]
        es = [jnp.exp2(s - m).astype(BF16) for s, m in zip(scores, ms)]
        outs = []
        t = jnp.zeros((BLK, LANES), F32)
        for h in range(A_HEADS):
            o2 = _dot(es[h], kv[h // G][1])
            l = -jnp.sum(jnp.where(lane == A_DH, o2, 0.0), axis=-1, keepdims=True)
            outs.append(o2 * (1.0 / l))
            t = jnp.where(lane == h, ms[h] + jnp.log(l) * LOG2E, t)
        for j in range(A_HEADS // 2):
            o_ref[:, LANES * j:LANES * (j + 1)] = jnp.where(lo, outs[2 * j], pltpu.roll(outs[2 * j + 1], A_DH, 1))
        lse_ref[...] = t

    qrow = lambda b, n: (b * nblk + n, 0)
    return pl.pallas_call(
        body, name=name, grid=(Bl, nblk),
        in_specs=[pl.BlockSpec((BLK, A_W), qrow),
                  pl.BlockSpec((Lp, 256), lambda b, n: (b, KA0 // 256)), pl.BlockSpec((Lp, 256), lambda b, n: (b, VA0 // 256)),
                  pl.BlockSpec((1, A_HEADS, BLK, 4 * BLK), lambda b, n: (_win_variant(n, NB), 0, 0, 0))],
        out_specs=(pl.BlockSpec((BLK, A_W), qrow), pl.BlockSpec((BLK, LANES), qrow)),
        out_shape=(jax.ShapeDtypeStruct((T, A_W), F32), jax.ShapeDtypeStruct((T, LANES), F32)),
        compiler_params=_params("parallel", "arbitrary"),
    )(proj, proj, proj, bias)


def _q_tiles(Lp):
    S = Lp - BLK
    return S // 256, S


def _mla_fwd(q, k, v, Bl, Lp, name):
    T = q.shape[0]
    n_big, S = _q_tiles(Lp)

    def body(q_ref, k_ref, v_ref, o_ref, lse_ref):
        def qtile(r0, nq):
            lane = _lane((nq, LANES))
            outs = []
            t = jnp.zeros((nq, LANES), F32)
            cols = [slice(LANES * hh, LANES * (hh + 1)) for hh in range(2)]
            scores = [_dot_nt(q_ref[pl.ds(r0, nq), cs], k_ref[:, cs]) for cs in cols]
            for hh in range(2):
                cs, s = cols[hh], scores[hh]
                m = jnp.max(s, axis=-1, keepdims=True)
                o2 = _dot(jnp.exp2(s - m).astype(BF16), v_ref[:, cs])
                l = -jnp.sum(jnp.where(lane == SUM_LANE, o2, 0.0), axis=-1, keepdims=True)
                outs.append(o2 * (1.0 / l))
                t = jnp.where(lane == hh, m + jnp.log(l) * LOG2E, t)
            o_ref[pl.ds(r0, nq), :] = jnp.where(lane < B_V, outs[0], pltpu.roll(outs[1], B_V, 1))
            lse_ref[pl.ds(r0, nq), :] = t

        def step(i, c):
            qtile(pl.multiple_of(i * 256, 256), 256)
            return c

        lax.fori_loop(0, n_big, step, 0)
        qtile(S, BLK)

    pair = lambda b, j: (b, j)
    wide, narrow = pl.BlockSpec((Lp, 256), pair), pl.BlockSpec((Lp, LANES), pair)
    return pl.pallas_call(
        body, name=name, grid=(Bl, B_HEADS // 2), in_specs=[wide, wide, wide], out_specs=(narrow, narrow),
        out_shape=(jax.ShapeDtypeStruct((T, B_W), F32), jax.ShapeDtypeStruct((T, B_W), F32)),
        compiler_params=_params("parallel", "parallel"),
    )(q, k, v)


def _gated(y, gate, gain):
    r = _rstd(y)
    nrm = y * r
    sg = jax.nn.sigmoid(gate)
    return r, nrm, sg, nrm * gain, gate * sg


def _out_fwd(ya, yb, proj, na, nb, wout, h, tm, name):
    T = h.shape[0]

    def body(ya_ref, yb_ref, ga_ref, gb_ref, na_ref, nb_ref, w_ref, h_ref, o_ref):
        _, _, _, n_a, sl_a = _gated(ya_ref[...], ga_ref[...], na_ref[...])
        _, _, _, n_b, sl_b = _gated(yb_ref[...], gb_ref[...], nb_ref[...])
        acc = _dot((n_a * sl_a).astype(BF16), w_ref[0:A_W, :]) + _dot((n_b * sl_b).astype(BF16), w_ref[A_W:A_W + B_W, :])
        o_ref[...] = h_ref[...] + acc

    half = pl.BlockSpec((tm, 512), _row(0))
    return pl.pallas_call(
        body, name=name, grid=(T // tm,),
        in_specs=[half, half, pl.BlockSpec((tm, 512), _row(GA0 // 512)), pl.BlockSpec((tm, 512), _row(GB0 // 512)),
                  pl.BlockSpec((1, 512), _const2), pl.BlockSpec((1, 512), _const2), pl.BlockSpec((D, D), _const2),
                  pl.BlockSpec((tm, D), _row(0))],
        out_specs=pl.BlockSpec((tm, D), _row(0)), out_shape=jax.ShapeDtypeStruct((T, D), F32),
        compiler_params=_params("parallel"),
    )(ya, yb, proj, proj, na, nb, wout, h)


def _loss_head(h, gf, target, Bl, S, name):
    T = h.shape[0]
    nblk = (S + BLK) // BLK
    NB = S // BLK

    def body(h_ref, g_ref, t_ref, dh_ref, loss_ref, dg_ref):
        b, n = pl.program_id(0), pl.program_id(1)

        @pl.when((b == 0) & (n == 0))
        def _():
            loss_ref[...] = jnp.zeros_like(loss_ref)
            dg_ref[...] = jnp.zeros_like(dg_ref)

        @pl.when(n < NB)
        def _():
            x = h_ref[...]
            g = g_ref[...]
            r = _rstd(x)
            err = x * r * g - t_ref[0]
            loss_ref[...] += 0.5 * jnp.sum(jnp.mean(err * err, axis=-1, keepdims=True))
            dx, dg = _rms_bwd(err * (1.0 / D), x, r, g)
            dh_ref[...] = dx
            dg_ref[...] += dg

        @pl.when(n >= NB)
        def _():
            dh_ref[...] = jnp.zeros_like(dh_ref)

    return pl.pallas_call(
        body, name=name, grid=(Bl, nblk),
        in_specs=[pl.BlockSpec((BLK, D), lambda b, n: (b * nblk + n, 0)), pl.BlockSpec((1, D), lambda b, n: (0, 0)),
                  pl.BlockSpec((1, BLK, D), lambda b, n: (b, jnp.minimum(n, NB - 1), 0))],
        out_specs=(pl.BlockSpec((BLK, D), lambda b, n: (b * nblk + n, 0)), pl.BlockSpec((8, LANES), lambda b, n: (0, 0)),
                   pl.BlockSpec((1, D), lambda b, n: (0, 0))),
        out_shape=(jax.ShapeDtypeStruct((T, D), F32), jax.ShapeDtypeStruct((8, LANES), F32), jax.ShapeDtypeStruct((1, D), F32)),
        compiler_params=_params("arbitrary", "arbitrary"),
    )(h, gf, target)


def _out_bwd(dh, ya, yb, proj, na, nb, wout, tm, name):
    T = dh.shape[0]

    def body(dh_ref, ya_ref, yb_ref, ga_ref, gb_ref, na_ref, nb_ref, w_ref,
             dya_ref, dyb_ref, dga_ref, dgb_ref, dw_ref, dna_ref, dnb_ref):
        @pl.when(pl.program_id(0) == 0)
        def _():
            dw_ref[...] = jnp.zeros_like(dw_ref)
            dna_ref[...] = jnp.zeros_like(dna_ref)
            dnb_ref[...] = jnp.zeros_like(dnb_ref)

        dhb = dh_ref[...].astype(BF16)
        dy = _dot_nt(dhb, w_ref[...])
        parts = ((ya_ref, ga_ref, na_ref, dya_ref, dga_ref, dna_ref, 0), (yb_ref, gb_ref, nb_ref, dyb_ref, dgb_ref, dnb_ref, A_W))
        for y_ref, gate_ref, gain_ref, dy_out, dgate_out, dgain_out, c0 in parts:
            y, gate, gain = y_ref[...], gate_ref[...], gain_ref[...]
            r, nrm, sg, n_g, sl = _gated(y, gate, gain)
            dyp = dy[:, c0:c0 + 512]
            dgate_out[...] = dyp * n_g * (sg * (1.0 + gate * (1.0 - sg)))
            dx, dgain = _rms_bwd(dyp * sl, y, r, gain)
            dy_out[...] = dx
            dgain_out[...] += dgain
            dw_ref[c0:c0 + 512, :] += _dot_tn((n_g * sl).astype(BF16), dhb)

    half = pl.BlockSpec((tm, 512), _row(0))
    vec = pl.BlockSpec((1, 512), _const2)
    return pl.pallas_call(
        body, name=name, grid=(T // tm,),
        in_specs=[pl.BlockSpec((tm, D), _row(0)), half, half, pl.BlockSpec((tm, 512), _row(GA0 // 512)),
                  pl.BlockSpec((tm, 512), _row(GB0 // 512)), vec, vec, pl.BlockSpec((D, D), _const2)],
        out_specs=(half, half, half, half, pl.BlockSpec((D, D), _const2), vec, vec),
        out_shape=(jax.ShapeDtypeStruct((T, 512), F32),) * 4 + (jax.ShapeDtypeStruct((D, D), F32),)
        + (jax.ShapeDtypeStruct((1, 512), F32),) * 2,
        compiler_params=_params("arbitrary"),
    )(dh, ya, yb, proj, proj, na, nb, wout)


def _mla_bwd(q, k, v, o, do, lse, after, Bl, Lp, name):
    T = q.shape[0]
    n_big, S = _q_tiles(Lp)

    def body(q_ref, k_ref, v_ref, o_ref, do_ref, lse_ref, _, dq_ref, dk_ref, dv_ref):
        dk_ref[...] = jnp.zeros_like(dk_ref)
        dv_ref[...] = jnp.zeros_like(dv_ref)

        def qtile(r0, nq):
            lane = _lane((nq, LANES))
            dop = do_ref[pl.ds(r0, nq), :]
            op = o_ref[pl.ds(r0, nq), :]
            lsev = lse_ref[pl.ds(r0, nq), :]
            for hh in range(2):
                cs = slice(LANES * hh, LANES * (hh + 1))
                dom = jnp.where((lane < B_V) if hh == 0 else (lane >= B_V), dop, 0.0)
                delta = jnp.sum(dom * op, axis=-1, keepdims=True)
                d_hi = delta.astype(BF16).astype(F32)
                x = dom if hh == 0 else pltpu.roll(dom, B_V, 1)
                domx = jnp.where(lane == SUM_LANE, d_hi, jnp.where(lane == SUM_LANE + 1, delta - d_hi, x)).astype(BF16)
                qh, kh, vh = q_ref[pl.ds(r0, nq), cs], k_ref[:, cs], v_ref[:, cs]
                lse_h = jnp.sum(jnp.where(lane == hh, lsev, 0.0), axis=-1, keepdims=True)
                p = jnp.exp2(_dot_nt(qh, kh) - lse_h)
                ds = (p * _dot_nt(domx, vh)).astype(BF16)
                dq_ref[pl.ds(r0, nq), cs] = _dot(ds, kh) * B_SCALE
                dk_ref[:, cs] += _dot_tn(ds, qh)
                dv_ref[:, cs] += _dot_tn(p.astype(BF16), domx)

        def step(i, c):
            qtile(pl.multiple_of(i * 256, 256), 256)
            return c

        lax.fori_loop(0, n_big, step, 0)
        qtile(S, BLK)
        dk_ref[...] = dk_ref[...] * (1.0 / LOG2E)

    pair = lambda b, j: (b, j)
    wide, narrow = pl.BlockSpec((Lp, 256), pair), pl.BlockSpec((Lp, LANES), pair)
    return pl.pallas_call(
        body, name=name, grid=(Bl, B_HEADS // 2),
        in_specs=[wide, wide, wide, narrow, narrow, narrow, pl.BlockSpec(memory_space=pl.ANY)],
        out_specs=(wide, wide, wide), out_shape=(jax.ShapeDtypeStruct((T, 1024), F32),) * 3,
        compiler_params=_params("parallel", "parallel"),
    )(q, k, v, o, do, lse, after)


def _win_bwd(proj, bias, ya, dya, lse, Bl, S, name):
    T = proj.shape[0]
    Lp = S + BLK
    NB = S // BLK
    nblk = Lp // BLK
    G = A_HEADS // A_KV

    def body(q_ref, k_ref, v_ref, b_ref, o_ref, do_ref, lse_ref, dq_ref, dk_ref, dv_ref, db_ref):
        n = pl.program_id(1)

        @pl.when(n == 0)
        def _():
            dk_ref[...] = jnp.zeros_like(dk_ref)
            dv_ref[...] = jnp.zeros_like(dv_ref)

        @pl.when((n == 0) | (n == 1) | (n == NB - 1) | (n == NB))
        def _():
            db_ref[...] = jnp.zeros_like(db_ref)

        base = pl.multiple_of(_win_base(n, NB) * BLK, BLK)
        lane = _lane((BLK, LANES))
        lo = lane < A_DH
        lsev = lse_ref[...]
        kv = [_win_kv(k_ref, v_ref, base, S, kvh) for kvh in range(A_KV)]
        qms, pbs, domxs, dsls = [], [], [], []
        for h in range(A_HEADS):
            k2, v2 = kv[h // G]
            ps = slice(LANES * (h // 2), LANES * (h // 2 + 1))
            qm = (jnp.where(lo if h % 2 == 0 else ~lo, q_ref[:, ps], 0.0) * (A_SCALE * LOG2E)).astype(BF16)
            lse_h = jnp.sum(jnp.where(lane == h, lsev, 0.0), axis=-1, keepdims=True)
            p = jnp.exp2(_dot_nt(qm, k2) + b_ref[0, h] - lse_h)
            dop, op = do_ref[:, ps], o_ref[:, ps]
            if h % 2 == 1:
                dop, op = pltpu.roll(dop, A_DH, 1), pltpu.roll(op, A_DH, 1)
            dom = jnp.where(lo, dop, 0.0)
            delta = jnp.sum(dom * op, axis=-1, keepdims=True)
            d_hi = delta.astype(BF16).astype(F32)
            domx = jnp.where(lane == A_DH, d_hi, jnp.where(lane == A_DH + 1, delta - d_hi, dom)).astype(BF16)
            dsc = p * _dot_nt(domx, v2)
            db_ref[0, 0, h] += dsc
            qms.append(qm)
            pbs.append(p.astype(BF16))
            domxs.append(domx)
            dsls.append(dsc.astype(BF16))
        dqs = [_dot(dsls[h], kv[h // G][0]) * A_SCALE for h in range(A_HEADS)]
        for kvh in range(A_KV):
            cs = slice(LANES * kvh, LANES * (kvh + 1))
            group = range(kvh * G, (kvh + 1) * G)
            stack = lambda parts: jnp.concatenate([parts[h] for h in group], axis=0)
            dk_acc = _dot_tn(stack(dsls), stack(qms)) * (1.0 / LOG2E)
            dv_acc = _dot_tn(stack(pbs), stack(domxs))
            dv_acc = jnp.where(_lane(dv_acc.shape) < A_DH, dv_acc, 0.0)
            dk_ref[pl.ds(base, 3 * BLK), cs] += dk_acc[0:3 * BLK]
            dk_ref[S:S + BLK, cs] += dk_acc[3 * BLK:4 * BLK]
            dv_ref[pl.ds(base, 3 * BLK), cs] += dv_acc[0:3 * BLK]
            dv_ref[S:S + BLK, cs] += dv_acc[3 * BLK:4 * BLK]
        for j in range(A_HEADS // 2):
            dq_ref[:, LANES * j:LANES * (j + 1)] = jnp.where(lo, dqs[2 * j], dqs[2 * j + 1])

    qrow = lambda b, n: (b * nblk + n, 0)
    kvs = pl.BlockSpec((Lp, 256), lambda b, n: (b, 0))
    return pl.pallas_call(
        body, name=name, grid=(Bl, nblk),
        in_specs=[pl.BlockSpec((BLK, A_W), qrow),
                  pl.BlockSpec((Lp, 256), lambda b, n: (b, KA0 // 256)), pl.BlockSpec((Lp, 256), lambda b, n: (b, VA0 // 256)),
                  pl.BlockSpec((1, A_HEADS, BLK, 4 * BLK), lambda b, n: (_win_variant(n, NB), 0, 0, 0)),
                  pl.BlockSpec((BLK, A_W), qrow), pl.BlockSpec((BLK, A_W), qrow), pl.BlockSpec((BLK, LANES), qrow)],
        out_specs=(pl.BlockSpec((BLK, A_W), qrow), kvs, kvs,
                   pl.BlockSpec((1, 1, A_HEADS, BLK, 4 * BLK), lambda b, n: (b, _win_variant(n, NB), 0, 0, 0))),
        out_shape=(jax.ShapeDtypeStruct((T, A_W), F32), jax.ShapeDtypeStruct((T, 256), F32),
                   jax.ShapeDtypeStruct((T, 256), F32), jax.ShapeDtypeStruct((Bl, 4, A_HEADS, BLK, 4 * BLK), F32)),
        compiler_params=_params("parallel", "arbitrary"),
    )(proj, proj, proj, bias, ya, dya, lse)


def _table_grad(dbias_list, buckets, after, name):
    nl = len(dbias_list)
    Bl = dbias_list[0].shape[0]

    def body(*refs):
        d_refs = refs[:nl]
        b_ref, _, o_ref, s_ref, acc, part = refs[nl:]
        v, b = pl.program_id(0), pl.program_id(1)

        @pl.when((v == 0) & (b == 0))
        def _():
            o_ref[...] = jnp.zeros_like(o_ref)
            s_ref[...] = jnp.zeros_like(s_ref)

        slot = _lane((BLK, 4 * BLK))
        for l in range(nl):
            for h in range(A_HEADS):
                col = jnp.sum(jnp.where(slot == SINK_SLOT, d_refs[l][0, 0, h], 0.0), axis=0, keepdims=True)
                s_ref[l, h:h + 1, :] += jnp.sum(col, axis=1, keepdims=True)

        tot = d_refs[0][0, 0]
        for r in d_refs[1:]:
            tot = tot + r[0, 0]

        @pl.when(b == 0)
        def _():
            acc[...] = tot

        @pl.when(b > 0)
        def _():
            acc[...] += tot

        @pl.when(b == Bl - 1)
        def _():
            bidx = b_ref[0]

            def step(j, c):
                mask = bidx == j
                for h in range(A_HEADS):
                    part[h, j] = jnp.sum(jnp.where(mask, acc[h], 0.0).reshape(BLK // 8, 8, 4 * BLK), axis=0)
                return c

            lax.fori_loop(0, N_BUCKETS, step, 0)
            rows = lax.broadcasted_iota(jnp.int32, (N_BUCKETS, LANES), 0)
            lanes = _lane((N_BUCKETS, LANES))
            for h in range(A_HEADS):
                col = jnp.sum(jnp.sum(part[h], axis=1), axis=-1, keepdims=True)
                o_ref[h:h + 1, :] += jnp.sum(jnp.where(rows == lanes, col, 0.0), axis=0, keepdims=True)

    return pl.pallas_call(
        body, name=name, grid=(4, Bl),
        in_specs=[pl.BlockSpec((1, 1, A_HEADS, BLK, 4 * BLK), lambda v, b: (b, v, 0, 0, 0))] * nl
        + [pl.BlockSpec((1, BLK, 4 * BLK), lambda v, b: (v, 0, 0)), pl.BlockSpec(memory_space=pl.ANY)],
        out_specs=(pl.BlockSpec((8, LANES), lambda v, b: (0, 0)), pl.BlockSpec((nl, 8, LANES), lambda v, b: (0, 0, 0))),
        out_shape=(jax.ShapeDtypeStruct((8, LANES), F32), jax.ShapeDtypeStruct((nl, 8, LANES), F32)),
        scratch_shapes=[pltpu.VMEM((A_HEADS, BLK, 4 * BLK), F32), pltpu.VMEM((A_HEADS, N_BUCKETS, 8, 4 * BLK), F32)],
        compiler_params=_params("arbitrary", "arbitrary"),
    )(*dbias_list, buckets, after)


def _mla_prep_bwd(dq, dk, dv, proj, gq, gkv, wq, wkk, wkv, tq, tk, tm, name):
    T = proj.shape[0]

    def body(dq_ref, dk_ref, dv_ref, cq_ref, ckv_ref, gq_ref, gkv_ref, wq_ref, wkk_ref, wkv_ref, tq_ref, tk_ref,
             dcq_ref, dckv_ref, dkr_ref, dwq_ref, dwkk_ref, dwkv_ref, dgq_ref, dgkv_ref):
        @pl.when(pl.program_id(0) == 0)
        def _():
            for r in (dwq_ref, dwkk_ref, dwkv_ref, dgq_ref, dgkv_ref):
                r[...] = jnp.zeros_like(r)

        tqv = tq_ref[...]
        dqp = jnp.concatenate([_rope_t(dq_ref[:, LANES * h:LANES * (h + 1)], tqv) for h in range(B_HEADS)],
                              axis=1).astype(BF16)
        cq, gq_ = cq_ref[...], gq_ref[...]
        rq = _rstd(cq)
        dwq_ref[...] += _dot_tn((cq * rq * gq_).astype(BF16), dqp)
        dx, dg = _rms_bwd(_dot_nt(dqp, wq_ref[...]), cq, rq, gq_)
        dcq_ref[...] = dx
        dgq_ref[...] += dg

        dkv_ = dk_ref[...]
        dks = dkv_[:, 0:LANES]
        for h in range(1, B_HEADS):
            dks = dks + dkv_[:, LANES * h:LANES * (h + 1)]
        dkr_ref[...] = _rope_t(pltpu.roll(dks, NOPE, 1), tk_ref[...])
        dkb = dkv_.astype(BF16)
        dvb = dv_ref[...].astype(BF16)
        ckv, gkv_ = ckv_ref[...], gkv_ref[...]
        rk = _rstd(ckv)
        cb = (ckv * rk * gkv_).astype(BF16)
        dwkk_ref[...] += _dot_tn(cb, dkb)
        dwkv_ref[...] += _dot_tn(cb, dvb)
        dx, dg = _rms_bwd(_dot_nt(dkb, wkk_ref[...]) + _dot_nt(dvb, wkv_ref[...]), ckv, rk, gkv_)
        dckv_ref[...] = dx
        dgkv_ref[...] += dg

    tab = lambda n: pl.BlockSpec((n, tm, LANES), lambda i: (0, i, 0))
    wide = pl.BlockSpec((tm, 1024), _row(0))
    return pl.pallas_call(
        body, name=name, grid=(T // tm,),
        in_specs=[wide, wide, wide, pl.BlockSpec((tm, 256), _row(CQ0 // 256)),
                  pl.BlockSpec((tm, 128), _row(CKV0 // 128)), pl.BlockSpec((1, Q_RANK), _const2),
                  pl.BlockSpec((1, KV_RANK), _const2), pl.BlockSpec((Q_RANK, 1024), _const2),
                  pl.BlockSpec((KV_RANK, 1024), _const2), pl.BlockSpec((KV_RANK, 1024), _const2), tab(3), tab(4)],
        out_specs=(pl.BlockSpec((tm, 256), _row(0)), pl.BlockSpec((tm, 128), _row(0)), pl.BlockSpec((tm, 128), _row(0)),
                   pl.BlockSpec((Q_RANK, 1024), _const2), pl.BlockSpec((KV_RANK, 1024), _const2),
                   pl.BlockSpec((KV_RANK, 1024), _const2), pl.BlockSpec((1, Q_RANK), _const2),
                   pl.BlockSpec((1, KV_RANK), _const2)),
        out_shape=(jax.ShapeDtypeStruct((T, 256), F32), jax.ShapeDtypeStruct((T, 128), F32),
                   jax.ShapeDtypeStruct((T, 128), F32), jax.ShapeDtypeStruct((Q_RANK, 1024), F32),
                   jax.ShapeDtypeStruct((KV_RANK, 1024), F32), jax.ShapeDtypeStruct((KV_RANK, 1024), F32),
                   jax.ShapeDtypeStruct((1, Q_RANK), F32), jax.ShapeDtypeStruct((1, KV_RANK), F32)),
        compiler_params=_params("arbitrary"),
    )(dq, dk, dv, proj, proj, gq, gkv, wq, wkk, wkv, tq, tk)


def _inproj_bwd(pieces, h, g, wx, dh_out, tm, name):
    T = h.shape[0]
    nsteps = T // tm
    CH = 512

    def body(*refs):
        p_refs = refs[:len(pieces)]
        h_ref, g_ref, w_ref, dho_ref, dh_ref, dw_hbm, dg_ref, acc, sem = refs[len(pieces):]
        i = pl.program_id(0)

        @pl.when(i == 0)
        def _():
            acc[...] = jnp.zeros_like(acc)
            dg_ref[...] = jnp.zeros_like(dg_ref)

        dp = jnp.concatenate([r[...].astype(BF16) for r in p_refs], axis=1)
        x, gain = h_ref[...], g_ref[...]
        r = _rstd(x)
        u = (x * r * gain).astype(BF16)
        for c in range(0, NX, CH):
            acc[:, c:c + CH] += _dot_tn(u, dp[:, c:c + CH])
        dx, dg = _rms_bwd(_dot_nt(dp, w_ref[...]), x, r, gain)
        dh_ref[...] = dho_ref[...] + dx
        dg_ref[...] += dg

        @pl.when(i == nsteps - 1)
        def _():
            cp = pltpu.make_async_copy(acc, dw_hbm, sem)
            cp.start()
            cp.wait()

    return pl.pallas_call(
        body, name=name, grid=(nsteps,),
        in_specs=[pl.BlockSpec((tm, p.shape[1]), _row(0)) for p in pieces]
        + [pl.BlockSpec((tm, D), _row(0)), pl.BlockSpec((1, D), _const2), pl.BlockSpec((D, NX), _const2),
           pl.BlockSpec((tm, D), _row(0))],
        out_specs=(pl.BlockSpec((tm, D), _row(0)), pl.BlockSpec(memory_space=pl.ANY), pl.BlockSpec((1, D), _const2)),
        out_shape=(jax.ShapeDtypeStruct((T, D), F32), jax.ShapeDtypeStruct((D, NX), F32), jax.ShapeDtypeStruct((1, D), F32)),
        scratch_shapes=[pltpu.VMEM((D, NX), F32), pltpu.SemaphoreType.DMA(())],
        compiler_params=_params("arbitrary"),
    )(*pieces, h, g, wx, dh_out)


def kernel(x, meta_tokens, rel_bias_table, norm_in, w_in, sink_a, norm_q_lat, w_uq, norm_kv_lat, w_ukv, norm_out_a, norm_out_b, w_out, norm_final, loss_target, m_meta_tokens, m_rel_bias_table, m_norm_in, m_w_in, m_sink_a, m_norm_q_lat, m_w_uq, m_norm_kv_lat, m_w_ukv, m_norm_out_a, m_norm_out_b, m_w_out, m_norm_final, v_meta_tokens, v_rel_bias_table, v_norm_in, v_w_in, v_sink_a, v_norm_q_lat, v_w_uq, v_norm_kv_lat, v_w_ukv, v_norm_out_a, v_norm_out_b, v_w_out, v_norm_final):
    Bl, S, _ = x.shape
    assert S % 256 == 0 and S >= 3 * BLK, "the attention kernels tile the real tokens in 256-row blocks"
    Lp = S + BLK
    T = Bl * Lp
    tm = Lp // 4
    tq = Lp // 8
    depth = w_in.shape[0]
    me = 4 * lax.axis_index("x") + 2 * lax.axis_index("y") + lax.axis_index("c")

    def finish(handles, after, name):
        srcs, lands = _xchg_wait(handles, after, name)
        full = []
        for hd, src, land in zip(handles, srcs, lands):
            own = src if hd[4] == "gather" else lax.dynamic_index_in_dim(src, me, 0, keepdims=False)
            full.append(_own_slot(land, own, me))
        return full

    def weights_of(i):
        return [w_in[i].astype(BF16), w_uq[i].astype(BF16), w_ukv[i].astype(BF16), w_out[i].astype(BF16)]

    def packed(l_in, l_uq, l_ukv, l_out):
        cols = lambda t: jnp.transpose(t, (1, 0, 2)).reshape(t.shape[1], NDEV * t.shape[2])
        wkk, wkv = _pack_w_ukv(cols(l_ukv))
        return _pack_w_in(cols(l_in)), _pack_w_uq(cols(l_uq)), wkk, wkv, l_out.reshape(D, D)

    wb = weights_of(0)
    gat_a, tok = _xchg_start([wb[0], meta_tokens], ["gather", "gather"], "gather_start_0a")
    buckets = _bias_buckets(S)
    biases = _build_bias(buckets, _after(rel_bias_table, tok), sink_a, "build_bias")
    tb = 256 if T % 256 == 0 else tq
    tq_tab = jnp.tile(_rope_tables(S, Lp, NOPE, True), (1, Bl, 1))
    key_pad = jnp.where((jnp.arange(Lp) >= S + NMETA)[:, None] & (jnp.arange(LANES) == MASK_LANE)[None, :], NEG, 0.0)
    tk_tab = jnp.tile(jnp.concatenate([_rope_tables(S, Lp, 0, False), key_pad.astype(F32)[None]], axis=0), (1, Bl, 1))
    l_in, l_meta = finish(gat_a, biases[0], "gather_wait_0a")
    gat_b, tok = _xchg_start(wb[1:], ["gather"] * 3, "gather_start_0b", after=l_in)

    meta_f = jnp.transpose(l_meta, (1, 0, 2)).reshape(NMETA, D)
    tail = jnp.concatenate([meta_f, jnp.zeros((BLK - NMETA, D), F32)], axis=0)
    h = jnp.concatenate([x, jnp.broadcast_to(tail[None], (Bl, BLK, D))], axis=1).reshape(T, D)

    saved = []
    w_out_f = [None] * depth
    gat_next = None
    for i in range(depth):
        g_in, g_q, g_kv = norm_in[i][None], norm_q_lat[i][None], norm_kv_lat[i][None]
        g_a, g_b = norm_out_a[i][None], norm_out_b[i][None]
        if i == 0:
            proj = _inproj_fwd(h, _after(g_in, tok), _pack_w_in(jnp.transpose(l_in, (1, 0, 2)).reshape(D, IN_W)), tm,
                               f"inproj_fwd_{i}")
            l_uq, l_ukv, l_out = finish(gat_b, proj, "gather_wait_0b")
            wx, wq, wkk, wkv, w_out_f[i] = packed(l_in, l_uq, l_ukv, l_out)
        else:
            wx, wq, wkk, wkv, w_out_f[i] = packed(*finish(gat_next, h, f"gather_wait_{i}"))
            proj = _inproj_fwd(h, g_in, wx, tm, f"inproj_fwd_{i}")
        if i + 1 < depth:
            gat_next, tok = _xchg_start(weights_of(i + 1), ["gather"] * 4, f"gather_start_{i + 1}", after=w_out_f[i])
            g_q = _after(g_q, tok)
        q, k, v = _mla_prep_fwd(proj, g_q, g_kv, wq, wkk, wkv, tq_tab, tk_tab, tm, f"mla_prep_fwd_{i}")
        ya, lse_a = _win_fwd(proj, biases[i], Bl, S, f"win_fwd_{i}")
        yb, lse_b = _mla_fwd(q, k, v, Bl, Lp, f"mla_fwd_{i}")
        h_new = _out_fwd(ya, yb, proj, g_a, g_b, w_out_f[i], h, tm, f"out_fwd_{i}")
        saved.append((h, proj, q, k, v, ya, lse_a, yb, lse_b, wx, wq, wkk, wkv))
        h = h_new

    dh, loss_acc, dg_final = _loss_head(h, norm_final[None], loss_target, Bl, S, "loss_head")

    g_n_in, g_nq, g_nkv, g_na, g_nb = ([None] * depth for _ in range(5))
    dbias_all, sc_out, sc_rest = [None] * depth, [None] * depth, [None] * depth
    split = lambda t, n: jnp.transpose(t.reshape(t.shape[0], NDEV, n), (1, 0, 2)).astype(BF16)
    tok = None
    for i in reversed(range(depth)):
        h_in, proj, q, k, v, ya, lse_a, yb, lse_b, wx, wq, wkk, wkv = saved[i]
        g_in, g_q, g_kv = norm_in[i][None], norm_q_lat[i][None], norm_kv_lat[i][None]
        g_a, g_b = norm_out_a[i][None], norm_out_b[i][None]
        if tok is not None:
            g_a = _after(g_a, tok)
        dya, dyb, dga, dgb, g_w_out, g_na[i], g_nb[i] = _out_bwd(dh, ya, yb, proj, g_a, g_b, w_out_f[i], tb, f"out_bwd_{i}")
        sc_out[i], tok = _xchg_start([g_w_out.reshape(NDEV, D // NDEV, D).astype(BF16)], ["scatter"], f"scatter_start_{i}a")
        dq, dk, dv = _mla_bwd(q, k, v, yb, dyb, lse_b, tok, Bl, Lp, f"mla_bwd_{i}")
        dqa, dka, dva, dbias = _win_bwd(proj, biases[i], ya, dya, lse_a, Bl, S, f"win_bwd_{i}")
        dcq, dckv, dkr, dwq, dwkk, dwkv, g_nq[i], g_nkv[i] = _mla_prep_bwd(
            dq, dk, dv, proj, g_q, g_kv, wq, wkk, wkv, tq_tab, tk_tab, tm, f"mla_prep_bwd_{i}")
        dh, dwx, g_n_in[i] = _inproj_bwd((dqa, dka, dva, dga, dcq, dckv, dkr, dgb), h_in, g_in, wx, dh, tb,
                                         f"inproj_bwd_{i}")
        dbias_all[i] = dbias
        rest = [split(_unpack_w_in_grad(dwx), IN_W // NDEV), split(_unpack_w_uq_grad(dwq), 768 // NDEV),
                split(_unpack_w_ukv_grad(dwkk, dwkv), 1024 // NDEV)]
        if i == 0:
            dh3 = dh.reshape(Bl, Lp, D)
            grad_x = dh3[:, :S]
            rest.append(jnp.transpose(jnp.sum(dh3[:, S:S + NMETA], axis=0).reshape(NMETA, NDEV, D // NDEV), (1, 0, 2)))
        sc_rest[i], tok = _xchg_start(rest, ["scatter"] * len(rest), f"scatter_start_{i}b")

    dtab, dsink = _table_grad(dbias_all, buckets, tok, "table_grad")
    two_d = lambda a: a.reshape(1, -1) if a.ndim == 1 else a
    small_w = [rel_bias_table, norm_in, sink_a, norm_q_lat, norm_kv_lat, norm_out_a, norm_out_b, norm_final]
    small_m = [m_rel_bias_table, m_norm_in, m_sink_a, m_norm_q_lat, m_norm_kv_lat, m_norm_out_a, m_norm_out_b, m_norm_final]
    small_v = [v_rel_bias_table, v_norm_in, v_sink_a, v_norm_q_lat, v_norm_kv_lat, v_norm_out_a, v_norm_out_b, v_norm_final]
    small_g = [jnp.transpose(dtab[:, :N_BUCKETS]), jnp.concatenate(g_n_in), dsink[:, :, 0], jnp.concatenate(g_nq),
               jnp.concatenate(g_nkv), jnp.concatenate(g_na), jnp.concatenate(g_nb), dg_final]
    sc_small, tok = _xchg_start(small_g + [loss_acc], ["gather"] * (len(small_g) + 1), "scatter_start_small")

    r_out, r_in, r_uq, r_ukv = [None] * depth, [None] * depth, [None] * depth, [None] * depth
    for i in reversed(range(depth)):
        (r_out[i],) = finish(sc_out[i], tok, f"scatter_wait_{i}a")
        got = finish(sc_rest[i], tok, f"scatter_wait_{i}b")
        r_in[i], r_uq[i], r_ukv[i] = got[:3]
        if i == 0:
            r_meta = got[3]
    r_small = finish(sc_small, tok, "scatter_wait_small")

    res = {}
    sharded = dict(w_in=(r_in, w_in, m_w_in, v_w_in), w_uq=(r_uq, w_uq, m_w_uq, v_w_uq), w_ukv=(r_ukv, w_ukv, m_w_ukv, v_w_ukv),
                   w_out=(r_out, w_out, m_w_out, v_w_out),
                   meta_tokens=([r_meta], meta_tokens[None], m_meta_tokens[None], v_meta_tokens[None]))
    for n, (r, w, m, v) in sharded.items():
        outs = _adamw_param(r, w, m, v, f"adamw_{n}")
        res[n] = [o[0] for o in outs] if n == "meta_tokens" else list(outs)
    names_small = ["rel_bias_table", "norm_in", "sink_a", "norm_q_lat", "norm_kv_lat", "norm_out_a", "norm_out_b", "norm_final"]
    outs = _adamw_small(r_small[:-1], [two_d(a) for a in small_w], [two_d(a) for a in small_m], [two_d(a) for a in small_v],
                        r_small[-1], "adamw_replicated")
    loss = outs[-1][0, 0]
    ns = len(names_small)
    for j, n in enumerate(names_small):
        res[n] = [outs[kk * ns + j].reshape(small_w[j].shape) for kk in range(4)]
    order = ["meta_tokens", "rel_bias_table", "norm_in", "w_in", "sink_a", "norm_q_lat", "w_uq", "norm_kv_lat", "w_ukv",
             "norm_out_a", "norm_out_b", "w_out", "norm_final"]
    return (loss, grad_x, *[res[n][kk] for kk in range(4) for n in order])
```

```python
import functools
import math

import numpy as np
import jax
import jax.numpy as jnp
from jax import lax
from jax.experimental import pallas as pl
from jax.experimental.pallas import tpu as pltpu

F32, BF16 = jnp.float32, jnp.bfloat16
D = 1024
NMETA = 16
BLK = 128
A_HEADS, A_KV, A_DH, A_W = 8, 2, 64, 512
B_HEADS, NOPE, ROPE, B_V, B_W = 8, 64, 32, 64, 512
Q_RANK, KV_RANK = 256, 128
IN_W = 2208
N_BUCKETS, MAX_DIST = 32, 128
THETA = 10000.0
EPS = 1e-6
NEG = -1e30
A_SCALE = A_DH ** -0.5
B_SCALE = (NOPE + ROPE) ** -0.5
LOG2E = math.log2(math.e)
MASK_LANE = NOPE + ROPE
SUM_LANE = B_V
LANES = 128
NDEV = 8
MESH = pl.DeviceIdType.MESH

NX = 2560
QA0, KA0, VA0, GA0, CQ0, CKV0, KR0, GB0 = 0, 512, 768, 1024, 1536, 1792, 1920, 2048

ADAM_LR, ADAM_B1, ADAM_B2, ADAM_EPS, ADAM_WD, ADAM_STEP = 0.001, 0.9, 0.999, 1e-08, 0.01, 10


def _dot(a, b):
    return jnp.dot(a, b, preferred_element_type=F32)


def _dot_nt(a, b):
    return lax.dot_general(a, b, (((1,), (1,)), ((), ())), preferred_element_type=F32)


def _dot_tn(a, b):
    return lax.dot_general(a, b, (((0,), (0,)), ((), ())), preferred_element_type=F32)


def _lane(shape):
    return lax.broadcasted_iota(jnp.int32, shape, len(shape) - 1)


def _rstd(x):
    return lax.rsqrt(jnp.mean(x * x, axis=-1, keepdims=True) + EPS)


def _rms_bwd(dn, x, r, g):
    z = dn * g
    dx = r * z - x * (r * r * r) * jnp.mean(z * x, axis=-1, keepdims=True)
    return dx, jnp.sum(dn * (x * r), axis=0, keepdims=True)


def _row(i):
    return lambda *ids: (ids[0], i)


def _const2(*ids):
    return (0, 0)


def _params(*sem):
    return pltpu.CompilerParams(dimension_semantics=sem)


SHARD_W = IN_W // NDEV


def _pack_w_in(shards):
    def nat(a, b):
        parts = []
        while a < b:
            p, e = a // SHARD_W, min(b, (a // SHARD_W + 1) * SHARD_W)
            parts.append(shards[p][:, a - p * SHARD_W:e - p * SHARD_W])
            a = e
        return parts
    dup = lambda a: nat(a, a + 64) * 2 + nat(a + 64, a + 128) * 2
    zeros = [jnp.zeros((shards.shape[1], LANES - ROPE), shards.dtype)]
    return jnp.concatenate(nat(0, 512) + dup(512) + dup(640) + nat(768, 1280) + nat(1280, 1536) + nat(1536, 1664)
                           + nat(1664, 1696) + zeros + nat(1696, 2208), axis=1)


_NAT_GROUPS = ((0, 512, QA0), (512, 576, KA0), (576, 640, KA0 + 128), (640, 704, VA0), (704, 768, VA0 + 128),
               (768, 1280, GA0), (1280, 1536, CQ0), (1536, 1664, CKV0), (1664, 1696, KR0), (1696, 2208, GB0))


def _unpack_w_in_grad(g):
    def nat(a, b):
        lo, _, pk = next(grp for grp in _NAT_GROUPS if grp[0] <= a < grp[1])
        t = g[:, pk + a - lo:pk + b - lo]
        return t + g[:, pk + 64 + a - lo:pk + 64 + b - lo] if 512 <= a < 768 else t
    cuts = sorted({c for grp in _NAT_GROUPS for c in grp[:2]} | {p * SHARD_W for p in range(NDEV + 1)})
    shards = [[] for _ in range(NDEV)]
    for a, b in zip(cuts[:-1], cuts[1:]):
        shards[a // SHARD_W].append(nat(a, b))
    return jnp.stack([jnp.concatenate(parts, axis=1) for parts in shards])


def _pack_w_uq(w):
    t = w.reshape(Q_RANK, B_HEADS, NOPE + ROPE)
    t = jnp.concatenate([t, jnp.zeros((Q_RANK, B_HEADS, LANES - NOPE - ROPE), w.dtype)], axis=-1)
    return t.reshape(Q_RANK, B_HEADS * LANES)


def _unpack_w_uq_grad(g):
    return g.reshape(Q_RANK, B_HEADS, LANES)[:, :, :NOPE + ROPE].reshape(Q_RANK, B_HEADS * (NOPE + ROPE))


def _pack_w_ukv(w):
    t = w.reshape(KV_RANK, B_HEADS, NOPE + B_V)
    z = jnp.zeros((KV_RANK, B_HEADS, LANES - NOPE), w.dtype)
    pad = lambda u: jnp.concatenate([u, z], axis=-1).reshape(KV_RANK, B_HEADS * LANES)
    return pad(t[:, :, :NOPE]), pad(t[:, :, NOPE:])


def _unpack_w_ukv_grad(gk, gv):
    head = lambda g: g.reshape(KV_RANK, B_HEADS, LANES)[:, :, :NOPE]
    return jnp.concatenate([head(gk), head(gv)], axis=-1).reshape(KV_RANK, B_HEADS * (NOPE + B_V))


def _t5_bucket(rel):
    nb = N_BUCKETS // 2
    max_exact = nb // 2
    ret = jnp.where(rel > 0, nb, 0)
    n = jnp.abs(rel)
    nf = jnp.maximum(n, 1).astype(F32)
    large = max_exact + (jnp.log(nf / max_exact) / math.log(MAX_DIST / max_exact) * (nb - max_exact)).astype(jnp.int32)
    large = jnp.minimum(large, nb - 1)
    return ret + jnp.where(n < max_exact, n, large)


def _bias_buckets(S):
    q = np.arange(BLK)[:, None]
    k = np.arange(4 * BLK)[None, :]
    is_meta_key = (k >= 3 * BLK) & (k < 3 * BLK + NMETA)
    mi = k - 3 * BLK
    rels, valids = [], []
    for shift in (0, BLK, 2 * BLK):
        rel_real = k - shift - q
        valid_real = (k < 3 * BLK) & (np.abs(rel_real) <= BLK)
        far = -(NMETA + MAX_DIST + BLK)
        rel_meta = (mi - (NMETA + q)) if shift == 0 else np.full_like(k + q, far)
        rels.append(np.where(is_meta_key, rel_meta, rel_real))
        valids.append(valid_real | is_meta_key)
    qm = q < NMETA
    rel_real = NMETA + k - q
    valid_real = (k < BLK) & (np.abs(rel_real) <= BLK) & qm
    rels.append(np.where(is_meta_key, mi - q, rel_real))
    valids.append(valid_real | is_meta_key)
    rel = jnp.asarray(np.stack(rels).astype(np.int32))
    valid = jnp.asarray(np.stack(valids))
    return jnp.where(valid, _t5_bucket(rel), -1).astype(jnp.int32)


def _rope_tables(S, Lp, off, passthrough):
    half = ROPE // 2
    r = np.arange(Lp)
    pos = np.where(r < S, NMETA + r, np.where(r < S + NMETA, r - S, 0)).astype(np.float32)
    freqs = THETA ** (-jnp.arange(half, dtype=F32) / half)
    ang = jnp.asarray(pos)[:, None] * freqs[None, :]
    cos, sin = jnp.cos(ang), jnp.sin(ang)
    z = lambda n: jnp.zeros((Lp, n), F32)
    head = jnp.ones((Lp, off), F32) if passthrough else z(off)
    c = jnp.concatenate([head, cos, cos, z(LANES - off - ROPE)], axis=1)
    s1 = jnp.concatenate([z(off), -sin, z(LANES - off - half)], axis=1)
    s2 = jnp.concatenate([z(off + half), sin, z(LANES - off - ROPE)], axis=1)
    return jnp.stack([c, s1, s2])


def _rope(x, t):
    return x * t[0] + pltpu.roll(x, LANES - 16, 1) * t[1] + pltpu.roll(x, 16, 1) * t[2]


def _rope_t(dy, t):
    return dy * t[0] + pltpu.roll(dy * t[1], 16, 1) + pltpu.roll(dy * t[2], LANES - 16, 1)


def _me_and_peers():
    x, y, c = lax.axis_index("x"), lax.axis_index("y"), lax.axis_index("c")
    flip = lambda v, b: 1 - v if b else v
    peers = [(flip(x, k & 4), flip(y, k & 2), flip(c, k & 1)) for k in range(1, NDEV)]
    return 4 * x + 2 * y + c, peers


def _xchg_start(srcs, kinds, name, after=None):
    n = len(srcs)
    lands = [lax.empty((NDEV,) + s.shape[-2:], s.dtype) for s in srcs]
    extra = [] if after is None else [after]

    def body(*refs):
        src_refs, land_refs = refs[:n], refs[n:2 * n]
        ssems, rsems = refs[2 * n + len(extra):3 * n + len(extra)], refs[3 * n + len(extra):4 * n + len(extra)]
        token = refs[6 * n + len(extra)]
        me, peers = _me_and_peers()
        for it in range(n):
            for k, (px, py, pc) in enumerate(peers):
                src = src_refs[it] if kinds[it] == "gather" else src_refs[it].at[4 * px + 2 * py + pc]
                pltpu.make_async_remote_copy(src_ref=src, dst_ref=land_refs[it].at[me], send_sem=ssems[it].at[k],
                                             recv_sem=rsems[it].at[k], device_id=(px, py, pc), device_id_type=MESH).start()
        token[...] = jnp.zeros_like(token)

    hbm = pl.BlockSpec(memory_space=pltpu.HBM)
    sem = pl.BlockSpec(memory_space=pltpu.SEMAPHORE)
    outs = pl.pallas_call(
        body, name=name,
        out_shape=tuple([pltpu.SemaphoreType.DMA((NDEV - 1,))] * (2 * n) + [pltpu.HBM(a.shape, a.dtype) for a in srcs + lands]
                        + [jax.ShapeDtypeStruct((8, LANES), F32)]),
        in_specs=[hbm] * (2 * n) + [pl.BlockSpec(memory_space=pl.ANY)] * len(extra),
        out_specs=tuple([sem] * (2 * n) + [hbm] * (2 * n) + [pl.BlockSpec(memory_space=pltpu.VMEM)]),
        input_output_aliases={i: 2 * n + i for i in range(2 * n)},
        compiler_params=pltpu.CompilerParams(has_side_effects=pltpu.SideEffectType.DATAFLOW_SIDE_EFFECTING),
    )(*[pltpu.with_memory_space_constraint(a, pltpu.HBM) for a in srcs + lands], *extra)
    handles = [(outs[it], outs[n + it], outs[2 * n + it], outs[3 * n + it], kinds[it]) for it in range(n)]
    return handles, outs[4 * n]


def _xchg_wait(handles, after, name):
    n = len(handles)

    def body(*refs):
        src_refs, land_refs = refs[:n], refs[n:2 * n]
        ssems, rsems = refs[2 * n:3 * n], refs[3 * n:4 * n]
        me, peers = _me_and_peers()
        for it in range(n):
            for k, (px, py, pc) in enumerate(peers):
                src = src_refs[it] if handles[it][4] == "gather" else src_refs[it].at[4 * px + 2 * py + pc]
                cp = pltpu.make_async_remote_copy(src_ref=src, dst_ref=land_refs[it].at[me], send_sem=ssems[it].at[k],
                                                  recv_sem=rsems[it].at[k], device_id=(px, py, pc), device_id_type=MESH)
                cp.wait_send()
                cp.wait_recv()

    hbm = pl.BlockSpec(memory_space=pltpu.HBM)
    sem = pl.BlockSpec(memory_space=pltpu.SEMAPHORE)
    srcs, lands = [h[2] for h in handles], [h[3] for h in handles]
    outs = pl.pallas_call(
        body, name=name, out_shape=tuple(pltpu.HBM(a.shape, a.dtype) for a in srcs + lands),
        in_specs=[hbm] * (2 * n) + [sem] * (2 * n) + [pl.BlockSpec(memory_space=pl.ANY)], out_specs=tuple([hbm] * (2 * n)),
        input_output_aliases={i: i for i in range(2 * n)},
        compiler_params=pltpu.CompilerParams(has_side_effects=pltpu.SideEffectType.DATAFLOW_SIDE_EFFECTING),
    )(*srcs, *lands, *[h[0] for h in handles], *[h[1] for h in handles], after)
    return list(outs[:n]), list(outs[n:])


def _own_slot(land, own, me):
    return lax.dynamic_update_slice(land, own[None].astype(land.dtype), (me, 0, 0))


def _after(x, token):
    return x + token[0, 0]


def _adamw_update(g, w, m, v):
    mn = ADAM_B1 * m + (1.0 - ADAM_B1) * g
    vn = ADAM_B2 * v + (1.0 - ADAM_B2) * (g * g)
    m_hat = mn / (1.0 - ADAM_B1 ** ADAM_STEP)
    v_hat = vn / (1.0 - ADAM_B2 ** ADAM_STEP)
    return -ADAM_LR * (m_hat / (jnp.sqrt(v_hat) + ADAM_EPS) + ADAM_WD * w), mn, vn


def _sum_partials(r_ref):
    g = r_ref[0].astype(F32)
    for p in range(1, NDEV):
        g = g + r_ref[p].astype(F32)
    return g


def _adamw_param(recvs, w, m, v, name):
    depth, r, c = w.shape
    tr = min(r, 256)
    nl = len(recvs)

    def body(*refs):
        r_refs = refs[:nl]
        w_ref, m_ref, v_ref, g_out, d_out, m_out, v_out = refs[nl:]
        for li in range(nl):
            @pl.when(pl.program_id(0) == li)
            def _(li=li):
                g = _sum_partials(r_refs[li])
                g_out[0] = g
                d_out[0], m_out[0], v_out[0] = _adamw_update(g, w_ref[0], m_ref[0], v_ref[0])

    blk = pl.BlockSpec((1, tr, c), lambda l, i: (l, i, 0))
    return pl.pallas_call(
        body, name=name, grid=(depth, r // tr),
        in_specs=[pl.BlockSpec((NDEV, tr, c), lambda l, i: (0, i, 0))] * nl + [blk, blk, blk],
        out_specs=(blk, blk, blk, blk), out_shape=(jax.ShapeDtypeStruct(w.shape, F32),) * 4,
        compiler_params=_params("arbitrary", "arbitrary"),
    )(*recvs, w, m, v)


def _adamw_small(recvs, ws, ms, vs, loss_parts, name):
    n = len(ws)

    def body(*refs):
        r_refs, w_refs, m_refs, v_refs = refs[:n], refs[n:2 * n], refs[2 * n:3 * n], refs[3 * n:4 * n]
        outs = refs[4 * n + 1:]
        for j in range(n):
            g = _sum_partials(r_refs[j])
            outs[j][...] = g
            outs[n + j][...], outs[2 * n + j][...], outs[3 * n + j][...] = _adamw_update(
                g, w_refs[j][...], m_refs[j][...], v_refs[j][...])
        outs[4 * n][...] = _sum_partials(refs[4 * n])

    vm = pl.BlockSpec(memory_space=pltpu.VMEM)
    return pl.pallas_call(
        body, name=name, in_specs=[vm] * (4 * n + 1), out_specs=tuple([vm] * (4 * n + 1)),
        out_shape=tuple(jax.ShapeDtypeStruct(a.shape, F32) for a in ws) * 4 + (jax.ShapeDtypeStruct(loss_parts.shape[1:], F32),),
    )(*recvs, *ws, *ms, *vs, loss_parts)


def _inproj_fwd(h, g, wx, tm, name):
    T = h.shape[0]

    def body(h_ref, g_ref, w_ref, o_ref):
        x = h_ref[...]
        u = (x * _rstd(x) * g_ref[...]).astype(BF16)
        o_ref[...] = _dot(u, w_ref[...])

    return pl.pallas_call(
        body, name=name, grid=(T // tm,),
        in_specs=[pl.BlockSpec((tm, D), _row(0)), pl.BlockSpec((1, D), _const2), pl.BlockSpec((D, NX), _const2)],
        out_specs=pl.BlockSpec((tm, NX), _row(0)), out_shape=jax.ShapeDtypeStruct((T, NX), F32),
        compiler_params=_params("parallel"),
    )(h, g, wx)


def _mla_prep_fwd(proj, gq, gkv, wq, wkk, wkv, tq, tk, tm, name):
    T = proj.shape[0]

    def body(cq_ref, ckv_ref, kr_ref, gq_ref, gkv_ref, wq_ref, wkk_ref, wkv_ref, tq_ref, tk_ref, q_ref, k_ref, v_ref):
        cq = cq_ref[...]
        q = _dot((cq * _rstd(cq) * gq_ref[...]).astype(BF16), wq_ref[...])
        tqv = tq_ref[...]
        lane = _lane((1, LANES))
        one = jnp.where(lane == MASK_LANE, 1.0, 0.0)
        for h in range(B_HEADS):
            cs = slice(LANES * h, LANES * (h + 1))
            q_ref[:, cs] = (_rope(q[:, cs], tqv) * (B_SCALE * LOG2E) + one).astype(BF16)
        tkv = tk_ref[...]
        ksh = pltpu.roll(_rope(kr_ref[...], tkv), NOPE, 1) + tkv[3]
        ckv = ckv_ref[...]
        cb = (ckv * _rstd(ckv) * gkv_ref[...]).astype(BF16)
        kn = _dot(cb, wkk_ref[...])
        vn = _dot(cb, wkv_ref[...])
        minus = jnp.where((lane == SUM_LANE) | (lane == SUM_LANE + 1), -1.0, 0.0)
        for h in range(B_HEADS):
            cs = slice(LANES * h, LANES * (h + 1))
            k_ref[:, cs] = (kn[:, cs] + ksh).astype(BF16)
            v_ref[:, cs] = (vn[:, cs] + minus).astype(BF16)

    tab = lambda n: pl.BlockSpec((n, tm, LANES), lambda i: (0, i, 0))
    wide = pl.BlockSpec((tm, 1024), _row(0))
    return pl.pallas_call(
        body, name=name, grid=(T // tm,),
        in_specs=[pl.BlockSpec((tm, 256), _row(CQ0 // 256)), pl.BlockSpec((tm, 128), _row(CKV0 // 128)),
                  pl.BlockSpec((tm, 128), _row(KR0 // 128)), pl.BlockSpec((1, Q_RANK), _const2),
                  pl.BlockSpec((1, KV_RANK), _const2), pl.BlockSpec((Q_RANK, 1024), _const2),
                  pl.BlockSpec((KV_RANK, 1024), _const2), pl.BlockSpec((KV_RANK, 1024), _const2), tab(3), tab(4)],
        out_specs=(wide, wide, wide), out_shape=(jax.ShapeDtypeStruct((T, 1024), BF16),) * 3,
        compiler_params=_params("parallel"),
    )(proj, proj, proj, gq, gkv, wq, wkk, wkv, tq, tk)


SINK_SLOT = 4 * BLK - 1


def _build_bias(buckets, table, sinks, name):
    depth = sinks.shape[0]

    def body(tab_ref, sink_ref, b_ref, *o_refs):
        bidx = b_ref[0]
        slot = _lane(bidx.shape)
        for h in range(A_HEADS):
            acc = jnp.full(bidx.shape, NEG, F32)
            for j in range(N_BUCKETS):
                acc = jnp.where(bidx == j, tab_ref[j, h] * LOG2E, acc)
            for l in range(depth):
                o_refs[l][0, h] = jnp.where(slot == SINK_SLOT, sink_ref[l, h] * LOG2E, acc)

    smem = pl.BlockSpec(memory_space=pltpu.SMEM)
    blk = pl.BlockSpec((1, A_HEADS, BLK, 4 * BLK), lambda i: (i, 0, 0, 0))
    return pl.pallas_call(
        body, name=name, grid=(4,),
        in_specs=[smem, smem, pl.BlockSpec((1, BLK, 4 * BLK), lambda i: (i, 0, 0))],
        out_specs=tuple([blk] * depth), out_shape=tuple([jax.ShapeDtypeStruct((4, A_HEADS, BLK, 4 * BLK), F32)] * depth),
        compiler_params=_params("parallel"),
    )(table, sinks, buckets)


def _win_base(n, NB):
    return jnp.where(n == NB, 0, jnp.clip(n - 1, 0, NB - 3))


def _win_variant(n, NB):
    return jnp.where(n == 0, 0, jnp.where(n < NB - 1, 1, jnp.where(n == NB - 1, 2, 3)))


def _win_kv(k_ref, v_ref, base, S, kvh):
    cs = slice(LANES * kvh, LANES * (kvh + 1))
    k2 = jnp.concatenate([k_ref[pl.ds(base, 3 * BLK), cs], k_ref[S:S + BLK, cs]], axis=0).astype(BF16)
    v2 = jnp.concatenate([v_ref[pl.ds(base, 3 * BLK), cs], v_ref[S:S + BLK, cs]], axis=0)
    lane = _lane(v2.shape)
    v2 = jnp.where(lane < A_DH, v2, jnp.where(lane < A_DH + 2, -1.0, 0.0)).astype(BF16)
    return k2, v2


def _win_fwd(proj, bias, Bl, S, name):
    T = proj.shape[0]
    Lp = S + BLK
    NB = S // BLK
    nblk = Lp // BLK
    G = A_HEADS // A_KV

    def body(q_ref, k_ref, v_ref, b_ref, o_ref, lse_ref):
        n = pl.program_id(1)
        base = pl.multiple_of(_win_base(n, NB) * BLK, BLK)
        lane = _lane((BLK, LANES))
        lo = lane < A_DH
        kv = [_win_kv(k_ref, v_ref, base, S, kvh) for kvh in range(A_KV)]
        scores = []
        for h in range(A_HEADS):
            qp = q_ref[:, LANES * (h // 2):LANES * (h // 2 + 1)]
            qm = (jnp.where(lo if h % 2 == 0 else ~lo, qp, 0.0) * (A_SCALE * LOG2E)).astype(BF16)
            scores.append(_dot_nt(qm, kv[h // G][0]) + b_ref[0, h])
        ms = [jnp.max(s, axis=-1, keepdims=True) for s in scores]
        es = [jnp.exp2(s - m).astype(BF16) for s, m in zip(scores, ms)]
        outs = []
        t = jnp.zeros((BLK, LANES), F32)
        for h in range(A_HEADS):
            o2 = _dot(es[h], kv[h // G][1])
            l = -jnp.sum(jnp.where(lane == A_DH, o2, 0.0), axis=-1, keepdims=True)
            outs.append(o2 * (1.0 / l))
            t = jnp.where(lane == h, ms[h] + jnp.log(l) * LOG2E, t)
        for j in range(A_HEADS // 2):
            o_ref[:, LANES * j:LANES * (j + 1)] = jnp.where(lo, outs[2 * j], pltpu.roll(outs[2 * j + 1], A_DH, 1))
        lse_ref[...] = t

    qrow = lambda b, n: (b * nblk + n, 0)
    return pl.pallas_call(
        body, name=name, grid=(Bl, nblk),
        in_specs=[pl.BlockSpec((BLK, A_W), qrow),
                  pl.BlockSpec((Lp, 256), lambda b, n: (b, KA0 // 256)), pl.BlockSpec((Lp, 256), lambda b, n: (b, VA0 // 256)),
                  pl.BlockSpec((1, A_HEADS, BLK, 4 * BLK), lambda b, n: (_win_variant(n, NB), 0, 0, 0))],
        out_specs=(pl.BlockSpec((BLK, A_W), qrow), pl.BlockSpec((BLK, LANES), qrow)),
        out_shape=(jax.ShapeDtypeStruct((T, A_W), F32), jax.ShapeDtypeStruct((T, LANES), F32)),
        compiler_params=_params("parallel", "arbitrary"),
    )(proj, proj, proj, bias)


def _q_tiles(Lp):
    S = Lp - BLK
    return S // 256, S


def _mla_fwd(q, k, v, Bl, Lp, name):
    T = q.shape[0]
    n_big, S = _q_tiles(Lp)

    def body(q_ref, k_ref, v_ref, o_ref, lse_ref):
        def qtile(r0, nq):
            lane = _lane((nq, LANES))
            outs = []
            t = jnp.zeros((nq, LANES), F32)
            cols = [slice(LANES * hh, LANES * (hh + 1)) for hh in range(2)]
            scores = [_dot_nt(q_ref[pl.ds(r0, nq), cs], k_ref[:, cs]) for cs in cols]
            for hh in range(2):
                cs, s = cols[hh], scores[hh]
                m = jnp.max(s, axis=-1, keepdims=True)
                o2 = _dot(jnp.exp2(s - m).astype(BF16), v_ref[:, cs])
                l = -jnp.sum(jnp.where(lane == SUM_LANE, o2, 0.0), axis=-1, keepdims=True)
                outs.append(o2 * (1.0 / l))
                t = jnp.where(lane == hh, m + jnp.log(l) * LOG2E, t)
            o_ref[pl.ds(r0, nq), :] = jnp.where(lane < B_V, outs[0], pltpu.roll(outs[1], B_V, 1))
            lse_ref[pl.ds(r0, nq), :] = t

        def step(i, c):
            qtile(pl.multiple_of(i * 256, 256), 256)
            return c

        lax.fori_loop(0, n_big, step, 0)
        qtile(S, BLK)

    pair = lambda b, j: (b, j)
    wide, narrow = pl.BlockSpec((Lp, 256), pair), pl.BlockSpec((Lp, LANES), pair)
    return pl.pallas_call(
        body, name=name, grid=(Bl, B_HEADS // 2), in_specs=[wide, wide, wide], out_specs=(narrow, narrow),
        out_shape=(jax.ShapeDtypeStruct((T, B_W), F32), jax.ShapeDtypeStruct((T, B_W), F32)),
        compiler_params=_params("parallel", "parallel"),
    )(q, k, v)


def _gated(y, gate, gain):
    r = _rstd(y)
    nrm = y * r
    sg = jax.nn.sigmoid(gate)
    return r, nrm, sg, nrm * gain, gate * sg


def _out_fwd(ya, yb, proj, na, nb, wout, h, tm, name):
    T = h.shape[0]

    def body(ya_ref, yb_ref, ga_ref, gb_ref, na_ref, nb_ref, w_ref, h_ref, o_ref):
        _, _, _, n_a, sl_a = _gated(ya_ref[...], ga_ref[...], na_ref[...])
        _, _, _, n_b, sl_b = _gated(yb_ref[...], gb_ref[...], nb_ref[...])
        acc = _dot((n_a * sl_a).astype(BF16), w_ref[0:A_W, :]) + _dot((n_b * sl_b).astype(BF16), w_ref[A_W:A_W + B_W, :])
        o_ref[...] = h_ref[...] + acc

    half = pl.BlockSpec((tm, 512), _row(0))
    return pl.pallas_call(
        body, name=name, grid=(T // tm,),
        in_specs=[half, half, pl.BlockSpec((tm, 512), _row(GA0 // 512)), pl.BlockSpec((tm, 512), _row(GB0 // 512)),
                  pl.BlockSpec((1, 512), _const2), pl.BlockSpec((1, 512), _const2), pl.BlockSpec((D, D), _const2),
                  pl.BlockSpec((tm, D), _row(0))],
        out_specs=pl.BlockSpec((tm, D), _row(0)), out_shape=jax.ShapeDtypeStruct((T, D), F32),
        compiler_params=_params("parallel"),
    )(ya, yb, proj, proj, na, nb, wout, h)


def _loss_head(h, gf, target, Bl, S, name):
    T = h.shape[0]
    nblk = (S + BLK) // BLK
    NB = S // BLK

    def body(h_ref, g_ref, t_ref, dh_ref, loss_ref, dg_ref):
        b, n = pl.program_id(0), pl.program_id(1)

        @pl.when((b == 0) & (n == 0))
        def _():
            loss_ref[...] = jnp.zeros_like(loss_ref)
            dg_ref[...] = jnp.zeros_like(dg_ref)

        @pl.when(n < NB)
        def _():
            x = h_ref[...]
            g = g_ref[...]
            r = _rstd(x)
            err = x * r * g - t_ref[0]
            loss_ref[...] += 0.5 * jnp.sum(jnp.mean(err * err, axis=-1, keepdims=True))
            dx, dg = _rms_bwd(err * (1.0 / D), x, r, g)
            dh_ref[...] = dx
            dg_ref[...] += dg

        @pl.when(n >= NB)
        def _():
            dh_ref[...] = jnp.zeros_like(dh_ref)

    return pl.pallas_call(
        body, name=name, grid=(Bl, nblk),
        in_specs=[pl.BlockSpec((BLK, D), lambda b, n: (b * nblk + n, 0)), pl.BlockSpec((1, D), lambda b, n: (0, 0)),
                  pl.BlockSpec((1, BLK, D), lambda b, n: (b, jnp.minimum(n, NB - 1), 0))],
        out_specs=(pl.BlockSpec((BLK, D), lambda b, n: (b * nblk + n, 0)), pl.BlockSpec((8, LANES), lambda b, n: (0, 0)),
                   pl.BlockSpec((1, D), lambda b, n: (0, 0))),
        out_shape=(jax.ShapeDtypeStruct((T, D), F32), jax.ShapeDtypeStruct((8, LANES), F32), jax.ShapeDtypeStruct((1, D), F32)),
        compiler_params=_params("arbitrary", "arbitrary"),
    )(h, gf, target)


def _out_bwd(dh, ya, yb, proj, na, nb, wout, tm, name):
    T = dh.shape[0]

    def body(dh_ref, ya_ref, yb_ref, ga_ref, gb_ref, na_ref, nb_ref, w_ref,
             dya_ref, dyb_ref, dga_ref, dgb_ref, dw_ref, dna_ref, dnb_ref):
        @pl.when(pl.program_id(0) == 0)
        def _():
            dw_ref[...] = jnp.zeros_like(dw_ref)
            dna_ref[...] = jnp.zeros_like(dna_ref)
            dnb_ref[...] = jnp.zeros_like(dnb_ref)

        dhb = dh_ref[...].astype(BF16)
        dy = _dot_nt(dhb, w_ref[...])
        parts = ((ya_ref, ga_ref, na_ref, dya_ref, dga_ref, dna_ref, 0), (yb_ref, gb_ref, nb_ref, dyb_ref, dgb_ref, dnb_ref, A_W))
        for y_ref, gate_ref, gain_ref, dy_out, dgate_out, dgain_out, c0 in parts:
            y, gate, gain = y_ref[...], gate_ref[...], gain_ref[...]
            r, nrm, sg, n_g, sl = _gated(y, gate, gain)
            dyp = dy[:, c0:c0 + 512]
            dgate_out[...] = dyp * n_g * (sg * (1.0 + gate * (1.0 - sg)))
            dx, dgain = _rms_bwd(dyp * sl, y, r, gain)
            dy_out[...] = dx
            dgain_out[...] += dgain
            dw_ref[c0:c0 + 512, :] += _dot_tn((n_g * sl).astype(BF16), dhb)

    half = pl.BlockSpec((tm, 512), _row(0))
    vec = pl.BlockSpec((1, 512), _const2)
    return pl.pallas_call(
        body, name=name, grid=(T // tm,),
        in_specs=[pl.BlockSpec((tm, D), _row(0)), half, half, pl.BlockSpec((tm, 512), _row(GA0 // 512)),
                  pl.BlockSpec((tm, 512), _row(GB0 // 512)), vec, vec, pl.BlockSpec((D, D), _const2)],
        out_specs=(half, half, half, half, pl.BlockSpec((D, D), _const2), vec, vec),
        out_shape=(jax.ShapeDtypeStruct((T, 512), F32),) * 4 + (jax.ShapeDtypeStruct((D, D), F32),)
        + (jax.ShapeDtypeStruct((1, 512), F32),) * 2,
        compiler_params=_params("arbitrary"),
    )(dh, ya, yb, proj, proj, na, nb, wout)


def _mla_bwd(q, k, v, o, do, lse, after, Bl, Lp, name):
    T = q.shape[0]
    n_big, S = _q_tiles(Lp)

    def body(q_ref, k_ref, v_ref, o_ref, do_ref, lse_ref, _, dq_ref, dk_ref, dv_ref):
        dk_ref[...] = jnp.zeros_like(dk_ref)
        dv_ref[...] = jnp.zeros_like(dv_ref)

        def qtile(r0, nq):
            lane = _lane((nq, LANES))
            dop = do_ref[pl.ds(r0, nq), :]
            op = o_ref[pl.ds(r0, nq), :]
            lsev = lse_ref[pl.ds(r0, nq), :]
            for hh in range(2):
                cs = slice(LANES * hh, LANES * (hh + 1))
                dom = jnp.where((lane < B_V) if hh == 0 else (lane >= B_V), dop, 0.0)
                delta = jnp.sum(dom * op, axis=-1, keepdims=True)
                d_hi = delta.astype(BF16).astype(F32)
                x = dom if hh == 0 else pltpu.roll(dom, B_V, 1)
                domx = jnp.where(lane == SUM_LANE, d_hi, jnp.where(lane == SUM_LANE + 1, delta - d_hi, x)).astype(BF16)
                qh, kh, vh = q_ref[pl.ds(r0, nq), cs], k_ref[:, cs], v_ref[:, cs]
                lse_h = jnp.sum(jnp.where(lane == hh, lsev, 0.0), axis=-1, keepdims=True)
                p = jnp.exp2(_dot_nt(qh, kh) - lse_h)
                ds = (p * _dot_nt(domx, vh)).astype(BF16)
                dq_ref[pl.ds(r0, nq), cs] = _dot(ds, kh) * B_SCALE
                dk_ref[:, cs] += _dot_tn(ds, qh)
                dv_ref[:, cs] += _dot_tn(p.astype(BF16), domx)

        def step(i, c):
            qtile(pl.multiple_of(i * 256, 256), 256)
            return c

        lax.fori_loop(0, n_big, step, 0)
        qtile(S, BLK)
        dk_ref[...] = dk_ref[...] * (1.0 / LOG2E)

    pair = lambda b, j: (b, j)
    wide, narrow = pl.BlockSpec((Lp, 256), pair), pl.BlockSpec((Lp, LANES), pair)
    return pl.pallas_call(
        body, name=name, grid=(Bl, B_HEADS // 2),
        in_specs=[wide, wide, wide, narrow, narrow, narrow, pl.BlockSpec(memory_space=pl.ANY)],
        out_specs=(wide, wide, wide), out_shape=(jax.ShapeDtypeStruct((T, 1024), F32),) * 3,
        compiler_params=_params("parallel", "parallel"),
    )(q, k, v, o, do, lse, after)


def _win_bwd(proj, bias, ya, dya, lse, Bl, S, name):
    T = proj.shape[0]
    Lp = S + BLK
    NB = S // BLK
    nblk = Lp // BLK
    G = A_HEADS // A_KV

    def body(q_ref, k_ref, v_ref, b_ref, o_ref, do_ref, lse_ref, dq_ref, dk_ref, dv_ref, db_ref):
        n = pl.program_id(1)

        @pl.when(n == 0)
        def _():
            dk_ref[...] = jnp.zeros_like(dk_ref)
            dv_ref[...] = jnp.zeros_like(dv_ref)

        @pl.when((n == 0) | (n == 1) | (n == NB - 1) | (n == NB))
        def _():
            db_ref[...] = jnp.zeros_like(db_ref)

        base = pl.multiple_of(_win_base(n, NB) * BLK, BLK)
        lane = _lane((BLK, LANES))
        lo = lane < A_DH
        lsev = lse_ref[...]
        kv = [_win_kv(k_ref, v_ref, base, S, kvh) for kvh in range(A_KV)]
        qms, pbs, domxs, dsls = [], [], [], []
        for h in range(A_HEADS):
            k2, v2 = kv[h // G]
            ps = slice(LANES * (h // 2), LANES * (h // 2 + 1))
            qm = (jnp.where(lo if h % 2 == 0 else ~lo, q_ref[:, ps], 0.0) * (A_SCALE * LOG2E)).astype(BF16)
            lse_h = jnp.sum(jnp.where(lane == h, lsev, 0.0), axis=-1, keepdims=True)
            p = jnp.exp2(_dot_nt(qm, k2) + b_ref[0, h] - lse_h)
            dop, op = do_ref[:, ps], o_ref[:, ps]
            if h % 2 == 1:
                dop, op = pltpu.roll(dop, A_DH, 1), pltpu.roll(op, A_DH, 1)
            dom = jnp.where(lo, dop, 0.0)
            delta = jnp.sum(dom * op, axis=-1, keepdims=True)
            d_hi = delta.astype(BF16).astype(F32)
            domx = jnp.where(lane == A_DH, d_hi, jnp.where(lane == A_DH + 1, delta - d_hi, dom)).astype(BF16)
            dsc = p * _dot_nt(domx, v2)
            db_ref[0, 0, h] += dsc
            qms.append(qm)
            pbs.append(p.astype(BF16))
            domxs.append(domx)
            dsls.append(dsc.astype(BF16))
        dqs = [_dot(dsls[h], kv[h // G][0]) * A_SCALE for h in range(A_HEADS)]
        for kvh in range(A_KV):
            cs = slice(LANES * kvh, LANES * (kvh + 1))
            group = range(kvh * G, (kvh + 1) * G)
            stack = lambda parts: jnp.concatenate([parts[h] for h in group], axis=0)
            dk_acc = _dot_tn(stack(dsls), stack(qms)) * (1.0 / LOG2E)
            dv_acc = _dot_tn(stack(pbs), stack(domxs))
            dv_acc = jnp.where(_lane(dv_acc.shape) < A_DH, dv_acc, 0.0)
            dk_ref[pl.ds(base, 3 * BLK), cs] += dk_acc[0:3 * BLK]
            dk_ref[S:S + BLK, cs] += dk_acc[3 * BLK:4 * BLK]
            dv_ref[pl.ds(base, 3 * BLK), cs] += dv_acc[0:3 * BLK]
            dv_ref[S:S + BLK, cs] += dv_acc[3 * BLK:4 * BLK]
        for j in range(A_HEADS // 2):
            dq_ref[:, LANES * j:LANES * (j + 1)] = jnp.where(lo, dqs[2 * j], dqs[2 * j + 1])

    qrow = lambda b, n: (b * nblk + n, 0)
    kvs = pl.BlockSpec((Lp, 256), lambda b, n: (b, 0))
    return pl.pallas_call(
        body, name=name, grid=(Bl, nblk),
        in_specs=[pl.BlockSpec((BLK, A_W), qrow),
                  pl.BlockSpec((Lp, 256), lambda b, n: (b, KA0 // 256)), pl.BlockSpec((Lp, 256), lambda b, n: (b, VA0 // 256)),
                  pl.BlockSpec((1, A_HEADS, BLK, 4 * BLK), lambda b, n: (_win_variant(n, NB), 0, 0, 0)),
                  pl.BlockSpec((BLK, A_W), qrow), pl.BlockSpec((BLK, A_W), qrow), pl.BlockSpec((BLK, LANES), qrow)],
        out_specs=(pl.BlockSpec((BLK, A_W), qrow), kvs, kvs,
                   pl.BlockSpec((1, 1, A_HEADS, BLK, 4 * BLK), lambda b, n: (b, _win_variant(n, NB), 0, 0, 0))),
        out_shape=(jax.ShapeDtypeStruct((T, A_W), F32), jax.ShapeDtypeStruct((T, 256), F32),
                   jax.ShapeDtypeStruct((T, 256), F32), jax.ShapeDtypeStruct((Bl, 4, A_HEADS, BLK, 4 * BLK), F32)),
        compiler_params=_params("parallel", "arbitrary"),
    )(proj, proj, proj, bias, ya, dya, lse)


def _table_grad(dbias_list, buckets, after, name):
    nl = len(dbias_list)
    Bl = dbias_list[0].shape[0]

    def body(*refs):
        d_refs = refs[:nl]
        b_ref, _, o_ref, s_ref, acc, part = refs[nl:]
        v, b = pl.program_id(0), pl.program_id(1)

        @pl.when((v == 0) & (b == 0))
        def _():
            o_ref[...] = jnp.zeros_like(o_ref)
            s_ref[...] = jnp.zeros_like(s_ref)

        slot = _lane((BLK, 4 * BLK))
        for l in range(nl):
            for h in range(A_HEADS):
                col = jnp.sum(jnp.where(slot == SINK_SLOT, d_refs[l][0, 0, h], 0.0), axis=0, keepdims=True)
                s_ref[l, h:h + 1, :] += jnp.sum(col, axis=1, keepdims=True)

        tot = d_refs[0][0, 0]
        for r in d_refs[1:]:
            tot = tot + r[0, 0]

        @pl.when(b == 0)
        def _():
            acc[...] = tot

        @pl.when(b > 0)
        def _():
            acc[...] += tot

        @pl.when(b == Bl - 1)
        def _():
            bidx = b_ref[0]

            def step(j, c):
                mask = bidx == j
                for h in range(A_HEADS):
                    part[h, j] = jnp.sum(jnp.where(mask, acc[h], 0.0).reshape(BLK // 8, 8, 4 * BLK), axis=0)
                return c

            lax.fori_loop(0, N_BUCKETS, step, 0)
            rows = lax.broadcasted_iota(jnp.int32, (N_BUCKETS, LANES), 0)
            lanes = _lane((N_BUCKETS, LANES))
            for h in range(A_HEADS):
                col = jnp.sum(jnp.sum(part[h], axis=1), axis=-1, keepdims=True)
                o_ref[h:h + 1, :] += jnp.sum(jnp.where(rows == lanes, col, 0.0), axis=0, keepdims=True)

    return pl.pallas_call(
        body, name=name, grid=(4, Bl),
        in_specs=[pl.BlockSpec((1, 1, A_HEADS, BLK, 4 * BLK), lambda v, b: (b, v, 0, 0, 0))] * nl
        + [pl.BlockSpec((1, BLK, 4 * BLK), lambda v, b: (v, 0, 0)), pl.BlockSpec(memory_space=pl.ANY)],
        out_specs=(pl.BlockSpec((8, LANES), lambda v, b: (0, 0)), pl.BlockSpec((nl, 8, LANES), lambda v, b: (0, 0, 0))),
        out_shape=(jax.ShapeDtypeStruct((8, LANES), F32), jax.ShapeDtypeStruct((nl, 8, LANES), F32)),
        scratch_shapes=[pltpu.VMEM((A_HEADS, BLK, 4 * BLK), F32), pltpu.VMEM((A_HEADS, N_BUCKETS, 8, 4 * BLK), F32)],
        compiler_params=_params("arbitrary", "arbitrary"),
    )(*dbias_list, buckets, after)


def _mla_prep_bwd(dq, dk, dv, proj, gq, gkv, wq, wkk, wkv, tq, tk, tm, name):
    T = proj.shape[0]

    def body(dq_ref, dk_ref, dv_ref, cq_ref, ckv_ref, gq_ref, gkv_ref, wq_ref, wkk_ref, wkv_ref, tq_ref, tk_ref,
             dcq_ref, dckv_ref, dkr_ref, dwq_ref, dwkk_ref, dwkv_ref, dgq_ref, dgkv_ref):
        @pl.when(pl.program_id(0) == 0)
        def _():
            for r in (dwq_ref, dwkk_ref, dwkv_ref, dgq_ref, dgkv_ref):
                r[...] = jnp.zeros_like(r)

        tqv = tq_ref[...]
        dqp = jnp.concatenate([_rope_t(dq_ref[:, LANES * h:LANES * (h + 1)], tqv) for h in range(B_HEADS)],
                              axis=1).astype(BF16)
        cq, gq_ = cq_ref[...], gq_ref[...]
        rq = _rstd(cq)
        dwq_ref[...] += _dot_tn((cq * rq * gq_).astype(BF16), dqp)
        dx, dg = _rms_bwd(_dot_nt(dqp, wq_ref[...]), cq, rq, gq_)
        dcq_ref[...] = dx
        dgq_ref[...] += dg

        dkv_ = dk_ref[...]
        dks = dkv_[:, 0:LANES]
        for h in range(1, B_HEADS):
            dks = dks + dkv_[:, LANES * h:LANES * (h + 1)]
        dkr_ref[...] = _rope_t(pltpu.roll(dks, NOPE, 1), tk_ref[...])
        dkb = dkv_.astype(BF16)
        dvb = dv_ref[...].astype(BF16)
        ckv, gkv_ = ckv_ref[...], gkv_ref[...]
        rk = _rstd(ckv)
        cb = (ckv * rk * gkv_).astype(BF16)
        dwkk_ref[...] += _dot_tn(cb, dkb)
        dwkv_ref[...] += _dot_tn(cb, dvb)
        dx, dg = _rms_bwd(_dot_nt(dkb, wkk_ref[...]) + _dot_nt(dvb, wkv_ref[...]), ckv, rk, gkv_)
        dckv_ref[...] = dx
        dgkv_ref[...] += dg

    tab = lambda n: pl.BlockSpec((n, tm, LANES), lambda i: (0, i, 0))
    wide = pl.BlockSpec((tm, 1024), _row(0))
    return pl.pallas_call(
        body, name=name, grid=(T // tm,),
        in_specs=[wide, wide, wide, pl.BlockSpec((tm, 256), _row(CQ0 // 256)),
                  pl.BlockSpec((tm, 128), _row(CKV0 // 128)), pl.BlockSpec((1, Q_RANK), _const2),
                  pl.BlockSpec((1, KV_RANK), _const2), pl.BlockSpec((Q_RANK, 1024), _const2),
                  pl.BlockSpec((KV_RANK, 1024), _const2), pl.BlockSpec((KV_RANK, 1024), _const2), tab(3), tab(4)],
        out_specs=(pl.BlockSpec((tm, 256), _row(0)), pl.BlockSpec((tm, 128), _row(0)), pl.BlockSpec((tm, 128), _row(0)),
                   pl.BlockSpec((Q_RANK, 1024), _const2), pl.BlockSpec((KV_RANK, 1024), _const2),
                   pl.BlockSpec((KV_RANK, 1024), _const2), pl.BlockSpec((1, Q_RANK), _const2),
                   pl.BlockSpec((1, KV_RANK), _const2)),
        out_shape=(jax.ShapeDtypeStruct((T, 256), F32), jax.ShapeDtypeStruct((T, 128), F32),
                   jax.ShapeDtypeStruct((T, 128), F32), jax.ShapeDtypeStruct((Q_RANK, 1024), F32),
                   jax.ShapeDtypeStruct((KV_RANK, 1024), F32), jax.ShapeDtypeStruct((KV_RANK, 1024), F32),
                   jax.ShapeDtypeStruct((1, Q_RANK), F32), jax.ShapeDtypeStruct((1, KV_RANK), F32)),
        compiler_params=_params("arbitrary"),
    )(dq, dk, dv, proj, proj, gq, gkv, wq, wkk, wkv, tq, tk)


def _inproj_bwd(pieces, h, g, wx, dh_out, tm, name):
    T = h.shape[0]
    nsteps = T // tm
    CH = 512

    def body(*refs):
        p_refs = refs[:len(pieces)]
        h_ref, g_ref, w_ref, dho_ref, dh_ref, dw_hbm, dg_ref, acc, sem = refs[len(pieces):]
        i = pl.program_id(0)

        @pl.when(i == 0)
        def _():
            acc[...] = jnp.zeros_like(acc)
            dg_ref[...] = jnp.zeros_like(dg_ref)

        dp = jnp.concatenate([r[...].astype(BF16) for r in p_refs], axis=1)
        x, gain = h_ref[...], g_ref[...]
        r = _rstd(x)
        u = (x * r * gain).astype(BF16)
        for c in range(0, NX, CH):
            acc[:, c:c + CH] += _dot_tn(u, dp[:, c:c + CH])
        dx, dg = _rms_bwd(_dot_nt(dp, w_ref[...]), x, r, gain)
        dh_ref[...] = dho_ref[...] + dx
        dg_ref[...] += dg

        @pl.when(i == nsteps - 1)
        def _():
            cp = pltpu.make_async_copy(acc, dw_hbm, sem)
            cp.start()
            cp.wait()

    return pl.pallas_call(
        body, name=name, grid=(nsteps,),
        in_specs=[pl.BlockSpec((tm, p.shape[1]), _row(0)) for p in pieces]
        + [pl.BlockSpec((tm, D), _row(0)), pl.BlockSpec((1, D), _const2), pl.BlockSpec((D, NX), _const2),
           pl.BlockSpec((tm, D), _row(0))],
        out_specs=(pl.BlockSpec((tm, D), _row(0)), pl.BlockSpec(memory_space=pl.ANY), pl.BlockSpec((1, D), _const2)),
        out_shape=(jax.ShapeDtypeStruct((T, D), F32), jax.ShapeDtypeStruct((D, NX), F32), jax.ShapeDtypeStruct((1, D), F32)),
        scratch_shapes=[pltpu.VMEM((D, NX), F32), pltpu.SemaphoreType.DMA(())],
        compiler_params=_params("arbitrary"),
    )(*pieces, h, g, wx, dh_out)


def kernel(x, meta_tokens, rel_bias_table, norm_in, w_in, sink_a, norm_q_lat, w_uq, norm_kv_lat, w_ukv, norm_out_a, norm_out_b, w_out, norm_final, loss_target, m_meta_tokens, m_rel_bias_table, m_norm_in, m_w_in, m_sink_a, m_norm_q_lat, m_w_uq, m_norm_kv_lat, m_w_ukv, m_norm_out_a, m_norm_out_b, m_w_out, m_norm_final, v_meta_tokens, v_rel_bias_table, v_norm_in, v_w_in, v_sink_a, v_norm_q_lat, v_w_uq, v_norm_kv_lat, v_w_ukv, v_norm_out_a, v_norm_out_b, v_w_out, v_norm_final):
    Bl, S, _ = x.shape
    assert S % 256 == 0 and S >= 3 * BLK, "the attention kernels tile the real tokens in 256-row blocks"
    Lp = S + BLK
    T = Bl * Lp
    tm = Lp // 4
    tq = Lp // 8
    depth = w_in.shape[0]
    me = 4 * lax.axis_index("x") + 2 * lax.axis_index("y") + lax.axis_index("c")

    def finish(handles, after, name):
        srcs, lands = _xchg_wait(handles, after, name)
        full = []
        for hd, src, land in zip(handles, srcs, lands):
            own = src if hd[4] == "gather" else lax.dynamic_index_in_dim(src, me, 0, keepdims=False)
            full.append(_own_slot(land, own, me))
        return full

    def weights_of(i):
        return [w_in[i].astype(BF16), w_uq[i].astype(BF16), w_ukv[i].astype(BF16), w_out[i].astype(BF16)]

    def packed(l_in, l_uq, l_ukv, l_out):
        cols = lambda t: jnp.transpose(t, (1, 0, 2)).reshape(t.shape[1], NDEV * t.shape[2])
        wkk, wkv = _pack_w_ukv(cols(l_ukv))
        return _pack_w_in(l_in), _pack_w_uq(cols(l_uq)), wkk, wkv, l_out.reshape(D, D)

    wb = weights_of(0)
    gat_a, tok = _xchg_start([wb[0], meta_tokens], ["gather", "gather"], "gather_start_0a")
    buckets = _bias_buckets(S)
    biases = _build_bias(buckets, _after(rel_bias_table, tok), sink_a, "build_bias")
    tb = 256 if T % 256 == 0 else tq
    tq_tab = jnp.tile(_rope_tables(S, Lp, NOPE, True), (1, Bl, 1))
    key_pad = jnp.where((jnp.arange(Lp) >= S + NMETA)[:, None] & (jnp.arange(LANES) == MASK_LANE)[None, :], NEG, 0.0)
    tk_tab = jnp.tile(jnp.concatenate([_rope_tables(S, Lp, 0, False), key_pad.astype(F32)[None]], axis=0), (1, Bl, 1))
    l_in, l_meta = finish(gat_a, biases[0], "gather_wait_0a")
    gat_b, tok = _xchg_start(wb[1:], ["gather"] * 3, "gather_start_0b", after=l_in)

    meta_f = jnp.transpose(l_meta, (1, 0, 2)).reshape(NMETA, D)
    tail = jnp.concatenate([meta_f, jnp.zeros((BLK - NMETA, D), F32)], axis=0)
    h = jnp.concatenate([x, jnp.broadcast_to(tail[None], (Bl, BLK, D))], axis=1).reshape(T, D)

    saved = []
    w_out_f = [None] * depth
    gat_next = None
    for i in range(depth):
        g_in, g_q, g_kv = norm_in[i][None], norm_q_lat[i][None], norm_kv_lat[i][None]
        g_a, g_b = norm_out_a[i][None], norm_out_b[i][None]
        if i == 0:
            proj = _inproj_fwd(h, _after(g_in, tok), _pack_w_in(l_in), tm,
                               f"inproj_fwd_{i}")
            l_uq, l_ukv, l_out = finish(gat_b, proj, "gather_wait_0b")
            wx, wq, wkk, wkv, w_out_f[i] = packed(l_in, l_uq, l_ukv, l_out)
        else:
            wx, wq, wkk, wkv, w_out_f[i] = packed(*finish(gat_next, h, f"gather_wait_{i}"))
            proj = _inproj_fwd(h, g_in, wx, tm, f"inproj_fwd_{i}")
        if i + 1 < depth:
            gat_next, tok = _xchg_start(weights_of(i + 1), ["gather"] * 4, f"gather_start_{i + 1}", after=w_out_f[i])
            g_q = _after(g_q, tok)
        q, k, v = _mla_prep_fwd(proj, g_q, g_kv, wq, wkk, wkv, tq_tab, tk_tab, tm, f"mla_prep_fwd_{i}")
        ya, lse_a = _win_fwd(proj, biases[i], Bl, S, f"win_fwd_{i}")
        yb, lse_b = _mla_fwd(q, k, v, Bl, Lp, f"mla_fwd_{i}")
        h_new = _out_fwd(ya, yb, proj, g_a, g_b, w_out_f[i], h, tm, f"out_fwd_{i}")
        saved.append((h, proj, q, k, v, ya, lse_a, yb, lse_b, wx, wq, wkk, wkv))
        h = h_new

    dh, loss_acc, dg_final = _loss_head(h, norm_final[None], loss_target, Bl, S, "loss_head")

    g_n_in, g_nq, g_nkv, g_na, g_nb = ([None] * depth for _ in range(5))
    dbias_all, sc_out, sc_rest = [None] * depth, [None] * depth, [None] * depth
    split = lambda t, n: jnp.transpose(t.reshape(t.shape[0], NDEV, n), (1, 0, 2)).astype(BF16)
    tok = None
    for i in reversed(range(depth)):
        h_in, proj, q, k, v, ya, lse_a, yb, lse_b, wx, wq, wkk, wkv = saved[i]
        g_in, g_q, g_kv = norm_in[i][None], norm_q_lat[i][None], norm_kv_lat[i][None]
        g_a, g_b = norm_out_a[i][None], norm_out_b[i][None]
        if tok is not None:
            g_a = _after(g_a, tok)
        dya, dyb, dga, dgb, g_w_out, g_na[i], g_nb[i] = _out_bwd(dh, ya, yb, proj, g_a, g_b, w_out_f[i], tb, f"out_bwd_{i}")
        sc_out[i], tok = _xchg_start([g_w_out.reshape(NDEV, D // NDEV, D).astype(BF16)], ["scatter"], f"scatter_start_{i}a")
        dq, dk, dv = _mla_bwd(q, k, v, yb, dyb, lse_b, tok, Bl, Lp, f"mla_bwd_{i}")
        dqa, dka, dva, dbias = _win_bwd(proj, biases[i], ya, dya, lse_a, Bl, S, f"win_bwd_{i}")
        dcq, dckv, dkr, dwq, dwkk, dwkv, g_nq[i], g_nkv[i] = _mla_prep_bwd(
            dq, dk, dv, proj, g_q, g_kv, wq, wkk, wkv, tq_tab, tk_tab, tm, f"mla_prep_bwd_{i}")
        dh, dwx, g_n_in[i] = _inproj_bwd((dqa, dka, dva, dga, dcq, dckv, dkr, dgb), h_in, g_in, wx, dh, tb,
                                         f"inproj_bwd_{i}")
        dbias_all[i] = dbias
        rest = [_unpack_w_in_grad(dwx).astype(BF16), split(_unpack_w_uq_grad(dwq), 768 // NDEV),
                split(_unpack_w_ukv_grad(dwkk, dwkv), 1024 // NDEV)]
        if i == 0:
            dh3 = dh.reshape(Bl, Lp, D)
            grad_x = dh3[:, :S]
            rest.append(jnp.transpose(jnp.sum(dh3[:, S:S + NMETA], axis=0).reshape(NMETA, NDEV, D // NDEV), (1, 0, 2)))
        sc_rest[i], tok = _xchg_start(rest, ["scatter"] * len(rest), f"scatter_start_{i}b")

    dtab, dsink = _table_grad(dbias_all, buckets, tok, "table_grad")
    two_d = lambda a: a.reshape(1, -1) if a.ndim == 1 else a
    small_w = [rel_bias_table, norm_in, sink_a, norm_q_lat, norm_kv_lat, norm_out_a, norm_out_b, norm_final]
    small_m = [m_rel_bias_table, m_norm_in, m_sink_a, m_norm_q_lat, m_norm_kv_lat, m_norm_out_a, m_norm_out_b, m_norm_final]
    small_v = [v_rel_bias_table, v_norm_in, v_sink_a, v_norm_q_lat, v_norm_kv_lat, v_norm_out_a, v_norm_out_b, v_norm_final]
    small_g = [jnp.transpose(dtab[:, :N_BUCKETS]), jnp.concatenate(g_n_in), dsink[:, :, 0], jnp.concatenate(g_nq),
               jnp.concatenate(g_nkv), jnp.concatenate(g_na), jnp.concatenate(g_nb), dg_final]
    sc_small, tok = _xchg_start(small_g + [loss_acc], ["gather"] * (len(small_g) + 1), "scatter_start_small")

    r_out, r_in, r_uq, r_ukv = [None] * depth, [None] * depth, [None] * depth, [None] * depth
    for i in reversed(range(depth)):
        (r_out[i],) = finish(sc_out[i], tok, f"scatter_wait_{i}a")
        got = finish(sc_rest[i], tok, f"scatter_wait_{i}b")
        r_in[i], r_uq[i], r_ukv[i] = got[:3]
        if i == 0:
            r_meta = got[3]
    r_small = finish(sc_small, tok, "scatter_wait_small")

    res = {}
    sharded = dict(w_in=(r_in, w_in, m_w_in, v_w_in), w_uq=(r_uq, w_uq, m_w_uq, v_w_uq), w_ukv=(r_ukv, w_ukv, m_w_ukv, v_w_ukv),
                   w_out=(r_out, w_out, m_w_out, v_w_out),
                   meta_tokens=([r_meta], meta_tokens[None], m_meta_tokens[None], v_meta_tokens[None]))
    for n, (r, w, m, v) in sharded.items():
        outs = _adamw_param(r, w, m, v, f"adamw_{n}")
        res[n] = [o[0] for o in outs] if n == "meta_tokens" else list(outs)
    names_small = ["rel_bias_table", "norm_in", "sink_a", "norm_q_lat", "norm_kv_lat", "norm_out_a", "norm_out_b", "norm_final"]
    outs = _adamw_small(r_small[:-1], [two_d(a) for a in small_w], [two_d(a) for a in small_m], [two_d(a) for a in small_v],
                        r_small[-1], "adamw_replicated")
    loss = outs[-1][0, 0]
    ns = len(names_small)
    for j, n in enumerate(names_small):
        res[n] = [outs[kk * ns + j].reshape(small_w[j].shape) for kk in range(4)]
    order = ["meta_tokens", "rel_bias_table", "norm_in", "w_in", "sink_a", "norm_q_lat", "w_uq", "norm_kv_lat", "w_ukv",
             "norm_out_a", "norm_out_b", "w_out", "norm_final"]
    return (loss, grad_x, *[res[n][kk] for kk in range(4) for n in order])
```

```python
import functools
import math

import numpy as np
import jax
import jax.numpy as jnp
from jax import lax
from jax.experimental import pallas as pl
from jax.experimental.pallas import tpu as pltpu

F32, BF16 = jnp.float32, jnp.bfloat16
D = 1024
NMETA = 16
BLK = 128
A_HEADS, A_KV, A_DH, A_W = 8, 2, 64, 512
B_HEADS, NOPE, ROPE, B_V, B_W = 8, 64, 32, 64, 512
Q_RANK, KV_RANK = 256, 128
IN_W = 2208
N_BUCKETS, MAX_DIST = 32, 128
THETA = 10000.0
EPS = 1e-6
NEG = -1e30
A_SCALE = A_DH ** -0.5
B_SCALE = (NOPE + ROPE) ** -0.5
LOG2E = math.log2(math.e)
MASK_LANE = NOPE + ROPE
SUM_LANE = B_V
LANES = 128
NDEV = 8
MESH = pl.DeviceIdType.MESH

NX = 2560
QA0, KA0, VA0, GA0, CQ0, CKV0, KR0, GB0 = 0, 512, 768, 1024, 1536, 1792, 1920, 2048

ADAM_LR, ADAM_B1, ADAM_B2, ADAM_EPS, ADAM_WD, ADAM_STEP = 0.001, 0.9, 0.999, 1e-08, 0.01, 10


def _dot(a, b):
    return jnp.dot(a, b, preferred_element_type=F32)


def _dot_nt(a, b):
    return lax.dot_general(a, b, (((1,), (1,)), ((), ())), preferred_element_type=F32)


def _dot_tn(a, b):
    return lax.dot_general(a, b, (((0,), (0,)), ((), ())), preferred_element_type=F32)


def _lane(shape):
    return lax.broadcasted_iota(jnp.int32, shape, len(shape) - 1)


def _rstd(x):
    return lax.rsqrt(jnp.mean(x * x, axis=-1, keepdims=True) + EPS)


def _rms_bwd(dn, x, r, g):
    z = dn * g
    dx = r * z - x * (r * r * r) * jnp.mean(z * x, axis=-1, keepdims=True)
    return dx, jnp.sum(dn * (x * r), axis=0, keepdims=True)


def _row(i):
    return lambda *ids: (ids[0], i)


def _const2(*ids):
    return (0, 0)


def _params(*sem):
    return pltpu.CompilerParams(dimension_semantics=sem)


SHARD_W = IN_W // NDEV


def _pack_w_in(shards):
    def nat(a, b):
        parts = []
        while a < b:
            p, e = a // SHARD_W, min(b, (a // SHARD_W + 1) * SHARD_W)
            parts.append(shards[p][:, a - p * SHARD_W:e - p * SHARD_W])
            a = e
        return parts
    dup = lambda a: nat(a, a + 64) * 2 + nat(a + 64, a + 128) * 2
    zeros = [jnp.zeros((shards.shape[1], LANES - ROPE), shards.dtype)]
    return jnp.concatenate(nat(0, 512) + dup(512) + dup(640) + nat(768, 1280) + nat(1280, 1536) + nat(1536, 1664)
                           + nat(1664, 1696) + zeros + nat(1696, 2208), axis=1)


_NAT_GROUPS = ((0, 512, QA0), (512, 576, KA0), (576, 640, KA0 + 128), (640, 704, VA0), (704, 768, VA0 + 128),
               (768, 1280, GA0), (1280, 1536, CQ0), (1536, 1664, CKV0), (1664, 1696, KR0), (1696, 2208, GB0))


def _unpack_w_in_grad(g):
    def nat(a, b):
        lo, _, pk = next(grp for grp in _NAT_GROUPS if grp[0] <= a < grp[1])
        t = g[:, pk + a - lo:pk + b - lo]
        return t + g[:, pk + 64 + a - lo:pk + 64 + b - lo] if 512 <= a < 768 else t
    cuts = sorted({c for grp in _NAT_GROUPS for c in grp[:2]} | {p * SHARD_W for p in range(NDEV + 1)})
    shards = [[] for _ in range(NDEV)]
    for a, b in zip(cuts[:-1], cuts[1:]):
        shards[a // SHARD_W].append(nat(a, b))
    return jnp.stack([jnp.concatenate(parts, axis=1) for parts in shards])


def _pack_w_uq(w):
    t = w.reshape(Q_RANK, B_HEADS, NOPE + ROPE)
    t = jnp.concatenate([t, jnp.zeros((Q_RANK, B_HEADS, LANES - NOPE - ROPE), w.dtype)], axis=-1)
    return t.reshape(Q_RANK, B_HEADS * LANES)


def _unpack_w_uq_grad(g):
    return g.reshape(Q_RANK, B_HEADS, LANES)[:, :, :NOPE + ROPE].reshape(Q_RANK, B_HEADS * (NOPE + ROPE))


def _pack_w_ukv(w):
    t = w.reshape(KV_RANK, B_HEADS, NOPE + B_V)
    z = jnp.zeros((KV_RANK, B_HEADS, LANES - NOPE), w.dtype)
    pad = lambda u: jnp.concatenate([u, z], axis=-1).reshape(KV_RANK, B_HEADS * LANES)
    return pad(t[:, :, :NOPE]), pad(t[:, :, NOPE:])


def _unpack_w_ukv_grad(gk, gv):
    head = lambda g: g.reshape(KV_RANK, B_HEADS, LANES)[:, :, :NOPE]
    return jnp.concatenate([head(gk), head(gv)], axis=-1).reshape(KV_RANK, B_HEADS * (NOPE + B_V))


def _t5_bucket(rel):
    nb = N_BUCKETS // 2
    max_exact = nb // 2
    ret = jnp.where(rel > 0, nb, 0)
    n = jnp.abs(rel)
    nf = jnp.maximum(n, 1).astype(F32)
    large = max_exact + (jnp.log(nf / max_exact) / math.log(MAX_DIST / max_exact) * (nb - max_exact)).astype(jnp.int32)
    large = jnp.minimum(large, nb - 1)
    return ret + jnp.where(n < max_exact, n, large)


def _bias_buckets(S):
    q = np.arange(BLK)[:, None]
    k = np.arange(4 * BLK)[None, :]
    is_meta_key = (k >= 3 * BLK) & (k < 3 * BLK + NMETA)
    mi = k - 3 * BLK
    rels, valids = [], []
    for shift in (0, BLK, 2 * BLK):
        rel_real = k - shift - q
        valid_real = (k < 3 * BLK) & (np.abs(rel_real) <= BLK)
        far = -(NMETA + MAX_DIST + BLK)
        rel_meta = (mi - (NMETA + q)) if shift == 0 else np.full_like(k + q, far)
        rels.append(np.where(is_meta_key, rel_meta, rel_real))
        valids.append(valid_real | is_meta_key)
    qm = q < NMETA
    rel_real = NMETA + k - q
    valid_real = (k < BLK) & (np.abs(rel_real) <= BLK) & qm
    rels.append(np.where(is_meta_key, mi - q, rel_real))
    valids.append(valid_real | is_meta_key)
    rel = jnp.asarray(np.stack(rels).astype(np.int32))
    valid = jnp.asarray(np.stack(valids))
    return jnp.where(valid, _t5_bucket(rel), -1).astype(jnp.int32)


def _rope_tables(S, Lp, off, passthrough):
    half = ROPE // 2
    r = np.arange(Lp)
    pos = np.where(r < S, NMETA + r, np.where(r < S + NMETA, r - S, 0)).astype(np.float32)
    freqs = THETA ** (-jnp.arange(half, dtype=F32) / half)
    ang = jnp.asarray(pos)[:, None] * freqs[None, :]
    cos, sin = jnp.cos(ang), jnp.sin(ang)
    z = lambda n: jnp.zeros((Lp, n), F32)
    head = jnp.ones((Lp, off), F32) if passthrough else z(off)
    c = jnp.concatenate([head, cos, cos, z(LANES - off - ROPE)], axis=1)
    s1 = jnp.concatenate([z(off), -sin, z(LANES - off - half)], axis=1)
    s2 = jnp.concatenate([z(off + half), sin, z(LANES - off - ROPE)], axis=1)
    return jnp.stack([c, s1, s2])


def _rope(x, t):
    return x * t[0] + pltpu.roll(x, LANES - 16, 1) * t[1] + pltpu.roll(x, 16, 1) * t[2]


def _rope_t(dy, t):
    return dy * t[0] + pltpu.roll(dy * t[1], 16, 1) + pltpu.roll(dy * t[2], LANES - 16, 1)


def _me_and_peers():
    x, y, c = lax.axis_index("x"), lax.axis_index("y"), lax.axis_index("c")
    flip = lambda v, b: 1 - v if b else v
    peers = [(flip(x, k & 4), flip(y, k & 2), flip(c, k & 1)) for k in range(1, NDEV)]
    return 4 * x + 2 * y + c, peers


def _xchg_start(srcs, kinds, name, after=None):
    n = len(srcs)
    lands = [lax.empty((NDEV,) + s.shape[-2:], s.dtype) for s in srcs]
    extra = [] if after is None else [after]

    def body(*refs):
        src_refs, land_refs = refs[:n], refs[n:2 * n]
        ssems, rsems = refs[2 * n + len(extra):3 * n + len(extra)], refs[3 * n + len(extra):4 * n + len(extra)]
        token = refs[6 * n + len(extra)]
        me, peers = _me_and_peers()
        for it in range(n):
            for k, (px, py, pc) in enumerate(peers):
                src = src_refs[it] if kinds[it] == "gather" else src_refs[it].at[4 * px + 2 * py + pc]
                pltpu.make_async_remote_copy(src_ref=src, dst_ref=land_refs[it].at[me], send_sem=ssems[it].at[k],
                                             recv_sem=rsems[it].at[k], device_id=(px, py, pc), device_id_type=MESH).start()
        token[...] = jnp.zeros_like(token)

    hbm = pl.BlockSpec(memory_space=pltpu.HBM)
    sem = pl.BlockSpec(memory_space=pltpu.SEMAPHORE)
    outs = pl.pallas_call(
        body, name=name,
        out_shape=tuple([pltpu.SemaphoreType.DMA((NDEV - 1,))] * (2 * n) + [pltpu.HBM(a.shape, a.dtype) for a in srcs + lands]
                        + [jax.ShapeDtypeStruct((8, LANES), F32)]),
        in_specs=[hbm] * (2 * n) + [pl.BlockSpec(memory_space=pl.ANY)] * len(extra),
        out_specs=tuple([sem] * (2 * n) + [hbm] * (2 * n) + [pl.BlockSpec(memory_space=pltpu.VMEM)]),
        input_output_aliases={i: 2 * n + i for i in range(2 * n)},
        compiler_params=pltpu.CompilerParams(has_side_effects=pltpu.SideEffectType.DATAFLOW_SIDE_EFFECTING),
    )(*[pltpu.with_memory_space_constraint(a, pltpu.HBM) for a in srcs + lands], *extra)
    handles = [(outs[it], outs[n + it], outs[2 * n + it], outs[3 * n + it], kinds[it]) for it in range(n)]
    return handles, outs[4 * n]


def _xchg_wait(handles, after, name):
    n = len(handles)

    def body(*refs):
        src_refs, land_refs = refs[:n], refs[n:2 * n]
        ssems, rsems = refs[2 * n:3 * n], refs[3 * n:4 * n]
        me, peers = _me_and_peers()
        for it in range(n):
            for k, (px, py, pc) in enumerate(peers):
                src = src_refs[it] if handles[it][4] == "gather" else src_refs[it].at[4 * px + 2 * py + pc]
                cp = pltpu.make_async_remote_copy(src_ref=src, dst_ref=land_refs[it].at[me], send_sem=ssems[it].at[k],
                                                  recv_sem=rsems[it].at[k], device_id=(px, py, pc), device_id_type=MESH)
                cp.wait_send()
                cp.wait_recv()

    hbm = pl.BlockSpec(memory_space=pltpu.HBM)
    sem = pl.BlockSpec(memory_space=pltpu.SEMAPHORE)
    srcs, lands = [h[2] for h in handles], [h[3] for h in handles]
    outs = pl.pallas_call(
        body, name=name, out_shape=tuple(pltpu.HBM(a.shape, a.dtype) for a in srcs + lands),
        in_specs=[hbm] * (2 * n) + [sem] * (2 * n) + [pl.BlockSpec(memory_space=pl.ANY)], out_specs=tuple([hbm] * (2 * n)),
        input_output_aliases={i: i for i in range(2 * n)},
        compiler_params=pltpu.CompilerParams(has_side_effects=pltpu.SideEffectType.DATAFLOW_SIDE_EFFECTING),
    )(*srcs, *lands, *[h[0] for h in handles], *[h[1] for h in handles], after)
    return list(outs[:n]), list(outs[n:])


def _own_slot(land, own, me):
    return lax.dynamic_update_slice(land, own[None].astype(land.dtype), (me, 0, 0))


def _after(x, token):
    return x + token[0, 0]


def _adamw_update(g, w, m, v):
    mn = ADAM_B1 * m + (1.0 - ADAM_B1) * g
    vn = ADAM_B2 * v + (1.0 - ADAM_B2) * (g * g)
    m_hat = mn / (1.0 - ADAM_B1 ** ADAM_STEP)
    v_hat = vn / (1.0 - ADAM_B2 ** ADAM_STEP)
    return -ADAM_LR * (m_hat / (jnp.sqrt(v_hat) + ADAM_EPS) + ADAM_WD * w), mn, vn


def _sum_partials(r_ref):
    g = r_ref[0].astype(F32)
    for p in range(1, NDEV):
        g = g + r_ref[p].astype(F32)
    return g


def _adamw_param(recvs, w, m, v, name):
    depth, r, c = w.shape
    tr = min(r, 256)
    nl = len(recvs)

    def body(*refs):
        r_refs = refs[:nl]
        w_ref, m_ref, v_ref, g_out, d_out, m_out, v_out = refs[nl:]
        for li in range(nl):
            @pl.when(pl.program_id(0) == li)
            def _(li=li):
                g = _sum_partials(r_refs[li])
                g_out[0] = g
                d_out[0], m_out[0], v_out[0] = _adamw_update(g, w_ref[0], m_ref[0], v_ref[0])

    blk = pl.BlockSpec((1, tr, c), lambda l, i: (l, i, 0))
    return pl.pallas_call(
        body, name=name, grid=(depth, r // tr),
        in_specs=[pl.BlockSpec((NDEV, tr, c), lambda l, i: (0, i, 0))] * nl + [blk, blk, blk],
        out_specs=(blk, blk, blk, blk), out_shape=(jax.ShapeDtypeStruct(w.shape, F32),) * 4,
        compiler_params=_params("arbitrary", "arbitrary"),
    )(*recvs, w, m, v)


def _adamw_small(recvs, ws, ms, vs, loss_parts, name):
    n = len(ws)

    def body(*refs):
        r_refs, w_refs, m_refs, v_refs = refs[:n], refs[n:2 * n], refs[2 * n:3 * n], refs[3 * n:4 * n]
        outs = refs[4 * n + 1:]
        for j in range(n):
            g = _sum_partials(r_refs[j])
            outs[j][...] = g
            outs[n + j][...], outs[2 * n + j][...], outs[3 * n + j][...] = _adamw_update(
                g, w_refs[j][...], m_refs[j][...], v_refs[j][...])
        outs[4 * n][...] = _sum_partials(refs[4 * n])

    vm = pl.BlockSpec(memory_space=pltpu.VMEM)
    return pl.pallas_call(
        body, name=name, in_specs=[vm] * (4 * n + 1), out_specs=tuple([vm] * (4 * n + 1)),
        out_shape=tuple(jax.ShapeDtypeStruct(a.shape, F32) for a in ws) * 4 + (jax.ShapeDtypeStruct(loss_parts.shape[1:], F32),),
    )(*recvs, *ws, *ms, *vs, loss_parts)


def _inproj_fwd(h, g, wx, tm, name):
    T = h.shape[0]

    def body(h_ref, g_ref, w_ref, o_ref):
        x = h_ref[...]
        u = (x * _rstd(x) * g_ref[...]).astype(BF16)
        o_ref[...] = _dot(u, w_ref[...])

    return pl.pallas_call(
        body, name=name, grid=(T // tm,),
        in_specs=[pl.BlockSpec((tm, D), _row(0)), pl.BlockSpec((1, D), _const2), pl.BlockSpec((D, NX), _const2)],
        out_specs=pl.BlockSpec((tm, NX), _row(0)), out_shape=jax.ShapeDtypeStruct((T, NX), F32),
        compiler_params=_params("parallel"),
    )(h, g, wx)


def _mla_prep_fwd(proj, gq, gkv, wq, wkk, wkv, tq, tk, tm, name):
    T = proj.shape[0]

    def body(cq_ref, ckv_ref, kr_ref, gq_ref, gkv_ref, wq_ref, wkk_ref, wkv_ref, tq_ref, tk_ref, q_ref, k_ref, v_ref):
        cq = cq_ref[...]
        q = _dot((cq * _rstd(cq) * gq_ref[...]).astype(BF16), wq_ref[...])
        tqv = tq_ref[...]
        lane = _lane((1, LANES))
        one = jnp.where(lane == MASK_LANE, 1.0, 0.0)
        for h in range(B_HEADS):
            cs = slice(LANES * h, LANES * (h + 1))
            q_ref[:, cs] = (_rope(q[:, cs], tqv) * (B_SCALE * LOG2E) + one).astype(BF16)
        tkv = tk_ref[...]
        ksh = pltpu.roll(_rope(kr_ref[...], tkv), NOPE, 1) + tkv[3]
        ckv = ckv_ref[...]
        cb = (ckv * _rstd(ckv) * gkv_ref[...]).astype(BF16)
        kn = _dot(cb, wkk_ref[...])
        vn = _dot(cb, wkv_ref[...])
        minus = jnp.where((lane == SUM_LANE) | (lane == SUM_LANE + 1), -1.0, 0.0)
        for h in range(B_HEADS):
            cs = slice(LANES * h, LANES * (h + 1))
            k_ref[:, cs] = (kn[:, cs] + ksh).astype(BF16)
            v_ref[:, cs] = (vn[:, cs] + minus).astype(BF16)

    tab = lambda n: pl.BlockSpec((n, tm, LANES), lambda i: (0, i, 0))
    wide = pl.BlockSpec((tm, 1024), _row(0))
    return pl.pallas_call(
        body, name=name, grid=(T // tm,),
        in_specs=[pl.BlockSpec((tm, 256), _row(CQ0 // 256)), pl.BlockSpec((tm, 128), _row(CKV0 // 128)),
                  pl.BlockSpec((tm, 128), _row(KR0 // 128)), pl.BlockSpec((1, Q_RANK), _const2),
                  pl.BlockSpec((1, KV_RANK), _const2), pl.BlockSpec((Q_RANK, 1024), _const2),
                  pl.BlockSpec((KV_RANK, 1024), _const2), pl.BlockSpec((KV_RANK, 1024), _const2), tab(3), tab(4)],
        out_specs=(wide, wide, wide), out_shape=(jax.ShapeDtypeStruct((T, 1024), BF16),) * 3,
        compiler_params=_params("parallel"),
    )(proj, proj, proj, gq, gkv, wq, wkk, wkv, tq, tk)


SINK_SLOT = 4 * BLK - 1


def _build_bias(buckets, table, sinks, name):
    depth = sinks.shape[0]

    def body(tab_ref, sink_ref, b_ref, *o_refs):
        bidx = b_ref[0]
        slot = _lane(bidx.shape)
        for h in range(A_HEADS):
            acc = jnp.full(bidx.shape, NEG, F32)
            for j in range(N_BUCKETS):
                acc = jnp.where(bidx == j, tab_ref[j, h] * LOG2E, acc)
            for l in range(depth):
                o_refs[l][0, h] = jnp.where(slot == SINK_SLOT, sink_ref[l, h] * LOG2E, acc)

    smem = pl.BlockSpec(memory_space=pltpu.SMEM)
    blk = pl.BlockSpec((1, A_HEADS, BLK, 4 * BLK), lambda i: (i, 0, 0, 0))
    return pl.pallas_call(
        body, name=name, grid=(4,),
        in_specs=[smem, smem, pl.BlockSpec((1, BLK, 4 * BLK), lambda i: (i, 0, 0))],
        out_specs=tuple([blk] * depth), out_shape=tuple([jax.ShapeDtypeStruct((4, A_HEADS, BLK, 4 * BLK), F32)] * depth),
        compiler_params=_params("parallel"),
    )(table, sinks, buckets)


def _win_base(n, NB):
    return jnp.where(n == NB, 0, jnp.clip(n - 1, 0, NB - 3))


def _win_variant(n, NB):
    return jnp.where(n == 0, 0, jnp.where(n < NB - 1, 1, jnp.where(n == NB - 1, 2, 3)))


def _win_kv(k_ref, v_ref, base, S, kvh):
    cs = slice(LANES * kvh, LANES * (kvh + 1))
    k2 = jnp.concatenate([k_ref[pl.ds(base, 3 * BLK), cs], k_ref[S:S + BLK, cs]], axis=0).astype(BF16)
    v2 = jnp.concatenate([v_ref[pl.ds(base, 3 * BLK), cs], v_ref[S:S + BLK, cs]], axis=0)
    lane = _lane(v2.shape)
    v2 = jnp.where(lane < A_DH, v2, jnp.where(lane < A_DH + 2, -1.0, 0.0)).astype(BF16)
    return k2, v2


def _win_fwd(proj, bias, Bl, S, name):
    T = proj.shape[0]
    Lp = S + BLK
    NB = S // BLK
    nblk = Lp // BLK
    G = A_HEADS // A_KV

    def body(q_ref, k_ref, v_ref, b_ref, o_ref, lse_ref):
        n = pl.program_id(1)
        base = pl.multiple_of(_win_base(n, NB) * BLK, BLK)
        lane = _lane((BLK, LANES))
        lo = lane < A_DH
        kv = [_win_kv(k_ref, v_ref, base, S, kvh) for kvh in range(A_KV)]
        scores = []
        for h in range(A_HEADS):
            qp = q_ref[:, LANES * (h // 2):LANES * (h // 2 + 1)]
            qm = (jnp.where(lo if h % 2 == 0 else ~lo, qp, 0.0) * (A_SCALE * LOG2E)).astype(BF16)
            scores.append(_dot_nt(qm, kv[h // G][0]) + b_ref[0, h])
        ms = [jnp.max(s, axis=-1, keepdims=True) for s in scores]
        es = [jnp.exp2(s - m).astype(BF16) for s, m in zip(scores, ms)]
        outs = []
        t = jnp.zeros((BLK, LANES), F32)
        for h in range(A_HEADS):
            o2 = _dot(es[h], kv[h // G][1])
            l = -jnp.sum(jnp.where(lane == A_DH, o2, 0.0), axis=-1, keepdims=True)
            outs.append(o2 * (1.0 / l))
            t = jnp.where(lane == h, ms[h] + jnp.log(l) * LOG2E, t)
        for j in range(A_HEADS // 2):
            o_ref[:, LANES * j:LANES * (j + 1)] = jnp.where(lo, outs[2 * j], pltpu.roll(outs[2 * j + 1], A_DH, 1))
        lse_ref[...] = t

    qrow = lambda b, n: (b * nblk + n, 0)
    return pl.pallas_call(
        body, name=name, grid=(Bl, nblk),
        in_specs=[pl.BlockSpec((BLK, A_W), qrow),
                  pl.BlockSpec((Lp, 256), lambda b, n: (b, KA0 // 256)), pl.BlockSpec((Lp, 256), lambda b, n: (b, VA0 // 256)),
                  pl.BlockSpec((1, A_HEADS, BLK, 4 * BLK), lambda b, n: (_win_variant(n, NB), 0, 0, 0))],
        out_specs=(pl.BlockSpec((BLK, A_W), qrow), pl.BlockSpec((BLK, LANES), qrow)),
        out_shape=(jax.ShapeDtypeStruct((T, A_W), F32), jax.ShapeDtypeStruct((T, LANES), F32)),
        compiler_params=_params("parallel", "arbitrary"),
    )(proj, proj, proj, bias)


def _q_tiles(Lp):
    S = Lp - BLK
    return S // 256, S


def _mla_fwd(q, k, v, Bl, Lp, name):
    T = q.shape[0]
    n_big, S = _q_tiles(Lp)

    def body(q_ref, k_ref, v_ref, o_ref, lse_ref):
        def qtile(r0, nq):
            lane = _lane((nq, LANES))
            outs = []
            t = jnp.zeros((nq, LANES), F32)
            cols = [slice(LANES * hh, LANES * (hh + 1)) for hh in range(2)]
            scores = [_dot_nt(q_ref[pl.ds(r0, nq), cs], k_ref[:, cs]) for cs in cols]
            for hh in range(2):
                cs, s = cols[hh], scores[hh]
                m = jnp.max(s, axis=-1, keepdims=True)
                o2 = _dot(jnp.exp2(s - m).astype(BF16), v_ref[:, cs])
                l = -jnp.sum(jnp.where(lane == SUM_LANE, o2, 0.0), axis=-1, keepdims=True)
                outs.append(o2 * (1.0 / l))
                t = jnp.where(lane == hh, m + jnp.log(l) * LOG2E, t)
            o_ref[pl.ds(r0, nq), :] = jnp.where(lane < B_V, outs[0], pltpu.roll(outs[1], B_V, 1))
            lse_ref[pl.ds(r0, nq), :] = t

        def step(i, c):
            qtile(pl.multiple_of(i * 256, 256), 256)
            return c

        lax.fori_loop(0, n_big, step, 0)
        qtile(S, BLK)

    pair = lambda b, j: (b, j)
    wide, narrow = pl.BlockSpec((Lp, 256), pair), pl.BlockSpec((Lp, LANES), pair)
    return pl.pallas_call(
        body, name=name, grid=(Bl, B_HEADS // 2), in_specs=[wide, wide, wide], out_specs=(narrow, narrow),
        out_shape=(jax.ShapeDtypeStruct((T, B_W), F32), jax.ShapeDtypeStruct((T, B_W), F32)),
        compiler_params=_params("parallel", "parallel"),
    )(q, k, v)


def _gated(y, gate, gain):
    r = _rstd(y)
    nrm = y * r
    sg = jax.nn.sigmoid(gate)
    return r, nrm, sg, nrm * gain, gate * sg


def _out_fwd(ya, yb, proj, na, nb, wout, h, tm, name):
    T = h.shape[0]

    def body(ya_ref, yb_ref, ga_ref, gb_ref, na_ref, nb_ref, w_ref, h_ref, o_ref):
        _, _, _, n_a, sl_a = _gated(ya_ref[...], ga_ref[...], na_ref[...])
        _, _, _, n_b, sl_b = _gated(yb_ref[...], gb_ref[...], nb_ref[...])
        acc = _dot((n_a * sl_a).astype(BF16), w_ref[0:A_W, :]) + _dot((n_b * sl_b).astype(BF16), w_ref[A_W:A_W + B_W, :])
        o_ref[...] = h_ref[...] + acc

    half = pl.BlockSpec((tm, 512), _row(0))
    return pl.pallas_call(
        body, name=name, grid=(T // tm,),
        in_specs=[half, half, pl.BlockSpec((tm, 512), _row(GA0 // 512)), pl.BlockSpec((tm, 512), _row(GB0 // 512)),
                  pl.BlockSpec((1, 512), _const2), pl.BlockSpec((1, 512), _const2), pl.BlockSpec((D, D), _const2),
                  pl.BlockSpec((tm, D), _row(0))],
        out_specs=pl.BlockSpec((tm, D), _row(0)), out_shape=jax.ShapeDtypeStruct((T, D), F32),
        compiler_params=_params("parallel"),
    )(ya, yb, proj, proj, na, nb, wout, h)


def _loss_head(h, gf, target, Bl, S, name):
    T = h.shape[0]
    nblk = (S + BLK) // BLK
    NB = S // BLK

    def body(h_ref, g_ref, t_ref, dh_ref, loss_ref, dg_ref):
        b, n = pl.program_id(0), pl.program_id(1)

        @pl.when((b == 0) & (n == 0))
        def _():
            loss_ref[...] = jnp.zeros_like(loss_ref)
            dg_ref[...] = jnp.zeros_like(dg_ref)

        @pl.when(n < NB)
        def _():
            x = h_ref[...]
            g = g_ref[...]
            r = _rstd(x)
            err = x * r * g - t_ref[0]
            loss_ref[...] += 0.5 * jnp.sum(jnp.mean(err * err, axis=-1, keepdims=True))
            dx, dg = _rms_bwd(err * (1.0 / D), x, r, g)
            dh_ref[...] = dx
            dg_ref[...] += dg

        @pl.when(n >= NB)
        def _():
            dh_ref[...] = jnp.zeros_like(dh_ref)

    return pl.pallas_call(
        body, name=name, grid=(Bl, nblk),
        in_specs=[pl.BlockSpec((BLK, D), lambda b, n: (b * nblk + n, 0)), pl.BlockSpec((1, D), lambda b, n: (0, 0)),
                  pl.BlockSpec((1, BLK, D), lambda b, n: (b, jnp.minimum(n, NB - 1), 0))],
        out_specs=(pl.BlockSpec((BLK, D), lambda b, n: (b * nblk + n, 0)), pl.BlockSpec((8, LANES), lambda b, n: (0, 0)),
                   pl.BlockSpec((1, D), lambda b, n: (0, 0))),
        out_shape=(jax.ShapeDtypeStruct((T, D), F32), jax.ShapeDtypeStruct((8, LANES), F32), jax.ShapeDtypeStruct((1, D), F32)),
        compiler_params=_params("arbitrary", "arbitrary"),
    )(h, gf, target)


def _out_bwd(dh, ya, yb, proj, na, nb, wout, tm, name):
    T = dh.shape[0]

    def body(dh_ref, ya_ref, yb_ref, ga_ref, gb_ref, na_ref, nb_ref, w_ref,
             dya_ref, dyb_ref, dga_ref, dgb_ref, dw_ref, dna_ref, dnb_ref):
        @pl.when(pl.program_id(0) == 0)
        def _():
            dw_ref[...] = jnp.zeros_like(dw_ref)
            dna_ref[...] = jnp.zeros_like(dna_ref)
            dnb_ref[...] = jnp.zeros_like(dnb_ref)

        dhb = dh_ref[...].astype(BF16)
        dy = _dot_nt(dhb, w_ref[...])
        parts = ((ya_ref, ga_ref, na_ref, dya_ref, dga_ref, dna_ref, 0), (yb_ref, gb_ref, nb_ref, dyb_ref, dgb_ref, dnb_ref, A_W))
        for y_ref, gate_ref, gain_ref, dy_out, dgate_out, dgain_out, c0 in parts:
            y, gate, gain = y_ref[...], gate_ref[...], gain_ref[...]
            r, nrm, sg, n_g, sl = _gated(y, gate, gain)
            dyp = dy[:, c0:c0 + 512]
            dgate_out[...] = dyp * n_g * (sg * (1.0 + gate * (1.0 - sg)))
            dx, dgain = _rms_bwd(dyp * sl, y, r, gain)
            dy_out[...] = dx
            dgain_out[...] += dgain
            dw_ref[c0:c0 + 512, :] += _dot_tn((n_g * sl).astype(BF16), dhb)

    half = pl.BlockSpec((tm, 512), _row(0))
    vec = pl.BlockSpec((1, 512), _const2)
    return pl.pallas_call(
        body, name=name, grid=(T // tm,),
        in_specs=[pl.BlockSpec((tm, D), _row(0)), half, half, pl.BlockSpec((tm, 512), _row(GA0 // 512)),
                  pl.BlockSpec((tm, 512), _row(GB0 // 512)), vec, vec, pl.BlockSpec((D, D), _const2)],
        out_specs=(half, half, half, half, pl.BlockSpec((D, D), _const2), vec, vec),
        out_shape=(jax.ShapeDtypeStruct((T, 512), F32),) * 4 + (jax.ShapeDtypeStruct((D, D), F32),)
        + (jax.ShapeDtypeStruct((1, 512), F32),) * 2,
        compiler_params=_params("arbitrary"),
    )(dh, ya, yb, proj, proj, na, nb, wout)


def _mla_bwd(q, k, v, o, do, lse, after, Bl, Lp, name):
    T = q.shape[0]
    n_big, S = _q_tiles(Lp)

    def body(q_ref, k_ref, v_ref, o_ref, do_ref, lse_ref, _, dq_ref, dk_ref, dv_ref):
        dk_ref[...] = jnp.zeros_like(dk_ref)
        dv_ref[...] = jnp.zeros_like(dv_ref)

        def qtile(r0, nq):
            lane = _lane((nq, LANES))
            dop = do_ref[pl.ds(r0, nq), :]
            op = o_ref[pl.ds(r0, nq), :]
            lsev = lse_ref[pl.ds(r0, nq), :]
            for hh in range(2):
                cs = slice(LANES * hh, LANES * (hh + 1))
                dom = jnp.where((lane < B_V) if hh == 0 else (lane >= B_V), dop, 0.0)
                delta = jnp.sum(dom * op, axis=-1, keepdims=True)
                d_hi = delta.astype(BF16).astype(F32)
                x = dom if hh == 0 else pltpu.roll(dom, B_V, 1)
                domx = jnp.where(lane == SUM_LANE, d_hi, jnp.where(lane == SUM_LANE + 1, delta - d_hi, x)).astype(BF16)
                qh, kh, vh = q_ref[pl.ds(r0, nq), cs], k_ref[:, cs], v_ref[:, cs]
                lse_h = jnp.sum(jnp.where(lane == hh, lsev, 0.0), axis=-1, keepdims=True)
                p = jnp.exp2(_dot_nt(qh, kh) - lse_h)
                ds = (p * _dot_nt(domx, vh)).astype(BF16)
                dq_ref[pl.ds(r0, nq), cs] = _dot(ds, kh) * B_SCALE
                dk_ref[:, cs] += _dot_tn(ds, qh)
                dv_ref[:, cs] += _dot_tn(p.astype(BF16), domx)

        def step(i, c):
            qtile(pl.multiple_of(i * 256, 256), 256)
            return c

        lax.fori_loop(0, n_big, step, 0)
        qtile(S, BLK)
        dk_ref[...] = dk_ref[...] * (1.0 / LOG2E)

    pair = lambda b, j: (b, j)
    wide, narrow = pl.BlockSpec((Lp, 256), pair), pl.BlockSpec((Lp, LANES), pair)
    return pl.pallas_call(
        body, name=name, grid=(Bl, B_HEADS // 2),
        in_specs=[wide, wide, wide, narrow, narrow, narrow, pl.BlockSpec(memory_space=pl.ANY)],
        out_specs=(wide, wide, wide), out_shape=(jax.ShapeDtypeStruct((T, 1024), F32),) * 3,
        compiler_params=_params("parallel", "parallel"),
    )(q, k, v, o, do, lse, after)


def _win_bwd(proj, bias, ya, dya, lse, Bl, S, name):
    T = proj.shape[0]
    Lp = S + BLK
    NB = S // BLK
    nblk = Lp // BLK
    G = A_HEADS // A_KV

    def body(q_ref, k_ref, v_ref, b_ref, o_ref, do_ref, lse_ref, dq_ref, dk_ref, dv_ref, db_ref):
        n = pl.program_id(1)

        @pl.when(n == 0)
        def _():
            dk_ref[...] = jnp.zeros_like(dk_ref)
            dv_ref[...] = jnp.zeros_like(dv_ref)

        @pl.when((n == 0) | (n == 1) | (n == NB - 1) | (n == NB))
        def _():
            db_ref[...] = jnp.zeros_like(db_ref)

        base = pl.multiple_of(_win_base(n, NB) * BLK, BLK)
        lane = _lane((BLK, LANES))
        lo = lane < A_DH
        lsev = lse_ref[...]
        kv = [_win_kv(k_ref, v_ref, base, S, kvh) for kvh in range(A_KV)]
        qms, pbs, domxs, dsls = [], [], [], []
        for h in range(A_HEADS):
            k2, v2 = kv[h // G]
            ps = slice(LANES * (h // 2), LANES * (h // 2 + 1))
            qm = (jnp.where(lo if h % 2 == 0 else ~lo, q_ref[:, ps], 0.0) * (A_SCALE * LOG2E)).astype(BF16)
            lse_h = jnp.sum(jnp.where(lane == h, lsev, 0.0), axis=-1, keepdims=True)
            p = jnp.exp2(_dot_nt(qm, k2) + b_ref[0, h] - lse_h)
            dop, op = do_ref[:, ps], o_ref[:, ps]
            if h % 2 == 1:
                dop, op = pltpu.roll(dop, A_DH, 1), pltpu.roll(op, A_DH, 1)
            dom = jnp.where(lo, dop, 0.0)
            delta = jnp.sum(dom * op, axis=-1, keepdims=True)
            d_hi = delta.astype(BF16).astype(F32)
            domx = jnp.where(lane == A_DH, d_hi, jnp.where(lane == A_DH + 1, delta - d_hi, dom)).astype(BF16)
            dsc = p * _dot_nt(domx, v2)
            db_ref[0, 0, h] += dsc
            qms.append(qm)
            pbs.append(p.astype(BF16))
            domxs.append(domx)
            dsls.append(dsc.astype(BF16))
        dqs = [_dot(dsls[h], kv[h // G][0]) * A_SCALE for h in range(A_HEADS)]
        for kvh in range(A_KV):
            cs = slice(LANES * kvh, LANES * (kvh + 1))
            group = range(kvh * G, (kvh + 1) * G)
            stack = lambda parts: jnp.concatenate([parts[h] for h in group], axis=0)
            dk_acc = _dot_tn(stack(dsls), stack(qms)) * (1.0 / LOG2E)
            dv_acc = _dot_tn(stack(pbs), stack(domxs))
            dv_acc = jnp.where(_lane(dv_acc.shape) < A_DH, dv_acc, 0.0)
            dk_ref[pl.ds(base, 3 * BLK), cs] += dk_acc[0:3 * BLK]
            dk_ref[S:S + BLK, cs] += dk_acc[3 * BLK:4 * BLK]
            dv_ref[pl.ds(base, 3 * BLK), cs] += dv_acc[0:3 * BLK]
            dv_ref[S:S + BLK, cs] += dv_acc[3 * BLK:4 * BLK]
        for j in range(A_HEADS // 2):
            dq_ref[:, LANES * j:LANES * (j + 1)] = jnp.where(lo, dqs[2 * j], dqs[2 * j + 1])

    qrow = lambda b, n: (b * nblk + n, 0)
    kvs = pl.BlockSpec((Lp, 256), lambda b, n: (b, 0))
    return pl.pallas_call(
        body, name=name, grid=(Bl, nblk),
        in_specs=[pl.BlockSpec((BLK, A_W), qrow),
                  pl.BlockSpec((Lp, 256), lambda b, n: (b, KA0 // 256)), pl.BlockSpec((Lp, 256), lambda b, n: (b, VA0 // 256)),
                  pl.BlockSpec((1, A_HEADS, BLK, 4 * BLK), lambda b, n: (_win_variant(n, NB), 0, 0, 0)),
                  pl.BlockSpec((BLK, A_W), qrow), pl.BlockSpec((BLK, A_W), qrow), pl.BlockSpec((BLK, LANES), qrow)],
        out_specs=(pl.BlockSpec((BLK, A_W), qrow), kvs, kvs,
                   pl.BlockSpec((1, 1, A_HEADS, BLK, 4 * BLK), lambda b, n: (b, _win_variant(n, NB), 0, 0, 0))),
        out_shape=(jax.ShapeDtypeStruct((T, A_W), F32), jax.ShapeDtypeStruct((T, 256), F32),
                   jax.ShapeDtypeStruct((T, 256), F32), jax.ShapeDtypeStruct((Bl, 4, A_HEADS, BLK, 4 * BLK), F32)),
        compiler_params=_params("parallel", "arbitrary"),
    )(proj, proj, proj, bias, ya, dya, lse)


def _table_grad(dbias_list, buckets, after, name):
    nl = len(dbias_list)
    Bl = dbias_list[0].shape[0]

    def body(*refs):
        d_refs = refs[:nl]
        b_ref, _, o_ref, s_ref, acc, part = refs[nl:]
        v, b = pl.program_id(0), pl.program_id(1)

        @pl.when((v == 0) & (b == 0))
        def _():
            o_ref[...] = jnp.zeros_like(o_ref)
            s_ref[...] = jnp.zeros_like(s_ref)

        slot = _lane((BLK, 4 * BLK))
        for l in range(nl):
            for h in range(A_HEADS):
                col = jnp.sum(jnp.where(slot == SINK_SLOT, d_refs[l][0, 0, h], 0.0), axis=0, keepdims=True)
                s_ref[l, h:h + 1, :] += jnp.sum(col, axis=1, keepdims=True)

        tot = d_refs[0][0, 0]
        for r in d_refs[1:]:
            tot = tot + r[0, 0]

        @pl.when(b == 0)
        def _():
            acc[...] = tot

        @pl.when(b > 0)
        def _():
            acc[...] += tot

        @pl.when(b == Bl - 1)
        def _():
            bidx = b_ref[0]

            def step(j, c):
                mask = bidx == j
                for h in range(A_HEADS):
                    part[h, j] = jnp.sum(jnp.where(mask, acc[h], 0.0).reshape(BLK // 8, 8, 4 * BLK), axis=0)
                return c

            lax.fori_loop(0, N_BUCKETS, step, 0)
            rows = lax.broadcasted_iota(jnp.int32, (N_BUCKETS, LANES), 0)
            lanes = _lane((N_BUCKETS, LANES))
            for h in range(A_HEADS):
                col = jnp.sum(jnp.sum(part[h], axis=1), axis=-1, keepdims=True)
                o_ref[h:h + 1, :] += jnp.sum(jnp.where(rows == lanes, col, 0.0), axis=0, keepdims=True)

    return pl.pallas_call(
        body, name=name, grid=(4, Bl),
        in_specs=[pl.BlockSpec((1, 1, A_HEADS, BLK, 4 * BLK), lambda v, b: (b, v, 0, 0, 0))] * nl
        + [pl.BlockSpec((1, BLK, 4 * BLK), lambda v, b: (v, 0, 0)), pl.BlockSpec(memory_space=pl.ANY)],
        out_specs=(pl.BlockSpec((8, LANES), lambda v, b: (0, 0)), pl.BlockSpec((nl, 8, LANES), lambda v, b: (0, 0, 0))),
        out_shape=(jax.ShapeDtypeStruct((8, LANES), F32), jax.ShapeDtypeStruct((nl, 8, LANES), F32)),
        scratch_shapes=[pltpu.VMEM((A_HEADS, BLK, 4 * BLK), F32), pltpu.VMEM((A_HEADS, N_BUCKETS, 8, 4 * BLK), F32)],
        compiler_params=_params("arbitrary", "arbitrary"),
    )(*dbias_list, buckets, after)


def _mla_prep_bwd(dq, dk, dv, proj, gq, gkv, wq, wkk, wkv, tq, tk, tm, name):
    T = proj.shape[0]

    def body(dq_ref, dk_ref, dv_ref, cq_ref, ckv_ref, gq_ref, gkv_ref, wq_ref, wkk_ref, wkv_ref, tq_ref, tk_ref,
             dcq_ref, dckv_ref, dkr_ref, dwq_ref, dwkk_ref, dwkv_ref, dgq_ref, dgkv_ref):
        @pl.when(pl.program_id(0) == 0)
        def _():
            for r in (dwq_ref, dwkk_ref, dwkv_ref, dgq_ref, dgkv_ref):
                r[...] = jnp.zeros_like(r)

        tqv = tq_ref[...]
        dqp = jnp.concatenate([_rope_t(dq_ref[:, LANES * h:LANES * (h + 1)], tqv) for h in range(B_HEADS)],
                              axis=1).astype(BF16)
        cq, gq_ = cq_ref[...], gq_ref[...]
        rq = _rstd(cq)
        dwq_ref[...] += _dot_tn((cq * rq * gq_).astype(BF16), dqp)
        dx, dg = _rms_bwd(_dot_nt(dqp, wq_ref[...]), cq, rq, gq_)
        dcq_ref[...] = dx
        dgq_ref[...] += dg

        dkv_ = dk_ref[...]
        dks = dkv_[:, 0:LANES]
        for h in range(1, B_HEADS):
            dks = dks + dkv_[:, LANES * h:LANES * (h + 1)]
        dkr_ref[...] = _rope_t(pltpu.roll(dks, NOPE, 1), tk_ref[...])
        dkb = dkv_.astype(BF16)
        dvb = dv_ref[...].astype(BF16)
        ckv, gkv_ = ckv_ref[...], gkv_ref[...]
        rk = _rstd(ckv)
        cb = (ckv * rk * gkv_).astype(BF16)
        dwkk_ref[...] += _dot_tn(cb, dkb)
        dwkv_ref[...] += _dot_tn(cb, dvb)
        dx, dg = _rms_bwd(_dot_nt(dkb, wkk_ref[...]) + _dot_nt(dvb, wkv_ref[...]), ckv, rk, gkv_)
        dckv_ref[...] = dx
        dgkv_ref[...] += dg

    tab = lambda n: pl.BlockSpec((n, tm, LANES), lambda i: (0, i, 0))
    wide = pl.BlockSpec((tm, 1024), _row(0))
    return pl.pallas_call(
        body, name=name, grid=(T // tm,),
        in_specs=[wide, wide, wide, pl.BlockSpec((tm, 256), _row(CQ0 // 256)),
                  pl.BlockSpec((tm, 128), _row(CKV0 // 128)), pl.BlockSpec((1, Q_RANK), _const2),
                  pl.BlockSpec((1, KV_RANK), _const2), pl.BlockSpec((Q_RANK, 1024), _const2),
                  pl.BlockSpec((KV_RANK, 1024), _const2), pl.BlockSpec((KV_RANK, 1024), _const2), tab(3), tab(4)],
        out_specs=(pl.BlockSpec((tm, 256), _row(0)), pl.BlockSpec((tm, 128), _row(0)), pl.BlockSpec((tm, 128), _row(0)),
                   pl.BlockSpec((Q_RANK, 1024), _const2), pl.BlockSpec((KV_RANK, 1024), _const2),
                   pl.BlockSpec((KV_RANK, 1024), _const2), pl.BlockSpec((1, Q_RANK), _const2),
                   pl.BlockSpec((1, KV_RANK), _const2)),
        out_shape=(jax.ShapeDtypeStruct((T, 256), F32), jax.ShapeDtypeStruct((T, 128), F32),
                   jax.ShapeDtypeStruct((T, 128), F32), jax.ShapeDtypeStruct((Q_RANK, 1024), F32),
                   jax.ShapeDtypeStruct((KV_RANK, 1024), F32), jax.ShapeDtypeStruct((KV_RANK, 1024), F32),
                   jax.ShapeDtypeStruct((1, Q_RANK), F32), jax.ShapeDtypeStruct((1, KV_RANK), F32)),
        compiler_params=_params("arbitrary"),
    )(dq, dk, dv, proj, proj, gq, gkv, wq, wkk, wkv, tq, tk)


def _inproj_bwd(pieces, h, g, wx, dh_out, tm, name, part="both"):
    T = h.shape[0]
    nsteps = T // tm
    CH = 512
    want_dw, want_dx = part != "dx", part != "dw"
    npc = len(pieces)

    def body(*refs):
        p_refs, (h_ref, g_ref) = refs[:npc], refs[npc:npc + 2]
        rest = list(refs[npc + 2:])
        if want_dx:
            w_ref, dho_ref, dh_ref, dg_ref = rest[0], rest[1], rest[2], rest[3]
            rest = rest[4:]
        if want_dw:
            dw_hbm, acc, sem = rest
        i = pl.program_id(0)

        @pl.when(i == 0)
        def _():
            if want_dw:
                acc[...] = jnp.zeros_like(acc)
            if want_dx:
                dg_ref[...] = jnp.zeros_like(dg_ref)

        dp = jnp.concatenate([r[...].astype(BF16) for r in p_refs], axis=1)
        x, gain = h_ref[...], g_ref[...]
        r = _rstd(x)
        if want_dw:
            u = (x * r * gain).astype(BF16)
            for c in range(0, NX, CH):
                acc[:, c:c + CH] += _dot_tn(u, dp[:, c:c + CH])
        if want_dx:
            dx, dg = _rms_bwd(_dot_nt(dp, w_ref[...]), x, r, gain)
            dh_ref[...] = dho_ref[...] + dx
            dg_ref[...] += dg

        if want_dw:
            @pl.when(i == nsteps - 1)
            def _():
                cp = pltpu.make_async_copy(acc, dw_hbm, sem)
                cp.start()
                cp.wait()

    rows = pl.BlockSpec((tm, D), _row(0))
    in_specs = [pl.BlockSpec((tm, p.shape[1]), _row(0)) for p in pieces] + [rows, pl.BlockSpec((1, D), _const2)]
    args, out_specs, out_shape, scratch = [*pieces, h, g], [], [], []
    if want_dx:
        in_specs += [pl.BlockSpec((D, NX), _const2), rows]
        args += [wx, dh_out]
        out_specs += [rows, pl.BlockSpec((1, D), _const2)]
        out_shape += [jax.ShapeDtypeStruct((T, D), F32), jax.ShapeDtypeStruct((1, D), F32)]
    if want_dw:
        out_specs.append(pl.BlockSpec(memory_space=pl.ANY))
        out_shape.append(jax.ShapeDtypeStruct((D, NX), F32))
        scratch = [pltpu.VMEM((D, NX), F32), pltpu.SemaphoreType.DMA(())]
    outs = list(pl.pallas_call(
        body, name=name, grid=(nsteps,), in_specs=in_specs, out_specs=tuple(out_specs), out_shape=tuple(out_shape),
        scratch_shapes=scratch, compiler_params=_params("arbitrary"),
    )(*args))
    dh, dg = (outs.pop(0), outs.pop(0)) if want_dx else (None, None)
    return dh, dg, (outs.pop(0) if want_dw else None)


def kernel(x, meta_tokens, rel_bias_table, norm_in, w_in, sink_a, norm_q_lat, w_uq, norm_kv_lat, w_ukv, norm_out_a, norm_out_b, w_out, norm_final, loss_target, m_meta_tokens, m_rel_bias_table, m_norm_in, m_w_in, m_sink_a, m_norm_q_lat, m_w_uq, m_norm_kv_lat, m_w_ukv, m_norm_out_a, m_norm_out_b, m_w_out, m_norm_final, v_meta_tokens, v_rel_bias_table, v_norm_in, v_w_in, v_sink_a, v_norm_q_lat, v_w_uq, v_norm_kv_lat, v_w_ukv, v_norm_out_a, v_norm_out_b, v_w_out, v_norm_final):
    Bl, S, _ = x.shape
    assert S % 256 == 0 and S >= 3 * BLK, "the attention kernels tile the real tokens in 256-row blocks"
    Lp = S + BLK
    T = Bl * Lp
    tm = Lp // 4
    tq = Lp // 8
    depth = w_in.shape[0]
    me = 4 * lax.axis_index("x") + 2 * lax.axis_index("y") + lax.axis_index("c")

    def finish(handles, after, name):
        srcs, lands = _xchg_wait(handles, after, name)
        full = []
        for hd, src, land in zip(handles, srcs, lands):
            own = src if hd[4] == "gather" else lax.dynamic_index_in_dim(src, me, 0, keepdims=False)
            full.append(_own_slot(land, own, me))
        return full

    def weights_of(i):
        return [w_in[i].astype(BF16), w_uq[i].astype(BF16), w_ukv[i].astype(BF16), w_out[i].astype(BF16)]

    def packed(l_in, l_uq, l_ukv, l_out):
        cols = lambda t: jnp.transpose(t, (1, 0, 2)).reshape(t.shape[1], NDEV * t.shape[2])
        wkk, wkv = _pack_w_ukv(cols(l_ukv))
        return _pack_w_in(l_in), _pack_w_uq(cols(l_uq)), wkk, wkv, l_out.reshape(D, D)

    wb = weights_of(0)
    gat_a, tok = _xchg_start([wb[0], meta_tokens], ["gather", "gather"], "gather_start_0a")
    buckets = _bias_buckets(S)
    biases = _build_bias(buckets, _after(rel_bias_table, tok), sink_a, "build_bias")
    tb = 256 if T % 256 == 0 else tq
    tq_tab = jnp.tile(_rope_tables(S, Lp, NOPE, True), (1, Bl, 1))
    key_pad = jnp.where((jnp.arange(Lp) >= S + NMETA)[:, None] & (jnp.arange(LANES) == MASK_LANE)[None, :], NEG, 0.0)
    tk_tab = jnp.tile(jnp.concatenate([_rope_tables(S, Lp, 0, False), key_pad.astype(F32)[None]], axis=0), (1, Bl, 1))
    l_in, l_meta = finish(gat_a, biases[0], "gather_wait_0a")
    gat_b, tok = _xchg_start(wb[1:], ["gather"] * 3, "gather_start_0b", after=l_in)

    meta_f = jnp.transpose(l_meta, (1, 0, 2)).reshape(NMETA, D)
    tail = jnp.concatenate([meta_f, jnp.zeros((BLK - NMETA, D), F32)], axis=0)
    h = jnp.concatenate([x, jnp.broadcast_to(tail[None], (Bl, BLK, D))], axis=1).reshape(T, D)

    saved = []
    w_out_f = [None] * depth
    gat_next = None
    for i in range(depth):
        g_in, g_q, g_kv = norm_in[i][None], norm_q_lat[i][None], norm_kv_lat[i][None]
        g_a, g_b = norm_out_a[i][None], norm_out_b[i][None]
        if i == 0:
            proj = _inproj_fwd(h, _after(g_in, tok), _pack_w_in(l_in), tm,
                               f"inproj_fwd_{i}")
            l_uq, l_ukv, l_out = finish(gat_b, proj, "gather_wait_0b")
            wx, wq, wkk, wkv, w_out_f[i] = packed(l_in, l_uq, l_ukv, l_out)
        else:
            wx, wq, wkk, wkv, w_out_f[i] = packed(*finish(gat_next, h, f"gather_wait_{i}"))
            proj = _inproj_fwd(h, g_in, wx, tm, f"inproj_fwd_{i}")
        if i + 1 < depth:
            gat_next, tok = _xchg_start(weights_of(i + 1), ["gather"] * 4, f"gather_start_{i + 1}", after=w_out_f[i])
            g_q = _after(g_q, tok)
        q, k, v = _mla_prep_fwd(proj, g_q, g_kv, wq, wkk, wkv, tq_tab, tk_tab, tm, f"mla_prep_fwd_{i}")
        ya, lse_a = _win_fwd(proj, biases[i], Bl, S, f"win_fwd_{i}")
        yb, lse_b = _mla_fwd(q, k, v, Bl, Lp, f"mla_fwd_{i}")
        h_new = _out_fwd(ya, yb, proj, g_a, g_b, w_out_f[i], h, tm, f"out_fwd_{i}")
        saved.append((h, proj, q, k, v, ya, lse_a, yb, lse_b, wx, wq, wkk, wkv))
        h = h_new

    dh, loss_acc, dg_final = _loss_head(h, norm_final[None], loss_target, Bl, S, "loss_head")

    g_n_in, g_nq, g_nkv, g_na, g_nb = ([None] * depth for _ in range(5))
    dbias_all, sc_out, sc_rest = [None] * depth, [None] * depth, [None] * depth
    split = lambda t, n: jnp.transpose(t.reshape(t.shape[0], NDEV, n), (1, 0, 2)).astype(BF16)
    tok = None
    for i in reversed(range(depth)):
        h_in, proj, q, k, v, ya, lse_a, yb, lse_b, wx, wq, wkk, wkv = saved[i]
        g_in, g_q, g_kv = norm_in[i][None], norm_q_lat[i][None], norm_kv_lat[i][None]
        g_a, g_b = norm_out_a[i][None], norm_out_b[i][None]
        if tok is not None:
            g_a = _after(g_a, tok)
        dya, dyb, dga, dgb, g_w_out, g_na[i], g_nb[i] = _out_bwd(dh, ya, yb, proj, g_a, g_b, w_out_f[i], tb, f"out_bwd_{i}")
        sc_out[i], tok = _xchg_start([g_w_out.reshape(NDEV, D // NDEV, D).astype(BF16)], ["scatter"], f"scatter_start_{i}a")
        dq, dk, dv = _mla_bwd(q, k, v, yb, dyb, lse_b, tok, Bl, Lp, f"mla_bwd_{i}")
        dqa, dka, dva, dbias = _win_bwd(proj, biases[i], ya, dya, lse_a, Bl, S, f"win_bwd_{i}")
        dcq, dckv, dkr, dwq, dwkk, dwkv, g_nq[i], g_nkv[i] = _mla_prep_bwd(
            dq, dk, dv, proj, g_q, g_kv, wq, wkk, wkv, tq_tab, tk_tab, tm, f"mla_prep_bwd_{i}")
        pieces = (dqa, dka, dva, dga, dcq, dckv, dkr, dgb)
        dbias_all[i] = dbias
        rest = [split(_unpack_w_uq_grad(dwq), 768 // NDEV), split(_unpack_w_ukv_grad(dwkk, dwkv), 1024 // NDEV)]
        if i > 0:
            dh, g_n_in[i], dwx = _inproj_bwd(pieces, h_in, g_in, wx, dh, tb, f"inproj_bwd_{i}")
            sc_rest[i], tok = _xchg_start([_unpack_w_in_grad(dwx).astype(BF16)] + rest, ["scatter"] * 3, f"scatter_start_{i}b")
        else:
            _, _, dwx = _inproj_bwd(pieces, h_in, g_in, None, None, tb, f"inproj_bwd_{i}w", part="dw")
            sc_rest[i], tok = _xchg_start([_unpack_w_in_grad(dwx).astype(BF16)] + rest, ["scatter"] * 3, f"scatter_start_{i}b")
            dh, g_n_in[i], _ = _inproj_bwd(pieces, h_in, _after(g_in, tok), wx, dh, tb, f"inproj_bwd_{i}x", part="dx")

    dh3 = dh.reshape(Bl, Lp, D)
    grad_x = dh3[:, :S]
    g_meta = jnp.transpose(jnp.sum(dh3[:, S:S + NMETA], axis=0).reshape(NMETA, NDEV, D // NDEV), (1, 0, 2))
    dtab, dsink = _table_grad(dbias_all, buckets, tok, "table_grad")
    two_d = lambda a: a.reshape(1, -1) if a.ndim == 1 else a
    small_w = [rel_bias_table, norm_in, sink_a, norm_q_lat, norm_kv_lat, norm_out_a, norm_out_b, norm_final]
    small_m = [m_rel_bias_table, m_norm_in, m_sink_a, m_norm_q_lat, m_norm_kv_lat, m_norm_out_a, m_norm_out_b, m_norm_final]
    small_v = [v_rel_bias_table, v_norm_in, v_sink_a, v_norm_q_lat, v_norm_kv_lat, v_norm_out_a, v_norm_out_b, v_norm_final]
    small_g = [jnp.transpose(dtab[:, :N_BUCKETS]), jnp.concatenate(g_n_in), dsink[:, :, 0], jnp.concatenate(g_nq),
               jnp.concatenate(g_nkv), jnp.concatenate(g_na), jnp.concatenate(g_nb), dg_final]
    sc_small, tok = _xchg_start(small_g + [loss_acc, g_meta], ["gather"] * (len(small_g) + 1) + ["scatter"],
                                "scatter_start_small")

    res = {}

    def update(n, r, w, m, v):
        outs = _adamw_param(r, w, m, v, f"adamw_{n}")
        res[n] = list(outs)
        return outs[0]

    r_out = [finish(sc_out[i], tok, f"scatter_wait_{i}a")[0] for i in range(depth)]
    done = update("w_out", r_out, w_out, m_w_out, v_w_out)
    r_in, r_uq, r_ukv = [None] * depth, [None] * depth, [None] * depth
    for i in reversed(range(depth)):
        r_in[i], r_uq[i], r_ukv[i] = finish(sc_rest[i], done if i > 0 else grad_x, f"scatter_wait_{i}b")
    r_small = finish(sc_small, r_in[0], "scatter_wait_small")
    update("w_in", r_in, w_in, m_w_in, v_w_in)
    update("w_uq", r_uq, w_uq, m_w_uq, v_w_uq)
    update("w_ukv", r_ukv, w_ukv, m_w_ukv, v_w_ukv)
    update("meta_tokens", [r_small[-1]], meta_tokens[None], m_meta_tokens[None], v_meta_tokens[None])
    res["meta_tokens"] = [o[0] for o in res["meta_tokens"]]
    names_small = ["rel_bias_table", "norm_in", "sink_a", "norm_q_lat", "norm_kv_lat", "norm_out_a", "norm_out_b", "norm_final"]
    outs = _adamw_small(r_small[:-2], [two_d(a) for a in small_w], [two_d(a) for a in small_m], [two_d(a) for a in small_v],
                        r_small[-2], "adamw_replicated")
    loss = outs[-1][0, 0]
    ns = len(names_small)
    for j, n in enumerate(names_small):
        res[n] = [outs[kk * ns + j].reshape(small_w[j].shape) for kk in range(4)]
    order = ["meta_tokens", "rel_bias_table", "norm_in", "w_in", "sink_a", "norm_q_lat", "w_uq", "norm_kv_lat", "w_ukv",
             "norm_out_a", "norm_out_b", "w_out", "norm_final"]
    return (loss, grad_x, *[res[n][kk] for kk in range(4) for n in order])
```

```python
import functools
import math

import numpy as np
import jax
import jax.numpy as jnp
from jax import lax
from jax.experimental import pallas as pl
from jax.experimental.pallas import tpu as pltpu

F32, BF16 = jnp.float32, jnp.bfloat16
D = 1024
NMETA = 16
BLK = 128
A_HEADS, A_KV, A_DH, A_W = 8, 2, 64, 512
B_HEADS, NOPE, ROPE, B_V, B_W = 8, 64, 32, 64, 512
Q_RANK, KV_RANK = 256, 128
IN_W = 2208
N_BUCKETS, MAX_DIST = 32, 128
THETA = 10000.0
EPS = 1e-6
NEG = -1e30
A_SCALE = A_DH ** -0.5
B_SCALE = (NOPE + ROPE) ** -0.5
LOG2E = math.log2(math.e)
MASK_LANE = NOPE + ROPE
SUM_LANE = B_V
LANES = 128
NDEV = 8
MESH = pl.DeviceIdType.MESH

NX = 2560
QA0, KA0, VA0, GA0, CQ0, CKV0, KR0, GB0 = 0, 512, 768, 1024, 1536, 1792, 1920, 2048

ADAM_LR, ADAM_B1, ADAM_B2, ADAM_EPS, ADAM_WD, ADAM_STEP = 0.001, 0.9, 0.999, 1e-08, 0.01, 10


def _dot(a, b):
    return jnp.dot(a, b, preferred_element_type=F32)


def _dot_nt(a, b):
    return lax.dot_general(a, b, (((1,), (1,)), ((), ())), preferred_element_type=F32)


def _dot_tn(a, b):
    return lax.dot_general(a, b, (((0,), (0,)), ((), ())), preferred_element_type=F32)


def _lane(shape):
    return lax.broadcasted_iota(jnp.int32, shape, len(shape) - 1)


def _rstd(x):
    return lax.rsqrt(jnp.mean(x * x, axis=-1, keepdims=True) + EPS)


def _rms_bwd(dn, x, r, g):
    z = dn * g
    dx = r * z - x * (r * r * r) * jnp.mean(z * x, axis=-1, keepdims=True)
    return dx, jnp.sum(dn * (x * r), axis=0, keepdims=True)


def _row(i):
    return lambda *ids: (ids[0], i)


def _const2(*ids):
    return (0, 0)


def _params(*sem):
    return pltpu.CompilerParams(dimension_semantics=sem)


SHARD_W = IN_W // NDEV


def _pack_w_in(shards):
    def nat(a, b):
        parts = []
        while a < b:
            p, e = a // SHARD_W, min(b, (a // SHARD_W + 1) * SHARD_W)
            parts.append(shards[p][:, a - p * SHARD_W:e - p * SHARD_W])
            a = e
        return parts
    dup = lambda a: nat(a, a + 64) * 2 + nat(a + 64, a + 128) * 2
    zeros = [jnp.zeros((shards.shape[1], LANES - ROPE), shards.dtype)]
    return jnp.concatenate(nat(0, 512) + dup(512) + dup(640) + nat(768, 1280) + nat(1280, 1536) + nat(1536, 1664)
                           + nat(1664, 1696) + zeros + nat(1696, 2208), axis=1)


_NAT_GROUPS = ((0, 512, QA0), (512, 576, KA0), (576, 640, KA0 + 128), (640, 704, VA0), (704, 768, VA0 + 128),
               (768, 1280, GA0), (1280, 1536, CQ0), (1536, 1664, CKV0), (1664, 1696, KR0), (1696, 2208, GB0))


def _unpack_w_in_grad(g):
    def nat(a, b):
        lo, _, pk = next(grp for grp in _NAT_GROUPS if grp[0] <= a < grp[1])
        t = g[:, pk + a - lo:pk + b - lo]
        return t + g[:, pk + 64 + a - lo:pk + 64 + b - lo] if 512 <= a < 768 else t
    cuts = sorted({c for grp in _NAT_GROUPS for c in grp[:2]} | {p * SHARD_W for p in range(NDEV + 1)})
    shards = [[] for _ in range(NDEV)]
    for a, b in zip(cuts[:-1], cuts[1:]):
        shards[a // SHARD_W].append(nat(a, b))
    return jnp.stack([jnp.concatenate(parts, axis=1) for parts in shards])


def _pack_w_uq(w):
    t = w.reshape(Q_RANK, B_HEADS, NOPE + ROPE)
    t = jnp.concatenate([t, jnp.zeros((Q_RANK, B_HEADS, LANES - NOPE - ROPE), w.dtype)], axis=-1)
    return t.reshape(Q_RANK, B_HEADS * LANES)


def _unpack_w_uq_grad(g):
    return g.reshape(Q_RANK, B_HEADS, LANES)[:, :, :NOPE + ROPE].reshape(Q_RANK, B_HEADS * (NOPE + ROPE))


def _pack_w_ukv(w):
    t = w.reshape(KV_RANK, B_HEADS, NOPE + B_V)
    z = jnp.zeros((KV_RANK, B_HEADS, LANES - NOPE), w.dtype)
    pad = lambda u: jnp.concatenate([u, z], axis=-1).reshape(KV_RANK, B_HEADS * LANES)
    return pad(t[:, :, :NOPE]), pad(t[:, :, NOPE:])


def _unpack_w_ukv_grad(gk, gv):
    head = lambda g: g.reshape(KV_RANK, B_HEADS, LANES)[:, :, :NOPE]
    return jnp.concatenate([head(gk), head(gv)], axis=-1).reshape(KV_RANK, B_HEADS * (NOPE + B_V))


def _t5_bucket(rel):
    nb = N_BUCKETS // 2
    max_exact = nb // 2
    ret = jnp.where(rel > 0, nb, 0)
    n = jnp.abs(rel)
    nf = jnp.maximum(n, 1).astype(F32)
    large = max_exact + (jnp.log(nf / max_exact) / math.log(MAX_DIST / max_exact) * (nb - max_exact)).astype(jnp.int32)
    large = jnp.minimum(large, nb - 1)
    return ret + jnp.where(n < max_exact, n, large)


def _bias_buckets(S):
    q = np.arange(BLK)[:, None]
    k = np.arange(4 * BLK)[None, :]
    is_meta_key = (k >= 3 * BLK) & (k < 3 * BLK + NMETA)
    mi = k - 3 * BLK
    rels, valids = [], []
    for shift in (0, BLK, 2 * BLK):
        rel_real = k - shift - q
        valid_real = (k < 3 * BLK) & (np.abs(rel_real) <= BLK)
        far = -(NMETA + MAX_DIST + BLK)
        rel_meta = (mi - (NMETA + q)) if shift == 0 else np.full_like(k + q, far)
        rels.append(np.where(is_meta_key, rel_meta, rel_real))
        valids.append(valid_real | is_meta_key)
    qm = q < NMETA
    rel_real = NMETA + k - q
    valid_real = (k < BLK) & (np.abs(rel_real) <= BLK) & qm
    rels.append(np.where(is_meta_key, mi - q, rel_real))
    valids.append(valid_real | is_meta_key)
    rel = jnp.asarray(np.stack(rels).astype(np.int32))
    valid = jnp.asarray(np.stack(valids))
    return jnp.where(valid, _t5_bucket(rel), -1).astype(jnp.int32)


def _rope_tables(S, Lp, off, passthrough):
    half = ROPE // 2
    r = np.arange(Lp)
    pos = np.where(r < S, NMETA + r, np.where(r < S + NMETA, r - S, 0)).astype(np.float32)
    freqs = THETA ** (-jnp.arange(half, dtype=F32) / half)
    ang = jnp.asarray(pos)[:, None] * freqs[None, :]
    cos, sin = jnp.cos(ang), jnp.sin(ang)
    z = lambda n: jnp.zeros((Lp, n), F32)
    head = jnp.ones((Lp, off), F32) if passthrough else z(off)
    c = jnp.concatenate([head, cos, cos, z(LANES - off - ROPE)], axis=1)
    s1 = jnp.concatenate([z(off), -sin, z(LANES - off - half)], axis=1)
    s2 = jnp.concatenate([z(off + half), sin, z(LANES - off - ROPE)], axis=1)
    return jnp.stack([c, s1, s2])


def _rope(x, t):
    return x * t[0] + pltpu.roll(x, LANES - 16, 1) * t[1] + pltpu.roll(x, 16, 1) * t[2]


def _rope_t(dy, t):
    return dy * t[0] + pltpu.roll(dy * t[1], 16, 1) + pltpu.roll(dy * t[2], LANES - 16, 1)


def _me_and_peers():
    x, y, c = lax.axis_index("x"), lax.axis_index("y"), lax.axis_index("c")
    flip = lambda v, b: 1 - v if b else v
    peers = [(flip(x, k & 4), flip(y, k & 2), flip(c, k & 1)) for k in range(1, NDEV)]
    return 4 * x + 2 * y + c, peers


def _xchg_start(srcs, kinds, name, after=None):
    n = len(srcs)
    lands = [lax.empty((NDEV,) + s.shape[-2:], s.dtype) for s in srcs]
    extra = [] if after is None else [after]

    def body(*refs):
        src_refs, land_refs = refs[:n], refs[n:2 * n]
        ssems, rsems = refs[2 * n + len(extra):3 * n + len(extra)], refs[3 * n + len(extra):4 * n + len(extra)]
        token = refs[6 * n + len(extra)]
        me, peers = _me_and_peers()
        for it in range(n):
            for k, (px, py, pc) in enumerate(peers):
                src = src_refs[it] if kinds[it] == "gather" else src_refs[it].at[4 * px + 2 * py + pc]
                pltpu.make_async_remote_copy(src_ref=src, dst_ref=land_refs[it].at[me], send_sem=ssems[it].at[k],
                                             recv_sem=rsems[it].at[k], device_id=(px, py, pc), device_id_type=MESH).start()
        token[...] = jnp.zeros_like(token)

    hbm = pl.BlockSpec(memory_space=pltpu.HBM)
    sem = pl.BlockSpec(memory_space=pltpu.SEMAPHORE)
    outs = pl.pallas_call(
        body, name=name,
        out_shape=tuple([pltpu.SemaphoreType.DMA((NDEV - 1,))] * (2 * n) + [pltpu.HBM(a.shape, a.dtype) for a in srcs + lands]
                        + [jax.ShapeDtypeStruct((8, LANES), F32)]),
        in_specs=[hbm] * (2 * n) + [pl.BlockSpec(memory_space=pl.ANY)] * len(extra),
        out_specs=tuple([sem] * (2 * n) + [hbm] * (2 * n) + [pl.BlockSpec(memory_space=pltpu.VMEM)]),
        input_output_aliases={i: 2 * n + i for i in range(2 * n)},
        compiler_params=pltpu.CompilerParams(has_side_effects=pltpu.SideEffectType.DATAFLOW_SIDE_EFFECTING),
    )(*[pltpu.with_memory_space_constraint(a, pltpu.HBM) for a in srcs + lands], *extra)
    handles = [(outs[it], outs[n + it], outs[2 * n + it], outs[3 * n + it], kinds[it]) for it in range(n)]
    return handles, outs[4 * n]


def _xchg_wait(handles, after, name):
    n = len(handles)

    def body(*refs):
        src_refs, land_refs = refs[:n], refs[n:2 * n]
        ssems, rsems = refs[2 * n:3 * n], refs[3 * n:4 * n]
        me, peers = _me_and_peers()
        for it in range(n):
            for k, (px, py, pc) in enumerate(peers):
                src = src_refs[it] if handles[it][4] == "gather" else src_refs[it].at[4 * px + 2 * py + pc]
                cp = pltpu.make_async_remote_copy(src_ref=src, dst_ref=land_refs[it].at[me], send_sem=ssems[it].at[k],
                                                  recv_sem=rsems[it].at[k], device_id=(px, py, pc), device_id_type=MESH)
                cp.wait_send()
                cp.wait_recv()

    hbm = pl.BlockSpec(memory_space=pltpu.HBM)
    sem = pl.BlockSpec(memory_space=pltpu.SEMAPHORE)
    srcs, lands = [h[2] for h in handles], [h[3] for h in handles]
    outs = pl.pallas_call(
        body, name=name, out_shape=tuple(pltpu.HBM(a.shape, a.dtype) for a in srcs + lands),
        in_specs=[hbm] * (2 * n) + [sem] * (2 * n) + [pl.BlockSpec(memory_space=pl.ANY)], out_specs=tuple([hbm] * (2 * n)),
        input_output_aliases={i: i for i in range(2 * n)},
        compiler_params=pltpu.CompilerParams(has_side_effects=pltpu.SideEffectType.DATAFLOW_SIDE_EFFECTING),
    )(*srcs, *lands, *[h[0] for h in handles], *[h[1] for h in handles], after)
    return list(outs[:n]), list(outs[n:])


def _own_slot(land, own, me):
    return lax.dynamic_update_slice(land, own[None].astype(land.dtype), (me, 0, 0))


def _after(x, token):
    return x + token[0, 0]


def _adamw_update(g, w, m, v):
    mn = ADAM_B1 * m + (1.0 - ADAM_B1) * g
    vn = ADAM_B2 * v + (1.0 - ADAM_B2) * (g * g)
    m_hat = mn / (1.0 - ADAM_B1 ** ADAM_STEP)
    v_hat = vn / (1.0 - ADAM_B2 ** ADAM_STEP)
    return -ADAM_LR * (m_hat / (jnp.sqrt(v_hat) + ADAM_EPS) + ADAM_WD * w), mn, vn


def _sum_partials(r_ref):
    g = r_ref[0].astype(F32)
    for p in range(1, NDEV):
        g = g + r_ref[p].astype(F32)
    return g


def _adamw_param(recvs, w, m, v, name):
    depth, r, c = w.shape
    tr = min(r, 256)
    nl = len(recvs)

    def body(*refs):
        r_refs = refs[:nl]
        w_ref, m_ref, v_ref, g_out, d_out, m_out, v_out = refs[nl:]
        for li in range(nl):
            @pl.when(pl.program_id(0) == li)
            def _(li=li):
                g = _sum_partials(r_refs[li])
                g_out[0] = g
                d_out[0], m_out[0], v_out[0] = _adamw_update(g, w_ref[0], m_ref[0], v_ref[0])

    blk = pl.BlockSpec((1, tr, c), lambda l, i: (l, i, 0))
    return pl.pallas_call(
        body, name=name, grid=(depth, r // tr),
        in_specs=[pl.BlockSpec((NDEV, tr, c), lambda l, i: (0, i, 0))] * nl + [blk, blk, blk],
        out_specs=(blk, blk, blk, blk), out_shape=(jax.ShapeDtypeStruct(w.shape, F32),) * 4,
        compiler_params=_params("arbitrary", "arbitrary"),
    )(*recvs, w, m, v)


def _adamw_small(recvs, ws, ms, vs, loss_parts, name):
    n = len(ws)

    def body(*refs):
        r_refs, w_refs, m_refs, v_refs = refs[:n], refs[n:2 * n], refs[2 * n:3 * n], refs[3 * n:4 * n]
        outs = refs[4 * n + 1:]
        for j in range(n):
            g = _sum_partials(r_refs[j])
            outs[j][...] = g
            outs[n + j][...], outs[2 * n + j][...], outs[3 * n + j][...] = _adamw_update(
                g, w_refs[j][...], m_refs[j][...], v_refs[j][...])
        outs[4 * n][...] = _sum_partials(refs[4 * n])

    vm = pl.BlockSpec(memory_space=pltpu.VMEM)
    return pl.pallas_call(
        body, name=name, in_specs=[vm] * (4 * n + 1), out_specs=tuple([vm] * (4 * n + 1)),
        out_shape=tuple(jax.ShapeDtypeStruct(a.shape, F32) for a in ws) * 4 + (jax.ShapeDtypeStruct(loss_parts.shape[1:], F32),),
    )(*recvs, *ws, *ms, *vs, loss_parts)


def _inproj_fwd(h, g, wx, tm, name):
    T = h.shape[0]

    def body(h_ref, g_ref, w_ref, o_ref):
        x = h_ref[...]
        u = (x * _rstd(x) * g_ref[...]).astype(BF16)
        o_ref[...] = _dot(u, w_ref[...])

    return pl.pallas_call(
        body, name=name, grid=(T // tm,),
        in_specs=[pl.BlockSpec((tm, D), _row(0)), pl.BlockSpec((1, D), _const2), pl.BlockSpec((D, NX), _const2)],
        out_specs=pl.BlockSpec((tm, NX), _row(0)), out_shape=jax.ShapeDtypeStruct((T, NX), F32),
        compiler_params=_params("parallel"),
    )(h, g, wx)


def _mla_prep_fwd(proj, gq, gkv, wq, wkk, wkv, tq, tk, tm, name):
    T = proj.shape[0]

    def body(cq_ref, ckv_ref, kr_ref, gq_ref, gkv_ref, wq_ref, wkk_ref, wkv_ref, tq_ref, tk_ref, q_ref, k_ref, v_ref):
        cq = cq_ref[...]
        q = _dot((cq * _rstd(cq) * gq_ref[...]).astype(BF16), wq_ref[...])
        tqv = tq_ref[...]
        lane = _lane((1, LANES))
        one = jnp.where(lane == MASK_LANE, 1.0, 0.0)
        for h in range(B_HEADS):
            cs = slice(LANES * h, LANES * (h + 1))
            q_ref[:, cs] = (_rope(q[:, cs], tqv) * (B_SCALE * LOG2E) + one).astype(BF16)
        tkv = tk_ref[...]
        ksh = pltpu.roll(_rope(kr_ref[...], tkv), NOPE, 1) + tkv[3]
        ckv = ckv_ref[...]
        cb = (ckv * _rstd(ckv) * gkv_ref[...]).astype(BF16)
        kn = _dot(cb, wkk_ref[...])
        vn = _dot(cb, wkv_ref[...])
        minus = jnp.where((lane == SUM_LANE) | (lane == SUM_LANE + 1), -1.0, 0.0)
        for h in range(B_HEADS):
            cs = slice(LANES * h, LANES * (h + 1))
            k_ref[:, cs] = (kn[:, cs] + ksh).astype(BF16)
            v_ref[:, cs] = (vn[:, cs] + minus).astype(BF16)

    tab = lambda n: pl.BlockSpec((n, tm, LANES), lambda i: (0, i, 0))
    wide = pl.BlockSpec((tm, 1024), _row(0))
    return pl.pallas_call(
        body, name=name, grid=(T // tm,),
        in_specs=[pl.BlockSpec((tm, 256), _row(CQ0 // 256)), pl.BlockSpec((tm, 128), _row(CKV0 // 128)),
                  pl.BlockSpec((tm, 128), _row(KR0 // 128)), pl.BlockSpec((1, Q_RANK), _const2),
                  pl.BlockSpec((1, KV_RANK), _const2), pl.BlockSpec((Q_RANK, 1024), _const2),
                  pl.BlockSpec((KV_RANK, 1024), _const2), pl.BlockSpec((KV_RANK, 1024), _const2), tab(3), tab(4)],
        out_specs=(wide, wide, wide), out_shape=(jax.ShapeDtypeStruct((T, 1024), BF16),) * 3,
        compiler_params=_params("parallel"),
    )(proj, proj, proj, gq, gkv, wq, wkk, wkv, tq, tk)


SINK_SLOT = 4 * BLK - 1


def _build_bias(buckets, table, sinks, name):
    depth = sinks.shape[0]

    def body(tab_ref, sink_ref, b_ref, *o_refs):
        bidx = b_ref[0]
        slot = _lane(bidx.shape)
        for h in range(A_HEADS):
            acc = jnp.full(bidx.shape, NEG, F32)
            for j in range(N_BUCKETS):
                acc = jnp.where(bidx == j, tab_ref[j, h] * LOG2E, acc)
            for l in range(depth):
                o_refs[l][0, h] = jnp.where(slot == SINK_SLOT, sink_ref[l, h] * LOG2E, acc)

    smem = pl.BlockSpec(memory_space=pltpu.SMEM)
    blk = pl.BlockSpec((1, A_HEADS, BLK, 4 * BLK), lambda i: (i, 0, 0, 0))
    return pl.pallas_call(
        body, name=name, grid=(4,),
        in_specs=[smem, smem, pl.BlockSpec((1, BLK, 4 * BLK), lambda i: (i, 0, 0))],
        out_specs=tuple([blk] * depth), out_shape=tuple([jax.ShapeDtypeStruct((4, A_HEADS, BLK, 4 * BLK), F32)] * depth),
        compiler_params=_params("parallel"),
    )(table, sinks, buckets)


def _win_base(n, NB):
    return jnp.where(n == NB, 0, jnp.clip(n - 1, 0, NB - 3))


def _win_variant(n, NB):
    return jnp.where(n == 0, 0, jnp.where(n < NB - 1, 1, jnp.where(n == NB - 1, 2, 3)))


def _win_kv(k_ref, v_ref, base, S, kvh):
    cs = slice(LANES * kvh, LANES * (kvh + 1))
    k2 = jnp.concatenate([k_ref[pl.ds(base, 3 * BLK), cs], k_ref[S:S + BLK, cs]], axis=0).astype(BF16)
    v2 = jnp.concatenate([v_ref[pl.ds(base, 3 * BLK), cs], v_ref[S:S + BLK, cs]], axis=0)
    lane = _lane(v2.shape)
    v2 = jnp.where(lane < A_DH, v2, jnp.where(lane < A_DH + 2, -1.0, 0.0)).astype(BF16)
    return k2, v2


def _win_fwd(proj, bias, Bl, S, name):
    T = proj.shape[0]
    Lp = S + BLK
    NB = S // BLK
    nblk = Lp // BLK
    G = A_HEADS // A_KV

    def body(q_ref, k_ref, v_ref, b_ref, o_ref, lse_ref):
        n = pl.program_id(1)
        base = pl.multiple_of(_win_base(n, NB) * BLK, BLK)
        lane = _lane((BLK, LANES))
        lo = lane < A_DH
        kv = [_win_kv(k_ref, v_ref, base, S, kvh) for kvh in range(A_KV)]
        scores = []
        for h in range(A_HEADS):
            qp = q_ref[:, LANES * (h // 2):LANES * (h // 2 + 1)]
            qm = (jnp.where(lo if h % 2 == 0 else ~lo, qp, 0.0) * (A_SCALE * LOG2E)).astype(BF16)
            scores.append(_dot_nt(qm, kv[h // G][0]) + b_ref[0, h])
        ms = [jnp.max(s, axis=-1, keepdims=True) for s in scores]
        es = [jnp.exp2(s - m).astype(BF16) for s, m in zip(scores, ms)]
        outs = []
        t = jnp.zeros((BLK, LANES), F32)
        for h in range(A_HEADS):
            o2 = _dot(es[h], kv[h // G][1])
            l = -jnp.sum(jnp.where(lane == A_DH, o2, 0.0), axis=-1, keepdims=True)
            outs.append(o2 * (1.0 / l))
            t = jnp.where(lane == h, ms[h] + jnp.log(l) * LOG2E, t)
        for j in range(A_HEADS // 2):
            o_ref[:, LANES * j:LANES * (j + 1)] = jnp.where(lo, outs[2 * j], pltpu.roll(outs[2 * j + 1], A_DH, 1))
        lse_ref[...] = t

    qrow = lambda b, n: (b * nblk + n, 0)
    return pl.pallas_call(
        body, name=name, grid=(Bl, nblk),
        in_specs=[pl.BlockSpec((BLK, A_W), qrow),
                  pl.BlockSpec((Lp, 256), lambda b, n: (b, KA0 // 256)), pl.BlockSpec((Lp, 256), lambda b, n: (b, VA0 // 256)),
                  pl.BlockSpec((1, A_HEADS, BLK, 4 * BLK), lambda b, n: (_win_variant(n, NB), 0, 0, 0))],
        out_specs=(pl.BlockSpec((BLK, A_W), qrow), pl.BlockSpec((BLK, LANES), qrow)),
        out_shape=(jax.ShapeDtypeStruct((T, A_W), F32), jax.ShapeDtypeStruct((T, LANES), F32)),
        compiler_params=_params("parallel", "arbitrary"),
    )(proj, proj, proj, bias)


def _q_tiles(Lp):
    S = Lp - BLK
    return S // 256, S


def _mla_fwd(q, k, v, Bl, Lp, name):
    T = q.shape[0]
    n_big, S = _q_tiles(Lp)

    def body(q_ref, k_ref, v_ref, o_ref, lse_ref):
        def qtile(r0, nq):
            lane = _lane((nq, LANES))
            outs = []
            t = jnp.zeros((nq, LANES), F32)
            cols = [slice(LANES * hh, LANES * (hh + 1)) for hh in range(2)]
            scores = [_dot_nt(q_ref[pl.ds(r0, nq), cs], k_ref[:, cs]) for cs in cols]
            for hh in range(2):
                cs, s = cols[hh], scores[hh]
                m = jnp.max(s, axis=-1, keepdims=True)
                o2 = _dot(jnp.exp2(s - m).astype(BF16), v_ref[:, cs])
                l = -jnp.sum(jnp.where(lane == SUM_LANE, o2, 0.0), axis=-1, keepdims=True)
                outs.append(o2 * (1.0 / l))
                t = jnp.where(lane == hh, m + jnp.log(l) * LOG2E, t)
            o_ref[pl.ds(r0, nq), :] = jnp.where(lane < B_V, outs[0], pltpu.roll(outs[1], B_V, 1))
            lse_ref[pl.ds(r0, nq), :] = t

        def step(i, c):
            qtile(pl.multiple_of(i * 256, 256), 256)
            return c

        lax.fori_loop(0, n_big, step, 0)
        qtile(S, BLK)

    pair = lambda b, j: (b, j)
    wide, narrow = pl.BlockSpec((Lp, 256), pair), pl.BlockSpec((Lp, LANES), pair)
    return pl.pallas_call(
        body, name=name, grid=(Bl, B_HEADS // 2), in_specs=[wide, wide, wide], out_specs=(narrow, narrow),
        out_shape=(jax.ShapeDtypeStruct((T, B_W), F32), jax.ShapeDtypeStruct((T, B_W), F32)),
        compiler_params=_params("parallel", "parallel"),
    )(q, k, v)


def _gated(y, gate, gain):
    r = _rstd(y)
    nrm = y * r
    sg = jax.nn.sigmoid(gate)
    return r, nrm, sg, nrm * gain, gate * sg


def _out_fwd(ya, yb, proj, na, nb, wout, h, tm, name):
    T = h.shape[0]

    def body(ya_ref, yb_ref, ga_ref, gb_ref, na_ref, nb_ref, w_ref, h_ref, o_ref):
        _, _, _, n_a, sl_a = _gated(ya_ref[...], ga_ref[...], na_ref[...])
        _, _, _, n_b, sl_b = _gated(yb_ref[...], gb_ref[...], nb_ref[...])
        acc = _dot((n_a * sl_a).astype(BF16), w_ref[0:A_W, :]) + _dot((n_b * sl_b).astype(BF16), w_ref[A_W:A_W + B_W, :])
        o_ref[...] = h_ref[...] + acc

    half = pl.BlockSpec((tm, 512), _row(0))
    return pl.pallas_call(
        body, name=name, grid=(T // tm,),
        in_specs=[half, half, pl.BlockSpec((tm, 512), _row(GA0 // 512)), pl.BlockSpec((tm, 512), _row(GB0 // 512)),
                  pl.BlockSpec((1, 512), _const2), pl.BlockSpec((1, 512), _const2), pl.BlockSpec((D, D), _const2),
                  pl.BlockSpec((tm, D), _row(0))],
        out_specs=pl.BlockSpec((tm, D), _row(0)), out_shape=jax.ShapeDtypeStruct((T, D), F32),
        compiler_params=_params("parallel"),
    )(ya, yb, proj, proj, na, nb, wout, h)


def _loss_head(h, gf, target, Bl, S, name):
    T = h.shape[0]
    nblk = (S + BLK) // BLK
    NB = S // BLK

    def body(h_ref, g_ref, t_ref, dh_ref, loss_ref, dg_ref):
        b, n = pl.program_id(0), pl.program_id(1)

        @pl.when((b == 0) & (n == 0))
        def _():
            loss_ref[...] = jnp.zeros_like(loss_ref)
            dg_ref[...] = jnp.zeros_like(dg_ref)

        @pl.when(n < NB)
        def _():
            x = h_ref[...]
            g = g_ref[...]
            r = _rstd(x)
            err = x * r * g - t_ref[0]
            loss_ref[...] += 0.5 * jnp.sum(jnp.mean(err * err, axis=-1, keepdims=True))
            dx, dg = _rms_bwd(err * (1.0 / D), x, r, g)
            dh_ref[...] = dx
            dg_ref[...] += dg

        @pl.when(n >= NB)
        def _():
            dh_ref[...] = jnp.zeros_like(dh_ref)

    return pl.pallas_call(
        body, name=name, grid=(Bl, nblk),
        in_specs=[pl.BlockSpec((BLK, D), lambda b, n: (b * nblk + n, 0)), pl.BlockSpec((1, D), lambda b, n: (0, 0)),
                  pl.BlockSpec((1, BLK, D), lambda b, n: (b, jnp.minimum(n, NB - 1), 0))],
        out_specs=(pl.BlockSpec((BLK, D), lambda b, n: (b * nblk + n, 0)), pl.BlockSpec((8, LANES), lambda b, n: (0, 0)),
                   pl.BlockSpec((1, D), lambda b, n: (0, 0))),
        out_shape=(jax.ShapeDtypeStruct((T, D), F32), jax.ShapeDtypeStruct((8, LANES), F32), jax.ShapeDtypeStruct((1, D), F32)),
        compiler_params=_params("arbitrary", "arbitrary"),
    )(h, gf, target)


def _out_bwd(dh, ya, yb, proj, na, nb, wout, tm, name):
    T = dh.shape[0]

    def body(dh_ref, ya_ref, yb_ref, ga_ref, gb_ref, na_ref, nb_ref, w_ref,
             dya_ref, dyb_ref, dga_ref, dgb_ref, dw_ref, dna_ref, dnb_ref):
        @pl.when(pl.program_id(0) == 0)
        def _():
            dw_ref[...] = jnp.zeros_like(dw_ref)
            dna_ref[...] = jnp.zeros_like(dna_ref)
            dnb_ref[...] = jnp.zeros_like(dnb_ref)

        dhb = dh_ref[...].astype(BF16)
        dy = _dot_nt(dhb, w_ref[...])
        parts = ((ya_ref, ga_ref, na_ref, dya_ref, dga_ref, dna_ref, 0), (yb_ref, gb_ref, nb_ref, dyb_ref, dgb_ref, dnb_ref, A_W))
        for y_ref, gate_ref, gain_ref, dy_out, dgate_out, dgain_out, c0 in parts:
            y, gate, gain = y_ref[...], gate_ref[...], gain_ref[...]
            r, nrm, sg, n_g, sl = _gated(y, gate, gain)
            dyp = dy[:, c0:c0 + 512]
            dgate_out[...] = dyp * n_g * (sg * (1.0 + gate * (1.0 - sg)))
            dx, dgain = _rms_bwd(dyp * sl, y, r, gain)
            dy_out[...] = dx
            dgain_out[...] += dgain
            dw_ref[c0:c0 + 512, :] += _dot_tn((n_g * sl).astype(BF16), dhb)

    half = pl.BlockSpec((tm, 512), _row(0))
    vec = pl.BlockSpec((1, 512), _const2)
    return pl.pallas_call(
        body, name=name, grid=(T // tm,),
        in_specs=[pl.BlockSpec((tm, D), _row(0)), half, half, pl.BlockSpec((tm, 512), _row(GA0 // 512)),
                  pl.BlockSpec((tm, 512), _row(GB0 // 512)), vec, vec, pl.BlockSpec((D, D), _const2)],
        out_specs=(half, half, half, half, pl.BlockSpec((D, D), _const2), vec, vec),
        out_shape=(jax.ShapeDtypeStruct((T, 512), F32),) * 4 + (jax.ShapeDtypeStruct((D, D), F32),)
        + (jax.ShapeDtypeStruct((1, 512), F32),) * 2,
        compiler_params=_params("arbitrary"),
    )(dh, ya, yb, proj, proj, na, nb, wout)


def _mla_bwd(q, k, v, o, do, lse, after, Bl, Lp, name):
    T = q.shape[0]
    n_big, S = _q_tiles(Lp)

    def body(q_ref, k_ref, v_ref, o_ref, do_ref, lse_ref, _, dq_ref, dk_ref, dv_ref):
        dk_ref[...] = jnp.zeros_like(dk_ref)
        dv_ref[...] = jnp.zeros_like(dv_ref)

        def qtile(r0, nq):
            lane = _lane((nq, LANES))
            dop = do_ref[pl.ds(r0, nq), :]
            op = o_ref[pl.ds(r0, nq), :]
            lsev = lse_ref[pl.ds(r0, nq), :]
            for hh in range(2):
                cs = slice(LANES * hh, LANES * (hh + 1))
                dom = jnp.where((lane < B_V) if hh == 0 else (lane >= B_V), dop, 0.0)
                delta = jnp.sum(dom * op, axis=-1, keepdims=True)
                d_hi = delta.astype(BF16).astype(F32)
                x = dom if hh == 0 else pltpu.roll(dom, B_V, 1)
                domx = jnp.where(lane == SUM_LANE, d_hi, jnp.where(lane == SUM_LANE + 1, delta - d_hi, x)).astype(BF16)
                qh, kh, vh = q_ref[pl.ds(r0, nq), cs], k_ref[:, cs], v_ref[:, cs]
                lse_h = jnp.sum(jnp.where(lane == hh, lsev, 0.0), axis=-1, keepdims=True)
                p = jnp.exp2(_dot_nt(qh, kh) - lse_h)
                ds = (p * _dot_nt(domx, vh)).astype(BF16)
                dq_ref[pl.ds(r0, nq), cs] = _dot(ds, kh) * B_SCALE
                dk_ref[:, cs] += _dot_tn(ds, qh)
                dv_ref[:, cs] += _dot_tn(p.astype(BF16), domx)

        def step(i, c):
            qtile(pl.multiple_of(i * 256, 256), 256)
            return c

        lax.fori_loop(0, n_big, step, 0)
        qtile(S, BLK)
        dk_ref[...] = dk_ref[...] * (1.0 / LOG2E)

    pair = lambda b, j: (b, j)
    wide, narrow = pl.BlockSpec((Lp, 256), pair), pl.BlockSpec((Lp, LANES), pair)
    return pl.pallas_call(
        body, name=name, grid=(Bl, B_HEADS // 2),
        in_specs=[wide, wide, wide, narrow, narrow, narrow, pl.BlockSpec(memory_space=pl.ANY)],
        out_specs=(wide, wide, wide), out_shape=(jax.ShapeDtypeStruct((T, 1024), F32),) * 3,
        compiler_params=_params("parallel", "parallel"),
    )(q, k, v, o, do, lse, after)


def _win_bwd(proj, bias, ya, dya, lse, Bl, S, name):
    T = proj.shape[0]
    Lp = S + BLK
    NB = S // BLK
    nblk = Lp // BLK
    G = A_HEADS // A_KV

    def body(q_ref, k_ref, v_ref, b_ref, o_ref, do_ref, lse_ref, dq_ref, dk_ref, dv_ref, db_ref):
        n = pl.program_id(1)

        @pl.when(n == 0)
        def _():
            dk_ref[...] = jnp.zeros_like(dk_ref)
            dv_ref[...] = jnp.zeros_like(dv_ref)

        @pl.when((n == 0) | (n == 1) | (n == NB - 1) | (n == NB))
        def _():
            db_ref[...] = jnp.zeros_like(db_ref)

        base = pl.multiple_of(_win_base(n, NB) * BLK, BLK)
        lane = _lane((BLK, LANES))
        lo = lane < A_DH
        lsev = lse_ref[...]
        kv = [_win_kv(k_ref, v_ref, base, S, kvh) for kvh in range(A_KV)]
        qms, pbs, domxs, dsls = [], [], [], []
        for h in range(A_HEADS):
            k2, v2 = kv[h // G]
            ps = slice(LANES * (h // 2), LANES * (h // 2 + 1))
            qm = (jnp.where(lo if h % 2 == 0 else ~lo, q_ref[:, ps], 0.0) * (A_SCALE * LOG2E)).astype(BF16)
            lse_h = jnp.sum(jnp.where(lane == h, lsev, 0.0), axis=-1, keepdims=True)
            p = jnp.exp2(_dot_nt(qm, k2) + b_ref[0, h] - lse_h)
            dop, op = do_ref[:, ps], o_ref[:, ps]
            if h % 2 == 1:
                dop, op = pltpu.roll(dop, A_DH, 1), pltpu.roll(op, A_DH, 1)
            dom = jnp.where(lo, dop, 0.0)
            delta = jnp.sum(dom * op, axis=-1, keepdims=True)
            d_hi = delta.astype(BF16).astype(F32)
            domx = jnp.where(lane == A_DH, d_hi, jnp.where(lane == A_DH + 1, delta - d_hi, dom)).astype(BF16)
            dsc = p * _dot_nt(domx, v2)
            db_ref[0, 0, h] += dsc
            qms.append(qm)
            pbs.append(p.astype(BF16))
            domxs.append(domx)
            dsls.append(dsc.astype(BF16))
        dqs = [_dot(dsls[h], kv[h // G][0]) * A_SCALE for h in range(A_HEADS)]
        for kvh in range(A_KV):
            cs = slice(LANES * kvh, LANES * (kvh + 1))
            group = range(kvh * G, (kvh + 1) * G)
            stack = lambda parts: jnp.concatenate([parts[h] for h in group], axis=0)
            dk_acc = _dot_tn(stack(dsls), stack(qms)) * (1.0 / LOG2E)
            dv_acc = _dot_tn(stack(pbs), stack(domxs))
            dv_acc = jnp.where(_lane(dv_acc.shape) < A_DH, dv_acc, 0.0)
            dk_ref[pl.ds(base, 3 * BLK), cs] += dk_acc[0:3 * BLK]
            dk_ref[S:S + BLK, cs] += dk_acc[3 * BLK:4 * BLK]
            dv_ref[pl.ds(base, 3 * BLK), cs] += dv_acc[0:3 * BLK]
            dv_ref[S:S + BLK, cs] += dv_acc[3 * BLK:4 * BLK]
        for j in range(A_HEADS // 2):
            dq_ref[:, LANES * j:LANES * (j + 1)] = jnp.where(lo, dqs[2 * j], dqs[2 * j + 1])

    qrow = lambda b, n: (b * nblk + n, 0)
    kvs = pl.BlockSpec((Lp, 256), lambda b, n: (b, 0))
    return pl.pallas_call(
        body, name=name, grid=(Bl, nblk),
        in_specs=[pl.BlockSpec((BLK, A_W), qrow),
                  pl.BlockSpec((Lp, 256), lambda b, n: (b, KA0 // 256)), pl.BlockSpec((Lp, 256), lambda b, n: (b, VA0 // 256)),
                  pl.BlockSpec((1, A_HEADS, BLK, 4 * BLK), lambda b, n: (_win_variant(n, NB), 0, 0, 0)),
                  pl.BlockSpec((BLK, A_W), qrow), pl.BlockSpec((BLK, A_W), qrow), pl.BlockSpec((BLK, LANES), qrow)],
        out_specs=(pl.BlockSpec((BLK, A_W), qrow), kvs, kvs,
                   pl.BlockSpec((1, 1, A_HEADS, BLK, 4 * BLK), lambda b, n: (b, _win_variant(n, NB), 0, 0, 0))),
        out_shape=(jax.ShapeDtypeStruct((T, A_W), F32), jax.ShapeDtypeStruct((T, 256), F32),
                   jax.ShapeDtypeStruct((T, 256), F32), jax.ShapeDtypeStruct((Bl, 4, A_HEADS, BLK, 4 * BLK), F32)),
        compiler_params=_params("parallel", "arbitrary"),
    )(proj, proj, proj, bias, ya, dya, lse)


def _table_grad(dbias_list, buckets, after, name):
    nl = len(dbias_list)
    Bl = dbias_list[0].shape[0]

    def body(*refs):
        d_refs = refs[:nl]
        b_ref, _, o_ref, s_ref, acc, part = refs[nl:]
        v, b = pl.program_id(0), pl.program_id(1)

        @pl.when((v == 0) & (b == 0))
        def _():
            o_ref[...] = jnp.zeros_like(o_ref)
            s_ref[...] = jnp.zeros_like(s_ref)

        slot = _lane((BLK, 4 * BLK))
        for l in range(nl):
            for h in range(A_HEADS):
                col = jnp.sum(jnp.where(slot == SINK_SLOT, d_refs[l][0, 0, h], 0.0), axis=0, keepdims=True)
                s_ref[l, h:h + 1, :] += jnp.sum(col, axis=1, keepdims=True)

        tot = d_refs[0][0, 0]
        for r in d_refs[1:]:
            tot = tot + r[0, 0]

        @pl.when(b == 0)
        def _():
            acc[...] = tot

        @pl.when(b > 0)
        def _():
            acc[...] += tot

        @pl.when(b == Bl - 1)
        def _():
            bidx = b_ref[0]

            def step(j, c):
                mask = bidx == j
                for h in range(A_HEADS):
                    part[h, j] = jnp.sum(jnp.where(mask, acc[h], 0.0).reshape(BLK // 8, 8, 4 * BLK), axis=0)
                return c

            lax.fori_loop(0, N_BUCKETS, step, 0)
            rows = lax.broadcasted_iota(jnp.int32, (N_BUCKETS, LANES), 0)
            lanes = _lane((N_BUCKETS, LANES))
            for h in range(A_HEADS):
                col = jnp.sum(jnp.sum(part[h], axis=1), axis=-1, keepdims=True)
                o_ref[h:h + 1, :] += jnp.sum(jnp.where(rows == lanes, col, 0.0), axis=0, keepdims=True)

    return pl.pallas_call(
        body, name=name, grid=(4, Bl),
        in_specs=[pl.BlockSpec((1, 1, A_HEADS, BLK, 4 * BLK), lambda v, b: (b, v, 0, 0, 0))] * nl
        + [pl.BlockSpec((1, BLK, 4 * BLK), lambda v, b: (v, 0, 0)), pl.BlockSpec(memory_space=pl.ANY)],
        out_specs=(pl.BlockSpec((8, LANES), lambda v, b: (0, 0)), pl.BlockSpec((nl, 8, LANES), lambda v, b: (0, 0, 0))),
        out_shape=(jax.ShapeDtypeStruct((8, LANES), F32), jax.ShapeDtypeStruct((nl, 8, LANES), F32)),
        scratch_shapes=[pltpu.VMEM((A_HEADS, BLK, 4 * BLK), F32), pltpu.VMEM((A_HEADS, N_BUCKETS, 8, 4 * BLK), F32)],
        compiler_params=_params("arbitrary", "arbitrary"),
    )(*dbias_list, buckets, after)


def _mla_prep_bwd(dq, dk, dv, proj, gq, gkv, wq, wkk, wkv, tq, tk, tm, name):
    T = proj.shape[0]

    def body(dq_ref, dk_ref, dv_ref, cq_ref, ckv_ref, gq_ref, gkv_ref, wq_ref, wkk_ref, wkv_ref, tq_ref, tk_ref,
             dcq_ref, dckv_ref, dkr_ref, dwq_ref, dwkk_ref, dwkv_ref, dgq_ref, dgkv_ref):
        @pl.when(pl.program_id(0) == 0)
        def _():
            for r in (dwq_ref, dwkk_ref, dwkv_ref, dgq_ref, dgkv_ref):
                r[...] = jnp.zeros_like(r)

        tqv = tq_ref[...]
        dqp = jnp.concatenate([_rope_t(dq_ref[:, LANES * h:LANES * (h + 1)], tqv) for h in range(B_HEADS)],
                              axis=1).astype(BF16)
        cq, gq_ = cq_ref[...], gq_ref[...]
        rq = _rstd(cq)
        dwq_ref[...] += _dot_tn((cq * rq * gq_).astype(BF16), dqp)
        dx, dg = _rms_bwd(_dot_nt(dqp, wq_ref[...]), cq, rq, gq_)
        dcq_ref[...] = dx
        dgq_ref[...] += dg

        dkv_ = dk_ref[...]
        dks = dkv_[:, 0:LANES]
        for h in range(1, B_HEADS):
            dks = dks + dkv_[:, LANES * h:LANES * (h + 1)]
        dkr_ref[...] = _rope_t(pltpu.roll(dks, NOPE, 1), tk_ref[...])
        dkb = dkv_.astype(BF16)
        dvb = dv_ref[...].astype(BF16)
        ckv, gkv_ = ckv_ref[...], gkv_ref[...]
        rk = _rstd(ckv)
        cb = (ckv * rk * gkv_).astype(BF16)
        dwkk_ref[...] += _dot_tn(cb, dkb)
        dwkv_ref[...] += _dot_tn(cb, dvb)
        dx, dg = _rms_bwd(_dot_nt(dkb, wkk_ref[...]) + _dot_nt(dvb, wkv_ref[...]), ckv, rk, gkv_)
        dckv_ref[...] = dx
        dgkv_ref[...] += dg

    tab = lambda n: pl.BlockSpec((n, tm, LANES), lambda i: (0, i, 0))
    wide = pl.BlockSpec((tm, 1024), _row(0))
    return pl.pallas_call(
        body, name=name, grid=(T // tm,),
        in_specs=[wide, wide, wide, pl.BlockSpec((tm, 256), _row(CQ0 // 256)),
                  pl.BlockSpec((tm, 128), _row(CKV0 // 128)), pl.BlockSpec((1, Q_RANK), _const2),
                  pl.BlockSpec((1, KV_RANK), _const2), pl.BlockSpec((Q_RANK, 1024), _const2),
                  pl.BlockSpec((KV_RANK, 1024), _const2), pl.BlockSpec((KV_RANK, 1024), _const2), tab(3), tab(4)],
        out_specs=(pl.BlockSpec((tm, 256), _row(0)), pl.BlockSpec((tm, 128), _row(0)), pl.BlockSpec((tm, 128), _row(0)),
                   pl.BlockSpec((Q_RANK, 1024), _const2), pl.BlockSpec((KV_RANK, 1024), _const2),
                   pl.BlockSpec((KV_RANK, 1024), _const2), pl.BlockSpec((1, Q_RANK), _const2),
                   pl.BlockSpec((1, KV_RANK), _const2)),
        out_shape=(jax.ShapeDtypeStruct((T, 256), F32), jax.ShapeDtypeStruct((T, 128), F32),
                   jax.ShapeDtypeStruct((T, 128), F32), jax.ShapeDtypeStruct((Q_RANK, 1024), F32),
                   jax.ShapeDtypeStruct((KV_RANK, 1024), F32), jax.ShapeDtypeStruct((KV_RANK, 1024), F32),
                   jax.ShapeDtypeStruct((1, Q_RANK), F32), jax.ShapeDtypeStruct((1, KV_RANK), F32)),
        compiler_params=_params("arbitrary"),
    )(dq, dk, dv, proj, proj, gq, gkv, wq, wkk, wkv, tq, tk)


def _inproj_bwd(pieces, h, g, wx, dh_out, tm, name):
    T = h.shape[0]
    nsteps = T // tm
    CH = 512

    def body(*refs):
        p_refs = refs[:len(pieces)]
        h_ref, g_ref, w_ref, dho_ref, dh_ref, dw_hbm, dg_ref, acc, sem = refs[len(pieces):]
        i = pl.program_id(0)

        @pl.when(i == 0)
        def _():
            acc[...] = jnp.zeros_like(acc)
            dg_ref[...] = jnp.zeros_like(dg_ref)

        dp = jnp.concatenate([r[...].astype(BF16) for r in p_refs], axis=1)
        x, gain = h_ref[...], g_ref[...]
        r = _rstd(x)
        u = (x * r * gain).astype(BF16)
        for c in range(0, NX, CH):
            acc[:, c:c + CH] += _dot_tn(u, dp[:, c:c + CH])
        dx, dg = _rms_bwd(_dot_nt(dp, w_ref[...]), x, r, gain)
        dh_ref[...] = dho_ref[...] + dx
        dg_ref[...] += dg

        @pl.when(i == nsteps - 1)
        def _():
            cp = pltpu.make_async_copy(acc, dw_hbm, sem)
            cp.start()
            cp.wait()

    return pl.pallas_call(
        body, name=name, grid=(nsteps,),
        in_specs=[pl.BlockSpec((tm, p.shape[1]), _row(0)) for p in pieces]
        + [pl.BlockSpec((tm, D), _row(0)), pl.BlockSpec((1, D), _const2), pl.BlockSpec((D, NX), _const2),
           pl.BlockSpec((tm, D), _row(0))],
        out_specs=(pl.BlockSpec((tm, D), _row(0)), pl.BlockSpec(memory_space=pl.ANY), pl.BlockSpec((1, D), _const2)),
        out_shape=(jax.ShapeDtypeStruct((T, D), F32), jax.ShapeDtypeStruct((D, NX), F32), jax.ShapeDtypeStruct((1, D), F32)),
        scratch_shapes=[pltpu.VMEM((D, NX), F32), pltpu.SemaphoreType.DMA(())],
        compiler_params=_params("arbitrary"),
    )(*pieces, h, g, wx, dh_out)


def kernel(x, meta_tokens, rel_bias_table, norm_in, w_in, sink_a, norm_q_lat, w_uq, norm_kv_lat, w_ukv, norm_out_a, norm_out_b, w_out, norm_final, loss_target, m_meta_tokens, m_rel_bias_table, m_norm_in, m_w_in, m_sink_a, m_norm_q_lat, m_w_uq, m_norm_kv_lat, m_w_ukv, m_norm_out_a, m_norm_out_b, m_w_out, m_norm_final, v_meta_tokens, v_rel_bias_table, v_norm_in, v_w_in, v_sink_a, v_norm_q_lat, v_w_uq, v_norm_kv_lat, v_w_ukv, v_norm_out_a, v_norm_out_b, v_w_out, v_norm_final):
    Bl, S, _ = x.shape
    assert S % 256 == 0 and S >= 3 * BLK, "the attention kernels tile the real tokens in 256-row blocks"
    Lp = S + BLK
    T = Bl * Lp
    tm = Lp // 4
    tq = Lp // 8
    depth = w_in.shape[0]
    me = 4 * lax.axis_index("x") + 2 * lax.axis_index("y") + lax.axis_index("c")

    def finish(handles, after, name):
        srcs, lands = _xchg_wait(handles, after, name)
        full = []
        for hd, src, land in zip(handles, srcs, lands):
            own = src if hd[4] == "gather" else lax.dynamic_index_in_dim(src, me, 0, keepdims=False)
            full.append(_own_slot(land, own, me))
        return full

    def weights_of(i):
        return [w_in[i].astype(BF16), w_uq[i].astype(BF16), w_ukv[i].astype(BF16), w_out[i].astype(BF16)]

    def packed(l_in, l_uq, l_ukv, l_out):
        cols = lambda t: jnp.transpose(t, (1, 0, 2)).reshape(t.shape[1], NDEV * t.shape[2])
        wkk, wkv = _pack_w_ukv(cols(l_ukv))
        return _pack_w_in(l_in), _pack_w_uq(cols(l_uq)), wkk, wkv, l_out.reshape(D, D)

    wb = weights_of(0)
    gat_a, tok = _xchg_start([wb[0], meta_tokens], ["gather", "gather"], "gather_start_0a")
    buckets = _bias_buckets(S)
    biases = _build_bias(buckets, _after(rel_bias_table, tok), sink_a, "build_bias")
    tb = 256 if T % 256 == 0 else tq
    tq_tab = jnp.tile(_rope_tables(S, Lp, NOPE, True), (1, Bl, 1))
    key_pad = jnp.where((jnp.arange(Lp) >= S + NMETA)[:, None] & (jnp.arange(LANES) == MASK_LANE)[None, :], NEG, 0.0)
    tk_tab = jnp.tile(jnp.concatenate([_rope_tables(S, Lp, 0, False), key_pad.astype(F32)[None]], axis=0), (1, Bl, 1))
    l_in, l_meta = finish(gat_a, biases[0], "gather_wait_0a")
    gat_b, tok = _xchg_start(wb[1:], ["gather"] * 3, "gather_start_0b", after=l_in)

    meta_f = jnp.transpose(l_meta, (1, 0, 2)).reshape(NMETA, D)
    tail = jnp.concatenate([meta_f, jnp.zeros((BLK - NMETA, D), F32)], axis=0)
    h = jnp.concatenate([x, jnp.broadcast_to(tail[None], (Bl, BLK, D))], axis=1).reshape(T, D)

    saved = []
    w_out_f = [None] * depth
    gat_next = None
    for i in range(depth):
        g_in, g_q, g_kv = norm_in[i][None], norm_q_lat[i][None], norm_kv_lat[i][None]
        g_a, g_b = norm_out_a[i][None], norm_out_b[i][None]
        if i == 0:
            proj = _inproj_fwd(h, _after(g_in, tok), _pack_w_in(l_in), tm,
                               f"inproj_fwd_{i}")
            l_uq, l_ukv, l_out = finish(gat_b, proj, "gather_wait_0b")
            wx, wq, wkk, wkv, w_out_f[i] = packed(l_in, l_uq, l_ukv, l_out)
        else:
            wx, wq, wkk, wkv, w_out_f[i] = packed(*finish(gat_next, h, f"gather_wait_{i}"))
            proj = _inproj_fwd(h, g_in, wx, tm, f"inproj_fwd_{i}")
        if i + 1 < depth:
            gat_next, tok = _xchg_start(weights_of(i + 1), ["gather"] * 4, f"gather_start_{i + 1}", after=w_out_f[i])
            g_q = _after(g_q, tok)
        q, k, v = _mla_prep_fwd(proj, g_q, g_kv, wq, wkk, wkv, tq_tab, tk_tab, tm, f"mla_prep_fwd_{i}")
        ya, lse_a = _win_fwd(proj, biases[i], Bl, S, f"win_fwd_{i}")
        yb, lse_b = _mla_fwd(q, k, v, Bl, Lp, f"mla_fwd_{i}")
        h_new = _out_fwd(ya, yb, proj, g_a, g_b, w_out_f[i], h, tm, f"out_fwd_{i}")
        saved.append((h, proj, q, k, v, ya, lse_a, yb, lse_b, wx, wq, wkk, wkv))
        h = h_new

    dh, loss_acc, dg_final = _loss_head(h, norm_final[None], loss_target, Bl, S, "loss_head")

    g_n_in, g_nq, g_nkv, g_na, g_nb = ([None] * depth for _ in range(5))
    dbias_all, sc_out, sc_rest = [None] * depth, [None] * depth, [None] * depth
    split = lambda t, n: jnp.transpose(t.reshape(t.shape[0], NDEV, n), (1, 0, 2)).astype(BF16)
    tok = None
    for i in reversed(range(depth)):
        h_in, proj, q, k, v, ya, lse_a, yb, lse_b, wx, wq, wkk, wkv = saved[i]
        g_in, g_q, g_kv = norm_in[i][None], norm_q_lat[i][None], norm_kv_lat[i][None]
        g_a, g_b = norm_out_a[i][None], norm_out_b[i][None]
        if tok is not None:
            g_a = _after(g_a, tok)
        dya, dyb, dga, dgb, g_w_out, g_na[i], g_nb[i] = _out_bwd(dh, ya, yb, proj, g_a, g_b, w_out_f[i], tb, f"out_bwd_{i}")
        sc_out[i], tok = _xchg_start([g_w_out.reshape(NDEV, D // NDEV, D).astype(BF16)], ["scatter"], f"scatter_start_{i}a")
        dq, dk, dv = _mla_bwd(q, k, v, yb, dyb, lse_b, tok, Bl, Lp, f"mla_bwd_{i}")
        dqa, dka, dva, dbias = _win_bwd(proj, biases[i], ya, dya, lse_a, Bl, S, f"win_bwd_{i}")
        dcq, dckv, dkr, dwq, dwkk, dwkv, g_nq[i], g_nkv[i] = _mla_prep_bwd(
            dq, dk, dv, proj, g_q, g_kv, wq, wkk, wkv, tq_tab, tk_tab, tm, f"mla_prep_bwd_{i}")
        pieces = (dqa, dka, dva, dga, dcq, dckv, dkr, dgb)
        dbias_all[i] = dbias
        rest = [split(_unpack_w_uq_grad(dwq), 768 // NDEV), split(_unpack_w_ukv_grad(dwkk, dwkv), 1024 // NDEV)]
        dh, dwx, g_n_in[i] = _inproj_bwd(pieces, h_in, g_in, wx, dh, tb, f"inproj_bwd_{i}")
        sc_rest[i], tok = _xchg_start([_unpack_w_in_grad(dwx).astype(BF16)] + rest, ["scatter"] * 3, f"scatter_start_{i}b")

    dh3 = dh.reshape(Bl, Lp, D)
    grad_x = dh3[:, :S]
    g_meta = jnp.transpose(jnp.sum(dh3[:, S:S + NMETA], axis=0).reshape(NMETA, NDEV, D // NDEV), (1, 0, 2))
    dtab, dsink = _table_grad(dbias_all, buckets, tok, "table_grad")
    two_d = lambda a: a.reshape(1, -1) if a.ndim == 1 else a
    small_w = [rel_bias_table, norm_in, sink_a, norm_q_lat, norm_kv_lat, norm_out_a, norm_out_b, norm_final]
    small_m = [m_rel_bias_table, m_norm_in, m_sink_a, m_norm_q_lat, m_norm_kv_lat, m_norm_out_a, m_norm_out_b, m_norm_final]
    small_v = [v_rel_bias_table, v_norm_in, v_sink_a, v_norm_q_lat, v_norm_kv_lat, v_norm_out_a, v_norm_out_b, v_norm_final]
    small_g = [jnp.transpose(dtab[:, :N_BUCKETS]), jnp.concatenate(g_n_in), dsink[:, :, 0], jnp.concatenate(g_nq),
               jnp.concatenate(g_nkv), jnp.concatenate(g_na), jnp.concatenate(g_nb), dg_final]
    sc_small, tok = _xchg_start(small_g + [loss_acc, g_meta], ["gather"] * (len(small_g) + 1) + ["scatter"],
                                "scatter_start_small")

    res = {}

    def update(n, r, w, m, v):
        outs = _adamw_param(r, w, m, v, f"adamw_{n}")
        res[n] = list(outs)
        return outs[0]

    r_out = [finish(sc_out[i], tok, f"scatter_wait_{i}a")[0] for i in range(depth)]
    done = update("w_out", r_out, w_out, m_w_out, v_w_out)
    r_in, r_uq, r_ukv = [None] * depth, [None] * depth, [None] * depth
    for i in reversed(range(depth)):
        r_in[i], r_uq[i], r_ukv[i] = finish(sc_rest[i], grad_x if i > 0 else done, f"scatter_wait_{i}b")
    r_small = finish(sc_small, r_in[0], "scatter_wait_small")
    update("w_in", r_in, w_in, m_w_in, v_w_in)
    update("w_uq", r_uq, w_uq, m_w_uq, v_w_uq)
    update("w_ukv", r_ukv, w_ukv, m_w_ukv, v_w_ukv)
    update("meta_tokens", [r_small[-1]], meta_tokens[None], m_meta_tokens[None], v_meta_tokens[None])
    res["meta_tokens"] = [o[0] for o in res["meta_tokens"]]
    names_small = ["rel_bias_table", "norm_in", "sink_a", "norm_q_lat", "norm_kv_lat", "norm_out_a", "norm_out_b", "norm_final"]
    outs = _adamw_small(r_small[:-2], [two_d(a) for a in small_w], [two_d(a) for a in small_m], [two_d(a) for a in small_v],
                        r_small[-2], "adamw_replicated")
    loss = outs[-1][0, 0]
    ns = len(names_small)
    for j, n in enumerate(names_small):
        res[n] = [outs[kk * ns + j].reshape(small_w[j].shape) for kk in range(4)]
    order = ["meta_tokens", "rel_bias_table", "norm_in", "w_in", "sink_a", "norm_q_lat", "w_uq", "norm_kv_lat", "w_ukv",
             "norm_out_a", "norm_out_b", "w_out", "norm_final"]
    return (loss, grad_x, *[res[n][kk] for kk in range(4) for n in order])
```

```python
import functools
import math

import numpy as np
import jax
import jax.numpy as jnp
from jax import lax
from jax.experimental import pallas as pl
from jax.experimental.pallas import tpu as pltpu

F32, BF16 = jnp.float32, jnp.bfloat16
D = 1024
NMETA = 16
BLK = 128
A_HEADS, A_KV, A_DH, A_W = 8, 2, 64, 512
B_HEADS, NOPE, ROPE, B_V, B_W = 8, 64, 32, 64, 512
Q_RANK, KV_RANK = 256, 128
IN_W = 2208
N_BUCKETS, MAX_DIST = 32, 128
THETA = 10000.0
EPS = 1e-6
NEG = -1e30
A_SCALE = A_DH ** -0.5
B_SCALE = (NOPE + ROPE) ** -0.5
LOG2E = math.log2(math.e)
MASK_LANE = NOPE + ROPE
SUM_LANE = B_V
LANES = 128
NDEV = 8
MESH = pl.DeviceIdType.MESH

NX = 2560
QA0, KA0, VA0, GA0, CQ0, CKV0, KR0, GB0 = 0, 512, 768, 1024, 1536, 1792, 1920, 2048

ADAM_LR, ADAM_B1, ADAM_B2, ADAM_EPS, ADAM_WD, ADAM_STEP = 0.001, 0.9, 0.999, 1e-08, 0.01, 10


def _dot(a, b):
    return jnp.dot(a, b, preferred_element_type=F32)


def _dot_nt(a, b):
    return lax.dot_general(a, b, (((1,), (1,)), ((), ())), preferred_element_type=F32)


def _dot_tn(a, b):
    return lax.dot_general(a, b, (((0,), (0,)), ((), ())), preferred_element_type=F32)


def _lane(shape):
    return lax.broadcasted_iota(jnp.int32, shape, len(shape) - 1)


def _rstd(x):
    return lax.rsqrt(jnp.mean(x * x, axis=-1, keepdims=True) + EPS)


def _rms_bwd(dn, x, r, g):
    z = dn * g
    dx = r * z - x * (r * r * r) * jnp.mean(z * x, axis=-1, keepdims=True)
    return dx, jnp.sum(dn * (x * r), axis=0, keepdims=True)


def _row(i):
    return lambda *ids: (ids[0], i)


def _const2(*ids):
    return (0, 0)


def _params(*sem):
    return pltpu.CompilerParams(dimension_semantics=sem)


SHARD_W = IN_W // NDEV


def _pack_w_in(shards):
    def nat(a, b):
        parts = []
        while a < b:
            p, e = a // SHARD_W, min(b, (a // SHARD_W + 1) * SHARD_W)
            parts.append(shards[p][:, a - p * SHARD_W:e - p * SHARD_W])
            a = e
        return parts
    dup = lambda a: nat(a, a + 64) * 2 + nat(a + 64, a + 128) * 2
    zeros = [jnp.zeros((shards.shape[1], LANES - ROPE), shards.dtype)]
    return jnp.concatenate(nat(0, 512) + dup(512) + dup(640) + nat(768, 1280) + nat(1280, 1536) + nat(1536, 1664)
                           + nat(1664, 1696) + zeros + nat(1696, 2208), axis=1)


_NAT_GROUPS = ((0, 512, QA0), (512, 576, KA0), (576, 640, KA0 + 128), (640, 704, VA0), (704, 768, VA0 + 128),
               (768, 1280, GA0), (1280, 1536, CQ0), (1536, 1664, CKV0), (1664, 1696, KR0), (1696, 2208, GB0))


def _unpack_w_in_grad(g):
    def nat(a, b):
        lo, _, pk = next(grp for grp in _NAT_GROUPS if grp[0] <= a < grp[1])
        t = g[:, pk + a - lo:pk + b - lo]
        return t + g[:, pk + 64 + a - lo:pk + 64 + b - lo] if 512 <= a < 768 else t
    cuts = sorted({c for grp in _NAT_GROUPS for c in grp[:2]} | {p * SHARD_W for p in range(NDEV + 1)})
    shards = [[] for _ in range(NDEV)]
    for a, b in zip(cuts[:-1], cuts[1:]):
        shards[a // SHARD_W].append(nat(a, b))
    return jnp.stack([jnp.concatenate(parts, axis=1) for parts in shards])


def _pack_w_uq(w):
    t = w.reshape(Q_RANK, B_HEADS, NOPE + ROPE)
    t = jnp.concatenate([t, jnp.zeros((Q_RANK, B_HEADS, LANES - NOPE - ROPE), w.dtype)], axis=-1)
    return t.reshape(Q_RANK, B_HEADS * LANES)


def _unpack_w_uq_grad(g):
    return g.reshape(Q_RANK, B_HEADS, LANES)[:, :, :NOPE + ROPE].reshape(Q_RANK, B_HEADS * (NOPE + ROPE))


def _pack_w_ukv(w):
    t = w.reshape(KV_RANK, B_HEADS, NOPE + B_V)
    z = jnp.zeros((KV_RANK, B_HEADS, LANES - NOPE), w.dtype)
    pad = lambda u: jnp.concatenate([u, z], axis=-1).reshape(KV_RANK, B_HEADS * LANES)
    return pad(t[:, :, :NOPE]), pad(t[:, :, NOPE:])


def _unpack_w_ukv_grad(gk, gv):
    head = lambda g: g.reshape(KV_RANK, B_HEADS, LANES)[:, :, :NOPE]
    return jnp.concatenate([head(gk), head(gv)], axis=-1).reshape(KV_RANK, B_HEADS * (NOPE + B_V))


def _t5_bucket(rel):
    nb = N_BUCKETS // 2
    max_exact = nb // 2
    ret = jnp.where(rel > 0, nb, 0)
    n = jnp.abs(rel)
    nf = jnp.maximum(n, 1).astype(F32)
    large = max_exact + (jnp.log(nf / max_exact) / math.log(MAX_DIST / max_exact) * (nb - max_exact)).astype(jnp.int32)
    large = jnp.minimum(large, nb - 1)
    return ret + jnp.where(n < max_exact, n, large)


def _bias_buckets(S):
    q = np.arange(BLK)[:, None]
    k = np.arange(4 * BLK)[None, :]
    is_meta_key = (k >= 3 * BLK) & (k < 3 * BLK + NMETA)
    mi = k - 3 * BLK
    rels, valids = [], []
    for shift in (0, BLK, 2 * BLK):
        rel_real = k - shift - q
        valid_real = (k < 3 * BLK) & (np.abs(rel_real) <= BLK)
        far = -(NMETA + MAX_DIST + BLK)
        rel_meta = (mi - (NMETA + q)) if shift == 0 else np.full_like(k + q, far)
        rels.append(np.where(is_meta_key, rel_meta, rel_real))
        valids.append(valid_real | is_meta_key)
    qm = q < NMETA
    rel_real = NMETA + k - q
    valid_real = (k < BLK) & (np.abs(rel_real) <= BLK) & qm
    rels.append(np.where(is_meta_key, mi - q, rel_real))
    valids.append(valid_real | is_meta_key)
    rel = jnp.asarray(np.stack(rels).astype(np.int32))
    valid = jnp.asarray(np.stack(valids))
    return jnp.where(valid, _t5_bucket(rel), -1).astype(jnp.int32)


def _rope_tables(S, Lp, off, passthrough):
    half = ROPE // 2
    r = np.arange(Lp)
    pos = np.where(r < S, NMETA + r, np.where(r < S + NMETA, r - S, 0)).astype(np.float32)
    freqs = THETA ** (-jnp.arange(half, dtype=F32) / half)
    ang = jnp.asarray(pos)[:, None] * freqs[None, :]
    cos, sin = jnp.cos(ang), jnp.sin(ang)
    z = lambda n: jnp.zeros((Lp, n), F32)
    head = jnp.ones((Lp, off), F32) if passthrough else z(off)
    c = jnp.concatenate([head, cos, cos, z(LANES - off - ROPE)], axis=1)
    s1 = jnp.concatenate([z(off), -sin, z(LANES - off - half)], axis=1)
    s2 = jnp.concatenate([z(off + half), sin, z(LANES - off - ROPE)], axis=1)
    return jnp.stack([c, s1, s2])


def _rope(x, t):
    return x * t[0] + pltpu.roll(x, LANES - 16, 1) * t[1] + pltpu.roll(x, 16, 1) * t[2]


def _rope_t(dy, t):
    return dy * t[0] + pltpu.roll(dy * t[1], 16, 1) + pltpu.roll(dy * t[2], LANES - 16, 1)


def _me_and_peers():
    x, y, c = lax.axis_index("x"), lax.axis_index("y"), lax.axis_index("c")
    flip = lambda v, b: 1 - v if b else v
    peers = [(flip(x, k & 4), flip(y, k & 2), flip(c, k & 1)) for k in range(1, NDEV)]
    return 4 * x + 2 * y + c, peers


def _xchg_start(srcs, kinds, name, after=None):
    n = len(srcs)
    lands = [lax.empty((NDEV,) + s.shape[-2:], s.dtype) for s in srcs]
    extra = [] if after is None else [after]

    def body(*refs):
        src_refs, land_refs = refs[:n], refs[n:2 * n]
        ssems, rsems = refs[2 * n + len(extra):3 * n + len(extra)], refs[3 * n + len(extra):4 * n + len(extra)]
        token = refs[6 * n + len(extra)]
        me, peers = _me_and_peers()
        for it in range(n):
            for k, (px, py, pc) in enumerate(peers):
                src = src_refs[it] if kinds[it] == "gather" else src_refs[it].at[4 * px + 2 * py + pc]
                pltpu.make_async_remote_copy(src_ref=src, dst_ref=land_refs[it].at[me], send_sem=ssems[it].at[k],
                                             recv_sem=rsems[it].at[k], device_id=(px, py, pc), device_id_type=MESH).start()
        token[...] = jnp.zeros_like(token)

    hbm = pl.BlockSpec(memory_space=pltpu.HBM)
    sem = pl.BlockSpec(memory_space=pltpu.SEMAPHORE)
    outs = pl.pallas_call(
        body, name=name,
        out_shape=tuple([pltpu.SemaphoreType.DMA((NDEV - 1,))] * (2 * n) + [pltpu.HBM(a.shape, a.dtype) for a in srcs + lands]
                        + [jax.ShapeDtypeStruct((8, LANES), F32)]),
        in_specs=[hbm] * (2 * n) + [pl.BlockSpec(memory_space=pl.ANY)] * len(extra),
        out_specs=tuple([sem] * (2 * n) + [hbm] * (2 * n) + [pl.BlockSpec(memory_space=pltpu.VMEM)]),
        input_output_aliases={i: 2 * n + i for i in range(2 * n)},
        compiler_params=pltpu.CompilerParams(has_side_effects=pltpu.SideEffectType.DATAFLOW_SIDE_EFFECTING),
    )(*[pltpu.with_memory_space_constraint(a, pltpu.HBM) for a in srcs + lands], *extra)
    handles = [(outs[it], outs[n + it], outs[2 * n + it], outs[3 * n + it], kinds[it]) for it in range(n)]
    return handles, outs[4 * n]


def _xchg_wait(handles, after, name):
    n = len(handles)

    def body(*refs):
        src_refs, land_refs = refs[:n], refs[n:2 * n]
        ssems, rsems = refs[2 * n:3 * n], refs[3 * n:4 * n]
        me, peers = _me_and_peers()
        for it in range(n):
            for k, (px, py, pc) in enumerate(peers):
                src = src_refs[it] if handles[it][4] == "gather" else src_refs[it].at[4 * px + 2 * py + pc]
                cp = pltpu.make_async_remote_copy(src_ref=src, dst_ref=land_refs[it].at[me], send_sem=ssems[it].at[k],
                                                  recv_sem=rsems[it].at[k], device_id=(px, py, pc), device_id_type=MESH)
                cp.wait_send()
                cp.wait_recv()

    hbm = pl.BlockSpec(memory_space=pltpu.HBM)
    sem = pl.BlockSpec(memory_space=pltpu.SEMAPHORE)
    srcs, lands = [h[2] for h in handles], [h[3] for h in handles]
    outs = pl.pallas_call(
        body, name=name, out_shape=tuple(pltpu.HBM(a.shape, a.dtype) for a in srcs + lands),
        in_specs=[hbm] * (2 * n) + [sem] * (2 * n) + [pl.BlockSpec(memory_space=pl.ANY)], out_specs=tuple([hbm] * (2 * n)),
        input_output_aliases={i: i for i in range(2 * n)},
        compiler_params=pltpu.CompilerParams(has_side_effects=pltpu.SideEffectType.DATAFLOW_SIDE_EFFECTING),
    )(*srcs, *lands, *[h[0] for h in handles], *[h[1] for h in handles], after)
    return list(outs[:n]), list(outs[n:])


def _own_slot(land, own, me):
    return lax.dynamic_update_slice(land, own[None].astype(land.dtype), (me, 0, 0))


def _after(x, token):
    return x + token[0, 0]


def _adamw_update(g, w, m, v):
    mn = ADAM_B1 * m + (1.0 - ADAM_B1) * g
    vn = ADAM_B2 * v + (1.0 - ADAM_B2) * (g * g)
    m_hat = mn / (1.0 - ADAM_B1 ** ADAM_STEP)
    v_hat = vn / (1.0 - ADAM_B2 ** ADAM_STEP)
    return -ADAM_LR * (m_hat / (jnp.sqrt(v_hat) + ADAM_EPS) + ADAM_WD * w), mn, vn


def _sum_partials(r_ref):
    g = r_ref[0].astype(F32)
    for p in range(1, NDEV):
        g = g + r_ref[p].astype(F32)
    return g


def _adamw_param(recvs, w, m, v, name):
    depth, r, c = w.shape
    tr = min(r, 256)
    nl = len(recvs)

    def body(*refs):
        r_refs = refs[:nl]
        w_ref, m_ref, v_ref, g_out, d_out, m_out, v_out = refs[nl:]
        for li in range(nl):
            @pl.when(pl.program_id(0) == li)
            def _(li=li):
                g = _sum_partials(r_refs[li])
                g_out[0] = g
                d_out[0], m_out[0], v_out[0] = _adamw_update(g, w_ref[0], m_ref[0], v_ref[0])

    blk = pl.BlockSpec((1, tr, c), lambda l, i: (l, i, 0))
    return pl.pallas_call(
        body, name=name, grid=(depth, r // tr),
        in_specs=[pl.BlockSpec((NDEV, tr, c), lambda l, i: (0, i, 0))] * nl + [blk, blk, blk],
        out_specs=(blk, blk, blk, blk), out_shape=(jax.ShapeDtypeStruct(w.shape, F32),) * 4,
        compiler_params=_params("arbitrary", "arbitrary"),
    )(*recvs, w, m, v)


def _adamw_small(recvs, ws, ms, vs, loss_parts, name):
    n = len(ws)

    def body(*refs):
        r_refs, w_refs, m_refs, v_refs = refs[:n], refs[n:2 * n], refs[2 * n:3 * n], refs[3 * n:4 * n]
        outs = refs[4 * n + 1:]
        for j in range(n):
            g = _sum_partials(r_refs[j])
            outs[j][...] = g
            outs[n + j][...], outs[2 * n + j][...], outs[3 * n + j][...] = _adamw_update(
                g, w_refs[j][...], m_refs[j][...], v_refs[j][...])
        outs[4 * n][...] = _sum_partials(refs[4 * n])

    vm = pl.BlockSpec(memory_space=pltpu.VMEM)
    return pl.pallas_call(
        body, name=name, in_specs=[vm] * (4 * n + 1), out_specs=tuple([vm] * (4 * n + 1)),
        out_shape=tuple(jax.ShapeDtypeStruct(a.shape, F32) for a in ws) * 4 + (jax.ShapeDtypeStruct(loss_parts.shape[1:], F32),),
    )(*recvs, *ws, *ms, *vs, loss_parts)


def _inproj_fwd(h, g, wx, tm, name):
    T = h.shape[0]

    def body(h_ref, g_ref, w_ref, o_ref):
        x = h_ref[...]
        u = (x * _rstd(x) * g_ref[...]).astype(BF16)
        o_ref[...] = _dot(u, w_ref[...])

    return pl.pallas_call(
        body, name=name, grid=(T // tm,),
        in_specs=[pl.BlockSpec((tm, D), _row(0)), pl.BlockSpec((1, D), _const2), pl.BlockSpec((D, NX), _const2)],
        out_specs=pl.BlockSpec((tm, NX), _row(0)), out_shape=jax.ShapeDtypeStruct((T, NX), F32),
        compiler_params=_params("parallel"),
    )(h, g, wx)


def _mla_prep_fwd(proj, gq, gkv, wq, wkk, wkv, tq, tk, tm, name):
    T = proj.shape[0]

    def body(cq_ref, ckv_ref, kr_ref, gq_ref, gkv_ref, wq_ref, wkk_ref, wkv_ref, tq_ref, tk_ref, q_ref, k_ref, v_ref):
        cq = cq_ref[...]
        q = _dot((cq * _rstd(cq) * gq_ref[...]).astype(BF16), wq_ref[...])
        tqv = tq_ref[...]
        lane = _lane((1, LANES))
        one = jnp.where(lane == MASK_LANE, 1.0, 0.0)
        for h in range(B_HEADS):
            cs = slice(LANES * h, LANES * (h + 1))
            q_ref[:, cs] = (_rope(q[:, cs], tqv) * (B_SCALE * LOG2E) + one).astype(BF16)
        tkv = tk_ref[...]
        ksh = pltpu.roll(_rope(kr_ref[...], tkv), NOPE, 1) + tkv[3]
        ckv = ckv_ref[...]
        cb = (ckv * _rstd(ckv) * gkv_ref[...]).astype(BF16)
        kn = _dot(cb, wkk_ref[...])
        vn = _dot(cb, wkv_ref[...])
        minus = jnp.where((lane == SUM_LANE) | (lane == SUM_LANE + 1), -1.0, 0.0)
        for h in range(B_HEADS):
            cs = slice(LANES * h, LANES * (h + 1))
            k_ref[:, cs] = (kn[:, cs] + ksh).astype(BF16)
            v_ref[:, cs] = (vn[:, cs] + minus).astype(BF16)

    tab = lambda n: pl.BlockSpec((n, tm, LANES), lambda i: (0, i, 0))
    wide = pl.BlockSpec((tm, 1024), _row(0))
    return pl.pallas_call(
        body, name=name, grid=(T // tm,),
        in_specs=[pl.BlockSpec((tm, 256), _row(CQ0 // 256)), pl.BlockSpec((tm, 128), _row(CKV0 // 128)),
                  pl.BlockSpec((tm, 128), _row(KR0 // 128)), pl.BlockSpec((1, Q_RANK), _const2),
                  pl.BlockSpec((1, KV_RANK), _const2), pl.BlockSpec((Q_RANK, 1024), _const2),
                  pl.BlockSpec((KV_RANK, 1024), _const2), pl.BlockSpec((KV_RANK, 1024), _const2), tab(3), tab(4)],
        out_specs=(wide, wide, wide), out_shape=(jax.ShapeDtypeStruct((T, 1024), BF16),) * 3,
        compiler_params=_params("parallel"),
    )(proj, proj, proj, gq, gkv, wq, wkk, wkv, tq, tk)


SINK_SLOT = 4 * BLK - 1


def _build_bias(buckets, table, sinks, name):
    depth = sinks.shape[0]

    def body(tab_ref, sink_ref, b_ref, *o_refs):
        bidx = b_ref[0]
        slot = _lane(bidx.shape)
        for h in range(A_HEADS):
            acc = jnp.full(bidx.shape, NEG, F32)
            for j in range(N_BUCKETS):
                acc = jnp.where(bidx == j, tab_ref[j, h] * LOG2E, acc)
            for l in range(depth):
                o_refs[l][0, h] = jnp.where(slot == SINK_SLOT, sink_ref[l, h] * LOG2E, acc)

    smem = pl.BlockSpec(memory_space=pltpu.SMEM)
    blk = pl.BlockSpec((1, A_HEADS, BLK, 4 * BLK), lambda i: (i, 0, 0, 0))
    return pl.pallas_call(
        body, name=name, grid=(4,),
        in_specs=[smem, smem, pl.BlockSpec((1, BLK, 4 * BLK), lambda i: (i, 0, 0))],
        out_specs=tuple([blk] * depth), out_shape=tuple([jax.ShapeDtypeStruct((4, A_HEADS, BLK, 4 * BLK), F32)] * depth),
        compiler_params=_params("parallel"),
    )(table, sinks, buckets)


def _win_base(n, NB):
    return jnp.where(n == NB, 0, jnp.clip(n - 1, 0, NB - 3))


def _win_variant(n, NB):
    return jnp.where(n == 0, 0, jnp.where(n < NB - 1, 1, jnp.where(n == NB - 1, 2, 3)))


def _win_kv(k_ref, v_ref, base, S, kvh):
    cs = slice(LANES * kvh, LANES * (kvh + 1))
    k2 = jnp.concatenate([k_ref[pl.ds(base, 3 * BLK), cs], k_ref[S:S + BLK, cs]], axis=0).astype(BF16)
    v2 = jnp.concatenate([v_ref[pl.ds(base, 3 * BLK), cs], v_ref[S:S + BLK, cs]], axis=0)
    lane = _lane(v2.shape)
    v2 = jnp.where(lane < A_DH, v2, jnp.where(lane < A_DH + 2, -1.0, 0.0)).astype(BF16)
    return k2, v2


def _win_fwd(proj, bias, Bl, S, name):
    T = proj.shape[0]
    Lp = S + BLK
    NB = S // BLK
    nblk = Lp // BLK
    G = A_HEADS // A_KV

    def body(q_ref, k_ref, v_ref, b_ref, o_ref, lse_ref):
        n = pl.program_id(1)
        base = pl.multiple_of(_win_base(n, NB) * BLK, BLK)
        lane = _lane((BLK, LANES))
        lo = lane < A_DH
        kv = [_win_kv(k_ref, v_ref, base, S, kvh) for kvh in range(A_KV)]
        scores = []
        for h in range(A_HEADS):
            qp = q_ref[:, LANES * (h // 2):LANES * (h // 2 + 1)]
            qm = (jnp.where(lo if h % 2 == 0 else ~lo, qp, 0.0) * (A_SCALE * LOG2E)).astype(BF16)
            scores.append(_dot_nt(qm, kv[h // G][0]) + b_ref[0, h])
        ms = [jnp.max(s, axis=-1, keepdims=True) for s in scores]
        es = [jnp.exp2(s - m).astype(BF16) for s, m in zip(scores, ms)]
        outs = []
        t = jnp.zeros((BLK, LANES), F32)
        for h in range(A_HEADS):
            o2 = _dot(es[h], kv[h // G][1])
            l = -jnp.sum(jnp.where(lane == A_DH, o2, 0.0), axis=-1, keepdims=True)
            outs.append(o2 * (1.0 / l))
            t = jnp.where(lane == h, ms[h] + jnp.log(l) * LOG2E, t)
        for j in range(A_HEADS // 2):
            o_ref[:, LANES * j:LANES * (j + 1)] = jnp.where(lo, outs[2 * j], pltpu.roll(outs[2 * j + 1], A_DH, 1))
        lse_ref[...] = t

    qrow = lambda b, n: (b * nblk + n, 0)
    return pl.pallas_call(
        body, name=name, grid=(Bl, nblk),
        in_specs=[pl.BlockSpec((BLK, A_W), qrow),
                  pl.BlockSpec((Lp, 256), lambda b, n: (b, KA0 // 256)), pl.BlockSpec((Lp, 256), lambda b, n: (b, VA0 // 256)),
                  pl.BlockSpec((1, A_HEADS, BLK, 4 * BLK), lambda b, n: (_win_variant(n, NB), 0, 0, 0))],
        out_specs=(pl.BlockSpec((BLK, A_W), qrow), pl.BlockSpec((BLK, LANES), qrow)),
        out_shape=(jax.ShapeDtypeStruct((T, A_W), F32), jax.ShapeDtypeStruct((T, LANES), F32)),
        compiler_params=_params("parallel", "arbitrary"),
    )(proj, proj, proj, bias)


def _q_tiles(Lp):
    S = Lp - BLK
    return S // 256, S


def _mla_fwd(q, k, v, Bl, Lp, name):
    T = q.shape[0]
    n_big, S = _q_tiles(Lp)

    def body(q_ref, k_ref, v_ref, o_ref, lse_ref):
        def qtiles(starts, nq):
            lane = _lane((nq, LANES))
            cols = [slice(LANES * hh, LANES * (hh + 1)) for hh in range(2)]
            scores = [[_dot_nt(q_ref[pl.ds(r0, nq), cs], k_ref[:, cs]) for cs in cols] for r0 in starts]
            for r0, tile_scores in zip(starts, scores):
                outs = []
                t = jnp.zeros((nq, LANES), F32)
                for hh in range(2):
                    cs, s = cols[hh], tile_scores[hh]
                    m = jnp.max(s, axis=-1, keepdims=True)
                    o2 = _dot(jnp.exp2(s - m).astype(BF16), v_ref[:, cs])
                    l = -jnp.sum(jnp.where(lane == SUM_LANE, o2, 0.0), axis=-1, keepdims=True)
                    outs.append(o2 * (1.0 / l))
                    t = jnp.where(lane == hh, m + jnp.log(l) * LOG2E, t)
                o_ref[pl.ds(r0, nq), :] = jnp.where(lane < B_V, outs[0], pltpu.roll(outs[1], B_V, 1))
                lse_ref[pl.ds(r0, nq), :] = t

        def step(i, c):
            qtiles([pl.multiple_of(i * 512, 256), pl.multiple_of(i * 512 + 256, 256)], 256)
            return c

        lax.fori_loop(0, n_big // 2, step, 0)
        if n_big % 2:
            qtiles([(n_big - 1) * 256], 256)
        qtiles([S], BLK)

    pair = lambda b, j: (b, j)
    wide, narrow = pl.BlockSpec((Lp, 256), pair), pl.BlockSpec((Lp, LANES), pair)
    return pl.pallas_call(
        body, name=name, grid=(Bl, B_HEADS // 2), in_specs=[wide, wide, wide], out_specs=(narrow, narrow),
        out_shape=(jax.ShapeDtypeStruct((T, B_W), F32), jax.ShapeDtypeStruct((T, B_W), F32)),
        compiler_params=_params("parallel", "parallel"),
    )(q, k, v)


def _gated(y, gate, gain):
    r = _rstd(y)
    nrm = y * r
    sg = jax.nn.sigmoid(gate)
    return r, nrm, sg, nrm * gain, gate * sg


def _out_fwd(ya, yb, proj, na, nb, wout, h, tm, name):
    T = h.shape[0]

    def body(ya_ref, yb_ref, ga_ref, gb_ref, na_ref, nb_ref, w_ref, h_ref, o_ref):
        _, _, _, n_a, sl_a = _gated(ya_ref[...], ga_ref[...], na_ref[...])
        _, _, _, n_b, sl_b = _gated(yb_ref[...], gb_ref[...], nb_ref[...])
        acc = _dot((n_a * sl_a).astype(BF16), w_ref[0:A_W, :]) + _dot((n_b * sl_b).astype(BF16), w_ref[A_W:A_W + B_W, :])
        o_ref[...] = h_ref[...] + acc

    half = pl.BlockSpec((tm, 512), _row(0))
    return pl.pallas_call(
        body, name=name, grid=(T // tm,),
        in_specs=[half, half, pl.BlockSpec((tm, 512), _row(GA0 // 512)), pl.BlockSpec((tm, 512), _row(GB0 // 512)),
                  pl.BlockSpec((1, 512), _const2), pl.BlockSpec((1, 512), _const2), pl.BlockSpec((D, D), _const2),
                  pl.BlockSpec((tm, D), _row(0))],
        out_specs=pl.BlockSpec((tm, D), _row(0)), out_shape=jax.ShapeDtypeStruct((T, D), F32),
        compiler_params=_params("parallel"),
    )(ya, yb, proj, proj, na, nb, wout, h)


def _loss_head(h, gf, target, Bl, S, name):
    T = h.shape[0]
    nblk = (S + BLK) // BLK
    NB = S // BLK

    def body(h_ref, g_ref, t_ref, dh_ref, loss_ref, dg_ref):
        b, n = pl.program_id(0), pl.program_id(1)

        @pl.when((b == 0) & (n == 0))
        def _():
            loss_ref[...] = jnp.zeros_like(loss_ref)
            dg_ref[...] = jnp.zeros_like(dg_ref)

        @pl.when(n < NB)
        def _():
            x = h_ref[...]
            g = g_ref[...]
            r = _rstd(x)
            err = x * r * g - t_ref[0]
            loss_ref[...] += 0.5 * jnp.sum(jnp.mean(err * err, axis=-1, keepdims=True))
            dx, dg = _rms_bwd(err * (1.0 / D), x, r, g)
            dh_ref[...] = dx
            dg_ref[...] += dg

        @pl.when(n >= NB)
        def _():
            dh_ref[...] = jnp.zeros_like(dh_ref)

    return pl.pallas_call(
        body, name=name, grid=(Bl, nblk),
        in_specs=[pl.BlockSpec((BLK, D), lambda b, n: (b * nblk + n, 0)), pl.BlockSpec((1, D), lambda b, n: (0, 0)),
                  pl.BlockSpec((1, BLK, D), lambda b, n: (b, jnp.minimum(n, NB - 1), 0))],
        out_specs=(pl.BlockSpec((BLK, D), lambda b, n: (b * nblk + n, 0)), pl.BlockSpec((8, LANES), lambda b, n: (0, 0)),
                   pl.BlockSpec((1, D), lambda b, n: (0, 0))),
        out_shape=(jax.ShapeDtypeStruct((T, D), F32), jax.ShapeDtypeStruct((8, LANES), F32), jax.ShapeDtypeStruct((1, D), F32)),
        compiler_params=_params("arbitrary", "arbitrary"),
    )(h, gf, target)


def _out_bwd(dh, ya, yb, proj, na, nb, wout, tm, name):
    T = dh.shape[0]

    def body(dh_ref, ya_ref, yb_ref, ga_ref, gb_ref, na_ref, nb_ref, w_ref,
             dya_ref, dyb_ref, dga_ref, dgb_ref, dw_ref, dna_ref, dnb_ref):
        @pl.when(pl.program_id(0) == 0)
        def _():
            dw_ref[...] = jnp.zeros_like(dw_ref)
            dna_ref[...] = jnp.zeros_like(dna_ref)
            dnb_ref[...] = jnp.zeros_like(dnb_ref)

        dhb = dh_ref[...].astype(BF16)
        dy = _dot_nt(dhb, w_ref[...])
        parts = ((ya_ref, ga_ref, na_ref, dya_ref, dga_ref, dna_ref, 0), (yb_ref, gb_ref, nb_ref, dyb_ref, dgb_ref, dnb_ref, A_W))
        for y_ref, gate_ref, gain_ref, dy_out, dgate_out, dgain_out, c0 in parts:
            y, gate, gain = y_ref[...], gate_ref[...], gain_ref[...]
            r, nrm, sg, n_g, sl = _gated(y, gate, gain)
            dyp = dy[:, c0:c0 + 512]
            dgate_out[...] = dyp * n_g * (sg * (1.0 + gate * (1.0 - sg)))
            dx, dgain = _rms_bwd(dyp * sl, y, r, gain)
            dy_out[...] = dx
            dgain_out[...] += dgain
            dw_ref[c0:c0 + 512, :] += _dot_tn((n_g * sl).astype(BF16), dhb)

    half = pl.BlockSpec((tm, 512), _row(0))
    vec = pl.BlockSpec((1, 512), _const2)
    return pl.pallas_call(
        body, name=name, grid=(T // tm,),
        in_specs=[pl.BlockSpec((tm, D), _row(0)), half, half, pl.BlockSpec((tm, 512), _row(GA0 // 512)),
                  pl.BlockSpec((tm, 512), _row(GB0 // 512)), vec, vec, pl.BlockSpec((D, D), _const2)],
        out_specs=(half, half, half, half, pl.BlockSpec((D, D), _const2), vec, vec),
        out_shape=(jax.ShapeDtypeStruct((T, 512), F32),) * 4 + (jax.ShapeDtypeStruct((D, D), F32),)
        + (jax.ShapeDtypeStruct((1, 512), F32),) * 2,
        compiler_params=_params("arbitrary"),
    )(dh, ya, yb, proj, proj, na, nb, wout)


def _mla_bwd(q, k, v, o, do, lse, after, Bl, Lp, name):
    T = q.shape[0]
    n_big, S = _q_tiles(Lp)

    def body(q_ref, k_ref, v_ref, o_ref, do_ref, lse_ref, _, dq_ref, dk_ref, dv_ref):
        dk_ref[...] = jnp.zeros_like(dk_ref)
        dv_ref[...] = jnp.zeros_like(dv_ref)

        def qtile(r0, nq):
            lane = _lane((nq, LANES))
            dop = do_ref[pl.ds(r0, nq), :]
            op = o_ref[pl.ds(r0, nq), :]
            lsev = lse_ref[pl.ds(r0, nq), :]
            for hh in range(2):
                cs = slice(LANES * hh, LANES * (hh + 1))
                dom = jnp.where((lane < B_V) if hh == 0 else (lane >= B_V), dop, 0.0)
                delta = jnp.sum(dom * op, axis=-1, keepdims=True)
                d_hi = delta.astype(BF16).astype(F32)
                x = dom if hh == 0 else pltpu.roll(dom, B_V, 1)
                domx = jnp.where(lane == SUM_LANE, d_hi, jnp.where(lane == SUM_LANE + 1, delta - d_hi, x)).astype(BF16)
                qh, kh, vh = q_ref[pl.ds(r0, nq), cs], k_ref[:, cs], v_ref[:, cs]
                lse_h = jnp.sum(jnp.where(lane == hh, lsev, 0.0), axis=-1, keepdims=True)
                p = jnp.exp2(_dot_nt(qh, kh) - lse_h)
                ds = (p * _dot_nt(domx, vh)).astype(BF16)
                dq_ref[pl.ds(r0, nq), cs] = _dot(ds, kh) * B_SCALE
                dk_ref[:, cs] += _dot_tn(ds, qh)
                dv_ref[:, cs] += _dot_tn(p.astype(BF16), domx)

        def step(i, c):
            qtile(pl.multiple_of(i * 256, 256), 256)
            return c

        lax.fori_loop(0, n_big, step, 0)
        qtile(S, BLK)
        dk_ref[...] = dk_ref[...] * (1.0 / LOG2E)

    pair = lambda b, j: (b, j)
    wide, narrow = pl.BlockSpec((Lp, 256), pair), pl.BlockSpec((Lp, LANES), pair)
    return pl.pallas_call(
        body, name=name, grid=(Bl, B_HEADS // 2),
        in_specs=[wide, wide, wide, narrow, narrow, narrow, pl.BlockSpec(memory_space=pl.ANY)],
        out_specs=(wide, wide, wide), out_shape=(jax.ShapeDtypeStruct((T, 1024), F32),) * 3,
        compiler_params=_params("parallel", "parallel"),
    )(q, k, v, o, do, lse, after)


def _win_bwd(proj, bias, ya, dya, lse, Bl, S, name):
    T = proj.shape[0]
    Lp = S + BLK
    NB = S // BLK
    nblk = Lp // BLK
    G = A_HEADS // A_KV

    def body(q_ref, k_ref, v_ref, b_ref, o_ref, do_ref, lse_ref, dq_ref, dk_ref, dv_ref, db_ref):
        n = pl.program_id(1)

        @pl.when(n == 0)
        def _():
            dk_ref[...] = jnp.zeros_like(dk_ref)
            dv_ref[...] = jnp.zeros_like(dv_ref)

        @pl.when((n == 0) | (n == 1) | (n == NB - 1) | (n == NB))
        def _():
            db_ref[...] = jnp.zeros_like(db_ref)

        base = pl.multiple_of(_win_base(n, NB) * BLK, BLK)
        lane = _lane((BLK, LANES))
        lo = lane < A_DH
        lsev = lse_ref[...]
        kv = [_win_kv(k_ref, v_ref, base, S, kvh) for kvh in range(A_KV)]
        qms, pbs, domxs, dsls = [], [], [], []
        for h in range(A_HEADS):
            k2, v2 = kv[h // G]
            ps = slice(LANES * (h // 2), LANES * (h // 2 + 1))
            qm = (jnp.where(lo if h % 2 == 0 else ~lo, q_ref[:, ps], 0.0) * (A_SCALE * LOG2E)).astype(BF16)
            lse_h = jnp.sum(jnp.where(lane == h, lsev, 0.0), axis=-1, keepdims=True)
            p = jnp.exp2(_dot_nt(qm, k2) + b_ref[0, h] - lse_h)
            dop, op = do_ref[:, ps], o_ref[:, ps]
            if h % 2 == 1:
                dop, op = pltpu.roll(dop, A_DH, 1), pltpu.roll(op, A_DH, 1)
            dom = jnp.where(lo, dop, 0.0)
            delta = jnp.sum(dom * op, axis=-1, keepdims=True)
            d_hi = delta.astype(BF16).astype(F32)
            domx = jnp.where(lane == A_DH, d_hi, jnp.where(lane == A_DH + 1, delta - d_hi, dom)).astype(BF16)
            dsc = p * _dot_nt(domx, v2)
            db_ref[0, 0, h] += dsc
            qms.append(qm)
            pbs.append(p.astype(BF16))
            domxs.append(domx)
            dsls.append(dsc.astype(BF16))
        dqs = [_dot(dsls[h], kv[h // G][0]) * A_SCALE for h in range(A_HEADS)]
        for kvh in range(A_KV):
            cs = slice(LANES * kvh, LANES * (kvh + 1))
            group = range(kvh * G, (kvh + 1) * G)
            stack = lambda parts: jnp.concatenate([parts[h] for h in group], axis=0)
            dk_acc = _dot_tn(stack(dsls), stack(qms)) * (1.0 / LOG2E)
            dv_acc = _dot_tn(stack(pbs), stack(domxs))
            dv_acc = jnp.where(_lane(dv_acc.shape) < A_DH, dv_acc, 0.0)
            dk_ref[pl.ds(base, 3 * BLK), cs] += dk_acc[0:3 * BLK]
            dk_ref[S:S + BLK, cs] += dk_acc[3 * BLK:4 * BLK]
            dv_ref[pl.ds(base, 3 * BLK), cs] += dv_acc[0:3 * BLK]
            dv_ref[S:S + BLK, cs] += dv_acc[3 * BLK:4 * BLK]
        for j in range(A_HEADS // 2):
            dq_ref[:, LANES * j:LANES * (j + 1)] = jnp.where(lo, dqs[2 * j], dqs[2 * j + 1])

    qrow = lambda b, n: (b * nblk + n, 0)
    kvs = pl.BlockSpec((Lp, 256), lambda b, n: (b, 0))
    return pl.pallas_call(
        body, name=name, grid=(Bl, nblk),
        in_specs=[pl.BlockSpec((BLK, A_W), qrow),
                  pl.BlockSpec((Lp, 256), lambda b, n: (b, KA0 // 256)), pl.BlockSpec((Lp, 256), lambda b, n: (b, VA0 // 256)),
                  pl.BlockSpec((1, A_HEADS, BLK, 4 * BLK), lambda b, n: (_win_variant(n, NB), 0, 0, 0)),
                  pl.BlockSpec((BLK, A_W), qrow), pl.BlockSpec((BLK, A_W), qrow), pl.BlockSpec((BLK, LANES), qrow)],
        out_specs=(pl.BlockSpec((BLK, A_W), qrow), kvs, kvs,
                   pl.BlockSpec((1, 1, A_HEADS, BLK, 4 * BLK), lambda b, n: (b, _win_variant(n, NB), 0, 0, 0))),
        out_shape=(jax.ShapeDtypeStruct((T, A_W), F32), jax.ShapeDtypeStruct((T, 256), F32),
                   jax.ShapeDtypeStruct((T, 256), F32), jax.ShapeDtypeStruct((Bl, 4, A_HEADS, BLK, 4 * BLK), F32)),
        compiler_params=_params("parallel", "arbitrary"),
    )(proj, proj, proj, bias, ya, dya, lse)


def _table_grad(dbias_list, buckets, after, name):
    nl = len(dbias_list)
    Bl = dbias_list[0].shape[0]

    def body(*refs):
        d_refs = refs[:nl]
        b_ref, _, o_ref, s_ref, acc, part = refs[nl:]
        v, b = pl.program_id(0), pl.program_id(1)

        @pl.when((v == 0) & (b == 0))
        def _():
            o_ref[...] = jnp.zeros_like(o_ref)
            s_ref[...] = jnp.zeros_like(s_ref)

        slot = _lane((BLK, 4 * BLK))
        for l in range(nl):
            for h in range(A_HEADS):
                col = jnp.sum(jnp.where(slot == SINK_SLOT, d_refs[l][0, 0, h], 0.0), axis=0, keepdims=True)
                s_ref[l, h:h + 1, :] += jnp.sum(col, axis=1, keepdims=True)

        tot = d_refs[0][0, 0]
        for r in d_refs[1:]:
            tot = tot + r[0, 0]

        @pl.when(b == 0)
        def _():
            acc[...] = tot

        @pl.when(b > 0)
        def _():
            acc[...] += tot

        @pl.when(b == Bl - 1)
        def _():
            bidx = b_ref[0]

            def step(j, c):
                mask = bidx == j
                for h in range(A_HEADS):
                    part[h, j] = jnp.sum(jnp.where(mask, acc[h], 0.0).reshape(BLK // 8, 8, 4 * BLK), axis=0)
                return c

            lax.fori_loop(0, N_BUCKETS, step, 0)
            rows = lax.broadcasted_iota(jnp.int32, (N_BUCKETS, LANES), 0)
            lanes = _lane((N_BUCKETS, LANES))
            for h in range(A_HEADS):
                col = jnp.sum(jnp.sum(part[h], axis=1), axis=-1, keepdims=True)
                o_ref[h:h + 1, :] += jnp.sum(jnp.where(rows == lanes, col, 0.0), axis=0, keepdims=True)

    return pl.pallas_call(
        body, name=name, grid=(4, Bl),
        in_specs=[pl.BlockSpec((1, 1, A_HEADS, BLK, 4 * BLK), lambda v, b: (b, v, 0, 0, 0))] * nl
        + [pl.BlockSpec((1, BLK, 4 * BLK), lambda v, b: (v, 0, 0)), pl.BlockSpec(memory_space=pl.ANY)],
        out_specs=(pl.BlockSpec((8, LANES), lambda v, b: (0, 0)), pl.BlockSpec((nl, 8, LANES), lambda v, b: (0, 0, 0))),
        out_shape=(jax.ShapeDtypeStruct((8, LANES), F32), jax.ShapeDtypeStruct((nl, 8, LANES), F32)),
        scratch_shapes=[pltpu.VMEM((A_HEADS, BLK, 4 * BLK), F32), pltpu.VMEM((A_HEADS, N_BUCKETS, 8, 4 * BLK), F32)],
        compiler_params=_params("arbitrary", "arbitrary"),
    )(*dbias_list, buckets, after)


def _mla_prep_bwd(dq, dk, dv, proj, gq, gkv, wq, wkk, wkv, tq, tk, tm, name):
    T = proj.shape[0]

    def body(dq_ref, dk_ref, dv_ref, cq_ref, ckv_ref, gq_ref, gkv_ref, wq_ref, wkk_ref, wkv_ref, tq_ref, tk_ref,
             dcq_ref, dckv_ref, dkr_ref, dwq_ref, dwkk_ref, dwkv_ref, dgq_ref, dgkv_ref):
        @pl.when(pl.program_id(0) == 0)
        def _():
            for r in (dwq_ref, dwkk_ref, dwkv_ref, dgq_ref, dgkv_ref):
                r[...] = jnp.zeros_like(r)

        tqv = tq_ref[...]
        dqp = jnp.concatenate([_rope_t(dq_ref[:, LANES * h:LANES * (h + 1)], tqv) for h in range(B_HEADS)],
                              axis=1).astype(BF16)
        cq, gq_ = cq_ref[...], gq_ref[...]
        rq = _rstd(cq)
        dwq_ref[...] += _dot_tn((cq * rq * gq_).astype(BF16), dqp)
        dx, dg = _rms_bwd(_dot_nt(dqp, wq_ref[...]), cq, rq, gq_)
        dcq_ref[...] = dx
        dgq_ref[...] += dg

        dkv_ = dk_ref[...]
        dks = dkv_[:, 0:LANES]
        for h in range(1, B_HEADS):
            dks = dks + dkv_[:, LANES * h:LANES * (h + 1)]
        dkr_ref[...] = _rope_t(pltpu.roll(dks, NOPE, 1), tk_ref[...])
        dkb = dkv_.astype(BF16)
        dvb = dv_ref[...].astype(BF16)
        ckv, gkv_ = ckv_ref[...], gkv_ref[...]
        rk = _rstd(ckv)
        cb = (ckv * rk * gkv_).astype(BF16)
        dwkk_ref[...] += _dot_tn(cb, dkb)
        dwkv_ref[...] += _dot_tn(cb, dvb)
        dx, dg = _rms_bwd(_dot_nt(dkb, wkk_ref[...]) + _dot_nt(dvb, wkv_ref[...]), ckv, rk, gkv_)
        dckv_ref[...] = dx
        dgkv_ref[...] += dg

    tab = lambda n: pl.BlockSpec((n, tm, LANES), lambda i: (0, i, 0))
    wide = pl.BlockSpec((tm, 1024), _row(0))
    return pl.pallas_call(
        body, name=name, grid=(T // tm,),
        in_specs=[wide, wide, wide, pl.BlockSpec((tm, 256), _row(CQ0 // 256)),
                  pl.BlockSpec((tm, 128), _row(CKV0 // 128)), pl.BlockSpec((1, Q_RANK), _const2),
                  pl.BlockSpec((1, KV_RANK), _const2), pl.BlockSpec((Q_RANK, 1024), _const2),
                  pl.BlockSpec((KV_RANK, 1024), _const2), pl.BlockSpec((KV_RANK, 1024), _const2), tab(3), tab(4)],
        out_specs=(pl.BlockSpec((tm, 256), _row(0)), pl.BlockSpec((tm, 128), _row(0)), pl.BlockSpec((tm, 128), _row(0)),
                   pl.BlockSpec((Q_RANK, 1024), _const2), pl.BlockSpec((KV_RANK, 1024), _const2),
                   pl.BlockSpec((KV_RANK, 1024), _const2), pl.BlockSpec((1, Q_RANK), _const2),
                   pl.BlockSpec((1, KV_RANK), _const2)),
        out_shape=(jax.ShapeDtypeStruct((T, 256), F32), jax.ShapeDtypeStruct((T, 128), F32),
                   jax.ShapeDtypeStruct((T, 128), F32), jax.ShapeDtypeStruct((Q_RANK, 1024), F32),
                   jax.ShapeDtypeStruct((KV_RANK, 1024), F32), jax.ShapeDtypeStruct((KV_RANK, 1024), F32),
                   jax.ShapeDtypeStruct((1, Q_RANK), F32), jax.ShapeDtypeStruct((1, KV_RANK), F32)),
        compiler_params=_params("arbitrary"),
    )(dq, dk, dv, proj, proj, gq, gkv, wq, wkk, wkv, tq, tk)


def _inproj_bwd(pieces, h, g, wx, dh_out, tm, name):
    T = h.shape[0]
    nsteps = T // tm
    CH = 512

    def body(*refs):
        p_refs = refs[:len(pieces)]
        h_ref, g_ref, w_ref, dho_ref, dh_ref, dw_hbm, dg_ref, acc, sem = refs[len(pieces):]
        i = pl.program_id(0)

        @pl.when(i == 0)
        def _():
            acc[...] = jnp.zeros_like(acc)
            dg_ref[...] = jnp.zeros_like(dg_ref)

        dp = jnp.concatenate([r[...].astype(BF16) for r in p_refs], axis=1)
        x, gain = h_ref[...], g_ref[...]
        r = _rstd(x)
        u = (x * r * gain).astype(BF16)
        for c in range(0, NX, CH):
            acc[:, c:c + CH] += _dot_tn(u, dp[:, c:c + CH])
        dx, dg = _rms_bwd(_dot_nt(dp, w_ref[...]), x, r, gain)
        dh_ref[...] = dho_ref[...] + dx
        dg_ref[...] += dg

        @pl.when(i == nsteps - 1)
        def _():
            cp = pltpu.make_async_copy(acc, dw_hbm, sem)
            cp.start()
            cp.wait()

    return pl.pallas_call(
        body, name=name, grid=(nsteps,),
        in_specs=[pl.BlockSpec((tm, p.shape[1]), _row(0)) for p in pieces]
        + [pl.BlockSpec((tm, D), _row(0)), pl.BlockSpec((1, D), _const2), pl.BlockSpec((D, NX), _const2),
           pl.BlockSpec((tm, D), _row(0))],
        out_specs=(pl.BlockSpec((tm, D), _row(0)), pl.BlockSpec(memory_space=pl.ANY), pl.BlockSpec((1, D), _const2)),
        out_shape=(jax.ShapeDtypeStruct((T, D), F32), jax.ShapeDtypeStruct((D, NX), F32), jax.ShapeDtypeStruct((1, D), F32)),
        scratch_shapes=[pltpu.VMEM((D, NX), F32), pltpu.SemaphoreType.DMA(())],
        compiler_params=_params("arbitrary"),
    )(*pieces, h, g, wx, dh_out)


def kernel(x, meta_tokens, rel_bias_table, norm_in, w_in, sink_a, norm_q_lat, w_uq, norm_kv_lat, w_ukv, norm_out_a, norm_out_b, w_out, norm_final, loss_target, m_meta_tokens, m_rel_bias_table, m_norm_in, m_w_in, m_sink_a, m_norm_q_lat, m_w_uq, m_norm_kv_lat, m_w_ukv, m_norm_out_a, m_norm_out_b, m_w_out, m_norm_final, v_meta_tokens, v_rel_bias_table, v_norm_in, v_w_in, v_sink_a, v_norm_q_lat, v_w_uq, v_norm_kv_lat, v_w_ukv, v_norm_out_a, v_norm_out_b, v_w_out, v_norm_final):
    Bl, S, _ = x.shape
    assert S % 256 == 0 and S >= 3 * BLK, "the attention kernels tile the real tokens in 256-row blocks"
    Lp = S + BLK
    T = Bl * Lp
    tm = Lp // 4
    tq = Lp // 8
    depth = w_in.shape[0]
    me = 4 * lax.axis_index("x") + 2 * lax.axis_index("y") + lax.axis_index("c")

    def finish(handles, after, name):
        srcs, lands = _xchg_wait(handles, after, name)
        full = []
        for hd, src, land in zip(handles, srcs, lands):
            own = src if hd[4] == "gather" else lax.dynamic_index_in_dim(src, me, 0, keepdims=False)
            full.append(_own_slot(land, own, me))
        return full

    def weights_of(i):
        return [w_in[i].astype(BF16), w_uq[i].astype(BF16), w_ukv[i].astype(BF16), w_out[i].astype(BF16)]

    def packed(l_in, l_uq, l_ukv, l_out):
        cols = lambda t: jnp.transpose(t, (1, 0, 2)).reshape(t.shape[1], NDEV * t.shape[2])
        wkk, wkv = _pack_w_ukv(cols(l_ukv))
        return _pack_w_in(l_in), _pack_w_uq(cols(l_uq)), wkk, wkv, l_out.reshape(D, D)

    wb = weights_of(0)
    gat_a, tok = _xchg_start([wb[0], meta_tokens], ["gather", "gather"], "gather_start_0a")
    buckets = _bias_buckets(S)
    biases = _build_bias(buckets, _after(rel_bias_table, tok), sink_a, "build_bias")
    tb = 256 if T % 256 == 0 else tq
    tq_tab = jnp.tile(_rope_tables(S, Lp, NOPE, True), (1, Bl, 1))
    key_pad = jnp.where((jnp.arange(Lp) >= S + NMETA)[:, None] & (jnp.arange(LANES) == MASK_LANE)[None, :], NEG, 0.0)
    tk_tab = jnp.tile(jnp.concatenate([_rope_tables(S, Lp, 0, False), key_pad.astype(F32)[None]], axis=0), (1, Bl, 1))
    l_in, l_meta = finish(gat_a, biases[0], "gather_wait_0a")
    gat_b, tok = _xchg_start(wb[1:], ["gather"] * 3, "gather_start_0b", after=l_in)

    meta_f = jnp.transpose(l_meta, (1, 0, 2)).reshape(NMETA, D)
    tail = jnp.concatenate([meta_f, jnp.zeros((BLK - NMETA, D), F32)], axis=0)
    h = jnp.concatenate([x, jnp.broadcast_to(tail[None], (Bl, BLK, D))], axis=1).reshape(T, D)

    saved = []
    w_out_f = [None] * depth
    gat_next = None
    for i in range(depth):
        g_in, g_q, g_kv = norm_in[i][None], norm_q_lat[i][None], norm_kv_lat[i][None]
        g_a, g_b = norm_out_a[i][None], norm_out_b[i][None]
        if i == 0:
            proj = _inproj_fwd(h, _after(g_in, tok), _pack_w_in(l_in), tm,
                               f"inproj_fwd_{i}")
            l_uq, l_ukv, l_out = finish(gat_b, proj, "gather_wait_0b")
            wx, wq, wkk, wkv, w_out_f[i] = packed(l_in, l_uq, l_ukv, l_out)
        else:
            wx, wq, wkk, wkv, w_out_f[i] = packed(*finish(gat_next, h, f"gather_wait_{i}"))
            proj = _inproj_fwd(h, g_in, wx, tm, f"inproj_fwd_{i}")
        if i + 1 < depth:
            gat_next, tok = _xchg_start(weights_of(i + 1), ["gather"] * 4, f"gather_start_{i + 1}", after=w_out_f[i])
            g_q = _after(g_q, tok)
        q, k, v = _mla_prep_fwd(proj, g_q, g_kv, wq, wkk, wkv, tq_tab, tk_tab, tm, f"mla_prep_fwd_{i}")
        ya, lse_a = _win_fwd(proj, biases[i], Bl, S, f"win_fwd_{i}")
        yb, lse_b = _mla_fwd(q, k, v, Bl, Lp, f"mla_fwd_{i}")
        h_new = _out_fwd(ya, yb, proj, g_a, g_b, w_out_f[i], h, tm, f"out_fwd_{i}")
        saved.append((h, proj, q, k, v, ya, lse_a, yb, lse_b, wx, wq, wkk, wkv))
        h = h_new

    dh, loss_acc, dg_final = _loss_head(h, norm_final[None], loss_target, Bl, S, "loss_head")

    g_n_in, g_nq, g_nkv, g_na, g_nb = ([None] * depth for _ in range(5))
    dbias_all, sc_out, sc_rest = [None] * depth, [None] * depth, [None] * depth
    split = lambda t, n: jnp.transpose(t.reshape(t.shape[0], NDEV, n), (1, 0, 2)).astype(BF16)
    tok = None
    for i in reversed(range(depth)):
        h_in, proj, q, k, v, ya, lse_a, yb, lse_b, wx, wq, wkk, wkv = saved[i]
        g_in, g_q, g_kv = norm_in[i][None], norm_q_lat[i][None], norm_kv_lat[i][None]
        g_a, g_b = norm_out_a[i][None], norm_out_b[i][None]
        if tok is not None:
            g_a = _after(g_a, tok)
        dya, dyb, dga, dgb, g_w_out, g_na[i], g_nb[i] = _out_bwd(dh, ya, yb, proj, g_a, g_b, w_out_f[i], tb, f"out_bwd_{i}")
        sc_out[i], tok = _xchg_start([g_w_out.reshape(NDEV, D // NDEV, D).astype(BF16)], ["scatter"], f"scatter_start_{i}a")
        dq, dk, dv = _mla_bwd(q, k, v, yb, dyb, lse_b, tok, Bl, Lp, f"mla_bwd_{i}")
        dqa, dka, dva, dbias = _win_bwd(proj, biases[i], ya, dya, lse_a, Bl, S, f"win_bwd_{i}")
        dcq, dckv, dkr, dwq, dwkk, dwkv, g_nq[i], g_nkv[i] = _mla_prep_bwd(
            dq, dk, dv, proj, g_q, g_kv, wq, wkk, wkv, tq_tab, tk_tab, tm, f"mla_prep_bwd_{i}")
        pieces = (dqa, dka, dva, dga, dcq, dckv, dkr, dgb)
        dbias_all[i] = dbias
        rest = [split(_unpack_w_uq_grad(dwq), 768 // NDEV), split(_unpack_w_ukv_grad(dwkk, dwkv), 1024 // NDEV)]
        dh, dwx, g_n_in[i] = _inproj_bwd(pieces, h_in, g_in, wx, dh, tb, f"inproj_bwd_{i}")
        sc_rest[i], tok = _xchg_start([_unpack_w_in_grad(dwx).astype(BF16)] + rest, ["scatter"] * 3, f"scatter_start_{i}b")

    dh3 = dh.reshape(Bl, Lp, D)
    grad_x = dh3[:, :S]
    g_meta = jnp.transpose(jnp.sum(dh3[:, S:S + NMETA], axis=0).reshape(NMETA, NDEV, D // NDEV), (1, 0, 2))
    dtab, dsink = _table_grad(dbias_all, buckets, tok, "table_grad")
    two_d = lambda a: a.reshape(1, -1) if a.ndim == 1 else a
    small_w = [rel_bias_table, norm_in, sink_a, norm_q_lat, norm_kv_lat, norm_out_a, norm_out_b, norm_final]
    small_m = [m_rel_bias_table, m_norm_in, m_sink_a, m_norm_q_lat, m_norm_kv_lat, m_norm_out_a, m_norm_out_b, m_norm_final]
    small_v = [v_rel_bias_table, v_norm_in, v_sink_a, v_norm_q_lat, v_norm_kv_lat, v_norm_out_a, v_norm_out_b, v_norm_final]
    small_g = [jnp.transpose(dtab[:, :N_BUCKETS]), jnp.concatenate(g_n_in), dsink[:, :, 0], jnp.concatenate(g_nq),
               jnp.concatenate(g_nkv), jnp.concatenate(g_na), jnp.concatenate(g_nb), dg_final]
    sc_small, tok = _xchg_start(small_g + [loss_acc, g_meta], ["gather"] * (len(small_g) + 1) + ["scatter"],
                                "scatter_start_small")

    res = {}

    def update(n, r, w, m, v):
        outs = _adamw_param(r, w, m, v, f"adamw_{n}")
        res[n] = list(outs)
        return outs[0]

    r_out = [finish(sc_out[i], tok, f"scatter_wait_{i}a")[0] for i in range(depth)]
    done = update("w_out", r_out, w_out, m_w_out, v_w_out)
    r_in, r_uq, r_ukv = [None] * depth, [None] * depth, [None] * depth
    for i in reversed(range(depth)):
        r_in[i], r_uq[i], r_ukv[i] = finish(sc_rest[i], grad_x if i > 0 else done, f"scatter_wait_{i}b")
    r_small = finish(sc_small, r_in[0], "scatter_wait_small")
    update("w_in", r_in, w_in, m_w_in, v_w_in)
    update("w_uq", r_uq, w_uq, m_w_uq, v_w_uq)
    update("w_ukv", r_ukv, w_ukv, m_w_ukv, v_w_ukv)
    update("meta_tokens", [r_small[-1]], meta_tokens[None], m_meta_tokens[None], v_meta_tokens[None])
    res["meta_tokens"] = [o[0] for o in res["meta_tokens"]]
    names_small = ["rel_bias_table", "norm_in", "sink_a", "norm_q_lat", "norm_kv_lat", "norm_out_a", "norm_out_b", "norm_final"]
    outs = _adamw_small(r_small[:-2], [two_d(a) for a in small_w], [two_d(a) for a in small_m], [two_d(a) for a in small_v],
                        r_small[-2], "adamw_replicated")
    loss = outs[-1][0, 0]
    ns = len(names_small)
    for j, n in enumerate(names_small):
        res[n] = [outs[kk * ns + j].reshape(small_w[j].shape) for kk in range(4)]
    order = ["meta_tokens", "rel_bias_table", "norm_in", "w_in", "sink_a", "norm_q_lat", "w_uq", "norm_kv_lat", "w_ukv",
             "norm_out_a", "norm_out_b", "w_out", "norm_final"]
    return (loss, grad_x, *[res[n][kk] for kk in range(4) for n in order])
```

```python
import functools
import math

import numpy as np
import jax
import jax.numpy as jnp
from jax import lax
from jax.experimental import pallas as pl
from jax.experimental.pallas import tpu as pltpu

F32, BF16 = jnp.float32, jnp.bfloat16
D = 1024
NMETA = 16
BLK = 128
A_HEADS, A_KV, A_DH, A_W = 8, 2, 64, 512
B_HEADS, NOPE, ROPE, B_V, B_W = 8, 64, 32, 64, 512
Q_RANK, KV_RANK = 256, 128
IN_W = 2208
N_BUCKETS, MAX_DIST = 32, 128
THETA = 10000.0
EPS = 1e-6
NEG = -1e30
A_SCALE = A_DH ** -0.5
B_SCALE = (NOPE + ROPE) ** -0.5
LOG2E = math.log2(math.e)
MASK_LANE = NOPE + ROPE
SUM_LANE = B_V
LANES = 128
NDEV = 8
MESH = pl.DeviceIdType.MESH

NX = 2560
QA0, KA0, VA0, GA0, CQ0, CKV0, KR0, GB0 = 0, 512, 768, 1024, 1536, 1792, 1920, 2048

ADAM_LR, ADAM_B1, ADAM_B2, ADAM_EPS, ADAM_WD, ADAM_STEP = 0.001, 0.9, 0.999, 1e-08, 0.01, 10


def _dot(a, b):
    return jnp.dot(a, b, preferred_element_type=F32)


def _dot_nt(a, b):
    return lax.dot_general(a, b, (((1,), (1,)), ((), ())), preferred_element_type=F32)


def _dot_tn(a, b):
    return lax.dot_general(a, b, (((0,), (0,)), ((), ())), preferred_element_type=F32)


def _lane(shape):
    return lax.broadcasted_iota(jnp.int32, shape, len(shape) - 1)


def _rstd(x):
    return lax.rsqrt(jnp.mean(x * x, axis=-1, keepdims=True) + EPS)


def _rms_bwd(dn, x, r, g):
    z = dn * g
    dx = r * z - x * (r * r * r) * jnp.mean(z * x, axis=-1, keepdims=True)
    return dx, jnp.sum(dn * (x * r), axis=0, keepdims=True)


def _row(i):
    return lambda *ids: (ids[0], i)


def _const2(*ids):
    return (0, 0)


def _params(*sem):
    return pltpu.CompilerParams(dimension_semantics=sem)


SHARD_W = IN_W // NDEV


def _pack_w_in(shards):
    def nat(a, b):
        parts = []
        while a < b:
            p, e = a // SHARD_W, min(b, (a // SHARD_W + 1) * SHARD_W)
            parts.append(shards[p][:, a - p * SHARD_W:e - p * SHARD_W])
            a = e
        return parts
    dup = lambda a: nat(a, a + 64) * 2 + nat(a + 64, a + 128) * 2
    zeros = [jnp.zeros((shards.shape[1], LANES - ROPE), shards.dtype)]
    return jnp.concatenate(nat(0, 512) + dup(512) + dup(640) + nat(768, 1280) + nat(1280, 1536) + nat(1536, 1664)
                           + nat(1664, 1696) + zeros + nat(1696, 2208), axis=1)


_NAT_GROUPS = ((0, 512, QA0), (512, 576, KA0), (576, 640, KA0 + 128), (640, 704, VA0), (704, 768, VA0 + 128),
               (768, 1280, GA0), (1280, 1536, CQ0), (1536, 1664, CKV0), (1664, 1696, KR0), (1696, 2208, GB0))


def _unpack_w_in_grad(g):
    def nat(a, b):
        lo, _, pk = next(grp for grp in _NAT_GROUPS if grp[0] <= a < grp[1])
        t = g[:, pk + a - lo:pk + b - lo]
        return t + g[:, pk + 64 + a - lo:pk + 64 + b - lo] if 512 <= a < 768 else t
    cuts = sorted({c for grp in _NAT_GROUPS for c in grp[:2]} | {p * SHARD_W for p in range(NDEV + 1)})
    shards = [[] for _ in range(NDEV)]
    for a, b in zip(cuts[:-1], cuts[1:]):
        shards[a // SHARD_W].append(nat(a, b))
    return jnp.stack([jnp.concatenate(parts, axis=1) for parts in shards])


def _pack_w_uq(w):
    t = w.reshape(Q_RANK, B_HEADS, NOPE + ROPE)
    t = jnp.concatenate([t, jnp.zeros((Q_RANK, B_HEADS, LANES - NOPE - ROPE), w.dtype)], axis=-1)
    return t.reshape(Q_RANK, B_HEADS * LANES)


def _unpack_w_uq_grad(g):
    return g.reshape(Q_RANK, B_HEADS, LANES)[:, :, :NOPE + ROPE].reshape(Q_RANK, B_HEADS * (NOPE + ROPE))


def _pack_w_ukv(w):
    t = w.reshape(KV_RANK, B_HEADS, NOPE + B_V)
    z = jnp.zeros((KV_RANK, B_HEADS, LANES - NOPE), w.dtype)
    pad = lambda u: jnp.concatenate([u, z], axis=-1).reshape(KV_RANK, B_HEADS * LANES)
    return pad(t[:, :, :NOPE]), pad(t[:, :, NOPE:])


def _unpack_w_ukv_grad(gk, gv):
    head = lambda g: g.reshape(KV_RANK, B_HEADS, LANES)[:, :, :NOPE]
    return jnp.concatenate([head(gk), head(gv)], axis=-1).reshape(KV_RANK, B_HEADS * (NOPE + B_V))


def _t5_bucket(rel):
    nb = N_BUCKETS // 2
    max_exact = nb // 2
    ret = jnp.where(rel > 0, nb, 0)
    n = jnp.abs(rel)
    nf = jnp.maximum(n, 1).astype(F32)
    large = max_exact + (jnp.log(nf / max_exact) / math.log(MAX_DIST / max_exact) * (nb - max_exact)).astype(jnp.int32)
    large = jnp.minimum(large, nb - 1)
    return ret + jnp.where(n < max_exact, n, large)


def _bias_buckets(S):
    q = np.arange(BLK)[:, None]
    k = np.arange(4 * BLK)[None, :]
    is_meta_key = (k >= 3 * BLK) & (k < 3 * BLK + NMETA)
    mi = k - 3 * BLK
    rels, valids = [], []
    for shift in (0, BLK, 2 * BLK):
        rel_real = k - shift - q
        valid_real = (k < 3 * BLK) & (np.abs(rel_real) <= BLK)
        far = -(NMETA + MAX_DIST + BLK)
        rel_meta = (mi - (NMETA + q)) if shift == 0 else np.full_like(k + q, far)
        rels.append(np.where(is_meta_key, rel_meta, rel_real))
        valids.append(valid_real | is_meta_key)
    qm = q < NMETA
    rel_real = NMETA + k - q
    valid_real = (k < BLK) & (np.abs(rel_real) <= BLK) & qm
    rels.append(np.where(is_meta_key, mi - q, rel_real))
    valids.append(valid_real | is_meta_key)
    rel = jnp.asarray(np.stack(rels).astype(np.int32))
    valid = jnp.asarray(np.stack(valids))
    return jnp.where(valid, _t5_bucket(rel), -1).astype(jnp.int32)


def _rope_tables(S, Lp, off, passthrough):
    half = ROPE // 2
    r = np.arange(Lp)
    pos = np.where(r < S, NMETA + r, np.where(r < S + NMETA, r - S, 0)).astype(np.float32)
    freqs = THETA ** (-jnp.arange(half, dtype=F32) / half)
    ang = jnp.asarray(pos)[:, None] * freqs[None, :]
    cos, sin = jnp.cos(ang), jnp.sin(ang)
    z = lambda n: jnp.zeros((Lp, n), F32)
    head = jnp.ones((Lp, off), F32) if passthrough else z(off)
    c = jnp.concatenate([head, cos, cos, z(LANES - off - ROPE)], axis=1)
    s1 = jnp.concatenate([z(off), -sin, z(LANES - off - half)], axis=1)
    s2 = jnp.concatenate([z(off + half), sin, z(LANES - off - ROPE)], axis=1)
    return jnp.stack([c, s1, s2])


def _rope(x, t):
    return x * t[0] + pltpu.roll(x, LANES - 16, 1) * t[1] + pltpu.roll(x, 16, 1) * t[2]


def _rope_t(dy, t):
    return dy * t[0] + pltpu.roll(dy * t[1], 16, 1) + pltpu.roll(dy * t[2], LANES - 16, 1)


def _me_and_peers():
    x, y, c = lax.axis_index("x"), lax.axis_index("y"), lax.axis_index("c")
    flip = lambda v, b: 1 - v if b else v
    peers = [(flip(x, k & 4), flip(y, k & 2), flip(c, k & 1)) for k in range(1, NDEV)]
    return 4 * x + 2 * y + c, peers


def _xchg_start(srcs, kinds, name, after=None):
    n = len(srcs)
    lands = [lax.empty((NDEV,) + s.shape[-2:], s.dtype) for s in srcs]
    extra = [] if after is None else [after]

    def body(*refs):
        src_refs, land_refs = refs[:n], refs[n:2 * n]
        ssems, rsems = refs[2 * n + len(extra):3 * n + len(extra)], refs[3 * n + len(extra):4 * n + len(extra)]
        token = refs[6 * n + len(extra)]
        me, peers = _me_and_peers()
        for it in range(n):
            for k, (px, py, pc) in enumerate(peers):
                src = src_refs[it] if kinds[it] == "gather" else src_refs[it].at[4 * px + 2 * py + pc]
                pltpu.make_async_remote_copy(src_ref=src, dst_ref=land_refs[it].at[me], send_sem=ssems[it].at[k],
                                             recv_sem=rsems[it].at[k], device_id=(px, py, pc), device_id_type=MESH).start()
        token[...] = jnp.zeros_like(token)

    hbm = pl.BlockSpec(memory_space=pltpu.HBM)
    sem = pl.BlockSpec(memory_space=pltpu.SEMAPHORE)
    outs = pl.pallas_call(
        body, name=name,
        out_shape=tuple([pltpu.SemaphoreType.DMA((NDEV - 1,))] * (2 * n) + [pltpu.HBM(a.shape, a.dtype) for a in srcs + lands]
                        + [jax.ShapeDtypeStruct((8, LANES), F32)]),
        in_specs=[hbm] * (2 * n) + [pl.BlockSpec(memory_space=pl.ANY)] * len(extra),
        out_specs=tuple([sem] * (2 * n) + [hbm] * (2 * n) + [pl.BlockSpec(memory_space=pltpu.VMEM)]),
        input_output_aliases={i: 2 * n + i for i in range(2 * n)},
        compiler_params=pltpu.CompilerParams(has_side_effects=pltpu.SideEffectType.DATAFLOW_SIDE_EFFECTING),
    )(*[pltpu.with_memory_space_constraint(a, pltpu.HBM) for a in srcs + lands], *extra)
    handles = [(outs[it], outs[n + it], outs[2 * n + it], outs[3 * n + it], kinds[it]) for it in range(n)]
    return handles, outs[4 * n]


def _xchg_wait(handles, after, name):
    n = len(handles)

    def body(*refs):
        src_refs, land_refs = refs[:n], refs[n:2 * n]
        ssems, rsems = refs[2 * n:3 * n], refs[3 * n:4 * n]
        me, peers = _me_and_peers()
        for it in range(n):
            for k, (px, py, pc) in enumerate(peers):
                src = src_refs[it] if handles[it][4] == "gather" else src_refs[it].at[4 * px + 2 * py + pc]
                cp = pltpu.make_async_remote_copy(src_ref=src, dst_ref=land_refs[it].at[me], send_sem=ssems[it].at[k],
                                                  recv_sem=rsems[it].at[k], device_id=(px, py, pc), device_id_type=MESH)
                cp.wait_send()
                cp.wait_recv()

    hbm = pl.BlockSpec(memory_space=pltpu.HBM)
    sem = pl.BlockSpec(memory_space=pltpu.SEMAPHORE)
    srcs, lands = [h[2] for h in handles], [h[3] for h in handles]
    outs = pl.pallas_call(
        body, name=name, out_shape=tuple(pltpu.HBM(a.shape, a.dtype) for a in srcs + lands),
        in_specs=[hbm] * (2 * n) + [sem] * (2 * n) + [pl.BlockSpec(memory_space=pl.ANY)], out_specs=tuple([hbm] * (2 * n)),
        input_output_aliases={i: i for i in range(2 * n)},
        compiler_params=pltpu.CompilerParams(has_side_effects=pltpu.SideEffectType.DATAFLOW_SIDE_EFFECTING),
    )(*srcs, *lands, *[h[0] for h in handles], *[h[1] for h in handles], after)
    return list(outs[:n]), list(outs[n:])


def _own_slot(land, own, me):
    return lax.dynamic_update_slice(land, own[None].astype(land.dtype), (me, 0, 0))


def _after(x, token):
    return x + token[0, 0]


def _adamw_update(g, w, m, v):
    mn = ADAM_B1 * m + (1.0 - ADAM_B1) * g
    vn = ADAM_B2 * v + (1.0 - ADAM_B2) * (g * g)
    m_hat = mn / (1.0 - ADAM_B1 ** ADAM_STEP)
    v_hat = vn / (1.0 - ADAM_B2 ** ADAM_STEP)
    return -ADAM_LR * (m_hat / (jnp.sqrt(v_hat) + ADAM_EPS) + ADAM_WD * w), mn, vn


def _sum_partials(r_ref):
    g = r_ref[0].astype(F32)
    for p in range(1, NDEV):
        g = g + r_ref[p].astype(F32)
    return g


def _adamw_param(recvs, w, m, v, name):
    depth, r, c = w.shape
    tr = min(r, 256)
    nl = len(recvs)

    def body(*refs):
        r_refs = refs[:nl]
        w_ref, m_ref, v_ref, g_out, d_out, m_out, v_out = refs[nl:]
        for li in range(nl):
            @pl.when(pl.program_id(0) == li)
            def _(li=li):
                g = _sum_partials(r_refs[li])
                g_out[0] = g
                d_out[0], m_out[0], v_out[0] = _adamw_update(g, w_ref[0], m_ref[0], v_ref[0])

    blk = pl.BlockSpec((1, tr, c), lambda l, i: (l, i, 0))
    return pl.pallas_call(
        body, name=name, grid=(depth, r // tr),
        in_specs=[pl.BlockSpec((NDEV, tr, c), lambda l, i: (0, i, 0))] * nl + [blk, blk, blk],
        out_specs=(blk, blk, blk, blk), out_shape=(jax.ShapeDtypeStruct(w.shape, F32),) * 4,
        compiler_params=_params("arbitrary", "arbitrary"),
    )(*recvs, w, m, v)


def _adamw_small(recvs, ws, ms, vs, loss_parts, name):
    n = len(ws)

    def body(*refs):
        r_refs, w_refs, m_refs, v_refs = refs[:n], refs[n:2 * n], refs[2 * n:3 * n], refs[3 * n:4 * n]
        outs = refs[4 * n + 1:]
        for j in range(n):
            g = _sum_partials(r_refs[j])
            outs[j][...] = g
            outs[n + j][...], outs[2 * n + j][...], outs[3 * n + j][...] = _adamw_update(
                g, w_refs[j][...], m_refs[j][...], v_refs[j][...])
        outs[4 * n][...] = _sum_partials(refs[4 * n])

    vm = pl.BlockSpec(memory_space=pltpu.VMEM)
    return pl.pallas_call(
        body, name=name, in_specs=[vm] * (4 * n + 1), out_specs=tuple([vm] * (4 * n + 1)),
        out_shape=tuple(jax.ShapeDtypeStruct(a.shape, F32) for a in ws) * 4 + (jax.ShapeDtypeStruct(loss_parts.shape[1:], F32),),
    )(*recvs, *ws, *ms, *vs, loss_parts)


def _inproj_fwd(h, g, wx, tm, name):
    T = h.shape[0]

    def body(h_ref, g_ref, w_ref, o_ref):
        x = h_ref[...]
        u = (x * _rstd(x) * g_ref[...]).astype(BF16)
        o_ref[...] = _dot(u, w_ref[...])

    return pl.pallas_call(
        body, name=name, grid=(T // tm,),
        in_specs=[pl.BlockSpec((tm, D), _row(0)), pl.BlockSpec((1, D), _const2), pl.BlockSpec((D, NX), _const2)],
        out_specs=pl.BlockSpec((tm, NX), _row(0)), out_shape=jax.ShapeDtypeStruct((T, NX), F32),
        compiler_params=_params("parallel"),
    )(h, g, wx)


def _mla_prep_fwd(proj, gq, gkv, wq, wkk, wkv, tq, tk, tm, name):
    T = proj.shape[0]

    def body(cq_ref, ckv_ref, kr_ref, gq_ref, gkv_ref, wq_ref, wkk_ref, wkv_ref, tq_ref, tk_ref, q_ref, k_ref, v_ref):
        cq = cq_ref[...]
        q = _dot((cq * _rstd(cq) * gq_ref[...]).astype(BF16), wq_ref[...])
        tqv = tq_ref[...]
        lane = _lane((1, LANES))
        one = jnp.where(lane == MASK_LANE, 1.0, 0.0)
        for h in range(B_HEADS):
            cs = slice(LANES * h, LANES * (h + 1))
            q_ref[:, cs] = (_rope(q[:, cs], tqv) * (B_SCALE * LOG2E) + one).astype(BF16)
        tkv = tk_ref[...]
        ksh = pltpu.roll(_rope(kr_ref[...], tkv), NOPE, 1) + tkv[3]
        ckv = ckv_ref[...]
        cb = (ckv * _rstd(ckv) * gkv_ref[...]).astype(BF16)
        kn = _dot(cb, wkk_ref[...])
        vn = _dot(cb, wkv_ref[...])
        minus = jnp.where((lane == SUM_LANE) | (lane == SUM_LANE + 1), -1.0, 0.0)
        for h in range(B_HEADS):
            cs = slice(LANES * h, LANES * (h + 1))
            k_ref[:, cs] = (kn[:, cs] + ksh).astype(BF16)
            v_ref[:, cs] = (vn[:, cs] + minus).astype(BF16)

    tab = lambda n: pl.BlockSpec((n, tm, LANES), lambda i: (0, i, 0))
    wide = pl.BlockSpec((tm, 1024), _row(0))
    return pl.pallas_call(
        body, name=name, grid=(T // tm,),
        in_specs=[pl.BlockSpec((tm, 256), _row(CQ0 // 256)), pl.BlockSpec((tm, 128), _row(CKV0 // 128)),
                  pl.BlockSpec((tm, 128), _row(KR0 // 128)), pl.BlockSpec((1, Q_RANK), _const2),
                  pl.BlockSpec((1, KV_RANK), _const2), pl.BlockSpec((Q_RANK, 1024), _const2),
                  pl.BlockSpec((KV_RANK, 1024), _const2), pl.BlockSpec((KV_RANK, 1024), _const2), tab(3), tab(4)],
        out_specs=(wide, wide, wide), out_shape=(jax.ShapeDtypeStruct((T, 1024), BF16),) * 3,
        compiler_params=_params("parallel"),
    )(proj, proj, proj, gq, gkv, wq, wkk, wkv, tq, tk)


SINK_SLOT = 4 * BLK - 1


def _build_bias(buckets, table, sinks, name):
    depth = sinks.shape[0]

    def body(tab_ref, sink_ref, b_ref, *o_refs):
        bidx = b_ref[0]
        slot = _lane(bidx.shape)
        for h in range(A_HEADS):
            acc = jnp.full(bidx.shape, NEG, F32)
            for j in range(N_BUCKETS):
                acc = jnp.where(bidx == j, tab_ref[j, h] * LOG2E, acc)
            for l in range(depth):
                o_refs[l][0, h] = jnp.where(slot == SINK_SLOT, sink_ref[l, h] * LOG2E, acc)

    smem = pl.BlockSpec(memory_space=pltpu.SMEM)
    blk = pl.BlockSpec((1, A_HEADS, BLK, 4 * BLK), lambda i: (i, 0, 0, 0))
    return pl.pallas_call(
        body, name=name, grid=(4,),
        in_specs=[smem, smem, pl.BlockSpec((1, BLK, 4 * BLK), lambda i: (i, 0, 0))],
        out_specs=tuple([blk] * depth), out_shape=tuple([jax.ShapeDtypeStruct((4, A_HEADS, BLK, 4 * BLK), F32)] * depth),
        compiler_params=_params("parallel"),
    )(table, sinks, buckets)


def _win_base(n, NB):
    return jnp.where(n == NB, 0, jnp.clip(n - 1, 0, NB - 3))


def _win_variant(n, NB):
    return jnp.where(n == 0, 0, jnp.where(n < NB - 1, 1, jnp.where(n == NB - 1, 2, 3)))


def _win_kv(k_ref, v_ref, base, S, kvh):
    cs = slice(LANES * kvh, LANES * (kvh + 1))
    k2 = jnp.concatenate([k_ref[pl.ds(base, 3 * BLK), cs], k_ref[S:S + BLK, cs]], axis=0).astype(BF16)
    v2 = jnp.concatenate([v_ref[pl.ds(base, 3 * BLK), cs], v_ref[S:S + BLK, cs]], axis=0)
    lane = _lane(v2.shape)
    v2 = jnp.where(lane < A_DH, v2, jnp.where(lane < A_DH + 2, -1.0, 0.0)).astype(BF16)
    return k2, v2


def _win_fwd(proj, bias, Bl, S, name):
    T = proj.shape[0]
    Lp = S + BLK
    NB = S // BLK
    nblk = Lp // BLK
    G = A_HEADS // A_KV

    def body(q_ref, k_ref, v_ref, b_ref, o_ref, lse_ref):
        n = pl.program_id(1)
        base = pl.multiple_of(_win_base(n, NB) * BLK, BLK)
        lane = _lane((BLK, LANES))
        lo = lane < A_DH
        kv = [_win_kv(k_ref, v_ref, base, S, kvh) for kvh in range(A_KV)]
        scores = []
        for h in range(A_HEADS):
            qp = q_ref[:, LANES * (h // 2):LANES * (h // 2 + 1)]
            qm = (jnp.where(lo if h % 2 == 0 else ~lo, qp, 0.0) * (A_SCALE * LOG2E)).astype(BF16)
            scores.append(_dot_nt(qm, kv[h // G][0]) + b_ref[0, h])
        ms = [jnp.max(s, axis=-1, keepdims=True) for s in scores]
        es = [jnp.exp2(s - m).astype(BF16) for s, m in zip(scores, ms)]
        outs = []
        t = jnp.zeros((BLK, LANES), F32)
        for h in range(A_HEADS):
            o2 = _dot(es[h], kv[h // G][1])
            l = -jnp.sum(jnp.where(lane == A_DH, o2, 0.0), axis=-1, keepdims=True)
            outs.append(o2 * (1.0 / l))
            t = jnp.where(lane == h, ms[h] + jnp.log(l) * LOG2E, t)
        for j in range(A_HEADS // 2):
            o_ref[:, LANES * j:LANES * (j + 1)] = jnp.where(lo, outs[2 * j], pltpu.roll(outs[2 * j + 1], A_DH, 1))
        lse_ref[...] = t

    qrow = lambda b, n: (b * nblk + n, 0)
    return pl.pallas_call(
        body, name=name, grid=(Bl, nblk),
        in_specs=[pl.BlockSpec((BLK, A_W), qrow),
                  pl.BlockSpec((Lp, 256), lambda b, n: (b, KA0 // 256)), pl.BlockSpec((Lp, 256), lambda b, n: (b, VA0 // 256)),
                  pl.BlockSpec((1, A_HEADS, BLK, 4 * BLK), lambda b, n: (_win_variant(n, NB), 0, 0, 0))],
        out_specs=(pl.BlockSpec((BLK, A_W), qrow), pl.BlockSpec((BLK, LANES), qrow)),
        out_shape=(jax.ShapeDtypeStruct((T, A_W), F32), jax.ShapeDtypeStruct((T, LANES), F32)),
        compiler_params=_params("parallel", "arbitrary"),
    )(proj, proj, proj, bias)


def _q_tiles(Lp):
    S = Lp - BLK
    return S // 256, S


def _mla_fwd(q, k, v, Bl, Lp, name):
    T = q.shape[0]
    n_big, S = _q_tiles(Lp)

    def body(q_ref, k_ref, v_ref, o_ref, lse_ref):
        def qtiles(starts, nq):
            lane = _lane((nq, LANES))
            cols = [slice(LANES * hh, LANES * (hh + 1)) for hh in range(2)]
            scores = [[_dot_nt(q_ref[pl.ds(r0, nq), cs], k_ref[:, cs]) for cs in cols] for r0 in starts]
            for r0, tile_scores in zip(starts, scores):
                outs = []
                t = jnp.zeros((nq, LANES), F32)
                for hh in range(2):
                    cs, s = cols[hh], tile_scores[hh]
                    m = jnp.max(s, axis=-1, keepdims=True)
                    o2 = _dot(jnp.exp2(s - m).astype(BF16), v_ref[:, cs])
                    l = -jnp.sum(jnp.where(lane == SUM_LANE, o2, 0.0), axis=-1, keepdims=True)
                    outs.append(o2 * (1.0 / l))
                    t = jnp.where(lane == hh, m + jnp.log(l) * LOG2E, t)
                o_ref[pl.ds(r0, nq), :] = jnp.where(lane < B_V, outs[0], pltpu.roll(outs[1], B_V, 1))
                lse_ref[pl.ds(r0, nq), :] = t

        def step(i, c):
            qtiles([pl.multiple_of(i * 512, 256), pl.multiple_of(i * 512 + 256, 256)], 256)
            return c

        lax.fori_loop(0, n_big // 2, step, 0)
        if n_big % 2:
            qtiles([(n_big - 1) * 256], 256)
        qtiles([S], BLK)

    pair = lambda b, j: (b, j)
    wide, narrow = pl.BlockSpec((Lp, 256), pair), pl.BlockSpec((Lp, LANES), pair)
    return pl.pallas_call(
        body, name=name, grid=(Bl, B_HEADS // 2), in_specs=[wide, wide, wide], out_specs=(narrow, narrow),
        out_shape=(jax.ShapeDtypeStruct((T, B_W), F32), jax.ShapeDtypeStruct((T, B_W), F32)),
        compiler_params=_params("parallel", "parallel"),
    )(q, k, v)


def _gated(y, gate, gain):
    r = _rstd(y)
    nrm = y * r
    sg = jax.nn.sigmoid(gate)
    return r, nrm, sg, nrm * gain, gate * sg


def _out_fwd(ya, yb, proj, na, nb, wout, h, tm, name):
    T = h.shape[0]

    def body(ya_ref, yb_ref, ga_ref, gb_ref, na_ref, nb_ref, w_ref, h_ref, o_ref):
        _, _, _, n_a, sl_a = _gated(ya_ref[...], ga_ref[...], na_ref[...])
        _, _, _, n_b, sl_b = _gated(yb_ref[...], gb_ref[...], nb_ref[...])
        acc = _dot((n_a * sl_a).astype(BF16), w_ref[0:A_W, :]) + _dot((n_b * sl_b).astype(BF16), w_ref[A_W:A_W + B_W, :])
        o_ref[...] = h_ref[...] + acc

    half = pl.BlockSpec((tm, 512), _row(0))
    return pl.pallas_call(
        body, name=name, grid=(T // tm,),
        in_specs=[half, half, pl.BlockSpec((tm, 512), _row(GA0 // 512)), pl.BlockSpec((tm, 512), _row(GB0 // 512)),
                  pl.BlockSpec((1, 512), _const2), pl.BlockSpec((1, 512), _const2), pl.BlockSpec((D, D), _const2),
                  pl.BlockSpec((tm, D), _row(0))],
        out_specs=pl.BlockSpec((tm, D), _row(0)), out_shape=jax.ShapeDtypeStruct((T, D), F32),
        compiler_params=_params("parallel"),
    )(ya, yb, proj, proj, na, nb, wout, h)


def _loss_head(h, gf, target, Bl, S, name):
    T = h.shape[0]
    nblk = (S + BLK) // BLK
    NB = S // BLK

    def body(h_ref, g_ref, t_ref, dh_ref, loss_ref, dg_ref):
        b, n = pl.program_id(0), pl.program_id(1)

        @pl.when((b == 0) & (n == 0))
        def _():
            loss_ref[...] = jnp.zeros_like(loss_ref)
            dg_ref[...] = jnp.zeros_like(dg_ref)

        @pl.when(n < NB)
        def _():
            x = h_ref[...]
            g = g_ref[...]
            r = _rstd(x)
            err = x * r * g - t_ref[0]
            loss_ref[...] += 0.5 * jnp.sum(jnp.mean(err * err, axis=-1, keepdims=True))
            dx, dg = _rms_bwd(err * (1.0 / D), x, r, g)
            dh_ref[...] = dx
            dg_ref[...] += dg

        @pl.when(n >= NB)
        def _():
            dh_ref[...] = jnp.zeros_like(dh_ref)

    return pl.pallas_call(
        body, name=name, grid=(Bl, nblk),
        in_specs=[pl.BlockSpec((BLK, D), lambda b, n: (b * nblk + n, 0)), pl.BlockSpec((1, D), lambda b, n: (0, 0)),
                  pl.BlockSpec((1, BLK, D), lambda b, n: (b, jnp.minimum(n, NB - 1), 0))],
        out_specs=(pl.BlockSpec((BLK, D), lambda b, n: (b * nblk + n, 0)), pl.BlockSpec((8, LANES), lambda b, n: (0, 0)),
                   pl.BlockSpec((1, D), lambda b, n: (0, 0))),
        out_shape=(jax.ShapeDtypeStruct((T, D), F32), jax.ShapeDtypeStruct((8, LANES), F32), jax.ShapeDtypeStruct((1, D), F32)),
        compiler_params=_params("arbitrary", "arbitrary"),
    )(h, gf, target)


def _out_bwd(dh, ya, yb, proj, na, nb, wout, tm, name):
    T = dh.shape[0]

    def body(dh_ref, ya_ref, yb_ref, ga_ref, gb_ref, na_ref, nb_ref, w_ref,
             dya_ref, dyb_ref, dga_ref, dgb_ref, dw_ref, dna_ref, dnb_ref):
        @pl.when(pl.program_id(0) == 0)
        def _():
            dw_ref[...] = jnp.zeros_like(dw_ref)
            dna_ref[...] = jnp.zeros_like(dna_ref)
            dnb_ref[...] = jnp.zeros_like(dnb_ref)

        dhb = dh_ref[...].astype(BF16)
        dy = _dot_nt(dhb, w_ref[...])
        parts = ((ya_ref, ga_ref, na_ref, dya_ref, dga_ref, dna_ref, 0), (yb_ref, gb_ref, nb_ref, dyb_ref, dgb_ref, dnb_ref, A_W))
        for y_ref, gate_ref, gain_ref, dy_out, dgate_out, dgain_out, c0 in parts:
            y, gate, gain = y_ref[...], gate_ref[...], gain_ref[...]
            r, nrm, sg, n_g, sl = _gated(y, gate, gain)
            dyp = dy[:, c0:c0 + 512]
            dgate_out[...] = dyp * n_g * (sg * (1.0 + gate * (1.0 - sg)))
            dx, dgain = _rms_bwd(dyp * sl, y, r, gain)
            dy_out[...] = dx
            dgain_out[...] += dgain
            dw_ref[c0:c0 + 512, :] += _dot_tn((n_g * sl).astype(BF16), dhb)

    half = pl.BlockSpec((tm, 512), _row(0))
    vec = pl.BlockSpec((1, 512), _const2)
    return pl.pallas_call(
        body, name=name, grid=(T // tm,),
        in_specs=[pl.BlockSpec((tm, D), _row(0)), half, half, pl.BlockSpec((tm, 512), _row(GA0 // 512)),
                  pl.BlockSpec((tm, 512), _row(GB0 // 512)), vec, vec, pl.BlockSpec((D, D), _const2)],
        out_specs=(half, half, half, half, pl.BlockSpec((D, D), _const2), vec, vec),
        out_shape=(jax.ShapeDtypeStruct((T, 512), F32),) * 4 + (jax.ShapeDtypeStruct((D, D), F32),)
        + (jax.ShapeDtypeStruct((1, 512), F32),) * 2,
        compiler_params=_params("arbitrary"),
    )(dh, ya, yb, proj, proj, na, nb, wout)


def _mla_bwd(q, k, v, o, do, lse, after, Bl, Lp, name):
    T = q.shape[0]
    n_big, S = _q_tiles(Lp)

    def body(q_ref, k_ref, v_ref, o_ref, do_ref, lse_ref, _, dq_ref, dk_ref, dv_ref):
        dk_ref[...] = jnp.zeros_like(dk_ref)
        dv_ref[...] = jnp.zeros_like(dv_ref)

        def qtile(r0, nq):
            lane = _lane((nq, LANES))
            dop = do_ref[pl.ds(r0, nq), :]
            op = o_ref[pl.ds(r0, nq), :]
            lsev = lse_ref[pl.ds(r0, nq), :]
            for hh in range(2):
                cs = slice(LANES * hh, LANES * (hh + 1))
                dom = jnp.where((lane < B_V) if hh == 0 else (lane >= B_V), dop, 0.0)
                delta = jnp.sum(dom * op, axis=-1, keepdims=True)
                d_hi = delta.astype(BF16).astype(F32)
                x = dom if hh == 0 else pltpu.roll(dom, B_V, 1)
                domx = jnp.where(lane == SUM_LANE, d_hi, jnp.where(lane == SUM_LANE + 1, delta - d_hi, x)).astype(BF16)
                qh, kh, vh = q_ref[pl.ds(r0, nq), cs], k_ref[:, cs], v_ref[:, cs]
                lse_h = jnp.sum(jnp.where(lane == hh, lsev, 0.0), axis=-1, keepdims=True)
                p = jnp.exp2(_dot_nt(qh, kh) - lse_h)
                ds = (p * _dot_nt(domx, vh)).astype(BF16)
                dq_ref[pl.ds(r0, nq), cs] = _dot(ds, kh) * B_SCALE
                dk_ref[:, cs] += _dot_tn(ds, qh)
                dv_ref[:, cs] += _dot_tn(p.astype(BF16), domx)

        def step(i, c):
            qtile(pl.multiple_of(i * 512, 256), 256)
            qtile(pl.multiple_of(i * 512 + 256, 256), 256)
            return c

        lax.fori_loop(0, n_big // 2, step, 0)
        if n_big % 2:
            qtile((n_big - 1) * 256, 256)
        qtile(S, BLK)
        dk_ref[...] = dk_ref[...] * (1.0 / LOG2E)

    pair = lambda b, j: (b, j)
    wide, narrow = pl.BlockSpec((Lp, 256), pair), pl.BlockSpec((Lp, LANES), pair)
    return pl.pallas_call(
        body, name=name, grid=(Bl, B_HEADS // 2),
        in_specs=[wide, wide, wide, narrow, narrow, narrow, pl.BlockSpec(memory_space=pl.ANY)],
        out_specs=(wide, wide, wide), out_shape=(jax.ShapeDtypeStruct((T, 1024), F32),) * 3,
        compiler_params=_params("parallel", "parallel"),
    )(q, k, v, o, do, lse, after)


def _win_bwd(proj, bias, ya, dya, lse, Bl, S, name):
    T = proj.shape[0]
    Lp = S + BLK
    NB = S // BLK
    nblk = Lp // BLK
    G = A_HEADS // A_KV

    def body(q_ref, k_ref, v_ref, b_ref, o_ref, do_ref, lse_ref, dq_ref, dk_ref, dv_ref, db_ref):
        n = pl.program_id(1)

        @pl.when(n == 0)
        def _():
            dk_ref[...] = jnp.zeros_like(dk_ref)
            dv_ref[...] = jnp.zeros_like(dv_ref)

        @pl.when((n == 0) | (n == 1) | (n == NB - 1) | (n == NB))
        def _():
            db_ref[...] = jnp.zeros_like(db_ref)

        base = pl.multiple_of(_win_base(n, NB) * BLK, BLK)
        lane = _lane((BLK, LANES))
        lo = lane < A_DH
        lsev = lse_ref[...]
        kv = [_win_kv(k_ref, v_ref, base, S, kvh) for kvh in range(A_KV)]
        qms, pbs, domxs, dsls = [], [], [], []
        for h in range(A_HEADS):
            k2, v2 = kv[h // G]
            ps = slice(LANES * (h // 2), LANES * (h // 2 + 1))
            qm = (jnp.where(lo if h % 2 == 0 else ~lo, q_ref[:, ps], 0.0) * (A_SCALE * LOG2E)).astype(BF16)
            lse_h = jnp.sum(jnp.where(lane == h, lsev, 0.0), axis=-1, keepdims=True)
            p = jnp.exp2(_dot_nt(qm, k2) + b_ref[0, h] - lse_h)
            dop, op = do_ref[:, ps], o_ref[:, ps]
            if h % 2 == 1:
                dop, op = pltpu.roll(dop, A_DH, 1), pltpu.roll(op, A_DH, 1)
            dom = jnp.where(lo, dop, 0.0)
            delta = jnp.sum(dom * op, axis=-1, keepdims=True)
            d_hi = delta.astype(BF16).astype(F32)
            domx = jnp.where(lane == A_DH, d_hi, jnp.where(lane == A_DH + 1, delta - d_hi, dom)).astype(BF16)
            dsc = p * _dot_nt(domx, v2)
            db_ref[0, 0, h] += dsc
            qms.append(qm)
            pbs.append(p.astype(BF16))
            domxs.append(domx)
            dsls.append(dsc.astype(BF16))
        dqs = [_dot(dsls[h], kv[h // G][0]) * A_SCALE for h in range(A_HEADS)]
        for kvh in range(A_KV):
            cs = slice(LANES * kvh, LANES * (kvh + 1))
            group = range(kvh * G, (kvh + 1) * G)
            stack = lambda parts: jnp.concatenate([parts[h] for h in group], axis=0)
            dk_acc = _dot_tn(stack(dsls), stack(qms)) * (1.0 / LOG2E)
            dv_acc = _dot_tn(stack(pbs), stack(domxs))
            dv_acc = jnp.where(_lane(dv_acc.shape) < A_DH, dv_acc, 0.0)
            dk_ref[pl.ds(base, 3 * BLK), cs] += dk_acc[0:3 * BLK]
            dk_ref[S:S + BLK, cs] += dk_acc[3 * BLK:4 * BLK]
            dv_ref[pl.ds(base, 3 * BLK), cs] += dv_acc[0:3 * BLK]
            dv_ref[S:S + BLK, cs] += dv_acc[3 * BLK:4 * BLK]
        for j in range(A_HEADS // 2):
            dq_ref[:, LANES * j:LANES * (j + 1)] = jnp.where(lo, dqs[2 * j], dqs[2 * j + 1])

    qrow = lambda b, n: (b * nblk + n, 0)
    kvs = pl.BlockSpec((Lp, 256), lambda b, n: (b, 0))
    return pl.pallas_call(
        body, name=name, grid=(Bl, nblk),
        in_specs=[pl.BlockSpec((BLK, A_W), qrow),
                  pl.BlockSpec((Lp, 256), lambda b, n: (b, KA0 // 256)), pl.BlockSpec((Lp, 256), lambda b, n: (b, VA0 // 256)),
                  pl.BlockSpec((1, A_HEADS, BLK, 4 * BLK), lambda b, n: (_win_variant(n, NB), 0, 0, 0)),
                  pl.BlockSpec((BLK, A_W), qrow), pl.BlockSpec((BLK, A_W), qrow), pl.BlockSpec((BLK, LANES), qrow)],
        out_specs=(pl.BlockSpec((BLK, A_W), qrow), kvs, kvs,
                   pl.BlockSpec((1, 1, A_HEADS, BLK, 4 * BLK), lambda b, n: (b, _win_variant(n, NB), 0, 0, 0))),
        out_shape=(jax.ShapeDtypeStruct((T, A_W), F32), jax.ShapeDtypeStruct((T, 256), F32),
                   jax.ShapeDtypeStruct((T, 256), F32), jax.ShapeDtypeStruct((Bl, 4, A_HEADS, BLK, 4 * BLK), F32)),
        compiler_params=_params("parallel", "arbitrary"),
    )(proj, proj, proj, bias, ya, dya, lse)


def _table_grad(dbias_list, buckets, after, name):
    nl = len(dbias_list)
    Bl = dbias_list[0].shape[0]

    def body(*refs):
        d_refs = refs[:nl]
        b_ref, _, o_ref, s_ref, acc, part = refs[nl:]
        v, b = pl.program_id(0), pl.program_id(1)

        @pl.when((v == 0) & (b == 0))
        def _():
            o_ref[...] = jnp.zeros_like(o_ref)
            s_ref[...] = jnp.zeros_like(s_ref)

        slot = _lane((BLK, 4 * BLK))
        for l in range(nl):
            for h in range(A_HEADS):
                col = jnp.sum(jnp.where(slot == SINK_SLOT, d_refs[l][0, 0, h], 0.0), axis=0, keepdims=True)
                s_ref[l, h:h + 1, :] += jnp.sum(col, axis=1, keepdims=True)

        tot = d_refs[0][0, 0]
        for r in d_refs[1:]:
            tot = tot + r[0, 0]

        @pl.when(b == 0)
        def _():
            acc[...] = tot

        @pl.when(b > 0)
        def _():
            acc[...] += tot

        @pl.when(b == Bl - 1)
        def _():
            bidx = b_ref[0]

            def step(j, c):
                mask = bidx == j
                for h in range(A_HEADS):
                    part[h, j] = jnp.sum(jnp.where(mask, acc[h], 0.0).reshape(BLK // 8, 8, 4 * BLK), axis=0)
                return c

            lax.fori_loop(0, N_BUCKETS, step, 0)
            rows = lax.broadcasted_iota(jnp.int32, (N_BUCKETS, LANES), 0)
            lanes = _lane((N_BUCKETS, LANES))
            for h in range(A_HEADS):
                col = jnp.sum(jnp.sum(part[h], axis=1), axis=-1, keepdims=True)
                o_ref[h:h + 1, :] += jnp.sum(jnp.where(rows == lanes, col, 0.0), axis=0, keepdims=True)

    return pl.pallas_call(
        body, name=name, grid=(4, Bl),
        in_specs=[pl.BlockSpec((1, 1, A_HEADS, BLK, 4 * BLK), lambda v, b: (b, v, 0, 0, 0))] * nl
        + [pl.BlockSpec((1, BLK, 4 * BLK), lambda v, b: (v, 0, 0)), pl.BlockSpec(memory_space=pl.ANY)],
        out_specs=(pl.BlockSpec((8, LANES), lambda v, b: (0, 0)), pl.BlockSpec((nl, 8, LANES), lambda v, b: (0, 0, 0))),
        out_shape=(jax.ShapeDtypeStruct((8, LANES), F32), jax.ShapeDtypeStruct((nl, 8, LANES), F32)),
        scratch_shapes=[pltpu.VMEM((A_HEADS, BLK, 4 * BLK), F32), pltpu.VMEM((A_HEADS, N_BUCKETS, 8, 4 * BLK), F32)],
        compiler_params=_params("arbitrary", "arbitrary"),
    )(*dbias_list, buckets, after)


def _mla_prep_bwd(dq, dk, dv, proj, gq, gkv, wq, wkk, wkv, tq, tk, tm, name):
    T = proj.shape[0]

    def body(dq_ref, dk_ref, dv_ref, cq_ref, ckv_ref, gq_ref, gkv_ref, wq_ref, wkk_ref, wkv_ref, tq_ref, tk_ref,
             dcq_ref, dckv_ref, dkr_ref, dwq_ref, dwkk_ref, dwkv_ref, dgq_ref, dgkv_ref):
        @pl.when(pl.program_id(0) == 0)
        def _():
            for r in (dwq_ref, dwkk_ref, dwkv_ref, dgq_ref, dgkv_ref):
                r[...] = jnp.zeros_like(r)

        tqv = tq_ref[...]
        dqp = jnp.concatenate([_rope_t(dq_ref[:, LANES * h:LANES * (h + 1)], tqv) for h in range(B_HEADS)],
                              axis=1).astype(BF16)
        cq, gq_ = cq_ref[...], gq_ref[...]
        rq = _rstd(cq)
        dwq_ref[...] += _dot_tn((cq * rq * gq_).astype(BF16), dqp)
        dx, dg = _rms_bwd(_dot_nt(dqp, wq_ref[...]), cq, rq, gq_)
        dcq_ref[...] = dx
        dgq_ref[...] += dg

        dkv_ = dk_ref[...]
        dks = dkv_[:, 0:LANES]
        for h in range(1, B_HEADS):
            dks = dks + dkv_[:, LANES * h:LANES * (h + 1)]
        dkr_ref[...] = _rope_t(pltpu.roll(dks, NOPE, 1), tk_ref[...])
        dkb = dkv_.astype(BF16)
        dvb = dv_ref[...].astype(BF16)
        ckv, gkv_ = ckv_ref[...], gkv_ref[...]
        rk = _rstd(ckv)
        cb = (ckv * rk * gkv_).astype(BF16)
        dwkk_ref[...] += _dot_tn(cb, dkb)
        dwkv_ref[...] += _dot_tn(cb, dvb)
        dx, dg = _rms_bwd(_dot_nt(dkb, wkk_ref[...]) + _dot_nt(dvb, wkv_ref[...]), ckv, rk, gkv_)
        dckv_ref[...] = dx
        dgkv_ref[...] += dg

    tab = lambda n: pl.BlockSpec((n, tm, LANES), lambda i: (0, i, 0))
    wide = pl.BlockSpec((tm, 1024), _row(0))
    return pl.pallas_call(
        body, name=name, grid=(T // tm,),
        in_specs=[wide, wide, wide, pl.BlockSpec((tm, 256), _row(CQ0 // 256)),
                  pl.BlockSpec((tm, 128), _row(CKV0 // 128)), pl.BlockSpec((1, Q_RANK), _const2),
                  pl.BlockSpec((1, KV_RANK), _const2), pl.BlockSpec((Q_RANK, 1024), _const2),
                  pl.BlockSpec((KV_RANK, 1024), _const2), pl.BlockSpec((KV_RANK, 1024), _const2), tab(3), tab(4)],
        out_specs=(pl.BlockSpec((tm, 256), _row(0)), pl.BlockSpec((tm, 128), _row(0)), pl.BlockSpec((tm, 128), _row(0)),
                   pl.BlockSpec((Q_RANK, 1024), _const2), pl.BlockSpec((KV_RANK, 1024), _const2),
                   pl.BlockSpec((KV_RANK, 1024), _const2), pl.BlockSpec((1, Q_RANK), _const2),
                   pl.BlockSpec((1, KV_RANK), _const2)),
        out_shape=(jax.ShapeDtypeStruct((T, 256), F32), jax.ShapeDtypeStruct((T, 128), F32),
                   jax.ShapeDtypeStruct((T, 128), F32), jax.ShapeDtypeStruct((Q_RANK, 1024), F32),
                   jax.ShapeDtypeStruct((KV_RANK, 1024), F32), jax.ShapeDtypeStruct((KV_RANK, 1024), F32),
                   jax.ShapeDtypeStruct((1, Q_RANK), F32), jax.ShapeDtypeStruct((1, KV_RANK), F32)),
        compiler_params=_params("arbitrary"),
    )(dq, dk, dv, proj, proj, gq, gkv, wq, wkk, wkv, tq, tk)


def _inproj_bwd(pieces, h, g, wx, dh_out, tm, name):
    T = h.shape[0]
    nsteps = T // tm
    CH = 512

    def body(*refs):
        p_refs = refs[:len(pieces)]
        h_ref, g_ref, w_ref, dho_ref, dh_ref, dw_hbm, dg_ref, acc, sem = refs[len(pieces):]
        i = pl.program_id(0)

        @pl.when(i == 0)
        def _():
            acc[...] = jnp.zeros_like(acc)
            dg_ref[...] = jnp.zeros_like(dg_ref)

        dp = jnp.concatenate([r[...].astype(BF16) for r in p_refs], axis=1)
        x, gain = h_ref[...], g_ref[...]
        r = _rstd(x)
        u = (x * r * gain).astype(BF16)
        for c in range(0, NX, CH):
            acc[:, c:c + CH] += _dot_tn(u, dp[:, c:c + CH])
        dx, dg = _rms_bwd(_dot_nt(dp, w_ref[...]), x, r, gain)
        dh_ref[...] = dho_ref[...] + dx
        dg_ref[...] += dg

        @pl.when(i == nsteps - 1)
        def _():
            cp = pltpu.make_async_copy(acc, dw_hbm, sem)
            cp.start()
            cp.wait()

    return pl.pallas_call(
        body, name=name, grid=(nsteps,),
        in_specs=[pl.BlockSpec((tm, p.shape[1]), _row(0)) for p in pieces]
        + [pl.BlockSpec((tm, D), _row(0)), pl.BlockSpec((1, D), _const2), pl.BlockSpec((D, NX), _const2),
           pl.BlockSpec((tm, D), _row(0))],
        out_specs=(pl.BlockSpec((tm, D), _row(0)), pl.BlockSpec(memory_space=pl.ANY), pl.BlockSpec((1, D), _const2)),
        out_shape=(jax.ShapeDtypeStruct((T, D), F32), jax.ShapeDtypeStruct((D, NX), F32), jax.ShapeDtypeStruct((1, D), F32)),
        scratch_shapes=[pltpu.VMEM((D, NX), F32), pltpu.SemaphoreType.DMA(())],
        compiler_params=_params("arbitrary"),
    )(*pieces, h, g, wx, dh_out)


def kernel(x, meta_tokens, rel_bias_table, norm_in, w_in, sink_a, norm_q_lat, w_uq, norm_kv_lat, w_ukv, norm_out_a, norm_out_b, w_out, norm_final, loss_target, m_meta_tokens, m_rel_bias_table, m_norm_in, m_w_in, m_sink_a, m_norm_q_lat, m_w_uq, m_norm_kv_lat, m_w_ukv, m_norm_out_a, m_norm_out_b, m_w_out, m_norm_final, v_meta_tokens, v_rel_bias_table, v_norm_in, v_w_in, v_sink_a, v_norm_q_lat, v_w_uq, v_norm_kv_lat, v_w_ukv, v_norm_out_a, v_norm_out_b, v_w_out, v_norm_final):
    Bl, S, _ = x.shape
    assert S % 256 == 0 and S >= 3 * BLK, "the attention kernels tile the real tokens in 256-row blocks"
    Lp = S + BLK
    T = Bl * Lp
    tm = Lp // 4
    tq = Lp // 8
    depth = w_in.shape[0]
    me = 4 * lax.axis_index("x") + 2 * lax.axis_index("y") + lax.axis_index("c")

    def finish(handles, after, name):
        srcs, lands = _xchg_wait(handles, after, name)
        full = []
        for hd, src, land in zip(handles, srcs, lands):
            own = src if hd[4] == "gather" else lax.dynamic_index_in_dim(src, me, 0, keepdims=False)
            full.append(_own_slot(land, own, me))
        return full

    def weights_of(i):
        return [w_in[i].astype(BF16), w_uq[i].astype(BF16), w_ukv[i].astype(BF16), w_out[i].astype(BF16)]

    def packed(l_in, l_uq, l_ukv, l_out):
        cols = lambda t: jnp.transpose(t, (1, 0, 2)).reshape(t.shape[1], NDEV * t.shape[2])
        wkk, wkv = _pack_w_ukv(cols(l_ukv))
        return _pack_w_in(l_in), _pack_w_uq(cols(l_uq)), wkk, wkv, l_out.reshape(D, D)

    wb = weights_of(0)
    gat_a, tok = _xchg_start([wb[0], meta_tokens], ["gather", "gather"], "gather_start_0a")
    buckets = _bias_buckets(S)
    biases = _build_bias(buckets, _after(rel_bias_table, tok), sink_a, "build_bias")
    tb = 256 if T % 256 == 0 else tq
    tq_tab = jnp.tile(_rope_tables(S, Lp, NOPE, True), (1, Bl, 1))
    key_pad = jnp.where((jnp.arange(Lp) >= S + NMETA)[:, None] & (jnp.arange(LANES) == MASK_LANE)[None, :], NEG, 0.0)
    tk_tab = jnp.tile(jnp.concatenate([_rope_tables(S, Lp, 0, False), key_pad.astype(F32)[None]], axis=0), (1, Bl, 1))
    l_in, l_meta = finish(gat_a, biases[0], "gather_wait_0a")
    gat_b, tok = _xchg_start(wb[1:], ["gather"] * 3, "gather_start_0b", after=l_in)

    meta_f = jnp.transpose(l_meta, (1, 0, 2)).reshape(NMETA, D)
    tail = jnp.concatenate([meta_f, jnp.zeros((BLK - NMETA, D), F32)], axis=0)
    h = jnp.concatenate([x, jnp.broadcast_to(tail[None], (Bl, BLK, D))], axis=1).reshape(T, D)

    saved = []
    w_out_f = [None] * depth
    gat_next = None
    for i in range(depth):
        g_in, g_q, g_kv = norm_in[i][None], norm_q_lat[i][None], norm_kv_lat[i][None]
        g_a, g_b = norm_out_a[i][None], norm_out_b[i][None]
        if i == 0:
            proj = _inproj_fwd(h, _after(g_in, tok), _pack_w_in(l_in), tm,
                               f"inproj_fwd_{i}")
            l_uq, l_ukv, l_out = finish(gat_b, proj, "gather_wait_0b")
            wx, wq, wkk, wkv, w_out_f[i] = packed(l_in, l_uq, l_ukv, l_out)
        else:
            wx, wq, wkk, wkv, w_out_f[i] = packed(*finish(gat_next, h, f"gather_wait_{i}"))
            proj = _inproj_fwd(h, g_in, wx, tm, f"inproj_fwd_{i}")
        if i + 1 < depth:
            gat_next, tok = _xchg_start(weights_of(i + 1), ["gather"] * 4, f"gather_start_{i + 1}", after=w_out_f[i])
            g_q = _after(g_q, tok)
        q, k, v = _mla_prep_fwd(proj, g_q, g_kv, wq, wkk, wkv, tq_tab, tk_tab, tm, f"mla_prep_fwd_{i}")
        ya, lse_a = _win_fwd(proj, biases[i], Bl, S, f"win_fwd_{i}")
        yb, lse_b = _mla_fwd(q, k, v, Bl, Lp, f"mla_fwd_{i}")
        h_new = _out_fwd(ya, yb, proj, g_a, g_b, w_out_f[i], h, tm, f"out_fwd_{i}")
        saved.append((h, proj, q, k, v, ya, lse_a, yb, lse_b, wx, wq, wkk, wkv))
        h = h_new

    dh, loss_acc, dg_final = _loss_head(h, norm_final[None], loss_target, Bl, S, "loss_head")

    g_n_in, g_nq, g_nkv, g_na, g_nb = ([None] * depth for _ in range(5))
    dbias_all, sc_out, sc_rest = [None] * depth, [None] * depth, [None] * depth
    split = lambda t, n: jnp.transpose(t.reshape(t.shape[0], NDEV, n), (1, 0, 2)).astype(BF16)
    tok = None
    for i in reversed(range(depth)):
        h_in, proj, q, k, v, ya, lse_a, yb, lse_b, wx, wq, wkk, wkv = saved[i]
        g_in, g_q, g_kv = norm_in[i][None], norm_q_lat[i][None], norm_kv_lat[i][None]
        g_a, g_b = norm_out_a[i][None], norm_out_b[i][None]
        if tok is not None:
            g_a = _after(g_a, tok)
        dya, dyb, dga, dgb, g_w_out, g_na[i], g_nb[i] = _out_bwd(dh, ya, yb, proj, g_a, g_b, w_out_f[i], tb, f"out_bwd_{i}")
        sc_out[i], tok = _xchg_start([g_w_out.reshape(NDEV, D // NDEV, D).astype(BF16)], ["scatter"], f"scatter_start_{i}a")
        dq, dk, dv = _mla_bwd(q, k, v, yb, dyb, lse_b, tok, Bl, Lp, f"mla_bwd_{i}")
        dqa, dka, dva, dbias = _win_bwd(proj, biases[i], ya, dya, lse_a, Bl, S, f"win_bwd_{i}")
        dcq, dckv, dkr, dwq, dwkk, dwkv, g_nq[i], g_nkv[i] = _mla_prep_bwd(
            dq, dk, dv, proj, g_q, g_kv, wq, wkk, wkv, tq_tab, tk_tab, tm, f"mla_prep_bwd_{i}")
        pieces = (dqa, dka, dva, dga, dcq, dckv, dkr, dgb)
        dbias_all[i] = dbias
        rest = [split(_unpack_w_uq_grad(dwq), 768 // NDEV), split(_unpack_w_ukv_grad(dwkk, dwkv), 1024 // NDEV)]
        dh, dwx, g_n_in[i] = _inproj_bwd(pieces, h_in, g_in, wx, dh, tb, f"inproj_bwd_{i}")
        sc_rest[i], tok = _xchg_start([_unpack_w_in_grad(dwx).astype(BF16)] + rest, ["scatter"] * 3, f"scatter_start_{i}b")

    dh3 = dh.reshape(Bl, Lp, D)
    grad_x = dh3[:, :S]
    g_meta = jnp.transpose(jnp.sum(dh3[:, S:S + NMETA], axis=0).reshape(NMETA, NDEV, D // NDEV), (1, 0, 2))
    dtab, dsink = _table_grad(dbias_all, buckets, tok, "table_grad")
    two_d = lambda a: a.reshape(1, -1) if a.ndim == 1 else a
    small_w = [rel_bias_table, norm_in, sink_a, norm_q_lat, norm_kv_lat, norm_out_a, norm_out_b, norm_final]
    small_m = [m_rel_bias_table, m_norm_in, m_sink_a, m_norm_q_lat, m_norm_kv_lat, m_norm_out_a, m_norm_out_b, m_norm_final]
    small_v = [v_rel_bias_table, v_norm_in, v_sink_a, v_norm_q_lat, v_norm_kv_lat, v_norm_out_a, v_norm_out_b, v_norm_final]
    small_g = [jnp.transpose(dtab[:, :N_BUCKETS]), jnp.concatenate(g_n_in), dsink[:, :, 0], jnp.concatenate(g_nq),
               jnp.concatenate(g_nkv), jnp.concatenate(g_na), jnp.concatenate(g_nb), dg_final]
    sc_small, tok = _xchg_start(small_g + [loss_acc, g_meta], ["gather"] * (len(small_g) + 1) + ["scatter"],
                                "scatter_start_small")

    res = {}

    def update(n, r, w, m, v):
        outs = _adamw_param(r, w, m, v, f"adamw_{n}")
        res[n] = list(outs)
        return outs[0]

    r_out = [finish(sc_out[i], tok, f"scatter_wait_{i}a")[0] for i in range(depth)]
    done = update("w_out", r_out, w_out, m_w_out, v_w_out)
    r_in, r_uq, r_ukv = [None] * depth, [None] * depth, [None] * depth
    for i in reversed(range(depth)):
        r_in[i], r_uq[i], r_ukv[i] = finish(sc_rest[i], grad_x if i > 0 else done, f"scatter_wait_{i}b")
    r_small = finish(sc_small, r_in[0], "scatter_wait_small")
    update("w_in", r_in, w_in, m_w_in, v_w_in)
    update("w_uq", r_uq, w_uq, m_w_uq, v_w_uq)
    update("w_ukv", r_ukv, w_ukv, m_w_ukv, v_w_ukv)
    update("meta_tokens", [r_small[-1]], meta_tokens[None], m_meta_tokens[None], v_meta_tokens[None])
    res["meta_tokens"] = [o[0] for o in res["meta_tokens"]]
    names_small = ["rel_bias_table", "norm_in", "sink_a", "norm_q_lat", "norm_kv_lat", "norm_out_a", "norm_out_b", "norm_final"]
    outs = _adamw_small(r_small[:-2], [two_d(a) for a in small_w], [two_d(a) for a in small_m], [two_d(a) for a in small_v],
                        r_small[-2], "adamw_replicated")
    loss = outs[-1][0, 0]
    ns = len(names_small)
    for j, n in enumerate(names_small):
        res[n] = [outs[kk * ns + j].reshape(small_w[j].shape) for kk in range(4)]
    order = ["meta_tokens", "rel_bias_table", "norm_in", "w_in", "sink_a", "norm_q_lat", "w_uq", "norm_kv_lat", "w_ukv",
             "norm_out_a", "norm_out_b", "w_out", "norm_final"]
    return (loss, grad_x, *[res[n][kk] for kk in range(4) for n in order])
```

```python
import functools
import math

import numpy as np
import jax
import jax.numpy as jnp
from jax import lax
from jax.experimental import pallas as pl
from jax.experimental.pallas import tpu as pltpu

F32, BF16 = jnp.float32, jnp.bfloat16
D = 1024
NMETA = 16
BLK = 128
A_HEADS, A_KV, A_DH, A_W = 8, 2, 64, 512
B_HEADS, NOPE, ROPE, B_V, B_W = 8, 64, 32, 64, 512
Q_RANK, KV_RANK = 256, 128
IN_W = 2208
N_BUCKETS, MAX_DIST = 32, 128
THETA = 10000.0
EPS = 1e-6
NEG = -1e30
A_SCALE = A_DH ** -0.5
B_SCALE = (NOPE + ROPE) ** -0.5
LOG2E = math.log2(math.e)
MASK_LANE = NOPE + ROPE
SUM_LANE = B_V
LANES = 128
NDEV = 8
MESH = pl.DeviceIdType.MESH

NX = 2560
QA0, KA0, VA0, GA0, CQ0, CKV0, KR0, GB0 = 0, 512, 768, 1024, 1536, 1792, 1920, 2048

ADAM_LR, ADAM_B1, ADAM_B2, ADAM_EPS, ADAM_WD, ADAM_STEP = 0.001, 0.9, 0.999, 1e-08, 0.01, 10


def _dot(a, b):
    return jnp.dot(a, b, preferred_element_type=F32)


def _dot_nt(a, b):
    return lax.dot_general(a, b, (((1,), (1,)), ((), ())), preferred_element_type=F32)


def _dot_tn(a, b):
    return lax.dot_general(a, b, (((0,), (0,)), ((), ())), preferred_element_type=F32)


def _lane(shape):
    return lax.broadcasted_iota(jnp.int32, shape, len(shape) - 1)


def _rstd(x):
    return lax.rsqrt(jnp.mean(x * x, axis=-1, keepdims=True) + EPS)


def _rms_bwd(dn, x, r, g):
    z = dn * g
    dx = r * z - x * (r * r * r) * jnp.mean(z * x, axis=-1, keepdims=True)
    return dx, jnp.sum(dn * (x * r), axis=0, keepdims=True)


def _row(i):
    return lambda *ids: (ids[0], i)


def _const2(*ids):
    return (0, 0)


def _params(*sem):
    return pltpu.CompilerParams(dimension_semantics=sem)


SHARD_W = IN_W // NDEV


def _pack_w_in(shards):
    def nat(a, b):
        parts = []
        while a < b:
            p, e = a // SHARD_W, min(b, (a // SHARD_W + 1) * SHARD_W)
            parts.append(shards[p][:, a - p * SHARD_W:e - p * SHARD_W])
            a = e
        return parts
    dup = lambda a: nat(a, a + 64) * 2 + nat(a + 64, a + 128) * 2
    zeros = [jnp.zeros((shards.shape[1], LANES - ROPE), shards.dtype)]
    return jnp.concatenate(nat(0, 512) + dup(512) + dup(640) + nat(768, 1280) + nat(1280, 1536) + nat(1536, 1664)
                           + nat(1664, 1696) + zeros + nat(1696, 2208), axis=1)


_NAT_GROUPS = ((0, 512, QA0), (512, 576, KA0), (576, 640, KA0 + 128), (640, 704, VA0), (704, 768, VA0 + 128),
               (768, 1280, GA0), (1280, 1536, CQ0), (1536, 1664, CKV0), (1664, 1696, KR0), (1696, 2208, GB0))


def _unpack_w_in_grad(g):
    def nat(a, b):
        lo, _, pk = next(grp for grp in _NAT_GROUPS if grp[0] <= a < grp[1])
        t = g[:, pk + a - lo:pk + b - lo]
        return t + g[:, pk + 64 + a - lo:pk + 64 + b - lo] if 512 <= a < 768 else t
    cuts = sorted({c for grp in _NAT_GROUPS for c in grp[:2]} | {p * SHARD_W for p in range(NDEV + 1)})
    shards = [[] for _ in range(NDEV)]
    for a, b in zip(cuts[:-1], cuts[1:]):
        shards[a // SHARD_W].append(nat(a, b))
    return jnp.stack([jnp.concatenate(parts, axis=1) for parts in shards])


def _pack_w_uq(w):
    t = w.reshape(Q_RANK, B_HEADS, NOPE + ROPE)
    t = jnp.concatenate([t, jnp.zeros((Q_RANK, B_HEADS, LANES - NOPE - ROPE), w.dtype)], axis=-1)
    return t.reshape(Q_RANK, B_HEADS * LANES)


def _unpack_w_uq_grad(g):
    return g.reshape(Q_RANK, B_HEADS, LANES)[:, :, :NOPE + ROPE].reshape(Q_RANK, B_HEADS * (NOPE + ROPE))


def _pack_w_ukv(w):
    t = w.reshape(KV_RANK, B_HEADS, NOPE + B_V)
    z = jnp.zeros((KV_RANK, B_HEADS, LANES - NOPE), w.dtype)
    pad = lambda u: jnp.concatenate([u, z], axis=-1).reshape(KV_RANK, B_HEADS * LANES)
    return pad(t[:, :, :NOPE]), pad(t[:, :, NOPE:])


def _unpack_w_ukv_grad(gk, gv):
    head = lambda g: g.reshape(KV_RANK, B_HEADS, LANES)[:, :, :NOPE]
    return jnp.concatenate([head(gk), head(gv)], axis=-1).reshape(KV_RANK, B_HEADS * (NOPE + B_V))


def _t5_bucket(rel):
    nb = N_BUCKETS // 2
    max_exact = nb // 2
    ret = jnp.where(rel > 0, nb, 0)
    n = jnp.abs(rel)
    nf = jnp.maximum(n, 1).astype(F32)
    large = max_exact + (jnp.log(nf / max_exact) / math.log(MAX_DIST / max_exact) * (nb - max_exact)).astype(jnp.int32)
    large = jnp.minimum(large, nb - 1)
    return ret + jnp.where(n < max_exact, n, large)


def _bias_buckets(S):
    q = np.arange(BLK)[:, None]
    k = np.arange(4 * BLK)[None, :]
    is_meta_key = (k >= 3 * BLK) & (k < 3 * BLK + NMETA)
    mi = k - 3 * BLK
    rels, valids = [], []
    for shift in (0, BLK, 2 * BLK):
        rel_real = k - shift - q
        valid_real = (k < 3 * BLK) & (np.abs(rel_real) <= BLK)
        far = -(NMETA + MAX_DIST + BLK)
        rel_meta = (mi - (NMETA + q)) if shift == 0 else np.full_like(k + q, far)
        rels.append(np.where(is_meta_key, rel_meta, rel_real))
        valids.append(valid_real | is_meta_key)
    qm = q < NMETA
    rel_real = NMETA + k - q
    valid_real = (k < BLK) & (np.abs(rel_real) <= BLK) & qm
    rels.append(np.where(is_meta_key, mi - q, rel_real))
    valids.append(valid_real | is_meta_key)
    rel = jnp.asarray(np.stack(rels).astype(np.int32))
    valid = jnp.asarray(np.stack(valids))
    return jnp.where(valid, _t5_bucket(rel), -1).astype(jnp.int32)


def _rope_tables(S, Lp, off, passthrough):
    half = ROPE // 2
    r = np.arange(Lp)
    pos = np.where(r < S, NMETA + r, np.where(r < S + NMETA, r - S, 0)).astype(np.float32)
    freqs = THETA ** (-jnp.arange(half, dtype=F32) / half)
    ang = jnp.asarray(pos)[:, None] * freqs[None, :]
    cos, sin = jnp.cos(ang), jnp.sin(ang)
    z = lambda n: jnp.zeros((Lp, n), F32)
    head = jnp.ones((Lp, off), F32) if passthrough else z(off)
    c = jnp.concatenate([head, cos, cos, z(LANES - off - ROPE)], axis=1)
    s1 = jnp.concatenate([z(off), -sin, z(LANES - off - half)], axis=1)
    s2 = jnp.concatenate([z(off + half), sin, z(LANES - off - ROPE)], axis=1)
    return jnp.stack([c, s1, s2])


def _rope(x, t):
    return x * t[0] + pltpu.roll(x, LANES - 16, 1) * t[1] + pltpu.roll(x, 16, 1) * t[2]


def _rope_t(dy, t):
    return dy * t[0] + pltpu.roll(dy * t[1], 16, 1) + pltpu.roll(dy * t[2], LANES - 16, 1)


def _me_and_peers():
    x, y, c = lax.axis_index("x"), lax.axis_index("y"), lax.axis_index("c")
    flip = lambda v, b: 1 - v if b else v
    peers = [(flip(x, k & 4), flip(y, k & 2), flip(c, k & 1)) for k in range(1, NDEV)]
    return 4 * x + 2 * y + c, peers


def _xchg_start(srcs, kinds, name, after=None):
    n = len(srcs)
    lands = [lax.empty((NDEV,) + s.shape[-2:], s.dtype) for s in srcs]
    extra = [] if after is None else [after]

    def body(*refs):
        src_refs, land_refs = refs[:n], refs[n:2 * n]
        ssems, rsems = refs[2 * n + len(extra):3 * n + len(extra)], refs[3 * n + len(extra):4 * n + len(extra)]
        token = refs[6 * n + len(extra)]
        me, peers = _me_and_peers()
        for it in range(n):
            for k, (px, py, pc) in enumerate(peers):
                src = src_refs[it] if kinds[it] == "gather" else src_refs[it].at[4 * px + 2 * py + pc]
                pltpu.make_async_remote_copy(src_ref=src, dst_ref=land_refs[it].at[me], send_sem=ssems[it].at[k],
                                             recv_sem=rsems[it].at[k], device_id=(px, py, pc), device_id_type=MESH).start()
        token[...] = jnp.zeros_like(token)

    hbm = pl.BlockSpec(memory_space=pltpu.HBM)
    sem = pl.BlockSpec(memory_space=pltpu.SEMAPHORE)
    outs = pl.pallas_call(
        body, name=name,
        out_shape=tuple([pltpu.SemaphoreType.DMA((NDEV - 1,))] * (2 * n) + [pltpu.HBM(a.shape, a.dtype) for a in srcs + lands]
                        + [jax.ShapeDtypeStruct((8, LANES), F32)]),
        in_specs=[hbm] * (2 * n) + [pl.BlockSpec(memory_space=pl.ANY)] * len(extra),
        out_specs=tuple([sem] * (2 * n) + [hbm] * (2 * n) + [pl.BlockSpec(memory_space=pltpu.VMEM)]),
        input_output_aliases={i: 2 * n + i for i in range(2 * n)},
        compiler_params=pltpu.CompilerParams(has_side_effects=pltpu.SideEffectType.DATAFLOW_SIDE_EFFECTING),
    )(*[pltpu.with_memory_space_constraint(a, pltpu.HBM) for a in srcs + lands], *extra)
    handles = [(outs[it], outs[n + it], outs[2 * n + it], outs[3 * n + it], kinds[it]) for it in range(n)]
    return handles, outs[4 * n]


def _xchg_wait(handles, after, name):
    n = len(handles)

    def body(*refs):
        src_refs, land_refs = refs[:n], refs[n:2 * n]
        ssems, rsems = refs[2 * n:3 * n], refs[3 * n:4 * n]
        me, peers = _me_and_peers()
        for it in range(n):
            for k, (px, py, pc) in enumerate(peers):
                src = src_refs[it] if handles[it][4] == "gather" else src_refs[it].at[4 * px + 2 * py + pc]
                cp = pltpu.make_async_remote_copy(src_ref=src, dst_ref=land_refs[it].at[me], send_sem=ssems[it].at[k],
                                                  recv_sem=rsems[it].at[k], device_id=(px, py, pc), device_id_type=MESH)
                cp.wait_send()
                cp.wait_recv()

    hbm = pl.BlockSpec(memory_space=pltpu.HBM)
    sem = pl.BlockSpec(memory_space=pltpu.SEMAPHORE)
    srcs, lands = [h[2] for h in handles], [h[3] for h in handles]
    outs = pl.pallas_call(
        body, name=name, out_shape=tuple(pltpu.HBM(a.shape, a.dtype) for a in srcs + lands),
        in_specs=[hbm] * (2 * n) + [sem] * (2 * n) + [pl.BlockSpec(memory_space=pl.ANY)], out_specs=tuple([hbm] * (2 * n)),
        input_output_aliases={i: i for i in range(2 * n)},
        compiler_params=pltpu.CompilerParams(has_side_effects=pltpu.SideEffectType.DATAFLOW_SIDE_EFFECTING),
    )(*srcs, *lands, *[h[0] for h in handles], *[h[1] for h in handles], after)
    return list(outs[:n]), list(outs[n:])


def _own_slot(land, own, me):
    return lax.dynamic_update_slice(land, own[None].astype(land.dtype), (me, 0, 0))


def _after(x, token):
    return x + token[0, 0]


def _adamw_update(g, w, m, v):
    mn = ADAM_B1 * m + (1.0 - ADAM_B1) * g
    vn = ADAM_B2 * v + (1.0 - ADAM_B2) * (g * g)
    m_hat = mn / (1.0 - ADAM_B1 ** ADAM_STEP)
    v_hat = vn / (1.0 - ADAM_B2 ** ADAM_STEP)
    return -ADAM_LR * (m_hat / (jnp.sqrt(v_hat) + ADAM_EPS) + ADAM_WD * w), mn, vn


def _sum_partials(r_ref):
    g = r_ref[0].astype(F32)
    for p in range(1, NDEV):
        g = g + r_ref[p].astype(F32)
    return g


def _adamw_param(recvs, w, m, v, name):
    depth, r, c = w.shape
    tr = min(r, 256)
    nl = len(recvs)

    def body(*refs):
        r_refs = refs[:nl]
        w_ref, m_ref, v_ref, g_out, d_out, m_out, v_out = refs[nl:]
        for li in range(nl):
            @pl.when(pl.program_id(0) == li)
            def _(li=li):
                g = _sum_partials(r_refs[li])
                g_out[0] = g
                d_out[0], m_out[0], v_out[0] = _adamw_update(g, w_ref[0], m_ref[0], v_ref[0])

    blk = pl.BlockSpec((1, tr, c), lambda l, i: (l, i, 0))
    return pl.pallas_call(
        body, name=name, grid=(depth, r // tr),
        in_specs=[pl.BlockSpec((NDEV, tr, c), lambda l, i: (0, i, 0))] * nl + [blk, blk, blk],
        out_specs=(blk, blk, blk, blk), out_shape=(jax.ShapeDtypeStruct(w.shape, F32),) * 4,
        compiler_params=_params("arbitrary", "arbitrary"),
    )(*recvs, w, m, v)


def _adamw_small(recvs, ws, ms, vs, loss_parts, name):
    n = len(ws)

    def body(*refs):
        r_refs, w_refs, m_refs, v_refs = refs[:n], refs[n:2 * n], refs[2 * n:3 * n], refs[3 * n:4 * n]
        outs = refs[4 * n + 1:]
        for j in range(n):
            g = _sum_partials(r_refs[j])
            outs[j][...] = g
            outs[n + j][...], outs[2 * n + j][...], outs[3 * n + j][...] = _adamw_update(
                g, w_refs[j][...], m_refs[j][...], v_refs[j][...])
        outs[4 * n][...] = _sum_partials(refs[4 * n])

    vm = pl.BlockSpec(memory_space=pltpu.VMEM)
    return pl.pallas_call(
        body, name=name, in_specs=[vm] * (4 * n + 1), out_specs=tuple([vm] * (4 * n + 1)),
        out_shape=tuple(jax.ShapeDtypeStruct(a.shape, F32) for a in ws) * 4 + (jax.ShapeDtypeStruct(loss_parts.shape[1:], F32),),
    )(*recvs, *ws, *ms, *vs, loss_parts)


def _inproj_fwd(h, g, wx, tm, name):
    T = h.shape[0]

    def body(h_ref, g_ref, w_ref, o_ref):
        x = h_ref[...]
        u = (x * _rstd(x) * g_ref[...]).astype(BF16)
        o_ref[...] = _dot(u, w_ref[...])

    return pl.pallas_call(
        body, name=name, grid=(T // tm,),
        in_specs=[pl.BlockSpec((tm, D), _row(0)), pl.BlockSpec((1, D), _const2), pl.BlockSpec((D, NX), _const2)],
        out_specs=pl.BlockSpec((tm, NX), _row(0)), out_shape=jax.ShapeDtypeStruct((T, NX), F32),
        compiler_params=_params("parallel"),
    )(h, g, wx)


def _mla_prep_fwd(proj, gq, gkv, wq, wkk, wkv, tq, tk, tm, name):
    T = proj.shape[0]

    def body(cq_ref, ckv_ref, kr_ref, gq_ref, gkv_ref, wq_ref, wkk_ref, wkv_ref, tq_ref, tk_ref, q_ref, k_ref, v_ref):
        cq = cq_ref[...]
        q = _dot((cq * _rstd(cq) * gq_ref[...]).astype(BF16), wq_ref[...])
        tqv = tq_ref[...]
        lane = _lane((1, LANES))
        one = jnp.where(lane == MASK_LANE, 1.0, 0.0)
        for h in range(B_HEADS):
            cs = slice(LANES * h, LANES * (h + 1))
            q_ref[:, cs] = (_rope(q[:, cs], tqv) * (B_SCALE * LOG2E) + one).astype(BF16)
        tkv = tk_ref[...]
        ksh = pltpu.roll(_rope(kr_ref[...], tkv), NOPE, 1) + tkv[3]
        ckv = ckv_ref[...]
        cb = (ckv * _rstd(ckv) * gkv_ref[...]).astype(BF16)
        kn = _dot(cb, wkk_ref[...])
        vn = _dot(cb, wkv_ref[...])
        minus = jnp.where((lane == SUM_LANE) | (lane == SUM_LANE + 1), -1.0, 0.0)
        for h in range(B_HEADS):
            cs = slice(LANES * h, LANES * (h + 1))
            k_ref[:, cs] = (kn[:, cs] + ksh).astype(BF16)
            v_ref[:, cs] = (vn[:, cs] + minus).astype(BF16)

    tab = lambda n: pl.BlockSpec((n, tm, LANES), lambda i: (0, i, 0))
    wide = pl.BlockSpec((tm, 1024), _row(0))
    return pl.pallas_call(
        body, name=name, grid=(T // tm,),
        in_specs=[pl.BlockSpec((tm, 256), _row(CQ0 // 256)), pl.BlockSpec((tm, 128), _row(CKV0 // 128)),
                  pl.BlockSpec((tm, 128), _row(KR0 // 128)), pl.BlockSpec((1, Q_RANK), _const2),
                  pl.BlockSpec((1, KV_RANK), _const2), pl.BlockSpec((Q_RANK, 1024), _const2),
                  pl.BlockSpec((KV_RANK, 1024), _const2), pl.BlockSpec((KV_RANK, 1024), _const2), tab(3), tab(4)],
        out_specs=(wide, wide, wide), out_shape=(jax.ShapeDtypeStruct((T, 1024), BF16),) * 3,
        compiler_params=_params("parallel"),
    )(proj, proj, proj, gq, gkv, wq, wkk, wkv, tq, tk)


SINK_SLOT = 4 * BLK - 1


def _build_bias(buckets, table, sinks, name):
    depth = sinks.shape[0]

    def body(tab_ref, sink_ref, b_ref, *o_refs):
        bidx = b_ref[0]
        slot = _lane(bidx.shape)
        for h in range(A_HEADS):
            acc = jnp.full(bidx.shape, NEG, F32)
            for j in range(N_BUCKETS):
                acc = jnp.where(bidx == j, tab_ref[j, h] * LOG2E, acc)
            for l in range(depth):
                o_refs[l][0, h] = jnp.where(slot == SINK_SLOT, sink_ref[l, h] * LOG2E, acc)

    smem = pl.BlockSpec(memory_space=pltpu.SMEM)
    blk = pl.BlockSpec((1, A_HEADS, BLK, 4 * BLK), lambda i: (i, 0, 0, 0))
    return pl.pallas_call(
        body, name=name, grid=(4,),
        in_specs=[smem, smem, pl.BlockSpec((1, BLK, 4 * BLK), lambda i: (i, 0, 0))],
        out_specs=tuple([blk] * depth), out_shape=tuple([jax.ShapeDtypeStruct((4, A_HEADS, BLK, 4 * BLK), F32)] * depth),
        compiler_params=_params("parallel"),
    )(table, sinks, buckets)


def _win_base(n, NB):
    return jnp.where(n == NB, 0, jnp.clip(n - 1, 0, NB - 3))


def _win_variant(n, NB):
    return jnp.where(n == 0, 0, jnp.where(n < NB - 1, 1, jnp.where(n == NB - 1, 2, 3)))


def _win_kv(k_ref, v_ref, base, S, kvh):
    cs = slice(LANES * kvh, LANES * (kvh + 1))
    k2 = jnp.concatenate([k_ref[pl.ds(base, 3 * BLK), cs], k_ref[S:S + BLK, cs]], axis=0).astype(BF16)
    v2 = jnp.concatenate([v_ref[pl.ds(base, 3 * BLK), cs], v_ref[S:S + BLK, cs]], axis=0)
    lane = _lane(v2.shape)
    v2 = jnp.where(lane < A_DH, v2, jnp.where(lane < A_DH + 2, -1.0, 0.0)).astype(BF16)
    return k2, v2


def _win_fwd(proj, bias, Bl, S, name):
    T = proj.shape[0]
    Lp = S + BLK
    NB = S // BLK
    nblk = Lp // BLK
    G = A_HEADS // A_KV

    def body(q_ref, k_ref, v_ref, b_ref, o_ref, lse_ref):
        n = pl.program_id(1)
        base = pl.multiple_of(_win_base(n, NB) * BLK, BLK)
        lane = _lane((BLK, LANES))
        lo = lane < A_DH
        kv = [_win_kv(k_ref, v_ref, base, S, kvh) for kvh in range(A_KV)]
        scores = []
        for h in range(A_HEADS):
            qp = q_ref[:, LANES * (h // 2):LANES * (h // 2 + 1)]
            qm = (jnp.where(lo if h % 2 == 0 else ~lo, qp, 0.0) * (A_SCALE * LOG2E)).astype(BF16)
            scores.append(_dot_nt(qm, kv[h // G][0]) + b_ref[0, h])
        ms = [jnp.max(s, axis=-1, keepdims=True) for s in scores]
        es = [jnp.exp2(s - m).astype(BF16) for s, m in zip(scores, ms)]
        outs = []
        t = jnp.zeros((BLK, LANES), F32)
        for h in range(A_HEADS):
            o2 = _dot(es[h], kv[h // G][1])
            l = -jnp.sum(jnp.where(lane == A_DH, o2, 0.0), axis=-1, keepdims=True)
            outs.append(o2 * (1.0 / l))
            t = jnp.where(lane == h, ms[h] + jnp.log(l) * LOG2E, t)
        for j in range(A_HEADS // 2):
            o_ref[:, LANES * j:LANES * (j + 1)] = jnp.where(lo, outs[2 * j], pltpu.roll(outs[2 * j + 1], A_DH, 1))
        lse_ref[...] = t

    qrow = lambda b, n: (b * nblk + n, 0)
    return pl.pallas_call(
        body, name=name, grid=(Bl, nblk),
        in_specs=[pl.BlockSpec((BLK, A_W), qrow),
                  pl.BlockSpec((Lp, 256), lambda b, n: (b, KA0 // 256)), pl.BlockSpec((Lp, 256), lambda b, n: (b, VA0 // 256)),
                  pl.BlockSpec((1, A_HEADS, BLK, 4 * BLK), lambda b, n: (_win_variant(n, NB), 0, 0, 0))],
        out_specs=(pl.BlockSpec((BLK, A_W), qrow), pl.BlockSpec((BLK, LANES), qrow)),
        out_shape=(jax.ShapeDtypeStruct((T, A_W), F32), jax.ShapeDtypeStruct((T, LANES), F32)),
        compiler_params=_params("parallel", "arbitrary"),
    )(proj, proj, proj, bias)


def _q_tiles(Lp):
    S = Lp - BLK
    return S // 256, S


def _mla_fwd(q, k, v, Bl, Lp, name):
    T = q.shape[0]
    n_big, S = _q_tiles(Lp)

    def body(q_ref, k_ref, v_ref, o_ref, lse_ref):
        def qtiles(starts, nq):
            lane = _lane((nq, LANES))
            cols = [slice(LANES * hh, LANES * (hh + 1)) for hh in range(2)]
            scores = [[_dot_nt(q_ref[pl.ds(r0, nq), cs], k_ref[:, cs]) for cs in cols] for r0 in starts]
            for r0, tile_scores in zip(starts, scores):
                outs = []
                t = jnp.zeros((nq, LANES), F32)
                for hh in range(2):
                    cs, s = cols[hh], tile_scores[hh]
                    m = jnp.max(s, axis=-1, keepdims=True)
                    o2 = _dot(jnp.exp2(s - m).astype(BF16), v_ref[:, cs])
                    l = -jnp.sum(jnp.where(lane == SUM_LANE, o2, 0.0), axis=-1, keepdims=True)
                    outs.append(o2 * (1.0 / l))
                    t = jnp.where(lane == hh, m + jnp.log(l) * LOG2E, t)
                o_ref[pl.ds(r0, nq), :] = jnp.where(lane < B_V, outs[0], pltpu.roll(outs[1], B_V, 1))
                lse_ref[pl.ds(r0, nq), :] = t

        def step(i, c):
            qtiles([pl.multiple_of(i * 512, 256), pl.multiple_of(i * 512 + 256, 256)], 256)
            return c

        lax.fori_loop(0, n_big // 2, step, 0)
        if n_big % 2:
            qtiles([(n_big - 1) * 256], 256)
        qtiles([S], BLK)

    pair = lambda b, j: (b, j)
    wide, narrow = pl.BlockSpec((Lp, 256), pair), pl.BlockSpec((Lp, LANES), pair)
    return pl.pallas_call(
        body, name=name, grid=(Bl, B_HEADS // 2), in_specs=[wide, wide, wide], out_specs=(narrow, narrow),
        out_shape=(jax.ShapeDtypeStruct((T, B_W), F32), jax.ShapeDtypeStruct((T, B_W), F32)),
        compiler_params=_params("parallel", "parallel"),
    )(q, k, v)


def _gated(y, gate, gain):
    r = _rstd(y)
    nrm = y * r
    sg = jax.nn.sigmoid(gate)
    return r, nrm, sg, nrm * gain, gate * sg


def _out_fwd(ya, yb, proj, na, nb, wout, h, tm, name):
    T = h.shape[0]

    def body(ya_ref, yb_ref, ga_ref, gb_ref, na_ref, nb_ref, w_ref, h_ref, o_ref):
        _, _, _, n_a, sl_a = _gated(ya_ref[...], ga_ref[...], na_ref[...])
        _, _, _, n_b, sl_b = _gated(yb_ref[...], gb_ref[...], nb_ref[...])
        acc = _dot((n_a * sl_a).astype(BF16), w_ref[0:A_W, :]) + _dot((n_b * sl_b).astype(BF16), w_ref[A_W:A_W + B_W, :])
        o_ref[...] = h_ref[...] + acc

    half = pl.BlockSpec((tm, 512), _row(0))
    return pl.pallas_call(
        body, name=name, grid=(T // tm,),
        in_specs=[half, half, pl.BlockSpec((tm, 512), _row(GA0 // 512)), pl.BlockSpec((tm, 512), _row(GB0 // 512)),
                  pl.BlockSpec((1, 512), _const2), pl.BlockSpec((1, 512), _const2), pl.BlockSpec((D, D), _const2),
                  pl.BlockSpec((tm, D), _row(0))],
        out_specs=pl.BlockSpec((tm, D), _row(0)), out_shape=jax.ShapeDtypeStruct((T, D), F32),
        compiler_params=_params("parallel"),
    )(ya, yb, proj, proj, na, nb, wout, h)


def _loss_head(h, gf, target, Bl, S, name):
    T = h.shape[0]
    nblk = (S + BLK) // BLK
    NB = S // BLK

    def body(h_ref, g_ref, t_ref, dh_ref, loss_ref, dg_ref):
        b, n = pl.program_id(0), pl.program_id(1)

        @pl.when((b == 0) & (n == 0))
        def _():
            loss_ref[...] = jnp.zeros_like(loss_ref)
            dg_ref[...] = jnp.zeros_like(dg_ref)

        @pl.when(n < NB)
        def _():
            x = h_ref[...]
            g = g_ref[...]
            r = _rstd(x)
            err = x * r * g - t_ref[0]
            loss_ref[...] += 0.5 * jnp.sum(jnp.mean(err * err, axis=-1, keepdims=True))
            dx, dg = _rms_bwd(err * (1.0 / D), x, r, g)
            dh_ref[...] = dx
            dg_ref[...] += dg

        @pl.when(n >= NB)
        def _():
            dh_ref[...] = jnp.zeros_like(dh_ref)

    return pl.pallas_call(
        body, name=name, grid=(Bl, nblk),
        in_specs=[pl.BlockSpec((BLK, D), lambda b, n: (b * nblk + n, 0)), pl.BlockSpec((1, D), lambda b, n: (0, 0)),
                  pl.BlockSpec((1, BLK, D), lambda b, n: (b, jnp.minimum(n, NB - 1), 0))],
        out_specs=(pl.BlockSpec((BLK, D), lambda b, n: (b * nblk + n, 0)), pl.BlockSpec((8, LANES), lambda b, n: (0, 0)),
                   pl.BlockSpec((1, D), lambda b, n: (0, 0))),
        out_shape=(jax.ShapeDtypeStruct((T, D), F32), jax.ShapeDtypeStruct((8, LANES), F32), jax.ShapeDtypeStruct((1, D), F32)),
        compiler_params=_params("arbitrary", "arbitrary"),
    )(h, gf, target)


def _out_bwd(dh, ya, yb, proj, na, nb, wout, tm, name):
    T = dh.shape[0]

    def body(dh_ref, ya_ref, yb_ref, ga_ref, gb_ref, na_ref, nb_ref, w_ref,
             dya_ref, dyb_ref, dga_ref, dgb_ref, dw_ref, dna_ref, dnb_ref):
        @pl.when(pl.program_id(0) == 0)
        def _():
            dw_ref[...] = jnp.zeros_like(dw_ref)
            dna_ref[...] = jnp.zeros_like(dna_ref)
            dnb_ref[...] = jnp.zeros_like(dnb_ref)

        dhb = dh_ref[...].astype(BF16)
        dy = _dot_nt(dhb, w_ref[...])
        parts = ((ya_ref, ga_ref, na_ref, dya_ref, dga_ref, dna_ref, 0), (yb_ref, gb_ref, nb_ref, dyb_ref, dgb_ref, dnb_ref, A_W))
        for y_ref, gate_ref, gain_ref, dy_out, dgate_out, dgain_out, c0 in parts:
            y, gate, gain = y_ref[...], gate_ref[...], gain_ref[...]
            r, nrm, sg, n_g, sl = _gated(y, gate, gain)
            dyp = dy[:, c0:c0 + 512]
            dgate_out[...] = dyp * n_g * (sg * (1.0 + gate * (1.0 - sg)))
            dx, dgain = _rms_bwd(dyp * sl, y, r, gain)
            dy_out[...] = dx
            dgain_out[...] += dgain
            dw_ref[c0:c0 + 512, :] += _dot_tn((n_g * sl).astype(BF16), dhb)

    half = pl.BlockSpec((tm, 512), _row(0))
    vec = pl.BlockSpec((1, 512), _const2)
    return pl.pallas_call(
        body, name=name, grid=(T // tm,),
        in_specs=[pl.BlockSpec((tm, D), _row(0)), half, half, pl.BlockSpec((tm, 512), _row(GA0 // 512)),
                  pl.BlockSpec((tm, 512), _row(GB0 // 512)), vec, vec, pl.BlockSpec((D, D), _const2)],
        out_specs=(half, half, half, half, pl.BlockSpec((D, D), _const2), vec, vec),
        out_shape=(jax.ShapeDtypeStruct((T, 512), F32),) * 4 + (jax.ShapeDtypeStruct((D, D), F32),)
        + (jax.ShapeDtypeStruct((1, 512), F32),) * 2,
        compiler_params=_params("arbitrary"),
    )(dh, ya, yb, proj, proj, na, nb, wout)


def _mla_bwd(q, k, v, o, do, lse, after, Bl, Lp, name):
    T = q.shape[0]
    n_big, S = _q_tiles(Lp)

    def body(q_ref, k_ref, v_ref, o_ref, do_ref, lse_ref, _, dq_ref, dk_ref, dv_ref):
        dk_ref[...] = jnp.zeros_like(dk_ref)
        dv_ref[...] = jnp.zeros_like(dv_ref)

        def qtile(r0, nq):
            lane = _lane((nq, LANES))
            dop = do_ref[pl.ds(r0, nq), :]
            op = o_ref[pl.ds(r0, nq), :]
            lsev = lse_ref[pl.ds(r0, nq), :]
            for hh in range(2):
                cs = slice(LANES * hh, LANES * (hh + 1))
                dom = jnp.where((lane < B_V) if hh == 0 else (lane >= B_V), dop, 0.0)
                delta = jnp.sum(dom * op, axis=-1, keepdims=True)
                d_hi = delta.astype(BF16).astype(F32)
                x = dom if hh == 0 else pltpu.roll(dom, B_V, 1)
                domx = jnp.where(lane == SUM_LANE, d_hi, jnp.where(lane == SUM_LANE + 1, delta - d_hi, x)).astype(BF16)
                qh, kh, vh = q_ref[pl.ds(r0, nq), cs], k_ref[:, cs], v_ref[:, cs]
                lse_h = jnp.sum(jnp.where(lane == hh, lsev, 0.0), axis=-1, keepdims=True)
                p = jnp.exp2(_dot_nt(qh, kh) - lse_h)
                ds = (p * _dot_nt(domx, vh)).astype(BF16)
                dq_ref[pl.ds(r0, nq), cs] = _dot(ds, kh) * B_SCALE
                dk_ref[:, cs] += _dot_tn(ds, qh)
                dv_ref[:, cs] += _dot_tn(p.astype(BF16), domx)

        def step(i, c):
            qtile(pl.multiple_of(i * 512, 256), 256)
            qtile(pl.multiple_of(i * 512 + 256, 256), 256)
            return c

        lax.fori_loop(0, n_big // 2, step, 0)
        if n_big % 2:
            qtile((n_big - 1) * 256, 256)
        qtile(S, BLK)
        dk_ref[...] = dk_ref[...] * (1.0 / LOG2E)

    pair = lambda b, j: (b, j)
    wide, narrow = pl.BlockSpec((Lp, 256), pair), pl.BlockSpec((Lp, LANES), pair)
    return pl.pallas_call(
        body, name=name, grid=(Bl, B_HEADS // 2),
        in_specs=[wide, wide, wide, narrow, narrow, narrow, pl.BlockSpec(memory_space=pl.ANY)],
        out_specs=(wide, wide, wide), out_shape=(jax.ShapeDtypeStruct((T, 1024), F32),) * 3,
        compiler_params=_params("parallel", "parallel"),
    )(q, k, v, o, do, lse, after)


def _win_bwd(proj, bias, ya, dya, lse, Bl, S, name):
    T = proj.shape[0]
    Lp = S + BLK
    NB = S // BLK
    nblk = Lp // BLK
    G = A_HEADS // A_KV

    def body(q_ref, k_ref, v_ref, b_ref, o_ref, do_ref, lse_ref, dq_ref, dk_ref, dv_ref, db_ref):
        n = pl.program_id(1)

        @pl.when(n == 0)
        def _():
            dk_ref[...] = jnp.zeros_like(dk_ref)
            dv_ref[...] = jnp.zeros_like(dv_ref)

        @pl.when((n == 0) | (n == 1) | (n == NB - 1) | (n == NB))
        def _():
            db_ref[...] = jnp.zeros_like(db_ref)

        base = pl.multiple_of(_win_base(n, NB) * BLK, BLK)
        lane = _lane((BLK, LANES))
        lo = lane < A_DH
        lsev = lse_ref[...]
        kv = [_win_kv(k_ref, v_ref, base, S, kvh) for kvh in range(A_KV)]
        qms, pbs, domxs, dsls = [], [], [], []
        for h in range(A_HEADS):
            k2, v2 = kv[h // G]
            ps = slice(LANES * (h // 2), LANES * (h // 2 + 1))
            qm = (jnp.where(lo if h % 2 == 0 else ~lo, q_ref[:, ps], 0.0) * (A_SCALE * LOG2E)).astype(BF16)
            lse_h = jnp.sum(jnp.where(lane == h, lsev, 0.0), axis=-1, keepdims=True)
            p = jnp.exp2(_dot_nt(qm, k2) + b_ref[0, h] - lse_h)
            dop, op = do_ref[:, ps], o_ref[:, ps]
            if h % 2 == 1:
                dop, op = pltpu.roll(dop, A_DH, 1), pltpu.roll(op, A_DH, 1)
            dom = jnp.where(lo, dop, 0.0)
            delta = jnp.sum(dom * op, axis=-1, keepdims=True)
            d_hi = delta.astype(BF16).astype(F32)
            domx = jnp.where(lane == A_DH, d_hi, jnp.where(lane == A_DH + 1, delta - d_hi, dom)).astype(BF16)
            dsc = p * _dot_nt(domx, v2)
            db_ref[0, 0, h] += dsc
            qms.append(qm)
            pbs.append(p.astype(BF16))
            domxs.append(domx)
            dsls.append(dsc.astype(BF16))
        dqs = [_dot(dsls[h], kv[h // G][0]) * A_SCALE for h in range(A_HEADS)]
        for kvh in range(A_KV):
            cs = slice(LANES * kvh, LANES * (kvh + 1))
            group = range(kvh * G, (kvh + 1) * G)
            stack = lambda parts: jnp.concatenate([parts[h] for h in group], axis=0)
            dk_acc = _dot_tn(stack(dsls), stack(qms)) * (1.0 / LOG2E)
            dv_acc = _dot_tn(stack(pbs), stack(domxs))
            dv_acc = jnp.where(_lane(dv_acc.shape) < A_DH, dv_acc, 0.0)
            dk_ref[pl.ds(base, 3 * BLK), cs] += dk_acc[0:3 * BLK]
            dk_ref[S:S + BLK, cs] += dk_acc[3 * BLK:4 * BLK]
            dv_ref[pl.ds(base, 3 * BLK), cs] += dv_acc[0:3 * BLK]
            dv_ref[S:S + BLK, cs] += dv_acc[3 * BLK:4 * BLK]
        for j in range(A_HEADS // 2):
            dq_ref[:, LANES * j:LANES * (j + 1)] = jnp.where(lo, dqs[2 * j], dqs[2 * j + 1])

    qrow = lambda b, n: (b * nblk + n, 0)
    kvs = pl.BlockSpec((Lp, 256), lambda b, n: (b, 0))
    return pl.pallas_call(
        body, name=name, grid=(Bl, nblk),
        in_specs=[pl.BlockSpec((BLK, A_W), qrow),
                  pl.BlockSpec((Lp, 256), lambda b, n: (b, KA0 // 256)), pl.BlockSpec((Lp, 256), lambda b, n: (b, VA0 // 256)),
                  pl.BlockSpec((1, A_HEADS, BLK, 4 * BLK), lambda b, n: (_win_variant(n, NB), 0, 0, 0)),
                  pl.BlockSpec((BLK, A_W), qrow), pl.BlockSpec((BLK, A_W), qrow), pl.BlockSpec((BLK, LANES), qrow)],
        out_specs=(pl.BlockSpec((BLK, A_W), qrow), kvs, kvs,
                   pl.BlockSpec((1, 1, A_HEADS, BLK, 4 * BLK), lambda b, n: (b, _win_variant(n, NB), 0, 0, 0))),
        out_shape=(jax.ShapeDtypeStruct((T, A_W), F32), jax.ShapeDtypeStruct((T, 256), F32),
                   jax.ShapeDtypeStruct((T, 256), F32), jax.ShapeDtypeStruct((Bl, 4, A_HEADS, BLK, 4 * BLK), F32)),
        compiler_params=_params("parallel", "arbitrary"),
    )(proj, proj, proj, bias, ya, dya, lse)


def _table_grad(dbias_list, buckets, after, name):
    nl = len(dbias_list)
    Bl = dbias_list[0].shape[0]

    def body(*refs):
        d_refs = refs[:nl]
        b_ref, _, o_ref, s_ref, acc, part = refs[nl:]
        v, b = pl.program_id(0), pl.program_id(1)

        @pl.when((v == 0) & (b == 0))
        def _():
            o_ref[...] = jnp.zeros_like(o_ref)
            s_ref[...] = jnp.zeros_like(s_ref)

        slot = _lane((BLK, 4 * BLK))
        for l in range(nl):
            for h in range(A_HEADS):
                col = jnp.sum(jnp.where(slot == SINK_SLOT, d_refs[l][0, 0, h], 0.0), axis=0, keepdims=True)
                s_ref[l, h:h + 1, :] += jnp.sum(col, axis=1, keepdims=True)

        tot = d_refs[0][0, 0]
        for r in d_refs[1:]:
            tot = tot + r[0, 0]

        @pl.when(b == 0)
        def _():
            acc[...] = tot

        @pl.when(b > 0)
        def _():
            acc[...] += tot

        @pl.when(b == Bl - 1)
        def _():
            bidx = b_ref[0]

            def step(j, c):
                mask = bidx == j
                for h in range(A_HEADS):
                    part[h, j] = jnp.sum(jnp.where(mask, acc[h], 0.0).reshape(BLK // 8, 8, 4 * BLK), axis=0)
                return c

            lax.fori_loop(0, N_BUCKETS, step, 0)
            rows = lax.broadcasted_iota(jnp.int32, (N_BUCKETS, LANES), 0)
            lanes = _lane((N_BUCKETS, LANES))
            for h in range(A_HEADS):
                col = jnp.sum(jnp.sum(part[h], axis=1), axis=-1, keepdims=True)
                o_ref[h:h + 1, :] += jnp.sum(jnp.where(rows == lanes, col, 0.0), axis=0, keepdims=True)

    return pl.pallas_call(
        body, name=name, grid=(4, Bl),
        in_specs=[pl.BlockSpec((1, 1, A_HEADS, BLK, 4 * BLK), lambda v, b: (b, v, 0, 0, 0))] * nl
        + [pl.BlockSpec((1, BLK, 4 * BLK), lambda v, b: (v, 0, 0)), pl.BlockSpec(memory_space=pl.ANY)],
        out_specs=(pl.BlockSpec((8, LANES), lambda v, b: (0, 0)), pl.BlockSpec((nl, 8, LANES), lambda v, b: (0, 0, 0))),
        out_shape=(jax.ShapeDtypeStruct((8, LANES), F32), jax.ShapeDtypeStruct((nl, 8, LANES), F32)),
        scratch_shapes=[pltpu.VMEM((A_HEADS, BLK, 4 * BLK), F32), pltpu.VMEM((A_HEADS, N_BUCKETS, 8, 4 * BLK), F32)],
        compiler_params=_params("arbitrary", "arbitrary"),
    )(*dbias_list, buckets, after)


def _mla_prep_bwd(dq, dk, dv, proj, gq, gkv, wq, wkk, wkv, tq, tk, tm, name):
    T = proj.shape[0]

    def body(dq_ref, dk_ref, dv_ref, cq_ref, ckv_ref, gq_ref, gkv_ref, wq_ref, wkk_ref, wkv_ref, tq_ref, tk_ref,
             dcq_ref, dckv_ref, dkr_ref, dwq_ref, dwkk_ref, dwkv_ref, dgq_ref, dgkv_ref):
        @pl.when(pl.program_id(0) == 0)
        def _():
            for r in (dwq_ref, dwkk_ref, dwkv_ref, dgq_ref, dgkv_ref):
                r[...] = jnp.zeros_like(r)

        tqv = tq_ref[...]
        dqp = jnp.concatenate([_rope_t(dq_ref[:, LANES * h:LANES * (h + 1)], tqv) for h in range(B_HEADS)],
                              axis=1).astype(BF16)
        cq, gq_ = cq_ref[...], gq_ref[...]
        rq = _rstd(cq)
        dwq_ref[...] += _dot_tn((cq * rq * gq_).astype(BF16), dqp)
        dx, dg = _rms_bwd(_dot_nt(dqp, wq_ref[...]), cq, rq, gq_)
        dcq_ref[...] = dx
        dgq_ref[...] += dg

        dkv_ = dk_ref[...]
        dks = dkv_[:, 0:LANES]
        for h in range(1, B_HEADS):
            dks = dks + dkv_[:, LANES * h:LANES * (h + 1)]
        dkr_ref[...] = _rope_t(pltpu.roll(dks, NOPE, 1), tk_ref[...])
        dkb = dkv_.astype(BF16)
        dvb = dv_ref[...].astype(BF16)
        ckv, gkv_ = ckv_ref[...], gkv_ref[...]
        rk = _rstd(ckv)
        cb = (ckv * rk * gkv_).astype(BF16)
        dwkk_ref[...] += _dot_tn(cb, dkb)
        dwkv_ref[...] += _dot_tn(cb, dvb)
        dx, dg = _rms_bwd(_dot_nt(dkb, wkk_ref[...]) + _dot_nt(dvb, wkv_ref[...]), ckv, rk, gkv_)
        dckv_ref[...] = dx
        dgkv_ref[...] += dg

    tab = lambda n: pl.BlockSpec((n, tm, LANES), lambda i: (0, i, 0))
    wide = pl.BlockSpec((tm, 1024), _row(0))
    return pl.pallas_call(
        body, name=name, grid=(T // tm,),
        in_specs=[wide, wide, wide, pl.BlockSpec((tm, 256), _row(CQ0 // 256)),
                  pl.BlockSpec((tm, 128), _row(CKV0 // 128)), pl.BlockSpec((1, Q_RANK), _const2),
                  pl.BlockSpec((1, KV_RANK), _const2), pl.BlockSpec((Q_RANK, 1024), _const2),
                  pl.BlockSpec((KV_RANK, 1024), _const2), pl.BlockSpec((KV_RANK, 1024), _const2), tab(3), tab(4)],
        out_specs=(pl.BlockSpec((tm, 256), _row(0)), pl.BlockSpec((tm, 128), _row(0)), pl.BlockSpec((tm, 128), _row(0)),
                   pl.BlockSpec((Q_RANK, 1024), _const2), pl.BlockSpec((KV_RANK, 1024), _const2),
                   pl.BlockSpec((KV_RANK, 1024), _const2), pl.BlockSpec((1, Q_RANK), _const2),
                   pl.BlockSpec((1, KV_RANK), _const2)),
        out_shape=(jax.ShapeDtypeStruct((T, 256), F32), jax.ShapeDtypeStruct((T, 128), F32),
                   jax.ShapeDtypeStruct((T, 128), F32), jax.ShapeDtypeStruct((Q_RANK, 1024), F32),
                   jax.ShapeDtypeStruct((KV_RANK, 1024), F32), jax.ShapeDtypeStruct((KV_RANK, 1024), F32),
                   jax.ShapeDtypeStruct((1, Q_RANK), F32), jax.ShapeDtypeStruct((1, KV_RANK), F32)),
        compiler_params=_params("arbitrary"),
    )(dq, dk, dv, proj, proj, gq, gkv, wq, wkk, wkv, tq, tk)


def _inproj_bwd(pieces, h, g, wx, dh_out, tm, name):
    T = h.shape[0]
    nsteps = T // tm
    CH = 512

    def body(*refs):
        p_refs = refs[:len(pieces)]
        h_ref, g_ref, w_ref, dho_ref, dh_ref, dw_hbm, dg_ref, acc, sem = refs[len(pieces):]
        i = pl.program_id(0)

        @pl.when(i == 0)
        def _():
            acc[...] = jnp.zeros_like(acc)
            dg_ref[...] = jnp.zeros_like(dg_ref)

        dp = jnp.concatenate([r[...].astype(BF16) for r in p_refs], axis=1)
        x, gain = h_ref[...], g_ref[...]
        r = _rstd(x)
        u = (x * r * gain).astype(BF16)
        for c in range(0, NX, CH):
            acc[:, c:c + CH] += _dot_tn(u, dp[:, c:c + CH])
        dx, dg = _rms_bwd(_dot_nt(dp, w_ref[...]), x, r, gain)
        dh_ref[...] = dho_ref[...] + dx
        dg_ref[...] += dg

        @pl.when(i == nsteps - 1)
        def _():
            cp = pltpu.make_async_copy(acc, dw_hbm, sem)
            cp.start()
            cp.wait()

    return pl.pallas_call(
        body, name=name, grid=(nsteps,),
        in_specs=[pl.BlockSpec((tm, p.shape[1]), _row(0)) for p in pieces]
        + [pl.BlockSpec((tm, D), _row(0)), pl.BlockSpec((1, D), _const2), pl.BlockSpec((D, NX), _const2),
           pl.BlockSpec((tm, D), _row(0))],
        out_specs=(pl.BlockSpec((tm, D), _row(0)), pl.BlockSpec(memory_space=pl.ANY), pl.BlockSpec((1, D), _const2)),
        out_shape=(jax.ShapeDtypeStruct((T, D), F32), jax.ShapeDtypeStruct((D, NX), F32), jax.ShapeDtypeStruct((1, D), F32)),
        scratch_shapes=[pltpu.VMEM((D, NX), F32), pltpu.SemaphoreType.DMA(())],
        compiler_params=_params("arbitrary"),
    )(*pieces, h, g, wx, dh_out)


def kernel(x, meta_tokens, rel_bias_table, norm_in, w_in, sink_a, norm_q_lat, w_uq, norm_kv_lat, w_ukv, norm_out_a, norm_out_b, w_out, norm_final, loss_target, m_meta_tokens, m_rel_bias_table, m_norm_in, m_w_in, m_sink_a, m_norm_q_lat, m_w_uq, m_norm_kv_lat, m_w_ukv, m_norm_out_a, m_norm_out_b, m_w_out, m_norm_final, v_meta_tokens, v_rel_bias_table, v_norm_in, v_w_in, v_sink_a, v_norm_q_lat, v_w_uq, v_norm_kv_lat, v_w_ukv, v_norm_out_a, v_norm_out_b, v_w_out, v_norm_final):
    Bl, S, _ = x.shape
    assert S % 256 == 0 and S >= 3 * BLK, "the attention kernels tile the real tokens in 256-row blocks"
    Lp = S + BLK
    T = Bl * Lp
    tm = Lp // 4
    tq = Lp // 8
    depth = w_in.shape[0]
    me = 4 * lax.axis_index("x") + 2 * lax.axis_index("y") + lax.axis_index("c")

    def finish(handles, after, name):
        srcs, lands = _xchg_wait(handles, after, name)
        full = []
        for hd, src, land in zip(handles, srcs, lands):
            own = src if hd[4] == "gather" else lax.dynamic_index_in_dim(src, me, 0, keepdims=False)
            full.append(_own_slot(land, own, me))
        return full

    def weights_of(i):
        return [w_in[i].astype(BF16), w_uq[i].astype(BF16), w_ukv[i].astype(BF16), w_out[i].astype(BF16)]

    def packed(l_in, l_uq, l_ukv, l_out):
        cols = lambda t: jnp.transpose(t, (1, 0, 2)).reshape(t.shape[1], NDEV * t.shape[2])
        wkk, wkv = _pack_w_ukv(cols(l_ukv))
        return _pack_w_in(l_in), _pack_w_uq(cols(l_uq)), wkk, wkv, l_out.reshape(D, D)

    wb = weights_of(0)
    gat_m, tok_m = _xchg_start([meta_tokens], ["gather"], "gather_start_meta")
    gat_a, tok = _xchg_start([wb[0]], ["gather"], "gather_start_0a", after=tok_m)
    buckets = _bias_buckets(S)
    biases = _build_bias(buckets, _after(rel_bias_table, tok), sink_a, "build_bias")
    tb = 256 if T % 256 == 0 else tq
    tq_tab = jnp.tile(_rope_tables(S, Lp, NOPE, True), (1, Bl, 1))
    key_pad = jnp.where((jnp.arange(Lp) >= S + NMETA)[:, None] & (jnp.arange(LANES) == MASK_LANE)[None, :], NEG, 0.0)
    tk_tab = jnp.tile(jnp.concatenate([_rope_tables(S, Lp, 0, False), key_pad.astype(F32)[None]], axis=0), (1, Bl, 1))
    (l_meta,) = finish(gat_m, biases[0], "gather_wait_meta")
    meta_f = jnp.transpose(l_meta, (1, 0, 2)).reshape(NMETA, D)
    tail = jnp.concatenate([meta_f, jnp.zeros((BLK - NMETA, D), F32)], axis=0)
    h = jnp.concatenate([x, jnp.broadcast_to(tail[None], (Bl, BLK, D))], axis=1).reshape(T, D)
    (l_in,) = finish(gat_a, h, "gather_wait_0a")
    gat_b, tok = _xchg_start(wb[1:], ["gather"] * 3, "gather_start_0b", after=l_in)

    saved = []
    w_out_f = [None] * depth
    gat_next = None
    for i in range(depth):
        g_in, g_q, g_kv = norm_in[i][None], norm_q_lat[i][None], norm_kv_lat[i][None]
        g_a, g_b = norm_out_a[i][None], norm_out_b[i][None]
        if i == 0:
            proj = _inproj_fwd(h, _after(g_in, tok), _pack_w_in(l_in), tm,
                               f"inproj_fwd_{i}")
            l_uq, l_ukv, l_out = finish(gat_b, proj, "gather_wait_0b")
            wx, wq, wkk, wkv, w_out_f[i] = packed(l_in, l_uq, l_ukv, l_out)
        else:
            wx, wq, wkk, wkv, w_out_f[i] = packed(*finish(gat_next, h, f"gather_wait_{i}"))
            proj = _inproj_fwd(h, g_in, wx, tm, f"inproj_fwd_{i}")
        if i + 1 < depth:
            gat_next, tok = _xchg_start(weights_of(i + 1), ["gather"] * 4, f"gather_start_{i + 1}", after=w_out_f[i])
            g_q = _after(g_q, tok)
        q, k, v = _mla_prep_fwd(proj, g_q, g_kv, wq, wkk, wkv, tq_tab, tk_tab, tm, f"mla_prep_fwd_{i}")
        ya, lse_a = _win_fwd(proj, biases[i], Bl, S, f"win_fwd_{i}")
        yb, lse_b = _mla_fwd(q, k, v, Bl, Lp, f"mla_fwd_{i}")
        h_new = _out_fwd(ya, yb, proj, g_a, g_b, w_out_f[i], h, tm, f"out_fwd_{i}")
        saved.append((h, proj, q, k, v, ya, lse_a, yb, lse_b, wx, wq, wkk, wkv))
        h = h_new

    dh, loss_acc, dg_final = _loss_head(h, norm_final[None], loss_target, Bl, S, "loss_head")

    g_n_in, g_nq, g_nkv, g_na, g_nb = ([None] * depth for _ in range(5))
    dbias_all, sc_out, sc_rest = [None] * depth, [None] * depth, [None] * depth
    split = lambda t, n: jnp.transpose(t.reshape(t.shape[0], NDEV, n), (1, 0, 2)).astype(BF16)
    tok = None
    for i in reversed(range(depth)):
        h_in, proj, q, k, v, ya, lse_a, yb, lse_b, wx, wq, wkk, wkv = saved[i]
        g_in, g_q, g_kv = norm_in[i][None], norm_q_lat[i][None], norm_kv_lat[i][None]
        g_a, g_b = norm_out_a[i][None], norm_out_b[i][None]
        if tok is not None:
            g_a = _after(g_a, tok)
        dya, dyb, dga, dgb, g_w_out, g_na[i], g_nb[i] = _out_bwd(dh, ya, yb, proj, g_a, g_b, w_out_f[i], tb, f"out_bwd_{i}")
        sc_out[i], tok = _xchg_start([g_w_out.reshape(NDEV, D // NDEV, D).astype(BF16)], ["scatter"], f"scatter_start_{i}a")
        dq, dk, dv = _mla_bwd(q, k, v, yb, dyb, lse_b, tok, Bl, Lp, f"mla_bwd_{i}")
        dqa, dka, dva, dbias = _win_bwd(proj, biases[i], ya, dya, lse_a, Bl, S, f"win_bwd_{i}")
        dcq, dckv, dkr, dwq, dwkk, dwkv, g_nq[i], g_nkv[i] = _mla_prep_bwd(
            dq, dk, dv, proj, g_q, g_kv, wq, wkk, wkv, tq_tab, tk_tab, tm, f"mla_prep_bwd_{i}")
        pieces = (dqa, dka, dva, dga, dcq, dckv, dkr, dgb)
        dbias_all[i] = dbias
        rest = [split(_unpack_w_uq_grad(dwq), 768 // NDEV), split(_unpack_w_ukv_grad(dwkk, dwkv), 1024 // NDEV)]
        dh, dwx, g_n_in[i] = _inproj_bwd(pieces, h_in, g_in, wx, dh, tb, f"inproj_bwd_{i}")
        sc_rest[i], tok = _xchg_start([_unpack_w_in_grad(dwx).astype(BF16)] + rest, ["scatter"] * 3, f"scatter_start_{i}b")

    dh3 = dh.reshape(Bl, Lp, D)
    grad_x = dh3[:, :S]
    g_meta = jnp.transpose(jnp.sum(dh3[:, S:S + NMETA], axis=0).reshape(NMETA, NDEV, D // NDEV), (1, 0, 2))
    dtab, dsink = _table_grad(dbias_all, buckets, tok, "table_grad")
    two_d = lambda a: a.reshape(1, -1) if a.ndim == 1 else a
    small_w = [rel_bias_table, norm_in, sink_a, norm_q_lat, norm_kv_lat, norm_out_a, norm_out_b, norm_final]
    small_m = [m_rel_bias_table, m_norm_in, m_sink_a, m_norm_q_lat, m_norm_kv_lat, m_norm_out_a, m_norm_out_b, m_norm_final]
    small_v = [v_rel_bias_table, v_norm_in, v_sink_a, v_norm_q_lat, v_norm_kv_lat, v_norm_out_a, v_norm_out_b, v_norm_final]
    small_g = [jnp.transpose(dtab[:, :N_BUCKETS]), jnp.concatenate(g_n_in), dsink[:, :, 0], jnp.concatenate(g_nq),
               jnp.concatenate(g_nkv), jnp.concatenate(g_na), jnp.concatenate(g_nb), dg_final]
    sc_small, tok = _xchg_start(small_g + [loss_acc, g_meta], ["gather"] * (len(small_g) + 1) + ["scatter"],
                                "scatter_start_small")

    res = {}

    def update(n, r, w, m, v):
        outs = _adamw_param(r, w, m, v, f"adamw_{n}")
        res[n] = list(outs)
        return outs[0]

    r_out = [finish(sc_out[i], tok, f"scatter_wait_{i}a")[0] for i in range(depth)]
    done = update("w_out", r_out, w_out, m_w_out, v_w_out)
    r_in, r_uq, r_ukv = [None] * depth, [None] * depth, [None] * depth
    for i in reversed(range(depth)):
        r_in[i], r_uq[i], r_ukv[i] = finish(sc_rest[i], grad_x if i > 0 else done, f"scatter_wait_{i}b")
    r_small = finish(sc_small, r_in[0], "scatter_wait_small")
    update("w_in", r_in, w_in, m_w_in, v_w_in)
    update("w_uq", r_uq, w_uq, m_w_uq, v_w_uq)
    update("w_ukv", r_ukv, w_ukv, m_w_ukv, v_w_ukv)
    update("meta_tokens", [r_small[-1]], meta_tokens[None], m_meta_tokens[None], v_meta_tokens[None])
    res["meta_tokens"] = [o[0] for o in res["meta_tokens"]]
    names_small = ["rel_bias_table", "norm_in", "sink_a", "norm_q_lat", "norm_kv_lat", "norm_out_a", "norm_out_b", "norm_final"]
    outs = _adamw_small(r_small[:-2], [two_d(a) for a in small_w], [two_d(a) for a in small_m], [two_d(a) for a in small_v],
                        r_small[-2], "adamw_replicated")
    loss = outs[-1][0, 0]
    ns = len(names_small)
    for j, n in enumerate(names_small):
        res[n] = [outs[kk * ns + j].reshape(small_w[j].shape) for kk in range(4)]
    order = ["meta_tokens", "rel_bias_table", "norm_in", "w_in", "sink_a", "norm_q_lat", "w_uq", "norm_kv_lat", "w_ukv",
             "norm_out_a", "norm_out_b", "w_out", "norm_final"]
    return (loss, grad_x, *[res[n][kk] for kk in range(4) for n in order])
```

```python
import functools
import math

import numpy as np
import jax
import jax.numpy as jnp
from jax import lax
from jax.experimental import pallas as pl
from jax.experimental.pallas import tpu as pltpu

F32, BF16 = jnp.float32, jnp.bfloat16
D = 1024
NMETA = 16
BLK = 128
A_HEADS, A_KV, A_DH, A_W = 8, 2, 64, 512
B_HEADS, NOPE, ROPE, B_V, B_W = 8, 64, 32, 64, 512
Q_RANK, KV_RANK = 256, 128
IN_W = 2208
N_BUCKETS, MAX_DIST = 32, 128
THETA = 10000.0
EPS = 1e-6
NEG = -1e30
A_SCALE = A_DH ** -0.5
B_SCALE = (NOPE + ROPE) ** -0.5
LOG2E = math.log2(math.e)
MASK_LANE = NOPE + ROPE
SUM_LANE = B_V
LANES = 128
NDEV = 8
MESH = pl.DeviceIdType.MESH

NX = 2560
QA0, KA0, VA0, GA0, CQ0, CKV0, KR0, GB0 = 0, 512, 768, 1024, 1536, 1792, 1920, 2048

ADAM_LR, ADAM_B1, ADAM_B2, ADAM_EPS, ADAM_WD, ADAM_STEP = 0.001, 0.9, 0.999, 1e-08, 0.01, 10


def _dot(a, b):
    return jnp.dot(a, b, preferred_element_type=F32)


def _dot_nt(a, b):
    return lax.dot_general(a, b, (((1,), (1,)), ((), ())), preferred_element_type=F32)


def _dot_tn(a, b):
    return lax.dot_general(a, b, (((0,), (0,)), ((), ())), preferred_element_type=F32)


def _lane(shape):
    return lax.broadcasted_iota(jnp.int32, shape, len(shape) - 1)


def _rstd(x):
    return lax.rsqrt(jnp.mean(x * x, axis=-1, keepdims=True) + EPS)


def _rms_bwd(dn, x, r, g):
    z = dn * g
    dx = r * z - x * (r * r * r) * jnp.mean(z * x, axis=-1, keepdims=True)
    return dx, jnp.sum(dn * (x * r), axis=0, keepdims=True)


def _row(i):
    return lambda *ids: (ids[0], i)


def _const2(*ids):
    return (0, 0)


def _params(*sem):
    return pltpu.CompilerParams(dimension_semantics=sem)


SHARD_W = IN_W // NDEV


def _pack_w_in(shards):
    def nat(a, b):
        parts = []
        while a < b:
            p, e = a // SHARD_W, min(b, (a // SHARD_W + 1) * SHARD_W)
            parts.append(shards[p][:, a - p * SHARD_W:e - p * SHARD_W])
            a = e
        return parts
    dup = lambda a: nat(a, a + 64) * 2 + nat(a + 64, a + 128) * 2
    zeros = [jnp.zeros((shards.shape[1], LANES - ROPE), shards.dtype)]
    return jnp.concatenate(nat(0, 512) + dup(512) + dup(640) + nat(768, 1280) + nat(1280, 1536) + nat(1536, 1664)
                           + nat(1664, 1696) + zeros + nat(1696, 2208), axis=1)


_NAT_GROUPS = ((0, 512, QA0), (512, 576, KA0), (576, 640, KA0 + 128), (640, 704, VA0), (704, 768, VA0 + 128),
               (768, 1280, GA0), (1280, 1536, CQ0), (1536, 1664, CKV0), (1664, 1696, KR0), (1696, 2208, GB0))


def _unpack_w_in_grad(g):
    def nat(a, b):
        lo, _, pk = next(grp for grp in _NAT_GROUPS if grp[0] <= a < grp[1])
        t = g[:, pk + a - lo:pk + b - lo]
        return t + g[:, pk + 64 + a - lo:pk + 64 + b - lo] if 512 <= a < 768 else t
    cuts = sorted({c for grp in _NAT_GROUPS for c in grp[:2]} | {p * SHARD_W for p in range(NDEV + 1)})
    shards = [[] for _ in range(NDEV)]
    for a, b in zip(cuts[:-1], cuts[1:]):
        shards[a // SHARD_W].append(nat(a, b))
    return jnp.stack([jnp.concatenate(parts, axis=1) for parts in shards])


def _pack_w_uq(w):
    t = w.reshape(Q_RANK, B_HEADS, NOPE + ROPE)
    t = jnp.concatenate([t, jnp.zeros((Q_RANK, B_HEADS, LANES - NOPE - ROPE), w.dtype)], axis=-1)
    return t.reshape(Q_RANK, B_HEADS * LANES)


def _unpack_w_uq_grad(g):
    return g.reshape(Q_RANK, B_HEADS, LANES)[:, :, :NOPE + ROPE].reshape(Q_RANK, B_HEADS * (NOPE + ROPE))


def _pack_w_ukv(w):
    t = w.reshape(KV_RANK, B_HEADS, NOPE + B_V)
    z = jnp.zeros((KV_RANK, B_HEADS, LANES - NOPE), w.dtype)
    pad = lambda u: jnp.concatenate([u, z], axis=-1).reshape(KV_RANK, B_HEADS * LANES)
    return pad(t[:, :, :NOPE]), pad(t[:, :, NOPE:])


def _unpack_w_ukv_grad(gk, gv):
    head = lambda g: g.reshape(KV_RANK, B_HEADS, LANES)[:, :, :NOPE]
    return jnp.concatenate([head(gk), head(gv)], axis=-1).reshape(KV_RANK, B_HEADS * (NOPE + B_V))


def _t5_bucket(rel):
    nb = N_BUCKETS // 2
    max_exact = nb // 2
    ret = jnp.where(rel > 0, nb, 0)
    n = jnp.abs(rel)
    nf = jnp.maximum(n, 1).astype(F32)
    large = max_exact + (jnp.log(nf / max_exact) / math.log(MAX_DIST / max_exact) * (nb - max_exact)).astype(jnp.int32)
    large = jnp.minimum(large, nb - 1)
    return ret + jnp.where(n < max_exact, n, large)


def _bias_buckets(S):
    q = np.arange(BLK)[:, None]
    k = np.arange(4 * BLK)[None, :]
    is_meta_key = (k >= 3 * BLK) & (k < 3 * BLK + NMETA)
    mi = k - 3 * BLK
    rels, valids = [], []
    for shift in (0, BLK, 2 * BLK):
        rel_real = k - shift - q
        valid_real = (k < 3 * BLK) & (np.abs(rel_real) <= BLK)
        far = -(NMETA + MAX_DIST + BLK)
        rel_meta = (mi - (NMETA + q)) if shift == 0 else np.full_like(k + q, far)
        rels.append(np.where(is_meta_key, rel_meta, rel_real))
        valids.append(valid_real | is_meta_key)
    qm = q < NMETA
    rel_real = NMETA + k - q
    valid_real = (k < BLK) & (np.abs(rel_real) <= BLK) & qm
    rels.append(np.where(is_meta_key, mi - q, rel_real))
    valids.append(valid_real | is_meta_key)
    rel = jnp.asarray(np.stack(rels).astype(np.int32))
    valid = jnp.asarray(np.stack(valids))
    return jnp.where(valid, _t5_bucket(rel), -1).astype(jnp.int32)


def _rope_tables(S, Lp, off, passthrough):
    half = ROPE // 2
    r = np.arange(Lp)
    pos = np.where(r < S, NMETA + r, np.where(r < S + NMETA, r - S, 0)).astype(np.float32)
    freqs = THETA ** (-jnp.arange(half, dtype=F32) / half)
    ang = jnp.asarray(pos)[:, None] * freqs[None, :]
    cos, sin = jnp.cos(ang), jnp.sin(ang)
    z = lambda n: jnp.zeros((Lp, n), F32)
    head = jnp.ones((Lp, off), F32) if passthrough else z(off)
    c = jnp.concatenate([head, cos, cos, z(LANES - off - ROPE)], axis=1)
    s1 = jnp.concatenate([z(off), -sin, z(LANES - off - half)], axis=1)
    s2 = jnp.concatenate([z(off + half), sin, z(LANES - off - ROPE)], axis=1)
    return jnp.stack([c, s1, s2])


def _rope(x, t):
    return x * t[0] + pltpu.roll(x, LANES - 16, 1) * t[1] + pltpu.roll(x, 16, 1) * t[2]


def _rope_t(dy, t):
    return dy * t[0] + pltpu.roll(dy * t[1], 16, 1) + pltpu.roll(dy * t[2], LANES - 16, 1)


def _me_and_peers():
    x, y, c = lax.axis_index("x"), lax.axis_index("y"), lax.axis_index("c")
    flip = lambda v, b: 1 - v if b else v
    peers = [(flip(x, k & 4), flip(y, k & 2), flip(c, k & 1)) for k in range(1, NDEV)]
    return 4 * x + 2 * y + c, peers


def _xchg_start(srcs, kinds, name, after=None):
    n = len(srcs)
    lands = [lax.empty((NDEV,) + s.shape[-2:], s.dtype) for s in srcs]
    extra = [] if after is None else [after]

    def body(*refs):
        src_refs, land_refs = refs[:n], refs[n:2 * n]
        ssems, rsems = refs[2 * n + len(extra):3 * n + len(extra)], refs[3 * n + len(extra):4 * n + len(extra)]
        token = refs[6 * n + len(extra)]
        me, peers = _me_and_peers()
        for it in range(n):
            for k, (px, py, pc) in enumerate(peers):
                src = src_refs[it] if kinds[it] == "gather" else src_refs[it].at[4 * px + 2 * py + pc]
                pltpu.make_async_remote_copy(src_ref=src, dst_ref=land_refs[it].at[me], send_sem=ssems[it].at[k],
                                             recv_sem=rsems[it].at[k], device_id=(px, py, pc), device_id_type=MESH).start()
        token[...] = jnp.zeros_like(token)

    hbm = pl.BlockSpec(memory_space=pltpu.HBM)
    sem = pl.BlockSpec(memory_space=pltpu.SEMAPHORE)
    outs = pl.pallas_call(
        body, name=name,
        out_shape=tuple([pltpu.SemaphoreType.DMA((NDEV - 1,))] * (2 * n) + [pltpu.HBM(a.shape, a.dtype) for a in srcs + lands]
                        + [jax.ShapeDtypeStruct((8, LANES), F32)]),
        in_specs=[hbm] * (2 * n) + [pl.BlockSpec(memory_space=pl.ANY)] * len(extra),
        out_specs=tuple([sem] * (2 * n) + [hbm] * (2 * n) + [pl.BlockSpec(memory_space=pltpu.VMEM)]),
        input_output_aliases={i: 2 * n + i for i in range(2 * n)},
        compiler_params=pltpu.CompilerParams(has_side_effects=pltpu.SideEffectType.DATAFLOW_SIDE_EFFECTING),
    )(*[pltpu.with_memory_space_constraint(a, pltpu.HBM) for a in srcs + lands], *extra)
    handles = [(outs[it], outs[n + it], outs[2 * n + it], outs[3 * n + it], kinds[it]) for it in range(n)]
    return handles, outs[4 * n]


def _xchg_wait(handles, after, name):
    n = len(handles)
    after = list(after) if isinstance(after, (list, tuple)) else [after]

    def body(*refs):
        src_refs, land_refs = refs[:n], refs[n:2 * n]
        ssems, rsems = refs[2 * n:3 * n], refs[3 * n:4 * n]
        me, peers = _me_and_peers()
        for it in range(n):
            for k, (px, py, pc) in enumerate(peers):
                src = src_refs[it] if handles[it][4] == "gather" else src_refs[it].at[4 * px + 2 * py + pc]
                cp = pltpu.make_async_remote_copy(src_ref=src, dst_ref=land_refs[it].at[me], send_sem=ssems[it].at[k],
                                                  recv_sem=rsems[it].at[k], device_id=(px, py, pc), device_id_type=MESH)
                cp.wait_send()
                cp.wait_recv()

    hbm = pl.BlockSpec(memory_space=pltpu.HBM)
    sem = pl.BlockSpec(memory_space=pltpu.SEMAPHORE)
    srcs, lands = [h[2] for h in handles], [h[3] for h in handles]
    outs = pl.pallas_call(
        body, name=name, out_shape=tuple(pltpu.HBM(a.shape, a.dtype) for a in srcs + lands),
        in_specs=[hbm] * (2 * n) + [sem] * (2 * n) + [pl.BlockSpec(memory_space=pl.ANY)] * len(after),
        out_specs=tuple([hbm] * (2 * n)),
        input_output_aliases={i: i for i in range(2 * n)},
        compiler_params=pltpu.CompilerParams(has_side_effects=pltpu.SideEffectType.DATAFLOW_SIDE_EFFECTING),
    )(*srcs, *lands, *[h[0] for h in handles], *[h[1] for h in handles], *after)
    return list(outs[:n]), list(outs[n:])


def _own_slot(land, own, me):
    return lax.dynamic_update_slice(land, own[None].astype(land.dtype), (me, 0, 0))


def _after(x, token):
    return x + token[0, 0]


def _adamw_update(g, w, m, v):
    mn = ADAM_B1 * m + (1.0 - ADAM_B1) * g
    vn = ADAM_B2 * v + (1.0 - ADAM_B2) * (g * g)
    m_hat = mn / (1.0 - ADAM_B1 ** ADAM_STEP)
    v_hat = vn / (1.0 - ADAM_B2 ** ADAM_STEP)
    return -ADAM_LR * (m_hat / (jnp.sqrt(v_hat) + ADAM_EPS) + ADAM_WD * w), mn, vn


def _sum_partials(r_ref):
    g = r_ref[0].astype(F32)
    for p in range(1, NDEV):
        g = g + r_ref[p].astype(F32)
    return g


def _adamw_param(recvs, w, m, v, name):
    depth, r, c = w.shape
    tr = min(r, 256)
    nl = len(recvs)

    def body(*refs):
        r_refs = refs[:nl]
        w_ref, m_ref, v_ref, g_out, d_out, m_out, v_out = refs[nl:]
        for li in range(nl):
            @pl.when(pl.program_id(0) == li)
            def _(li=li):
                g = _sum_partials(r_refs[li])
                g_out[0] = g
                d_out[0], m_out[0], v_out[0] = _adamw_update(g, w_ref[0], m_ref[0], v_ref[0])

    blk = pl.BlockSpec((1, tr, c), lambda l, i: (l, i, 0))
    return pl.pallas_call(
        body, name=name, grid=(depth, r // tr),
        in_specs=[pl.BlockSpec((NDEV, tr, c), lambda l, i: (0, i, 0))] * nl + [blk, blk, blk],
        out_specs=(blk, blk, blk, blk), out_shape=(jax.ShapeDtypeStruct(w.shape, F32),) * 4,
        compiler_params=_params("arbitrary", "arbitrary"),
    )(*recvs, w, m, v)


def _adamw_small(recvs, ws, ms, vs, loss_parts, name):
    n = len(ws)

    def body(*refs):
        r_refs, w_refs, m_refs, v_refs = refs[:n], refs[n:2 * n], refs[2 * n:3 * n], refs[3 * n:4 * n]
        outs = refs[4 * n + 1:]
        for j in range(n):
            g = _sum_partials(r_refs[j])
            outs[j][...] = g
            outs[n + j][...], outs[2 * n + j][...], outs[3 * n + j][...] = _adamw_update(
                g, w_refs[j][...], m_refs[j][...], v_refs[j][...])
        outs[4 * n][...] = _sum_partials(refs[4 * n])

    vm = pl.BlockSpec(memory_space=pltpu.VMEM)
    return pl.pallas_call(
        body, name=name, in_specs=[vm] * (4 * n + 1), out_specs=tuple([vm] * (4 * n + 1)),
        out_shape=tuple(jax.ShapeDtypeStruct(a.shape, F32) for a in ws) * 4 + (jax.ShapeDtypeStruct(loss_parts.shape[1:], F32),),
    )(*recvs, *ws, *ms, *vs, loss_parts)


def _inproj_fwd(h, g, wx, tm, name):
    T = h.shape[0]

    def body(h_ref, g_ref, w_ref, o_ref):
        x = h_ref[...]
        u = (x * _rstd(x) * g_ref[...]).astype(BF16)
        o_ref[...] = _dot(u, w_ref[...])

    return pl.pallas_call(
        body, name=name, grid=(T // tm,),
        in_specs=[pl.BlockSpec((tm, D), _row(0)), pl.BlockSpec((1, D), _const2), pl.BlockSpec((D, NX), _const2)],
        out_specs=pl.BlockSpec((tm, NX), _row(0)), out_shape=jax.ShapeDtypeStruct((T, NX), F32),
        compiler_params=_params("parallel"),
    )(h, g, wx)


def _mla_prep_fwd(proj, gq, gkv, wq, wkk, wkv, tq, tk, tm, name):
    T = proj.shape[0]

    def body(cq_ref, ckv_ref, kr_ref, gq_ref, gkv_ref, wq_ref, wkk_ref, wkv_ref, tq_ref, tk_ref, q_ref, k_ref, v_ref):
        cq = cq_ref[...]
        q = _dot((cq * _rstd(cq) * gq_ref[...]).astype(BF16), wq_ref[...])
        tqv = tq_ref[...]
        lane = _lane((1, LANES))
        one = jnp.where(lane == MASK_LANE, 1.0, 0.0)
        for h in range(B_HEADS):
            cs = slice(LANES * h, LANES * (h + 1))
            q_ref[:, cs] = (_rope(q[:, cs], tqv) * (B_SCALE * LOG2E) + one).astype(BF16)
        tkv = tk_ref[...]
        ksh = pltpu.roll(_rope(kr_ref[...], tkv), NOPE, 1) + tkv[3]
        ckv = ckv_ref[...]
        cb = (ckv * _rstd(ckv) * gkv_ref[...]).astype(BF16)
        kn = _dot(cb, wkk_ref[...])
        vn = _dot(cb, wkv_ref[...])
        minus = jnp.where((lane == SUM_LANE) | (lane == SUM_LANE + 1), -1.0, 0.0)
        for h in range(B_HEADS):
            cs = slice(LANES * h, LANES * (h + 1))
            k_ref[:, cs] = (kn[:, cs] + ksh).astype(BF16)
            v_ref[:, cs] = (vn[:, cs] + minus).astype(BF16)

    tab = lambda n: pl.BlockSpec((n, tm, LANES), lambda i: (0, i, 0))
    wide = pl.BlockSpec((tm, 1024), _row(0))
    return pl.pallas_call(
        body, name=name, grid=(T // tm,),
        in_specs=[pl.BlockSpec((tm, 256), _row(CQ0 // 256)), pl.BlockSpec((tm, 128), _row(CKV0 // 128)),
                  pl.BlockSpec((tm, 128), _row(KR0 // 128)), pl.BlockSpec((1, Q_RANK), _const2),
                  pl.BlockSpec((1, KV_RANK), _const2), pl.BlockSpec((Q_RANK, 1024), _const2),
                  pl.BlockSpec((KV_RANK, 1024), _const2), pl.BlockSpec((KV_RANK, 1024), _const2), tab(3), tab(4)],
        out_specs=(wide, wide, wide), out_shape=(jax.ShapeDtypeStruct((T, 1024), BF16),) * 3,
        compiler_params=_params("parallel"),
    )(proj, proj, proj, gq, gkv, wq, wkk, wkv, tq, tk)


SINK_SLOT = 4 * BLK - 1


def _build_bias(buckets, table, sinks, name):
    depth = sinks.shape[0]

    def body(tab_ref, sink_ref, b_ref, *o_refs):
        bidx = b_ref[0]
        slot = _lane(bidx.shape)
        for h in range(A_HEADS):
            acc = jnp.full(bidx.shape, NEG, F32)
            for j in range(N_BUCKETS):
                acc = jnp.where(bidx == j, tab_ref[j, h] * LOG2E, acc)
            for l in range(depth):
                o_refs[l][0, h] = jnp.where(slot == SINK_SLOT, sink_ref[l, h] * LOG2E, acc)

    smem = pl.BlockSpec(memory_space=pltpu.SMEM)
    blk = pl.BlockSpec((1, A_HEADS, BLK, 4 * BLK), lambda i: (i, 0, 0, 0))
    return pl.pallas_call(
        body, name=name, grid=(4,),
        in_specs=[smem, smem, pl.BlockSpec((1, BLK, 4 * BLK), lambda i: (i, 0, 0))],
        out_specs=tuple([blk] * depth), out_shape=tuple([jax.ShapeDtypeStruct((4, A_HEADS, BLK, 4 * BLK), F32)] * depth),
        compiler_params=_params("parallel"),
    )(table, sinks, buckets)


def _win_base(n, NB):
    return jnp.where(n == NB, 0, jnp.clip(n - 1, 0, NB - 3))


def _win_variant(n, NB):
    return jnp.where(n == 0, 0, jnp.where(n < NB - 1, 1, jnp.where(n == NB - 1, 2, 3)))


def _win_kv(k_ref, v_ref, base, S, kvh):
    cs = slice(LANES * kvh, LANES * (kvh + 1))
    k2 = jnp.concatenate([k_ref[pl.ds(base, 3 * BLK), cs], k_ref[S:S + BLK, cs]], axis=0).astype(BF16)
    v2 = jnp.concatenate([v_ref[pl.ds(base, 3 * BLK), cs], v_ref[S:S + BLK, cs]], axis=0)
    lane = _lane(v2.shape)
    v2 = jnp.where(lane < A_DH, v2, jnp.where(lane < A_DH + 2, -1.0, 0.0)).astype(BF16)
    return k2, v2


def _win_fwd(proj, bias, Bl, S, name):
    T = proj.shape[0]
    Lp = S + BLK
    NB = S // BLK
    nblk = Lp // BLK
    G = A_HEADS // A_KV

    def body(q_ref, k_ref, v_ref, b_ref, o_ref, lse_ref):
        n = pl.program_id(1)
        base = pl.multiple_of(_win_base(n, NB) * BLK, BLK)
        lane = _lane((BLK, LANES))
        lo = lane < A_DH
        kv = [_win_kv(k_ref, v_ref, base, S, kvh) for kvh in range(A_KV)]
        scores = []
        for h in range(A_HEADS):
            qp = q_ref[:, LANES * (h // 2):LANES * (h // 2 + 1)]
            qm = (jnp.where(lo if h % 2 == 0 else ~lo, qp, 0.0) * (A_SCALE * LOG2E)).astype(BF16)
            scores.append(_dot_nt(qm, kv[h // G][0]) + b_ref[0, h])
        ms = [jnp.max(s, axis=-1, keepdims=True) for s in scores]
        es = [jnp.exp2(s - m).astype(BF16) for s, m in zip(scores, ms)]
        outs = []
        t = jnp.zeros((BLK, LANES), F32)
        for h in range(A_HEADS):
            o2 = _dot(es[h], kv[h // G][1])
            l = -jnp.sum(jnp.where(lane == A_DH, o2, 0.0), axis=-1, keepdims=True)
            outs.append(o2 * (1.0 / l))
            t = jnp.where(lane == h, ms[h] + jnp.log(l) * LOG2E, t)
        for j in range(A_HEADS // 2):
            o_ref[:, LANES * j:LANES * (j + 1)] = jnp.where(lo, outs[2 * j], pltpu.roll(outs[2 * j + 1], A_DH, 1))
        lse_ref[...] = t

    qrow = lambda b, n: (b * nblk + n, 0)
    return pl.pallas_call(
        body, name=name, grid=(Bl, nblk),
        in_specs=[pl.BlockSpec((BLK, A_W), qrow),
                  pl.BlockSpec((Lp, 256), lambda b, n: (b, KA0 // 256)), pl.BlockSpec((Lp, 256), lambda b, n: (b, VA0 // 256)),
                  pl.BlockSpec((1, A_HEADS, BLK, 4 * BLK), lambda b, n: (_win_variant(n, NB), 0, 0, 0))],
        out_specs=(pl.BlockSpec((BLK, A_W), qrow), pl.BlockSpec((BLK, LANES), qrow)),
        out_shape=(jax.ShapeDtypeStruct((T, A_W), F32), jax.ShapeDtypeStruct((T, LANES), F32)),
        compiler_params=_params("parallel", "arbitrary"),
    )(proj, proj, proj, bias)


def _q_tiles(Lp):
    S = Lp - BLK
    return S // 256, S


def _mla_fwd(q, k, v, Bl, Lp, name):
    T = q.shape[0]
    n_big, S = _q_tiles(Lp)

    def body(q_ref, k_ref, v_ref, o_ref, lse_ref):
        def qtiles(starts, nq):
            lane = _lane((nq, LANES))
            cols = [slice(LANES * hh, LANES * (hh + 1)) for hh in range(2)]
            scores = [[_dot_nt(q_ref[pl.ds(r0, nq), cs], k_ref[:, cs]) for cs in cols] for r0 in starts]
            for r0, tile_scores in zip(starts, scores):
                outs = []
                t = jnp.zeros((nq, LANES), F32)
                for hh in range(2):
                    cs, s = cols[hh], tile_scores[hh]
                    m = jnp.max(s, axis=-1, keepdims=True)
                    o2 = _dot(jnp.exp2(s - m).astype(BF16), v_ref[:, cs])
                    l = -jnp.sum(jnp.where(lane == SUM_LANE, o2, 0.0), axis=-1, keepdims=True)
                    outs.append(o2 * (1.0 / l))
                    t = jnp.where(lane == hh, m + jnp.log(l) * LOG2E, t)
                o_ref[pl.ds(r0, nq), :] = jnp.where(lane < B_V, outs[0], pltpu.roll(outs[1], B_V, 1))
                lse_ref[pl.ds(r0, nq), :] = t

        def step(i, c):
            qtiles([pl.multiple_of(i * 512, 256), pl.multiple_of(i * 512 + 256, 256)], 256)
            return c

        lax.fori_loop(0, n_big // 2, step, 0)
        if n_big % 2:
            qtiles([(n_big - 1) * 256], 256)
        qtiles([S], BLK)

    pair = lambda b, j: (b, j)
    wide, narrow = pl.BlockSpec((Lp, 256), pair), pl.BlockSpec((Lp, LANES), pair)
    return pl.pallas_call(
        body, name=name, grid=(Bl, B_HEADS // 2), in_specs=[wide, wide, wide], out_specs=(narrow, narrow),
        out_shape=(jax.ShapeDtypeStruct((T, B_W), F32), jax.ShapeDtypeStruct((T, B_W), F32)),
        compiler_params=_params("parallel", "parallel"),
    )(q, k, v)


def _gated(y, gate, gain):
    r = _rstd(y)
    nrm = y * r
    sg = jax.nn.sigmoid(gate)
    return r, nrm, sg, nrm * gain, gate * sg


def _out_fwd(ya, yb, proj, na, nb, wout, h, tm, name):
    T = h.shape[0]

    def body(ya_ref, yb_ref, ga_ref, gb_ref, na_ref, nb_ref, w_ref, h_ref, o_ref):
        _, _, _, n_a, sl_a = _gated(ya_ref[...], ga_ref[...], na_ref[...])
        _, _, _, n_b, sl_b = _gated(yb_ref[...], gb_ref[...], nb_ref[...])
        acc = _dot((n_a * sl_a).astype(BF16), w_ref[0:A_W, :]) + _dot((n_b * sl_b).astype(BF16), w_ref[A_W:A_W + B_W, :])
        o_ref[...] = h_ref[...] + acc

    half = pl.BlockSpec((tm, 512), _row(0))
    return pl.pallas_call(
        body, name=name, grid=(T // tm,),
        in_specs=[half, half, pl.BlockSpec((tm, 512), _row(GA0 // 512)), pl.BlockSpec((tm, 512), _row(GB0 // 512)),
                  pl.BlockSpec((1, 512), _const2), pl.BlockSpec((1, 512), _const2), pl.BlockSpec((D, D), _const2),
                  pl.BlockSpec((tm, D), _row(0))],
        out_specs=pl.BlockSpec((tm, D), _row(0)), out_shape=jax.ShapeDtypeStruct((T, D), F32),
        compiler_params=_params("parallel"),
    )(ya, yb, proj, proj, na, nb, wout, h)


def _loss_head(h, gf, target, Bl, S, name):
    T = h.shape[0]
    nblk = (S + BLK) // BLK
    NB = S // BLK

    def body(h_ref, g_ref, t_ref, dh_ref, loss_ref, dg_ref):
        b, n = pl.program_id(0), pl.program_id(1)

        @pl.when((b == 0) & (n == 0))
        def _():
            loss_ref[...] = jnp.zeros_like(loss_ref)
            dg_ref[...] = jnp.zeros_like(dg_ref)

        @pl.when(n < NB)
        def _():
            x = h_ref[...]
            g = g_ref[...]
            r = _rstd(x)
            err = x * r * g - t_ref[0]
            loss_ref[...] += 0.5 * jnp.sum(jnp.mean(err * err, axis=-1, keepdims=True))
            dx, dg = _rms_bwd(err * (1.0 / D), x, r, g)
            dh_ref[...] = dx
            dg_ref[...] += dg

        @pl.when(n >= NB)
        def _():
            dh_ref[...] = jnp.zeros_like(dh_ref)

    return pl.pallas_call(
        body, name=name, grid=(Bl, nblk),
        in_specs=[pl.BlockSpec((BLK, D), lambda b, n: (b * nblk + n, 0)), pl.BlockSpec((1, D), lambda b, n: (0, 0)),
                  pl.BlockSpec((1, BLK, D), lambda b, n: (b, jnp.minimum(n, NB - 1), 0))],
        out_specs=(pl.BlockSpec((BLK, D), lambda b, n: (b * nblk + n, 0)), pl.BlockSpec((8, LANES), lambda b, n: (0, 0)),
                   pl.BlockSpec((1, D), lambda b, n: (0, 0))),
        out_shape=(jax.ShapeDtypeStruct((T, D), F32), jax.ShapeDtypeStruct((8, LANES), F32), jax.ShapeDtypeStruct((1, D), F32)),
        compiler_params=_params("arbitrary", "arbitrary"),
    )(h, gf, target)


def _out_bwd(dh, ya, yb, proj, na, nb, wout, tm, name):
    T = dh.shape[0]

    def body(dh_ref, ya_ref, yb_ref, ga_ref, gb_ref, na_ref, nb_ref, w_ref,
             dya_ref, dyb_ref, dga_ref, dgb_ref, dw_ref, dna_ref, dnb_ref):
        @pl.when(pl.program_id(0) == 0)
        def _():
            dw_ref[...] = jnp.zeros_like(dw_ref)
            dna_ref[...] = jnp.zeros_like(dna_ref)
            dnb_ref[...] = jnp.zeros_like(dnb_ref)

        dhb = dh_ref[...].astype(BF16)
        dy = _dot_nt(dhb, w_ref[...])
        parts = ((ya_ref, ga_ref, na_ref, dya_ref, dga_ref, dna_ref, 0), (yb_ref, gb_ref, nb_ref, dyb_ref, dgb_ref, dnb_ref, A_W))
        for y_ref, gate_ref, gain_ref, dy_out, dgate_out, dgain_out, c0 in parts:
            y, gate, gain = y_ref[...], gate_ref[...], gain_ref[...]
            r, nrm, sg, n_g, sl = _gated(y, gate, gain)
            dyp = dy[:, c0:c0 + 512]
            dgate_out[...] = dyp * n_g * (sg * (1.0 + gate * (1.0 - sg)))
            dx, dgain = _rms_bwd(dyp * sl, y, r, gain)
            dy_out[...] = dx
            dgain_out[...] += dgain
            dw_ref[c0:c0 + 512, :] += _dot_tn((n_g * sl).astype(BF16), dhb)

    half = pl.BlockSpec((tm, 512), _row(0))
    vec = pl.BlockSpec((1, 512), _const2)
    return pl.pallas_call(
        body, name=name, grid=(T // tm,),
        in_specs=[pl.BlockSpec((tm, D), _row(0)), half, half, pl.BlockSpec((tm, 512), _row(GA0 // 512)),
                  pl.BlockSpec((tm, 512), _row(GB0 // 512)), vec, vec, pl.BlockSpec((D, D), _const2)],
        out_specs=(half, half, half, half, pl.BlockSpec((D, D), _const2), vec, vec),
        out_shape=(jax.ShapeDtypeStruct((T, 512), F32),) * 4 + (jax.ShapeDtypeStruct((D, D), F32),)
        + (jax.ShapeDtypeStruct((1, 512), F32),) * 2,
        compiler_params=_params("arbitrary"),
    )(dh, ya, yb, proj, proj, na, nb, wout)


def _mla_bwd(q, k, v, o, do, lse, after, Bl, Lp, name):
    T = q.shape[0]
    n_big, S = _q_tiles(Lp)

    def body(q_ref, k_ref, v_ref, o_ref, do_ref, lse_ref, _, dq_ref, dk_ref, dv_ref):
        dk_ref[...] = jnp.zeros_like(dk_ref)
        dv_ref[...] = jnp.zeros_like(dv_ref)

        def qtile(r0, nq):
            lane = _lane((nq, LANES))
            dop = do_ref[pl.ds(r0, nq), :]
            op = o_ref[pl.ds(r0, nq), :]
            lsev = lse_ref[pl.ds(r0, nq), :]
            for hh in range(2):
                cs = slice(LANES * hh, LANES * (hh + 1))
                dom = jnp.where((lane < B_V) if hh == 0 else (lane >= B_V), dop, 0.0)
                delta = jnp.sum(dom * op, axis=-1, keepdims=True)
                d_hi = delta.astype(BF16).astype(F32)
                x = dom if hh == 0 else pltpu.roll(dom, B_V, 1)
                domx = jnp.where(lane == SUM_LANE, d_hi, jnp.where(lane == SUM_LANE + 1, delta - d_hi, x)).astype(BF16)
                qh, kh, vh = q_ref[pl.ds(r0, nq), cs], k_ref[:, cs], v_ref[:, cs]
                lse_h = jnp.sum(jnp.where(lane == hh, lsev, 0.0), axis=-1, keepdims=True)
                p = jnp.exp2(_dot_nt(qh, kh) - lse_h)
                ds = (p * _dot_nt(domx, vh)).astype(BF16)
                dq_ref[pl.ds(r0, nq), cs] = _dot(ds, kh) * B_SCALE
                dk_ref[:, cs] += _dot_tn(ds, qh)
                dv_ref[:, cs] += _dot_tn(p.astype(BF16), domx)

        def step(i, c):
            qtile(pl.multiple_of(i * 512, 256), 256)
            qtile(pl.multiple_of(i * 512 + 256, 256), 256)
            return c

        lax.fori_loop(0, n_big // 2, step, 0)
        if n_big % 2:
            qtile((n_big - 1) * 256, 256)
        qtile(S, BLK)
        dk_ref[...] = dk_ref[...] * (1.0 / LOG2E)

    pair = lambda b, j: (b, j)
    wide, narrow = pl.BlockSpec((Lp, 256), pair), pl.BlockSpec((Lp, LANES), pair)
    return pl.pallas_call(
        body, name=name, grid=(Bl, B_HEADS // 2),
        in_specs=[wide, wide, wide, narrow, narrow, narrow, pl.BlockSpec(memory_space=pl.ANY)],
        out_specs=(wide, wide, wide), out_shape=(jax.ShapeDtypeStruct((T, 1024), F32),) * 3,
        compiler_params=_params("parallel", "parallel"),
    )(q, k, v, o, do, lse, after)


def _win_bwd(proj, bias, ya, dya, lse, Bl, S, name):
    T = proj.shape[0]
    Lp = S + BLK
    NB = S // BLK
    nblk = Lp // BLK
    G = A_HEADS // A_KV

    def body(q_ref, k_ref, v_ref, b_ref, o_ref, do_ref, lse_ref, dq_ref, dk_ref, dv_ref, db_ref):
        n = pl.program_id(1)

        @pl.when(n == 0)
        def _():
            dk_ref[...] = jnp.zeros_like(dk_ref)
            dv_ref[...] = jnp.zeros_like(dv_ref)

        @pl.when((n == 0) | (n == 1) | (n == NB - 1) | (n == NB))
        def _():
            db_ref[...] = jnp.zeros_like(db_ref)

        base = pl.multiple_of(_win_base(n, NB) * BLK, BLK)
        lane = _lane((BLK, LANES))
        lo = lane < A_DH
        lsev = lse_ref[...]
        kv = [_win_kv(k_ref, v_ref, base, S, kvh) for kvh in range(A_KV)]
        qms, pbs, domxs, dsls = [], [], [], []
        for h in range(A_HEADS):
            k2, v2 = kv[h // G]
            ps = slice(LANES * (h // 2), LANES * (h // 2 + 1))
            qm = (jnp.where(lo if h % 2 == 0 else ~lo, q_ref[:, ps], 0.0) * (A_SCALE * LOG2E)).astype(BF16)
            lse_h = jnp.sum(jnp.where(lane == h, lsev, 0.0), axis=-1, keepdims=True)
            p = jnp.exp2(_dot_nt(qm, k2) + b_ref[0, h] - lse_h)
            dop, op = do_ref[:, ps], o_ref[:, ps]
            if h % 2 == 1:
                dop, op = pltpu.roll(dop, A_DH, 1), pltpu.roll(op, A_DH, 1)
            dom = jnp.where(lo, dop, 0.0)
            delta = jnp.sum(dom * op, axis=-1, keepdims=True)
            d_hi = delta.astype(BF16).astype(F32)
            domx = jnp.where(lane == A_DH, d_hi, jnp.where(lane == A_DH + 1, delta - d_hi, dom)).astype(BF16)
            dsc = p * _dot_nt(domx, v2)
            db_ref[0, 0, h] += dsc
            qms.append(qm)
            pbs.append(p.astype(BF16))
            domxs.append(domx)
            dsls.append(dsc.astype(BF16))
        dqs = [_dot(dsls[h], kv[h // G][0]) * A_SCALE for h in range(A_HEADS)]
        for kvh in range(A_KV):
            cs = slice(LANES * kvh, LANES * (kvh + 1))
            group = range(kvh * G, (kvh + 1) * G)
            stack = lambda parts: jnp.concatenate([parts[h] for h in group], axis=0)
            dk_acc = _dot_tn(stack(dsls), stack(qms)) * (1.0 / LOG2E)
            dv_acc = _dot_tn(stack(pbs), stack(domxs))
            dv_acc = jnp.where(_lane(dv_acc.shape) < A_DH, dv_acc, 0.0)
            dk_ref[pl.ds(base, 3 * BLK), cs] += dk_acc[0:3 * BLK]
            dk_ref[S:S + BLK, cs] += dk_acc[3 * BLK:4 * BLK]
            dv_ref[pl.ds(base, 3 * BLK), cs] += dv_acc[0:3 * BLK]
            dv_ref[S:S + BLK, cs] += dv_acc[3 * BLK:4 * BLK]
        for j in range(A_HEADS // 2):
            dq_ref[:, LANES * j:LANES * (j + 1)] = jnp.where(lo, dqs[2 * j], dqs[2 * j + 1])

    qrow = lambda b, n: (b * nblk + n, 0)
    kvs = pl.BlockSpec((Lp, 256), lambda b, n: (b, 0))
    return pl.pallas_call(
        body, name=name, grid=(Bl, nblk),
        in_specs=[pl.BlockSpec((BLK, A_W), qrow),
                  pl.BlockSpec((Lp, 256), lambda b, n: (b, KA0 // 256)), pl.BlockSpec((Lp, 256), lambda b, n: (b, VA0 // 256)),
                  pl.BlockSpec((1, A_HEADS, BLK, 4 * BLK), lambda b, n: (_win_variant(n, NB), 0, 0, 0)),
                  pl.BlockSpec((BLK, A_W), qrow), pl.BlockSpec((BLK, A_W), qrow), pl.BlockSpec((BLK, LANES), qrow)],
        out_specs=(pl.BlockSpec((BLK, A_W), qrow), kvs, kvs,
                   pl.BlockSpec((1, 1, A_HEADS, BLK, 4 * BLK), lambda b, n: (b, _win_variant(n, NB), 0, 0, 0))),
        out_shape=(jax.ShapeDtypeStruct((T, A_W), F32), jax.ShapeDtypeStruct((T, 256), F32),
                   jax.ShapeDtypeStruct((T, 256), F32), jax.ShapeDtypeStruct((Bl, 4, A_HEADS, BLK, 4 * BLK), F32)),
        compiler_params=_params("parallel", "arbitrary"),
    )(proj, proj, proj, bias, ya, dya, lse)


def _table_grad(dbias_list, buckets, after, name):
    nl = len(dbias_list)
    Bl = dbias_list[0].shape[0]

    def body(*refs):
        d_refs = refs[:nl]
        b_ref, _, o_ref, s_ref, acc, part = refs[nl:]
        v, b = pl.program_id(0), pl.program_id(1)

        @pl.when((v == 0) & (b == 0))
        def _():
            o_ref[...] = jnp.zeros_like(o_ref)
            s_ref[...] = jnp.zeros_like(s_ref)

        slot = _lane((BLK, 4 * BLK))
        for l in range(nl):
            for h in range(A_HEADS):
                col = jnp.sum(jnp.where(slot == SINK_SLOT, d_refs[l][0, 0, h], 0.0), axis=0, keepdims=True)
                s_ref[l, h:h + 1, :] += jnp.sum(col, axis=1, keepdims=True)

        tot = d_refs[0][0, 0]
        for r in d_refs[1:]:
            tot = tot + r[0, 0]

        @pl.when(b == 0)
        def _():
            acc[...] = tot

        @pl.when(b > 0)
        def _():
            acc[...] += tot

        @pl.when(b == Bl - 1)
        def _():
            bidx = b_ref[0]

            def step(j, c):
                mask = bidx == j
                for h in range(A_HEADS):
                    part[h, j] = jnp.sum(jnp.where(mask, acc[h], 0.0).reshape(BLK // 8, 8, 4 * BLK), axis=0)
                return c

            lax.fori_loop(0, N_BUCKETS, step, 0)
            rows = lax.broadcasted_iota(jnp.int32, (N_BUCKETS, LANES), 0)
            lanes = _lane((N_BUCKETS, LANES))
            for h in range(A_HEADS):
                col = jnp.sum(jnp.sum(part[h], axis=1), axis=-1, keepdims=True)
                o_ref[h:h + 1, :] += jnp.sum(jnp.where(rows == lanes, col, 0.0), axis=0, keepdims=True)

    return pl.pallas_call(
        body, name=name, grid=(4, Bl),
        in_specs=[pl.BlockSpec((1, 1, A_HEADS, BLK, 4 * BLK), lambda v, b: (b, v, 0, 0, 0))] * nl
        + [pl.BlockSpec((1, BLK, 4 * BLK), lambda v, b: (v, 0, 0)), pl.BlockSpec(memory_space=pl.ANY)],
        out_specs=(pl.BlockSpec((8, LANES), lambda v, b: (0, 0)), pl.BlockSpec((nl, 8, LANES), lambda v, b: (0, 0, 0))),
        out_shape=(jax.ShapeDtypeStruct((8, LANES), F32), jax.ShapeDtypeStruct((nl, 8, LANES), F32)),
        scratch_shapes=[pltpu.VMEM((A_HEADS, BLK, 4 * BLK), F32), pltpu.VMEM((A_HEADS, N_BUCKETS, 8, 4 * BLK), F32)],
        compiler_params=_params("arbitrary", "arbitrary"),
    )(*dbias_list, buckets, after)


def _mla_prep_bwd(dq, dk, dv, proj, gq, gkv, wq, wkk, wkv, tq, tk, tm, name):
    T = proj.shape[0]

    def body(dq_ref, dk_ref, dv_ref, cq_ref, ckv_ref, gq_ref, gkv_ref, wq_ref, wkk_ref, wkv_ref, tq_ref, tk_ref,
             dcq_ref, dckv_ref, dkr_ref, dwq_ref, dwkk_ref, dwkv_ref, dgq_ref, dgkv_ref):
        @pl.when(pl.program_id(0) == 0)
        def _():
            for r in (dwq_ref, dwkk_ref, dwkv_ref, dgq_ref, dgkv_ref):
                r[...] = jnp.zeros_like(r)

        tqv = tq_ref[...]
        dqp = jnp.concatenate([_rope_t(dq_ref[:, LANES * h:LANES * (h + 1)], tqv) for h in range(B_HEADS)],
                              axis=1).astype(BF16)
        cq, gq_ = cq_ref[...], gq_ref[...]
        rq = _rstd(cq)
        dwq_ref[...] += _dot_tn((cq * rq * gq_).astype(BF16), dqp)
        dx, dg = _rms_bwd(_dot_nt(dqp, wq_ref[...]), cq, rq, gq_)
        dcq_ref[...] = dx
        dgq_ref[...] += dg

        dkv_ = dk_ref[...]
        dks = dkv_[:, 0:LANES]
        for h in range(1, B_HEADS):
            dks = dks + dkv_[:, LANES * h:LANES * (h + 1)]
        dkr_ref[...] = _rope_t(pltpu.roll(dks, NOPE, 1), tk_ref[...])
        dkb = dkv_.astype(BF16)
        dvb = dv_ref[...].astype(BF16)
        ckv, gkv_ = ckv_ref[...], gkv_ref[...]
        rk = _rstd(ckv)
        cb = (ckv * rk * gkv_).astype(BF16)
        dwkk_ref[...] += _dot_tn(cb, dkb)
        dwkv_ref[...] += _dot_tn(cb, dvb)
        dx, dg = _rms_bwd(_dot_nt(dkb, wkk_ref[...]) + _dot_nt(dvb, wkv_ref[...]), ckv, rk, gkv_)
        dckv_ref[...] = dx
        dgkv_ref[...] += dg

    tab = lambda n: pl.BlockSpec((n, tm, LANES), lambda i: (0, i, 0))
    wide = pl.BlockSpec((tm, 1024), _row(0))
    return pl.pallas_call(
        body, name=name, grid=(T // tm,),
        in_specs=[wide, wide, wide, pl.BlockSpec((tm, 256), _row(CQ0 // 256)),
                  pl.BlockSpec((tm, 128), _row(CKV0 // 128)), pl.BlockSpec((1, Q_RANK), _const2),
                  pl.BlockSpec((1, KV_RANK), _const2), pl.BlockSpec((Q_RANK, 1024), _const2),
                  pl.BlockSpec((KV_RANK, 1024), _const2), pl.BlockSpec((KV_RANK, 1024), _const2), tab(3), tab(4)],
        out_specs=(pl.BlockSpec((tm, 256), _row(0)), pl.BlockSpec((tm, 128), _row(0)), pl.BlockSpec((tm, 128), _row(0)),
                   pl.BlockSpec((Q_RANK, 1024), _const2), pl.BlockSpec((KV_RANK, 1024), _const2),
                   pl.BlockSpec((KV_RANK, 1024), _const2), pl.BlockSpec((1, Q_RANK), _const2),
                   pl.BlockSpec((1, KV_RANK), _const2)),
        out_shape=(jax.ShapeDtypeStruct((T, 256), F32), jax.ShapeDtypeStruct((T, 128), F32),
                   jax.ShapeDtypeStruct((T, 128), F32), jax.ShapeDtypeStruct((Q_RANK, 1024), F32),
                   jax.ShapeDtypeStruct((KV_RANK, 1024), F32), jax.ShapeDtypeStruct((KV_RANK, 1024), F32),
                   jax.ShapeDtypeStruct((1, Q_RANK), F32), jax.ShapeDtypeStruct((1, KV_RANK), F32)),
        compiler_params=_params("arbitrary"),
    )(dq, dk, dv, proj, proj, gq, gkv, wq, wkk, wkv, tq, tk)


def _inproj_bwd(pieces, h, g, wx, dh_out, tm, name):
    T = h.shape[0]
    nsteps = T // tm
    CH = 512

    def body(*refs):
        p_refs = refs[:len(pieces)]
        h_ref, g_ref, w_ref, dho_ref, dh_ref, dw_hbm, dg_ref, acc, sem = refs[len(pieces):]
        i = pl.program_id(0)

        @pl.when(i == 0)
        def _():
            acc[...] = jnp.zeros_like(acc)
            dg_ref[...] = jnp.zeros_like(dg_ref)

        dp = jnp.concatenate([r[...].astype(BF16) for r in p_refs], axis=1)
        x, gain = h_ref[...], g_ref[...]
        r = _rstd(x)
        u = (x * r * gain).astype(BF16)
        for c in range(0, NX, CH):
            acc[:, c:c + CH] += _dot_tn(u, dp[:, c:c + CH])
        dx, dg = _rms_bwd(_dot_nt(dp, w_ref[...]), x, r, gain)
        dh_ref[...] = dho_ref[...] + dx
        dg_ref[...] += dg

        @pl.when(i == nsteps - 1)
        def _():
            cp = pltpu.make_async_copy(acc, dw_hbm, sem)
            cp.start()
            cp.wait()

    return pl.pallas_call(
        body, name=name, grid=(nsteps,),
        in_specs=[pl.BlockSpec((tm, p.shape[1]), _row(0)) for p in pieces]
        + [pl.BlockSpec((tm, D), _row(0)), pl.BlockSpec((1, D), _const2), pl.BlockSpec((D, NX), _const2),
           pl.BlockSpec((tm, D), _row(0))],
        out_specs=(pl.BlockSpec((tm, D), _row(0)), pl.BlockSpec(memory_space=pl.ANY), pl.BlockSpec((1, D), _const2)),
        out_shape=(jax.ShapeDtypeStruct((T, D), F32), jax.ShapeDtypeStruct((D, NX), F32), jax.ShapeDtypeStruct((1, D), F32)),
        scratch_shapes=[pltpu.VMEM((D, NX), F32), pltpu.SemaphoreType.DMA(())],
        compiler_params=_params("arbitrary"),
    )(*pieces, h, g, wx, dh_out)


def kernel(x, meta_tokens, rel_bias_table, norm_in, w_in, sink_a, norm_q_lat, w_uq, norm_kv_lat, w_ukv, norm_out_a, norm_out_b, w_out, norm_final, loss_target, m_meta_tokens, m_rel_bias_table, m_norm_in, m_w_in, m_sink_a, m_norm_q_lat, m_w_uq, m_norm_kv_lat, m_w_ukv, m_norm_out_a, m_norm_out_b, m_w_out, m_norm_final, v_meta_tokens, v_rel_bias_table, v_norm_in, v_w_in, v_sink_a, v_norm_q_lat, v_w_uq, v_norm_kv_lat, v_w_ukv, v_norm_out_a, v_norm_out_b, v_w_out, v_norm_final):
    Bl, S, _ = x.shape
    assert S % 256 == 0 and S >= 3 * BLK, "the attention kernels tile the real tokens in 256-row blocks"
    Lp = S + BLK
    T = Bl * Lp
    tm = Lp // 4
    tq = Lp // 8
    depth = w_in.shape[0]
    me = 4 * lax.axis_index("x") + 2 * lax.axis_index("y") + lax.axis_index("c")

    def finish(handles, after, name):
        srcs, lands = _xchg_wait(handles, after, name)
        full = []
        for hd, src, land in zip(handles, srcs, lands):
            own = src if hd[4] == "gather" else lax.dynamic_index_in_dim(src, me, 0, keepdims=False)
            full.append(_own_slot(land, own, me))
        return full

    def weights_of(i):
        return [w_in[i].astype(BF16), w_uq[i].astype(BF16), w_ukv[i].astype(BF16), w_out[i].astype(BF16)]

    def packed(l_in, l_uq, l_ukv, l_out):
        cols = lambda t: jnp.transpose(t, (1, 0, 2)).reshape(t.shape[1], NDEV * t.shape[2])
        wkk, wkv = _pack_w_ukv(cols(l_ukv))
        return _pack_w_in(l_in), _pack_w_uq(cols(l_uq)), wkk, wkv, l_out.reshape(D, D)

    wb = weights_of(0)
    gat_m, tok_m = _xchg_start([meta_tokens], ["gather"], "gather_start_meta")
    gat_a, tok = _xchg_start([wb[0]], ["gather"], "gather_start_0a", after=tok_m)
    buckets = _bias_buckets(S)
    biases = _build_bias(buckets, _after(rel_bias_table, tok), sink_a, "build_bias")
    tb = 256 if T % 256 == 0 else tq
    tq_tab = jnp.tile(_rope_tables(S, Lp, NOPE, True), (1, Bl, 1))
    key_pad = jnp.where((jnp.arange(Lp) >= S + NMETA)[:, None] & (jnp.arange(LANES) == MASK_LANE)[None, :], NEG, 0.0)
    tk_tab = jnp.tile(jnp.concatenate([_rope_tables(S, Lp, 0, False), key_pad.astype(F32)[None]], axis=0), (1, Bl, 1))
    (l_meta,) = finish(gat_m, biases[0], "gather_wait_meta")
    meta_f = jnp.transpose(l_meta, (1, 0, 2)).reshape(NMETA, D)
    tail = jnp.concatenate([meta_f, jnp.zeros((BLK - NMETA, D), F32)], axis=0)
    h = jnp.concatenate([x, jnp.broadcast_to(tail[None], (Bl, BLK, D))], axis=1).reshape(T, D)
    (l_in,) = finish(gat_a, [h, tq_tab, tk_tab], "gather_wait_0a")
    gat_b, tok = _xchg_start(wb[1:], ["gather"] * 3, "gather_start_0b", after=l_in)

    saved = []
    w_out_f = [None] * depth
    gat_next = None
    for i in range(depth):
        g_in, g_q, g_kv = norm_in[i][None], norm_q_lat[i][None], norm_kv_lat[i][None]
        g_a, g_b = norm_out_a[i][None], norm_out_b[i][None]
        if i == 0:
            proj = _inproj_fwd(h, _after(g_in, tok), _pack_w_in(l_in), tm,
                               f"inproj_fwd_{i}")
            l_uq, l_ukv, l_out = finish(gat_b, proj, "gather_wait_0b")
            wx, wq, wkk, wkv, w_out_f[i] = packed(l_in, l_uq, l_ukv, l_out)
        else:
            wx, wq, wkk, wkv, w_out_f[i] = packed(*finish(gat_next, h, f"gather_wait_{i}"))
            proj = _inproj_fwd(h, g_in, wx, tm, f"inproj_fwd_{i}")
        if i + 1 < depth:
            gat_next, tok = _xchg_start(weights_of(i + 1), ["gather"] * 4, f"gather_start_{i + 1}", after=w_out_f[i])
            g_q = _after(g_q, tok)
        q, k, v = _mla_prep_fwd(proj, g_q, g_kv, wq, wkk, wkv, tq_tab, tk_tab, tm, f"mla_prep_fwd_{i}")
        ya, lse_a = _win_fwd(proj, biases[i], Bl, S, f"win_fwd_{i}")
        yb, lse_b = _mla_fwd(q, k, v, Bl, Lp, f"mla_fwd_{i}")
        h_new = _out_fwd(ya, yb, proj, g_a, g_b, w_out_f[i], h, tm, f"out_fwd_{i}")
        saved.append((h, proj, q, k, v, ya, lse_a, yb, lse_b, wx, wq, wkk, wkv))
        h = h_new

    dh, loss_acc, dg_final = _loss_head(h, norm_final[None], loss_target, Bl, S, "loss_head")

    g_n_in, g_nq, g_nkv, g_na, g_nb = ([None] * depth for _ in range(5))
    dbias_all, sc_out, sc_rest = [None] * depth, [None] * depth, [None] * depth
    split = lambda t, n: jnp.transpose(t.reshape(t.shape[0], NDEV, n), (1, 0, 2)).astype(BF16)
    tok = None
    for i in reversed(range(depth)):
        h_in, proj, q, k, v, ya, lse_a, yb, lse_b, wx, wq, wkk, wkv = saved[i]
        g_in, g_q, g_kv = norm_in[i][None], norm_q_lat[i][None], norm_kv_lat[i][None]
        g_a, g_b = norm_out_a[i][None], norm_out_b[i][None]
        if tok is not None:
            g_a = _after(g_a, tok)
        dya, dyb, dga, dgb, g_w_out, g_na[i], g_nb[i] = _out_bwd(dh, ya, yb, proj, g_a, g_b, w_out_f[i], tb, f"out_bwd_{i}")
        sc_out[i], tok = _xchg_start([g_w_out.reshape(NDEV, D // NDEV, D).astype(BF16)], ["scatter"], f"scatter_start_{i}a")
        dq, dk, dv = _mla_bwd(q, k, v, yb, dyb, lse_b, tok, Bl, Lp, f"mla_bwd_{i}")
        dqa, dka, dva, dbias = _win_bwd(proj, biases[i], ya, dya, lse_a, Bl, S, f"win_bwd_{i}")
        dcq, dckv, dkr, dwq, dwkk, dwkv, g_nq[i], g_nkv[i] = _mla_prep_bwd(
            dq, dk, dv, proj, g_q, g_kv, wq, wkk, wkv, tq_tab, tk_tab, tm, f"mla_prep_bwd_{i}")
        pieces = (dqa, dka, dva, dga, dcq, dckv, dkr, dgb)
        dbias_all[i] = dbias
        rest = [split(_unpack_w_uq_grad(dwq), 768 // NDEV), split(_unpack_w_ukv_grad(dwkk, dwkv), 1024 // NDEV)]
        dh, dwx, g_n_in[i] = _inproj_bwd(pieces, h_in, g_in, wx, dh, tb, f"inproj_bwd_{i}")
        sc_rest[i], tok = _xchg_start([_unpack_w_in_grad(dwx).astype(BF16)] + rest, ["scatter"] * 3, f"scatter_start_{i}b")

    dh3 = dh.reshape(Bl, Lp, D)
    grad_x = dh3[:, :S]
    g_meta = jnp.transpose(jnp.sum(dh3[:, S:S + NMETA], axis=0).reshape(NMETA, NDEV, D // NDEV), (1, 0, 2))
    dtab, dsink = _table_grad(dbias_all, buckets, tok, "table_grad")
    two_d = lambda a: a.reshape(1, -1) if a.ndim == 1 else a
    small_w = [rel_bias_table, norm_in, sink_a, norm_q_lat, norm_kv_lat, norm_out_a, norm_out_b, norm_final]
    small_m = [m_rel_bias_table, m_norm_in, m_sink_a, m_norm_q_lat, m_norm_kv_lat, m_norm_out_a, m_norm_out_b, m_norm_final]
    small_v = [v_rel_bias_table, v_norm_in, v_sink_a, v_norm_q_lat, v_norm_kv_lat, v_norm_out_a, v_norm_out_b, v_norm_final]
    small_g = [jnp.transpose(dtab[:, :N_BUCKETS]), jnp.concatenate(g_n_in), dsink[:, :, 0], jnp.concatenate(g_nq),
               jnp.concatenate(g_nkv), jnp.concatenate(g_na), jnp.concatenate(g_nb), dg_final]
    sc_small, tok = _xchg_start(small_g + [loss_acc, g_meta], ["gather"] * (len(small_g) + 1) + ["scatter"],
                                "scatter_start_small")

    res = {}

    def update(n, r, w, m, v):
        outs = _adamw_param(r, w, m, v, f"adamw_{n}")
        res[n] = list(outs)
        return outs[0]

    r_out = [finish(sc_out[i], tok, f"scatter_wait_{i}a")[0] for i in range(depth)]
    done = update("w_out", r_out, w_out, m_w_out, v_w_out)
    r_in, r_uq, r_ukv = [None] * depth, [None] * depth, [None] * depth
    for i in reversed(range(depth)):
        r_in[i], r_uq[i], r_ukv[i] = finish(sc_rest[i], [done, grad_x], f"scatter_wait_{i}b")
    r_small = finish(sc_small, r_in[0], "scatter_wait_small")
    update("w_in", r_in, w_in, m_w_in, v_w_in)
    update("w_uq", r_uq, w_uq, m_w_uq, v_w_uq)
    update("w_ukv", r_ukv, w_ukv, m_w_ukv, v_w_ukv)
    update("meta_tokens", [r_small[-1]], meta_tokens[None], m_meta_tokens[None], v_meta_tokens[None])
    res["meta_tokens"] = [o[0] for o in res["meta_tokens"]]
    names_small = ["rel_bias_table", "norm_in", "sink_a", "norm_q_lat", "norm_kv_lat", "norm_out_a", "norm_out_b", "norm_final"]
    outs = _adamw_small(r_small[:-2], [two_d(a) for a in small_w], [two_d(a) for a in small_m], [two_d(a) for a in small_v],
                        r_small[-2], "adamw_replicated")
    loss = outs[-1][0, 0]
    ns = len(names_small)
    for j, n in enumerate(names_small):
        res[n] = [outs[kk * ns + j].reshape(small_w[j].shape) for kk in range(4)]
    order = ["meta_tokens", "rel_bias_table", "norm_in", "w_in", "sink_a", "norm_q_lat", "w_uq", "norm_kv_lat", "w_ukv",
             "norm_out_a", "norm_out_b", "w_out", "norm_final"]
    return (loss, grad_x, *[res[n][kk] for kk in range(4) for n in order])
```

```python
import functools
import math

import numpy as np
import jax
import jax.numpy as jnp
from jax import lax
from jax.experimental import pallas as pl
from jax.experimental.pallas import tpu as pltpu

F32, BF16 = jnp.float32, jnp.bfloat16
D = 1024
NMETA = 16
BLK = 128
A_HEADS, A_KV, A_DH, A_W = 8, 2, 64, 512
B_HEADS, NOPE, ROPE, B_V, B_W = 8, 64, 32, 64, 512
Q_RANK, KV_RANK = 256, 128
IN_W = 2208
N_BUCKETS, MAX_DIST = 32, 128
THETA = 10000.0
EPS = 1e-6
NEG = -1e30
A_SCALE = A_DH ** -0.5
B_SCALE = (NOPE + ROPE) ** -0.5
LOG2E = math.log2(math.e)
MASK_LANE = NOPE + ROPE
SUM_LANE = B_V
LANES = 128
NDEV = 8
MESH = pl.DeviceIdType.MESH

NX = 2560
QA0, KA0, VA0, GA0, CQ0, CKV0, KR0, GB0 = 0, 512, 768, 1024, 1536, 1792, 1920, 2048

ADAM_LR, ADAM_B1, ADAM_B2, ADAM_EPS, ADAM_WD, ADAM_STEP = 0.001, 0.9, 0.999, 1e-08, 0.01, 10


def _dot(a, b):
    return jnp.dot(a, b, preferred_element_type=F32)


def _dot_nt(a, b):
    return lax.dot_general(a, b, (((1,), (1,)), ((), ())), preferred_element_type=F32)


def _dot_tn(a, b):
    return lax.dot_general(a, b, (((0,), (0,)), ((), ())), preferred_element_type=F32)


def _lane(shape):
    return lax.broadcasted_iota(jnp.int32, shape, len(shape) - 1)


def _rstd(x):
    return lax.rsqrt(jnp.mean(x * x, axis=-1, keepdims=True) + EPS)


def _rms_bwd(dn, x, r, g):
    z = dn * g
    dx = r * z - x * (r * r * r) * jnp.mean(z * x, axis=-1, keepdims=True)
    return dx, jnp.sum(dn * (x * r), axis=0, keepdims=True)


def _row(i):
    return lambda *ids: (ids[0], i)


def _const2(*ids):
    return (0, 0)


def _params(*sem):
    return pltpu.CompilerParams(dimension_semantics=sem)


SHARD_W = IN_W // NDEV


def _pack_w_in(shards):
    def nat(a, b):
        parts = []
        while a < b:
            p, e = a // SHARD_W, min(b, (a // SHARD_W + 1) * SHARD_W)
            parts.append(shards[p][:, a - p * SHARD_W:e - p * SHARD_W])
            a = e
        return parts
    dup = lambda a: nat(a, a + 64) * 2 + nat(a + 64, a + 128) * 2
    zeros = [jnp.zeros((shards.shape[1], LANES - ROPE), shards.dtype)]
    return jnp.concatenate(nat(0, 512) + dup(512) + dup(640) + nat(768, 1280) + nat(1280, 1536) + nat(1536, 1664)
                           + nat(1664, 1696) + zeros + nat(1696, 2208), axis=1)


_NAT_GROUPS = ((0, 512, QA0), (512, 576, KA0), (576, 640, KA0 + 128), (640, 704, VA0), (704, 768, VA0 + 128),
               (768, 1280, GA0), (1280, 1536, CQ0), (1536, 1664, CKV0), (1664, 1696, KR0), (1696, 2208, GB0))


def _unpack_w_in_grad(g):
    def nat(a, b):
        lo, _, pk = next(grp for grp in _NAT_GROUPS if grp[0] <= a < grp[1])
        t = g[:, pk + a - lo:pk + b - lo]
        return t + g[:, pk + 64 + a - lo:pk + 64 + b - lo] if 512 <= a < 768 else t
    cuts = sorted({c for grp in _NAT_GROUPS for c in grp[:2]} | {p * SHARD_W for p in range(NDEV + 1)})
    shards = [[] for _ in range(NDEV)]
    for a, b in zip(cuts[:-1], cuts[1:]):
        shards[a // SHARD_W].append(nat(a, b))
    return jnp.stack([jnp.concatenate(parts, axis=1) for parts in shards])


def _pack_w_uq(w):
    t = w.reshape(Q_RANK, B_HEADS, NOPE + ROPE)
    t = jnp.concatenate([t, jnp.zeros((Q_RANK, B_HEADS, LANES - NOPE - ROPE), w.dtype)], axis=-1)
    return t.reshape(Q_RANK, B_HEADS * LANES)


def _unpack_w_uq_grad(g):
    return g.reshape(Q_RANK, B_HEADS, LANES)[:, :, :NOPE + ROPE].reshape(Q_RANK, B_HEADS * (NOPE + ROPE))


def _pack_w_ukv(w):
    t = w.reshape(KV_RANK, B_HEADS, NOPE + B_V)
    z = jnp.zeros((KV_RANK, B_HEADS, LANES - NOPE), w.dtype)
    pad = lambda u: jnp.concatenate([u, z], axis=-1).reshape(KV_RANK, B_HEADS * LANES)
    return pad(t[:, :, :NOPE]), pad(t[:, :, NOPE:])


def _unpack_w_ukv_grad(gk, gv):
    head = lambda g: g.reshape(KV_RANK, B_HEADS, LANES)[:, :, :NOPE]
    return jnp.concatenate([head(gk), head(gv)], axis=-1).reshape(KV_RANK, B_HEADS * (NOPE + B_V))


def _t5_bucket(rel):
    nb = N_BUCKETS // 2
    max_exact = nb // 2
    ret = jnp.where(rel > 0, nb, 0)
    n = jnp.abs(rel)
    nf = jnp.maximum(n, 1).astype(F32)
    large = max_exact + (jnp.log(nf / max_exact) / math.log(MAX_DIST / max_exact) * (nb - max_exact)).astype(jnp.int32)
    large = jnp.minimum(large, nb - 1)
    return ret + jnp.where(n < max_exact, n, large)


def _bias_buckets(S):
    q = np.arange(BLK)[:, None]
    k = np.arange(4 * BLK)[None, :]
    is_meta_key = (k >= 3 * BLK) & (k < 3 * BLK + NMETA)
    mi = k - 3 * BLK
    rels, valids = [], []
    for shift in (0, BLK, 2 * BLK):
        rel_real = k - shift - q
        valid_real = (k < 3 * BLK) & (np.abs(rel_real) <= BLK)
        far = -(NMETA + MAX_DIST + BLK)
        rel_meta = (mi - (NMETA + q)) if shift == 0 else np.full_like(k + q, far)
        rels.append(np.where(is_meta_key, rel_meta, rel_real))
        valids.append(valid_real | is_meta_key)
    qm = q < NMETA
    rel_real = NMETA + k - q
    valid_real = (k < BLK) & (np.abs(rel_real) <= BLK) & qm
    rels.append(np.where(is_meta_key, mi - q, rel_real))
    valids.append(valid_real | is_meta_key)
    rel = jnp.asarray(np.stack(rels).astype(np.int32))
    valid = jnp.asarray(np.stack(valids))
    return jnp.where(valid, _t5_bucket(rel), -1).astype(jnp.int32)


def _rope_tables(S, Lp, off, passthrough):
    half = ROPE // 2
    r = np.arange(Lp)
    pos = np.where(r < S, NMETA + r, np.where(r < S + NMETA, r - S, 0)).astype(np.float32)
    freqs = THETA ** (-jnp.arange(half, dtype=F32) / half)
    ang = jnp.asarray(pos)[:, None] * freqs[None, :]
    cos, sin = jnp.cos(ang), jnp.sin(ang)
    z = lambda n: jnp.zeros((Lp, n), F32)
    head = jnp.ones((Lp, off), F32) if passthrough else z(off)
    c = jnp.concatenate([head, cos, cos, z(LANES - off - ROPE)], axis=1)
    s1 = jnp.concatenate([z(off), -sin, z(LANES - off - half)], axis=1)
    s2 = jnp.concatenate([z(off + half), sin, z(LANES - off - ROPE)], axis=1)
    return jnp.stack([c, s1, s2])


def _rope(x, t):
    return x * t[0] + pltpu.roll(x, LANES - 16, 1) * t[1] + pltpu.roll(x, 16, 1) * t[2]


def _rope_t(dy, t):
    return dy * t[0] + pltpu.roll(dy * t[1], 16, 1) + pltpu.roll(dy * t[2], LANES - 16, 1)


def _me_and_peers():
    x, y, c = lax.axis_index("x"), lax.axis_index("y"), lax.axis_index("c")
    flip = lambda v, b: 1 - v if b else v
    peers = [(flip(x, k & 4), flip(y, k & 2), flip(c, k & 1)) for k in range(1, NDEV)]
    return 4 * x + 2 * y + c, peers


def _xchg_start(srcs, kinds, name, after=None):
    n = len(srcs)
    lands = [lax.empty((NDEV,) + s.shape[-2:], s.dtype) for s in srcs]
    extra = [] if after is None else [after]

    def body(*refs):
        src_refs, land_refs = refs[:n], refs[n:2 * n]
        ssems, rsems = refs[2 * n + len(extra):3 * n + len(extra)], refs[3 * n + len(extra):4 * n + len(extra)]
        token = refs[6 * n + len(extra)]
        me, peers = _me_and_peers()
        for it in range(n):
            for k, (px, py, pc) in enumerate(peers):
                src = src_refs[it] if kinds[it] == "gather" else src_refs[it].at[4 * px + 2 * py + pc]
                pltpu.make_async_remote_copy(src_ref=src, dst_ref=land_refs[it].at[me], send_sem=ssems[it].at[k],
                                             recv_sem=rsems[it].at[k], device_id=(px, py, pc), device_id_type=MESH).start()
        token[...] = jnp.zeros_like(token)

    hbm = pl.BlockSpec(memory_space=pltpu.HBM)
    sem = pl.BlockSpec(memory_space=pltpu.SEMAPHORE)
    outs = pl.pallas_call(
        body, name=name,
        out_shape=tuple([pltpu.SemaphoreType.DMA((NDEV - 1,))] * (2 * n) + [pltpu.HBM(a.shape, a.dtype) for a in srcs + lands]
                        + [jax.ShapeDtypeStruct((8, LANES), F32)]),
        in_specs=[hbm] * (2 * n) + [pl.BlockSpec(memory_space=pl.ANY)] * len(extra),
        out_specs=tuple([sem] * (2 * n) + [hbm] * (2 * n) + [pl.BlockSpec(memory_space=pltpu.VMEM)]),
        input_output_aliases={i: 2 * n + i for i in range(2 * n)},
        compiler_params=pltpu.CompilerParams(has_side_effects=pltpu.SideEffectType.DATAFLOW_SIDE_EFFECTING),
    )(*[pltpu.with_memory_space_constraint(a, pltpu.HBM) for a in srcs + lands], *extra)
    handles = [(outs[it], outs[n + it], outs[2 * n + it], outs[3 * n + it], kinds[it]) for it in range(n)]
    return handles, outs[4 * n]


def _xchg_wait(handles, after, name):
    n = len(handles)
    after = list(after) if isinstance(after, (list, tuple)) else [after]

    def body(*refs):
        src_refs, land_refs = refs[:n], refs[n:2 * n]
        ssems, rsems = refs[2 * n:3 * n], refs[3 * n:4 * n]
        me, peers = _me_and_peers()
        for it in range(n):
            for k, (px, py, pc) in enumerate(peers):
                src = src_refs[it] if handles[it][4] == "gather" else src_refs[it].at[4 * px + 2 * py + pc]
                cp = pltpu.make_async_remote_copy(src_ref=src, dst_ref=land_refs[it].at[me], send_sem=ssems[it].at[k],
                                                  recv_sem=rsems[it].at[k], device_id=(px, py, pc), device_id_type=MESH)
                cp.wait_send()
                cp.wait_recv()

    hbm = pl.BlockSpec(memory_space=pltpu.HBM)
    sem = pl.BlockSpec(memory_space=pltpu.SEMAPHORE)
    srcs, lands = [h[2] for h in handles], [h[3] for h in handles]
    outs = pl.pallas_call(
        body, name=name, out_shape=tuple(pltpu.HBM(a.shape, a.dtype) for a in srcs + lands),
        in_specs=[hbm] * (2 * n) + [sem] * (2 * n) + [pl.BlockSpec(memory_space=pl.ANY)] * len(after),
        out_specs=tuple([hbm] * (2 * n)),
        input_output_aliases={i: i for i in range(2 * n)},
        compiler_params=pltpu.CompilerParams(has_side_effects=pltpu.SideEffectType.DATAFLOW_SIDE_EFFECTING),
    )(*srcs, *lands, *[h[0] for h in handles], *[h[1] for h in handles], *after)
    return list(outs[:n]), list(outs[n:])


def _own_slot(land, own, me):
    return lax.dynamic_update_slice(land, own[None].astype(land.dtype), (me, 0, 0))


def _after(x, token):
    return x + token[0, 0]


def _adamw_update(g, w, m, v):
    mn = ADAM_B1 * m + (1.0 - ADAM_B1) * g
    vn = ADAM_B2 * v + (1.0 - ADAM_B2) * (g * g)
    m_hat = mn / (1.0 - ADAM_B1 ** ADAM_STEP)
    v_hat = vn / (1.0 - ADAM_B2 ** ADAM_STEP)
    return -ADAM_LR * (m_hat / (jnp.sqrt(v_hat) + ADAM_EPS) + ADAM_WD * w), mn, vn


def _sum_partials(r_ref):
    g = r_ref[0].astype(F32)
    for p in range(1, NDEV):
        g = g + r_ref[p].astype(F32)
    return g


def _adamw_param(recvs, w, m, v, name):
    depth, r, c = w.shape
    tr = min(r, 256)
    nl = len(recvs)

    def body(*refs):
        r_refs = refs[:nl]
        w_ref, m_ref, v_ref, g_out, d_out, m_out, v_out = refs[nl:]
        for li in range(nl):
            @pl.when(pl.program_id(0) == li)
            def _(li=li):
                g = _sum_partials(r_refs[li])
                g_out[0] = g
                d_out[0], m_out[0], v_out[0] = _adamw_update(g, w_ref[0], m_ref[0], v_ref[0])

    blk = pl.BlockSpec((1, tr, c), lambda l, i: (l, i, 0))
    return pl.pallas_call(
        body, name=name, grid=(depth, r // tr),
        in_specs=[pl.BlockSpec((NDEV, tr, c), lambda l, i: (0, i, 0))] * nl + [blk, blk, blk],
        out_specs=(blk, blk, blk, blk), out_shape=(jax.ShapeDtypeStruct(w.shape, F32),) * 4,
        compiler_params=_params("arbitrary", "arbitrary"),
    )(*recvs, w, m, v)


def _adamw_small(recvs, ws, ms, vs, loss_parts, name):
    n = len(ws)

    def body(*refs):
        r_refs, w_refs, m_refs, v_refs = refs[:n], refs[n:2 * n], refs[2 * n:3 * n], refs[3 * n:4 * n]
        outs = refs[4 * n + 1:]
        for j in range(n):
            g = _sum_partials(r_refs[j])
            outs[j][...] = g
            outs[n + j][...], outs[2 * n + j][...], outs[3 * n + j][...] = _adamw_update(
                g, w_refs[j][...], m_refs[j][...], v_refs[j][...])
        outs[4 * n][...] = _sum_partials(refs[4 * n])

    vm = pl.BlockSpec(memory_space=pltpu.VMEM)
    return pl.pallas_call(
        body, name=name, in_specs=[vm] * (4 * n + 1), out_specs=tuple([vm] * (4 * n + 1)),
        out_shape=tuple(jax.ShapeDtypeStruct(a.shape, F32) for a in ws) * 4 + (jax.ShapeDtypeStruct(loss_parts.shape[1:], F32),),
    )(*recvs, *ws, *ms, *vs, loss_parts)


def _inproj_fwd(h, g, wx, tm, name):
    T = h.shape[0]

    def body(h_ref, g_ref, w_ref, o_ref):
        x = h_ref[...]
        u = (x * _rstd(x) * g_ref[...]).astype(BF16)
        o_ref[...] = _dot(u, w_ref[...])

    return pl.pallas_call(
        body, name=name, grid=(T // tm,),
        in_specs=[pl.BlockSpec((tm, D), _row(0)), pl.BlockSpec((1, D), _const2), pl.BlockSpec((D, NX), _const2)],
        out_specs=pl.BlockSpec((tm, NX), _row(0)), out_shape=jax.ShapeDtypeStruct((T, NX), F32),
        compiler_params=_params("parallel"),
    )(h, g, wx)


def _mla_prep_fwd(proj, gq, gkv, wq, wkk, wkv, tq, tk, tm, name):
    T = proj.shape[0]

    def body(cq_ref, ckv_ref, kr_ref, gq_ref, gkv_ref, wq_ref, wkk_ref, wkv_ref, tq_ref, tk_ref, q_ref, k_ref, v_ref):
        cq = cq_ref[...]
        q = _dot((cq * _rstd(cq) * gq_ref[...]).astype(BF16), wq_ref[...])
        tqv = tq_ref[...]
        lane = _lane((1, LANES))
        one = jnp.where(lane == MASK_LANE, 1.0, 0.0)
        for h in range(B_HEADS):
            cs = slice(LANES * h, LANES * (h + 1))
            q_ref[:, cs] = (_rope(q[:, cs], tqv) * (B_SCALE * LOG2E) + one).astype(BF16)
        tkv = tk_ref[...]
        ksh = pltpu.roll(_rope(kr_ref[...], tkv), NOPE, 1) + tkv[3]
        ckv = ckv_ref[...]
        cb = (ckv * _rstd(ckv) * gkv_ref[...]).astype(BF16)
        kn = _dot(cb, wkk_ref[...])
        vn = _dot(cb, wkv_ref[...])
        minus = jnp.where((lane == SUM_LANE) | (lane == SUM_LANE + 1), -1.0, 0.0)
        for h in range(B_HEADS):
            cs = slice(LANES * h, LANES * (h + 1))
            k_ref[:, cs] = (kn[:, cs] + ksh).astype(BF16)
            v_ref[:, cs] = (vn[:, cs] + minus).astype(BF16)

    tab = lambda n: pl.BlockSpec((n, tm, LANES), lambda i: (0, i, 0))
    wide = pl.BlockSpec((tm, 1024), _row(0))
    return pl.pallas_call(
        body, name=name, grid=(T // tm,),
        in_specs=[pl.BlockSpec((tm, 256), _row(CQ0 // 256)), pl.BlockSpec((tm, 128), _row(CKV0 // 128)),
                  pl.BlockSpec((tm, 128), _row(KR0 // 128)), pl.BlockSpec((1, Q_RANK), _const2),
                  pl.BlockSpec((1, KV_RANK), _const2), pl.BlockSpec((Q_RANK, 1024), _const2),
                  pl.BlockSpec((KV_RANK, 1024), _const2), pl.BlockSpec((KV_RANK, 1024), _const2), tab(3), tab(4)],
        out_specs=(wide, wide, wide), out_shape=(jax.ShapeDtypeStruct((T, 1024), BF16),) * 3,
        compiler_params=_params("parallel"),
    )(proj, proj, proj, gq, gkv, wq, wkk, wkv, tq, tk)


SINK_SLOT = 4 * BLK - 1


def _build_bias(buckets, table, sinks, name):
    depth = sinks.shape[0]

    def body(tab_ref, sink_ref, b_ref, *o_refs):
        bidx = b_ref[0]
        slot = _lane(bidx.shape)
        for h in range(A_HEADS):
            acc = jnp.full(bidx.shape, NEG, F32)
            for j in range(N_BUCKETS):
                acc = jnp.where(bidx == j, tab_ref[j, h] * LOG2E, acc)
            for l in range(depth):
                o_refs[l][0, h] = jnp.where(slot == SINK_SLOT, sink_ref[l, h] * LOG2E, acc)

    smem = pl.BlockSpec(memory_space=pltpu.SMEM)
    blk = pl.BlockSpec((1, A_HEADS, BLK, 4 * BLK), lambda i: (i, 0, 0, 0))
    return pl.pallas_call(
        body, name=name, grid=(4,),
        in_specs=[smem, smem, pl.BlockSpec((1, BLK, 4 * BLK), lambda i: (i, 0, 0))],
        out_specs=tuple([blk] * depth), out_shape=tuple([jax.ShapeDtypeStruct((4, A_HEADS, BLK, 4 * BLK), F32)] * depth),
        compiler_params=_params("parallel"),
    )(table, sinks, buckets)


def _win_base(n, NB):
    return jnp.where(n == NB, 0, jnp.clip(n - 1, 0, NB - 3))


def _win_variant(n, NB):
    return jnp.where(n == 0, 0, jnp.where(n < NB - 1, 1, jnp.where(n == NB - 1, 2, 3)))


def _win_kv(k_ref, v_ref, base, S, kvh):
    cs = slice(LANES * kvh, LANES * (kvh + 1))
    k2 = jnp.concatenate([k_ref[pl.ds(base, 3 * BLK), cs], k_ref[S:S + BLK, cs]], axis=0).astype(BF16)
    v2 = jnp.concatenate([v_ref[pl.ds(base, 3 * BLK), cs], v_ref[S:S + BLK, cs]], axis=0)
    lane = _lane(v2.shape)
    v2 = jnp.where(lane < A_DH, v2, jnp.where(lane < A_DH + 2, -1.0, 0.0)).astype(BF16)
    return k2, v2


def _win_fwd(proj, bias, Bl, S, name):
    T = proj.shape[0]
    Lp = S + BLK
    NB = S // BLK
    nblk = Lp // BLK
    G = A_HEADS // A_KV

    def body(q_ref, k_ref, v_ref, b_ref, o_ref, lse_ref):
        n = pl.program_id(1)
        base = pl.multiple_of(_win_base(n, NB) * BLK, BLK)
        lane = _lane((BLK, LANES))
        lo = lane < A_DH
        kv = [_win_kv(k_ref, v_ref, base, S, kvh) for kvh in range(A_KV)]
        scores = []
        for h in range(A_HEADS):
            qp = q_ref[:, LANES * (h // 2):LANES * (h // 2 + 1)]
            qm = (jnp.where(lo if h % 2 == 0 else ~lo, qp, 0.0) * (A_SCALE * LOG2E)).astype(BF16)
            scores.append(_dot_nt(qm, kv[h // G][0]) + b_ref[0, h])
        ms = [jnp.max(s, axis=-1, keepdims=True) for s in scores]
        es = [jnp.exp2(s - m).astype(BF16) for s, m in zip(scores, ms)]
        outs = []
        t = jnp.zeros((BLK, LANES), F32)
        for h in range(A_HEADS):
            o2 = _dot(es[h], kv[h // G][1])
            l = -jnp.sum(jnp.where(lane == A_DH, o2, 0.0), axis=-1, keepdims=True)
            outs.append(o2 * (1.0 / l))
            t = jnp.where(lane == h, ms[h] + jnp.log(l) * LOG2E, t)
        for j in range(A_HEADS // 2):
            o_ref[:, LANES * j:LANES * (j + 1)] = jnp.where(lo, outs[2 * j], pltpu.roll(outs[2 * j + 1], A_DH, 1))
        lse_ref[...] = t

    qrow = lambda b, n: (b * nblk + n, 0)
    return pl.pallas_call(
        body, name=name, grid=(Bl, nblk),
        in_specs=[pl.BlockSpec((BLK, A_W), qrow),
                  pl.BlockSpec((Lp, 256), lambda b, n: (b, KA0 // 256)), pl.BlockSpec((Lp, 256), lambda b, n: (b, VA0 // 256)),
                  pl.BlockSpec((1, A_HEADS, BLK, 4 * BLK), lambda b, n: (_win_variant(n, NB), 0, 0, 0))],
        out_specs=(pl.BlockSpec((BLK, A_W), qrow), pl.BlockSpec((BLK, LANES), qrow)),
        out_shape=(jax.ShapeDtypeStruct((T, A_W), F32), jax.ShapeDtypeStruct((T, LANES), F32)),
        compiler_params=_params("parallel", "arbitrary"),
    )(proj, proj, proj, bias)


def _q_tiles(Lp):
    S = Lp - BLK
    return S // 256, S


def _mla_fwd(q, k, v, Bl, Lp, name):
    T = q.shape[0]
    n_big, S = _q_tiles(Lp)

    def body(q_ref, k_ref, v_ref, o_ref, lse_ref):
        def qtiles(starts, nq):
            lane = _lane((nq, LANES))
            cols = [slice(LANES * hh, LANES * (hh + 1)) for hh in range(2)]
            scores = [[_dot_nt(q_ref[pl.ds(r0, nq), cs], k_ref[:, cs]) for cs in cols] for r0 in starts]
            for r0, tile_scores in zip(starts, scores):
                outs = []
                t = jnp.zeros((nq, LANES), F32)
                for hh in range(2):
                    cs, s = cols[hh], tile_scores[hh]
                    m = jnp.max(s, axis=-1, keepdims=True)
                    o2 = _dot(jnp.exp2(s - m).astype(BF16), v_ref[:, cs])
                    l = -jnp.sum(jnp.where(lane == SUM_LANE, o2, 0.0), axis=-1, keepdims=True)
                    outs.append(o2 * (1.0 / l))
                    t = jnp.where(lane == hh, m + jnp.log(l) * LOG2E, t)
                o_ref[pl.ds(r0, nq), :] = jnp.where(lane < B_V, outs[0], pltpu.roll(outs[1], B_V, 1))
                lse_ref[pl.ds(r0, nq), :] = t

        def step(i, c):
            qtiles([pl.multiple_of(i * 512, 256), pl.multiple_of(i * 512 + 256, 256)], 256)
            return c

        lax.fori_loop(0, n_big // 2, step, 0)
        if n_big % 2:
            qtiles([(n_big - 1) * 256], 256)
        qtiles([S], BLK)

    pair = lambda b, j: (b, j)
    wide, narrow = pl.BlockSpec((Lp, 256), pair), pl.BlockSpec((Lp, LANES), pair)
    return pl.pallas_call(
        body, name=name, grid=(Bl, B_HEADS // 2), in_specs=[wide, wide, wide], out_specs=(narrow, narrow),
        out_shape=(jax.ShapeDtypeStruct((T, B_W), F32), jax.ShapeDtypeStruct((T, B_W), F32)),
        compiler_params=_params("parallel", "parallel"),
    )(q, k, v)


def _gated(y, gate, gain):
    r = _rstd(y)
    nrm = y * r
    sg = jax.nn.sigmoid(gate)
    return r, nrm, sg, nrm * gain, gate * sg


def _out_fwd(ya, yb, proj, na, nb, wout, h, tm, name):
    T = h.shape[0]

    def body(ya_ref, yb_ref, ga_ref, gb_ref, na_ref, nb_ref, w_ref, h_ref, o_ref):
        _, _, _, n_a, sl_a = _gated(ya_ref[...], ga_ref[...], na_ref[...])
        _, _, _, n_b, sl_b = _gated(yb_ref[...], gb_ref[...], nb_ref[...])
        acc = _dot((n_a * sl_a).astype(BF16), w_ref[0:A_W, :]) + _dot((n_b * sl_b).astype(BF16), w_ref[A_W:A_W + B_W, :])
        o_ref[...] = h_ref[...] + acc

    half = pl.BlockSpec((tm, 512), _row(0))
    return pl.pallas_call(
        body, name=name, grid=(T // tm,),
        in_specs=[half, half, pl.BlockSpec((tm, 512), _row(GA0 // 512)), pl.BlockSpec((tm, 512), _row(GB0 // 512)),
                  pl.BlockSpec((1, 512), _const2), pl.BlockSpec((1, 512), _const2), pl.BlockSpec((D, D), _const2),
                  pl.BlockSpec((tm, D), _row(0))],
        out_specs=pl.BlockSpec((tm, D), _row(0)), out_shape=jax.ShapeDtypeStruct((T, D), F32),
        compiler_params=_params("parallel"),
    )(ya, yb, proj, proj, na, nb, wout, h)


def _loss_head(h, gf, target, Bl, S, name):
    T = h.shape[0]
    nblk = (S + BLK) // BLK
    NB = S // BLK

    def body(h_ref, g_ref, t_ref, dh_ref, loss_ref, dg_ref):
        b, n = pl.program_id(0), pl.program_id(1)

        @pl.when((b == 0) & (n == 0))
        def _():
            loss_ref[...] = jnp.zeros_like(loss_ref)
            dg_ref[...] = jnp.zeros_like(dg_ref)

        @pl.when(n < NB)
        def _():
            x = h_ref[...]
            g = g_ref[...]
            r = _rstd(x)
            err = x * r * g - t_ref[0]
            loss_ref[...] += 0.5 * jnp.sum(jnp.mean(err * err, axis=-1, keepdims=True))
            dx, dg = _rms_bwd(err * (1.0 / D), x, r, g)
            dh_ref[...] = dx
            dg_ref[...] += dg

        @pl.when(n >= NB)
        def _():
            dh_ref[...] = jnp.zeros_like(dh_ref)

    return pl.pallas_call(
        body, name=name, grid=(Bl, nblk),
        in_specs=[pl.BlockSpec((BLK, D), lambda b, n: (b * nblk + n, 0)), pl.BlockSpec((1, D), lambda b, n: (0, 0)),
                  pl.BlockSpec((1, BLK, D), lambda b, n: (b, jnp.minimum(n, NB - 1), 0))],
        out_specs=(pl.BlockSpec((BLK, D), lambda b, n: (b * nblk + n, 0)), pl.BlockSpec((8, LANES), lambda b, n: (0, 0)),
                   pl.BlockSpec((1, D), lambda b, n: (0, 0))),
        out_shape=(jax.ShapeDtypeStruct((T, D), F32), jax.ShapeDtypeStruct((8, LANES), F32), jax.ShapeDtypeStruct((1, D), F32)),
        compiler_params=_params("arbitrary", "arbitrary"),
    )(h, gf, target)


def _out_bwd(dh, ya, yb, proj, na, nb, wout, tm, name):
    T = dh.shape[0]

    def body(dh_ref, ya_ref, yb_ref, ga_ref, gb_ref, na_ref, nb_ref, w_ref,
             dya_ref, dyb_ref, dga_ref, dgb_ref, dw_ref, dna_ref, dnb_ref):
        @pl.when(pl.program_id(0) == 0)
        def _():
            dw_ref[...] = jnp.zeros_like(dw_ref)
            dna_ref[...] = jnp.zeros_like(dna_ref)
            dnb_ref[...] = jnp.zeros_like(dnb_ref)

        dhb = dh_ref[...].astype(BF16)
        dy = _dot_nt(dhb, w_ref[...])
        parts = ((ya_ref, ga_ref, na_ref, dya_ref, dga_ref, dna_ref, 0), (yb_ref, gb_ref, nb_ref, dyb_ref, dgb_ref, dnb_ref, A_W))
        for y_ref, gate_ref, gain_ref, dy_out, dgate_out, dgain_out, c0 in parts:
            y, gate, gain = y_ref[...], gate_ref[...], gain_ref[...]
            r, nrm, sg, n_g, sl = _gated(y, gate, gain)
            dyp = dy[:, c0:c0 + 512]
            dgate_out[...] = dyp * n_g * (sg * (1.0 + gate * (1.0 - sg)))
            dx, dgain = _rms_bwd(dyp * sl, y, r, gain)
            dy_out[...] = dx
            dgain_out[...] += dgain
            dw_ref[c0:c0 + 512, :] += _dot_tn((n_g * sl).astype(BF16), dhb)

    half = pl.BlockSpec((tm, 512), _row(0))
    vec = pl.BlockSpec((1, 512), _const2)
    return pl.pallas_call(
        body, name=name, grid=(T // tm,),
        in_specs=[pl.BlockSpec((tm, D), _row(0)), half, half, pl.BlockSpec((tm, 512), _row(GA0 // 512)),
                  pl.BlockSpec((tm, 512), _row(GB0 // 512)), vec, vec, pl.BlockSpec((D, D), _const2)],
        out_specs=(half, half, half, half, pl.BlockSpec((D, D), _const2), vec, vec),
        out_shape=(jax.ShapeDtypeStruct((T, 512), F32),) * 4 + (jax.ShapeDtypeStruct((D, D), F32),)
        + (jax.ShapeDtypeStruct((1, 512), F32),) * 2,
        compiler_params=_params("arbitrary"),
    )(dh, ya, yb, proj, proj, na, nb, wout)


def _mla_bwd(q, k, v, o, do, lse, after, Bl, Lp, name):
    T = q.shape[0]
    n_big, S = _q_tiles(Lp)

    def body(q_ref, k_ref, v_ref, o_ref, do_ref, lse_ref, _, dq_ref, dk_ref, dv_ref):
        dk_ref[...] = jnp.zeros_like(dk_ref)
        dv_ref[...] = jnp.zeros_like(dv_ref)

        def qtile(r0, nq):
            lane = _lane((nq, LANES))
            dop = do_ref[pl.ds(r0, nq), :]
            op = o_ref[pl.ds(r0, nq), :]
            lsev = lse_ref[pl.ds(r0, nq), :]
            for hh in range(2):
                cs = slice(LANES * hh, LANES * (hh + 1))
                dom = jnp.where((lane < B_V) if hh == 0 else (lane >= B_V), dop, 0.0)
                delta = jnp.sum(dom * op, axis=-1, keepdims=True)
                d_hi = delta.astype(BF16).astype(F32)
                x = dom if hh == 0 else pltpu.roll(dom, B_V, 1)
                domx = jnp.where(lane == SUM_LANE, d_hi, jnp.where(lane == SUM_LANE + 1, delta - d_hi, x)).astype(BF16)
                qh, kh, vh = q_ref[pl.ds(r0, nq), cs], k_ref[:, cs], v_ref[:, cs]
                lse_h = jnp.sum(jnp.where(lane == hh, lsev, 0.0), axis=-1, keepdims=True)
                p = jnp.exp2(_dot_nt(qh, kh) - lse_h)
                ds = (p * _dot_nt(domx, vh)).astype(BF16)
                dq_ref[pl.ds(r0, nq), cs] = _dot(ds, kh) * B_SCALE
                dk_ref[:, cs] += _dot_tn(ds, qh)
                dv_ref[:, cs] += _dot_tn(p.astype(BF16), domx)

        def step(i, c):
            qtile(pl.multiple_of(i * 512, 256), 256)
            qtile(pl.multiple_of(i * 512 + 256, 256), 256)
            return c

        lax.fori_loop(0, n_big // 2, step, 0)
        if n_big % 2:
            qtile((n_big - 1) * 256, 256)
        qtile(S, BLK)
        dk_ref[...] = dk_ref[...] * (1.0 / LOG2E)

    pair = lambda b, j: (b, j)
    wide, narrow = pl.BlockSpec((Lp, 256), pair), pl.BlockSpec((Lp, LANES), pair)
    return pl.pallas_call(
        body, name=name, grid=(Bl, B_HEADS // 2),
        in_specs=[wide, wide, wide, narrow, narrow, narrow, pl.BlockSpec(memory_space=pl.ANY)],
        out_specs=(wide, wide, wide), out_shape=(jax.ShapeDtypeStruct((T, 1024), F32),) * 3,
        compiler_params=_params("parallel", "parallel"),
    )(q, k, v, o, do, lse, after)


def _win_bwd(proj, bias, ya, dya, lse, Bl, S, name):
    T = proj.shape[0]
    Lp = S + BLK
    NB = S // BLK
    nblk = Lp // BLK
    G = A_HEADS // A_KV

    def body(q_ref, k_ref, v_ref, b_ref, o_ref, do_ref, lse_ref, dq_ref, dk_ref, dv_ref, db_ref):
        n = pl.program_id(1)

        @pl.when(n == 0)
        def _():
            dk_ref[...] = jnp.zeros_like(dk_ref)
            dv_ref[...] = jnp.zeros_like(dv_ref)

        @pl.when((n == 0) | (n == 1) | (n == NB - 1) | (n == NB))
        def _():
            db_ref[...] = jnp.zeros_like(db_ref)

        base = pl.multiple_of(_win_base(n, NB) * BLK, BLK)
        lane = _lane((BLK, LANES))
        lo = lane < A_DH
        lsev = lse_ref[...]
        kv = [_win_kv(k_ref, v_ref, base, S, kvh) for kvh in range(A_KV)]
        qms, pbs, domxs, dsls = [], [], [], []
        for h in range(A_HEADS):
            k2, v2 = kv[h // G]
            ps = slice(LANES * (h // 2), LANES * (h // 2 + 1))
            qm = (jnp.where(lo if h % 2 == 0 else ~lo, q_ref[:, ps], 0.0) * (A_SCALE * LOG2E)).astype(BF16)
            lse_h = jnp.sum(jnp.where(lane == h, lsev, 0.0), axis=-1, keepdims=True)
            p = jnp.exp2(_dot_nt(qm, k2) + b_ref[0, h] - lse_h)
            dop, op = do_ref[:, ps], o_ref[:, ps]
            if h % 2 == 1:
                dop, op = pltpu.roll(dop, A_DH, 1), pltpu.roll(op, A_DH, 1)
            dom = jnp.where(lo, dop, 0.0)
            delta = jnp.sum(dom * op, axis=-1, keepdims=True)
            d_hi = delta.astype(BF16).astype(F32)
            domx = jnp.where(lane == A_DH, d_hi, jnp.where(lane == A_DH + 1, delta - d_hi, dom)).astype(BF16)
            dsc = p * _dot_nt(domx, v2)
            db_ref[0, 0, h] += dsc
            qms.append(qm)
            pbs.append(p.astype(BF16))
            domxs.append(domx)
            dsls.append(dsc.astype(BF16))
        dqs = [_dot(dsls[h], kv[h // G][0]) * A_SCALE for h in range(A_HEADS)]
        for kvh in range(A_KV):
            cs = slice(LANES * kvh, LANES * (kvh + 1))
            group = range(kvh * G, (kvh + 1) * G)
            stack = lambda parts: jnp.concatenate([parts[h] for h in group], axis=0)
            dk_acc = _dot_tn(stack(dsls), stack(qms)) * (1.0 / LOG2E)
            dv_acc = _dot_tn(stack(pbs), stack(domxs))
            dv_acc = jnp.where(_lane(dv_acc.shape) < A_DH, dv_acc, 0.0)
            dk_ref[pl.ds(base, 3 * BLK), cs] += dk_acc[0:3 * BLK]
            dk_ref[S:S + BLK, cs] += dk_acc[3 * BLK:4 * BLK]
            dv_ref[pl.ds(base, 3 * BLK), cs] += dv_acc[0:3 * BLK]
            dv_ref[S:S + BLK, cs] += dv_acc[3 * BLK:4 * BLK]
        for j in range(A_HEADS // 2):
            dq_ref[:, LANES * j:LANES * (j + 1)] = jnp.where(lo, dqs[2 * j], dqs[2 * j + 1])

    qrow = lambda b, n: (b * nblk + n, 0)
    kvs = pl.BlockSpec((Lp, 256), lambda b, n: (b, 0))
    return pl.pallas_call(
        body, name=name, grid=(Bl, nblk),
        in_specs=[pl.BlockSpec((BLK, A_W), qrow),
                  pl.BlockSpec((Lp, 256), lambda b, n: (b, KA0 // 256)), pl.BlockSpec((Lp, 256), lambda b, n: (b, VA0 // 256)),
                  pl.BlockSpec((1, A_HEADS, BLK, 4 * BLK), lambda b, n: (_win_variant(n, NB), 0, 0, 0)),
                  pl.BlockSpec((BLK, A_W), qrow), pl.BlockSpec((BLK, A_W), qrow), pl.BlockSpec((BLK, LANES), qrow)],
        out_specs=(pl.BlockSpec((BLK, A_W), qrow), kvs, kvs,
                   pl.BlockSpec((1, 1, A_HEADS, BLK, 4 * BLK), lambda b, n: (b, _win_variant(n, NB), 0, 0, 0))),
        out_shape=(jax.ShapeDtypeStruct((T, A_W), F32), jax.ShapeDtypeStruct((T, 256), F32),
                   jax.ShapeDtypeStruct((T, 256), F32), jax.ShapeDtypeStruct((Bl, 4, A_HEADS, BLK, 4 * BLK), F32)),
        compiler_params=_params("parallel", "arbitrary"),
    )(proj, proj, proj, bias, ya, dya, lse)


def _table_grad(dbias_list, buckets, after, name):
    nl = len(dbias_list)
    Bl = dbias_list[0].shape[0]

    def body(*refs):
        d_refs = refs[:nl]
        b_ref, _, o_ref, s_ref, acc, part = refs[nl:]
        v, b = pl.program_id(0), pl.program_id(1)

        @pl.when((v == 0) & (b == 0))
        def _():
            o_ref[...] = jnp.zeros_like(o_ref)
            s_ref[...] = jnp.zeros_like(s_ref)

        slot = _lane((BLK, 4 * BLK))
        for l in range(nl):
            for h in range(A_HEADS):
                col = jnp.sum(jnp.where(slot == SINK_SLOT, d_refs[l][0, 0, h], 0.0), axis=0, keepdims=True)
                s_ref[l, h:h + 1, :] += jnp.sum(col, axis=1, keepdims=True)

        tot = d_refs[0][0, 0]
        for r in d_refs[1:]:
            tot = tot + r[0, 0]

        @pl.when(b == 0)
        def _():
            acc[...] = tot

        @pl.when(b > 0)
        def _():
            acc[...] += tot

        @pl.when(b == Bl - 1)
        def _():
            bidx = b_ref[0]

            def step(j, c):
                mask = bidx == j
                for h in range(A_HEADS):
                    part[h, j] = jnp.sum(jnp.where(mask, acc[h], 0.0).reshape(BLK // 8, 8, 4 * BLK), axis=0)
                return c

            lax.fori_loop(0, N_BUCKETS, step, 0)
            rows = lax.broadcasted_iota(jnp.int32, (N_BUCKETS, LANES), 0)
            lanes = _lane((N_BUCKETS, LANES))
            for h in range(A_HEADS):
                col = jnp.sum(jnp.sum(part[h], axis=1), axis=-1, keepdims=True)
                o_ref[h:h + 1, :] += jnp.sum(jnp.where(rows == lanes, col, 0.0), axis=0, keepdims=True)

    return pl.pallas_call(
        body, name=name, grid=(4, Bl),
        in_specs=[pl.BlockSpec((1, 1, A_HEADS, BLK, 4 * BLK), lambda v, b: (b, v, 0, 0, 0))] * nl
        + [pl.BlockSpec((1, BLK, 4 * BLK), lambda v, b: (v, 0, 0)), pl.BlockSpec(memory_space=pl.ANY)],
        out_specs=(pl.BlockSpec((8, LANES), lambda v, b: (0, 0)), pl.BlockSpec((nl, 8, LANES), lambda v, b: (0, 0, 0))),
        out_shape=(jax.ShapeDtypeStruct((8, LANES), F32), jax.ShapeDtypeStruct((nl, 8, LANES), F32)),
        scratch_shapes=[pltpu.VMEM((A_HEADS, BLK, 4 * BLK), F32), pltpu.VMEM((A_HEADS, N_BUCKETS, 8, 4 * BLK), F32)],
        compiler_params=_params("arbitrary", "arbitrary"),
    )(*dbias_list, buckets, after)


def _mla_prep_bwd(dq, dk, dv, proj, gq, gkv, wq, wkk, wkv, tq, tk, tm, name):
    T = proj.shape[0]

    def body(dq_ref, dk_ref, dv_ref, cq_ref, ckv_ref, gq_ref, gkv_ref, wq_ref, wkk_ref, wkv_ref, tq_ref, tk_ref,
             dcq_ref, dckv_ref, dkr_ref, dwq_ref, dwkk_ref, dwkv_ref, dgq_ref, dgkv_ref):
        @pl.when(pl.program_id(0) == 0)
        def _():
            for r in (dwq_ref, dwkk_ref, dwkv_ref, dgq_ref, dgkv_ref):
                r[...] = jnp.zeros_like(r)

        tqv = tq_ref[...]
        dqp = jnp.concatenate([_rope_t(dq_ref[:, LANES * h:LANES * (h + 1)], tqv) for h in range(B_HEADS)],
                              axis=1).astype(BF16)
        cq, gq_ = cq_ref[...], gq_ref[...]
        rq = _rstd(cq)
        dwq_ref[...] += _dot_tn((cq * rq * gq_).astype(BF16), dqp)
        dx, dg = _rms_bwd(_dot_nt(dqp, wq_ref[...]), cq, rq, gq_)
        dcq_ref[...] = dx
        dgq_ref[...] += dg

        dkv_ = dk_ref[...]
        dks = dkv_[:, 0:LANES]
        for h in range(1, B_HEADS):
            dks = dks + dkv_[:, LANES * h:LANES * (h + 1)]
        dkr_ref[...] = _rope_t(pltpu.roll(dks, NOPE, 1), tk_ref[...])
        dkb = dkv_.astype(BF16)
        dvb = dv_ref[...].astype(BF16)
        ckv, gkv_ = ckv_ref[...], gkv_ref[...]
        rk = _rstd(ckv)
        cb = (ckv * rk * gkv_).astype(BF16)
        dwkk_ref[...] += _dot_tn(cb, dkb)
        dwkv_ref[...] += _dot_tn(cb, dvb)
        dx, dg = _rms_bwd(_dot_nt(dkb, wkk_ref[...]) + _dot_nt(dvb, wkv_ref[...]), ckv, rk, gkv_)
        dckv_ref[...] = dx
        dgkv_ref[...] += dg

    tab = lambda n: pl.BlockSpec((n, tm, LANES), lambda i: (0, i, 0))
    wide = pl.BlockSpec((tm, 1024), _row(0))
    return pl.pallas_call(
        body, name=name, grid=(T // tm,),
        in_specs=[wide, wide, wide, pl.BlockSpec((tm, 256), _row(CQ0 // 256)),
                  pl.BlockSpec((tm, 128), _row(CKV0 // 128)), pl.BlockSpec((1, Q_RANK), _const2),
                  pl.BlockSpec((1, KV_RANK), _const2), pl.BlockSpec((Q_RANK, 1024), _const2),
                  pl.BlockSpec((KV_RANK, 1024), _const2), pl.BlockSpec((KV_RANK, 1024), _const2), tab(3), tab(4)],
        out_specs=(pl.BlockSpec((tm, 256), _row(0)), pl.BlockSpec((tm, 128), _row(0)), pl.BlockSpec((tm, 128), _row(0)),
                   pl.BlockSpec((Q_RANK, 1024), _const2), pl.BlockSpec((KV_RANK, 1024), _const2),
                   pl.BlockSpec((KV_RANK, 1024), _const2), pl.BlockSpec((1, Q_RANK), _const2),
                   pl.BlockSpec((1, KV_RANK), _const2)),
        out_shape=(jax.ShapeDtypeStruct((T, 256), F32), jax.ShapeDtypeStruct((T, 128), F32),
                   jax.ShapeDtypeStruct((T, 128), F32), jax.ShapeDtypeStruct((Q_RANK, 1024), F32),
                   jax.ShapeDtypeStruct((KV_RANK, 1024), F32), jax.ShapeDtypeStruct((KV_RANK, 1024), F32),
                   jax.ShapeDtypeStruct((1, Q_RANK), F32), jax.ShapeDtypeStruct((1, KV_RANK), F32)),
        compiler_params=_params("arbitrary"),
    )(dq, dk, dv, proj, proj, gq, gkv, wq, wkk, wkv, tq, tk)


def _inproj_bwd(pieces, h, g, wx, dh_out, tm, name):
    T = h.shape[0]
    nsteps = T // tm
    CH = 512

    def body(*refs):
        p_refs = refs[:len(pieces)]
        h_ref, g_ref, w_ref, dho_ref, dh_ref, dw_hbm, dg_ref, acc, sem = refs[len(pieces):]
        i = pl.program_id(0)

        @pl.when(i == 0)
        def _():
            acc[...] = jnp.zeros_like(acc)
            dg_ref[...] = jnp.zeros_like(dg_ref)

        dp = jnp.concatenate([r[...].astype(BF16) for r in p_refs], axis=1)
        x, gain = h_ref[...], g_ref[...]
        r = _rstd(x)
        u = (x * r * gain).astype(BF16)
        for c in range(0, NX, CH):
            acc[:, c:c + CH] += _dot_tn(u, dp[:, c:c + CH])
        dx, dg = _rms_bwd(_dot_nt(dp, w_ref[...]), x, r, gain)
        dh_ref[...] = dho_ref[...] + dx
        dg_ref[...] += dg

        @pl.when(i == nsteps - 1)
        def _():
            cp = pltpu.make_async_copy(acc, dw_hbm, sem)
            cp.start()
            cp.wait()

    return pl.pallas_call(
        body, name=name, grid=(nsteps,),
        in_specs=[pl.BlockSpec((tm, p.shape[1]), _row(0)) for p in pieces]
        + [pl.BlockSpec((tm, D), _row(0)), pl.BlockSpec((1, D), _const2), pl.BlockSpec((D, NX), _const2),
           pl.BlockSpec((tm, D), _row(0))],
        out_specs=(pl.BlockSpec((tm, D), _row(0)), pl.BlockSpec(memory_space=pl.ANY), pl.BlockSpec((1, D), _const2)),
        out_shape=(jax.ShapeDtypeStruct((T, D), F32), jax.ShapeDtypeStruct((D, NX), F32), jax.ShapeDtypeStruct((1, D), F32)),
        scratch_shapes=[pltpu.VMEM((D, NX), F32), pltpu.SemaphoreType.DMA(())],
        compiler_params=_params("arbitrary"),
    )(*pieces, h, g, wx, dh_out)


def kernel(x, meta_tokens, rel_bias_table, norm_in, w_in, sink_a, norm_q_lat, w_uq, norm_kv_lat, w_ukv, norm_out_a, norm_out_b, w_out, norm_final, loss_target, m_meta_tokens, m_rel_bias_table, m_norm_in, m_w_in, m_sink_a, m_norm_q_lat, m_w_uq, m_norm_kv_lat, m_w_ukv, m_norm_out_a, m_norm_out_b, m_w_out, m_norm_final, v_meta_tokens, v_rel_bias_table, v_norm_in, v_w_in, v_sink_a, v_norm_q_lat, v_w_uq, v_norm_kv_lat, v_w_ukv, v_norm_out_a, v_norm_out_b, v_w_out, v_norm_final):
    Bl, S, _ = x.shape
    assert S % 256 == 0 and S >= 3 * BLK, "the attention kernels tile the real tokens in 256-row blocks"
    Lp = S + BLK
    T = Bl * Lp
    tm = Lp // 4
    tq = Lp // 8
    depth = w_in.shape[0]
    me = 4 * lax.axis_index("x") + 2 * lax.axis_index("y") + lax.axis_index("c")

    def finish(handles, after, name):
        srcs, lands = _xchg_wait(handles, after, name)
        full = []
        for hd, src, land in zip(handles, srcs, lands):
            own = src if hd[4] == "gather" else lax.dynamic_index_in_dim(src, me, 0, keepdims=False)
            full.append(_own_slot(land, own, me))
        return full

    def weights_of(i):
        return [w_in[i].astype(BF16), w_uq[i].astype(BF16), w_ukv[i].astype(BF16), w_out[i].astype(BF16)]

    def packed(l_in, l_uq, l_ukv, l_out):
        cols = lambda t: jnp.transpose(t, (1, 0, 2)).reshape(t.shape[1], NDEV * t.shape[2])
        wkk, wkv = _pack_w_ukv(cols(l_ukv))
        return _pack_w_in(l_in), _pack_w_uq(cols(l_uq)), wkk, wkv, l_out.reshape(D, D)

    wbs = [weights_of(i) for i in range(depth)]
    wb = wbs[0]
    gat_m, tok_m = _xchg_start([meta_tokens], ["gather"], "gather_start_meta")
    gat_a, tok = _xchg_start([wb[0]], ["gather"], "gather_start_0a", after=tok_m)
    buckets = _bias_buckets(S)
    biases = _build_bias(buckets, _after(rel_bias_table, tok), sink_a, "build_bias")
    tb = 256 if T % 256 == 0 else tq
    tq_tab = jnp.tile(_rope_tables(S, Lp, NOPE, True), (1, Bl, 1))
    key_pad = jnp.where((jnp.arange(Lp) >= S + NMETA)[:, None] & (jnp.arange(LANES) == MASK_LANE)[None, :], NEG, 0.0)
    tk_tab = jnp.tile(jnp.concatenate([_rope_tables(S, Lp, 0, False), key_pad.astype(F32)[None]], axis=0), (1, Bl, 1))
    (l_meta,) = finish(gat_m, biases[0], "gather_wait_meta")
    meta_f = jnp.transpose(l_meta, (1, 0, 2)).reshape(NMETA, D)
    tail = jnp.concatenate([meta_f, jnp.zeros((BLK - NMETA, D), F32)], axis=0)
    h = jnp.concatenate([x, jnp.broadcast_to(tail[None], (Bl, BLK, D))], axis=1).reshape(T, D)
    (l_in,) = finish(gat_a, [h, tq_tab, tk_tab] + wb[1:] + [w for ws in wbs[1:] for w in ws], "gather_wait_0a")
    gat_b, tok = _xchg_start(wb[1:], ["gather"] * 3, "gather_start_0b", after=l_in)

    saved = []
    w_out_f = [None] * depth
    gat_next = None
    for i in range(depth):
        g_in, g_q, g_kv = norm_in[i][None], norm_q_lat[i][None], norm_kv_lat[i][None]
        g_a, g_b = norm_out_a[i][None], norm_out_b[i][None]
        if i == 0:
            proj = _inproj_fwd(h, _after(g_in, tok), _pack_w_in(l_in), tm,
                               f"inproj_fwd_{i}")
            l_uq, l_ukv, l_out = finish(gat_b, proj, "gather_wait_0b")
            wx, wq, wkk, wkv, w_out_f[i] = packed(l_in, l_uq, l_ukv, l_out)
        else:
            wx, wq, wkk, wkv, w_out_f[i] = packed(*finish(gat_next, h, f"gather_wait_{i}"))
            proj = _inproj_fwd(h, g_in, wx, tm, f"inproj_fwd_{i}")
        if i + 1 < depth:
            gat_next, tok = _xchg_start(wbs[i + 1], ["gather"] * 4, f"gather_start_{i + 1}", after=w_out_f[i])
            g_q = _after(g_q, tok)
        q, k, v = _mla_prep_fwd(proj, g_q, g_kv, wq, wkk, wkv, tq_tab, tk_tab, tm, f"mla_prep_fwd_{i}")
        ya, lse_a = _win_fwd(proj, biases[i], Bl, S, f"win_fwd_{i}")
        yb, lse_b = _mla_fwd(q, k, v, Bl, Lp, f"mla_fwd_{i}")
        h_new = _out_fwd(ya, yb, proj, g_a, g_b, w_out_f[i], h, tm, f"out_fwd_{i}")
        saved.append((h, proj, q, k, v, ya, lse_a, yb, lse_b, wx, wq, wkk, wkv))
        h = h_new

    dh, loss_acc, dg_final = _loss_head(h, norm_final[None], loss_target, Bl, S, "loss_head")

    g_n_in, g_nq, g_nkv, g_na, g_nb = ([None] * depth for _ in range(5))
    dbias_all, sc_out, sc_rest = [None] * depth, [None] * depth, [None] * depth
    split = lambda t, n: jnp.transpose(t.reshape(t.shape[0], NDEV, n), (1, 0, 2)).astype(BF16)
    tok = None
    for i in reversed(range(depth)):
        h_in, proj, q, k, v, ya, lse_a, yb, lse_b, wx, wq, wkk, wkv = saved[i]
        g_in, g_q, g_kv = norm_in[i][None], norm_q_lat[i][None], norm_kv_lat[i][None]
        g_a, g_b = norm_out_a[i][None], norm_out_b[i][None]
        if tok is not None:
            g_a = _after(g_a, tok)
        dya, dyb, dga, dgb, g_w_out, g_na[i], g_nb[i] = _out_bwd(dh, ya, yb, proj, g_a, g_b, w_out_f[i], tb, f"out_bwd_{i}")
        sc_out[i], tok = _xchg_start([g_w_out.reshape(NDEV, D // NDEV, D).astype(BF16)], ["scatter"], f"scatter_start_{i}a")
        dq, dk, dv = _mla_bwd(q, k, v, yb, dyb, lse_b, tok, Bl, Lp, f"mla_bwd_{i}")
        dqa, dka, dva, dbias = _win_bwd(proj, biases[i], ya, dya, lse_a, Bl, S, f"win_bwd_{i}")
        dcq, dckv, dkr, dwq, dwkk, dwkv, g_nq[i], g_nkv[i] = _mla_prep_bwd(
            dq, dk, dv, proj, g_q, g_kv, wq, wkk, wkv, tq_tab, tk_tab, tm, f"mla_prep_bwd_{i}")
        pieces = (dqa, dka, dva, dga, dcq, dckv, dkr, dgb)
        dbias_all[i] = dbias
        rest = [split(_unpack_w_uq_grad(dwq), 768 // NDEV), split(_unpack_w_ukv_grad(dwkk, dwkv), 1024 // NDEV)]
        dh, dwx, g_n_in[i] = _inproj_bwd(pieces, h_in, g_in, wx, dh, tb, f"inproj_bwd_{i}")
        sc_rest[i], tok = _xchg_start([_unpack_w_in_grad(dwx).astype(BF16)] + rest, ["scatter"] * 3, f"scatter_start_{i}b")

    dh3 = dh.reshape(Bl, Lp, D)
    grad_x = dh3[:, :S]
    g_meta = jnp.transpose(jnp.sum(dh3[:, S:S + NMETA], axis=0).reshape(NMETA, NDEV, D // NDEV), (1, 0, 2))
    dtab, dsink = _table_grad(dbias_all, buckets, tok, "table_grad")
    two_d = lambda a: a.reshape(1, -1) if a.ndim == 1 else a
    small_w = [rel_bias_table, norm_in, sink_a, norm_q_lat, norm_kv_lat, norm_out_a, norm_out_b, norm_final]
    small_m = [m_rel_bias_table, m_norm_in, m_sink_a, m_norm_q_lat, m_norm_kv_lat, m_norm_out_a, m_norm_out_b, m_norm_final]
    small_v = [v_rel_bias_table, v_norm_in, v_sink_a, v_norm_q_lat, v_norm_kv_lat, v_norm_out_a, v_norm_out_b, v_norm_final]
    small_g = [jnp.transpose(dtab[:, :N_BUCKETS]), jnp.concatenate(g_n_in), dsink[:, :, 0], jnp.concatenate(g_nq),
               jnp.concatenate(g_nkv), jnp.concatenate(g_na), jnp.concatenate(g_nb), dg_final]
    sc_small, tok = _xchg_start(small_g + [loss_acc, g_meta], ["gather"] * (len(small_g) + 1) + ["scatter"],
                                "scatter_start_small")

    res = {}

    def update(n, r, w, m, v):
        outs = _adamw_param(r, w, m, v, f"adamw_{n}")
        res[n] = list(outs)
        return outs[0]

    r_out = [finish(sc_out[i], tok, f"scatter_wait_{i}a")[0] for i in range(depth)]
    done = update("w_out", r_out, w_out, m_w_out, v_w_out)
    r_in, r_uq, r_ukv = [None] * depth, [None] * depth, [None] * depth
    for i in reversed(range(depth)):
        r_in[i], r_uq[i], r_ukv[i] = finish(sc_rest[i], [done, grad_x], f"scatter_wait_{i}b")
    r_small = finish(sc_small, r_in[0], "scatter_wait_small")
    update("w_in", r_in, w_in, m_w_in, v_w_in)
    update("w_uq", r_uq, w_uq, m_w_uq, v_w_uq)
    update("w_ukv", r_ukv, w_ukv, m_w_ukv, v_w_ukv)
    update("meta_tokens", [r_small[-1]], meta_tokens[None], m_meta_tokens[None], v_meta_tokens[None])
    res["meta_tokens"] = [o[0] for o in res["meta_tokens"]]
    names_small = ["rel_bias_table", "norm_in", "sink_a", "norm_q_lat", "norm_kv_lat", "norm_out_a", "norm_out_b", "norm_final"]
    outs = _adamw_small(r_small[:-2], [two_d(a) for a in small_w], [two_d(a) for a in small_m], [two_d(a) for a in small_v],
                        r_small[-2], "adamw_replicated")
    loss = outs[-1][0, 0]
    ns = len(names_small)
    for j, n in enumerate(names_small):
        res[n] = [outs[kk * ns + j].reshape(small_w[j].shape) for kk in range(4)]
    order = ["meta_tokens", "rel_bias_table", "norm_in", "w_in", "sink_a", "norm_q_lat", "w_uq", "norm_kv_lat", "w_ukv",
             "norm_out_a", "norm_out_b", "w_out", "norm_final"]
    return (loss, grad_x, *[res[n][kk] for kk in range(4) for n in order])
```

```python
import math

import numpy as np
import jax
import jax.numpy as jnp
from jax import lax
from jax.experimental import pallas as pl
from jax.experimental.pallas import tpu as pltpu

F32, BF16 = jnp.float32, jnp.bfloat16
D = 1024
NMETA = 16
BLK = 128
A_HEADS, A_KV, A_DH, A_W = 8, 2, 64, 512
B_HEADS, NOPE, ROPE, B_V, B_W = 8, 64, 32, 64, 512
Q_RANK, KV_RANK = 256, 128
IN_W = 2208
N_BUCKETS, MAX_DIST = 32, 128
THETA = 10000.0
EPS = 1e-6
NEG = -1e30
A_SCALE = A_DH ** -0.5
B_SCALE = (NOPE + ROPE) ** -0.5
LOG2E = math.log2(math.e)
MASK_LANE = NOPE + ROPE
SUM_LANE = B_V
LANES = 128
NDEV = 8
MESH = pl.DeviceIdType.MESH

NX = 2560
QA0, KA0, VA0, GA0, CQ0, CKV0, KR0, GB0 = 0, 512, 768, 1024, 1536, 1792, 1920, 2048

ADAM_LR, ADAM_B1, ADAM_B2, ADAM_EPS, ADAM_WD, ADAM_STEP = 0.001, 0.9, 0.999, 1e-08, 0.01, 10


def _dot(a, b):
    return jnp.dot(a, b, preferred_element_type=F32)


def _dot_nt(a, b):
    return lax.dot_general(a, b, (((1,), (1,)), ((), ())), preferred_element_type=F32)


def _dot_tn(a, b):
    return lax.dot_general(a, b, (((0,), (0,)), ((), ())), preferred_element_type=F32)


def _lane(shape):
    return lax.broadcasted_iota(jnp.int32, shape, len(shape) - 1)


def _rstd(x):
    return lax.rsqrt(jnp.mean(x * x, axis=-1, keepdims=True) + EPS)


def _rms_bwd(dn, x, r, g):
    z = dn * g
    dx = r * z - x * (r * r * r) * jnp.mean(z * x, axis=-1, keepdims=True)
    return dx, jnp.sum(dn * (x * r), axis=0, keepdims=True)


def _row(i):
    return lambda *ids: (ids[0], i)


def _const2(*ids):
    return (0, 0)


def _params(*sem):
    return pltpu.CompilerParams(dimension_semantics=sem)


SHARD_W = IN_W // NDEV


def _pack_w_in(shards):
    def nat(a, b):
        parts = []
        while a < b:
            p, e = a // SHARD_W, min(b, (a // SHARD_W + 1) * SHARD_W)
            parts.append(shards[p][:, a - p * SHARD_W:e - p * SHARD_W])
            a = e
        return parts
    dup = lambda a: nat(a, a + 64) * 2 + nat(a + 64, a + 128) * 2
    zeros = [jnp.zeros((shards.shape[1], LANES - ROPE), shards.dtype)]
    return jnp.concatenate(nat(0, 512) + dup(512) + dup(640) + nat(768, 1280) + nat(1280, 1536) + nat(1536, 1664)
                           + nat(1664, 1696) + zeros + nat(1696, 2208), axis=1)


_NAT_GROUPS = ((0, 512, QA0), (512, 576, KA0), (576, 640, KA0 + 128), (640, 704, VA0), (704, 768, VA0 + 128),
               (768, 1280, GA0), (1280, 1536, CQ0), (1536, 1664, CKV0), (1664, 1696, KR0), (1696, 2208, GB0))


def _unpack_w_in_grad(g):
    def nat(a, b):
        lo, _, pk = next(grp for grp in _NAT_GROUPS if grp[0] <= a < grp[1])
        t = g[:, pk + a - lo:pk + b - lo]
        return t + g[:, pk + 64 + a - lo:pk + 64 + b - lo] if 512 <= a < 768 else t
    cuts = sorted({c for grp in _NAT_GROUPS for c in grp[:2]} | {p * SHARD_W for p in range(NDEV + 1)})
    shards = [[] for _ in range(NDEV)]
    for a, b in zip(cuts[:-1], cuts[1:]):
        shards[a // SHARD_W].append(nat(a, b))
    return jnp.stack([jnp.concatenate(parts, axis=1) for parts in shards])


def _pack_w_uq(w):
    t = w.reshape(Q_RANK, B_HEADS, NOPE + ROPE)
    t = jnp.concatenate([t, jnp.zeros((Q_RANK, B_HEADS, LANES - NOPE - ROPE), w.dtype)], axis=-1)
    return t.reshape(Q_RANK, B_HEADS * LANES)


def _unpack_w_uq_grad(g):
    return g.reshape(Q_RANK, B_HEADS, LANES)[:, :, :NOPE + ROPE].reshape(Q_RANK, B_HEADS * (NOPE + ROPE))


def _pack_w_ukv(w):
    t = w.reshape(KV_RANK, B_HEADS, NOPE + B_V)
    z = jnp.zeros((KV_RANK, B_HEADS, LANES - NOPE), w.dtype)
    pad = lambda u: jnp.concatenate([u, z], axis=-1).reshape(KV_RANK, B_HEADS * LANES)
    return pad(t[:, :, :NOPE]), pad(t[:, :, NOPE:])


def _unpack_w_ukv_grad(gk, gv):
    head = lambda g: g.reshape(KV_RANK, B_HEADS, LANES)[:, :, :NOPE]
    return jnp.concatenate([head(gk), head(gv)], axis=-1).reshape(KV_RANK, B_HEADS * (NOPE + B_V))


def _t5_bucket(rel):
    nb = N_BUCKETS // 2
    max_exact = nb // 2
    ret = jnp.where(rel > 0, nb, 0)
    n = jnp.abs(rel)
    nf = jnp.maximum(n, 1).astype(F32)
    large = max_exact + (jnp.log(nf / max_exact) / math.log(MAX_DIST / max_exact) * (nb - max_exact)).astype(jnp.int32)
    large = jnp.minimum(large, nb - 1)
    return ret + jnp.where(n < max_exact, n, large)


def _bias_buckets(S):
    q = np.arange(BLK)[:, None]
    k = np.arange(4 * BLK)[None, :]
    is_meta_key = (k >= 3 * BLK) & (k < 3 * BLK + NMETA)
    mi = k - 3 * BLK
    rels, valids = [], []
    for shift in (0, BLK, 2 * BLK):
        rel_real = k - shift - q
        valid_real = (k < 3 * BLK) & (np.abs(rel_real) <= BLK)
        far = -(NMETA + MAX_DIST + BLK)
        rel_meta = (mi - (NMETA + q)) if shift == 0 else np.full_like(k + q, far)
        rels.append(np.where(is_meta_key, rel_meta, rel_real))
        valids.append(valid_real | is_meta_key)
    qm = q < NMETA
    rel_real = NMETA + k - q
    valid_real = (k < BLK) & (np.abs(rel_real) <= BLK) & qm
    rels.append(np.where(is_meta_key, mi - q, rel_real))
    valids.append(valid_real | is_meta_key)
    rel = jnp.asarray(np.stack(rels).astype(np.int32))
    valid = jnp.asarray(np.stack(valids))
    return jnp.where(valid, _t5_bucket(rel), -1).astype(jnp.int32)


def _rope_tables(S, Lp, off, passthrough):
    half = ROPE // 2
    r = np.arange(Lp)
    pos = np.where(r < S, NMETA + r, np.where(r < S + NMETA, r - S, 0)).astype(np.float32)
    freqs = THETA ** (-jnp.arange(half, dtype=F32) / half)
    ang = jnp.asarray(pos)[:, None] * freqs[None, :]
    cos, sin = jnp.cos(ang), jnp.sin(ang)
    z = lambda n: jnp.zeros((Lp, n), F32)
    head = jnp.ones((Lp, off), F32) if passthrough else z(off)
    c = jnp.concatenate([head, cos, cos, z(LANES - off - ROPE)], axis=1)
    s1 = jnp.concatenate([z(off), -sin, z(LANES - off - half)], axis=1)
    s2 = jnp.concatenate([z(off + half), sin, z(LANES - off - ROPE)], axis=1)
    return jnp.stack([c, s1, s2])


def _rope(x, t):
    return x * t[0] + pltpu.roll(x, LANES - 16, 1) * t[1] + pltpu.roll(x, 16, 1) * t[2]


def _rope_t(dy, t):
    return dy * t[0] + pltpu.roll(dy * t[1], 16, 1) + pltpu.roll(dy * t[2], LANES - 16, 1)


def _me_and_peers():
    x, y, c = lax.axis_index("x"), lax.axis_index("y"), lax.axis_index("c")
    flip = lambda v, b: 1 - v if b else v
    peers = [(flip(x, k & 4), flip(y, k & 2), flip(c, k & 1)) for k in range(1, NDEV)]
    return 4 * x + 2 * y + c, peers


def _xchg_start(srcs, kinds, name, after=None):
    n = len(srcs)
    lands = [lax.empty((NDEV,) + s.shape[-2:], s.dtype) for s in srcs]
    extra = [] if after is None else [after]

    def body(*refs):
        src_refs, land_refs = refs[:n], refs[n:2 * n]
        ssems, rsems = refs[2 * n + len(extra):3 * n + len(extra)], refs[3 * n + len(extra):4 * n + len(extra)]
        token = refs[6 * n + len(extra)]
        me, peers = _me_and_peers()
        for it in range(n):
            for k, (px, py, pc) in enumerate(peers):
                src = src_refs[it] if kinds[it] == "gather" else src_refs[it].at[4 * px + 2 * py + pc]
                pltpu.make_async_remote_copy(src_ref=src, dst_ref=land_refs[it].at[me], send_sem=ssems[it].at[k],
                                             recv_sem=rsems[it].at[k], device_id=(px, py, pc), device_id_type=MESH).start()
        token[...] = jnp.zeros_like(token)

    hbm = pl.BlockSpec(memory_space=pltpu.HBM)
    sem = pl.BlockSpec(memory_space=pltpu.SEMAPHORE)
    outs = pl.pallas_call(
        body, name=name,
        out_shape=tuple([pltpu.SemaphoreType.DMA((NDEV - 1,))] * (2 * n) + [pltpu.HBM(a.shape, a.dtype) for a in srcs + lands]
                        + [jax.ShapeDtypeStruct((8, LANES), F32)]),
        in_specs=[hbm] * (2 * n) + [pl.BlockSpec(memory_space=pl.ANY)] * len(extra),
        out_specs=tuple([sem] * (2 * n) + [hbm] * (2 * n) + [pl.BlockSpec(memory_space=pltpu.VMEM)]),
        input_output_aliases={i: 2 * n + i for i in range(2 * n)},
        compiler_params=pltpu.CompilerParams(has_side_effects=pltpu.SideEffectType.DATAFLOW_SIDE_EFFECTING),
    )(*[pltpu.with_memory_space_constraint(a, pltpu.HBM) for a in srcs + lands], *extra)
    handles = [(outs[it], outs[n + it], outs[2 * n + it], outs[3 * n + it], kinds[it]) for it in range(n)]
    return handles, outs[4 * n]


def _xchg_wait(handles, after, name):
    n = len(handles)
    after = list(after) if isinstance(after, (list, tuple)) else [after]

    def body(*refs):
        src_refs, land_refs = refs[:n], refs[n:2 * n]
        ssems, rsems = refs[2 * n:3 * n], refs[3 * n:4 * n]
        me, peers = _me_and_peers()
        for it in range(n):
            for k, (px, py, pc) in enumerate(peers):
                src = src_refs[it] if handles[it][4] == "gather" else src_refs[it].at[4 * px + 2 * py + pc]
                cp = pltpu.make_async_remote_copy(src_ref=src, dst_ref=land_refs[it].at[me], send_sem=ssems[it].at[k],
                                                  recv_sem=rsems[it].at[k], device_id=(px, py, pc), device_id_type=MESH)
                cp.wait_send()
                cp.wait_recv()

    hbm = pl.BlockSpec(memory_space=pltpu.HBM)
    sem = pl.BlockSpec(memory_space=pltpu.SEMAPHORE)
    srcs, lands = [h[2] for h in handles], [h[3] for h in handles]
    outs = pl.pallas_call(
        body, name=name, out_shape=tuple(pltpu.HBM(a.shape, a.dtype) for a in srcs + lands),
        in_specs=[hbm] * (2 * n) + [sem] * (2 * n) + [pl.BlockSpec(memory_space=pl.ANY)] * len(after),
        out_specs=tuple([hbm] * (2 * n)),
        input_output_aliases={i: i for i in range(2 * n)},
        compiler_params=pltpu.CompilerParams(has_side_effects=pltpu.SideEffectType.DATAFLOW_SIDE_EFFECTING),
    )(*srcs, *lands, *[h[0] for h in handles], *[h[1] for h in handles], *after)
    return list(outs[:n]), list(outs[n:])


def _own_slot(land, own, me):
    return lax.dynamic_update_slice(land, own[None].astype(land.dtype), (me, 0, 0))


def _after(x, token):
    return x + token[0, 0]


def _adamw_update(g, w, m, v):
    mn = ADAM_B1 * m + (1.0 - ADAM_B1) * g
    vn = ADAM_B2 * v + (1.0 - ADAM_B2) * (g * g)
    m_hat = mn / (1.0 - ADAM_B1 ** ADAM_STEP)
    v_hat = vn / (1.0 - ADAM_B2 ** ADAM_STEP)
    return -ADAM_LR * (m_hat / (jnp.sqrt(v_hat) + ADAM_EPS) + ADAM_WD * w), mn, vn


def _sum_partials(r_ref):
    g = r_ref[0].astype(F32)
    for p in range(1, NDEV):
        g = g + r_ref[p].astype(F32)
    return g


def _adamw_param(recvs, w, m, v, name):
    depth, r, c = w.shape
    tr = min(r, 256)
    nl = len(recvs)

    def body(*refs):
        r_refs = refs[:nl]
        w_ref, m_ref, v_ref, g_out, d_out, m_out, v_out = refs[nl:]
        for li in range(nl):
            @pl.when(pl.program_id(0) == li)
            def _(li=li):
                g = _sum_partials(r_refs[li])
                g_out[0] = g
                d_out[0], m_out[0], v_out[0] = _adamw_update(g, w_ref[0], m_ref[0], v_ref[0])

    blk = pl.BlockSpec((1, tr, c), lambda l, i: (l, i, 0))
    return pl.pallas_call(
        body, name=name, grid=(depth, r // tr),
        in_specs=[pl.BlockSpec((NDEV, tr, c), lambda l, i: (0, i, 0))] * nl + [blk, blk, blk],
        out_specs=(blk, blk, blk, blk), out_shape=(jax.ShapeDtypeStruct(w.shape, F32),) * 4,
        compiler_params=_params("arbitrary", "arbitrary"),
    )(*recvs, w, m, v)


def _adamw_small(recvs, ws, ms, vs, loss_parts, name):
    n = len(ws)

    def body(*refs):
        r_refs, w_refs, m_refs, v_refs = refs[:n], refs[n:2 * n], refs[2 * n:3 * n], refs[3 * n:4 * n]
        outs = refs[4 * n + 1:]
        for j in range(n):
            g = _sum_partials(r_refs[j])
            outs[j][...] = g
            outs[n + j][...], outs[2 * n + j][...], outs[3 * n + j][...] = _adamw_update(
                g, w_refs[j][...], m_refs[j][...], v_refs[j][...])
        outs[4 * n][...] = _sum_partials(refs[4 * n])

    vm = pl.BlockSpec(memory_space=pltpu.VMEM)
    return pl.pallas_call(
        body, name=name, in_specs=[vm] * (4 * n + 1), out_specs=tuple([vm] * (4 * n + 1)),
        out_shape=tuple(jax.ShapeDtypeStruct(a.shape, F32) for a in ws) * 4 + (jax.ShapeDtypeStruct(loss_parts.shape[1:], F32),),
    )(*recvs, *ws, *ms, *vs, loss_parts)


def _inproj_fwd(h, g, wx, tm, name):
    T = h.shape[0]

    def body(h_ref, g_ref, w_ref, o_ref):
        x = h_ref[...]
        u = (x * _rstd(x) * g_ref[...]).astype(BF16)
        o_ref[...] = _dot(u, w_ref[...])

    return pl.pallas_call(
        body, name=name, grid=(T // tm,),
        in_specs=[pl.BlockSpec((tm, D), _row(0)), pl.BlockSpec((1, D), _const2), pl.BlockSpec((D, NX), _const2)],
        out_specs=pl.BlockSpec((tm, NX), _row(0)), out_shape=jax.ShapeDtypeStruct((T, NX), F32),
        compiler_params=_params("parallel"),
    )(h, g, wx)


def _mla_prep_fwd(proj, gq, gkv, wq, wkk, wkv, tq, tk, tm, name):
    T = proj.shape[0]

    def body(cq_ref, ckv_ref, kr_ref, gq_ref, gkv_ref, wq_ref, wkk_ref, wkv_ref, tq_ref, tk_ref, q_ref, k_ref, v_ref):
        cq = cq_ref[...]
        q = _dot((cq * _rstd(cq) * gq_ref[...]).astype(BF16), wq_ref[...])
        tqv = tq_ref[...]
        lane = _lane((1, LANES))
        one = jnp.where(lane == MASK_LANE, 1.0, 0.0)
        for h in range(B_HEADS):
            cs = slice(LANES * h, LANES * (h + 1))
            q_ref[:, cs] = (_rope(q[:, cs], tqv) * (B_SCALE * LOG2E) + one).astype(BF16)
        tkv = tk_ref[...]
        ksh = pltpu.roll(_rope(kr_ref[...], tkv), NOPE, 1) + tkv[3]
        ckv = ckv_ref[...]
        cb = (ckv * _rstd(ckv) * gkv_ref[...]).astype(BF16)
        kn = _dot(cb, wkk_ref[...])
        vn = _dot(cb, wkv_ref[...])
        minus = jnp.where((lane == SUM_LANE) | (lane == SUM_LANE + 1), -1.0, 0.0)
        for h in range(B_HEADS):
            cs = slice(LANES * h, LANES * (h + 1))
            k_ref[:, cs] = (kn[:, cs] + ksh).astype(BF16)
            v_ref[:, cs] = (vn[:, cs] + minus).astype(BF16)

    tab = lambda n: pl.BlockSpec((n, tm, LANES), lambda i: (0, i, 0))
    wide = pl.BlockSpec((tm, 1024), _row(0))
    return pl.pallas_call(
        body, name=name, grid=(T // tm,),
        in_specs=[pl.BlockSpec((tm, 256), _row(CQ0 // 256)), pl.BlockSpec((tm, 128), _row(CKV0 // 128)),
                  pl.BlockSpec((tm, 128), _row(KR0 // 128)), pl.BlockSpec((1, Q_RANK), _const2),
                  pl.BlockSpec((1, KV_RANK), _const2), pl.BlockSpec((Q_RANK, 1024), _const2),
                  pl.BlockSpec((KV_RANK, 1024), _const2), pl.BlockSpec((KV_RANK, 1024), _const2), tab(3), tab(4)],
        out_specs=(wide, wide, wide), out_shape=(jax.ShapeDtypeStruct((T, 1024), BF16),) * 3,
        compiler_params=_params("parallel"),
    )(proj, proj, proj, gq, gkv, wq, wkk, wkv, tq, tk)


SINK_SLOT = 4 * BLK - 1


def _build_bias(buckets, table, sinks, name):
    depth = sinks.shape[0]

    def body(tab_ref, sink_ref, b_ref, *o_refs):
        bidx = b_ref[0]
        slot = _lane(bidx.shape)
        for h in range(A_HEADS):
            acc = jnp.full(bidx.shape, NEG, F32)
            for j in range(N_BUCKETS):
                acc = jnp.where(bidx == j, tab_ref[j, h] * LOG2E, acc)
            for l in range(depth):
                o_refs[l][0, h] = jnp.where(slot == SINK_SLOT, sink_ref[l, h] * LOG2E, acc)

    smem = pl.BlockSpec(memory_space=pltpu.SMEM)
    blk = pl.BlockSpec((1, A_HEADS, BLK, 4 * BLK), lambda i: (i, 0, 0, 0))
    return pl.pallas_call(
        body, name=name, grid=(4,),
        in_specs=[smem, smem, pl.BlockSpec((1, BLK, 4 * BLK), lambda i: (i, 0, 0))],
        out_specs=tuple([blk] * depth), out_shape=tuple([jax.ShapeDtypeStruct((4, A_HEADS, BLK, 4 * BLK), F32)] * depth),
        compiler_params=_params("parallel"),
    )(table, sinks, buckets)


def _win_base(n, NB):
    return jnp.where(n == NB, 0, jnp.clip(n - 1, 0, NB - 3))


def _win_variant(n, NB):
    return jnp.where(n == 0, 0, jnp.where(n < NB - 1, 1, jnp.where(n == NB - 1, 2, 3)))


def _win_kv(k_ref, v_ref, base, S, kvh):
    cs = slice(LANES * kvh, LANES * (kvh + 1))
    k2 = jnp.concatenate([k_ref[pl.ds(base, 3 * BLK), cs], k_ref[S:S + BLK, cs]], axis=0).astype(BF16)
    v2 = jnp.concatenate([v_ref[pl.ds(base, 3 * BLK), cs], v_ref[S:S + BLK, cs]], axis=0)
    lane = _lane(v2.shape)
    v2 = jnp.where(lane < A_DH, v2, jnp.where(lane < A_DH + 2, -1.0, 0.0)).astype(BF16)
    return k2, v2


def _win_fwd(proj, bias, Bl, S, name):
    T = proj.shape[0]
    Lp = S + BLK
    NB = S // BLK
    nblk = Lp // BLK
    G = A_HEADS // A_KV

    def body(q_ref, k_ref, v_ref, b_ref, o_ref, lse_ref):
        n = pl.program_id(1)
        base = pl.multiple_of(_win_base(n, NB) * BLK, BLK)
        lane = _lane((BLK, LANES))
        lo = lane < A_DH
        kv = [_win_kv(k_ref, v_ref, base, S, kvh) for kvh in range(A_KV)]
        scores = []
        for h in range(A_HEADS):
            qp = q_ref[:, LANES * (h // 2):LANES * (h // 2 + 1)]
            qm = (jnp.where(lo if h % 2 == 0 else ~lo, qp, 0.0) * (A_SCALE * LOG2E)).astype(BF16)
            scores.append(_dot_nt(qm, kv[h // G][0]) + b_ref[0, h])
        ms = [jnp.max(s, axis=-1, keepdims=True) for s in scores]
        es = [jnp.exp2(s - m).astype(BF16) for s, m in zip(scores, ms)]
        outs = []
        t = jnp.zeros((BLK, LANES), F32)
        for h in range(A_HEADS):
            o2 = _dot(es[h], kv[h // G][1])
            l = -jnp.sum(jnp.where(lane == A_DH, o2, 0.0), axis=-1, keepdims=True)
            outs.append(o2 * (1.0 / l))
            t = jnp.where(lane == h, ms[h] + jnp.log(l) * LOG2E, t)
        for j in range(A_HEADS // 2):
            o_ref[:, LANES * j:LANES * (j + 1)] = jnp.where(lo, outs[2 * j], pltpu.roll(outs[2 * j + 1], A_DH, 1))
        lse_ref[...] = t

    qrow = lambda b, n: (b * nblk + n, 0)
    return pl.pallas_call(
        body, name=name, grid=(Bl, nblk),
        in_specs=[pl.BlockSpec((BLK, A_W), qrow),
                  pl.BlockSpec((Lp, 256), lambda b, n: (b, KA0 // 256)), pl.BlockSpec((Lp, 256), lambda b, n: (b, VA0 // 256)),
                  pl.BlockSpec((1, A_HEADS, BLK, 4 * BLK), lambda b, n: (_win_variant(n, NB), 0, 0, 0))],
        out_specs=(pl.BlockSpec((BLK, A_W), qrow), pl.BlockSpec((BLK, LANES), qrow)),
        out_shape=(jax.ShapeDtypeStruct((T, A_W), F32), jax.ShapeDtypeStruct((T, LANES), F32)),
        compiler_params=_params("parallel", "arbitrary"),
    )(proj, proj, proj, bias)


def _q_tiles(Lp):
    S = Lp - BLK
    return S // 256, S


def _mla_fwd(q, k, v, Bl, Lp, name):
    T = q.shape[0]
    n_big, S = _q_tiles(Lp)

    def body(q_ref, k_ref, v_ref, o_ref, lse_ref):
        def qtiles(starts, nq):
            lane = _lane((nq, LANES))
            cols = [slice(LANES * hh, LANES * (hh + 1)) for hh in range(2)]
            scores = [[_dot_nt(q_ref[pl.ds(r0, nq), cs], k_ref[:, cs]) for cs in cols] for r0 in starts]
            for r0, tile_scores in zip(starts, scores):
                outs = []
                t = jnp.zeros((nq, LANES), F32)
                for hh in range(2):
                    cs, s = cols[hh], tile_scores[hh]
                    m = jnp.max(s, axis=-1, keepdims=True)
                    o2 = _dot(jnp.exp2(s - m).astype(BF16), v_ref[:, cs])
                    l = -jnp.sum(jnp.where(lane == SUM_LANE, o2, 0.0), axis=-1, keepdims=True)
                    outs.append(o2 * (1.0 / l))
                    t = jnp.where(lane == hh, m + jnp.log(l) * LOG2E, t)
                o_ref[pl.ds(r0, nq), :] = jnp.where(lane < B_V, outs[0], pltpu.roll(outs[1], B_V, 1))
                lse_ref[pl.ds(r0, nq), :] = t

        def step(i, c):
            qtiles([pl.multiple_of(i * 512, 256), pl.multiple_of(i * 512 + 256, 256)], 256)
            return c

        lax.fori_loop(0, n_big // 2, step, 0)
        if n_big % 2:
            qtiles([(n_big - 1) * 256], 256)
        qtiles([S], BLK)

    pair = lambda b, j: (b, j)
    wide, narrow = pl.BlockSpec((Lp, 256), pair), pl.BlockSpec((Lp, LANES), pair)
    return pl.pallas_call(
        body, name=name, grid=(Bl, B_HEADS // 2), in_specs=[wide, wide, wide], out_specs=(narrow, narrow),
        out_shape=(jax.ShapeDtypeStruct((T, B_W), F32), jax.ShapeDtypeStruct((T, B_W), F32)),
        compiler_params=_params("parallel", "parallel"),
    )(q, k, v)


def _gated(y, gate, gain):
    r = _rstd(y)
    nrm = y * r
    sg = jax.nn.sigmoid(gate)
    return r, nrm, sg, nrm * gain, gate * sg


def _out_fwd(ya, yb, proj, na, nb, wout, h, tm, name):
    T = h.shape[0]

    def body(ya_ref, yb_ref, ga_ref, gb_ref, na_ref, nb_ref, w_ref, h_ref, o_ref):
        _, _, _, n_a, sl_a = _gated(ya_ref[...], ga_ref[...], na_ref[...])
        _, _, _, n_b, sl_b = _gated(yb_ref[...], gb_ref[...], nb_ref[...])
        acc = _dot((n_a * sl_a).astype(BF16), w_ref[0:A_W, :]) + _dot((n_b * sl_b).astype(BF16), w_ref[A_W:A_W + B_W, :])
        o_ref[...] = h_ref[...] + acc

    half = pl.BlockSpec((tm, 512), _row(0))
    return pl.pallas_call(
        body, name=name, grid=(T // tm,),
        in_specs=[half, half, pl.BlockSpec((tm, 512), _row(GA0 // 512)), pl.BlockSpec((tm, 512), _row(GB0 // 512)),
                  pl.BlockSpec((1, 512), _const2), pl.BlockSpec((1, 512), _const2), pl.BlockSpec((D, D), _const2),
                  pl.BlockSpec((tm, D), _row(0))],
        out_specs=pl.BlockSpec((tm, D), _row(0)), out_shape=jax.ShapeDtypeStruct((T, D), F32),
        compiler_params=_params("parallel"),
    )(ya, yb, proj, proj, na, nb, wout, h)


def _loss_head(h, gf, target, Bl, S, name):
    T = h.shape[0]
    nblk = (S + BLK) // BLK
    NB = S // BLK

    def body(h_ref, g_ref, t_ref, dh_ref, loss_ref, dg_ref):
        b, n = pl.program_id(0), pl.program_id(1)

        @pl.when((b == 0) & (n == 0))
        def _():
            loss_ref[...] = jnp.zeros_like(loss_ref)
            dg_ref[...] = jnp.zeros_like(dg_ref)

        @pl.when(n < NB)
        def _():
            x = h_ref[...]
            g = g_ref[...]
            r = _rstd(x)
            err = x * r * g - t_ref[0]
            loss_ref[...] += 0.5 * jnp.sum(jnp.mean(err * err, axis=-1, keepdims=True))
            dx, dg = _rms_bwd(err * (1.0 / D), x, r, g)
            dh_ref[...] = dx
            dg_ref[...] += dg

        @pl.when(n >= NB)
        def _():
            dh_ref[...] = jnp.zeros_like(dh_ref)

    return pl.pallas_call(
        body, name=name, grid=(Bl, nblk),
        in_specs=[pl.BlockSpec((BLK, D), lambda b, n: (b * nblk + n, 0)), pl.BlockSpec((1, D), lambda b, n: (0, 0)),
                  pl.BlockSpec((1, BLK, D), lambda b, n: (b, jnp.minimum(n, NB - 1), 0))],
        out_specs=(pl.BlockSpec((BLK, D), lambda b, n: (b * nblk + n, 0)), pl.BlockSpec((8, LANES), lambda b, n: (0, 0)),
                   pl.BlockSpec((1, D), lambda b, n: (0, 0))),
        out_shape=(jax.ShapeDtypeStruct((T, D), F32), jax.ShapeDtypeStruct((8, LANES), F32), jax.ShapeDtypeStruct((1, D), F32)),
        compiler_params=_params("arbitrary", "arbitrary"),
    )(h, gf, target)


def _out_bwd(dh, ya, yb, proj, na, nb, wout, tm, name, head=None):
    T = ya.shape[0]
    fused = head is not None
    if fused:
        hf, gf, target, S = head
        assert tm == 2 * BLK
        nblk, NB = (S + BLK) // BLK, S // BLK

    def body(*refs):
        if fused:
            (hf_ref, gf_ref, t0_ref, t1_ref, ya_ref, yb_ref, ga_ref, gb_ref, na_ref, nb_ref, w_ref,
             dya_ref, dyb_ref, dga_ref, dgb_ref, dw_ref, dna_ref, dnb_ref, dh_out, loss_ref, dgf_ref) = refs
        else:
            (dh_ref, ya_ref, yb_ref, ga_ref, gb_ref, na_ref, nb_ref, w_ref,
             dya_ref, dyb_ref, dga_ref, dgb_ref, dw_ref, dna_ref, dnb_ref) = refs
        i = pl.program_id(0)

        @pl.when(i == 0)
        def _():
            dw_ref[...] = jnp.zeros_like(dw_ref)
            dna_ref[...] = jnp.zeros_like(dna_ref)
            dnb_ref[...] = jnp.zeros_like(dnb_ref)
            if fused:
                loss_ref[...] = jnp.zeros_like(loss_ref)
                dgf_ref[...] = jnp.zeros_like(dgf_ref)

        if fused:
            x, g = hf_ref[...], gf_ref[...]
            r = _rstd(x)
            real = [(((2 * i + half) % nblk) < NB).astype(F32) for half in range(2)]
            row = lax.broadcasted_iota(jnp.int32, (tm, 1), 0)
            keep = jnp.where(row < BLK, real[0], real[1])
            err = (x * r * g - jnp.concatenate([t0_ref[0], t1_ref[0]], axis=0)) * keep
            loss_ref[...] += 0.5 * jnp.sum(jnp.mean(err * err, axis=-1, keepdims=True))
            dh, dgf = _rms_bwd(err * (1.0 / D), x, r, g)
            dh_out[...] = dh
            dgf_ref[...] += dgf
        else:
            dh = dh_ref[...]
        dhb = dh.astype(BF16)
        dy = _dot_nt(dhb, w_ref[...])
        parts = ((ya_ref, ga_ref, na_ref, dya_ref, dga_ref, dna_ref, 0), (yb_ref, gb_ref, nb_ref, dyb_ref, dgb_ref, dnb_ref, A_W))
        for y_ref, gate_ref, gain_ref, dy_out, dgate_out, dgain_out, c0 in parts:
            y, gate, gain = y_ref[...], gate_ref[...], gain_ref[...]
            r, nrm, sg, n_g, sl = _gated(y, gate, gain)
            dyp = dy[:, c0:c0 + 512]
            dgate_out[...] = dyp * n_g * (sg * (1.0 + gate * (1.0 - sg)))
            dx, dgain = _rms_bwd(dyp * sl, y, r, gain)
            dy_out[...] = dx
            dgain_out[...] += dgain
            dw_ref[c0:c0 + 512, :] += _dot_tn((n_g * sl).astype(BF16), dhb)

    half = pl.BlockSpec((tm, 512), _row(0))
    vec = pl.BlockSpec((1, 512), _const2)
    rows = pl.BlockSpec((tm, D), _row(0))
    in_specs = [half, half, pl.BlockSpec((tm, 512), _row(GA0 // 512)), pl.BlockSpec((tm, 512), _row(GB0 // 512)), vec, vec,
                pl.BlockSpec((D, D), _const2)]
    out_specs = [half, half, half, half, pl.BlockSpec((D, D), _const2), vec, vec]
    out_shape = [jax.ShapeDtypeStruct((T, 512), F32)] * 4 + [jax.ShapeDtypeStruct((D, D), F32)] + [jax.ShapeDtypeStruct((1, 512), F32)] * 2
    if fused:
        def tgt(half_):
            def index(i):
                blk = 2 * i + half_
                return (blk // nblk, jnp.minimum(blk % nblk, NB - 1), 0)
            return pl.BlockSpec((1, BLK, D), index)
        in_specs = [rows, pl.BlockSpec((1, D), _const2), tgt(0), tgt(1)] + in_specs
        args = [hf, gf, target, target, ya, yb, proj, proj, na, nb, wout]
        out_specs += [rows, pl.BlockSpec((8, LANES), _const2), pl.BlockSpec((1, D), _const2)]
        out_shape += [jax.ShapeDtypeStruct((T, D), F32), jax.ShapeDtypeStruct((8, LANES), F32), jax.ShapeDtypeStruct((1, D), F32)]
    else:
        in_specs = [rows] + in_specs
        args = [dh, ya, yb, proj, proj, na, nb, wout]
    return pl.pallas_call(
        body, name=name, grid=(T // tm,), in_specs=in_specs, out_specs=tuple(out_specs), out_shape=tuple(out_shape),
        compiler_params=_params("arbitrary"),
    )(*args)


def _mla_bwd(q, k, v, o, do, lse, after, Bl, Lp, name):
    T = q.shape[0]
    n_big, S = _q_tiles(Lp)

    def body(q_ref, k_ref, v_ref, o_ref, do_ref, lse_ref, _, dq_ref, dk_ref, dv_ref):
        dk_ref[...] = jnp.zeros_like(dk_ref)
        dv_ref[...] = jnp.zeros_like(dv_ref)

        def qtile(r0, nq):
            lane = _lane((nq, LANES))
            dop = do_ref[pl.ds(r0, nq), :]
            op = o_ref[pl.ds(r0, nq), :]
            lsev = lse_ref[pl.ds(r0, nq), :]
            for hh in range(2):
                cs = slice(LANES * hh, LANES * (hh + 1))
                dom = jnp.where((lane < B_V) if hh == 0 else (lane >= B_V), dop, 0.0)
                delta = jnp.sum(dom * op, axis=-1, keepdims=True)
                d_hi = delta.astype(BF16).astype(F32)
                x = dom if hh == 0 else pltpu.roll(dom, B_V, 1)
                domx = jnp.where(lane == SUM_LANE, d_hi, jnp.where(lane == SUM_LANE + 1, delta - d_hi, x)).astype(BF16)
                qh, kh, vh = q_ref[pl.ds(r0, nq), cs], k_ref[:, cs], v_ref[:, cs]
                lse_h = jnp.sum(jnp.where(lane == hh, lsev, 0.0), axis=-1, keepdims=True)
                p = jnp.exp2(_dot_nt(qh, kh) - lse_h)
                ds = (p * _dot_nt(domx, vh)).astype(BF16)
                dq_ref[pl.ds(r0, nq), cs] = _dot(ds, kh) * B_SCALE
                dk_ref[:, cs] += _dot_tn(ds, qh)
                dv_ref[:, cs] += _dot_tn(p.astype(BF16), domx)

        def step(i, c):
            qtile(pl.multiple_of(i * 512, 256), 256)
            qtile(pl.multiple_of(i * 512 + 256, 256), 256)
            return c

        lax.fori_loop(0, n_big // 2, step, 0)
        if n_big % 2:
            qtile((n_big - 1) * 256, 256)
        qtile(S, BLK)
        dk_ref[...] = dk_ref[...] * (1.0 / LOG2E)

    pair = lambda b, j: (b, j)
    wide, narrow = pl.BlockSpec((Lp, 256), pair), pl.BlockSpec((Lp, LANES), pair)
    return pl.pallas_call(
        body, name=name, grid=(Bl, B_HEADS // 2),
        in_specs=[wide, wide, wide, narrow, narrow, narrow, pl.BlockSpec(memory_space=pl.ANY)],
        out_specs=(wide, wide, wide), out_shape=(jax.ShapeDtypeStruct((T, 1024), F32),) * 3,
        compiler_params=_params("parallel", "parallel"),
    )(q, k, v, o, do, lse, after)


def _win_bwd(proj, bias, ya, dya, lse, Bl, S, name):
    T = proj.shape[0]
    Lp = S + BLK
    NB = S // BLK
    nblk = Lp // BLK
    G = A_HEADS // A_KV

    def body(q_ref, k_ref, v_ref, b_ref, o_ref, do_ref, lse_ref, dq_ref, dk_ref, dv_ref, db_ref):
        n = pl.program_id(1)

        @pl.when(n == 0)
        def _():
            dk_ref[...] = jnp.zeros_like(dk_ref)
            dv_ref[...] = jnp.zeros_like(dv_ref)

        @pl.when((n == 0) | (n == 1) | (n == NB - 1) | (n == NB))
        def _():
            db_ref[...] = jnp.zeros_like(db_ref)

        base = pl.multiple_of(_win_base(n, NB) * BLK, BLK)
        lane = _lane((BLK, LANES))
        lo = lane < A_DH
        lsev = lse_ref[...]
        kv = [_win_kv(k_ref, v_ref, base, S, kvh) for kvh in range(A_KV)]
        qms, pbs, domxs, dsls = [], [], [], []
        for h in range(A_HEADS):
            k2, v2 = kv[h // G]
            ps = slice(LANES * (h // 2), LANES * (h // 2 + 1))
            qm = (jnp.where(lo if h % 2 == 0 else ~lo, q_ref[:, ps], 0.0) * (A_SCALE * LOG2E)).astype(BF16)
            lse_h = jnp.sum(jnp.where(lane == h, lsev, 0.0), axis=-1, keepdims=True)
            p = jnp.exp2(_dot_nt(qm, k2) + b_ref[0, h] - lse_h)
            dop, op = do_ref[:, ps], o_ref[:, ps]
            if h % 2 == 1:
                dop, op = pltpu.roll(dop, A_DH, 1), pltpu.roll(op, A_DH, 1)
            dom = jnp.where(lo, dop, 0.0)
            delta = jnp.sum(dom * op, axis=-1, keepdims=True)
            d_hi = delta.astype(BF16).astype(F32)
            domx = jnp.where(lane == A_DH, d_hi, jnp.where(lane == A_DH + 1, delta - d_hi, dom)).astype(BF16)
            dsc = p * _dot_nt(domx, v2)
            db_ref[0, 0, h] += dsc
            qms.append(qm)
            pbs.append(p.astype(BF16))
            domxs.append(domx)
            dsls.append(dsc.astype(BF16))
        dqs = [_dot(dsls[h], kv[h // G][0]) * A_SCALE for h in range(A_HEADS)]
        for kvh in range(A_KV):
            cs = slice(LANES * kvh, LANES * (kvh + 1))
            group = range(kvh * G, (kvh + 1) * G)
            stack = lambda parts: jnp.concatenate([parts[h] for h in group], axis=0)
            dk_acc = _dot_tn(stack(dsls), stack(qms)) * (1.0 / LOG2E)
            dv_acc = _dot_tn(stack(pbs), stack(domxs))
            dv_acc = jnp.where(_lane(dv_acc.shape) < A_DH, dv_acc, 0.0)
            dk_ref[pl.ds(base, 3 * BLK), cs] += dk_acc[0:3 * BLK]
            dk_ref[S:S + BLK, cs] += dk_acc[3 * BLK:4 * BLK]
            dv_ref[pl.ds(base, 3 * BLK), cs] += dv_acc[0:3 * BLK]
            dv_ref[S:S + BLK, cs] += dv_acc[3 * BLK:4 * BLK]
        for j in range(A_HEADS // 2):
            dq_ref[:, LANES * j:LANES * (j + 1)] = jnp.where(lo, dqs[2 * j], dqs[2 * j + 1])

    qrow = lambda b, n: (b * nblk + n, 0)
    kvs = pl.BlockSpec((Lp, 256), lambda b, n: (b, 0))
    return pl.pallas_call(
        body, name=name, grid=(Bl, nblk),
        in_specs=[pl.BlockSpec((BLK, A_W), qrow),
                  pl.BlockSpec((Lp, 256), lambda b, n: (b, KA0 // 256)), pl.BlockSpec((Lp, 256), lambda b, n: (b, VA0 // 256)),
                  pl.BlockSpec((1, A_HEADS, BLK, 4 * BLK), lambda b, n: (_win_variant(n, NB), 0, 0, 0)),
                  pl.BlockSpec((BLK, A_W), qrow), pl.BlockSpec((BLK, A_W), qrow), pl.BlockSpec((BLK, LANES), qrow)],
        out_specs=(pl.BlockSpec((BLK, A_W), qrow), kvs, kvs,
                   pl.BlockSpec((1, 1, A_HEADS, BLK, 4 * BLK), lambda b, n: (b, _win_variant(n, NB), 0, 0, 0))),
        out_shape=(jax.ShapeDtypeStruct((T, A_W), F32), jax.ShapeDtypeStruct((T, 256), F32),
                   jax.ShapeDtypeStruct((T, 256), F32), jax.ShapeDtypeStruct((Bl, 4, A_HEADS, BLK, 4 * BLK), F32)),
        compiler_params=_params("parallel", "arbitrary"),
    )(proj, proj, proj, bias, ya, dya, lse)


def _table_grad(dbias_list, buckets, after, name):
    nl = len(dbias_list)
    Bl = dbias_list[0].shape[0]

    def body(*refs):
        d_refs = refs[:nl]
        b_ref, _, o_ref, s_ref, acc, part = refs[nl:]
        v, b = pl.program_id(0), pl.program_id(1)

        @pl.when((v == 0) & (b == 0))
        def _():
            o_ref[...] = jnp.zeros_like(o_ref)
            s_ref[...] = jnp.zeros_like(s_ref)

        slot = _lane((BLK, 4 * BLK))
        for l in range(nl):
            for h in range(A_HEADS):
                col = jnp.sum(jnp.where(slot == SINK_SLOT, d_refs[l][0, 0, h], 0.0), axis=0, keepdims=True)
                s_ref[l, h:h + 1, :] += jnp.sum(col, axis=1, keepdims=True)

        tot = d_refs[0][0, 0]
        for r in d_refs[1:]:
            tot = tot + r[0, 0]

        @pl.when(b == 0)
        def _():
            acc[...] = tot

        @pl.when(b > 0)
        def _():
            acc[...] += tot

        @pl.when(b == Bl - 1)
        def _():
            bidx = b_ref[0]

            def step(j, c):
                mask = bidx == j
                for h in range(A_HEADS):
                    part[h, j] = jnp.sum(jnp.where(mask, acc[h], 0.0).reshape(BLK // 8, 8, 4 * BLK), axis=0)
                return c

            lax.fori_loop(0, N_BUCKETS, step, 0)
            rows = lax.broadcasted_iota(jnp.int32, (N_BUCKETS, LANES), 0)
            lanes = _lane((N_BUCKETS, LANES))
            for h in range(A_HEADS):
                col = jnp.sum(jnp.sum(part[h], axis=1), axis=-1, keepdims=True)
                o_ref[h:h + 1, :] += jnp.sum(jnp.where(rows == lanes, col, 0.0), axis=0, keepdims=True)

    return pl.pallas_call(
        body, name=name, grid=(4, Bl),
        in_specs=[pl.BlockSpec((1, 1, A_HEADS, BLK, 4 * BLK), lambda v, b: (b, v, 0, 0, 0))] * nl
        + [pl.BlockSpec((1, BLK, 4 * BLK), lambda v, b: (v, 0, 0)), pl.BlockSpec(memory_space=pl.ANY)],
        out_specs=(pl.BlockSpec((8, LANES), lambda v, b: (0, 0)), pl.BlockSpec((nl, 8, LANES), lambda v, b: (0, 0, 0))),
        out_shape=(jax.ShapeDtypeStruct((8, LANES), F32), jax.ShapeDtypeStruct((nl, 8, LANES), F32)),
        scratch_shapes=[pltpu.VMEM((A_HEADS, BLK, 4 * BLK), F32), pltpu.VMEM((A_HEADS, N_BUCKETS, 8, 4 * BLK), F32)],
        compiler_params=_params("arbitrary", "arbitrary"),
    )(*dbias_list, buckets, after)


def _mla_prep_bwd(dq, dk, dv, proj, gq, gkv, wq, wkk, wkv, tq, tk, tm, name):
    T = proj.shape[0]

    def body(dq_ref, dk_ref, dv_ref, cq_ref, ckv_ref, gq_ref, gkv_ref, wq_ref, wkk_ref, wkv_ref, tq_ref, tk_ref,
             dcq_ref, dckv_ref, dkr_ref, dwq_ref, dwkk_ref, dwkv_ref, dgq_ref, dgkv_ref):
        @pl.when(pl.program_id(0) == 0)
        def _():
            for r in (dwq_ref, dwkk_ref, dwkv_ref, dgq_ref, dgkv_ref):
                r[...] = jnp.zeros_like(r)

        tqv = tq_ref[...]
        dqp = jnp.concatenate([_rope_t(dq_ref[:, LANES * h:LANES * (h + 1)], tqv) for h in range(B_HEADS)],
                              axis=1).astype(BF16)
        cq, gq_ = cq_ref[...], gq_ref[...]
        rq = _rstd(cq)
        dwq_ref[...] += _dot_tn((cq * rq * gq_).astype(BF16), dqp)
        dx, dg = _rms_bwd(_dot_nt(dqp, wq_ref[...]), cq, rq, gq_)
        dcq_ref[...] = dx
        dgq_ref[...] += dg

        dkv_ = dk_ref[...]
        dks = dkv_[:, 0:LANES]
        for h in range(1, B_HEADS):
            dks = dks + dkv_[:, LANES * h:LANES * (h + 1)]
        dkr_ref[...] = _rope_t(pltpu.roll(dks, NOPE, 1), tk_ref[...])
        dkb = dkv_.astype(BF16)
        dvb = dv_ref[...].astype(BF16)
        ckv, gkv_ = ckv_ref[...], gkv_ref[...]
        rk = _rstd(ckv)
        cb = (ckv * rk * gkv_).astype(BF16)
        dwkk_ref[...] += _dot_tn(cb, dkb)
        dwkv_ref[...] += _dot_tn(cb, dvb)
        dx, dg = _rms_bwd(_dot_nt(dkb, wkk_ref[...]) + _dot_nt(dvb, wkv_ref[...]), ckv, rk, gkv_)
        dckv_ref[...] = dx
        dgkv_ref[...] += dg

    tab = lambda n: pl.BlockSpec((n, tm, LANES), lambda i: (0, i, 0))
    wide = pl.BlockSpec((tm, 1024), _row(0))
    return pl.pallas_call(
        body, name=name, grid=(T // tm,),
        in_specs=[wide, wide, wide, pl.BlockSpec((tm, 256), _row(CQ0 // 256)),
                  pl.BlockSpec((tm, 128), _row(CKV0 // 128)), pl.BlockSpec((1, Q_RANK), _const2),
                  pl.BlockSpec((1, KV_RANK), _const2), pl.BlockSpec((Q_RANK, 1024), _const2),
                  pl.BlockSpec((KV_RANK, 1024), _const2), pl.BlockSpec((KV_RANK, 1024), _const2), tab(3), tab(4)],
        out_specs=(pl.BlockSpec((tm, 256), _row(0)), pl.BlockSpec((tm, 128), _row(0)), pl.BlockSpec((tm, 128), _row(0)),
                   pl.BlockSpec((Q_RANK, 1024), _const2), pl.BlockSpec((KV_RANK, 1024), _const2),
                   pl.BlockSpec((KV_RANK, 1024), _const2), pl.BlockSpec((1, Q_RANK), _const2),
                   pl.BlockSpec((1, KV_RANK), _const2)),
        out_shape=(jax.ShapeDtypeStruct((T, 256), F32), jax.ShapeDtypeStruct((T, 128), F32),
                   jax.ShapeDtypeStruct((T, 128), F32), jax.ShapeDtypeStruct((Q_RANK, 1024), F32),
                   jax.ShapeDtypeStruct((KV_RANK, 1024), F32), jax.ShapeDtypeStruct((KV_RANK, 1024), F32),
                   jax.ShapeDtypeStruct((1, Q_RANK), F32), jax.ShapeDtypeStruct((1, KV_RANK), F32)),
        compiler_params=_params("arbitrary"),
    )(dq, dk, dv, proj, proj, gq, gkv, wq, wkk, wkv, tq, tk)


def _inproj_bwd(pieces, h, g, wx, dh_out, tm, name):
    T = h.shape[0]
    nsteps = T // tm
    CH = 512

    def body(*refs):
        p_refs = refs[:len(pieces)]
        h_ref, g_ref, w_ref, dho_ref, dh_ref, dw_hbm, dg_ref, acc, sem = refs[len(pieces):]
        i = pl.program_id(0)

        @pl.when(i == 0)
        def _():
            acc[...] = jnp.zeros_like(acc)
            dg_ref[...] = jnp.zeros_like(dg_ref)

        dp = jnp.concatenate([r[...].astype(BF16) for r in p_refs], axis=1)
        x, gain = h_ref[...], g_ref[...]
        r = _rstd(x)
        u = (x * r * gain).astype(BF16)
        for c in range(0, NX, CH):
            acc[:, c:c + CH] += _dot_tn(u, dp[:, c:c + CH])
        dx, dg = _rms_bwd(_dot_nt(dp, w_ref[...]), x, r, gain)
        dh_ref[...] = dho_ref[...] + dx
        dg_ref[...] += dg

        @pl.when(i == nsteps - 1)
        def _():
            cp = pltpu.make_async_copy(acc, dw_hbm, sem)
            cp.start()
            cp.wait()

    return pl.pallas_call(
        body, name=name, grid=(nsteps,),
        in_specs=[pl.BlockSpec((tm, p.shape[1]), _row(0)) for p in pieces]
        + [pl.BlockSpec((tm, D), _row(0)), pl.BlockSpec((1, D), _const2), pl.BlockSpec((D, NX), _const2),
           pl.BlockSpec((tm, D), _row(0))],
        out_specs=(pl.BlockSpec((tm, D), _row(0)), pl.BlockSpec(memory_space=pl.ANY), pl.BlockSpec((1, D), _const2)),
        out_shape=(jax.ShapeDtypeStruct((T, D), F32), jax.ShapeDtypeStruct((D, NX), F32), jax.ShapeDtypeStruct((1, D), F32)),
        scratch_shapes=[pltpu.VMEM((D, NX), F32), pltpu.SemaphoreType.DMA(())],
        compiler_params=_params("arbitrary"),
    )(*pieces, h, g, wx, dh_out)


def kernel(x, meta_tokens, rel_bias_table, norm_in, w_in, sink_a, norm_q_lat, w_uq, norm_kv_lat, w_ukv, norm_out_a, norm_out_b, w_out, norm_final, loss_target, m_meta_tokens, m_rel_bias_table, m_norm_in, m_w_in, m_sink_a, m_norm_q_lat, m_w_uq, m_norm_kv_lat, m_w_ukv, m_norm_out_a, m_norm_out_b, m_w_out, m_norm_final, v_meta_tokens, v_rel_bias_table, v_norm_in, v_w_in, v_sink_a, v_norm_q_lat, v_w_uq, v_norm_kv_lat, v_w_ukv, v_norm_out_a, v_norm_out_b, v_w_out, v_norm_final):
    Bl, S, _ = x.shape
    assert S % 256 == 0 and S >= 3 * BLK, "the attention kernels tile the real tokens in 256-row blocks"
    Lp = S + BLK
    T = Bl * Lp
    tm = Lp // 4
    tq = Lp // 8
    depth = w_in.shape[0]
    me = 4 * lax.axis_index("x") + 2 * lax.axis_index("y") + lax.axis_index("c")

    def finish(handles, after, name):
        srcs, lands = _xchg_wait(handles, after, name)
        full = []
        for hd, src, land in zip(handles, srcs, lands):
            own = src if hd[4] == "gather" else lax.dynamic_index_in_dim(src, me, 0, keepdims=False)
            full.append(_own_slot(land, own, me))
        return full

    def weights_of(i):
        return [w_in[i].astype(BF16), w_uq[i].astype(BF16), w_ukv[i].astype(BF16), w_out[i].astype(BF16)]

    def packed(l_in, l_uq, l_ukv, l_out):
        cols = lambda t: jnp.transpose(t, (1, 0, 2)).reshape(t.shape[1], NDEV * t.shape[2])
        wkk, wkv = _pack_w_ukv(cols(l_ukv))
        return _pack_w_in(l_in), _pack_w_uq(cols(l_uq)), wkk, wkv, l_out.reshape(D, D)

    wbs = [weights_of(i) for i in range(depth)]
    wb = wbs[0]
    gat_m, tok_m = _xchg_start([meta_tokens], ["gather"], "gather_start_meta")
    gat_a, tok = _xchg_start([wb[0]], ["gather"], "gather_start_0a", after=tok_m)
    buckets = _bias_buckets(S)
    biases = _build_bias(buckets, _after(rel_bias_table, tok), sink_a, "build_bias")
    tb = 256 if T % 256 == 0 else tq
    tq_tab = jnp.tile(_rope_tables(S, Lp, NOPE, True), (1, Bl, 1))
    key_pad = jnp.where((jnp.arange(Lp) >= S + NMETA)[:, None] & (jnp.arange(LANES) == MASK_LANE)[None, :], NEG, 0.0)
    tk_tab = jnp.tile(jnp.concatenate([_rope_tables(S, Lp, 0, False), key_pad.astype(F32)[None]], axis=0), (1, Bl, 1))
    (l_meta,) = finish(gat_m, biases[0], "gather_wait_meta")
    meta_f = jnp.transpose(l_meta, (1, 0, 2)).reshape(NMETA, D)
    tail = jnp.concatenate([meta_f, jnp.zeros((BLK - NMETA, D), F32)], axis=0)
    h = jnp.concatenate([x, jnp.broadcast_to(tail[None], (Bl, BLK, D))], axis=1).reshape(T, D)
    (l_in,) = finish(gat_a, [h, tq_tab, tk_tab] + wb[1:] + [w for ws in wbs[1:] for w in ws], "gather_wait_0a")
    gat_b, tok = _xchg_start(wb[1:], ["gather"] * 3, "gather_start_0b", after=l_in)

    saved = []
    w_out_f = [None] * depth
    gat_next = None
    for i in range(depth):
        g_in, g_q, g_kv = norm_in[i][None], norm_q_lat[i][None], norm_kv_lat[i][None]
        g_a, g_b = norm_out_a[i][None], norm_out_b[i][None]
        if i == 0:
            proj = _inproj_fwd(h, _after(g_in, tok), _pack_w_in(l_in), tm,
                               f"inproj_fwd_{i}")
            l_uq, l_ukv, l_out = finish(gat_b, proj, "gather_wait_0b")
            wx, wq, wkk, wkv, w_out_f[i] = packed(l_in, l_uq, l_ukv, l_out)
        else:
            wx, wq, wkk, wkv, w_out_f[i] = packed(*finish(gat_next, h, f"gather_wait_{i}"))
            proj = _inproj_fwd(h, g_in, wx, tm, f"inproj_fwd_{i}")
        if i + 1 < depth:
            gat_next, tok = _xchg_start(wbs[i + 1], ["gather"] * 4, f"gather_start_{i + 1}", after=w_out_f[i])
            g_q = _after(g_q, tok)
        q, k, v = _mla_prep_fwd(proj, g_q, g_kv, wq, wkk, wkv, tq_tab, tk_tab, tm, f"mla_prep_fwd_{i}")
        ya, lse_a = _win_fwd(proj, biases[i], Bl, S, f"win_fwd_{i}")
        yb, lse_b = _mla_fwd(q, k, v, Bl, Lp, f"mla_fwd_{i}")
        h_new = _out_fwd(ya, yb, proj, g_a, g_b, w_out_f[i], h, tm, f"out_fwd_{i}")
        saved.append((h, proj, q, k, v, ya, lse_a, yb, lse_b, wx, wq, wkk, wkv))
        h = h_new

    fuse_head = tb == 2 * BLK
    if not fuse_head:
        dh, loss_acc, dg_final = _loss_head(h, norm_final[None], loss_target, Bl, S, "loss_head")

    g_n_in, g_nq, g_nkv, g_na, g_nb = ([None] * depth for _ in range(5))
    dbias_all, sc_out, sc_rest = [None] * depth, [None] * depth, [None] * depth
    split = lambda t, n: jnp.transpose(t.reshape(t.shape[0], NDEV, n), (1, 0, 2)).astype(BF16)
    tok = None
    for i in reversed(range(depth)):
        h_in, proj, q, k, v, ya, lse_a, yb, lse_b, wx, wq, wkk, wkv = saved[i]
        g_in, g_q, g_kv = norm_in[i][None], norm_q_lat[i][None], norm_kv_lat[i][None]
        g_a, g_b = norm_out_a[i][None], norm_out_b[i][None]
        if tok is not None:
            g_a = _after(g_a, tok)
        if fuse_head and i == depth - 1:
            dya, dyb, dga, dgb, g_w_out, g_na[i], g_nb[i], dh, loss_acc, dg_final = _out_bwd(
                None, ya, yb, proj, g_a, g_b, w_out_f[i], tb, f"out_bwd_{i}", head=(h, norm_final[None], loss_target, S))
        else:
            dya, dyb, dga, dgb, g_w_out, g_na[i], g_nb[i] = _out_bwd(dh, ya, yb, proj, g_a, g_b, w_out_f[i], tb, f"out_bwd_{i}")
        sc_out[i], tok = _xchg_start([g_w_out.reshape(NDEV, D // NDEV, D).astype(BF16)], ["scatter"], f"scatter_start_{i}a")
        dq, dk, dv = _mla_bwd(q, k, v, yb, dyb, lse_b, tok, Bl, Lp, f"mla_bwd_{i}")
        dqa, dka, dva, dbias = _win_bwd(proj, biases[i], ya, dya, lse_a, Bl, S, f"win_bwd_{i}")
        dcq, dckv, dkr, dwq, dwkk, dwkv, g_nq[i], g_nkv[i] = _mla_prep_bwd(
            dq, dk, dv, proj, g_q, g_kv, wq, wkk, wkv, tq_tab, tk_tab, tm, f"mla_prep_bwd_{i}")
        pieces = (dqa, dka, dva, dga, dcq, dckv, dkr, dgb)
        dbias_all[i] = dbias
        rest = [split(_unpack_w_uq_grad(dwq), 768 // NDEV), split(_unpack_w_ukv_grad(dwkk, dwkv), 1024 // NDEV)]
        dh, dwx, g_n_in[i] = _inproj_bwd(pieces, h_in, g_in, wx, dh, tb, f"inproj_bwd_{i}")
        sc_rest[i], tok = _xchg_start([_unpack_w_in_grad(dwx).astype(BF16)] + rest, ["scatter"] * 3, f"scatter_start_{i}b")

    dh3 = dh.reshape(Bl, Lp, D)
    grad_x = dh3[:, :S]
    g_meta = jnp.transpose(jnp.sum(dh3[:, S:S + NMETA], axis=0).reshape(NMETA, NDEV, D // NDEV), (1, 0, 2))
    dtab, dsink = _table_grad(dbias_all, buckets, tok, "table_grad")
    two_d = lambda a: a.reshape(1, -1) if a.ndim == 1 else a
    small_w = [rel_bias_table, norm_in, sink_a, norm_q_lat, norm_kv_lat, norm_out_a, norm_out_b, norm_final]
    small_m = [m_rel_bias_table, m_norm_in, m_sink_a, m_norm_q_lat, m_norm_kv_lat, m_norm_out_a, m_norm_out_b, m_norm_final]
    small_v = [v_rel_bias_table, v_norm_in, v_sink_a, v_norm_q_lat, v_norm_kv_lat, v_norm_out_a, v_norm_out_b, v_norm_final]
    small_g = [jnp.transpose(dtab[:, :N_BUCKETS]), jnp.concatenate(g_n_in), dsink[:, :, 0], jnp.concatenate(g_nq),
               jnp.concatenate(g_nkv), jnp.concatenate(g_na), jnp.concatenate(g_nb), dg_final]
    sc_small, tok = _xchg_start(small_g + [loss_acc, g_meta], ["gather"] * (len(small_g) + 1) + ["scatter"],
                                "scatter_start_small")

    res = {}

    def update(n, r, w, m, v):
        outs = _adamw_param(r, w, m, v, f"adamw_{n}")
        res[n] = list(outs)
        return outs[0]

    r_out = [finish(sc_out[i], tok, f"scatter_wait_{i}a")[0] for i in range(depth)]
    done = update("w_out", r_out, w_out, m_w_out, v_w_out)
    r_in, r_uq, r_ukv = [None] * depth, [None] * depth, [None] * depth
    for i in reversed(range(depth)):
        r_in[i], r_uq[i], r_ukv[i] = finish(sc_rest[i], [done, grad_x], f"scatter_wait_{i}b")
    r_small = finish(sc_small, r_in[0], "scatter_wait_small")
    update("w_in", r_in, w_in, m_w_in, v_w_in)
    update("w_uq", r_uq, w_uq, m_w_uq, v_w_uq)
    update("w_ukv", r_ukv, w_ukv, m_w_ukv, v_w_ukv)
    update("meta_tokens", [r_small[-1]], meta_tokens[None], m_meta_tokens[None], v_meta_tokens[None])
    res["meta_tokens"] = [o[0] for o in res["meta_tokens"]]
    names_small = ["rel_bias_table", "norm_in", "sink_a", "norm_q_lat", "norm_kv_lat", "norm_out_a", "norm_out_b", "norm_final"]
    outs = _adamw_small(r_small[:-2], [two_d(a) for a in small_w], [two_d(a) for a in small_m], [two_d(a) for a in small_v],
                        r_small[-2], "adamw_replicated")
    loss = outs[-1][0, 0]
    ns = len(names_small)
    for j, n in enumerate(names_small):
        res[n] = [outs[kk * ns + j].reshape(small_w[j].shape) for kk in range(4)]
    order = ["meta_tokens", "rel_bias_table", "norm_in", "w_in", "sink_a", "norm_q_lat", "w_uq", "norm_kv_lat", "w_ukv",
             "norm_out_a", "norm_out_b", "w_out", "norm_final"]
    return (loss, grad_x, *[res[n][kk] for kk in range(4) for n in order])
```

```python
import math

import numpy as np
import jax
import jax.numpy as jnp
from jax import lax
from jax.experimental import pallas as pl
from jax.experimental.pallas import tpu as pltpu

F32, BF16 = jnp.float32, jnp.bfloat16
D = 1024
NMETA = 16
BLK = 128
A_HEADS, A_KV, A_DH, A_W = 8, 2, 64, 512
B_HEADS, NOPE, ROPE, B_V, B_W = 8, 64, 32, 64, 512
Q_RANK, KV_RANK = 256, 128
IN_W = 2208
N_BUCKETS, MAX_DIST = 32, 128
THETA = 10000.0
EPS = 1e-6
NEG = -1e30
A_SCALE = A_DH ** -0.5
B_SCALE = (NOPE + ROPE) ** -0.5
LOG2E = math.log2(math.e)
MASK_LANE = NOPE + ROPE
SUM_LANE = B_V
LANES = 128
NDEV = 8
MESH = pl.DeviceIdType.MESH

NX = 2560
QA0, KA0, VA0, GA0, CQ0, CKV0, KR0, GB0 = 0, 512, 768, 1024, 1536, 1792, 1920, 2048

ADAM_LR, ADAM_B1, ADAM_B2, ADAM_EPS, ADAM_WD, ADAM_STEP = 0.001, 0.9, 0.999, 1e-08, 0.01, 10


def _dot(a, b):
    return jnp.dot(a, b, preferred_element_type=F32)


def _dot_nt(a, b):
    return lax.dot_general(a, b, (((1,), (1,)), ((), ())), preferred_element_type=F32)


def _dot_tn(a, b):
    return lax.dot_general(a, b, (((0,), (0,)), ((), ())), preferred_element_type=F32)


def _lane(shape):
    return lax.broadcasted_iota(jnp.int32, shape, len(shape) - 1)


def _rstd(x):
    return lax.rsqrt(jnp.mean(x * x, axis=-1, keepdims=True) + EPS)


def _rms_bwd(dn, x, r, g):
    z = dn * g
    dx = r * z - x * (r * r * r) * jnp.mean(z * x, axis=-1, keepdims=True)
    return dx, jnp.sum(dn * (x * r), axis=0, keepdims=True)


def _row(i):
    return lambda *ids: (ids[0], i)


def _const2(*ids):
    return (0, 0)


def _params(*sem):
    return pltpu.CompilerParams(dimension_semantics=sem)


SHARD_W = IN_W // NDEV


def _pack_w_in(shards):
    def nat(a, b):
        parts = []
        while a < b:
            p, e = a // SHARD_W, min(b, (a // SHARD_W + 1) * SHARD_W)
            parts.append(shards[p][:, a - p * SHARD_W:e - p * SHARD_W])
            a = e
        return parts
    dup = lambda a: nat(a, a + 64) * 2 + nat(a + 64, a + 128) * 2
    zeros = [jnp.zeros((shards.shape[1], LANES - ROPE), shards.dtype)]
    return jnp.concatenate(nat(0, 512) + dup(512) + dup(640) + nat(768, 1280) + nat(1280, 1536) + nat(1536, 1664)
                           + nat(1664, 1696) + zeros + nat(1696, 2208), axis=1)


_NAT_GROUPS = ((0, 512, QA0), (512, 576, KA0), (576, 640, KA0 + 128), (640, 704, VA0), (704, 768, VA0 + 128),
               (768, 1280, GA0), (1280, 1536, CQ0), (1536, 1664, CKV0), (1664, 1696, KR0), (1696, 2208, GB0))


def _unpack_w_in_grad(g):
    def nat(a, b):
        lo, _, pk = next(grp for grp in _NAT_GROUPS if grp[0] <= a < grp[1])
        t = g[:, pk + a - lo:pk + b - lo]
        return t + g[:, pk + 64 + a - lo:pk + 64 + b - lo] if 512 <= a < 768 else t
    cuts = sorted({c for grp in _NAT_GROUPS for c in grp[:2]} | {p * SHARD_W for p in range(NDEV + 1)})
    shards = [[] for _ in range(NDEV)]
    for a, b in zip(cuts[:-1], cuts[1:]):
        shards[a // SHARD_W].append(nat(a, b))
    return jnp.stack([jnp.concatenate(parts, axis=1) for parts in shards])


def _pack_w_uq(w):
    t = w.reshape(Q_RANK, B_HEADS, NOPE + ROPE)
    t = jnp.concatenate([t, jnp.zeros((Q_RANK, B_HEADS, LANES - NOPE - ROPE), w.dtype)], axis=-1)
    return t.reshape(Q_RANK, B_HEADS * LANES)


def _unpack_w_uq_grad(g):
    return g.reshape(Q_RANK, B_HEADS, LANES)[:, :, :NOPE + ROPE].reshape(Q_RANK, B_HEADS * (NOPE + ROPE))


def _pack_w_ukv(w):
    t = w.reshape(KV_RANK, B_HEADS, NOPE + B_V)
    z = jnp.zeros((KV_RANK, B_HEADS, LANES - NOPE), w.dtype)
    pad = lambda u: jnp.concatenate([u, z], axis=-1).reshape(KV_RANK, B_HEADS * LANES)
    return pad(t[:, :, :NOPE]), pad(t[:, :, NOPE:])


def _unpack_w_ukv_grad(gk, gv):
    head = lambda g: g.reshape(KV_RANK, B_HEADS, LANES)[:, :, :NOPE]
    return jnp.concatenate([head(gk), head(gv)], axis=-1).reshape(KV_RANK, B_HEADS * (NOPE + B_V))


def _t5_bucket(rel):
    nb = N_BUCKETS // 2
    max_exact = nb // 2
    ret = jnp.where(rel > 0, nb, 0)
    n = jnp.abs(rel)
    nf = jnp.maximum(n, 1).astype(F32)
    large = max_exact + (jnp.log(nf / max_exact) / math.log(MAX_DIST / max_exact) * (nb - max_exact)).astype(jnp.int32)
    large = jnp.minimum(large, nb - 1)
    return ret + jnp.where(n < max_exact, n, large)


def _bias_buckets(S):
    q = np.arange(BLK)[:, None]
    k = np.arange(4 * BLK)[None, :]
    is_meta_key = (k >= 3 * BLK) & (k < 3 * BLK + NMETA)
    mi = k - 3 * BLK
    rels, valids = [], []
    for shift in (0, BLK, 2 * BLK):
        rel_real = k - shift - q
        valid_real = (k < 3 * BLK) & (np.abs(rel_real) <= BLK)
        far = -(NMETA + MAX_DIST + BLK)
        rel_meta = (mi - (NMETA + q)) if shift == 0 else np.full_like(k + q, far)
        rels.append(np.where(is_meta_key, rel_meta, rel_real))
        valids.append(valid_real | is_meta_key)
    qm = q < NMETA
    rel_real = NMETA + k - q
    valid_real = (k < BLK) & (np.abs(rel_real) <= BLK) & qm
    rels.append(np.where(is_meta_key, mi - q, rel_real))
    valids.append(valid_real | is_meta_key)
    rel = jnp.asarray(np.stack(rels).astype(np.int32))
    valid = jnp.asarray(np.stack(valids))
    return jnp.where(valid, _t5_bucket(rel), -1).astype(jnp.int32)


def _rope_tables(S, Lp, off, passthrough):
    half = ROPE // 2
    r = np.arange(Lp)
    pos = np.where(r < S, NMETA + r, np.where(r < S + NMETA, r - S, 0)).astype(np.float32)
    freqs = THETA ** (-jnp.arange(half, dtype=F32) / half)
    ang = jnp.asarray(pos)[:, None] * freqs[None, :]
    cos, sin = jnp.cos(ang), jnp.sin(ang)
    z = lambda n: jnp.zeros((Lp, n), F32)
    head = jnp.ones((Lp, off), F32) if passthrough else z(off)
    c = jnp.concatenate([head, cos, cos, z(LANES - off - ROPE)], axis=1)
    s1 = jnp.concatenate([z(off), -sin, z(LANES - off - half)], axis=1)
    s2 = jnp.concatenate([z(off + half), sin, z(LANES - off - ROPE)], axis=1)
    return jnp.stack([c, s1, s2])


def _rope(x, t):
    return x * t[0] + pltpu.roll(x, LANES - 16, 1) * t[1] + pltpu.roll(x, 16, 1) * t[2]


def _rope_t(dy, t):
    return dy * t[0] + pltpu.roll(dy * t[1], 16, 1) + pltpu.roll(dy * t[2], LANES - 16, 1)


def _me_and_peers():
    x, y, c = lax.axis_index("x"), lax.axis_index("y"), lax.axis_index("c")
    flip = lambda v, b: 1 - v if b else v
    peers = [(flip(x, k & 4), flip(y, k & 2), flip(c, k & 1)) for k in range(1, NDEV)]
    return 4 * x + 2 * y + c, peers


def _xchg_start(srcs, kinds, name, after=None):
    n = len(srcs)
    lands = [lax.empty((NDEV,) + s.shape[-2:], s.dtype) for s in srcs]
    extra = [] if after is None else [after]

    def body(*refs):
        src_refs, land_refs = refs[:n], refs[n:2 * n]
        ssems, rsems = refs[2 * n + len(extra):3 * n + len(extra)], refs[3 * n + len(extra):4 * n + len(extra)]
        token = refs[6 * n + len(extra)]
        me, peers = _me_and_peers()
        for it in range(n):
            for k, (px, py, pc) in enumerate(peers):
                src = src_refs[it] if kinds[it] == "gather" else src_refs[it].at[4 * px + 2 * py + pc]
                pltpu.make_async_remote_copy(src_ref=src, dst_ref=land_refs[it].at[me], send_sem=ssems[it].at[k],
                                             recv_sem=rsems[it].at[k], device_id=(px, py, pc), device_id_type=MESH).start()
        token[...] = jnp.zeros_like(token)

    hbm = pl.BlockSpec(memory_space=pltpu.HBM)
    sem = pl.BlockSpec(memory_space=pltpu.SEMAPHORE)
    outs = pl.pallas_call(
        body, name=name,
        out_shape=tuple([pltpu.SemaphoreType.DMA((NDEV - 1,))] * (2 * n) + [pltpu.HBM(a.shape, a.dtype) for a in srcs + lands]
                        + [jax.ShapeDtypeStruct((8, LANES), F32)]),
        in_specs=[hbm] * (2 * n) + [pl.BlockSpec(memory_space=pl.ANY)] * len(extra),
        out_specs=tuple([sem] * (2 * n) + [hbm] * (2 * n) + [pl.BlockSpec(memory_space=pltpu.VMEM)]),
        input_output_aliases={i: 2 * n + i for i in range(2 * n)},
        compiler_params=pltpu.CompilerParams(has_side_effects=pltpu.SideEffectType.DATAFLOW_SIDE_EFFECTING),
    )(*[pltpu.with_memory_space_constraint(a, pltpu.HBM) for a in srcs + lands], *extra)
    handles = [(outs[it], outs[n + it], outs[2 * n + it], outs[3 * n + it], kinds[it]) for it in range(n)]
    return handles, outs[4 * n]


def _xchg_wait(handles, after, name):
    n = len(handles)
    after = list(after) if isinstance(after, (list, tuple)) else [after]

    def body(*refs):
        src_refs, land_refs = refs[:n], refs[n:2 * n]
        ssems, rsems = refs[2 * n:3 * n], refs[3 * n:4 * n]
        me, peers = _me_and_peers()
        for it in range(n):
            for k, (px, py, pc) in enumerate(peers):
                src = src_refs[it] if handles[it][4] == "gather" else src_refs[it].at[4 * px + 2 * py + pc]
                cp = pltpu.make_async_remote_copy(src_ref=src, dst_ref=land_refs[it].at[me], send_sem=ssems[it].at[k],
                                                  recv_sem=rsems[it].at[k], device_id=(px, py, pc), device_id_type=MESH)
                cp.wait_send()
                cp.wait_recv()

    hbm = pl.BlockSpec(memory_space=pltpu.HBM)
    sem = pl.BlockSpec(memory_space=pltpu.SEMAPHORE)
    srcs, lands = [h[2] for h in handles], [h[3] for h in handles]
    outs = pl.pallas_call(
        body, name=name, out_shape=tuple(pltpu.HBM(a.shape, a.dtype) for a in srcs + lands),
        in_specs=[hbm] * (2 * n) + [sem] * (2 * n) + [pl.BlockSpec(memory_space=pl.ANY)] * len(after),
        out_specs=tuple([hbm] * (2 * n)),
        input_output_aliases={i: i for i in range(2 * n)},
        compiler_params=pltpu.CompilerParams(has_side_effects=pltpu.SideEffectType.DATAFLOW_SIDE_EFFECTING),
    )(*srcs, *lands, *[h[0] for h in handles], *[h[1] for h in handles], *after)
    return list(outs[:n]), list(outs[n:])


def _own_slot(land, own, me):
    return lax.dynamic_update_slice(land, own[None].astype(land.dtype), (me, 0, 0))


def _after(x, token):
    return x + token[0, 0]


def _adamw_update(g, w, m, v):
    mn = ADAM_B1 * m + (1.0 - ADAM_B1) * g
    vn = ADAM_B2 * v + (1.0 - ADAM_B2) * (g * g)
    m_hat = mn / (1.0 - ADAM_B1 ** ADAM_STEP)
    v_hat = vn / (1.0 - ADAM_B2 ** ADAM_STEP)
    return -ADAM_LR * (m_hat / (jnp.sqrt(v_hat) + ADAM_EPS) + ADAM_WD * w), mn, vn


def _sum_partials(r_ref):
    g = r_ref[0].astype(F32)
    for p in range(1, NDEV):
        g = g + r_ref[p].astype(F32)
    return g


def _adamw_param(recvs, w, m, v, name):
    depth, r, c = w.shape
    tr = min(r, 256)
    nl = len(recvs)

    def body(*refs):
        r_refs = refs[:nl]
        w_ref, m_ref, v_ref, g_out, d_out, m_out, v_out = refs[nl:]
        for li in range(nl):
            @pl.when(pl.program_id(0) == li)
            def _(li=li):
                g = _sum_partials(r_refs[li])
                g_out[0] = g
                d_out[0], m_out[0], v_out[0] = _adamw_update(g, w_ref[0], m_ref[0], v_ref[0])

    blk = pl.BlockSpec((1, tr, c), lambda l, i: (l, i, 0))
    return pl.pallas_call(
        body, name=name, grid=(depth, r // tr),
        in_specs=[pl.BlockSpec((NDEV, tr, c), lambda l, i: (0, i, 0))] * nl + [blk, blk, blk],
        out_specs=(blk, blk, blk, blk), out_shape=(jax.ShapeDtypeStruct(w.shape, F32),) * 4,
        compiler_params=_params("arbitrary", "arbitrary"),
    )(*recvs, w, m, v)


def _adamw_small(recvs, ws, ms, vs, loss_parts, name):
    n = len(ws)

    def body(*refs):
        r_refs, w_refs, m_refs, v_refs = refs[:n], refs[n:2 * n], refs[2 * n:3 * n], refs[3 * n:4 * n]
        outs = refs[4 * n + 1:]
        for j in range(n):
            g = _sum_partials(r_refs[j])
            outs[j][...] = g
            outs[n + j][...], outs[2 * n + j][...], outs[3 * n + j][...] = _adamw_update(
                g, w_refs[j][...], m_refs[j][...], v_refs[j][...])
        outs[4 * n][...] = _sum_partials(refs[4 * n])

    vm = pl.BlockSpec(memory_space=pltpu.VMEM)
    return pl.pallas_call(
        body, name=name, in_specs=[vm] * (4 * n + 1), out_specs=tuple([vm] * (4 * n + 1)),
        out_shape=tuple(jax.ShapeDtypeStruct(a.shape, F32) for a in ws) * 4 + (jax.ShapeDtypeStruct(loss_parts.shape[1:], F32),),
    )(*recvs, *ws, *ms, *vs, loss_parts)


def _inproj_fwd(h, g, wx, tm, name):
    T = h.shape[0]

    def body(h_ref, g_ref, w_ref, o_ref):
        x = h_ref[...]
        u = (x * _rstd(x) * g_ref[...]).astype(BF16)
        o_ref[...] = _dot(u, w_ref[...])

    return pl.pallas_call(
        body, name=name, grid=(T // tm,),
        in_specs=[pl.BlockSpec((tm, D), _row(0)), pl.BlockSpec((1, D), _const2), pl.BlockSpec((D, NX), _const2)],
        out_specs=pl.BlockSpec((tm, NX), _row(0)), out_shape=jax.ShapeDtypeStruct((T, NX), F32),
        compiler_params=_params("parallel"),
    )(h, g, wx)


def _mla_prep_fwd(proj, gq, gkv, wq, wkk, wkv, tq, tk, tm, name):
    T = proj.shape[0]

    def body(cq_ref, ckv_ref, kr_ref, gq_ref, gkv_ref, wq_ref, wkk_ref, wkv_ref, tq_ref, tk_ref, q_ref, k_ref, v_ref):
        cq = cq_ref[...]
        q = _dot((cq * _rstd(cq) * gq_ref[...]).astype(BF16), wq_ref[...])
        tqv = tq_ref[...]
        lane = _lane((1, LANES))
        one = jnp.where(lane == MASK_LANE, 1.0, 0.0)
        for h in range(B_HEADS):
            cs = slice(LANES * h, LANES * (h + 1))
            q_ref[:, cs] = (_rope(q[:, cs], tqv) * (B_SCALE * LOG2E) + one).astype(BF16)
        tkv = tk_ref[...]
        ksh = pltpu.roll(_rope(kr_ref[...], tkv), NOPE, 1) + tkv[3]
        ckv = ckv_ref[...]
        cb = (ckv * _rstd(ckv) * gkv_ref[...]).astype(BF16)
        kn = _dot(cb, wkk_ref[...])
        vn = _dot(cb, wkv_ref[...])
        minus = jnp.where((lane == SUM_LANE) | (lane == SUM_LANE + 1), -1.0, 0.0)
        for h in range(B_HEADS):
            cs = slice(LANES * h, LANES * (h + 1))
            k_ref[:, cs] = (kn[:, cs] + ksh).astype(BF16)
            v_ref[:, cs] = (vn[:, cs] + minus).astype(BF16)

    tab = lambda n: pl.BlockSpec((n, tm, LANES), lambda i: (0, i, 0))
    wide = pl.BlockSpec((tm, 1024), _row(0))
    return pl.pallas_call(
        body, name=name, grid=(T // tm,),
        in_specs=[pl.BlockSpec((tm, 256), _row(CQ0 // 256)), pl.BlockSpec((tm, 128), _row(CKV0 // 128)),
                  pl.BlockSpec((tm, 128), _row(KR0 // 128)), pl.BlockSpec((1, Q_RANK), _const2),
                  pl.BlockSpec((1, KV_RANK), _const2), pl.BlockSpec((Q_RANK, 1024), _const2),
                  pl.BlockSpec((KV_RANK, 1024), _const2), pl.BlockSpec((KV_RANK, 1024), _const2), tab(3), tab(4)],
        out_specs=(wide, wide, wide), out_shape=(jax.ShapeDtypeStruct((T, 1024), BF16),) * 3,
        compiler_params=_params("parallel"),
    )(proj, proj, proj, gq, gkv, wq, wkk, wkv, tq, tk)


SINK_SLOT = 4 * BLK - 1


def _build_bias(buckets, table, sinks, name):
    depth = sinks.shape[0]

    def body(tab_ref, sink_ref, b_ref, *o_refs):
        bidx = b_ref[0]
        slot = _lane(bidx.shape)
        for h in range(A_HEADS):
            acc = jnp.full(bidx.shape, NEG, F32)
            for j in range(N_BUCKETS):
                acc = jnp.where(bidx == j, tab_ref[j, h] * LOG2E, acc)
            for l in range(depth):
                o_refs[l][0, h] = jnp.where(slot == SINK_SLOT, sink_ref[l, h] * LOG2E, acc)

    smem = pl.BlockSpec(memory_space=pltpu.SMEM)
    blk = pl.BlockSpec((1, A_HEADS, BLK, 4 * BLK), lambda i: (i, 0, 0, 0))
    return pl.pallas_call(
        body, name=name, grid=(4,),
        in_specs=[smem, smem, pl.BlockSpec((1, BLK, 4 * BLK), lambda i: (i, 0, 0))],
        out_specs=tuple([blk] * depth), out_shape=tuple([jax.ShapeDtypeStruct((4, A_HEADS, BLK, 4 * BLK), F32)] * depth),
        compiler_params=_params("parallel"),
    )(table, sinks, buckets)


def _win_base(n, NB):
    return jnp.where(n == NB, 0, jnp.clip(n - 1, 0, NB - 3))


def _win_variant(n, NB):
    return jnp.where(n == 0, 0, jnp.where(n < NB - 1, 1, jnp.where(n == NB - 1, 2, 3)))


def _win_kv(k_ref, v_ref, base, S, kvh):
    cs = slice(LANES * kvh, LANES * (kvh + 1))
    k2 = jnp.concatenate([k_ref[pl.ds(base, 3 * BLK), cs], k_ref[S:S + BLK, cs]], axis=0).astype(BF16)
    v2 = jnp.concatenate([v_ref[pl.ds(base, 3 * BLK), cs], v_ref[S:S + BLK, cs]], axis=0)
    lane = _lane(v2.shape)
    v2 = jnp.where(lane < A_DH, v2, jnp.where(lane < A_DH + 2, -1.0, 0.0)).astype(BF16)
    return k2, v2


def _win_fwd(proj, bias, Bl, S, name):
    T = proj.shape[0]
    Lp = S + BLK
    NB = S // BLK
    nblk = Lp // BLK
    G = A_HEADS // A_KV

    def body(q_ref, k_ref, v_ref, b_ref, o_ref, lse_ref):
        n = pl.program_id(1)
        base = pl.multiple_of(_win_base(n, NB) * BLK, BLK)
        lane = _lane((BLK, LANES))
        lo = lane < A_DH
        kv = [_win_kv(k_ref, v_ref, base, S, kvh) for kvh in range(A_KV)]
        scores = []
        for h in range(A_HEADS):
            qp = q_ref[:, LANES * (h // 2):LANES * (h // 2 + 1)]
            qm = (jnp.where(lo if h % 2 == 0 else ~lo, qp, 0.0) * (A_SCALE * LOG2E)).astype(BF16)
            scores.append(_dot_nt(qm, kv[h // G][0]) + b_ref[0, h])
        ms = [jnp.max(s, axis=-1, keepdims=True) for s in scores]
        es = [jnp.exp2(s - m).astype(BF16) for s, m in zip(scores, ms)]
        outs = []
        t = jnp.zeros((BLK, LANES), F32)
        for h in range(A_HEADS):
            o2 = _dot(es[h], kv[h // G][1])
            l = -jnp.sum(jnp.where(lane == A_DH, o2, 0.0), axis=-1, keepdims=True)
            outs.append(o2 * (1.0 / l))
            t = jnp.where(lane == h, ms[h] + jnp.log(l) * LOG2E, t)
        for j in range(A_HEADS // 2):
            o_ref[:, LANES * j:LANES * (j + 1)] = jnp.where(lo, outs[2 * j], pltpu.roll(outs[2 * j + 1], A_DH, 1))
        lse_ref[...] = t

    qrow = lambda b, n: (b * nblk + n, 0)
    return pl.pallas_call(
        body, name=name, grid=(Bl, nblk),
        in_specs=[pl.BlockSpec((BLK, A_W), qrow),
                  pl.BlockSpec((Lp, 256), lambda b, n: (b, KA0 // 256)), pl.BlockSpec((Lp, 256), lambda b, n: (b, VA0 // 256)),
                  pl.BlockSpec((1, A_HEADS, BLK, 4 * BLK), lambda b, n: (_win_variant(n, NB), 0, 0, 0))],
        out_specs=(pl.BlockSpec((BLK, A_W), qrow), pl.BlockSpec((BLK, LANES), qrow)),
        out_shape=(jax.ShapeDtypeStruct((T, A_W), F32), jax.ShapeDtypeStruct((T, LANES), F32)),
        compiler_params=_params("parallel", "arbitrary"),
    )(proj, proj, proj, bias)


def _q_tiles(Lp):
    S = Lp - BLK
    return S // 256, S


def _mla_fwd(q, k, v, Bl, Lp, name):
    T = q.shape[0]
    n_big, S = _q_tiles(Lp)

    def body(q_ref, k_ref, v_ref, o_ref, lse_ref):
        def qtiles(starts, nq):
            lane = _lane((nq, LANES))
            cols = [slice(LANES * hh, LANES * (hh + 1)) for hh in range(2)]
            scores = [[_dot_nt(q_ref[pl.ds(r0, nq), cs], k_ref[:, cs]) for cs in cols] for r0 in starts]
            for r0, tile_scores in zip(starts, scores):
                outs = []
                t = jnp.zeros((nq, LANES), F32)
                for hh in range(2):
                    cs, s = cols[hh], tile_scores[hh]
                    m = jnp.max(s, axis=-1, keepdims=True)
                    o2 = _dot(jnp.exp2(s - m).astype(BF16), v_ref[:, cs])
                    l = -jnp.sum(jnp.where(lane == SUM_LANE, o2, 0.0), axis=-1, keepdims=True)
                    outs.append(o2 * (1.0 / l))
                    t = jnp.where(lane == hh, m + jnp.log(l) * LOG2E, t)
                o_ref[pl.ds(r0, nq), :] = jnp.where(lane < B_V, outs[0], pltpu.roll(outs[1], B_V, 1))
                lse_ref[pl.ds(r0, nq), :] = t

        def step(i, c):
            qtiles([pl.multiple_of(i * 512, 256), pl.multiple_of(i * 512 + 256, 256)], 256)
            return c

        lax.fori_loop(0, n_big // 2, step, 0)
        if n_big % 2:
            qtiles([(n_big - 1) * 256], 256)
        qtiles([S], BLK)

    pair = lambda b, j: (b, j)
    wide, narrow = pl.BlockSpec((Lp, 256), pair), pl.BlockSpec((Lp, LANES), pair)
    return pl.pallas_call(
        body, name=name, grid=(Bl, B_HEADS // 2), in_specs=[wide, wide, wide], out_specs=(narrow, narrow),
        out_shape=(jax.ShapeDtypeStruct((T, B_W), F32), jax.ShapeDtypeStruct((T, B_W), F32)),
        compiler_params=_params("parallel", "parallel"),
    )(q, k, v)


def _gated(y, gate, gain):
    r = _rstd(y)
    nrm = y * r
    sg = jax.nn.sigmoid(gate)
    return r, nrm, sg, nrm * gain, gate * sg


def _out_fwd(ya, yb, proj, na, nb, wout, h, tm, name):
    T = h.shape[0]

    def body(ya_ref, yb_ref, ga_ref, gb_ref, na_ref, nb_ref, w_ref, h_ref, o_ref):
        _, _, _, n_a, sl_a = _gated(ya_ref[...], ga_ref[...], na_ref[...])
        _, _, _, n_b, sl_b = _gated(yb_ref[...], gb_ref[...], nb_ref[...])
        acc = _dot((n_a * sl_a).astype(BF16), w_ref[0:A_W, :]) + _dot((n_b * sl_b).astype(BF16), w_ref[A_W:A_W + B_W, :])
        o_ref[...] = h_ref[...] + acc

    half = pl.BlockSpec((tm, 512), _row(0))
    return pl.pallas_call(
        body, name=name, grid=(T // tm,),
        in_specs=[half, half, pl.BlockSpec((tm, 512), _row(GA0 // 512)), pl.BlockSpec((tm, 512), _row(GB0 // 512)),
                  pl.BlockSpec((1, 512), _const2), pl.BlockSpec((1, 512), _const2), pl.BlockSpec((D, D), _const2),
                  pl.BlockSpec((tm, D), _row(0))],
        out_specs=pl.BlockSpec((tm, D), _row(0)), out_shape=jax.ShapeDtypeStruct((T, D), F32),
        compiler_params=_params("parallel"),
    )(ya, yb, proj, proj, na, nb, wout, h)


def _loss_head(h, gf, target, Bl, S, name):
    T = h.shape[0]
    nblk = (S + BLK) // BLK
    NB = S // BLK

    def body(h_ref, g_ref, t_ref, dh_ref, loss_ref, dg_ref):
        b, n = pl.program_id(0), pl.program_id(1)

        @pl.when((b == 0) & (n == 0))
        def _():
            loss_ref[...] = jnp.zeros_like(loss_ref)
            dg_ref[...] = jnp.zeros_like(dg_ref)

        @pl.when(n < NB)
        def _():
            x = h_ref[...]
            g = g_ref[...]
            r = _rstd(x)
            err = x * r * g - t_ref[0]
            loss_ref[...] += 0.5 * jnp.sum(jnp.mean(err * err, axis=-1, keepdims=True))
            dx, dg = _rms_bwd(err * (1.0 / D), x, r, g)
            dh_ref[...] = dx
            dg_ref[...] += dg

        @pl.when(n >= NB)
        def _():
            dh_ref[...] = jnp.zeros_like(dh_ref)

    return pl.pallas_call(
        body, name=name, grid=(Bl, nblk),
        in_specs=[pl.BlockSpec((BLK, D), lambda b, n: (b * nblk + n, 0)), pl.BlockSpec((1, D), lambda b, n: (0, 0)),
                  pl.BlockSpec((1, BLK, D), lambda b, n: (b, jnp.minimum(n, NB - 1), 0))],
        out_specs=(pl.BlockSpec((BLK, D), lambda b, n: (b * nblk + n, 0)), pl.BlockSpec((8, LANES), lambda b, n: (0, 0)),
                   pl.BlockSpec((1, D), lambda b, n: (0, 0))),
        out_shape=(jax.ShapeDtypeStruct((T, D), F32), jax.ShapeDtypeStruct((8, LANES), F32), jax.ShapeDtypeStruct((1, D), F32)),
        compiler_params=_params("arbitrary", "arbitrary"),
    )(h, gf, target)


def _out_bwd(dh, ya, yb, proj, na, nb, wout, tm, name, head=None):
    T = ya.shape[0]
    fused = head is not None
    if fused:
        hf, gf, target, S = head
        assert tm == 2 * BLK
        nblk, NB = (S + BLK) // BLK, S // BLK

    def body(*refs):
        if fused:
            (hf_ref, gf_ref, t0_ref, t1_ref, ya_ref, yb_ref, ga_ref, gb_ref, na_ref, nb_ref, w_ref,
             dya_ref, dyb_ref, dga_ref, dgb_ref, dw_ref, dna_ref, dnb_ref, dh_out, loss_ref, dgf_ref) = refs
        else:
            (dh_ref, ya_ref, yb_ref, ga_ref, gb_ref, na_ref, nb_ref, w_ref,
             dya_ref, dyb_ref, dga_ref, dgb_ref, dw_ref, dna_ref, dnb_ref) = refs
        i = pl.program_id(0)

        @pl.when(i == 0)
        def _():
            dw_ref[...] = jnp.zeros_like(dw_ref)
            dna_ref[...] = jnp.zeros_like(dna_ref)
            dnb_ref[...] = jnp.zeros_like(dnb_ref)
            if fused:
                loss_ref[...] = jnp.zeros_like(loss_ref)
                dgf_ref[...] = jnp.zeros_like(dgf_ref)

        if fused:
            x, g = hf_ref[...], gf_ref[...]
            r = _rstd(x)
            real = [(((2 * i + half) % nblk) < NB).astype(F32) for half in range(2)]
            row = lax.broadcasted_iota(jnp.int32, (tm, 1), 0)
            keep = jnp.where(row < BLK, real[0], real[1])
            err = (x * r * g - jnp.concatenate([t0_ref[0], t1_ref[0]], axis=0)) * keep
            loss_ref[...] += 0.5 * jnp.sum(jnp.mean(err * err, axis=-1, keepdims=True))
            dh, dgf = _rms_bwd(err * (1.0 / D), x, r, g)
            dh_out[...] = dh
            dgf_ref[...] += dgf
        else:
            dh = dh_ref[...]
        dhb = dh.astype(BF16)
        dy = _dot_nt(dhb, w_ref[...])
        parts = ((ya_ref, ga_ref, na_ref, dya_ref, dga_ref, dna_ref, 0), (yb_ref, gb_ref, nb_ref, dyb_ref, dgb_ref, dnb_ref, A_W))
        for y_ref, gate_ref, gain_ref, dy_out, dgate_out, dgain_out, c0 in parts:
            y, gate, gain = y_ref[...], gate_ref[...], gain_ref[...]
            r, nrm, sg, n_g, sl = _gated(y, gate, gain)
            dyp = dy[:, c0:c0 + 512]
            dgate_out[...] = dyp * n_g * (sg * (1.0 + gate * (1.0 - sg)))
            dx, dgain = _rms_bwd(dyp * sl, y, r, gain)
            dy_out[...] = dx
            dgain_out[...] += dgain
            dw_ref[c0:c0 + 512, :] += _dot_tn((n_g * sl).astype(BF16), dhb)

    half = pl.BlockSpec((tm, 512), _row(0))
    vec = pl.BlockSpec((1, 512), _const2)
    rows = pl.BlockSpec((tm, D), _row(0))
    in_specs = [half, half, pl.BlockSpec((tm, 512), _row(GA0 // 512)), pl.BlockSpec((tm, 512), _row(GB0 // 512)), vec, vec,
                pl.BlockSpec((D, D), _const2)]
    out_specs = [half, half, half, half, pl.BlockSpec((D, D), _const2), vec, vec]
    out_shape = [jax.ShapeDtypeStruct((T, 512), F32)] * 4 + [jax.ShapeDtypeStruct((D, D), F32)] + [jax.ShapeDtypeStruct((1, 512), F32)] * 2
    if fused:
        def tgt(half_):
            def index(i):
                blk = 2 * i + half_
                return (blk // nblk, jnp.minimum(blk % nblk, NB - 1), 0)
            return pl.BlockSpec((1, BLK, D), index)
        in_specs = [rows, pl.BlockSpec((1, D), _const2), tgt(0), tgt(1)] + in_specs
        args = [hf, gf, target, target, ya, yb, proj, proj, na, nb, wout]
        out_specs += [rows, pl.BlockSpec((8, LANES), _const2), pl.BlockSpec((1, D), _const2)]
        out_shape += [jax.ShapeDtypeStruct((T, D), F32), jax.ShapeDtypeStruct((8, LANES), F32), jax.ShapeDtypeStruct((1, D), F32)]
    else:
        in_specs = [rows] + in_specs
        args = [dh, ya, yb, proj, proj, na, nb, wout]
    return pl.pallas_call(
        body, name=name, grid=(T // tm,), in_specs=in_specs, out_specs=tuple(out_specs), out_shape=tuple(out_shape),
        compiler_params=_params("arbitrary"),
    )(*args)


def _mla_bwd(q, k, v, o, do, lse, after, Bl, Lp, name):
    T = q.shape[0]
    n_big, S = _q_tiles(Lp)

    def body(q_ref, k_ref, v_ref, o_ref, do_ref, lse_ref, _, dq_ref, dk_ref, dv_ref):
        dk_ref[...] = jnp.zeros_like(dk_ref)
        dv_ref[...] = jnp.zeros_like(dv_ref)

        def qtile(r0, nq):
            lane = _lane((nq, LANES))
            dop = do_ref[pl.ds(r0, nq), :]
            op = o_ref[pl.ds(r0, nq), :]
            lsev = lse_ref[pl.ds(r0, nq), :]
            for hh in range(2):
                cs = slice(LANES * hh, LANES * (hh + 1))
                dom = jnp.where((lane < B_V) if hh == 0 else (lane >= B_V), dop, 0.0)
                delta = jnp.sum(dom * op, axis=-1, keepdims=True)
                d_hi = delta.astype(BF16).astype(F32)
                x = dom if hh == 0 else pltpu.roll(dom, B_V, 1)
                domx = jnp.where(lane == SUM_LANE, d_hi, jnp.where(lane == SUM_LANE + 1, delta - d_hi, x)).astype(BF16)
                qh, kh, vh = q_ref[pl.ds(r0, nq), cs], k_ref[:, cs], v_ref[:, cs]
                lse_h = jnp.sum(jnp.where(lane == hh, lsev, 0.0), axis=-1, keepdims=True)
                p = jnp.exp2(_dot_nt(qh, kh) - lse_h)
                ds = (p * _dot_nt(domx, vh)).astype(BF16)
                dq_ref[pl.ds(r0, nq), cs] = _dot(ds, kh) * B_SCALE
                dk_ref[:, cs] += _dot_tn(ds, qh)
                dv_ref[:, cs] += _dot_tn(p.astype(BF16), domx)

        def step(i, c):
            qtile(pl.multiple_of(i * 512, 256), 256)
            qtile(pl.multiple_of(i * 512 + 256, 256), 256)
            return c

        lax.fori_loop(0, n_big // 2, step, 0)
        if n_big % 2:
            qtile((n_big - 1) * 256, 256)
        qtile(S, BLK)
        dk_ref[...] = dk_ref[...] * (1.0 / LOG2E)

    pair = lambda b, j: (b, j)
    wide, narrow = pl.BlockSpec((Lp, 256), pair), pl.BlockSpec((Lp, LANES), pair)
    return pl.pallas_call(
        body, name=name, grid=(Bl, B_HEADS // 2),
        in_specs=[wide, wide, wide, narrow, narrow, narrow, pl.BlockSpec(memory_space=pl.ANY)],
        out_specs=(wide, wide, wide), out_shape=(jax.ShapeDtypeStruct((T, 1024), F32),) * 3,
        compiler_params=_params("parallel", "parallel"),
    )(q, k, v, o, do, lse, after)


def _win_bwd(proj, bias, ya, dya, lse, Bl, S, name):
    T = proj.shape[0]
    Lp = S + BLK
    NB = S // BLK
    nblk = Lp // BLK
    G = A_HEADS // A_KV

    def body(q_ref, k_ref, v_ref, b_ref, o_ref, do_ref, lse_ref, dq_ref, dk_ref, dv_ref, db_ref):
        n = pl.program_id(1)

        @pl.when(n == 0)
        def _():
            dk_ref[...] = jnp.zeros_like(dk_ref)
            dv_ref[...] = jnp.zeros_like(dv_ref)

        @pl.when((n == 0) | (n == 1) | (n == NB - 1) | (n == NB))
        def _():
            db_ref[...] = jnp.zeros_like(db_ref)

        base = pl.multiple_of(_win_base(n, NB) * BLK, BLK)
        lane = _lane((BLK, LANES))
        lo = lane < A_DH
        lsev = lse_ref[...]
        kv = [_win_kv(k_ref, v_ref, base, S, kvh) for kvh in range(A_KV)]
        qms, pbs, domxs, dsls = [], [], [], []
        for h in range(A_HEADS):
            k2, v2 = kv[h // G]
            ps = slice(LANES * (h // 2), LANES * (h // 2 + 1))
            qm = (jnp.where(lo if h % 2 == 0 else ~lo, q_ref[:, ps], 0.0) * (A_SCALE * LOG2E)).astype(BF16)
            lse_h = jnp.sum(jnp.where(lane == h, lsev, 0.0), axis=-1, keepdims=True)
            p = jnp.exp2(_dot_nt(qm, k2) + b_ref[0, h] - lse_h)
            dop, op = do_ref[:, ps], o_ref[:, ps]
            if h % 2 == 1:
                dop, op = pltpu.roll(dop, A_DH, 1), pltpu.roll(op, A_DH, 1)
            dom = jnp.where(lo, dop, 0.0)
            delta = jnp.sum(dom * op, axis=-1, keepdims=True)
            d_hi = delta.astype(BF16).astype(F32)
            domx = jnp.where(lane == A_DH, d_hi, jnp.where(lane == A_DH + 1, delta - d_hi, dom)).astype(BF16)
            dsc = p * _dot_nt(domx, v2)
            db_ref[0, 0, h] += dsc
            qms.append(qm)
            pbs.append(p.astype(BF16))
            domxs.append(domx)
            dsls.append(dsc.astype(BF16))
        dqs = [_dot(dsls[h], kv[h // G][0]) * A_SCALE for h in range(A_HEADS)]
        for kvh in range(A_KV):
            cs = slice(LANES * kvh, LANES * (kvh + 1))
            group = range(kvh * G, (kvh + 1) * G)
            stack = lambda parts: jnp.concatenate([parts[h] for h in group], axis=0)
            dk_acc = _dot_tn(stack(dsls), stack(qms)) * (1.0 / LOG2E)
            dv_acc = _dot_tn(stack(pbs), stack(domxs))
            dv_acc = jnp.where(_lane(dv_acc.shape) < A_DH, dv_acc, 0.0)
            dk_ref[pl.ds(base, 3 * BLK), cs] += dk_acc[0:3 * BLK]
            dk_ref[S:S + BLK, cs] += dk_acc[3 * BLK:4 * BLK]
            dv_ref[pl.ds(base, 3 * BLK), cs] += dv_acc[0:3 * BLK]
            dv_ref[S:S + BLK, cs] += dv_acc[3 * BLK:4 * BLK]
        for j in range(A_HEADS // 2):
            dq_ref[:, LANES * j:LANES * (j + 1)] = jnp.where(lo, dqs[2 * j], dqs[2 * j + 1])

    qrow = lambda b, n: (b * nblk + n, 0)
    kvs = pl.BlockSpec((Lp, 256), lambda b, n: (b, 0))
    return pl.pallas_call(
        body, name=name, grid=(Bl, nblk),
        in_specs=[pl.BlockSpec((BLK, A_W), qrow),
                  pl.BlockSpec((Lp, 256), lambda b, n: (b, KA0 // 256)), pl.BlockSpec((Lp, 256), lambda b, n: (b, VA0 // 256)),
                  pl.BlockSpec((1, A_HEADS, BLK, 4 * BLK), lambda b, n: (_win_variant(n, NB), 0, 0, 0)),
                  pl.BlockSpec((BLK, A_W), qrow), pl.BlockSpec((BLK, A_W), qrow), pl.BlockSpec((BLK, LANES), qrow)],
        out_specs=(pl.BlockSpec((BLK, A_W), qrow), kvs, kvs,
                   pl.BlockSpec((1, 1, A_HEADS, BLK, 4 * BLK), lambda b, n: (b, _win_variant(n, NB), 0, 0, 0))),
        out_shape=(jax.ShapeDtypeStruct((T, A_W), F32), jax.ShapeDtypeStruct((T, 256), F32),
                   jax.ShapeDtypeStruct((T, 256), F32), jax.ShapeDtypeStruct((Bl, 4, A_HEADS, BLK, 4 * BLK), F32)),
        compiler_params=_params("parallel", "arbitrary"),
    )(proj, proj, proj, bias, ya, dya, lse)


def _table_grad(dbias_list, buckets, after, name):
    nl = len(dbias_list)
    Bl = dbias_list[0].shape[0]

    def body(*refs):
        d_refs = refs[:nl]
        b_ref, _, o_ref, s_ref, acc, part = refs[nl:]
        v, b = pl.program_id(0), pl.program_id(1)

        @pl.when((v == 0) & (b == 0))
        def _():
            o_ref[...] = jnp.zeros_like(o_ref)
            s_ref[...] = jnp.zeros_like(s_ref)

        slot = _lane((BLK, 4 * BLK))
        for l in range(nl):
            for h in range(A_HEADS):
                col = jnp.sum(jnp.where(slot == SINK_SLOT, d_refs[l][0, 0, h], 0.0), axis=0, keepdims=True)
                s_ref[l, h:h + 1, :] += jnp.sum(col, axis=1, keepdims=True)

        tot = d_refs[0][0, 0]
        for r in d_refs[1:]:
            tot = tot + r[0, 0]

        @pl.when(b == 0)
        def _():
            acc[...] = tot

        @pl.when(b > 0)
        def _():
            acc[...] += tot

        @pl.when(b == Bl - 1)
        def _():
            bidx = b_ref[0]

            def step(j, c):
                mask = bidx == j
                for h in range(A_HEADS):
                    part[h, j] = jnp.sum(jnp.where(mask, acc[h], 0.0).reshape(BLK // 8, 8, 4 * BLK), axis=0)
                return c

            lax.fori_loop(0, N_BUCKETS, step, 0)
            rows = lax.broadcasted_iota(jnp.int32, (N_BUCKETS, LANES), 0)
            lanes = _lane((N_BUCKETS, LANES))
            for h in range(A_HEADS):
                col = jnp.sum(jnp.sum(part[h], axis=1), axis=-1, keepdims=True)
                o_ref[h:h + 1, :] += jnp.sum(jnp.where(rows == lanes, col, 0.0), axis=0, keepdims=True)

    return pl.pallas_call(
        body, name=name, grid=(4, Bl),
        in_specs=[pl.BlockSpec((1, 1, A_HEADS, BLK, 4 * BLK), lambda v, b: (b, v, 0, 0, 0))] * nl
        + [pl.BlockSpec((1, BLK, 4 * BLK), lambda v, b: (v, 0, 0)), pl.BlockSpec(memory_space=pl.ANY)],
        out_specs=(pl.BlockSpec((8, LANES), lambda v, b: (0, 0)), pl.BlockSpec((nl, 8, LANES), lambda v, b: (0, 0, 0))),
        out_shape=(jax.ShapeDtypeStruct((8, LANES), F32), jax.ShapeDtypeStruct((nl, 8, LANES), F32)),
        scratch_shapes=[pltpu.VMEM((A_HEADS, BLK, 4 * BLK), F32), pltpu.VMEM((A_HEADS, N_BUCKETS, 8, 4 * BLK), F32)],
        compiler_params=_params("arbitrary", "arbitrary"),
    )(*dbias_list, buckets, after)


def _mla_prep_bwd(dq, dk, dv, proj, gq, gkv, wq, wkk, wkv, tq, tk, tm, name):
    T = proj.shape[0]

    def body(dq_ref, dk_ref, dv_ref, cq_ref, ckv_ref, gq_ref, gkv_ref, wq_ref, wkk_ref, wkv_ref, tq_ref, tk_ref,
             dcq_ref, dckv_ref, dkr_ref, dwq_ref, dwkk_ref, dwkv_ref, dgq_ref, dgkv_ref):
        @pl.when(pl.program_id(0) == 0)
        def _():
            for r in (dwq_ref, dwkk_ref, dwkv_ref, dgq_ref, dgkv_ref):
                r[...] = jnp.zeros_like(r)

        tqv = tq_ref[...]
        dqp = jnp.concatenate([_rope_t(dq_ref[:, LANES * h:LANES * (h + 1)], tqv) for h in range(B_HEADS)],
                              axis=1).astype(BF16)
        cq, gq_ = cq_ref[...], gq_ref[...]
        rq = _rstd(cq)
        dwq_ref[...] += _dot_tn((cq * rq * gq_).astype(BF16), dqp)
        dx, dg = _rms_bwd(_dot_nt(dqp, wq_ref[...]), cq, rq, gq_)
        dcq_ref[...] = dx
        dgq_ref[...] += dg

        dkv_ = dk_ref[...]
        dks = dkv_[:, 0:LANES]
        for h in range(1, B_HEADS):
            dks = dks + dkv_[:, LANES * h:LANES * (h + 1)]
        dkr_ref[...] = _rope_t(pltpu.roll(dks, NOPE, 1), tk_ref[...])
        dkb = dkv_.astype(BF16)
        dvb = dv_ref[...].astype(BF16)
        ckv, gkv_ = ckv_ref[...], gkv_ref[...]
        rk = _rstd(ckv)
        cb = (ckv * rk * gkv_).astype(BF16)
        dwkk_ref[...] += _dot_tn(cb, dkb)
        dwkv_ref[...] += _dot_tn(cb, dvb)
        dx, dg = _rms_bwd(_dot_nt(dkb, wkk_ref[...]) + _dot_nt(dvb, wkv_ref[...]), ckv, rk, gkv_)
        dckv_ref[...] = dx
        dgkv_ref[...] += dg

    tab = lambda n: pl.BlockSpec((n, tm, LANES), lambda i: (0, i, 0))
    wide = pl.BlockSpec((tm, 1024), _row(0))
    return pl.pallas_call(
        body, name=name, grid=(T // tm,),
        in_specs=[wide, wide, wide, pl.BlockSpec((tm, 256), _row(CQ0 // 256)),
                  pl.BlockSpec((tm, 128), _row(CKV0 // 128)), pl.BlockSpec((1, Q_RANK), _const2),
                  pl.BlockSpec((1, KV_RANK), _const2), pl.BlockSpec((Q_RANK, 1024), _const2),
                  pl.BlockSpec((KV_RANK, 1024), _const2), pl.BlockSpec((KV_RANK, 1024), _const2), tab(3), tab(4)],
        out_specs=(pl.BlockSpec((tm, 256), _row(0)), pl.BlockSpec((tm, 128), _row(0)), pl.BlockSpec((tm, 128), _row(0)),
                   pl.BlockSpec((Q_RANK, 1024), _const2), pl.BlockSpec((KV_RANK, 1024), _const2),
                   pl.BlockSpec((KV_RANK, 1024), _const2), pl.BlockSpec((1, Q_RANK), _const2),
                   pl.BlockSpec((1, KV_RANK), _const2)),
        out_shape=(jax.ShapeDtypeStruct((T, 256), F32), jax.ShapeDtypeStruct((T, 128), F32),
                   jax.ShapeDtypeStruct((T, 128), F32), jax.ShapeDtypeStruct((Q_RANK, 1024), F32),
                   jax.ShapeDtypeStruct((KV_RANK, 1024), F32), jax.ShapeDtypeStruct((KV_RANK, 1024), F32),
                   jax.ShapeDtypeStruct((1, Q_RANK), F32), jax.ShapeDtypeStruct((1, KV_RANK), F32)),
        compiler_params=_params("arbitrary"),
    )(dq, dk, dv, proj, proj, gq, gkv, wq, wkk, wkv, tq, tk)


def _inproj_bwd(pieces, h, g, wx, dh_out, tm, name):
    T = h.shape[0]
    nsteps = T // tm
    CH = 512

    def body(*refs):
        p_refs = refs[:len(pieces)]
        h_ref, g_ref, w_ref, dho_ref, dh_ref, dw_hbm, dg_ref, acc, sem = refs[len(pieces):]
        i = pl.program_id(0)

        @pl.when(i == 0)
        def _():
            acc[...] = jnp.zeros_like(acc)
            dg_ref[...] = jnp.zeros_like(dg_ref)

        dp = jnp.concatenate([r[...].astype(BF16) for r in p_refs], axis=1)
        x, gain = h_ref[...], g_ref[...]
        r = _rstd(x)
        u = (x * r * gain).astype(BF16)
        for c in range(0, NX, CH):
            acc[:, c:c + CH] += _dot_tn(u, dp[:, c:c + CH])
        dx, dg = _rms_bwd(_dot_nt(dp, w_ref[...]), x, r, gain)
        dh_ref[...] = dho_ref[...] + dx
        dg_ref[...] += dg

        @pl.when(i == nsteps - 1)
        def _():
            cp = pltpu.make_async_copy(acc, dw_hbm, sem)
            cp.start()
            cp.wait()

    return pl.pallas_call(
        body, name=name, grid=(nsteps,),
        in_specs=[pl.BlockSpec((tm, p.shape[1]), _row(0)) for p in pieces]
        + [pl.BlockSpec((tm, D), _row(0)), pl.BlockSpec((1, D), _const2), pl.BlockSpec((D, NX), _const2),
           pl.BlockSpec((tm, D), _row(0))],
        out_specs=(pl.BlockSpec((tm, D), _row(0)), pl.BlockSpec(memory_space=pl.ANY), pl.BlockSpec((1, D), _const2)),
        out_shape=(jax.ShapeDtypeStruct((T, D), F32), jax.ShapeDtypeStruct((D, NX), F32), jax.ShapeDtypeStruct((1, D), F32)),
        scratch_shapes=[pltpu.VMEM((D, NX), F32), pltpu.SemaphoreType.DMA(())],
        compiler_params=_params("arbitrary"),
    )(*pieces, h, g, wx, dh_out)


def kernel(x, meta_tokens, rel_bias_table, norm_in, w_in, sink_a, norm_q_lat, w_uq, norm_kv_lat, w_ukv, norm_out_a, norm_out_b, w_out, norm_final, loss_target, m_meta_tokens, m_rel_bias_table, m_norm_in, m_w_in, m_sink_a, m_norm_q_lat, m_w_uq, m_norm_kv_lat, m_w_ukv, m_norm_out_a, m_norm_out_b, m_w_out, m_norm_final, v_meta_tokens, v_rel_bias_table, v_norm_in, v_w_in, v_sink_a, v_norm_q_lat, v_w_uq, v_norm_kv_lat, v_w_ukv, v_norm_out_a, v_norm_out_b, v_w_out, v_norm_final):
    Bl, S, _ = x.shape
    assert S % 256 == 0 and S >= 3 * BLK, "the attention kernels tile the real tokens in 256-row blocks"
    Lp = S + BLK
    T = Bl * Lp
    tm = Lp // 4
    tq = Lp // 8
    depth = w_in.shape[0]
    me = 4 * lax.axis_index("x") + 2 * lax.axis_index("y") + lax.axis_index("c")

    def finish(handles, after, name):
        srcs, lands = _xchg_wait(handles, after, name)
        full = []
        for hd, src, land in zip(handles, srcs, lands):
            own = src if hd[4] == "gather" else lax.dynamic_index_in_dim(src, me, 0, keepdims=False)
            full.append(_own_slot(land, own, me))
        return full

    def weights_of(i):
        return [w_in[i].astype(BF16), w_uq[i].astype(BF16), w_ukv[i].astype(BF16), w_out[i].astype(BF16)]

    def packed(l_in, l_uq, l_ukv, l_out):
        cols = lambda t: jnp.transpose(t, (1, 0, 2)).reshape(t.shape[1], NDEV * t.shape[2])
        wkk, wkv = _pack_w_ukv(cols(l_ukv))
        return _pack_w_in(l_in), _pack_w_uq(cols(l_uq)), wkk, wkv, l_out.reshape(D, D)

    wbs = [weights_of(i) for i in range(depth)]
    wb = wbs[0]
    gat_m, tok_m = _xchg_start([meta_tokens], ["gather"], "gather_start_meta")
    gat_a, tok = _xchg_start([wb[0]], ["gather"], "gather_start_0a", after=tok_m)
    buckets = _bias_buckets(S)
    biases = _build_bias(buckets, _after(rel_bias_table, tok), sink_a, "build_bias")
    tb = 256 if T % 256 == 0 else tq
    tq_tab = jnp.tile(_rope_tables(S, Lp, NOPE, True), (1, Bl, 1))
    key_pad = jnp.where((jnp.arange(Lp) >= S + NMETA)[:, None] & (jnp.arange(LANES) == MASK_LANE)[None, :], NEG, 0.0)
    tk_tab = jnp.tile(jnp.concatenate([_rope_tables(S, Lp, 0, False), key_pad.astype(F32)[None]], axis=0), (1, Bl, 1))
    (l_meta,) = finish(gat_m, biases[0], "gather_wait_meta")
    meta_f = jnp.transpose(l_meta, (1, 0, 2)).reshape(NMETA, D)
    tail = jnp.concatenate([meta_f, jnp.zeros((BLK - NMETA, D), F32)], axis=0)
    h = jnp.concatenate([x, jnp.broadcast_to(tail[None], (Bl, BLK, D))], axis=1).reshape(T, D)
    (l_in,) = finish(gat_a, [h, tq_tab, tk_tab] + wb[1:] + [w for ws in wbs[1:] for w in ws], "gather_wait_0a")
    gat_b, tok = _xchg_start(wb[1:], ["gather"] * 3, "gather_start_0b", after=l_in)

    saved = []
    w_out_f = [None] * depth
    gat_next = None
    for i in range(depth):
        g_in, g_q, g_kv = norm_in[i][None], norm_q_lat[i][None], norm_kv_lat[i][None]
        g_a, g_b = norm_out_a[i][None], norm_out_b[i][None]
        if i == 0:
            proj = _inproj_fwd(h, _after(g_in, tok), _pack_w_in(l_in), tm,
                               f"inproj_fwd_{i}")
            l_uq, l_ukv, l_out = finish(gat_b, proj, "gather_wait_0b")
            wx, wq, wkk, wkv, w_out_f[i] = packed(l_in, l_uq, l_ukv, l_out)
        else:
            wx, wq, wkk, wkv, w_out_f[i] = packed(*finish(gat_next, h, f"gather_wait_{i}"))
            proj = _inproj_fwd(h, g_in, wx, tm, f"inproj_fwd_{i}")
        if i + 1 < depth:
            gat_next, tok = _xchg_start(wbs[i + 1], ["gather"] * 4, f"gather_start_{i + 1}", after=w_out_f[i])
            g_q = _after(g_q, tok)
        q, k, v = _mla_prep_fwd(proj, g_q, g_kv, wq, wkk, wkv, tq_tab, tk_tab, tm, f"mla_prep_fwd_{i}")
        ya, lse_a = _win_fwd(proj, biases[i], Bl, S, f"win_fwd_{i}")
        yb, lse_b = _mla_fwd(q, k, v, Bl, Lp, f"mla_fwd_{i}")
        h_new = _out_fwd(ya, yb, proj, g_a, g_b, w_out_f[i], h, tm, f"out_fwd_{i}")
        saved.append((h, proj, q, k, v, ya, lse_a, yb, lse_b, wx, wq, wkk, wkv))
        h = h_new

    fuse_head = tb == 2 * BLK
    if not fuse_head:
        dh, loss_acc, dg_final = _loss_head(h, norm_final[None], loss_target, Bl, S, "loss_head")

    g_n_in, g_nq, g_nkv, g_na, g_nb = ([None] * depth for _ in range(5))
    dbias_all, sc_out, sc_rest = [None] * depth, [None] * depth, [None] * depth
    split = lambda t, n: jnp.transpose(t.reshape(t.shape[0], NDEV, n), (1, 0, 2)).astype(BF16)
    tok = None
    for i in reversed(range(depth)):
        h_in, proj, q, k, v, ya, lse_a, yb, lse_b, wx, wq, wkk, wkv = saved[i]
        g_in, g_q, g_kv = norm_in[i][None], norm_q_lat[i][None], norm_kv_lat[i][None]
        g_a, g_b = norm_out_a[i][None], norm_out_b[i][None]
        if tok is not None:
            g_a = _after(g_a, tok)
        if fuse_head and i == depth - 1:
            dya, dyb, dga, dgb, g_w_out, g_na[i], g_nb[i], dh, loss_acc, dg_final = _out_bwd(
                None, ya, yb, proj, g_a, g_b, w_out_f[i], tb, f"out_bwd_{i}", head=(h, norm_final[None], loss_target, S))
        else:
            dya, dyb, dga, dgb, g_w_out, g_na[i], g_nb[i] = _out_bwd(dh, ya, yb, proj, g_a, g_b, w_out_f[i], tb, f"out_bwd_{i}")
        out_payload = g_w_out.reshape(NDEV, D // NDEV, D).astype(BF16)
        if i == 0:
            sc_out[i], tok = _xchg_start([out_payload], ["scatter"], f"scatter_start_{i}a")
        dq, dk, dv = _mla_bwd(q, k, v, yb, dyb, lse_b, dya if tok is None else tok, Bl, Lp, f"mla_bwd_{i}")
        dqa, dka, dva, dbias = _win_bwd(proj, biases[i], ya, dya, lse_a, Bl, S, f"win_bwd_{i}")
        dcq, dckv, dkr, dwq, dwkk, dwkv, g_nq[i], g_nkv[i] = _mla_prep_bwd(
            dq, dk, dv, proj, g_q, g_kv, wq, wkk, wkv, tq_tab, tk_tab, tm, f"mla_prep_bwd_{i}")
        pieces = (dqa, dka, dva, dga, dcq, dckv, dkr, dgb)
        dbias_all[i] = dbias
        rest = [split(_unpack_w_uq_grad(dwq), 768 // NDEV), split(_unpack_w_ukv_grad(dwkk, dwkv), 1024 // NDEV)]
        dh, dwx, g_n_in[i] = _inproj_bwd(pieces, h_in, g_in, wx, dh, tb, f"inproj_bwd_{i}")
        rest = [_unpack_w_in_grad(dwx).astype(BF16)] + rest + ([out_payload] if i > 0 else [])
        sc_rest[i], tok = _xchg_start(rest, ["scatter"] * len(rest), f"scatter_start_{i}b")

    dh3 = dh.reshape(Bl, Lp, D)
    grad_x = dh3[:, :S]
    g_meta = jnp.transpose(jnp.sum(dh3[:, S:S + NMETA], axis=0).reshape(NMETA, NDEV, D // NDEV), (1, 0, 2))
    dtab, dsink = _table_grad(dbias_all, buckets, tok, "table_grad")
    two_d = lambda a: a.reshape(1, -1) if a.ndim == 1 else a
    small_w = [rel_bias_table, norm_in, sink_a, norm_q_lat, norm_kv_lat, norm_out_a, norm_out_b, norm_final]
    small_m = [m_rel_bias_table, m_norm_in, m_sink_a, m_norm_q_lat, m_norm_kv_lat, m_norm_out_a, m_norm_out_b, m_norm_final]
    small_v = [v_rel_bias_table, v_norm_in, v_sink_a, v_norm_q_lat, v_norm_kv_lat, v_norm_out_a, v_norm_out_b, v_norm_final]
    small_g = [jnp.transpose(dtab[:, :N_BUCKETS]), jnp.concatenate(g_n_in), dsink[:, :, 0], jnp.concatenate(g_nq),
               jnp.concatenate(g_nkv), jnp.concatenate(g_na), jnp.concatenate(g_nb), dg_final]
    sc_small, tok = _xchg_start(small_g + [loss_acc, g_meta], ["gather"] * (len(small_g) + 1) + ["scatter"],
                                "scatter_start_small")

    res = {}

    def update(n, r, w, m, v):
        outs = _adamw_param(r, w, m, v, f"adamw_{n}")
        res[n] = list(outs)
        return outs[0]

    r_out, r_in, r_uq, r_ukv = [None] * depth, [None] * depth, [None] * depth, [None] * depth
    for i in reversed(range(1, depth)):
        r_in[i], r_uq[i], r_ukv[i], r_out[i] = finish(sc_rest[i], tok, f"scatter_wait_{i}b")
    (r_out[0],) = finish(sc_out[0], tok, "scatter_wait_0a")
    done = update("w_out", r_out, w_out, m_w_out, v_w_out)
    r_in[0], r_uq[0], r_ukv[0] = finish(sc_rest[0], [done, grad_x], "scatter_wait_0b")
    r_small = finish(sc_small, r_in[0], "scatter_wait_small")
    update("w_in", r_in, w_in, m_w_in, v_w_in)
    update("w_uq", r_uq, w_uq, m_w_uq, v_w_uq)
    update("w_ukv", r_ukv, w_ukv, m_w_ukv, v_w_ukv)
    update("meta_tokens", [r_small[-1]], meta_tokens[None], m_meta_tokens[None], v_meta_tokens[None])
    res["meta_tokens"] = [o[0] for o in res["meta_tokens"]]
    names_small = ["rel_bias_table", "norm_in", "sink_a", "norm_q_lat", "norm_kv_lat", "norm_out_a", "norm_out_b", "norm_final"]
    outs = _adamw_small(r_small[:-2], [two_d(a) for a in small_w], [two_d(a) for a in small_m], [two_d(a) for a in small_v],
                        r_small[-2], "adamw_replicated")
    loss = outs[-1][0, 0]
    ns = len(names_small)
    for j, n in enumerate(names_small):
        res[n] = [outs[kk * ns + j].reshape(small_w[j].shape) for kk in range(4)]
    order = ["meta_tokens", "rel_bias_table", "norm_in", "w_in", "sink_a", "norm_q_lat", "w_uq", "norm_kv_lat", "w_ukv",
             "norm_out_a", "norm_out_b", "w_out", "norm_final"]
    return (loss, grad_x, *[res[n][kk] for kk in range(4) for n in order])
```
